```python
import math
import jax, jax.numpy as jnp
from jax import lax
import numpy as np

D_MODEL = 1024
BATCH = 8
SEQ = 8192
DEPTH = 1

NORM_EPS = 1e-6
CHUNK = 128
N_BRANCHES = 2
GMLP_WIDTH = D_MODEL
GMLP_GROUPS = 8
GMLP_GROUP_DIM = GMLP_WIDTH // GMLP_GROUPS
SSM_EXPAND = 2
D_INNER = SSM_EXPAND * D_MODEL
HEAD_DIM = 64
N_SSM_HEADS = D_INNER // HEAD_DIM
N_SSM_GROUPS = 8
HEADS_PER_GROUP = N_SSM_HEADS // N_SSM_GROUPS
D_STATE = 128
CONV_WIDTH = 4
CONV_DIM = D_INNER + 2 * N_SSM_GROUPS * D_STATE
SSM_NORM_GROUP = D_INNER // N_SSM_GROUPS
D_FF = 4 * D_MODEL
IN_PROJ_DIM = 2 * GMLP_WIDTH + D_INNER + CONV_DIM + N_SSM_HEADS + N_BRANCHES * D_MODEL
_SPLITS = tuple(np.cumsum([2 * GMLP_WIDTH, D_INNER, CONV_DIM, N_SSM_HEADS]).tolist())

kernel_name = "hybrid_gmlp_ssd_gated_block"


def rms_norm(x, g, eps=NORM_EPS):
    xf = x.astype(jnp.float32)
    out = xf * lax.rsqrt(jnp.mean(xf * xf, axis=-1, keepdims=True) + eps)
    return out.astype(x.dtype) * g


def layer_norm(x, g, b, eps=NORM_EPS):
    xf = x.astype(jnp.float32)
    mu = jnp.mean(xf, axis=-1, keepdims=True)
    var = jnp.mean(jnp.square(xf - mu), axis=-1, keepdims=True)
    out = (xf - mu) * lax.rsqrt(var + eps)
    return out.astype(x.dtype) * g + b


def gmlp_spatial_gating(uv, v_g, v_b, w_spatial, b_spatial):
    bsz, seqlen, _ = uv.shape
    nc = seqlen // CHUNK
    z = jax.nn.gelu(uv, approximate=False)
    u, v = jnp.split(z, 2, axis=-1)
    v = layer_norm(v, v_g, v_b)
    v = v.reshape(bsz, nc, CHUNK, GMLP_GROUPS, GMLP_GROUP_DIM)
    causal = jnp.tril(jnp.ones((CHUNK, CHUNK), dtype=bool))
    w = jnp.where(causal[None], w_spatial, jnp.zeros_like(w_spatial))
    s = jnp.einsum("gij,bcjgd->bcigd", w, v) + b_spatial.T[None, None, :, :, None]
    return u * s.reshape(bsz, seqlen, GMLP_WIDTH)


def causal_depthwise_conv(x, w, b):
    y = lax.conv_general_dilated(
        x, w, window_strides=(1,), padding=[(CONV_WIDTH - 1, 0)],
        dimension_numbers=("NWC", "WIO", "NWC"), feature_group_count=x.shape[-1])
    return y + b


def ssd_chunked(xh, dt, a, bm, cm):
    bsz, seqlen = xh.shape[:2]
    nc = seqlen // CHUNK
    xc = xh.reshape(bsz, nc, CHUNK, N_SSM_GROUPS, HEADS_PER_GROUP, HEAD_DIM)
    dtc = dt.reshape(bsz, nc, CHUNK, N_SSM_GROUPS, HEADS_PER_GROUP)
    bc = bm.reshape(bsz, nc, CHUNK, N_SSM_GROUPS, D_STATE)
    cc = cm.reshape(bsz, nc, CHUNK, N_SSM_GROUPS, D_STATE)
    xdt = xc * dtc[..., None]
    da = (dtc * a).astype(jnp.float32).transpose(0, 3, 4, 1, 2)
    cs = jnp.cumsum(da, axis=-1)
    causal = jnp.tril(jnp.ones((CHUNK, CHUNK), dtype=bool))
    seg = cs[..., :, None] - cs[..., None, :]
    lmat = jnp.exp(jnp.where(causal, seg, -jnp.inf))
    cb = jnp.einsum("bclgn,bcsgn->bgcls", cc, bc)
    m = cb[:, :, None] * lmat
    y_diag = jnp.einsum("bgrcls,bcsgrp->bclgrp", m, xdt)
    decay_states = jnp.exp(cs[..., -1:] - cs)
    states = jnp.einsum("bcsgn,bgrcs,bcsgrp->bcgrpn", bc, decay_states, xdt)
    chunk_decay = jnp.exp(cs[..., -1])

    def step(h, inp):
        st, dec = inp
        return h * dec[..., None, None] + st, h

    h0 = jnp.zeros_like(states[:, 0])
    _, prev = lax.scan(step, h0, (jnp.moveaxis(states, 1, 0), jnp.moveaxis(chunk_decay, -1, 0)))
    prev = jnp.moveaxis(prev, 0, 1)
    y_off = jnp.einsum("bclgn,bcgrpn,bgrcl->bclgrp", cc, prev, jnp.exp(cs))
    y = (y_diag + y_off).reshape(bsz, seqlen, N_SSM_GROUPS, HEADS_PER_GROUP, HEAD_DIM)
    return y.astype(xh.dtype)


def mamba2_branch(z, xbc, dt_raw, conv_w, conv_b, dt_bias, a_log, d_skip, ssm_norm_g):
    bsz, seqlen, _ = z.shape
    xbc = jax.nn.silu(causal_depthwise_conv(xbc, conv_w, conv_b))
    xs, bm, cm = jnp.split(xbc, [D_INNER, D_INNER + N_SSM_GROUPS * D_STATE], axis=-1)
    xh = xs.reshape(bsz, seqlen, N_SSM_GROUPS, HEADS_PER_GROUP, HEAD_DIM)
    bm = bm.reshape(bsz, seqlen, N_SSM_GROUPS, D_STATE)
    cm = cm.reshape(bsz, seqlen, N_SSM_GROUPS, D_STATE)
    dt = jax.nn.softplus(dt_raw + dt_bias).reshape(bsz, seqlen, N_SSM_GROUPS, HEADS_PER_GROUP)
    a = -jnp.exp(a_log.astype(jnp.float32)).reshape(N_SSM_GROUPS, HEADS_PER_GROUP)
    y = ssd_chunked(xh, dt, a, bm, cm)
    y = y + d_skip.reshape(N_SSM_GROUPS, HEADS_PER_GROUP)[:, :, None] * xh
    y = y.reshape(bsz, seqlen, D_INNER)
    yg = (y * jax.nn.silu(z)).reshape(bsz, seqlen, N_SSM_GROUPS, SSM_NORM_GROUP)
    yf = yg.astype(jnp.float32)
    yn = yf * lax.rsqrt(jnp.mean(yf * yf, axis=-1, keepdims=True) + NORM_EPS)
    return yn.reshape(bsz, seqlen, D_INNER).astype(z.dtype) * ssm_norm_g


def _fwd_setup_inputs(seed: int = 0) -> dict:
    key = jax.random.key(seed)
    ks = jax.random.split(key, 24)
    L = DEPTH

    def nrm(k, shape, scale):
        return jax.random.normal(k, shape, jnp.float32) * scale

    x = nrm(ks[0], (BATCH, SEQ, D_MODEL), 1.0)
    norm_mix_g = 1.0 + nrm(ks[1], (L, D_MODEL), 0.02)
    w_in = nrm(ks[2], (L, D_MODEL, IN_PROJ_DIM), D_MODEL ** -0.5)
    conv_w = nrm(ks[3], (L, CONV_WIDTH, 1, CONV_DIM), CONV_WIDTH ** -0.5)
    conv_b = nrm(ks[4], (L, CONV_DIM), 0.02)
    dt0 = jnp.exp(jax.random.uniform(ks[5], (L, N_SSM_HEADS), jnp.float32,
                                     minval=math.log(1e-3), maxval=math.log(1e-1)))
    dt_bias = dt0 + jnp.log(-jnp.expm1(-dt0))
    a_log = jnp.log(jax.random.uniform(ks[6], (L, N_SSM_HEADS), jnp.float32, minval=1.0, maxval=16.0))
    d_skip = 1.0 + nrm(ks[7], (L, N_SSM_HEADS), 0.02)
    ssm_norm_g = 1.0 + nrm(ks[8], (L, D_INNER), 0.02)
    v_norm_g = 1.0 + nrm(ks[9], (L, GMLP_WIDTH), 0.02)
    v_norm_b = nrm(ks[10], (L, GMLP_WIDTH), 0.02)
    w_spatial = nrm(ks[11], (L, GMLP_GROUPS, CHUNK, CHUNK), CHUNK ** -0.5)
    b_spatial = 1.0 + nrm(ks[12], (L, GMLP_GROUPS, CHUNK), 0.02)
    b_gates = nrm(ks[13], (L, N_BRANCHES * D_MODEL), 0.02)
    w_proj_a = nrm(ks[14], (L, GMLP_WIDTH, D_MODEL), GMLP_WIDTH ** -0.5)
    w_proj_b = nrm(ks[15], (L, D_INNER, D_MODEL), D_INNER ** -0.5)
    w_out = nrm(ks[16], (L, D_MODEL, D_MODEL), D_MODEL ** -0.5)
    norm_mlp_g = 1.0 + nrm(ks[17], (L, D_MODEL), 0.02)
    w_mlp_up = nrm(ks[18], (L, D_MODEL, D_FF), D_MODEL ** -0.5)
    w_mlp_down = nrm(ks[19], (L, D_FF, D_MODEL), D_FF ** -0.5)
    norm_final_g = 1.0 + nrm(ks[20], (D_MODEL,), 0.02)
    return {"x": x, "norm_mix_g": norm_mix_g, "w_in": w_in, "conv_w": conv_w, "conv_b": conv_b,
            "dt_bias": dt_bias, "a_log": a_log, "d_skip": d_skip, "ssm_norm_g": ssm_norm_g,
            "v_norm_g": v_norm_g, "v_norm_b": v_norm_b, "w_spatial": w_spatial, "b_spatial": b_spatial,
            "b_gates": b_gates, "w_proj_a": w_proj_a, "w_proj_b": w_proj_b, "w_out": w_out,
            "norm_mlp_g": norm_mlp_g, "w_mlp_up": w_mlp_up, "w_mlp_down": w_mlp_down,
            "norm_final_g": norm_final_g}


def _fwd_reference(x, norm_mix_g, w_in, conv_w, conv_b, dt_bias, a_log, d_skip, ssm_norm_g,
              v_norm_g, v_norm_b, w_spatial, b_spatial, b_gates, w_proj_a, w_proj_b, w_out,
              norm_mlp_g, w_mlp_up, w_mlp_down, norm_final_g):
    for i in range(DEPTH):
        h = rms_norm(x, norm_mix_g[i])
        proj = h @ w_in[i]
        uv, z, xbc, dt_raw, gate_logits = jnp.split(proj, _SPLITS, axis=-1)
        y_a = gmlp_spatial_gating(uv, v_norm_g[i], v_norm_b[i], w_spatial[i], b_spatial[i])
        y_b = mamba2_branch(z, xbc, dt_raw, conv_w[i], conv_b[i], dt_bias[i], a_log[i],
                            d_skip[i], ssm_norm_g[i])
        gates = jax.nn.sigmoid(gate_logits + b_gates[i])
        gate_a, gate_b = jnp.split(gates, 2, axis=-1)
        merged = gate_a * (y_a @ w_proj_a[i]) + gate_b * (y_b @ w_proj_b[i])
        x = x + merged @ w_out[i]
        h2 = rms_norm(x, norm_mlp_g[i])
        x = x + jnp.square(jax.nn.relu(h2 @ w_mlp_up[i])) @ w_mlp_down[i]
    return rms_norm(x, norm_final_g)


import jax as _jax
import jax.numpy as _jnp

TWIN_FORMAT = 'train_step'
FWD_PARAMS = ['x', 'norm_mix_g', 'w_in', 'conv_w', 'conv_b', 'dt_bias', 'a_log', 'd_skip', 'ssm_norm_g', 'v_norm_g', 'v_norm_b', 'w_spatial', 'b_spatial', 'b_gates', 'w_proj_a', 'w_proj_b', 'w_out', 'norm_mlp_g', 'w_mlp_up', 'w_mlp_down', 'norm_final_g']
TWIN_WEIGHTS = ['norm_mix_g', 'w_in', 'conv_w', 'conv_b', 'dt_bias', 'a_log', 'd_skip', 'ssm_norm_g', 'v_norm_g', 'v_norm_b', 'w_spatial', 'b_spatial', 'b_gates', 'w_proj_a', 'w_proj_b', 'w_out', 'norm_mlp_g', 'w_mlp_up', 'w_mlp_down', 'norm_final_g']
TWIN_DIFF_INPUT = 'x'
TWIN_INPUTS = ['x', 'norm_mix_g', 'w_in', 'conv_w', 'conv_b', 'dt_bias', 'a_log', 'd_skip', 'ssm_norm_g', 'v_norm_g', 'v_norm_b', 'w_spatial', 'b_spatial', 'b_gates', 'w_proj_a', 'w_proj_b', 'w_out', 'norm_mlp_g', 'w_mlp_up', 'w_mlp_down', 'norm_final_g', 'loss_target', 'm_norm_mix_g', 'm_w_in', 'm_conv_w', 'm_conv_b', 'm_dt_bias', 'm_a_log', 'm_d_skip', 'm_ssm_norm_g', 'm_v_norm_g', 'm_v_norm_b', 'm_w_spatial', 'm_b_spatial', 'm_b_gates', 'm_w_proj_a', 'm_w_proj_b', 'm_w_out', 'm_norm_mlp_g', 'm_w_mlp_up', 'm_w_mlp_down', 'm_norm_final_g', 'v_norm_mix_g', 'v_w_in', 'v_conv_w', 'v_conv_b', 'v_dt_bias', 'v_a_log', 'v_d_skip', 'v_ssm_norm_g', 'v_v_norm_g', 'v_v_norm_b', 'v_w_spatial', 'v_b_spatial', 'v_b_gates', 'v_w_proj_a', 'v_w_proj_b', 'v_w_out', 'v_norm_mlp_g', 'v_w_mlp_up', 'v_w_mlp_down', 'v_norm_final_g']
TWIN_OUTPUTS = ['loss', 'grad_x', 'grad_norm_mix_g', 'grad_w_in', 'grad_conv_w', 'grad_conv_b', 'grad_dt_bias', 'grad_a_log', 'grad_d_skip', 'grad_ssm_norm_g', 'grad_v_norm_g', 'grad_v_norm_b', 'grad_w_spatial', 'grad_b_spatial', 'grad_b_gates', 'grad_w_proj_a', 'grad_w_proj_b', 'grad_w_out', 'grad_norm_mlp_g', 'grad_w_mlp_up', 'grad_w_mlp_down', 'grad_norm_final_g', 'delta_norm_mix_g', 'delta_w_in', 'delta_conv_w', 'delta_conv_b', 'delta_dt_bias', 'delta_a_log', 'delta_d_skip', 'delta_ssm_norm_g', 'delta_v_norm_g', 'delta_v_norm_b', 'delta_w_spatial', 'delta_b_spatial', 'delta_b_gates', 'delta_w_proj_a', 'delta_w_proj_b', 'delta_w_out', 'delta_norm_mlp_g', 'delta_w_mlp_up', 'delta_w_mlp_down', 'delta_norm_final_g', 'new_m_norm_mix_g', 'new_m_w_in', 'new_m_conv_w', 'new_m_conv_b', 'new_m_dt_bias', 'new_m_a_log', 'new_m_d_skip', 'new_m_ssm_norm_g', 'new_m_v_norm_g', 'new_m_v_norm_b', 'new_m_w_spatial', 'new_m_b_spatial', 'new_m_b_gates', 'new_m_w_proj_a', 'new_m_w_proj_b', 'new_m_w_out', 'new_m_norm_mlp_g', 'new_m_w_mlp_up', 'new_m_w_mlp_down', 'new_m_norm_final_g', 'new_v_norm_mix_g', 'new_v_w_in', 'new_v_conv_w', 'new_v_conv_b', 'new_v_dt_bias', 'new_v_a_log', 'new_v_d_skip', 'new_v_ssm_norm_g', 'new_v_v_norm_g', 'new_v_v_norm_b', 'new_v_w_spatial', 'new_v_b_spatial', 'new_v_b_gates', 'new_v_w_proj_a', 'new_v_w_proj_b', 'new_v_w_out', 'new_v_norm_mlp_g', 'new_v_w_mlp_up', 'new_v_w_mlp_down', 'new_v_norm_final_g']
TWIN_LEAF_KINDS = {'loss': 'loss', 'grad_x': 'grad_x', 'grad_norm_mix_g': 'grad_w', 'grad_w_in': 'grad_w', 'grad_conv_w': 'grad_w', 'grad_conv_b': 'grad_w', 'grad_dt_bias': 'grad_w', 'grad_a_log': 'grad_w', 'grad_d_skip': 'grad_w', 'grad_ssm_norm_g': 'grad_w', 'grad_v_norm_g': 'grad_w', 'grad_v_norm_b': 'grad_w', 'grad_w_spatial': 'grad_w', 'grad_b_spatial': 'grad_w', 'grad_b_gates': 'grad_w', 'grad_w_proj_a': 'grad_w', 'grad_w_proj_b': 'grad_w', 'grad_w_out': 'grad_w', 'grad_norm_mlp_g': 'grad_w', 'grad_w_mlp_up': 'grad_w', 'grad_w_mlp_down': 'grad_w', 'grad_norm_final_g': 'grad_w', 'delta_norm_mix_g': 'delta_w', 'delta_w_in': 'delta_w', 'delta_conv_w': 'delta_w', 'delta_conv_b': 'delta_w', 'delta_dt_bias': 'delta_w', 'delta_a_log': 'delta_w', 'delta_d_skip': 'delta_w', 'delta_ssm_norm_g': 'delta_w', 'delta_v_norm_g': 'delta_w', 'delta_v_norm_b': 'delta_w', 'delta_w_spatial': 'delta_w', 'delta_b_spatial': 'delta_w', 'delta_b_gates': 'delta_w', 'delta_w_proj_a': 'delta_w', 'delta_w_proj_b': 'delta_w', 'delta_w_out': 'delta_w', 'delta_norm_mlp_g': 'delta_w', 'delta_w_mlp_up': 'delta_w', 'delta_w_mlp_down': 'delta_w', 'delta_norm_final_g': 'delta_w', 'new_m_norm_mix_g': 'new_m', 'new_m_w_in': 'new_m', 'new_m_conv_w': 'new_m', 'new_m_conv_b': 'new_m', 'new_m_dt_bias': 'new_m', 'new_m_a_log': 'new_m', 'new_m_d_skip': 'new_m', 'new_m_ssm_norm_g': 'new_m', 'new_m_v_norm_g': 'new_m', 'new_m_v_norm_b': 'new_m', 'new_m_w_spatial': 'new_m', 'new_m_b_spatial': 'new_m', 'new_m_b_gates': 'new_m', 'new_m_w_proj_a': 'new_m', 'new_m_w_proj_b': 'new_m', 'new_m_w_out': 'new_m', 'new_m_norm_mlp_g': 'new_m', 'new_m_w_mlp_up': 'new_m', 'new_m_w_mlp_down': 'new_m', 'new_m_norm_final_g': 'new_m', 'new_v_norm_mix_g': 'new_v', 'new_v_w_in': 'new_v', 'new_v_conv_w': 'new_v', 'new_v_conv_b': 'new_v', 'new_v_dt_bias': 'new_v', 'new_v_a_log': 'new_v', 'new_v_d_skip': 'new_v', 'new_v_ssm_norm_g': 'new_v', 'new_v_v_norm_g': 'new_v', 'new_v_v_norm_b': 'new_v', 'new_v_w_spatial': 'new_v', 'new_v_b_spatial': 'new_v', 'new_v_b_gates': 'new_v', 'new_v_w_proj_a': 'new_v', 'new_v_w_proj_b': 'new_v', 'new_v_w_out': 'new_v', 'new_v_norm_mlp_g': 'new_v', 'new_v_w_mlp_up': 'new_v', 'new_v_w_mlp_down': 'new_v', 'new_v_norm_final_g': 'new_v'}


def _forward(args):
    return _fwd_reference(*[args[k] for k in FWD_PARAMS])


def _output_shape():
    def fwd():
        inp = _fwd_setup_inputs(0)
        return _fwd_reference(*[inp[k] for k in FWD_PARAMS])
    out = _jax.eval_shape(fwd)
    return out.shape, out.dtype

N_MICROBATCH = 1
ADAM_LR = 0.001
ADAM_B1 = 0.9
ADAM_B2 = 0.999
ADAM_EPS = 1e-08
ADAM_WD = 0.01
ADAM_STEP = 10
PER_EXAMPLE_BATCH_AXIS = {'x': 0, 'loss_target': 0}
SHARED_INPUTS = []
_WEIGHT_DTYPES = {'norm_mix_g': _jnp.float32, 'w_in': _jnp.float32, 'conv_w': _jnp.float32, 'conv_b': _jnp.float32, 'dt_bias': _jnp.float32, 'a_log': _jnp.float32, 'd_skip': _jnp.float32, 'ssm_norm_g': _jnp.float32, 'v_norm_g': _jnp.float32, 'v_norm_b': _jnp.float32, 'w_spatial': _jnp.float32, 'b_spatial': _jnp.float32, 'b_gates': _jnp.float32, 'w_proj_a': _jnp.float32, 'w_proj_b': _jnp.float32, 'w_out': _jnp.float32, 'norm_mlp_g': _jnp.float32, 'w_mlp_up': _jnp.float32, 'w_mlp_down': _jnp.float32, 'norm_final_g': _jnp.float32}
MOMENT_SCALE = {'norm_mix_g': 2.214545e-01, 'w_in': 7.156750e-02, 'conv_w': 6.550258e-02, 'conv_b': 1.002162e-01, 'dt_bias': 1.916045e-01, 'a_log': 3.544486e-01, 'd_skip': 4.149710e-01, 'ssm_norm_g': 9.165976e-02, 'v_norm_g': 5.600507e-02, 'v_norm_b': 6.056199e-02, 'w_spatial': 5.542070e-02, 'b_spatial': 8.209252e-02, 'b_gates': 4.192975e-02, 'w_proj_a': 1.042266e-01, 'w_proj_b': 1.244192e-01, 'w_out': 1.633469e-01, 'norm_mlp_g': 2.290960e-01, 'w_mlp_up': 1.043489e-01, 'w_mlp_down': 2.213073e-01, 'norm_final_g': 6.471471e+01}


def _to_microbatches(a, axis):
    t = _jnp.moveaxis(a, axis, 0)
    t = t.reshape((N_MICROBATCH, t.shape[0] // N_MICROBATCH) + t.shape[1:])
    return _jnp.moveaxis(t, 1, axis + 1)


def setup_inputs(seed: int = 0) -> dict:
    inp = _fwd_setup_inputs(seed)
    key = _jax.random.fold_in(_jax.random.key(seed), 7919)
    shape, _ = _output_shape()
    out = dict(inp)
    out["loss_target"] = _jax.random.normal(_jax.random.fold_in(key, 0), shape, _jnp.float32)
    for i, name in enumerate(TWIN_WEIGHTS):
        w = inp[name].astype(_jnp.float32)
        if MOMENT_SCALE is None:
            s = _jnp.sqrt(_jnp.mean(_jnp.square(w)) + 1e-30)
        else:
            s = MOMENT_SCALE[name]
        km, kv = _jax.random.split(_jax.random.fold_in(key, i + 1))
        out[name] = w
        out["m_" + name] = s * _jax.random.normal(km, w.shape, _jnp.float32)
        out["v_" + name] = (s * s) * _jax.random.uniform(kv, w.shape, _jnp.float32, 0.5, 1.5)
    if N_MICROBATCH > 1:
        for name, axis in PER_EXAMPLE_BATCH_AXIS.items():
            out[name] = _to_microbatches(out[name], axis)
    return {'x': out['x'], 'norm_mix_g': out['norm_mix_g'], 'w_in': out['w_in'], 'conv_w': out['conv_w'], 'conv_b': out['conv_b'], 'dt_bias': out['dt_bias'], 'a_log': out['a_log'], 'd_skip': out['d_skip'], 'ssm_norm_g': out['ssm_norm_g'], 'v_norm_g': out['v_norm_g'], 'v_norm_b': out['v_norm_b'], 'w_spatial': out['w_spatial'], 'b_spatial': out['b_spatial'], 'b_gates': out['b_gates'], 'w_proj_a': out['w_proj_a'], 'w_proj_b': out['w_proj_b'], 'w_out': out['w_out'], 'norm_mlp_g': out['norm_mlp_g'], 'w_mlp_up': out['w_mlp_up'], 'w_mlp_down': out['w_mlp_down'], 'norm_final_g': out['norm_final_g'], 'loss_target': out['loss_target'], 'm_norm_mix_g': out['m_norm_mix_g'], 'm_w_in': out['m_w_in'], 'm_conv_w': out['m_conv_w'], 'm_conv_b': out['m_conv_b'], 'm_dt_bias': out['m_dt_bias'], 'm_a_log': out['m_a_log'], 'm_d_skip': out['m_d_skip'], 'm_ssm_norm_g': out['m_ssm_norm_g'], 'm_v_norm_g': out['m_v_norm_g'], 'm_v_norm_b': out['m_v_norm_b'], 'm_w_spatial': out['m_w_spatial'], 'm_b_spatial': out['m_b_spatial'], 'm_b_gates': out['m_b_gates'], 'm_w_proj_a': out['m_w_proj_a'], 'm_w_proj_b': out['m_w_proj_b'], 'm_w_out': out['m_w_out'], 'm_norm_mlp_g': out['m_norm_mlp_g'], 'm_w_mlp_up': out['m_w_mlp_up'], 'm_w_mlp_down': out['m_w_mlp_down'], 'm_norm_final_g': out['m_norm_final_g'], 'v_norm_mix_g': out['v_norm_mix_g'], 'v_w_in': out['v_w_in'], 'v_conv_w': out['v_conv_w'], 'v_conv_b': out['v_conv_b'], 'v_dt_bias': out['v_dt_bias'], 'v_a_log': out['v_a_log'], 'v_d_skip': out['v_d_skip'], 'v_ssm_norm_g': out['v_ssm_norm_g'], 'v_v_norm_g': out['v_v_norm_g'], 'v_v_norm_b': out['v_v_norm_b'], 'v_w_spatial': out['v_w_spatial'], 'v_b_spatial': out['v_b_spatial'], 'v_b_gates': out['v_b_gates'], 'v_w_proj_a': out['v_w_proj_a'], 'v_w_proj_b': out['v_w_proj_b'], 'v_w_out': out['v_w_out'], 'v_norm_mlp_g': out['v_norm_mlp_g'], 'v_w_mlp_up': out['v_w_mlp_up'], 'v_w_mlp_down': out['v_w_mlp_down'], 'v_norm_final_g': out['v_norm_final_g']}


def _loss(weights, diff, rest, loss_target):
    with _jax.named_scope("forward"):
        args = {**rest, TWIN_DIFF_INPUT: diff, **{k: w.astype(_WEIGHT_DTYPES[k]) for k, w in weights.items()}}
        y = _forward(args)
    with _jax.named_scope("loss_head"):
        err = _jnp.square(y.astype(_jnp.float32) - loss_target)
        return 0.5 * _jnp.sum(_jnp.mean(err, axis=-1)) if err.ndim else 0.5 * err


def _adamw(w, g, m, v):
    m = ADAM_B1 * m + (1.0 - ADAM_B1) * g
    v = ADAM_B2 * v + (1.0 - ADAM_B2) * _jnp.square(g)
    m_hat = m / (1.0 - ADAM_B1 ** ADAM_STEP)
    v_hat = v / (1.0 - ADAM_B2 ** ADAM_STEP)
    delta = -ADAM_LR * (m_hat / (_jnp.sqrt(v_hat) + ADAM_EPS) + ADAM_WD * w)
    return delta, m, v


def reference(x, norm_mix_g, w_in, conv_w, conv_b, dt_bias, a_log, d_skip, ssm_norm_g, v_norm_g, v_norm_b, w_spatial, b_spatial, b_gates, w_proj_a, w_proj_b, w_out, norm_mlp_g, w_mlp_up, w_mlp_down, norm_final_g, loss_target, m_norm_mix_g, m_w_in, m_conv_w, m_conv_b, m_dt_bias, m_a_log, m_d_skip, m_ssm_norm_g, m_v_norm_g, m_v_norm_b, m_w_spatial, m_b_spatial, m_b_gates, m_w_proj_a, m_w_proj_b, m_w_out, m_norm_mlp_g, m_w_mlp_up, m_w_mlp_down, m_norm_final_g, v_norm_mix_g, v_w_in, v_conv_w, v_conv_b, v_dt_bias, v_a_log, v_d_skip, v_ssm_norm_g, v_v_norm_g, v_v_norm_b, v_w_spatial, v_b_spatial, v_b_gates, v_w_proj_a, v_w_proj_b, v_w_out, v_norm_mlp_g, v_w_mlp_up, v_w_mlp_down, v_norm_final_g):
    given = dict(x=x, norm_mix_g=norm_mix_g, w_in=w_in, conv_w=conv_w, conv_b=conv_b, dt_bias=dt_bias, a_log=a_log, d_skip=d_skip, ssm_norm_g=ssm_norm_g, v_norm_g=v_norm_g, v_norm_b=v_norm_b, w_spatial=w_spatial, b_spatial=b_spatial, b_gates=b_gates, w_proj_a=w_proj_a, w_proj_b=w_proj_b, w_out=w_out, norm_mlp_g=norm_mlp_g, w_mlp_up=w_mlp_up, w_mlp_down=w_mlp_down, norm_final_g=norm_final_g, loss_target=loss_target, m_norm_mix_g=m_norm_mix_g, m_w_in=m_w_in, m_conv_w=m_conv_w, m_conv_b=m_conv_b, m_dt_bias=m_dt_bias, m_a_log=m_a_log, m_d_skip=m_d_skip, m_ssm_norm_g=m_ssm_norm_g, m_v_norm_g=m_v_norm_g, m_v_norm_b=m_v_norm_b, m_w_spatial=m_w_spatial, m_b_spatial=m_b_spatial, m_b_gates=m_b_gates, m_w_proj_a=m_w_proj_a, m_w_proj_b=m_w_proj_b, m_w_out=m_w_out, m_norm_mlp_g=m_norm_mlp_g, m_w_mlp_up=m_w_mlp_up, m_w_mlp_down=m_w_mlp_down, m_norm_final_g=m_norm_final_g, v_norm_mix_g=v_norm_mix_g, v_w_in=v_w_in, v_conv_w=v_conv_w, v_conv_b=v_conv_b, v_dt_bias=v_dt_bias, v_a_log=v_a_log, v_d_skip=v_d_skip, v_ssm_norm_g=v_ssm_norm_g, v_v_norm_g=v_v_norm_g, v_v_norm_b=v_v_norm_b, v_w_spatial=v_w_spatial, v_b_spatial=v_b_spatial, v_b_gates=v_b_gates, v_w_proj_a=v_w_proj_a, v_w_proj_b=v_w_proj_b, v_w_out=v_w_out, v_norm_mlp_g=v_norm_mlp_g, v_w_mlp_up=v_w_mlp_up, v_w_mlp_down=v_w_mlp_down, v_norm_final_g=v_norm_final_g)
    weights = {n: given[n] for n in TWIN_WEIGHTS}
    shared = {n: given[n] for n in SHARED_INPUTS}
    per_example = {n: given[n] for n in ['x']}
    grad_fn = _jax.value_and_grad(_loss, argnums=(0, 1))

    def one_microbatch(ex, loss_target):
        ex = dict(ex)
        diff = ex.pop(TWIN_DIFF_INPUT)
        return grad_fn(weights, diff, {**shared, **ex}, loss_target)

    if N_MICROBATCH == 1:
        loss, (grad_w, grad_x) = one_microbatch(per_example, given["loss_target"])
    else:
        def body(carry, xs):
            loss_sum, grad_sum = carry
            l_k, (gw_k, gx_k) = one_microbatch(xs[0], xs[1])
            with _jax.named_scope("update"):
                return (loss_sum + l_k, _jax.tree.map(_jnp.add, grad_sum, gw_k)), gx_k

        init = (_jnp.zeros((), _jnp.float32), _jax.tree.map(_jnp.zeros_like, weights))
        (loss, grad_w), grad_x = _jax.lax.scan(body, init, (per_example, given["loss_target"]))
    with _jax.named_scope("update"):
        delta_w, new_m, new_v = {}, {}, {}
        for n in TWIN_WEIGHTS:
            delta_w[n], new_m[n], new_v[n] = _adamw(weights[n], grad_w[n], given["m_" + n], given["v_" + n])
    return (loss, grad_x, *[grad_w[n] for n in TWIN_WEIGHTS], *[delta_w[n] for n in TWIN_WEIGHTS],
            *[new_m[n] for n in TWIN_WEIGHTS], *[new_v[n] for n in TWIN_WEIGHTS])
```

```python
import functools
import math

import jax
import jax.numpy as jnp
from jax import lax
from jax.experimental import pallas as pl
from jax.experimental.pallas import tpu as pltpu

f32 = jnp.float32
bf16 = jnp.bfloat16

D_MODEL = 1024
CHUNK = 128
GMLP_WIDTH = 1024
GMLP_GROUPS = 8
D_INNER = 2048
HEAD_DIM = 64
N_HEADS = 32
N_GROUPS = 8
HEADS_PER_GROUP = 4
GROUP_W = HEADS_PER_GROUP * HEAD_DIM
D_STATE = 128
CONV_W = 4
CONV_DIM = 4096
D_FF = 4096
IN_PROJ = 10272
NORM_EPS = 1e-6
N_CHIPS = 4
N_DEV = 8
LANES = 128

ADAM_LR = 0.001
ADAM_B1 = 0.9
ADAM_B2 = 0.999
ADAM_EPS = 1e-08
ADAM_WD = 0.01
ADAM_STEP = 10

MESH = pl.DeviceIdType.MESH
_NT = (((1,), (1,)), ((), ()))
_NN = (((1,), (0,)), ((), ()))
_TN = (((0,), (0,)), ((), ()))
_MB = 2 ** 20


def _params(sem, vmem_mb=48):
    return pltpu.CompilerParams(dimension_semantics=sem, vmem_limit_bytes=vmem_mb * _MB)


def _dot(a, b, dims=_NN):
    return lax.dot_general(a.astype(bf16), b.astype(bf16), dims, preferred_element_type=f32)


def _dot32(a, b):
    return jnp.dot(a, b, preferred_element_type=f32, precision=lax.Precision.HIGHEST)


def _sigmoid(x):
    return 1.0 / (1.0 + jnp.exp(-x))


def _sum_all(a):
    return jnp.sum(jnp.sum(a, axis=1, keepdims=True), axis=0, keepdims=True)


def _iota(shape, dim):
    return lax.broadcasted_iota(jnp.int32, shape, dim)


def _matmul(a, b, *, nt=False, tm, tn, tk, out_dtypes, epilogue=None, extras=(), name):
    m, k_dim = a.shape
    n = b.shape[0] if nt else b.shape[1]
    nk = k_dim // tk
    ne, no = len(extras), len(out_dtypes)
    dims = _NT if nt else _NN

    def body(*refs):
        a_ref, b_ref = refs[0], refs[1]
        ex = refs[2:2 + ne]
        outs = refs[2 + ne:2 + ne + no]

        def finish(acc):
            vals = epilogue(acc, *[e[...] for e in ex]) if epilogue is not None else (acc,)
            for o, v in zip(outs, vals):
                o[...] = v.astype(o.dtype)

        part = lax.dot_general(a_ref[...], b_ref[...], dims, preferred_element_type=f32)
        if nk == 1:
            finish(part)
        else:
            acc_ref = refs[-1]
            kk = pl.program_id(2)

            @pl.when(kk == 0)
            def _():
                acc_ref[...] = part

            @pl.when(kk > 0)
            def _():
                acc_ref[...] += part

            @pl.when(kk == nk - 1)
            def _():
                finish(acc_ref[...])

    b_spec = pl.BlockSpec((tn, tk), lambda i, j, k: (j, k)) if nt else pl.BlockSpec((tk, tn), lambda i, j, k: (k, j))
    tile = pl.BlockSpec((tm, tn), lambda i, j, k: (i, j))
    outs = pl.pallas_call(
        body, name=name, grid=(m // tm, n // tn, nk),
        in_specs=[pl.BlockSpec((tm, tk), lambda i, j, k: (i, k)), b_spec] + [tile] * ne,
        out_specs=[tile] * no,
        out_shape=[jax.ShapeDtypeStruct((m, n), dt) for dt in out_dtypes],
        scratch_shapes=[pltpu.VMEM((tm, tn), f32)] if nk > 1 else [],
        compiler_params=_params(("parallel", "parallel", "arbitrary")),
    )(a, b, *extras)
    return outs if no > 1 else outs[0]


def _matmul_tn(a, b, *, tka, tn, tt, name):
    t, ka = a.shape
    n = b.shape[1]

    def body(a_ref, b_ref, o_ref):
        part = lax.dot_general(a_ref[...], b_ref[...], _TN, preferred_element_type=f32)
        kk = pl.program_id(2)

        @pl.when(kk == 0)
        def _():
            o_ref[...] = part

        @pl.when(kk > 0)
        def _():
            o_ref[...] += part

    return pl.pallas_call(
        body, name=name, grid=(ka // tka, n // tn, t // tt),
        in_specs=[pl.BlockSpec((tt, tka), lambda i, j, k: (k, i)), pl.BlockSpec((tt, tn), lambda i, j, k: (k, j))],
        out_specs=pl.BlockSpec((tka, tn), lambda i, j, k: (i, j)),
        out_shape=jax.ShapeDtypeStruct((ka, n), f32),
        compiler_params=_params(("parallel", "parallel", "arbitrary")),
    )(a, b)


def _row_tile(t):
    return min(t, 512)


def _rms_fwd(x, g, *, name):
    t, d = x.shape
    tr = _row_tile(t)

    def body(x_ref, g_ref, h_ref):
        xv = x_ref[...]
        r = lax.rsqrt(jnp.mean(xv * xv, axis=1, keepdims=True) + NORM_EPS)
        h_ref[...] = (xv * r * g_ref[...]).astype(bf16)

    return pl.pallas_call(
        body, name=name, grid=(t // tr,),
        in_specs=[pl.BlockSpec((tr, d), lambda i: (i, 0)), pl.BlockSpec((1, d), lambda i: (0, 0))],
        out_specs=pl.BlockSpec((tr, d), lambda i: (i, 0)),
        out_shape=jax.ShapeDtypeStruct((t, d), bf16),
        compiler_params=_params(("parallel",)),
    )(x, g)


def _rms_bwd(xin, g, dh, dres, *, want_bf16, name):
    t, d = xin.shape
    tr = _row_tile(t)

    def body(x_ref, g_ref, dh_ref, dres_ref, dx_ref, *rest):
        dg_ref = rest[-1]
        xv = x_ref[...]
        r = lax.rsqrt(jnp.mean(xv * xv, axis=1, keepdims=True) + NORM_EPS)
        xn = xv * r
        dhv = dh_ref[...]
        dxn = dhv * g_ref[...]
        dx = dres_ref[...] + r * (dxn - xn * jnp.mean(dxn * xn, axis=1, keepdims=True))
        dx_ref[...] = dx
        if want_bf16:
            rest[0][...] = dx.astype(bf16)
        part = jnp.sum(dhv * xn, axis=0, keepdims=True)

        @pl.when(pl.program_id(0) == 0)
        def _():
            dg_ref[...] = part

        @pl.when(pl.program_id(0) > 0)
        def _():
            dg_ref[...] += part

    row = pl.BlockSpec((tr, d), lambda i: (i, 0))
    vec = pl.BlockSpec((1, d), lambda i: (0, 0))
    out_shape = [jax.ShapeDtypeStruct((t, d), f32)] + ([jax.ShapeDtypeStruct((t, d), bf16)] if want_bf16 else []) \
        + [jax.ShapeDtypeStruct((1, d), f32)]
    return pl.pallas_call(
        body, name=name, grid=(t // tr,),
        in_specs=[row, vec, row, row],
        out_specs=[row] + ([row] if want_bf16 else []) + [vec],
        out_shape=out_shape,
        compiler_params=_params(("arbitrary",)),
    )(xin, g, dh, dres)


def _loss_head(x2, tgt, g, *, name):
    t, d = x2.shape
    tr = _row_tile(t)

    def body(x_ref, t_ref, g_ref, dx_ref, dxb_ref, dg_ref, loss_ref):
        xv = x_ref[...]
        gv = g_ref[...]
        r = lax.rsqrt(jnp.mean(xv * xv, axis=1, keepdims=True) + NORM_EPS)
        xn = xv * r
        e = xn * gv - t_ref[...]
        lpart = jnp.zeros((1, LANES), f32) + 0.5 * _sum_all(jnp.mean(e * e, axis=1, keepdims=True))
        dy = e * (1.0 / d)
        dxn = dy * gv
        dx = r * (dxn - xn * jnp.mean(dxn * xn, axis=1, keepdims=True))
        dx_ref[...] = dx
        dxb_ref[...] = dx.astype(bf16)
        gpart = jnp.sum(dy * xn, axis=0, keepdims=True)

        @pl.when(pl.program_id(0) == 0)
        def _():
            dg_ref[...] = gpart
            loss_ref[...] = lpart

        @pl.when(pl.program_id(0) > 0)
        def _():
            dg_ref[...] += gpart
            loss_ref[...] += lpart

    row = pl.BlockSpec((tr, d), lambda i: (i, 0))
    vec = pl.BlockSpec((1, d), lambda i: (0, 0))
    return pl.pallas_call(
        body, name=name, grid=(t // tr,),
        in_specs=[row, row, vec],
        out_specs=[row, row, vec, pl.BlockSpec((1, LANES), lambda i: (0, 0))],
        out_shape=[jax.ShapeDtypeStruct((t, d), f32), jax.ShapeDtypeStruct((t, d), bf16),
                   jax.ShapeDtypeStruct((1, d), f32), jax.ShapeDtypeStruct((1, LANES), f32)],
        compiler_params=_params(("arbitrary",)),
    )(x2, tgt, g)


def _merge_fwd(pa, pb, gl, bg, *, name):
    t, d = pa.shape
    tr = _row_tile(t)

    def body(pa_ref, pb_ref, gla_ref, glb_ref, bga_ref, bgb_ref, o_ref):
        ga = _sigmoid(gla_ref[...] + bga_ref[...])
        gb = _sigmoid(glb_ref[...] + bgb_ref[...])
        o_ref[...] = (ga * pa_ref[...] + gb * pb_ref[...]).astype(bf16)

    row = pl.BlockSpec((tr, d), lambda i: (i, 0))
    return pl.pallas_call(
        body, name=name, grid=(t // tr,),
        in_specs=[row, row, row, pl.BlockSpec((tr, d), lambda i: (i, 1)),
                  pl.BlockSpec((1, d), lambda i: (0, 0)), pl.BlockSpec((1, d), lambda i: (0, 1))],
        out_specs=row,
        out_shape=jax.ShapeDtypeStruct((t, d), bf16),
        compiler_params=_params(("parallel",)),
    )(pa, pb, gl, gl, bg, bg)


def _merge_bwd(dm, pa, pb, gl, bg, *, name):
    t, d = pa.shape
    tr = _row_tile(t)

    def body(dm_ref, pa_ref, pb_ref, gla_ref, glb_ref, bga_ref, bgb_ref, dpa_ref, dpb_ref, dgl_ref, dbg_ref):
        dmv = dm_ref[...]
        ga = _sigmoid(gla_ref[...] + bga_ref[...])
        gb = _sigmoid(glb_ref[...] + bgb_ref[...])
        dpa_ref[...] = (dmv * ga).astype(bf16)
        dpb_ref[...] = (dmv * gb).astype(bf16)
        dla = dmv * pa_ref[...] * ga * (1.0 - ga)
        dlb = dmv * pb_ref[...] * gb * (1.0 - gb)
        dgl_ref[:, :d] = dla.astype(bf16)
        dgl_ref[:, d:] = dlb.astype(bf16)
        sa = jnp.sum(dla, axis=0, keepdims=True)
        sb = jnp.sum(dlb, axis=0, keepdims=True)

        @pl.when(pl.program_id(0) == 0)
        def _():
            dbg_ref[:, :d] = sa
            dbg_ref[:, d:] = sb

        @pl.when(pl.program_id(0) > 0)
        def _():
            dbg_ref[:, :d] += sa
            dbg_ref[:, d:] += sb

    row = pl.BlockSpec((tr, d), lambda i: (i, 0))
    return pl.pallas_call(
        body, name=name, grid=(t // tr,),
        in_specs=[row, row, row, row, pl.BlockSpec((tr, d), lambda i: (i, 1)),
                  pl.BlockSpec((1, d), lambda i: (0, 0)), pl.BlockSpec((1, d), lambda i: (0, 1))],
        out_specs=[row, row, pl.BlockSpec((tr, 2 * d), lambda i: (i, 0)), pl.BlockSpec((1, 2 * d), lambda i: (0, 0))],
        out_shape=[jax.ShapeDtypeStruct((t, d), bf16), jax.ShapeDtypeStruct((t, d), bf16),
                   jax.ShapeDtypeStruct((t, 2 * d), bf16), jax.ShapeDtypeStruct((1, 2 * d), f32)],
        compiler_params=_params(("arbitrary",)),
    )(dm, pa, pb, gl, gl, bg, bg)


_INV_SQRT2 = 1.0 / math.sqrt(2.0)
_INV_SQRT2PI = 1.0 / math.sqrt(2.0 * math.pi)


def _gelu(x):
    return 0.5 * x * (1.0 + lax.erf(x * _INV_SQRT2))


def _gelu_grad(x):
    return 0.5 * (1.0 + lax.erf(x * _INV_SQRT2)) + x * jnp.exp(-0.5 * x * x) * _INV_SQRT2PI


def _gmlp_common(uv, vg, vb):
    zz = _gelu(uv)
    u = zz[:, :GMLP_WIDTH]
    v = zz[:, GMLP_WIDTH:]
    mu = jnp.mean(v, axis=1, keepdims=True)
    vc = v - mu
    rstd = lax.rsqrt(jnp.mean(vc * vc, axis=1, keepdims=True) + NORM_EPS)
    vhat = vc * rstd
    vn = vhat * vg + vb
    return u, vhat, rstd, vn


def _gmlp_fwd(uv, vg, vb, wsp, bsp_t, *, name):
    t = uv.shape[0]
    nc = t // CHUNK

    def body(uv_ref, vg_ref, vb_ref, w_ref, b_ref, y_ref):
        u, _, _, vn = _gmlp_common(uv_ref[...], vg_ref[...], vb_ref[...])
        tril = _iota((CHUNK, CHUNK), 0) >= _iota((CHUNK, CHUNK), 1)
        bt = b_ref[...]
        for g in range(GMLP_GROUPS):
            sl = slice(g * CHUNK, (g + 1) * CHUNK)
            w = jnp.where(tril, w_ref[g], 0.0)
            s = _dot(w, vn[:, sl]) + bt[:, g:g + 1]
            y_ref[:, sl] = (u[:, sl] * s).astype(bf16)

    return pl.pallas_call(
        body, name=name, grid=(nc,),
        in_specs=[pl.BlockSpec((CHUNK, 2 * GMLP_WIDTH), lambda c: (c, 0)),
                  pl.BlockSpec((1, GMLP_WIDTH), lambda c: (0, 0)), pl.BlockSpec((1, GMLP_WIDTH), lambda c: (0, 0)),
                  pl.BlockSpec((GMLP_GROUPS, CHUNK, CHUNK), lambda c: (0, 0, 0)),
                  pl.BlockSpec((CHUNK, LANES), lambda c: (0, 0))],
        out_specs=pl.BlockSpec((CHUNK, GMLP_WIDTH), lambda c: (c, 0)),
        out_shape=jax.ShapeDtypeStruct((t, GMLP_WIDTH), bf16),
        compiler_params=_params(("parallel",)),
    )(uv, vg, vb, wsp, bsp_t)


def _gmlp_bwd(uv, dya, vg, vb, wsp, bsp_t, *, name):
    t = uv.shape[0]
    nc = t // CHUNK

    def body(uv_ref, dy_ref, vg_ref, vb_ref, w_ref, b_ref, duv_ref, dw_ref, db_ref, dvg_ref, dvb_ref):
        first = pl.program_id(0) == 0

        @pl.when(first)
        def _():
            dw_ref[...] = jnp.zeros_like(dw_ref)
            db_ref[...] = jnp.zeros_like(db_ref)
            dvg_ref[...] = jnp.zeros_like(dvg_ref)
            dvb_ref[...] = jnp.zeros_like(dvb_ref)

        uvv = uv_ref[...]
        vgv = vg_ref[...]
        u, vhat, rstd, vn = _gmlp_common(uvv, vgv, vb_ref[...])
        dy = dy_ref[...]
        tril = _iota((CHUNK, CHUNK), 0) >= _iota((CHUNK, CHUNK), 1)
        lane = _iota((CHUNK, LANES), 1)
        bt = b_ref[...]
        ds_all = dy * u
        dbacc = jnp.zeros((CHUNK, LANES), f32)
        dvh_parts = []
        for g in range(GMLP_GROUPS):
            sl = slice(g * CHUNK, (g + 1) * CHUNK)
            w = jnp.where(tril, w_ref[g], 0.0)
            vng = vn[:, sl]
            s = _dot(w, vng) + bt[:, g:g + 1]
            ds = ds_all[:, sl]
            duv_ref[:, sl] = (dy[:, sl] * s * _gelu_grad(uvv[:, sl])).astype(bf16)
            dw_ref[g] += jnp.where(tril, _dot(ds, vng, _NT), 0.0)
            dbacc = dbacc + jnp.where(lane == g, jnp.sum(ds, axis=1, keepdims=True), 0.0)
            dvn = _dot(w, ds, _TN)
            vh = vhat[:, sl]
            dvg_ref[:, sl] += jnp.sum(dvn * vh, axis=0, keepdims=True)
            dvb_ref[:, sl] += jnp.sum(dvn, axis=0, keepdims=True)
            dvh_parts.append(dvn * vgv[:, sl])
        db_ref[...] += dbacc
        dvhat = jnp.concatenate(dvh_parts, axis=1)
        m1 = jnp.mean(dvhat, axis=1, keepdims=True)
        m2 = jnp.mean(dvhat * vhat, axis=1, keepdims=True)
        dv = rstd * (dvhat - m1 - vhat * m2)
        duv_ref[:, GMLP_WIDTH:] = (dv * _gelu_grad(uvv[:, GMLP_WIDTH:])).astype(bf16)

    vec = pl.BlockSpec((1, GMLP_WIDTH), lambda c: (0, 0))
    return pl.pallas_call(
        body, name=name, grid=(nc,),
        in_specs=[pl.BlockSpec((CHUNK, 2 * GMLP_WIDTH), lambda c: (c, 0)),
                  pl.BlockSpec((CHUNK, GMLP_WIDTH), lambda c: (c, 0)), vec, vec,
                  pl.BlockSpec((GMLP_GROUPS, CHUNK, CHUNK), lambda c: (0, 0, 0)),
                  pl.BlockSpec((CHUNK, LANES), lambda c: (0, 0))],
        out_specs=[pl.BlockSpec((CHUNK, 2 * GMLP_WIDTH), lambda c: (c, 0)),
                   pl.BlockSpec((GMLP_GROUPS, CHUNK, CHUNK), lambda c: (0, 0, 0)),
                   pl.BlockSpec((CHUNK, LANES), lambda c: (0, 0)), vec, vec],
        out_shape=[jax.ShapeDtypeStruct((t, 2 * GMLP_WIDTH), bf16),
                   jax.ShapeDtypeStruct((GMLP_GROUPS, CHUNK, CHUNK), f32),
                   jax.ShapeDtypeStruct((CHUNK, LANES), f32),
                   jax.ShapeDtypeStruct((1, GMLP_WIDTH), f32), jax.ShapeDtypeStruct((1, GMLP_WIDTH), f32)],
        compiler_params=_params(("arbitrary",)),
    )(uv, dya, vg, vb, wsp, bsp_t)


_CONV_COLS = 512
_XS0, _B0, _C0 = 0, D_INNER, D_INNER + N_GROUPS * D_STATE


def _conv_silu(cur_ref, prev_ref, w_ref, b_ref, has_prev, xc_ref, cv_ref):
    row = _iota((CHUNK, _CONV_COLS), 0)
    for j in range(CONV_DIM // _CONV_COLS):
        sl = slice(j * _CONV_COLS, (j + 1) * _CONV_COLS)
        cur = cur_ref[:, sl]
        prev = jnp.where(has_prev, prev_ref[:, sl], 0.0)
        acc = cur * w_ref[CONV_W - 1:CONV_W, sl] + b_ref[:, sl]
        for s in range(1, CONV_W):
            sh = jnp.where(row >= s, pltpu.roll(cur, s, 0), pltpu.roll(prev, s, 0))
            acc = acc + sh * w_ref[CONV_W - 1 - s:CONV_W - s, sl]
        if cv_ref is not None:
            cv_ref[:, sl] = acc
        xc_ref[:, sl] = acc * _sigmoid(acc)


def _ssd_scalars(dtr, dtb, alog):
    xdt = dtr + dtb
    dtv = jnp.maximum(xdt, 0.0) + jnp.log(1.0 + jnp.exp(-jnp.abs(xdt)))
    a = -jnp.exp(alog)
    ltri = (_iota((CHUNK, CHUNK), 0) >= _iota((CHUNK, CHUNK), 1)).astype(f32)
    cs = _dot32(ltri, dtv * a)
    return xdt, dtv, a, cs, cs.T


def _ssd_head_fwd(cs, cs_t, dtv, dsk, xs_g, cbm, goff, tril, h, r):
    psl = slice(r * HEAD_DIM, (r + 1) * HEAD_DIM)
    csc = cs[:, h:h + 1]
    lmat = jnp.exp(jnp.where(tril, csc - cs_t[h:h + 1, :], -1e30))
    dtc = dtv[:, h:h + 1]
    xs_h = xs_g[:, psl]
    xdt = xs_h * dtc
    mmat = cbm * lmat
    e = jnp.exp(csc)
    yoff = e * goff[:, psl]
    y = _dot(mmat, xdt) + yoff + dsk[:, h:h + 1] * xs_h
    cl = cs[CHUNK - 1:CHUNK, h:h + 1]
    dec = jnp.exp(cl - csc)
    return dict(lmat=lmat, dtc=dtc, xs=xs_h, xdt=xdt, mmat=mmat, e=e, yoff=yoff, y=y, cl=cl, dec=dec)


def _ssd_fwd(xbc, z, dtr, cw, cb, dtb, alog, dsk, gs, *, name):
    t = xbc.shape[0]
    nc = t // CHUNK

    def body(cur_ref, prev_ref, z_ref, dtr_ref, cw_ref, cb_ref, dtb_ref, alog_ref, dsk_ref, gs_ref,
             yb_ref, hp_ref, state_ref, xc_ref):
        c = pl.program_id(0)

        @pl.when(c == 0)
        def _():
            state_ref[...] = jnp.zeros_like(state_ref)

        _conv_silu(cur_ref, prev_ref, cw_ref, cb_ref, c > 0, xc_ref, None)
        _, dtv, _, cs, cs_t = _ssd_scalars(dtr_ref[...], dtb_ref[...], alog_ref[...])
        dsk_v = dsk_ref[...]
        tril = _iota((CHUNK, CHUNK), 0) >= _iota((CHUNK, CHUNK), 1)
        hp_ref[0] = state_ref[...]
        for g in range(N_GROUPS):
            gsl = slice(g * GROUP_W, (g + 1) * GROUP_W)
            xs_g = xc_ref[:, gsl]
            bg = xc_ref[:, _B0 + g * D_STATE:_B0 + (g + 1) * D_STATE]
            cg = xc_ref[:, _C0 + g * D_STATE:_C0 + (g + 1) * D_STATE]
            cbm = _dot(cg, bg, _NT)
            hg = state_ref[gsl, :]
            goff = _dot(cg, hg, _NT)
            ys = []
            for r in range(HEADS_PER_GROUP):
                h = g * HEADS_PER_GROUP + r
                hd = _ssd_head_fwd(cs, cs_t, dtv, dsk_v, xs_g, cbm, goff, tril, h, r)
                ys.append(hd["y"])
                s_new = _dot(hd["xdt"] * hd["dec"], bg, _TN)
                rows = slice(g * GROUP_W + r * HEAD_DIM, g * GROUP_W + (r + 1) * HEAD_DIM)
                state_ref[rows, :] = hg[r * HEAD_DIM:(r + 1) * HEAD_DIM, :] * jnp.exp(hd["cl"]) + s_new
            y_g = jnp.concatenate(ys, axis=1)
            zg = z_ref[:, gsl]
            yg = y_g * zg * _sigmoid(zg)
            rs = lax.rsqrt(jnp.mean(yg * yg, axis=1, keepdims=True) + NORM_EPS)
            yb_ref[:, gsl] = (yg * rs * gs_ref[:, gsl]).astype(bf16)

    def chunk(w):
        return pl.BlockSpec((CHUNK, w), lambda c: (c, 0))

    def const(shape):
        return pl.BlockSpec(shape, lambda c: (0,) * len(shape))

    return pl.pallas_call(
        body, name=name, grid=(nc,),
        in_specs=[chunk(CONV_DIM), pl.BlockSpec((CHUNK, CONV_DIM), lambda c: (jnp.maximum(c - 1, 0), 0)),
                  chunk(D_INNER), chunk(LANES), const((CONV_W, CONV_DIM)), const((1, CONV_DIM)),
                  const((1, LANES)), const((1, LANES)), const((1, LANES)), const((1, D_INNER))],
        out_specs=[chunk(D_INNER), pl.BlockSpec((1, N_HEADS * HEAD_DIM, D_STATE), lambda c: (c, 0, 0))],
        out_shape=[jax.ShapeDtypeStruct((t, D_INNER), bf16),
                   jax.ShapeDtypeStruct((nc, N_HEADS * HEAD_DIM, D_STATE), f32)],
        scratch_shapes=[pltpu.VMEM((N_HEADS * HEAD_DIM, D_STATE), f32), pltpu.VMEM((CHUNK, CONV_DIM), f32)],
        compiler_params=_params(("arbitrary",)),
    )(xbc, xbc, z, dtr, cw, cb, dtb, alog, dsk, gs)


def _ssd_bwd(xbc, z, dtr, hprev, dyb, cw, cb, dtb, alog, dsk, gs, *, name):
    t = xbc.shape[0]
    nc = t // CHUNK

    def body(cur_ref, prev_ref, z_ref, dtr_ref, hp_ref, dyb_ref, cw_ref, cb_ref, dtb_ref, alog_ref, dsk_ref, gs_ref,
             dz_ref, dxbc_ref, ddt_ref, dcw_ref, dcb_ref, ddtb_ref, dalog_ref, ddsk_ref, dgs_ref,
             dh_ref, dcnext_ref, xc_ref, cv_ref, dxc_ref):
        i = pl.program_id(0)
        cc = nc - 1 - i

        @pl.when(i == 0)
        def _():
            for ref in (dh_ref, dcnext_ref, dcw_ref, dcb_ref, ddtb_ref, dalog_ref, ddsk_ref, dgs_ref):
                ref[...] = jnp.zeros_like(ref)

        _conv_silu(cur_ref, prev_ref, cw_ref, cb_ref, cc > 0, xc_ref, cv_ref)
        xdt_pre, dtv, a, cs, cs_t = _ssd_scalars(dtr_ref[...], dtb_ref[...], alog_ref[...])
        dsk_v = dsk_ref[...]
        tril = _iota((CHUNK, CHUNK), 0) >= _iota((CHUNK, CHUNK), 1)
        lane = _iota((CHUNK, LANES), 1)
        rowi = _iota((CHUNK, LANES), 0)
        lane1 = _iota((1, LANES), 1)
        dcs_mat = jnp.zeros((CHUNK, LANES), f32)
        dcs_t_mat = jnp.zeros((LANES, CHUNK), f32)
        ddt_mat = jnp.zeros((CHUNK, LANES), f32)
        ddsk_acc = jnp.zeros((1, LANES), f32)
        for g in range(N_GROUPS):
            gsl = slice(g * GROUP_W, (g + 1) * GROUP_W)
            xs_g = xc_ref[:, gsl]
            bg = xc_ref[:, _B0 + g * D_STATE:_B0 + (g + 1) * D_STATE]
            cg = xc_ref[:, _C0 + g * D_STATE:_C0 + (g + 1) * D_STATE]
            cbm = _dot(cg, bg, _NT)
            hg = hp_ref[0, gsl, :]
            goff = _dot(cg, hg, _NT)
            heads = [_ssd_head_fwd(cs, cs_t, dtv, dsk_v, xs_g, cbm, goff, tril, g * HEADS_PER_GROUP + r, r)
                     for r in range(HEADS_PER_GROUP)]
            y_g = jnp.concatenate([hd["y"] for hd in heads], axis=1)
            zg = z_ref[:, gsl]
            sz = _sigmoid(zg)
            silu = zg * sz
            yg = y_g * silu
            rs = lax.rsqrt(jnp.mean(yg * yg, axis=1, keepdims=True) + NORM_EPS)
            yn = yg * rs
            dyb = dyb_ref[:, gsl]
            dgs_ref[:, gsl] += jnp.sum(dyb * yn, axis=0, keepdims=True)
            dyn = dyb * gs_ref[:, gsl]
            dyg = rs * (dyn - yn * jnp.mean(dyn * yn, axis=1, keepdims=True))
            dy_g = dyg * silu
            dz_ref[:, gsl] = (dyg * y_g * (sz * (1.0 + zg * (1.0 - sz)))).astype(bf16)
            dcb_acc = jnp.zeros((CHUNK, CHUNK), f32)
            dhn_g = dh_ref[gsl, :]
            dg_parts, dxs_parts, xdtdec_parts = [], [], []
            for r in range(HEADS_PER_GROUP):
                hd = heads[r]
                h = g * HEADS_PER_GROUP + r
                psl = slice(r * HEAD_DIM, (r + 1) * HEAD_DIM)
                d_y = dy_g[:, psl]
                ddsk_acc = ddsk_acc + jnp.where(lane1 == h, _sum_all(d_y * hd["xs"]), 0.0)
                dxs = dsk_v[:, h:h + 1] * d_y
                dcs_col = jnp.sum(d_y * hd["yoff"], axis=1, keepdims=True)
                dg_parts.append(hd["e"] * d_y)
                dm = _dot(d_y, hd["xdt"], _NT)
                dxdt = _dot(hd["mmat"], d_y, _TN)
                dml = dm * hd["lmat"]
                dcb_acc = dcb_acc + dml
                dseg = dml * cbm
                dcs_col = dcs_col + jnp.sum(dseg, axis=1, keepdims=True)
                dcs_row = -jnp.sum(dseg, axis=0, keepdims=True)
                dhn = dhn_g[psl, :]
                dk = jnp.exp(hd["cl"])
                dcl = _sum_all(dhn * hg[psl, :]) * dk
                wm = _dot(bg, dhn, _NT)
                dxdt = dxdt + hd["dec"] * wm
                tdec = jnp.sum(hd["xdt"] * wm, axis=1, keepdims=True) * hd["dec"]
                dcl = dcl + _sum_all(tdec)
                dcs_col = dcs_col - tdec
                xdtdec_parts.append(hd["xdt"] * hd["dec"])
                dxs_parts.append(dxs + dxdt * hd["dtc"])
                ddt_col = jnp.sum(dxdt * hd["xs"], axis=1, keepdims=True)
                dcs_mat = dcs_mat + jnp.where(lane == h, dcs_col, 0.0) \
                    + jnp.where((lane == h) & (rowi == CHUNK - 1), dcl, 0.0)
                dcs_t_mat = dcs_t_mat + jnp.where(rowi == h, dcs_row, 0.0)
                ddt_mat = ddt_mat + jnp.where(lane == h, ddt_col, 0.0)
                rows = slice(g * GROUP_W + r * HEAD_DIM, g * GROUP_W + (r + 1) * HEAD_DIM)
                dh_ref[rows, :] = dhn * dk
            dg_g = jnp.concatenate(dg_parts, axis=1)
            d_c = _dot(dg_g, hg) + _dot(dcb_acc, bg)
            d_b = _dot(dcb_acc, cg, _TN) + _dot(jnp.concatenate(xdtdec_parts, axis=1), dhn_g)
            dh_ref[gsl, :] += _dot(dg_g, cg, _TN)
            dxc_ref[:, gsl] = jnp.concatenate(dxs_parts, axis=1)
            dxc_ref[:, _B0 + g * D_STATE:_B0 + (g + 1) * D_STATE] = d_b
            dxc_ref[:, _C0 + g * D_STATE:_C0 + (g + 1) * D_STATE] = d_c
        utri = (_iota((CHUNK, CHUNK), 0) <= _iota((CHUNK, CHUNK), 1)).astype(f32)
        dda = _dot32(utri, dcs_mat + dcs_t_mat.T)
        ddt_total = ddt_mat + dda * a
        dalog_ref[...] += jnp.sum(dda * dtv, axis=0, keepdims=True) * a
        ddtr = jnp.where(lane < N_HEADS, ddt_total * _sigmoid(xdt_pre), 0.0)
        ddtb_ref[...] += jnp.sum(ddtr, axis=0, keepdims=True)
        ddt_ref[...] = ddtr.astype(bf16)
        ddsk_ref[...] += ddsk_acc
        row = _iota((CHUNK, _CONV_COLS), 0)
        has_prev = cc > 0
        for j in range(CONV_DIM // _CONV_COLS):
            sl = slice(j * _CONV_COLS, (j + 1) * _CONV_COLS)
            cvv = cv_ref[:, sl]
            sg = _sigmoid(cvv)
            dconv = dxc_ref[:, sl] * (sg * (1.0 + cvv * (1.0 - sg)))
            nxt = dcnext_ref[:, sl]
            cur = cur_ref[:, sl]
            prev = jnp.where(has_prev, prev_ref[:, sl], 0.0)
            dxin = dconv * cw_ref[CONV_W - 1:CONV_W, sl]
            dcw_ref[CONV_W - 1:CONV_W, sl] += jnp.sum(dconv * cur, axis=0, keepdims=True)
            for s in range(1, CONV_W):
                up = jnp.where(row < CHUNK - s, pltpu.roll(dconv, CHUNK - s, 0), pltpu.roll(nxt, CHUNK - s, 0))
                dxin = dxin + up * cw_ref[CONV_W - 1 - s:CONV_W - s, sl]
                sh = jnp.where(row >= s, pltpu.roll(cur, s, 0), pltpu.roll(prev, s, 0))
                dcw_ref[CONV_W - 1 - s:CONV_W - s, sl] += jnp.sum(dconv * sh, axis=0, keepdims=True)
            dcb_ref[:, sl] += jnp.sum(dconv, axis=0, keepdims=True)
            dxbc_ref[:, sl] = dxin.astype(bf16)
            dcnext_ref[:, sl] = dconv

    def chunk(w):
        return pl.BlockSpec((CHUNK, w), lambda i: (nc - 1 - i, 0))

    def const(shape):
        return pl.BlockSpec(shape, lambda i: (0,) * len(shape))

    hp_rows = N_HEADS * HEAD_DIM
    return pl.pallas_call(
        body, name=name, grid=(nc,),
        in_specs=[chunk(CONV_DIM), pl.BlockSpec((CHUNK, CONV_DIM), lambda i: (jnp.maximum(nc - 2 - i, 0), 0)),
                  chunk(D_INNER), chunk(LANES), pl.BlockSpec((1, hp_rows, D_STATE), lambda i: (nc - 1 - i, 0, 0)),
                  chunk(D_INNER), const((CONV_W, CONV_DIM)), const((1, CONV_DIM)),
                  const((1, LANES)), const((1, LANES)), const((1, LANES)), const((1, D_INNER))],
        out_specs=[chunk(D_INNER), chunk(CONV_DIM), chunk(LANES), const((CONV_W, CONV_DIM)), const((1, CONV_DIM)),
                   const((1, LANES)), const((1, LANES)), const((1, LANES)), const((1, D_INNER))],
        out_shape=[jax.ShapeDtypeStruct((t, D_INNER), bf16), jax.ShapeDtypeStruct((t, CONV_DIM), bf16),
                   jax.ShapeDtypeStruct((t, LANES), bf16), jax.ShapeDtypeStruct((CONV_W, CONV_DIM), f32),
                   jax.ShapeDtypeStruct((1, CONV_DIM), f32), jax.ShapeDtypeStruct((1, LANES), f32),
                   jax.ShapeDtypeStruct((1, LANES), f32), jax.ShapeDtypeStruct((1, LANES), f32),
                   jax.ShapeDtypeStruct((1, D_INNER), f32)],
        scratch_shapes=[pltpu.VMEM((hp_rows, D_STATE), f32), pltpu.VMEM((CHUNK, CONV_DIM), f32),
                        pltpu.VMEM((CHUNK, CONV_DIM), f32), pltpu.VMEM((CHUNK, CONV_DIM), f32),
                        pltpu.VMEM((CHUNK, CONV_DIM), f32)],
        compiler_params=_params(("arbitrary",)),
    )(xbc, xbc, z, dtr, hprev, dyb, cw, cb, dtb, alog, dsk, gs)


def _adamw(w, g, m, v, *, name):
    r, c = w.shape
    tr = r
    while tr * c * 4 > _MB and tr % 16 == 0:
        tr //= 2

    def body(w_ref, g_ref, m_ref, v_ref, d_ref, m2_ref, v2_ref):
        gv = g_ref[...]
        m2 = ADAM_B1 * m_ref[...] + (1.0 - ADAM_B1) * gv
        v2 = ADAM_B2 * v_ref[...] + (1.0 - ADAM_B2) * (gv * gv)
        m_hat = m2 / (1.0 - ADAM_B1 ** ADAM_STEP)
        v_hat = v2 / (1.0 - ADAM_B2 ** ADAM_STEP)
        d_ref[...] = -ADAM_LR * (m_hat / (jnp.sqrt(v_hat) + ADAM_EPS) + ADAM_WD * w_ref[...])
        m2_ref[...] = m2
        v2_ref[...] = v2

    blk = pl.BlockSpec((tr, c), lambda i: (i, 0))
    return pl.pallas_call(
        body, name=name, grid=(r // tr,),
        in_specs=[blk] * 4, out_specs=[blk] * 3,
        out_shape=[jax.ShapeDtypeStruct((r, c), f32)] * 3,
        compiler_params=_params(("parallel",)),
    )(w, g, m, v)


def _cast_bf16(a, *, name):
    r, c = a.shape
    tr = 2256 if r % 2256 == 0 else r

    def body(a_ref, o_ref):
        o_ref[...] = a_ref[...].astype(bf16)

    blk = pl.BlockSpec((tr, c), lambda i: (i, 0))
    return pl.pallas_call(
        body, name=name, grid=(r // tr,), in_specs=[blk], out_specs=blk,
        out_shape=jax.ShapeDtypeStruct((r, c), bf16), compiler_params=_params(("parallel",)),
    )(a)


_ANY = pl.BlockSpec(memory_space=pl.ANY)


def _place():
    x, y, c = lax.axis_index("x"), lax.axis_index("y"), lax.axis_index("c")
    other_chips = [(1 - x, y), (x, 1 - y), (1 - x, 1 - y)]
    return x, y, c, other_chips


def _gather_shards(shard, *, name):
    _, rh, lanes = shard.shape

    def body(in_ref, out_ref, send_sems, recv_sems, local_sem):
        x, y, c, chips = _place()
        me = 2 * x + y
        sibling = (x, y, 1 - c)

        def cp(k, chip, half, to, src=None):
            dst = out_ref.at[chip, half]
            return pltpu.make_async_remote_copy(
                src_ref=dst if src is None else src, dst_ref=dst, send_sem=send_sems.at[k], recv_sem=recv_sems.at[k],
                device_id=to, device_id_type=MESH)

        mine = pltpu.make_async_copy(in_ref, out_ref.at[me], local_sem)
        mine.start()
        first = [cp(j, me, c, (cx, cy, c), src=in_ref.at[c]) for j, (cx, cy) in enumerate(chips)]
        for f in first:
            f.start()
        passed = []
        for j, (cx, cy) in enumerate(chips):
            cp(j, 2 * cx + cy, c, sibling).wait_recv()
            p = cp(3 + j, 2 * cx + cy, c, sibling)
            p.start()
            passed.append(p)
        for j, (cx, cy) in enumerate(chips):
            cp(3 + j, 2 * cx + cy, 1 - c, sibling).wait_recv()
        for f in first + passed:
            f.wait_send()
        mine.wait()

    return pl.pallas_call(
        body, name=name, in_specs=[_ANY], out_specs=_ANY,
        out_shape=jax.ShapeDtypeStruct((N_CHIPS, 2, rh, lanes), shard.dtype),
        scratch_shapes=[pltpu.SemaphoreType.DMA((6,)), pltpu.SemaphoreType.DMA((6,)), pltpu.SemaphoreType.DMA],
    )(shard)


def _rs_swap_halves(g, *, name):
    nch, _, rh, lanes = g.shape

    def body(g_ref, out_ref, send_sems, recv_sems):
        x, y, c, _ = _place()
        copies = [pltpu.make_async_remote_copy(
            src_ref=g_ref.at[k, 1 - c], dst_ref=out_ref.at[k], send_sem=send_sems.at[k], recv_sem=recv_sems.at[k],
            device_id=(x, y, 1 - c), device_id_type=MESH) for k in range(nch)]
        for cpy in copies:
            cpy.start()
        for cpy in copies:
            cpy.wait()

    return pl.pallas_call(
        body, name=name, in_specs=[_ANY], out_specs=_ANY,
        out_shape=jax.ShapeDtypeStruct((nch, rh, lanes), g.dtype),
        scratch_shapes=[pltpu.SemaphoreType.DMA((nch,)), pltpu.SemaphoreType.DMA((nch,))],
    )(g)


def _rs_add_pair(g, got, c_idx, *, name):
    nch, _, rh, lanes = g.shape
    tr = rh // 10 if rh % 160 == 0 else rh

    def body(c_ref, g_ref, got_ref, p32_ref, p16_ref):
        s = g_ref[...] + got_ref[...]
        p32_ref[...] = s
        p16_ref[...] = s.astype(bf16)

    blk = pl.BlockSpec((None, tr, lanes), lambda k, i, c_ref: (k, i, 0))
    return pl.pallas_call(
        body, name=name,
        grid_spec=pltpu.PrefetchScalarGridSpec(
            num_scalar_prefetch=1, grid=(nch, rh // tr),
            in_specs=[pl.BlockSpec((None, None, tr, lanes), lambda k, i, c_ref: (k, c_ref[0], i, 0)), blk],
            out_specs=[blk, blk]),
        out_shape=[jax.ShapeDtypeStruct((nch, rh, lanes), f32), jax.ShapeDtypeStruct((nch, rh, lanes), bf16)],
        compiler_params=_params(("parallel", "parallel")),
    )(c_idx, g, got)


def _rs_scatter_chips(p16, *, name):
    _, rh, lanes = p16.shape

    def body(p_ref, out_ref, send_sems, recv_sems):
        x, y, c, chips = _place()
        copies = [pltpu.make_async_remote_copy(
            src_ref=p_ref.at[2 * cx + cy], dst_ref=out_ref.at[j], send_sem=send_sems.at[j], recv_sem=recv_sems.at[j],
            device_id=(cx, cy, c), device_id_type=MESH) for j, (cx, cy) in enumerate(chips)]
        for cpy in copies:
            cpy.start()
        for cpy in copies:
            cpy.wait()

    return pl.pallas_call(
        body, name=name, in_specs=[_ANY], out_specs=_ANY,
        out_shape=jax.ShapeDtypeStruct((3, rh, lanes), p16.dtype),
        scratch_shapes=[pltpu.SemaphoreType.DMA((3,)), pltpu.SemaphoreType.DMA((3,))],
    )(p16)


def _rs_add_chips(p32, got, me_idx, *, name):
    _, rh, lanes = p32.shape
    tr = rh // 10 if rh % 160 == 0 else rh

    def body(me_ref, p_ref, got_ref, o_ref):
        o_ref[...] = ((p_ref[...] + got_ref[0].astype(f32)) + got_ref[1].astype(f32)) + got_ref[2].astype(f32)

    return pl.pallas_call(
        body, name=name,
        grid_spec=pltpu.PrefetchScalarGridSpec(
            num_scalar_prefetch=1, grid=(rh // tr,),
            in_specs=[pl.BlockSpec((None, tr, lanes), lambda i, me_ref: (me_ref[0], i, 0)),
                      pl.BlockSpec((3, tr, lanes), lambda i, me_ref: (0, i, 0))],
            out_specs=pl.BlockSpec((tr, lanes), lambda i, me_ref: (i, 0))),
        out_shape=jax.ShapeDtypeStruct((rh, lanes), f32),
        compiler_params=_params(("parallel",)),
    )(me_idx, p32, got)


def _rs_join_halves(half, *, name):
    rh, lanes = half.shape

    def body(h_ref, out_ref, send_sem, recv_sem, local_sem):
        x, y, c, _ = _place()
        mine = pltpu.make_async_copy(h_ref, out_ref.at[c], local_sem)
        mine.start()
        cpy = pltpu.make_async_remote_copy(
            src_ref=h_ref, dst_ref=out_ref.at[c], send_sem=send_sem, recv_sem=recv_sem,
            device_id=(x, y, 1 - c), device_id_type=MESH)
        cpy.start()
        cpy.wait()
        mine.wait()

    return pl.pallas_call(
        body, name=name, in_specs=[_ANY], out_specs=_ANY,
        out_shape=jax.ShapeDtypeStruct((2, rh, lanes), half.dtype),
        scratch_shapes=[pltpu.SemaphoreType.DMA, pltpu.SemaphoreType.DMA, pltpu.SemaphoreType.DMA],
    )(half)


def _all_reduce_small(s, *, name):
    rs, lanes = s.shape

    def body(s_ref, o_ref, buf_ref, send_sems, recv_sems):
        x, y, c, _ = _place()
        me = 4 * x + 2 * y + c
        peers = []
        for k in range(1, N_DEV):
            px = 1 - x if (k >> 2) & 1 else x
            py = 1 - y if (k >> 1) & 1 else y
            pc = 1 - c if k & 1 else c
            peers.append((px, py, pc))
        copies = [pltpu.make_async_remote_copy(
            src_ref=s_ref, dst_ref=buf_ref.at[me], send_sem=send_sems.at[k], recv_sem=recv_sems.at[k],
            device_id=peer, device_id_type=MESH) for k, peer in enumerate(peers)]
        for cpy in copies:
            cpy.start()
        buf_ref[me] = s_ref[...]
        for k, (px, py, pc) in enumerate(peers):
            pltpu.make_async_remote_copy(
                src_ref=s_ref, dst_ref=buf_ref.at[4 * px + 2 * py + pc], send_sem=send_sems.at[k],
                recv_sem=recv_sems.at[k], device_id=(px, py, pc), device_id_type=MESH).wait_recv()
        for cpy in copies:
            cpy.wait_send()
        acc = buf_ref[0]
        for d in range(1, N_DEV):
            acc = acc + buf_ref[d]
        o_ref[...] = acc

    vm = pl.BlockSpec(memory_space=pltpu.VMEM)
    return pl.pallas_call(
        body, name=name, in_specs=[vm], out_specs=vm,
        out_shape=jax.ShapeDtypeStruct((rs, lanes), f32),
        scratch_shapes=[pltpu.VMEM((N_DEV, rs, lanes), f32), pltpu.SemaphoreType.DMA((N_DEV - 1,)),
                        pltpu.SemaphoreType.DMA((N_DEV - 1,))],
        compiler_params=pltpu.CompilerParams(vmem_limit_bytes=32 * _MB),
    )(s)


def _pad_lanes(a, width=LANES):
    return jnp.pad(a, ((0, 0), (0, width - a.shape[1])))


def _local_grads(x, tgt, wts, small):
    t = x.shape[0]
    tm = min(t, 512)
    d = D_MODEL
    mm = functools.partial(_matmul, tm=tm)

    dtb = _pad_lanes(small["dt_bias"])
    alog = _pad_lanes(small["a_log"])
    dsk = _pad_lanes(small["d_skip"])
    bsp_t = _pad_lanes(small["b_spatial"].T)
    wsp = small["w_spatial"]

    h = _rms_fwd(x, small["norm_mix_g"], name="rms_mix")
    uv = mm(h, wts["uv"], tn=1024, tk=d, out_dtypes=[f32], name="proj_uv")
    z = mm(h, wts["z"], tn=1024, tk=d, out_dtypes=[f32], name="proj_z")
    xbc = mm(h, wts["xbc"], tn=1024, tk=d, out_dtypes=[f32], name="proj_xbc")
    dtr = mm(h, wts["dt"], tn=LANES, tk=d, out_dtypes=[f32], name="proj_dt")
    gl = mm(h, wts["gate"], tn=1024, tk=d, out_dtypes=[f32], name="proj_gate")
    ya = _gmlp_fwd(uv, small["v_norm_g"], small["v_norm_b"], wsp, bsp_t, name="gmlp_fwd")
    yb, hprev = _ssd_fwd(xbc, z, dtr, small["conv_w"], small["conv_b"], dtb, alog, dsk, small["ssm_norm_g"],
                         name="ssd_fwd")
    pa = mm(ya, wts["pa"], tn=1024, tk=1024, out_dtypes=[f32], name="proj_a")
    pb = mm(yb, wts["pb"], tn=1024, tk=1024, out_dtypes=[f32], name="proj_b")
    merged = _merge_fwd(pa, pb, gl, small["b_gates"], name="merge_fwd")
    x1 = mm(merged, wts["out"], tn=1024, tk=1024, out_dtypes=[f32], extras=[x],
            epilogue=lambda acc, res: (res + acc,), name="out_proj")
    h2 = _rms_fwd(x1, small["norm_mlp_g"], name="rms_mlp")
    up, act = mm(h2, wts["up"], tn=1024, tk=d, out_dtypes=[f32, bf16],
                 epilogue=lambda acc: (acc, jnp.square(jnp.maximum(acc, 0.0))), name="mlp_up")
    x2 = mm(act, wts["down"], tn=1024, tk=1024, out_dtypes=[f32], extras=[x1],
            epilogue=lambda acc, res: (res + acc,), name="mlp_down")

    dx2, dx2b, dgf, loss = _loss_head(x2, tgt, small["norm_final_g"], name="loss_head")
    tt = min(t, 512)
    tn_mm = functools.partial(_matmul_tn, tt=tt)
    dw = {}
    dw["down"] = tn_mm(act, dx2b, tka=1024, tn=1024, name="dw_down")
    dup = mm(dx2b, wts["down"], nt=True, tn=1024, tk=1024, out_dtypes=[bf16], extras=[up],
             epilogue=lambda acc, u: (acc * (2.0 * jnp.maximum(u, 0.0)),), name="d_act")
    dw["up"] = tn_mm(h2, dup, tka=1024, tn=1024, name="dw_up")
    dh2 = mm(dup, wts["up"], nt=True, tn=1024, tk=1024, out_dtypes=[f32], name="d_h2")
    dx1, dx1b, dg_mlp = _rms_bwd(x1, small["norm_mlp_g"], dh2, dx2, want_bf16=True, name="rms_mlp_bwd")
    dw["out"] = tn_mm(merged, dx1b, tka=1024, tn=1024, name="dw_out")
    dmerged = mm(dx1b, wts["out"], nt=True, tn=1024, tk=1024, out_dtypes=[f32], name="d_merged")
    dpa, dpb, dgl, dbg = _merge_bwd(dmerged, pa, pb, gl, small["b_gates"], name="merge_bwd")
    dw["pa"] = tn_mm(ya, dpa, tka=1024, tn=1024, name="dw_pa")
    dw["pb"] = tn_mm(yb, dpb, tka=1024, tn=1024, name="dw_pb")
    dya = mm(dpa, wts["pa"], nt=True, tn=1024, tk=1024, out_dtypes=[f32], name="d_ya")
    dyb = mm(dpb, wts["pb"], nt=True, tn=1024, tk=1024, out_dtypes=[f32], name="d_yb")
    duv, dwsp, dbsp_t, dvg, dvb = _gmlp_bwd(uv, dya, small["v_norm_g"], small["v_norm_b"], wsp, bsp_t,
                                            name="gmlp_bwd")
    dz, dxbc, ddt, dcw, dcb, ddtb, dalog, ddsk, dgs = _ssd_bwd(
        xbc, z, dtr, hprev, dyb, small["conv_w"], small["conv_b"], dtb, alog, dsk, small["ssm_norm_g"],
        name="ssd_bwd")
    dw["uv"] = tn_mm(h, duv, tka=1024, tn=1024, name="dw_uv")
    dw["z"] = tn_mm(h, dz, tka=1024, tn=1024, name="dw_z")
    dw["xbc"] = tn_mm(h, dxbc, tka=1024, tn=1024, name="dw_xbc")
    dw["dt"] = tn_mm(h, ddt, tka=1024, tn=LANES, name="dw_dt")
    dw["gate"] = tn_mm(h, dgl, tka=1024, tn=1024, name="dw_gate")
    add = lambda acc, prev: (prev + acc,)
    dh = mm(duv, wts["uv"], nt=True, tn=1024, tk=1024, out_dtypes=[f32], name="d_h_uv")
    dh = mm(dz, wts["z"], nt=True, tn=1024, tk=1024, out_dtypes=[f32], extras=[dh], epilogue=add, name="d_h_z")
    dh = mm(dxbc, wts["xbc"], nt=True, tn=1024, tk=1024, out_dtypes=[f32], extras=[dh], epilogue=add, name="d_h_xbc")
    dh = mm(dgl, wts["gate"], nt=True, tn=1024, tk=1024, out_dtypes=[f32], extras=[dh], epilogue=add, name="d_h_gate")
    dh = mm(ddt, wts["dt"], nt=True, tn=1024, tk=LANES, out_dtypes=[f32], extras=[dh], epilogue=add, name="d_h_dt")
    dx, dg_mix = _rms_bwd(x, small["norm_mix_g"], dh, dx1, want_bf16=False, name="rms_mix_bwd")

    dsmall = {
        "norm_mix_g": dg_mix, "conv_w": dcw, "conv_b": dcb, "dt_bias": ddtb[:, :N_HEADS], "a_log": dalog[:, :N_HEADS],
        "d_skip": ddsk[:, :N_HEADS], "ssm_norm_g": dgs, "v_norm_g": dvg, "v_norm_b": dvb, "w_spatial": dwsp,
        "b_spatial": dbsp_t[:, :GMLP_GROUPS].T, "b_gates": dbg, "norm_mlp_g": dg_mlp, "norm_final_g": dgf,
    }
    return loss, dx, dw, dsmall


_IN_SHARD = IN_PROJ // N_CHIPS
_DENSE = ("w_in", "w_proj_a", "w_proj_b", "w_out", "w_mlp_up", "w_mlp_down")
_DENSE_SHARD_SHAPES = {"w_in": (D_MODEL, _IN_SHARD), "w_proj_a": (GMLP_WIDTH // N_CHIPS, D_MODEL),
                       "w_proj_b": (D_INNER // N_CHIPS, D_MODEL), "w_out": (D_MODEL // N_CHIPS, D_MODEL),
                       "w_mlp_up": (D_MODEL, D_FF // N_CHIPS), "w_mlp_down": (D_FF // N_CHIPS, D_MODEL)}
_DENSE_ROWS = {k: s[0] * s[1] // LANES for k, s in _DENSE_SHARD_SHAPES.items()}
_DENSE_TOTAL = sum(_DENSE_ROWS.values())
_CONV_ROWS = CONV_W * (CONV_DIM // N_CHIPS) * 2 // LANES


def _dense_offsets():
    off, out = 0, {}
    for k in _DENSE:
        out[k] = off
        off += _DENSE_ROWS[k]
    return out


_DENSE_OFF = _dense_offsets()

_SMALL = ("norm_mix_g", "conv_w", "conv_b", "dt_bias", "a_log", "d_skip", "ssm_norm_g", "v_norm_g", "v_norm_b",
          "w_spatial", "b_spatial", "b_gates", "norm_mlp_g", "norm_final_g")


def _pack_small(parts):
    flat = jnp.concatenate([parts[k].reshape(-1) for k in _SMALL])
    rows = -(-flat.shape[0] // (8 * LANES)) * 8
    return jnp.pad(flat, (0, rows * LANES - flat.shape[0])).reshape(rows, LANES)


def _unpack_small(packed, shapes):
    flat = packed.reshape(-1)
    out, off = {}, 0
    for k in _SMALL:
        n = math.prod(shapes[k])
        out[k] = flat[off:off + n].reshape(shapes[k])
        off += n
    return out


def _from_chip_columns(stacked, rows, cols):
    return stacked.reshape(N_CHIPS, rows, cols).transpose(1, 0, 2).reshape(rows, N_CHIPS * cols)


def _to_chip_columns(full, cols):
    rows = full.shape[0]
    return full.reshape(rows, N_CHIPS, cols).transpose(1, 0, 2).reshape(N_CHIPS, rows * cols // LANES, LANES)


def kernel(x, norm_mix_g, w_in, conv_w, conv_b, dt_bias, a_log, d_skip, ssm_norm_g, v_norm_g, v_norm_b, w_spatial, b_spatial, b_gates, w_proj_a, w_proj_b, w_out, norm_mlp_g, w_mlp_up, w_mlp_down, norm_final_g, loss_target, m_norm_mix_g, m_w_in, m_conv_w, m_conv_b, m_dt_bias, m_a_log, m_d_skip, m_ssm_norm_g, m_v_norm_g, m_v_norm_b, m_w_spatial, m_b_spatial, m_b_gates, m_w_proj_a, m_w_proj_b, m_w_out, m_norm_mlp_g, m_w_mlp_up, m_w_mlp_down, m_norm_final_g, v_norm_mix_g, v_w_in, v_conv_w, v_conv_b, v_dt_bias, v_a_log, v_d_skip, v_ssm_norm_g, v_v_norm_g, v_v_norm_b, v_w_spatial, v_b_spatial, v_b_gates, v_w_proj_a, v_w_proj_b, v_w_out, v_norm_mlp_g, v_w_mlp_up, v_w_mlp_down, v_norm_final_g):
    given = dict(locals())
    names = ("norm_mix_g", "w_in", "conv_w", "conv_b", "dt_bias", "a_log", "d_skip", "ssm_norm_g", "v_norm_g",
             "v_norm_b", "w_spatial", "b_spatial", "b_gates", "w_proj_a", "w_proj_b", "w_out", "norm_mlp_g",
             "w_mlp_up", "w_mlp_down", "norm_final_g")
    xi, yi, ci = lax.axis_index("x"), lax.axis_index("y"), lax.axis_index("c")
    me_chip = (2 * xi + yi).astype(jnp.int32)

    dense_f32 = jnp.concatenate([given[k][0].reshape(-1, LANES) for k in _DENSE])
    dense_b16 = _cast_bf16(dense_f32, name="cast_weights")
    conv_shard = conv_w.reshape(CONV_W, CONV_DIM // N_CHIPS)
    conv_bits = lax.bitcast_convert_type(conv_shard.reshape(-1, LANES), bf16).reshape(_CONV_ROWS, LANES)
    shard = jnp.concatenate([dense_b16, conv_bits]).reshape(2, (_DENSE_TOTAL + _CONV_ROWS) // 2, LANES)
    gathered = _gather_shards(shard, name="gather_weights").reshape(N_CHIPS, _DENSE_TOTAL + _CONV_ROWS, LANES)

    def rows_of(k):
        return gathered[:, _DENSE_OFF[k]:_DENSE_OFF[k] + _DENSE_ROWS[k]]

    w_in_full = _from_chip_columns(rows_of("w_in"), D_MODEL, _IN_SHARD)
    o_dt, o_gate = 2 * GMLP_WIDTH + D_INNER + CONV_DIM, 2 * GMLP_WIDTH + D_INNER + CONV_DIM + N_HEADS
    wts = {
        "uv": w_in_full[:, :2 * GMLP_WIDTH], "z": w_in_full[:, 2 * GMLP_WIDTH:2 * GMLP_WIDTH + D_INNER],
        "xbc": w_in_full[:, 2 * GMLP_WIDTH + D_INNER:o_dt], "dt": _pad_lanes(w_in_full[:, o_dt:o_gate]),
        "gate": w_in_full[:, o_gate:],
        "pa": rows_of("w_proj_a").reshape(GMLP_WIDTH, D_MODEL), "pb": rows_of("w_proj_b").reshape(D_INNER, D_MODEL),
        "out": rows_of("w_out").reshape(D_MODEL, D_MODEL),
        "up": _from_chip_columns(rows_of("w_mlp_up"), D_MODEL, D_FF // N_CHIPS),
        "down": rows_of("w_mlp_down").reshape(D_FF, D_MODEL),
    }
    conv_all = lax.bitcast_convert_type(
        gathered[:, _DENSE_TOTAL:].reshape(N_CHIPS, _CONV_ROWS // 2, LANES, 2), f32)
    conv_full = conv_all.reshape(N_CHIPS, CONV_W, CONV_DIM // N_CHIPS).transpose(1, 0, 2).reshape(CONV_W, CONV_DIM)

    small = {
        "norm_mix_g": norm_mix_g, "conv_w": conv_full, "conv_b": conv_b, "dt_bias": dt_bias, "a_log": a_log,
        "d_skip": d_skip, "ssm_norm_g": ssm_norm_g, "v_norm_g": v_norm_g, "v_norm_b": v_norm_b,
        "w_spatial": w_spatial[0], "b_spatial": b_spatial[0], "b_gates": b_gates, "norm_mlp_g": norm_mlp_g,
        "norm_final_g": norm_final_g.reshape(1, D_MODEL),
    }

    loss_part, grad_x, dw, dsmall = _local_grads(x[0], loss_target[0], wts, small)
    loss = lax.psum(loss_part[0, 0], ("x", "y", "c"))

    dw_in = jnp.concatenate([dw["uv"], dw["z"], dw["xbc"], dw["dt"][:, :N_HEADS], dw["gate"]], axis=1)
    per_chip = {
        "w_in": _to_chip_columns(dw_in, _IN_SHARD), "w_proj_a": dw["pa"].reshape(N_CHIPS, -1, LANES),
        "w_proj_b": dw["pb"].reshape(N_CHIPS, -1, LANES), "w_out": dw["out"].reshape(N_CHIPS, -1, LANES),
        "w_mlp_up": _to_chip_columns(dw["up"], D_FF // N_CHIPS), "w_mlp_down": dw["down"].reshape(N_CHIPS, -1, LANES),
    }
    g_all = jnp.concatenate([per_chip[k] for k in _DENSE], axis=1).reshape(N_CHIPS, 2, _DENSE_TOTAL // 2, LANES)
    c_idx = ci.astype(jnp.int32).reshape(1)
    got_pair = _rs_swap_halves(g_all, name="rs_swap_halves")
    p32, p16 = _rs_add_pair(g_all, got_pair, c_idx, name="rs_add_pair")
    got_chips = _rs_scatter_chips(p16, name="rs_scatter_chips")
    half = _rs_add_chips(p32, got_chips, me_chip.reshape(1), name="rs_add_chips")
    g_shard = _rs_join_halves(half, name="rs_join_halves").reshape(_DENSE_TOTAL, LANES)

    small_shapes = {k: dsmall[k].shape for k in _SMALL}
    red = _unpack_small(_all_reduce_small(_pack_small(dsmall), name="all_reduce_small"), small_shapes)
    conv_cols = CONV_DIM // N_CHIPS
    red["conv_w"] = lax.dynamic_slice_in_dim(red["conv_w"], me_chip * conv_cols, conv_cols, axis=1)

    grads, deltas, new_m, new_v = {}, {}, {}, {}
    for k in _DENSE:
        shp = _DENSE_SHARD_SHAPES[k]
        g2 = g_shard[_DENSE_OFF[k]:_DENSE_OFF[k] + _DENSE_ROWS[k]].reshape(shp)
        dlt, m2, v2 = _adamw(given[k][0], g2, given["m_" + k][0], given["v_" + k][0], name="adamw_" + k)
        grads[k], deltas[k], new_m[k], new_v[k] = g2, dlt, m2, v2
    adam_shapes = dict(small_shapes)
    adam_shapes["conv_w"] = (CONV_W, conv_cols)

    def small_pack_of(prefix):
        return _pack_small({k: given[prefix + k].reshape(adam_shapes[k]) for k in _SMALL})

    dlt_s, m_s, v_s = _adamw(small_pack_of(""), _pack_small(red), small_pack_of("m_"), small_pack_of("v_"),
                             name="adamw_small")
    for dst, packed in ((deltas, dlt_s), (new_m, m_s), (new_v, v_s)):
        dst.update(_unpack_small(packed, adam_shapes))
    grads.update(red)

    def shaped(dct):
        return [dct[k].reshape(given[k].shape) for k in names]

    return (loss, grad_x[None], *shaped(grads), *shaped(deltas), *shaped(new_m), *shaped(new_v))
```

```python
import functools
import math

import jax
import jax.numpy as jnp
from jax import lax
from jax.experimental import pallas as pl
from jax.experimental.pallas import tpu as pltpu

f32 = jnp.float32
bf16 = jnp.bfloat16

D_MODEL = 1024
CHUNK = 128
GMLP_WIDTH = 1024
GMLP_GROUPS = 8
D_INNER = 2048
HEAD_DIM = 64
N_HEADS = 32
N_GROUPS = 8
HEADS_PER_GROUP = 4
GROUP_W = HEADS_PER_GROUP * HEAD_DIM
D_STATE = 128
CONV_W = 4
CONV_DIM = 4096
D_FF = 4096
IN_PROJ = 10272
NORM_EPS = 1e-6
N_CHIPS = 4
N_DEV = 8
LANES = 128

ADAM_LR = 0.001
ADAM_B1 = 0.9
ADAM_B2 = 0.999
ADAM_EPS = 1e-08
ADAM_WD = 0.01
ADAM_STEP = 10

MESH = pl.DeviceIdType.MESH
_NT = (((1,), (1,)), ((), ()))
_NN = (((1,), (0,)), ((), ()))
_TN = (((0,), (0,)), ((), ()))
_MB = 2 ** 20


def _params(sem, vmem_mb=48):
    return pltpu.CompilerParams(dimension_semantics=sem, vmem_limit_bytes=vmem_mb * _MB)


def _dot(a, b, dims=_NN):
    return lax.dot_general(a.astype(bf16), b.astype(bf16), dims, preferred_element_type=f32)


def _dot32(a, b):
    return jnp.dot(a, b, preferred_element_type=f32, precision=lax.Precision.HIGHEST)


def _sigmoid(x):
    return 1.0 / (1.0 + jnp.exp(-x))


def _sum_all(a):
    return jnp.sum(jnp.sum(a, axis=1, keepdims=True), axis=0, keepdims=True)


def _iota(shape, dim):
    return lax.broadcasted_iota(jnp.int32, shape, dim)


def _matmul(a, b, *, nt=False, tm, tn, tk, out_dtypes, epilogue=None, extras=(), name):
    m, k_dim = a.shape
    n = b.shape[0] if nt else b.shape[1]
    nk = k_dim // tk
    ne, no = len(extras), len(out_dtypes)
    dims = _NT if nt else _NN

    def body(*refs):
        a_ref, b_ref = refs[0], refs[1]
        ex = refs[2:2 + ne]
        outs = refs[2 + ne:2 + ne + no]

        def finish(acc):
            vals = epilogue(acc, *[e[...] for e in ex]) if epilogue is not None else (acc,)
            for o, v in zip(outs, vals):
                o[...] = v.astype(o.dtype)

        part = lax.dot_general(a_ref[...], b_ref[...], dims, preferred_element_type=f32)
        if nk == 1:
            finish(part)
        else:
            acc_ref = refs[-1]
            kk = pl.program_id(2)

            @pl.when(kk == 0)
            def _():
                acc_ref[...] = part

            @pl.when(kk > 0)
            def _():
                acc_ref[...] += part

            @pl.when(kk == nk - 1)
            def _():
                finish(acc_ref[...])

    b_spec = pl.BlockSpec((tn, tk), lambda i, j, k: (j, k)) if nt else pl.BlockSpec((tk, tn), lambda i, j, k: (k, j))
    tile = pl.BlockSpec((tm, tn), lambda i, j, k: (i, j))
    outs = pl.pallas_call(
        body, name=name, grid=(m // tm, n // tn, nk),
        in_specs=[pl.BlockSpec((tm, tk), lambda i, j, k: (i, k)), b_spec] + [tile] * ne,
        out_specs=[tile] * no,
        out_shape=[jax.ShapeDtypeStruct((m, n), dt) for dt in out_dtypes],
        scratch_shapes=[pltpu.VMEM((tm, tn), f32)] if nk > 1 else [],
        compiler_params=_params(("parallel", "parallel", "arbitrary")),
    )(a, b, *extras)
    return outs if no > 1 else outs[0]


def _matmul_nt_sum(pairs, extra, *, tm, tk, name):
    m = pairs[0][0].shape[0]
    n = pairs[0][1].shape[0]
    nblk = [a.shape[1] // tk for a, _ in pairs]
    starts = [sum(nblk[:p]) for p in range(len(pairs))]
    nk = sum(nblk)
    npairs = len(pairs)

    def body(*refs):
        extra_ref, o_ref, acc_ref = refs[2 * npairs], refs[2 * npairs + 1], refs[2 * npairs + 2]
        kk = pl.program_id(1)

        @pl.when(kk == 0)
        def _():
            acc_ref[...] = extra_ref[...]

        for p in range(npairs):
            @pl.when((kk >= starts[p]) & (kk < starts[p] + nblk[p]))
            def _(p=p):
                acc_ref[...] += lax.dot_general(refs[2 * p][...], refs[2 * p + 1][...], _NT, preferred_element_type=f32)

        @pl.when(kk == nk - 1)
        def _():
            o_ref[...] = acc_ref[...]

    in_specs, args = [], []
    for p, (a, b) in enumerate(pairs):
        def kblock(k, s=starts[p], nb=nblk[p]):
            return jnp.clip(k - s, 0, nb - 1)
        in_specs.append(pl.BlockSpec((tm, tk), lambda i, k, kb=kblock: (i, kb(k))))
        in_specs.append(pl.BlockSpec((n, tk), lambda i, k, kb=kblock: (0, kb(k))))
        args += [a, b]
    tile = pl.BlockSpec((tm, n), lambda i, k: (i, 0))
    return pl.pallas_call(
        body, name=name, grid=(m // tm, nk), in_specs=in_specs + [tile], out_specs=tile,
        out_shape=jax.ShapeDtypeStruct((m, n), f32), scratch_shapes=[pltpu.VMEM((tm, n), f32)],
        compiler_params=_params(("parallel", "arbitrary")),
    )(*args, extra)


def _matmul_tn(a, b, *, tka, tn, tt, name):
    t, ka = a.shape
    n = b.shape[1]

    def body(a_ref, b_ref, o_ref):
        part = lax.dot_general(a_ref[...], b_ref[...], _TN, preferred_element_type=f32)
        kk = pl.program_id(2)

        @pl.when(kk == 0)
        def _():
            o_ref[...] = part

        @pl.when(kk > 0)
        def _():
            o_ref[...] += part

    return pl.pallas_call(
        body, name=name, grid=(ka // tka, n // tn, t // tt),
        in_specs=[pl.BlockSpec((tt, tka), lambda i, j, k: (k, i)), pl.BlockSpec((tt, tn), lambda i, j, k: (k, j))],
        out_specs=pl.BlockSpec((tka, tn), lambda i, j, k: (i, j)),
        out_shape=jax.ShapeDtypeStruct((ka, n), f32),
        compiler_params=_params(("parallel", "parallel", "arbitrary")),
    )(a, b)


def _row_tile(t):
    return min(t, 512)


def _rms_fwd(x, g, *, name):
    t, d = x.shape
    tr = _row_tile(t)

    def body(x_ref, g_ref, h_ref):
        xv = x_ref[...]
        r = lax.rsqrt(jnp.mean(xv * xv, axis=1, keepdims=True) + NORM_EPS)
        h_ref[...] = (xv * r * g_ref[...]).astype(bf16)

    return pl.pallas_call(
        body, name=name, grid=(t // tr,),
        in_specs=[pl.BlockSpec((tr, d), lambda i: (i, 0)), pl.BlockSpec((1, d), lambda i: (0, 0))],
        out_specs=pl.BlockSpec((tr, d), lambda i: (i, 0)),
        out_shape=jax.ShapeDtypeStruct((t, d), bf16),
        compiler_params=_params(("parallel",)),
    )(x, g)


def _rms_bwd(xin, g, dh, dres, *, want_bf16, name):
    t, d = xin.shape
    tr = _row_tile(t)

    def body(x_ref, g_ref, dh_ref, dres_ref, dx_ref, *rest):
        dg_ref = rest[-1]
        xv = x_ref[...]
        r = lax.rsqrt(jnp.mean(xv * xv, axis=1, keepdims=True) + NORM_EPS)
        xn = xv * r
        dhv = dh_ref[...]
        dxn = dhv * g_ref[...]
        dx = dres_ref[...] + r * (dxn - xn * jnp.mean(dxn * xn, axis=1, keepdims=True))
        dx_ref[...] = dx
        if want_bf16:
            rest[0][...] = dx.astype(bf16)
        part = jnp.sum(dhv * xn, axis=0, keepdims=True)

        @pl.when(pl.program_id(0) == 0)
        def _():
            dg_ref[...] = part

        @pl.when(pl.program_id(0) > 0)
        def _():
            dg_ref[...] += part

    row = pl.BlockSpec((tr, d), lambda i: (i, 0))
    vec = pl.BlockSpec((1, d), lambda i: (0, 0))
    out_shape = [jax.ShapeDtypeStruct((t, d), f32)] + ([jax.ShapeDtypeStruct((t, d), bf16)] if want_bf16 else []) \
        + [jax.ShapeDtypeStruct((1, d), f32)]
    return pl.pallas_call(
        body, name=name, grid=(t // tr,),
        in_specs=[row, vec, row, row],
        out_specs=[row] + ([row] if want_bf16 else []) + [vec],
        out_shape=out_shape,
        compiler_params=_params(("arbitrary",)),
    )(xin, g, dh, dres)


def _loss_head(x2, tgt, g, *, name):
    t, d = x2.shape
    tr = _row_tile(t)

    def body(x_ref, t_ref, g_ref, dx_ref, dxb_ref, dg_ref, loss_ref):
        xv = x_ref[...]
        gv = g_ref[...]
        r = lax.rsqrt(jnp.mean(xv * xv, axis=1, keepdims=True) + NORM_EPS)
        xn = xv * r
        e = xn * gv - t_ref[...]
        lpart = jnp.zeros((1, LANES), f32) + 0.5 * _sum_all(jnp.mean(e * e, axis=1, keepdims=True))
        dy = e * (1.0 / d)
        dxn = dy * gv
        dx = r * (dxn - xn * jnp.mean(dxn * xn, axis=1, keepdims=True))
        dx_ref[...] = dx
        dxb_ref[...] = dx.astype(bf16)
        gpart = jnp.sum(dy * xn, axis=0, keepdims=True)

        @pl.when(pl.program_id(0) == 0)
        def _():
            dg_ref[...] = gpart
            loss_ref[...] = lpart

        @pl.when(pl.program_id(0) > 0)
        def _():
            dg_ref[...] += gpart
            loss_ref[...] += lpart

    row = pl.BlockSpec((tr, d), lambda i: (i, 0))
    vec = pl.BlockSpec((1, d), lambda i: (0, 0))
    return pl.pallas_call(
        body, name=name, grid=(t // tr,),
        in_specs=[row, row, vec],
        out_specs=[row, row, vec, pl.BlockSpec((1, LANES), lambda i: (0, 0))],
        out_shape=[jax.ShapeDtypeStruct((t, d), f32), jax.ShapeDtypeStruct((t, d), bf16),
                   jax.ShapeDtypeStruct((1, d), f32), jax.ShapeDtypeStruct((1, LANES), f32)],
        compiler_params=_params(("arbitrary",)),
    )(x2, tgt, g)


def _merge_fwd(pa, pb, gl, bg, *, name):
    t, d = pa.shape
    tr = _row_tile(t)

    def body(pa_ref, pb_ref, gla_ref, glb_ref, bga_ref, bgb_ref, o_ref):
        ga = _sigmoid(gla_ref[...] + bga_ref[...])
        gb = _sigmoid(glb_ref[...] + bgb_ref[...])
        o_ref[...] = (ga * pa_ref[...] + gb * pb_ref[...]).astype(bf16)

    row = pl.BlockSpec((tr, d), lambda i: (i, 0))
    return pl.pallas_call(
        body, name=name, grid=(t // tr,),
        in_specs=[row, row, row, pl.BlockSpec((tr, d), lambda i: (i, 1)),
                  pl.BlockSpec((1, d), lambda i: (0, 0)), pl.BlockSpec((1, d), lambda i: (0, 1))],
        out_specs=row,
        out_shape=jax.ShapeDtypeStruct((t, d), bf16),
        compiler_params=_params(("parallel",)),
    )(pa, pb, gl, gl, bg, bg)


def _merge_bwd(dm, pa, pb, gl, bg, *, name):
    t, d = pa.shape
    tr = _row_tile(t)

    def body(dm_ref, pa_ref, pb_ref, gla_ref, glb_ref, bga_ref, bgb_ref, dpa_ref, dpb_ref, dgl_ref, dbg_ref):
        dmv = dm_ref[...]
        ga = _sigmoid(gla_ref[...] + bga_ref[...])
        gb = _sigmoid(glb_ref[...] + bgb_ref[...])
        dpa_ref[...] = (dmv * ga).astype(bf16)
        dpb_ref[...] = (dmv * gb).astype(bf16)
        dla = dmv * pa_ref[...] * ga * (1.0 - ga)
        dlb = dmv * pb_ref[...] * gb * (1.0 - gb)
        dgl_ref[:, :d] = dla.astype(bf16)
        dgl_ref[:, d:] = dlb.astype(bf16)
        sa = jnp.sum(dla, axis=0, keepdims=True)
        sb = jnp.sum(dlb, axis=0, keepdims=True)

        @pl.when(pl.program_id(0) == 0)
        def _():
            dbg_ref[:, :d] = sa
            dbg_ref[:, d:] = sb

        @pl.when(pl.program_id(0) > 0)
        def _():
            dbg_ref[:, :d] += sa
            dbg_ref[:, d:] += sb

    row = pl.BlockSpec((tr, d), lambda i: (i, 0))
    return pl.pallas_call(
        body, name=name, grid=(t // tr,),
        in_specs=[row, row, row, row, pl.BlockSpec((tr, d), lambda i: (i, 1)),
                  pl.BlockSpec((1, d), lambda i: (0, 0)), pl.BlockSpec((1, d), lambda i: (0, 1))],
        out_specs=[row, row, pl.BlockSpec((tr, 2 * d), lambda i: (i, 0)), pl.BlockSpec((1, 2 * d), lambda i: (0, 0))],
        out_shape=[jax.ShapeDtypeStruct((t, d), bf16), jax.ShapeDtypeStruct((t, d), bf16),
                   jax.ShapeDtypeStruct((t, 2 * d), bf16), jax.ShapeDtypeStruct((1, 2 * d), f32)],
        compiler_params=_params(("arbitrary",)),
    )(dm, pa, pb, gl, gl, bg, bg)


_INV_SQRT2 = 1.0 / math.sqrt(2.0)
_INV_SQRT2PI = 1.0 / math.sqrt(2.0 * math.pi)


def _gelu(x):
    return 0.5 * x * (1.0 + lax.erf(x * _INV_SQRT2))


def _gelu_grad(x):
    return 0.5 * (1.0 + lax.erf(x * _INV_SQRT2)) + x * jnp.exp(-0.5 * x * x) * _INV_SQRT2PI


def _gmlp_common(uv, vg, vb):
    zz = _gelu(uv)
    u = zz[:, :GMLP_WIDTH]
    v = zz[:, GMLP_WIDTH:]
    mu = jnp.mean(v, axis=1, keepdims=True)
    vc = v - mu
    rstd = lax.rsqrt(jnp.mean(vc * vc, axis=1, keepdims=True) + NORM_EPS)
    vhat = vc * rstd
    vn = vhat * vg + vb
    return u, vhat, rstd, vn


def _gmlp_fwd(uv, vg, vb, wsp, bsp_t, *, name):
    t = uv.shape[0]
    nc = t // CHUNK

    def body(uv_ref, vg_ref, vb_ref, w_ref, b_ref, y_ref):
        u, _, _, vn = _gmlp_common(uv_ref[...], vg_ref[...], vb_ref[...])
        tril = _iota((CHUNK, CHUNK), 0) >= _iota((CHUNK, CHUNK), 1)
        bt = b_ref[...]
        for g in range(GMLP_GROUPS):
            sl = slice(g * CHUNK, (g + 1) * CHUNK)
            w = jnp.where(tril, w_ref[g], 0.0)
            s = _dot(w, vn[:, sl]) + bt[:, g:g + 1]
            y_ref[:, sl] = (u[:, sl] * s).astype(bf16)

    return pl.pallas_call(
        body, name=name, grid=(nc,),
        in_specs=[pl.BlockSpec((CHUNK, 2 * GMLP_WIDTH), lambda c: (c, 0)),
                  pl.BlockSpec((1, GMLP_WIDTH), lambda c: (0, 0)), pl.BlockSpec((1, GMLP_WIDTH), lambda c: (0, 0)),
                  pl.BlockSpec((GMLP_GROUPS, CHUNK, CHUNK), lambda c: (0, 0, 0)),
                  pl.BlockSpec((CHUNK, LANES), lambda c: (0, 0))],
        out_specs=pl.BlockSpec((CHUNK, GMLP_WIDTH), lambda c: (c, 0)),
        out_shape=jax.ShapeDtypeStruct((t, GMLP_WIDTH), bf16),
        compiler_params=_params(("parallel",)),
    )(uv, vg, vb, wsp, bsp_t)


def _gmlp_bwd(uv, dya, vg, vb, wsp, bsp_t, *, name):
    t = uv.shape[0]
    nc = t // CHUNK

    def body(uv_ref, dy_ref, vg_ref, vb_ref, w_ref, b_ref, duv_ref, dw_ref, db_ref, dvg_ref, dvb_ref):
        first = pl.program_id(0) == 0

        @pl.when(first)
        def _():
            dw_ref[...] = jnp.zeros_like(dw_ref)
            db_ref[...] = jnp.zeros_like(db_ref)
            dvg_ref[...] = jnp.zeros_like(dvg_ref)
            dvb_ref[...] = jnp.zeros_like(dvb_ref)

        uvv = uv_ref[...]
        vgv = vg_ref[...]
        u, vhat, rstd, vn = _gmlp_common(uvv, vgv, vb_ref[...])
        dy = dy_ref[...]
        tril = _iota((CHUNK, CHUNK), 0) >= _iota((CHUNK, CHUNK), 1)
        lane = _iota((CHUNK, LANES), 1)
        bt = b_ref[...]
        ds_all = dy * u
        dbacc = jnp.zeros((CHUNK, LANES), f32)
        dvh_parts = []
        for g in range(GMLP_GROUPS):
            sl = slice(g * CHUNK, (g + 1) * CHUNK)
            w = jnp.where(tril, w_ref[g], 0.0)
            vng = vn[:, sl]
            s = _dot(w, vng) + bt[:, g:g + 1]
            ds = ds_all[:, sl]
            duv_ref[:, sl] = (dy[:, sl] * s * _gelu_grad(uvv[:, sl])).astype(bf16)
            dw_ref[g] += jnp.where(tril, _dot(ds, vng, _NT), 0.0)
            dbacc = dbacc + jnp.where(lane == g, jnp.sum(ds, axis=1, keepdims=True), 0.0)
            dvn = _dot(w, ds, _TN)
            vh = vhat[:, sl]
            dvg_ref[:, sl] += jnp.sum(dvn * vh, axis=0, keepdims=True)
            dvb_ref[:, sl] += jnp.sum(dvn, axis=0, keepdims=True)
            dvh_parts.append(dvn * vgv[:, sl])
        db_ref[...] += dbacc
        dvhat = jnp.concatenate(dvh_parts, axis=1)
        m1 = jnp.mean(dvhat, axis=1, keepdims=True)
        m2 = jnp.mean(dvhat * vhat, axis=1, keepdims=True)
        dv = rstd * (dvhat - m1 - vhat * m2)
        duv_ref[:, GMLP_WIDTH:] = (dv * _gelu_grad(uvv[:, GMLP_WIDTH:])).astype(bf16)

    vec = pl.BlockSpec((1, GMLP_WIDTH), lambda c: (0, 0))
    return pl.pallas_call(
        body, name=name, grid=(nc,),
        in_specs=[pl.BlockSpec((CHUNK, 2 * GMLP_WIDTH), lambda c: (c, 0)),
                  pl.BlockSpec((CHUNK, GMLP_WIDTH), lambda c: (c, 0)), vec, vec,
                  pl.BlockSpec((GMLP_GROUPS, CHUNK, CHUNK), lambda c: (0, 0, 0)),
                  pl.BlockSpec((CHUNK, LANES), lambda c: (0, 0))],
        out_specs=[pl.BlockSpec((CHUNK, 2 * GMLP_WIDTH), lambda c: (c, 0)),
                   pl.BlockSpec((GMLP_GROUPS, CHUNK, CHUNK), lambda c: (0, 0, 0)),
                   pl.BlockSpec((CHUNK, LANES), lambda c: (0, 0)), vec, vec],
        out_shape=[jax.ShapeDtypeStruct((t, 2 * GMLP_WIDTH), bf16),
                   jax.ShapeDtypeStruct((GMLP_GROUPS, CHUNK, CHUNK), f32),
                   jax.ShapeDtypeStruct((CHUNK, LANES), f32),
                   jax.ShapeDtypeStruct((1, GMLP_WIDTH), f32), jax.ShapeDtypeStruct((1, GMLP_WIDTH), f32)],
        compiler_params=_params(("arbitrary",)),
    )(uv, dya, vg, vb, wsp, bsp_t)


_CONV_COLS = 512
_XS0, _B0, _C0 = 0, D_INNER, D_INNER + N_GROUPS * D_STATE


def _conv_silu(cur_ref, prev_ref, w_ref, b_ref, has_prev, xc_ref, cv_ref):
    row = _iota((CHUNK, _CONV_COLS), 0)
    for j in range(CONV_DIM // _CONV_COLS):
        sl = slice(j * _CONV_COLS, (j + 1) * _CONV_COLS)
        cur = cur_ref[:, sl]
        prev = jnp.where(has_prev, prev_ref[:, sl], 0.0)
        acc = cur * w_ref[CONV_W - 1:CONV_W, sl] + b_ref[:, sl]
        for s in range(1, CONV_W):
            sh = jnp.where(row >= s, pltpu.roll(cur, s, 0), pltpu.roll(prev, s, 0))
            acc = acc + sh * w_ref[CONV_W - 1 - s:CONV_W - s, sl]
        if cv_ref is not None:
            cv_ref[:, sl] = acc
        xc_ref[:, sl] = acc * _sigmoid(acc)


def _ssd_scalars(dtr, dtb, alog):
    xdt = dtr + dtb
    dtv = jnp.maximum(xdt, 0.0) + jnp.log(1.0 + jnp.exp(-jnp.abs(xdt)))
    a = -jnp.exp(alog)
    ltri = (_iota((CHUNK, CHUNK), 0) >= _iota((CHUNK, CHUNK), 1)).astype(f32)
    cs = _dot32(ltri, dtv * a)
    return xdt, dtv, a, cs, cs.T


def _ssd_head_fwd(cs, cs_t, dtv, dsk, xs_g, cbm, goff, tril, h, r):
    psl = slice(r * HEAD_DIM, (r + 1) * HEAD_DIM)
    csc = cs[:, h:h + 1]
    lmat = jnp.exp(jnp.where(tril, csc - cs_t[h:h + 1, :], -1e30))
    dtc = dtv[:, h:h + 1]
    xs_h = xs_g[:, psl]
    xdt = xs_h * dtc
    mmat = cbm * lmat
    e = jnp.exp(csc)
    yoff = e * goff[:, psl]
    y = _dot(mmat, xdt) + yoff + dsk[:, h:h + 1] * xs_h
    cl = cs[CHUNK - 1:CHUNK, h:h + 1]
    dec = jnp.exp(cl - csc)
    return dict(lmat=lmat, dtc=dtc, xs=xs_h, xdt=xdt, mmat=mmat, e=e, yoff=yoff, y=y, cl=cl, dec=dec)


def _ssd_fwd(xbc, z, dtr, cw, cb, dtb, alog, dsk, gs, *, name):
    t = xbc.shape[0]
    nc = t // CHUNK

    def body(cur_ref, prev_ref, z_ref, dtr_ref, cw_ref, cb_ref, dtb_ref, alog_ref, dsk_ref, gs_ref,
             yb_ref, hp_ref, state_ref, xc_ref):
        c = pl.program_id(0)

        @pl.when(c == 0)
        def _():
            state_ref[...] = jnp.zeros_like(state_ref)

        _conv_silu(cur_ref, prev_ref, cw_ref, cb_ref, c > 0, xc_ref, None)
        _, dtv, _, cs, cs_t = _ssd_scalars(dtr_ref[...], dtb_ref[...], alog_ref[...])
        dsk_v = dsk_ref[...]
        tril = _iota((CHUNK, CHUNK), 0) >= _iota((CHUNK, CHUNK), 1)
        hp_ref[0] = state_ref[...]
        for g in range(N_GROUPS):
            gsl = slice(g * GROUP_W, (g + 1) * GROUP_W)
            xs_g = xc_ref[:, gsl]
            bg = xc_ref[:, _B0 + g * D_STATE:_B0 + (g + 1) * D_STATE]
            cg = xc_ref[:, _C0 + g * D_STATE:_C0 + (g + 1) * D_STATE]
            cbm = _dot(cg, bg, _NT)
            hg = state_ref[gsl, :]
            goff = _dot(cg, hg, _NT)
            ys = []
            for r in range(HEADS_PER_GROUP):
                h = g * HEADS_PER_GROUP + r
                hd = _ssd_head_fwd(cs, cs_t, dtv, dsk_v, xs_g, cbm, goff, tril, h, r)
                ys.append(hd["y"])
                s_new = _dot(hd["xdt"] * hd["dec"], bg, _TN)
                rows = slice(g * GROUP_W + r * HEAD_DIM, g * GROUP_W + (r + 1) * HEAD_DIM)
                state_ref[rows, :] = hg[r * HEAD_DIM:(r + 1) * HEAD_DIM, :] * jnp.exp(hd["cl"]) + s_new
            y_g = jnp.concatenate(ys, axis=1)
            zg = z_ref[:, gsl]
            yg = y_g * zg * _sigmoid(zg)
            rs = lax.rsqrt(jnp.mean(yg * yg, axis=1, keepdims=True) + NORM_EPS)
            yb_ref[:, gsl] = (yg * rs * gs_ref[:, gsl]).astype(bf16)

    def chunk(w):
        return pl.BlockSpec((CHUNK, w), lambda c: (c, 0))

    def const(shape):
        return pl.BlockSpec(shape, lambda c: (0,) * len(shape))

    return pl.pallas_call(
        body, name=name, grid=(nc,),
        in_specs=[chunk(CONV_DIM), pl.BlockSpec((CHUNK, CONV_DIM), lambda c: (jnp.maximum(c - 1, 0), 0)),
                  chunk(D_INNER), chunk(LANES), const((CONV_W, CONV_DIM)), const((1, CONV_DIM)),
                  const((1, LANES)), const((1, LANES)), const((1, LANES)), const((1, D_INNER))],
        out_specs=[chunk(D_INNER), pl.BlockSpec((1, N_HEADS * HEAD_DIM, D_STATE), lambda c: (c, 0, 0))],
        out_shape=[jax.ShapeDtypeStruct((t, D_INNER), bf16),
                   jax.ShapeDtypeStruct((nc, N_HEADS * HEAD_DIM, D_STATE), f32)],
        scratch_shapes=[pltpu.VMEM((N_HEADS * HEAD_DIM, D_STATE), f32), pltpu.VMEM((CHUNK, CONV_DIM), f32)],
        compiler_params=_params(("arbitrary",)),
    )(xbc, xbc, z, dtr, cw, cb, dtb, alog, dsk, gs)


def _ssd_bwd(xbc, z, dtr, hprev, dyb, cw, cb, dtb, alog, dsk, gs, *, name):
    t = xbc.shape[0]
    nc = t // CHUNK

    def body(cur_ref, prev_ref, z_ref, dtr_ref, hp_ref, dyb_ref, cw_ref, cb_ref, dtb_ref, alog_ref, dsk_ref, gs_ref,
             dz_ref, dxbc_ref, ddt_ref, dcw_ref, dcb_ref, ddtb_ref, dalog_ref, ddsk_ref, dgs_ref,
             dh_ref, dcnext_ref, xc_ref, cv_ref, dxc_ref):
        i = pl.program_id(0)
        cc = nc - 1 - i

        @pl.when(i == 0)
        def _():
            for ref in (dh_ref, dcnext_ref, dcw_ref, dcb_ref, ddtb_ref, dalog_ref, ddsk_ref, dgs_ref):
                ref[...] = jnp.zeros_like(ref)

        _conv_silu(cur_ref, prev_ref, cw_ref, cb_ref, cc > 0, xc_ref, cv_ref)
        xdt_pre, dtv, a, cs, cs_t = _ssd_scalars(dtr_ref[...], dtb_ref[...], alog_ref[...])
        dsk_v = dsk_ref[...]
        tril = _iota((CHUNK, CHUNK), 0) >= _iota((CHUNK, CHUNK), 1)
        lane = _iota((CHUNK, LANES), 1)
        rowi = _iota((CHUNK, LANES), 0)
        lane1 = _iota((1, LANES), 1)
        dcs_mat = jnp.zeros((CHUNK, LANES), f32)
        dcs_t_mat = jnp.zeros((LANES, CHUNK), f32)
        ddt_mat = jnp.zeros((CHUNK, LANES), f32)
        ddsk_acc = jnp.zeros((1, LANES), f32)
        for g in range(N_GROUPS):
            gsl = slice(g * GROUP_W, (g + 1) * GROUP_W)
            xs_g = xc_ref[:, gsl]
            bg = xc_ref[:, _B0 + g * D_STATE:_B0 + (g + 1) * D_STATE]
            cg = xc_ref[:, _C0 + g * D_STATE:_C0 + (g + 1) * D_STATE]
            cbm = _dot(cg, bg, _NT)
            hg = hp_ref[0, gsl, :]
            goff = _dot(cg, hg, _NT)
            heads = [_ssd_head_fwd(cs, cs_t, dtv, dsk_v, xs_g, cbm, goff, tril, g * HEADS_PER_GROUP + r, r)
                     for r in range(HEADS_PER_GROUP)]
            y_g = jnp.concatenate([hd["y"] for hd in heads], axis=1)
            zg = z_ref[:, gsl]
            sz = _sigmoid(zg)
            silu = zg * sz
            yg = y_g * silu
            rs = lax.rsqrt(jnp.mean(yg * yg, axis=1, keepdims=True) + NORM_EPS)
            yn = yg * rs
            dyb = dyb_ref[:, gsl]
            dgs_ref[:, gsl] += jnp.sum(dyb * yn, axis=0, keepdims=True)
            dyn = dyb * gs_ref[:, gsl]
            dyg = rs * (dyn - yn * jnp.mean(dyn * yn, axis=1, keepdims=True))
            dy_g = dyg * silu
            dz_ref[:, gsl] = (dyg * y_g * (sz * (1.0 + zg * (1.0 - sz)))).astype(bf16)
            dcb_acc = jnp.zeros((CHUNK, CHUNK), f32)
            dhn_g = dh_ref[gsl, :]
            dg_parts, dxs_parts, xdtdec_parts = [], [], []
            for r in range(HEADS_PER_GROUP):
                hd = heads[r]
                h = g * HEADS_PER_GROUP + r
                psl = slice(r * HEAD_DIM, (r + 1) * HEAD_DIM)
                d_y = dy_g[:, psl]
                ddsk_acc = ddsk_acc + jnp.where(lane1 == h, _sum_all(d_y * hd["xs"]), 0.0)
                dxs = dsk_v[:, h:h + 1] * d_y
                dcs_col = jnp.sum(d_y * hd["yoff"], axis=1, keepdims=True)
                dg_parts.append(hd["e"] * d_y)
                dm = _dot(d_y, hd["xdt"], _NT)
                dxdt = _dot(hd["mmat"], d_y, _TN)
                dml = dm * hd["lmat"]
                dcb_acc = dcb_acc + dml
                dseg = dml * cbm
                dcs_col = dcs_col + jnp.sum(dseg, axis=1, keepdims=True)
                dcs_row = -jnp.sum(dseg, axis=0, keepdims=True)
                dhn = dhn_g[psl, :]
                dk = jnp.exp(hd["cl"])
                dcl = _sum_all(dhn * hg[psl, :]) * dk
                wm = _dot(bg, dhn, _NT)
                dxdt = dxdt + hd["dec"] * wm
                tdec = jnp.sum(hd["xdt"] * wm, axis=1, keepdims=True) * hd["dec"]
                dcl = dcl + _sum_all(tdec)
                dcs_col = dcs_col - tdec
                xdtdec_parts.append(hd["xdt"] * hd["dec"])
                dxs_parts.append(dxs + dxdt * hd["dtc"])
                ddt_col = jnp.sum(dxdt * hd["xs"], axis=1, keepdims=True)
                dcs_mat = dcs_mat + jnp.where(lane == h, dcs_col, 0.0) \
                    + jnp.where((lane == h) & (rowi == CHUNK - 1), dcl, 0.0)
                dcs_t_mat = dcs_t_mat + jnp.where(rowi == h, dcs_row, 0.0)
                ddt_mat = ddt_mat + jnp.where(lane == h, ddt_col, 0.0)
                rows = slice(g * GROUP_W + r * HEAD_DIM, g * GROUP_W + (r + 1) * HEAD_DIM)
                dh_ref[rows, :] = dhn * dk
            dg_g = jnp.concatenate(dg_parts, axis=1)
            d_c = _dot(dg_g, hg) + _dot(dcb_acc, bg)
            d_b = _dot(dcb_acc, cg, _TN) + _dot(jnp.concatenate(xdtdec_parts, axis=1), dhn_g)
            dh_ref[gsl, :] += _dot(dg_g, cg, _TN)
            dxc_ref[:, gsl] = jnp.concatenate(dxs_parts, axis=1)
            dxc_ref[:, _B0 + g * D_STATE:_B0 + (g + 1) * D_STATE] = d_b
            dxc_ref[:, _C0 + g * D_STATE:_C0 + (g + 1) * D_STATE] = d_c
        utri = (_iota((CHUNK, CHUNK), 0) <= _iota((CHUNK, CHUNK), 1)).astype(f32)
        dda = _dot32(utri, dcs_mat + dcs_t_mat.T)
        ddt_total = ddt_mat + dda * a
        dalog_ref[...] += jnp.sum(dda * dtv, axis=0, keepdims=True) * a
        ddtr = jnp.where(lane < N_HEADS, ddt_total * _sigmoid(xdt_pre), 0.0)
        ddtb_ref[...] += jnp.sum(ddtr, axis=0, keepdims=True)
        ddt_ref[...] = ddtr.astype(bf16)
        ddsk_ref[...] += ddsk_acc
        row = _iota((CHUNK, _CONV_COLS), 0)
        has_prev = cc > 0
        for j in range(CONV_DIM // _CONV_COLS):
            sl = slice(j * _CONV_COLS, (j + 1) * _CONV_COLS)
            cvv = cv_ref[:, sl]
            sg = _sigmoid(cvv)
            dconv = dxc_ref[:, sl] * (sg * (1.0 + cvv * (1.0 - sg)))
            nxt = dcnext_ref[:, sl]
            cur = cur_ref[:, sl]
            prev = jnp.where(has_prev, prev_ref[:, sl], 0.0)
            dxin = dconv * cw_ref[CONV_W - 1:CONV_W, sl]
            dcw_ref[CONV_W - 1:CONV_W, sl] += jnp.sum(dconv * cur, axis=0, keepdims=True)
            for s in range(1, CONV_W):
                up = jnp.where(row < CHUNK - s, pltpu.roll(dconv, CHUNK - s, 0), pltpu.roll(nxt, CHUNK - s, 0))
                dxin = dxin + up * cw_ref[CONV_W - 1 - s:CONV_W - s, sl]
                sh = jnp.where(row >= s, pltpu.roll(cur, s, 0), pltpu.roll(prev, s, 0))
                dcw_ref[CONV_W - 1 - s:CONV_W - s, sl] += jnp.sum(dconv * sh, axis=0, keepdims=True)
            dcb_ref[:, sl] += jnp.sum(dconv, axis=0, keepdims=True)
            dxbc_ref[:, sl] = dxin.astype(bf16)
            dcnext_ref[:, sl] = dconv

    def chunk(w):
        return pl.BlockSpec((CHUNK, w), lambda i: (nc - 1 - i, 0))

    def const(shape):
        return pl.BlockSpec(shape, lambda i: (0,) * len(shape))

    hp_rows = N_HEADS * HEAD_DIM
    return pl.pallas_call(
        body, name=name, grid=(nc,),
        in_specs=[chunk(CONV_DIM), pl.BlockSpec((CHUNK, CONV_DIM), lambda i: (jnp.maximum(nc - 2 - i, 0), 0)),
                  chunk(D_INNER), chunk(LANES), pl.BlockSpec((1, hp_rows, D_STATE), lambda i: (nc - 1 - i, 0, 0)),
                  chunk(D_INNER), const((CONV_W, CONV_DIM)), const((1, CONV_DIM)),
                  const((1, LANES)), const((1, LANES)), const((1, LANES)), const((1, D_INNER))],
        out_specs=[chunk(D_INNER), chunk(CONV_DIM), chunk(LANES), const((CONV_W, CONV_DIM)), const((1, CONV_DIM)),
                   const((1, LANES)), const((1, LANES)), const((1, LANES)), const((1, D_INNER))],
        out_shape=[jax.ShapeDtypeStruct((t, D_INNER), bf16), jax.ShapeDtypeStruct((t, CONV_DIM), bf16),
                   jax.ShapeDtypeStruct((t, LANES), bf16), jax.ShapeDtypeStruct((CONV_W, CONV_DIM), f32),
                   jax.ShapeDtypeStruct((1, CONV_DIM), f32), jax.ShapeDtypeStruct((1, LANES), f32),
                   jax.ShapeDtypeStruct((1, LANES), f32), jax.ShapeDtypeStruct((1, LANES), f32),
                   jax.ShapeDtypeStruct((1, D_INNER), f32)],
        scratch_shapes=[pltpu.VMEM((hp_rows, D_STATE), f32), pltpu.VMEM((CHUNK, CONV_DIM), f32),
                        pltpu.VMEM((CHUNK, CONV_DIM), f32), pltpu.VMEM((CHUNK, CONV_DIM), f32),
                        pltpu.VMEM((CHUNK, CONV_DIM), f32)],
        compiler_params=_params(("arbitrary",)),
    )(xbc, xbc, z, dtr, hprev, dyb, cw, cb, dtb, alog, dsk, gs)


def _adamw(w, g, m, v, *, name):
    r, c = w.shape
    tr = r
    while tr * c * 4 > _MB and tr % 16 == 0:
        tr //= 2

    def body(w_ref, g_ref, m_ref, v_ref, d_ref, m2_ref, v2_ref):
        gv = g_ref[...]
        m2 = ADAM_B1 * m_ref[...] + (1.0 - ADAM_B1) * gv
        v2 = ADAM_B2 * v_ref[...] + (1.0 - ADAM_B2) * (gv * gv)
        m_hat = m2 / (1.0 - ADAM_B1 ** ADAM_STEP)
        v_hat = v2 / (1.0 - ADAM_B2 ** ADAM_STEP)
        d_ref[...] = -ADAM_LR * (m_hat / (jnp.sqrt(v_hat) + ADAM_EPS) + ADAM_WD * w_ref[...])
        m2_ref[...] = m2
        v2_ref[...] = v2

    blk = pl.BlockSpec((tr, c), lambda i: (i, 0))
    return pl.pallas_call(
        body, name=name, grid=(r // tr,),
        in_specs=[blk] * 4, out_specs=[blk] * 3,
        out_shape=[jax.ShapeDtypeStruct((r, c), f32)] * 3,
        compiler_params=_params(("parallel",)),
    )(w, g, m, v)


def _cast_bf16(a, *, name):
    r, c = a.shape
    tr = 2256 if r % 2256 == 0 else r

    def body(a_ref, o_ref):
        o_ref[...] = a_ref[...].astype(bf16)

    blk = pl.BlockSpec((tr, c), lambda i: (i, 0))
    return pl.pallas_call(
        body, name=name, grid=(r // tr,), in_specs=[blk], out_specs=blk,
        out_shape=jax.ShapeDtypeStruct((r, c), bf16), compiler_params=_params(("parallel",)),
    )(a)


_ANY = pl.BlockSpec(memory_space=pl.ANY)


def _place():
    x, y, c = lax.axis_index("x"), lax.axis_index("y"), lax.axis_index("c")
    other_chips = [(1 - x, y), (x, 1 - y), (1 - x, 1 - y)]
    return x, y, c, other_chips


def _gather_shards(shard, *, name):
    _, rh, lanes = shard.shape

    def body(in_ref, out_ref, send_sems, recv_sems):
        x, y, c, chips = _place()
        me = 2 * x + y
        sibling = (x, y, 1 - c)

        def cp(k, chip, half, to, src=None):
            dst = out_ref.at[chip, half]
            return pltpu.make_async_remote_copy(
                src_ref=dst if src is None else src, dst_ref=dst, send_sem=send_sems.at[k], recv_sem=recv_sems.at[k],
                device_id=to, device_id_type=MESH)

        first = [cp(j, me, c, (cx, cy, c), src=in_ref.at[c]) for j, (cx, cy) in enumerate(chips)]
        for f in first:
            f.start()
        passed = []
        for j, (cx, cy) in enumerate(chips):
            cp(j, 2 * cx + cy, c, sibling).wait_recv()
            p = cp(3 + j, 2 * cx + cy, c, sibling)
            p.start()
            passed.append(p)
        for j, (cx, cy) in enumerate(chips):
            cp(3 + j, 2 * cx + cy, 1 - c, sibling).wait_recv()
        for f in first + passed:
            f.wait_send()

    return pl.pallas_call(
        body, name=name, in_specs=[_ANY], out_specs=_ANY,
        out_shape=jax.ShapeDtypeStruct((N_CHIPS, 2, rh, lanes), shard.dtype),
        scratch_shapes=[pltpu.SemaphoreType.DMA((6,)), pltpu.SemaphoreType.DMA((6,))],
    )(shard)


def _rs_swap_halves(g, *, name):
    nch, _, rh, lanes = g.shape

    def body(g_ref, out_ref, send_sems, recv_sems):
        x, y, c, _ = _place()
        copies = [pltpu.make_async_remote_copy(
            src_ref=g_ref.at[k, 1 - c], dst_ref=out_ref.at[k], send_sem=send_sems.at[k], recv_sem=recv_sems.at[k],
            device_id=(x, y, 1 - c), device_id_type=MESH) for k in range(nch)]
        for cpy in copies:
            cpy.start()
        for cpy in copies:
            cpy.wait()

    return pl.pallas_call(
        body, name=name, in_specs=[_ANY], out_specs=_ANY,
        out_shape=jax.ShapeDtypeStruct((nch, rh, lanes), g.dtype),
        scratch_shapes=[pltpu.SemaphoreType.DMA((nch,)), pltpu.SemaphoreType.DMA((nch,))],
    )(g)


def _rs_add_pair(g, got, c_idx, *, name):
    nch, _, rh, lanes = g.shape
    tr = rh // 10 if rh % 160 == 0 else rh

    def body(c_ref, g_ref, got_ref, p32_ref, p16_ref):
        s = g_ref[...] + got_ref[...]
        p32_ref[...] = s
        p16_ref[...] = s.astype(bf16)

    blk = pl.BlockSpec((None, tr, lanes), lambda k, i, c_ref: (k, i, 0))
    return pl.pallas_call(
        body, name=name,
        grid_spec=pltpu.PrefetchScalarGridSpec(
            num_scalar_prefetch=1, grid=(nch, rh // tr),
            in_specs=[pl.BlockSpec((None, None, tr, lanes), lambda k, i, c_ref: (k, c_ref[0], i, 0)), blk],
            out_specs=[blk, blk]),
        out_shape=[jax.ShapeDtypeStruct((nch, rh, lanes), f32), jax.ShapeDtypeStruct((nch, rh, lanes), bf16)],
        compiler_params=_params(("parallel", "parallel")),
    )(c_idx, g, got)


def _rs_scatter_chips(p16, *, name):
    _, rh, lanes = p16.shape

    def body(p_ref, out_ref, send_sems, recv_sems):
        x, y, c, chips = _place()
        copies = [pltpu.make_async_remote_copy(
            src_ref=p_ref.at[2 * cx + cy], dst_ref=out_ref.at[j], send_sem=send_sems.at[j], recv_sem=recv_sems.at[j],
            device_id=(cx, cy, c), device_id_type=MESH) for j, (cx, cy) in enumerate(chips)]
        for cpy in copies:
            cpy.start()
        for cpy in copies:
            cpy.wait()

    return pl.pallas_call(
        body, name=name, in_specs=[_ANY], out_specs=_ANY,
        out_shape=jax.ShapeDtypeStruct((3, rh, lanes), p16.dtype),
        scratch_shapes=[pltpu.SemaphoreType.DMA((3,)), pltpu.SemaphoreType.DMA((3,))],
    )(p16)


def _rs_add_chips(p32, got, me_idx, *, name):
    _, rh, lanes = p32.shape
    tr = rh // 10 if rh % 160 == 0 else rh

    def body(me_ref, p_ref, got_ref, o_ref):
        o_ref[...] = ((p_ref[...] + got_ref[0].astype(f32)) + got_ref[1].astype(f32)) + got_ref[2].astype(f32)

    return pl.pallas_call(
        body, name=name,
        grid_spec=pltpu.PrefetchScalarGridSpec(
            num_scalar_prefetch=1, grid=(rh // tr,),
            in_specs=[pl.BlockSpec((None, tr, lanes), lambda i, me_ref: (me_ref[0], i, 0)),
                      pl.BlockSpec((3, tr, lanes), lambda i, me_ref: (0, i, 0))],
            out_specs=pl.BlockSpec((tr, lanes), lambda i, me_ref: (i, 0))),
        out_shape=jax.ShapeDtypeStruct((rh, lanes), f32),
        compiler_params=_params(("parallel",)),
    )(me_idx, p32, got)


def _rs_join_halves(half, *, name):
    rh, lanes = half.shape

    def body(h_ref, out_ref, send_sem, recv_sem):
        x, y, c, _ = _place()
        cpy = pltpu.make_async_remote_copy(
            src_ref=h_ref, dst_ref=out_ref, send_sem=send_sem, recv_sem=recv_sem,
            device_id=(x, y, 1 - c), device_id_type=MESH)
        cpy.start()
        cpy.wait()

    return pl.pallas_call(
        body, name=name, in_specs=[_ANY], out_specs=_ANY,
        out_shape=jax.ShapeDtypeStruct((rh, lanes), half.dtype),
        scratch_shapes=[pltpu.SemaphoreType.DMA, pltpu.SemaphoreType.DMA],
    )(half)


def _all_reduce_small(s, *, name):
    rs, lanes = s.shape

    def body(s_ref, o_ref, buf_ref, send_sems, recv_sems):
        x, y, c, _ = _place()
        me = 4 * x + 2 * y + c
        peers = []
        for k in range(1, N_DEV):
            px = 1 - x if (k >> 2) & 1 else x
            py = 1 - y if (k >> 1) & 1 else y
            pc = 1 - c if k & 1 else c
            peers.append((px, py, pc))
        copies = [pltpu.make_async_remote_copy(
            src_ref=s_ref, dst_ref=buf_ref.at[me], send_sem=send_sems.at[k], recv_sem=recv_sems.at[k],
            device_id=peer, device_id_type=MESH) for k, peer in enumerate(peers)]
        for cpy in copies:
            cpy.start()
        buf_ref[me] = s_ref[...]
        for k, (px, py, pc) in enumerate(peers):
            pltpu.make_async_remote_copy(
                src_ref=s_ref, dst_ref=buf_ref.at[4 * px + 2 * py + pc], send_sem=send_sems.at[k],
                recv_sem=recv_sems.at[k], device_id=(px, py, pc), device_id_type=MESH).wait_recv()
        for cpy in copies:
            cpy.wait_send()
        acc = buf_ref[0]
        for d in range(1, N_DEV):
            acc = acc + buf_ref[d]
        o_ref[...] = acc

    vm = pl.BlockSpec(memory_space=pltpu.VMEM)
    return pl.pallas_call(
        body, name=name, in_specs=[vm], out_specs=vm,
        out_shape=jax.ShapeDtypeStruct((rs, lanes), f32),
        scratch_shapes=[pltpu.VMEM((N_DEV, rs, lanes), f32), pltpu.SemaphoreType.DMA((N_DEV - 1,)),
                        pltpu.SemaphoreType.DMA((N_DEV - 1,))],
        compiler_params=pltpu.CompilerParams(vmem_limit_bytes=32 * _MB),
    )(s)


def _pad_lanes(a, width=LANES):
    return jnp.pad(a, ((0, 0), (0, width - a.shape[1])))


def _local_grads(x, tgt, wts, small):
    t = x.shape[0]
    tm = min(t, 1024)
    d = D_MODEL
    mm = functools.partial(_matmul, tm=tm)

    dtb = _pad_lanes(small["dt_bias"])
    alog = _pad_lanes(small["a_log"])
    dsk = _pad_lanes(small["d_skip"])
    bsp_t = _pad_lanes(small["b_spatial"].T)
    wsp = small["w_spatial"]

    h = _rms_fwd(x, small["norm_mix_g"], name="rms_mix")
    uv = mm(h, wts["uv"], tn=1024, tk=d, out_dtypes=[f32], name="proj_uv")
    z = mm(h, wts["z"], tn=1024, tk=d, out_dtypes=[f32], name="proj_z")
    xbc = mm(h, wts["xbc"], tn=1024, tk=d, out_dtypes=[f32], name="proj_xbc")
    dtr = mm(h, wts["dt"], tn=LANES, tk=d, out_dtypes=[f32], name="proj_dt")
    gl = mm(h, wts["gate"], tn=1024, tk=d, out_dtypes=[f32], name="proj_gate")
    ya = _gmlp_fwd(uv, small["v_norm_g"], small["v_norm_b"], wsp, bsp_t, name="gmlp_fwd")
    yb, hprev = _ssd_fwd(xbc, z, dtr, small["conv_w"], small["conv_b"], dtb, alog, dsk, small["ssm_norm_g"],
                         name="ssd_fwd")
    pa = mm(ya, wts["pa"], tn=1024, tk=1024, out_dtypes=[f32], name="proj_a")
    pb = mm(yb, wts["pb"], tn=1024, tk=1024, out_dtypes=[f32], name="proj_b")
    merged = _merge_fwd(pa, pb, gl, small["b_gates"], name="merge_fwd")
    x1 = mm(merged, wts["out"], tn=1024, tk=1024, out_dtypes=[f32], extras=[x],
            epilogue=lambda acc, res: (res + acc,), name="out_proj")
    h2 = _rms_fwd(x1, small["norm_mlp_g"], name="rms_mlp")
    up, act = mm(h2, wts["up"], tn=1024, tk=d, out_dtypes=[f32, bf16],
                 epilogue=lambda acc: (acc, jnp.square(jnp.maximum(acc, 0.0))), name="mlp_up")
    x2 = mm(act, wts["down"], tn=1024, tk=1024, out_dtypes=[f32], extras=[x1],
            epilogue=lambda acc, res: (res + acc,), name="mlp_down")

    dx2, dx2b, dgf, loss = _loss_head(x2, tgt, small["norm_final_g"], name="loss_head")
    tt = min(t, 1024)
    tn_mm = functools.partial(_matmul_tn, tt=tt)
    dw = {}
    dw["down"] = tn_mm(act, dx2b, tka=1024, tn=1024, name="dw_down")
    dup = mm(dx2b, wts["down"], nt=True, tn=1024, tk=1024, out_dtypes=[bf16], extras=[up],
             epilogue=lambda acc, u: (acc * (2.0 * jnp.maximum(u, 0.0)),), name="d_act")
    dw["up"] = tn_mm(h2, dup, tka=1024, tn=1024, name="dw_up")
    dh2 = mm(dup, wts["up"], nt=True, tn=1024, tk=1024, out_dtypes=[f32], name="d_h2")
    dx1, dx1b, dg_mlp = _rms_bwd(x1, small["norm_mlp_g"], dh2, dx2, want_bf16=True, name="rms_mlp_bwd")
    dw["out"] = tn_mm(merged, dx1b, tka=1024, tn=1024, name="dw_out")
    dmerged = mm(dx1b, wts["out"], nt=True, tn=1024, tk=1024, out_dtypes=[f32], name="d_merged")
    dpa, dpb, dgl, dbg = _merge_bwd(dmerged, pa, pb, gl, small["b_gates"], name="merge_bwd")
    dw["pa"] = tn_mm(ya, dpa, tka=1024, tn=1024, name="dw_pa")
    dw["pb"] = tn_mm(yb, dpb, tka=1024, tn=1024, name="dw_pb")
    dya = mm(dpa, wts["pa"], nt=True, tn=1024, tk=1024, out_dtypes=[f32], name="d_ya")
    dyb = mm(dpb, wts["pb"], nt=True, tn=1024, tk=1024, out_dtypes=[f32], name="d_yb")
    duv, dwsp, dbsp_t, dvg, dvb = _gmlp_bwd(uv, dya, small["v_norm_g"], small["v_norm_b"], wsp, bsp_t,
                                            name="gmlp_bwd")
    dz, dxbc, ddt, dcw, dcb, ddtb, dalog, ddsk, dgs = _ssd_bwd(
        xbc, z, dtr, hprev, dyb, small["conv_w"], small["conv_b"], dtb, alog, dsk, small["ssm_norm_g"],
        name="ssd_bwd")
    dw["uv"] = tn_mm(h, duv, tka=1024, tn=1024, name="dw_uv")
    dw["z"] = tn_mm(h, dz, tka=1024, tn=1024, name="dw_z")
    dw["xbc"] = tn_mm(h, dxbc, tka=1024, tn=1024, name="dw_xbc")
    dw["dt"] = tn_mm(h, ddt, tka=1024, tn=LANES, name="dw_dt")
    dw["gate"] = tn_mm(h, dgl, tka=1024, tn=1024, name="dw_gate")
    dh = mm(ddt, wts["dt"], nt=True, tn=1024, tk=LANES, out_dtypes=[f32], name="d_h_dt")
    dh = _matmul_nt_sum([(duv, wts["uv"]), (dz, wts["z"]), (dxbc, wts["xbc"]), (dgl, wts["gate"])], dh,
                        tm=tm, tk=512, name="d_h")
    dx, dg_mix = _rms_bwd(x, small["norm_mix_g"], dh, dx1, want_bf16=False, name="rms_mix_bwd")

    dsmall = {
        "norm_mix_g": dg_mix, "conv_w": dcw, "conv_b": dcb, "dt_bias": ddtb[:, :N_HEADS], "a_log": dalog[:, :N_HEADS],
        "d_skip": ddsk[:, :N_HEADS], "ssm_norm_g": dgs, "v_norm_g": dvg, "v_norm_b": dvb, "w_spatial": dwsp,
        "b_spatial": dbsp_t[:, :GMLP_GROUPS].T, "b_gates": dbg, "norm_mlp_g": dg_mlp, "norm_final_g": dgf,
    }
    return loss, dx, dw, dsmall


_IN_SHARD = IN_PROJ // N_CHIPS
_DENSE = ("w_in", "w_proj_a", "w_proj_b", "w_out", "w_mlp_up", "w_mlp_down")
_DENSE_SHARD_SHAPES = {"w_in": (D_MODEL, _IN_SHARD), "w_proj_a": (GMLP_WIDTH // N_CHIPS, D_MODEL),
                       "w_proj_b": (D_INNER // N_CHIPS, D_MODEL), "w_out": (D_MODEL // N_CHIPS, D_MODEL),
                       "w_mlp_up": (D_MODEL, D_FF // N_CHIPS), "w_mlp_down": (D_FF // N_CHIPS, D_MODEL)}
_DENSE_ROWS = {k: s[0] * s[1] // LANES for k, s in _DENSE_SHARD_SHAPES.items()}
_DENSE_TOTAL = sum(_DENSE_ROWS.values())
_CONV_ROWS = CONV_W * (CONV_DIM // N_CHIPS) * 2 // LANES


def _dense_offsets():
    off, out = 0, {}
    for k in _DENSE:
        out[k] = off
        off += _DENSE_ROWS[k]
    return out


_DENSE_OFF = _dense_offsets()

_SMALL = ("norm_mix_g", "conv_w", "conv_b", "dt_bias", "a_log", "d_skip", "ssm_norm_g", "v_norm_g", "v_norm_b",
          "w_spatial", "b_spatial", "b_gates", "norm_mlp_g", "norm_final_g")


def _pack_small(parts):
    flat = jnp.concatenate([parts[k].reshape(-1) for k in _SMALL])
    rows = -(-flat.shape[0] // (8 * LANES)) * 8
    return jnp.pad(flat, (0, rows * LANES - flat.shape[0])).reshape(rows, LANES)


def _unpack_small(packed, shapes):
    flat = packed.reshape(-1)
    out, off = {}, 0
    for k in _SMALL:
        n = math.prod(shapes[k])
        out[k] = flat[off:off + n].reshape(shapes[k])
        off += n
    return out


def _from_chip_columns(stacked, rows, cols):
    return stacked.reshape(N_CHIPS, rows, cols).transpose(1, 0, 2).reshape(rows, N_CHIPS * cols)


def _to_chip_columns(full, cols):
    rows = full.shape[0]
    return full.reshape(rows, N_CHIPS, cols).transpose(1, 0, 2).reshape(N_CHIPS, rows * cols // LANES, LANES)


def kernel(x, norm_mix_g, w_in, conv_w, conv_b, dt_bias, a_log, d_skip, ssm_norm_g, v_norm_g, v_norm_b, w_spatial, b_spatial, b_gates, w_proj_a, w_proj_b, w_out, norm_mlp_g, w_mlp_up, w_mlp_down, norm_final_g, loss_target, m_norm_mix_g, m_w_in, m_conv_w, m_conv_b, m_dt_bias, m_a_log, m_d_skip, m_ssm_norm_g, m_v_norm_g, m_v_norm_b, m_w_spatial, m_b_spatial, m_b_gates, m_w_proj_a, m_w_proj_b, m_w_out, m_norm_mlp_g, m_w_mlp_up, m_w_mlp_down, m_norm_final_g, v_norm_mix_g, v_w_in, v_conv_w, v_conv_b, v_dt_bias, v_a_log, v_d_skip, v_ssm_norm_g, v_v_norm_g, v_v_norm_b, v_w_spatial, v_b_spatial, v_b_gates, v_w_proj_a, v_w_proj_b, v_w_out, v_norm_mlp_g, v_w_mlp_up, v_w_mlp_down, v_norm_final_g):
    given = dict(locals())
    names = ("norm_mix_g", "w_in", "conv_w", "conv_b", "dt_bias", "a_log", "d_skip", "ssm_norm_g", "v_norm_g",
             "v_norm_b", "w_spatial", "b_spatial", "b_gates", "w_proj_a", "w_proj_b", "w_out", "norm_mlp_g",
             "w_mlp_up", "w_mlp_down", "norm_final_g")
    xi, yi, ci = lax.axis_index("x"), lax.axis_index("y"), lax.axis_index("c")
    me_chip = (2 * xi + yi).astype(jnp.int32)

    dense_f32 = jnp.concatenate([given[k][0].reshape(-1, LANES) for k in _DENSE])
    dense_b16 = _cast_bf16(dense_f32, name="cast_weights")
    conv_shard = conv_w.reshape(CONV_W, CONV_DIM // N_CHIPS)
    conv_bits = lax.bitcast_convert_type(conv_shard.reshape(-1, LANES), bf16).reshape(_CONV_ROWS, LANES)
    shard = jnp.concatenate([dense_b16, conv_bits]).reshape(2, (_DENSE_TOTAL + _CONV_ROWS) // 2, LANES)
    gathered = lax.dynamic_update_slice(_gather_shards(shard, name="gather_weights"), shard[None], (me_chip, 0, 0, 0))
    gathered = gathered.reshape(N_CHIPS, _DENSE_TOTAL + _CONV_ROWS, LANES)

    def rows_of(k):
        return gathered[:, _DENSE_OFF[k]:_DENSE_OFF[k] + _DENSE_ROWS[k]]

    w_in_full = _from_chip_columns(rows_of("w_in"), D_MODEL, _IN_SHARD)
    o_dt, o_gate = 2 * GMLP_WIDTH + D_INNER + CONV_DIM, 2 * GMLP_WIDTH + D_INNER + CONV_DIM + N_HEADS
    wts = {
        "uv": w_in_full[:, :2 * GMLP_WIDTH], "z": w_in_full[:, 2 * GMLP_WIDTH:2 * GMLP_WIDTH + D_INNER],
        "xbc": w_in_full[:, 2 * GMLP_WIDTH + D_INNER:o_dt], "dt": _pad_lanes(w_in_full[:, o_dt:o_gate]),
        "gate": w_in_full[:, o_gate:],
        "pa": rows_of("w_proj_a").reshape(GMLP_WIDTH, D_MODEL), "pb": rows_of("w_proj_b").reshape(D_INNER, D_MODEL),
        "out": rows_of("w_out").reshape(D_MODEL, D_MODEL),
        "up": _from_chip_columns(rows_of("w_mlp_up"), D_MODEL, D_FF // N_CHIPS),
        "down": rows_of("w_mlp_down").reshape(D_FF, D_MODEL),
    }
    conv_all = lax.bitcast_convert_type(
        gathered[:, _DENSE_TOTAL:].reshape(N_CHIPS, _CONV_ROWS // 2, LANES, 2), f32)
    conv_full = conv_all.reshape(N_CHIPS, CONV_W, CONV_DIM // N_CHIPS).transpose(1, 0, 2).reshape(CONV_W, CONV_DIM)

    small = {
        "norm_mix_g": norm_mix_g, "conv_w": conv_full, "conv_b": conv_b, "dt_bias": dt_bias, "a_log": a_log,
        "d_skip": d_skip, "ssm_norm_g": ssm_norm_g, "v_norm_g": v_norm_g, "v_norm_b": v_norm_b,
        "w_spatial": w_spatial[0], "b_spatial": b_spatial[0], "b_gates": b_gates, "norm_mlp_g": norm_mlp_g,
        "norm_final_g": norm_final_g.reshape(1, D_MODEL),
    }

    loss_part, grad_x, dw, dsmall = _local_grads(x[0], loss_target[0], wts, small)
    loss = lax.psum(loss_part[0, 0], ("x", "y", "c"))

    dw_in = jnp.concatenate([dw["uv"], dw["z"], dw["xbc"], dw["dt"][:, :N_HEADS], dw["gate"]], axis=1)
    per_chip = {
        "w_in": _to_chip_columns(dw_in, _IN_SHARD), "w_proj_a": dw["pa"].reshape(N_CHIPS, -1, LANES),
        "w_proj_b": dw["pb"].reshape(N_CHIPS, -1, LANES), "w_out": dw["out"].reshape(N_CHIPS, -1, LANES),
        "w_mlp_up": _to_chip_columns(dw["up"], D_FF // N_CHIPS), "w_mlp_down": dw["down"].reshape(N_CHIPS, -1, LANES),
    }
    g_all = jnp.concatenate([per_chip[k] for k in _DENSE], axis=1).reshape(N_CHIPS, 2, _DENSE_TOTAL // 2, LANES)
    c_idx = ci.astype(jnp.int32).reshape(1)
    got_pair = _rs_swap_halves(g_all, name="rs_swap_halves")
    p32, p16 = _rs_add_pair(g_all, got_pair, c_idx, name="rs_add_pair")
    got_chips = _rs_scatter_chips(p16, name="rs_scatter_chips")
    half = _rs_add_chips(p32, got_chips, me_chip.reshape(1), name="rs_add_chips")
    other_half = _rs_join_halves(half, name="rs_join_halves")
    g_shard = jnp.where(ci == 0, jnp.concatenate([half, other_half]), jnp.concatenate([other_half, half]))

    small_shapes = {k: dsmall[k].shape for k in _SMALL}
    red = _unpack_small(_all_reduce_small(_pack_small(dsmall), name="all_reduce_small"), small_shapes)
    conv_cols = CONV_DIM // N_CHIPS
    red["conv_w"] = lax.dynamic_slice_in_dim(red["conv_w"], me_chip * conv_cols, conv_cols, axis=1)

    grads, deltas, new_m, new_v = {}, {}, {}, {}
    for k in _DENSE:
        shp = _DENSE_SHARD_SHAPES[k]
        g2 = g_shard[_DENSE_OFF[k]:_DENSE_OFF[k] + _DENSE_ROWS[k]].reshape(shp)
        dlt, m2, v2 = _adamw(given[k][0], g2, given["m_" + k][0], given["v_" + k][0], name="adamw_" + k)
        grads[k], deltas[k], new_m[k], new_v[k] = g2, dlt, m2, v2
    adam_shapes = dict(small_shapes)
    adam_shapes["conv_w"] = (CONV_W, conv_cols)

    def small_pack_of(prefix):
        return _pack_small({k: given[prefix + k].reshape(adam_shapes[k]) for k in _SMALL})

    dlt_s, m_s, v_s = _adamw(small_pack_of(""), _pack_small(red), small_pack_of("m_"), small_pack_of("v_"),
                             name="adamw_small")
    for dst, packed in ((deltas, dlt_s), (new_m, m_s), (new_v, v_s)):
        dst.update(_unpack_small(packed, adam_shapes))
    grads.update(red)

    def shaped(dct):
        return [dct[k].reshape(given[k].shape) for k in names]

    return (loss, grad_x[None], *shaped(grads), *shaped(deltas), *shaped(new_m), *shaped(new_v))
```

```python
import functools
import math

import jax
import jax.numpy as jnp
from jax import lax
from jax.experimental import pallas as pl
from jax.experimental.pallas import tpu as pltpu

f32 = jnp.float32
bf16 = jnp.bfloat16

D_MODEL = 1024
CHUNK = 128
GMLP_WIDTH = 1024
GMLP_GROUPS = 8
D_INNER = 2048
HEAD_DIM = 64
N_HEADS = 32
N_GROUPS = 8
HEADS_PER_GROUP = 4
GROUP_W = HEADS_PER_GROUP * HEAD_DIM
D_STATE = 128
CONV_W = 4
CONV_DIM = 4096
D_FF = 4096
IN_PROJ = 10272
NORM_EPS = 1e-6
N_CHIPS = 4
N_DEV = 8
LANES = 128

ADAM_LR = 0.001
ADAM_B1 = 0.9
ADAM_B2 = 0.999
ADAM_EPS = 1e-08
ADAM_WD = 0.01
ADAM_STEP = 10

MESH = pl.DeviceIdType.MESH
_NT = (((1,), (1,)), ((), ()))
_NN = (((1,), (0,)), ((), ()))
_TN = (((0,), (0,)), ((), ()))
_MB = 2 ** 20


def _params(sem, vmem_mb=48):
    return pltpu.CompilerParams(dimension_semantics=sem, vmem_limit_bytes=vmem_mb * _MB)


def _dot(a, b, dims=_NN):
    return lax.dot_general(a.astype(bf16), b.astype(bf16), dims, preferred_element_type=f32)


def _dot32(a, b):
    return jnp.dot(a, b, preferred_element_type=f32, precision=lax.Precision.HIGHEST)


def _sigmoid(x):
    return 1.0 / (1.0 + jnp.exp(-x))


def _sum_all(a):
    return jnp.sum(jnp.sum(a, axis=1, keepdims=True), axis=0, keepdims=True)


def _iota(shape, dim):
    return lax.broadcasted_iota(jnp.int32, shape, dim)


def _matmul(a, b, *, nt=False, tm, tn, tk, out_dtypes, epilogue=None, extras=(), name):
    m, k_dim = a.shape
    n = b.shape[0] if nt else b.shape[1]
    nk = k_dim // tk
    ne, no = len(extras), len(out_dtypes)
    dims = _NT if nt else _NN

    def body(*refs):
        a_ref, b_ref = refs[0], refs[1]
        ex = refs[2:2 + ne]
        outs = refs[2 + ne:2 + ne + no]

        def finish(acc):
            vals = epilogue(acc, *[e[...] for e in ex]) if epilogue is not None else (acc,)
            for o, v in zip(outs, vals):
                o[...] = v.astype(o.dtype)

        part = lax.dot_general(a_ref[...], b_ref[...], dims, preferred_element_type=f32)
        if nk == 1:
            finish(part)
        else:
            acc_ref = refs[-1]
            kk = pl.program_id(2)

            @pl.when(kk == 0)
            def _():
                acc_ref[...] = part

            @pl.when(kk > 0)
            def _():
                acc_ref[...] += part

            @pl.when(kk == nk - 1)
            def _():
                finish(acc_ref[...])

    b_spec = pl.BlockSpec((tn, tk), lambda i, j, k: (j, k)) if nt else pl.BlockSpec((tk, tn), lambda i, j, k: (k, j))
    tile = pl.BlockSpec((tm, tn), lambda i, j, k: (i, j))
    outs = pl.pallas_call(
        body, name=name, grid=(m // tm, n // tn, nk),
        in_specs=[pl.BlockSpec((tm, tk), lambda i, j, k: (i, k)), b_spec] + [tile] * ne,
        out_specs=[tile] * no,
        out_shape=[jax.ShapeDtypeStruct((m, n), dt) for dt in out_dtypes],
        scratch_shapes=[pltpu.VMEM((tm, tn), f32)] if nk > 1 else [],
        compiler_params=_params(("parallel", "parallel", "arbitrary")),
    )(a, b, *extras)
    return outs if no > 1 else outs[0]


def _matmul_nt_sum(pairs, extra, *, tm, tk, name):
    m = pairs[0][0].shape[0]
    n = pairs[0][1].shape[0]
    nblk = [a.shape[1] // tk for a, _ in pairs]
    starts = [sum(nblk[:p]) for p in range(len(pairs))]
    nk = sum(nblk)
    npairs = len(pairs)

    def body(*refs):
        extra_ref, o_ref, acc_ref = refs[2 * npairs], refs[2 * npairs + 1], refs[2 * npairs + 2]
        kk = pl.program_id(1)

        @pl.when(kk == 0)
        def _():
            acc_ref[...] = extra_ref[...]

        for p in range(npairs):
            @pl.when((kk >= starts[p]) & (kk < starts[p] + nblk[p]))
            def _(p=p):
                acc_ref[...] += lax.dot_general(refs[2 * p][...], refs[2 * p + 1][...], _NT, preferred_element_type=f32)

        @pl.when(kk == nk - 1)
        def _():
            o_ref[...] = acc_ref[...]

    in_specs, args = [], []
    for p, (a, b) in enumerate(pairs):
        def kblock(k, s=starts[p], nb=nblk[p]):
            return jnp.clip(k - s, 0, nb - 1)
        in_specs.append(pl.BlockSpec((tm, tk), lambda i, k, kb=kblock: (i, kb(k))))
        in_specs.append(pl.BlockSpec((n, tk), lambda i, k, kb=kblock: (0, kb(k))))
        args += [a, b]
    tile = pl.BlockSpec((tm, n), lambda i, k: (i, 0))
    return pl.pallas_call(
        body, name=name, grid=(m // tm, nk), in_specs=in_specs + [tile], out_specs=tile,
        out_shape=jax.ShapeDtypeStruct((m, n), f32), scratch_shapes=[pltpu.VMEM((tm, n), f32)],
        compiler_params=_params(("parallel", "arbitrary")),
    )(*args, extra)


def _matmul_tn(a, b, *, tka, tn, tt, name):
    t, ka = a.shape
    n = b.shape[1]

    def body(a_ref, b_ref, o_ref):
        part = lax.dot_general(a_ref[...], b_ref[...], _TN, preferred_element_type=f32)
        kk = pl.program_id(2)

        @pl.when(kk == 0)
        def _():
            o_ref[...] = part

        @pl.when(kk > 0)
        def _():
            o_ref[...] += part

    return pl.pallas_call(
        body, name=name, grid=(ka // tka, n // tn, t // tt),
        in_specs=[pl.BlockSpec((tt, tka), lambda i, j, k: (k, i)), pl.BlockSpec((tt, tn), lambda i, j, k: (k, j))],
        out_specs=pl.BlockSpec((tka, tn), lambda i, j, k: (i, j)),
        out_shape=jax.ShapeDtypeStruct((ka, n), f32),
        compiler_params=_params(("parallel", "parallel", "arbitrary")),
    )(a, b)


def _row_tile(t):
    return min(t, 512)


def _rms_fwd(x, g, *, name):
    t, d = x.shape
    tr = _row_tile(t)

    def body(x_ref, g_ref, h_ref):
        xv = x_ref[...]
        r = lax.rsqrt(jnp.mean(xv * xv, axis=1, keepdims=True) + NORM_EPS)
        h_ref[...] = (xv * r * g_ref[...]).astype(bf16)

    return pl.pallas_call(
        body, name=name, grid=(t // tr,),
        in_specs=[pl.BlockSpec((tr, d), lambda i: (i, 0)), pl.BlockSpec((1, d), lambda i: (0, 0))],
        out_specs=pl.BlockSpec((tr, d), lambda i: (i, 0)),
        out_shape=jax.ShapeDtypeStruct((t, d), bf16),
        compiler_params=_params(("parallel",)),
    )(x, g)


def _rms_bwd(xin, g, dh, dres, *, want_bf16, name):
    t, d = xin.shape
    tr = _row_tile(t)

    def body(x_ref, g_ref, dh_ref, dres_ref, dx_ref, *rest):
        dg_ref = rest[-1]
        xv = x_ref[...]
        r = lax.rsqrt(jnp.mean(xv * xv, axis=1, keepdims=True) + NORM_EPS)
        xn = xv * r
        dhv = dh_ref[...]
        dxn = dhv * g_ref[...]
        dx = dres_ref[...] + r * (dxn - xn * jnp.mean(dxn * xn, axis=1, keepdims=True))
        dx_ref[...] = dx
        if want_bf16:
            rest[0][...] = dx.astype(bf16)
        part = jnp.sum(dhv * xn, axis=0, keepdims=True)

        @pl.when(pl.program_id(0) == 0)
        def _():
            dg_ref[...] = part

        @pl.when(pl.program_id(0) > 0)
        def _():
            dg_ref[...] += part

    row = pl.BlockSpec((tr, d), lambda i: (i, 0))
    vec = pl.BlockSpec((1, d), lambda i: (0, 0))
    out_shape = [jax.ShapeDtypeStruct((t, d), f32)] + ([jax.ShapeDtypeStruct((t, d), bf16)] if want_bf16 else []) \
        + [jax.ShapeDtypeStruct((1, d), f32)]
    return pl.pallas_call(
        body, name=name, grid=(t // tr,),
        in_specs=[row, vec, row, row],
        out_specs=[row] + ([row] if want_bf16 else []) + [vec],
        out_shape=out_shape,
        compiler_params=_params(("arbitrary",)),
    )(xin, g, dh, dres)


def _loss_head(x2, tgt, g, *, name):
    t, d = x2.shape
    tr = _row_tile(t)

    def body(x_ref, t_ref, g_ref, dx_ref, dxb_ref, dg_ref, loss_ref):
        xv = x_ref[...]
        gv = g_ref[...]
        r = lax.rsqrt(jnp.mean(xv * xv, axis=1, keepdims=True) + NORM_EPS)
        xn = xv * r
        e = xn * gv - t_ref[...]
        lpart = jnp.zeros((1, LANES), f32) + 0.5 * _sum_all(jnp.mean(e * e, axis=1, keepdims=True))
        dy = e * (1.0 / d)
        dxn = dy * gv
        dx = r * (dxn - xn * jnp.mean(dxn * xn, axis=1, keepdims=True))
        dx_ref[...] = dx
        dxb_ref[...] = dx.astype(bf16)
        gpart = jnp.sum(dy * xn, axis=0, keepdims=True)

        @pl.when(pl.program_id(0) == 0)
        def _():
            dg_ref[...] = gpart
            loss_ref[...] = lpart

        @pl.when(pl.program_id(0) > 0)
        def _():
            dg_ref[...] += gpart
            loss_ref[...] += lpart

    row = pl.BlockSpec((tr, d), lambda i: (i, 0))
    vec = pl.BlockSpec((1, d), lambda i: (0, 0))
    return pl.pallas_call(
        body, name=name, grid=(t // tr,),
        in_specs=[row, row, vec],
        out_specs=[row, row, vec, pl.BlockSpec((1, LANES), lambda i: (0, 0))],
        out_shape=[jax.ShapeDtypeStruct((t, d), f32), jax.ShapeDtypeStruct((t, d), bf16),
                   jax.ShapeDtypeStruct((1, d), f32), jax.ShapeDtypeStruct((1, LANES), f32)],
        compiler_params=_params(("arbitrary",)),
    )(x2, tgt, g)


def _merge_fwd(pa, pb, gl, bg, *, name):
    t, d = pa.shape
    tr = _row_tile(t)

    def body(pa_ref, pb_ref, gla_ref, glb_ref, bga_ref, bgb_ref, o_ref):
        ga = _sigmoid(gla_ref[...] + bga_ref[...])
        gb = _sigmoid(glb_ref[...] + bgb_ref[...])
        o_ref[...] = (ga * pa_ref[...] + gb * pb_ref[...]).astype(bf16)

    row = pl.BlockSpec((tr, d), lambda i: (i, 0))
    return pl.pallas_call(
        body, name=name, grid=(t // tr,),
        in_specs=[row, row, row, pl.BlockSpec((tr, d), lambda i: (i, 1)),
                  pl.BlockSpec((1, d), lambda i: (0, 0)), pl.BlockSpec((1, d), lambda i: (0, 1))],
        out_specs=row,
        out_shape=jax.ShapeDtypeStruct((t, d), bf16),
        compiler_params=_params(("parallel",)),
    )(pa, pb, gl, gl, bg, bg)


def _merge_bwd(dm, pa, pb, gl, bg, *, name):
    t, d = pa.shape
    tr = _row_tile(t)

    def body(dm_ref, pa_ref, pb_ref, gla_ref, glb_ref, bga_ref, bgb_ref, dpa_ref, dpb_ref, dgl_ref, dbg_ref):
        dmv = dm_ref[...]
        ga = _sigmoid(gla_ref[...] + bga_ref[...])
        gb = _sigmoid(glb_ref[...] + bgb_ref[...])
        dpa_ref[...] = (dmv * ga).astype(bf16)
        dpb_ref[...] = (dmv * gb).astype(bf16)
        dla = dmv * pa_ref[...] * ga * (1.0 - ga)
        dlb = dmv * pb_ref[...] * gb * (1.0 - gb)
        dgl_ref[:, :d] = dla.astype(bf16)
        dgl_ref[:, d:] = dlb.astype(bf16)
        sa = jnp.sum(dla, axis=0, keepdims=True)
        sb = jnp.sum(dlb, axis=0, keepdims=True)

        @pl.when(pl.program_id(0) == 0)
        def _():
            dbg_ref[:, :d] = sa
            dbg_ref[:, d:] = sb

        @pl.when(pl.program_id(0) > 0)
        def _():
            dbg_ref[:, :d] += sa
            dbg_ref[:, d:] += sb

    row = pl.BlockSpec((tr, d), lambda i: (i, 0))
    return pl.pallas_call(
        body, name=name, grid=(t // tr,),
        in_specs=[row, row, row, row, pl.BlockSpec((tr, d), lambda i: (i, 1)),
                  pl.BlockSpec((1, d), lambda i: (0, 0)), pl.BlockSpec((1, d), lambda i: (0, 1))],
        out_specs=[row, row, pl.BlockSpec((tr, 2 * d), lambda i: (i, 0)), pl.BlockSpec((1, 2 * d), lambda i: (0, 0))],
        out_shape=[jax.ShapeDtypeStruct((t, d), bf16), jax.ShapeDtypeStruct((t, d), bf16),
                   jax.ShapeDtypeStruct((t, 2 * d), bf16), jax.ShapeDtypeStruct((1, 2 * d), f32)],
        compiler_params=_params(("arbitrary",)),
    )(dm, pa, pb, gl, gl, bg, bg)


_INV_SQRT2 = 1.0 / math.sqrt(2.0)
_INV_SQRT2PI = 1.0 / math.sqrt(2.0 * math.pi)


def _gelu(x):
    return 0.5 * x * (1.0 + lax.erf(x * _INV_SQRT2))


def _gelu_grad(x):
    return 0.5 * (1.0 + lax.erf(x * _INV_SQRT2)) + x * jnp.exp(-0.5 * x * x) * _INV_SQRT2PI


def _gmlp_common(uv, vg, vb):
    zz = _gelu(uv)
    u = zz[:, :GMLP_WIDTH]
    v = zz[:, GMLP_WIDTH:]
    mu = jnp.mean(v, axis=1, keepdims=True)
    vc = v - mu
    rstd = lax.rsqrt(jnp.mean(vc * vc, axis=1, keepdims=True) + NORM_EPS)
    vhat = vc * rstd
    vn = vhat * vg + vb
    return u, vhat, rstd, vn


def _gmlp_fwd(uv, vg, vb, wsp, bsp_t, *, name):
    t = uv.shape[0]
    nc = t // CHUNK

    def body(uv_ref, vg_ref, vb_ref, w_ref, b_ref, y_ref):
        u, _, _, vn = _gmlp_common(uv_ref[...], vg_ref[...], vb_ref[...])
        tril = _iota((CHUNK, CHUNK), 0) >= _iota((CHUNK, CHUNK), 1)
        bt = b_ref[...]
        for g in range(GMLP_GROUPS):
            sl = slice(g * CHUNK, (g + 1) * CHUNK)
            w = jnp.where(tril, w_ref[g], 0.0)
            s = _dot(w, vn[:, sl]) + bt[:, g:g + 1]
            y_ref[:, sl] = (u[:, sl] * s).astype(bf16)

    return pl.pallas_call(
        body, name=name, grid=(nc,),
        in_specs=[pl.BlockSpec((CHUNK, 2 * GMLP_WIDTH), lambda c: (c, 0)),
                  pl.BlockSpec((1, GMLP_WIDTH), lambda c: (0, 0)), pl.BlockSpec((1, GMLP_WIDTH), lambda c: (0, 0)),
                  pl.BlockSpec((GMLP_GROUPS, CHUNK, CHUNK), lambda c: (0, 0, 0)),
                  pl.BlockSpec((CHUNK, LANES), lambda c: (0, 0))],
        out_specs=pl.BlockSpec((CHUNK, GMLP_WIDTH), lambda c: (c, 0)),
        out_shape=jax.ShapeDtypeStruct((t, GMLP_WIDTH), bf16),
        compiler_params=_params(("parallel",)),
    )(uv, vg, vb, wsp, bsp_t)


def _gmlp_bwd(uv, dya, vg, vb, wsp, bsp_t, *, name):
    t = uv.shape[0]
    nc = t // CHUNK

    def body(uv_ref, dy_ref, vg_ref, vb_ref, w_ref, b_ref, duv_ref, dw_ref, db_ref, dvg_ref, dvb_ref):
        first = pl.program_id(0) == 0

        @pl.when(first)
        def _():
            dw_ref[...] = jnp.zeros_like(dw_ref)
            db_ref[...] = jnp.zeros_like(db_ref)
            dvg_ref[...] = jnp.zeros_like(dvg_ref)
            dvb_ref[...] = jnp.zeros_like(dvb_ref)

        uvv = uv_ref[...]
        vgv = vg_ref[...]
        u, vhat, rstd, vn = _gmlp_common(uvv, vgv, vb_ref[...])
        dy = dy_ref[...]
        tril = _iota((CHUNK, CHUNK), 0) >= _iota((CHUNK, CHUNK), 1)
        lane = _iota((CHUNK, LANES), 1)
        bt = b_ref[...]
        ds_all = dy * u
        dbacc = jnp.zeros((CHUNK, LANES), f32)
        dvh_parts = []
        for g in range(GMLP_GROUPS):
            sl = slice(g * CHUNK, (g + 1) * CHUNK)
            w = jnp.where(tril, w_ref[g], 0.0)
            vng = vn[:, sl]
            s = _dot(w, vng) + bt[:, g:g + 1]
            ds = ds_all[:, sl]
            duv_ref[:, sl] = (dy[:, sl] * s * _gelu_grad(uvv[:, sl])).astype(bf16)
            dw_ref[g] += jnp.where(tril, _dot(ds, vng, _NT), 0.0)
            dbacc = dbacc + jnp.where(lane == g, jnp.sum(ds, axis=1, keepdims=True), 0.0)
            dvn = _dot(w, ds, _TN)
            vh = vhat[:, sl]
            dvg_ref[:, sl] += jnp.sum(dvn * vh, axis=0, keepdims=True)
            dvb_ref[:, sl] += jnp.sum(dvn, axis=0, keepdims=True)
            dvh_parts.append(dvn * vgv[:, sl])
        db_ref[...] += dbacc
        dvhat = jnp.concatenate(dvh_parts, axis=1)
        m1 = jnp.mean(dvhat, axis=1, keepdims=True)
        m2 = jnp.mean(dvhat * vhat, axis=1, keepdims=True)
        dv = rstd * (dvhat - m1 - vhat * m2)
        duv_ref[:, GMLP_WIDTH:] = (dv * _gelu_grad(uvv[:, GMLP_WIDTH:])).astype(bf16)

    vec = pl.BlockSpec((1, GMLP_WIDTH), lambda c: (0, 0))
    return pl.pallas_call(
        body, name=name, grid=(nc,),
        in_specs=[pl.BlockSpec((CHUNK, 2 * GMLP_WIDTH), lambda c: (c, 0)),
                  pl.BlockSpec((CHUNK, GMLP_WIDTH), lambda c: (c, 0)), vec, vec,
                  pl.BlockSpec((GMLP_GROUPS, CHUNK, CHUNK), lambda c: (0, 0, 0)),
                  pl.BlockSpec((CHUNK, LANES), lambda c: (0, 0))],
        out_specs=[pl.BlockSpec((CHUNK, 2 * GMLP_WIDTH), lambda c: (c, 0)),
                   pl.BlockSpec((GMLP_GROUPS, CHUNK, CHUNK), lambda c: (0, 0, 0)),
                   pl.BlockSpec((CHUNK, LANES), lambda c: (0, 0)), vec, vec],
        out_shape=[jax.ShapeDtypeStruct((t, 2 * GMLP_WIDTH), bf16),
                   jax.ShapeDtypeStruct((GMLP_GROUPS, CHUNK, CHUNK), f32),
                   jax.ShapeDtypeStruct((CHUNK, LANES), f32),
                   jax.ShapeDtypeStruct((1, GMLP_WIDTH), f32), jax.ShapeDtypeStruct((1, GMLP_WIDTH), f32)],
        compiler_params=_params(("arbitrary",)),
    )(uv, dya, vg, vb, wsp, bsp_t)


_CONV_COLS = 512
_XS0, _B0, _C0 = 0, D_INNER, D_INNER + N_GROUPS * D_STATE


_TAIL = 8


def _conv_silu(cur_ref, tail_ref, w_ref, b_ref, has_prev, xc_ref, cv_ref):
    row = _iota((_TAIL, _CONV_COLS), 0)
    for j in range(CONV_DIM // _CONV_COLS):
        sl = slice(j * _CONV_COLS, (j + 1) * _CONV_COLS)
        cur = cur_ref[:, sl]
        tail = jnp.where(has_prev, tail_ref[:, sl], 0.0)
        acc = cur * w_ref[CONV_W - 1:CONV_W, sl] + b_ref[:, sl]
        for s in range(1, CONV_W):
            rolled = pltpu.roll(cur, s, 0)
            top = jnp.where(row >= s, rolled[:_TAIL], pltpu.roll(tail, s, 0))
            sh = jnp.concatenate([top, rolled[_TAIL:]], axis=0)
            acc = acc + sh * w_ref[CONV_W - 1 - s:CONV_W - s, sl]
        if cv_ref is not None:
            cv_ref[:, sl] = acc
        xc_ref[:, sl] = acc * _sigmoid(acc)


def _col_bcast(mat, h):
    return jnp.broadcast_to(mat[:, h:h + 1], (CHUNK, LANES))


def _head_expand(cols):
    lo = _iota((CHUNK, LANES), 1) < HEAD_DIM
    return jnp.concatenate([jnp.where(lo, cols[2 * j], cols[2 * j + 1]) for j in range(N_HEADS // 2)], axis=1)


def _ssd_chunk_scalars(dtr, dtb, alog):
    xdt_pre = dtr + dtb
    dtv = jnp.maximum(xdt_pre, 0.0) + jnp.log(1.0 + jnp.exp(-jnp.abs(xdt_pre)))
    a = -jnp.exp(alog)
    ltri = (_iota((CHUNK, CHUNK), 0) >= _iota((CHUNK, CHUNK), 1)).astype(f32)
    cs = _dot32(ltri, dtv * a)
    csb = [_col_bcast(cs, h) for h in range(N_HEADS)]
    cs_x = _head_expand(csb)
    dt_x = _head_expand([_col_bcast(dtv, h) for h in range(N_HEADS)])
    cl_x = cs_x[CHUNK - 1:CHUNK, :]
    return dict(xdt_pre=xdt_pre, dtv=dtv, a=a, cs=cs, cs_t=cs.T, csb=csb, dt_x=dt_x, e_x=jnp.exp(cs_x),
                dec_x=jnp.exp(cl_x - cs_x), dk_x=jnp.exp(cl_x))


def _head_masks():
    lane = _iota((CHUNK, GROUP_W), 1)
    return [(lane >= r * HEAD_DIM) & (lane < (r + 1) * HEAD_DIM) for r in range(HEADS_PER_GROUP)]


def _stack_heads(a, masks):
    return jnp.concatenate([jnp.where(m, a, 0.0) for m in masks], axis=0).astype(bf16)


def _seg_sum(a, seg):
    hi = a.astype(jnp.bfloat16)
    lo = (a - hi.astype(f32)).astype(jnp.bfloat16)
    return (lax.dot_general(hi, seg, _NN, preferred_element_type=f32)
            + lax.dot_general(lo, seg, _NN, preferred_element_type=f32))


def _head_seg_matrix():
    return (_iota((D_INNER, LANES), 0) // HEAD_DIM == _iota((D_INNER, LANES), 1)).astype(jnp.bfloat16)


def _ssd_fwd(xbc, z, dtr, cw, cb, dtb, alog, dsk_x, gs, *, name):
    t = xbc.shape[0]
    nc = t // CHUNK
    tiles = CHUNK // _TAIL

    def body(cur_ref, tail_ref, z_ref, dtr_ref, cw_ref, cb_ref, dtb_ref, alog_ref, dsk_ref, gs_ref,
             yb_ref, hp_ref, state_ref, xc_ref):
        c = pl.program_id(0)

        @pl.when(c == 0)
        def _():
            state_ref[...] = jnp.zeros_like(state_ref)

        _conv_silu(cur_ref, tail_ref, cw_ref, cb_ref, c > 0, xc_ref, None)
        sc = _ssd_chunk_scalars(dtr_ref[...], dtb_ref[...], alog_ref[...])
        tril = _iota((CHUNK, CHUNK), 0) >= _iota((CHUNK, CHUNK), 1)
        masks = _head_masks()
        hp_ref[0] = state_ref[...]
        for g in range(N_GROUPS):
            gsl = slice(g * GROUP_W, (g + 1) * GROUP_W)
            xs_g = xc_ref[:, gsl]
            bg = xc_ref[:, _B0 + g * D_STATE:_B0 + (g + 1) * D_STATE]
            cg = xc_ref[:, _C0 + g * D_STATE:_C0 + (g + 1) * D_STATE]
            xdt_g = xs_g * sc["dt_x"][:, gsl]
            cbm = _dot(cg, bg, _NT)
            mw = jnp.concatenate(
                [cbm * jnp.exp(jnp.where(tril, sc["csb"][h] - sc["cs_t"][h:h + 1, :], -1e30))
                 for h in range(g * HEADS_PER_GROUP, (g + 1) * HEADS_PER_GROUP)], axis=1)
            ht_g = state_ref[:, gsl]
            y_g = _dot(mw, _stack_heads(xdt_g, masks)) + sc["e_x"][:, gsl] * _dot(cg, ht_g) + dsk_ref[:, gsl] * xs_g
            state_ref[:, gsl] = ht_g * sc["dk_x"][:, gsl] + _dot(bg, xdt_g * sc["dec_x"][:, gsl], _TN)
            zg = z_ref[:, gsl]
            yg = y_g * zg * _sigmoid(zg)
            rs = lax.rsqrt(jnp.mean(yg * yg, axis=1, keepdims=True) + NORM_EPS)
            yb_ref[:, gsl] = (yg * rs * gs_ref[:, gsl]).astype(bf16)

    def chunk(w):
        return pl.BlockSpec((CHUNK, w), lambda c: (c, 0))

    def const(shape):
        return pl.BlockSpec(shape, lambda c: (0,) * len(shape))

    return pl.pallas_call(
        body, name=name, grid=(nc,),
        in_specs=[chunk(CONV_DIM), pl.BlockSpec((_TAIL, CONV_DIM), lambda c: (jnp.maximum(c * tiles - 1, 0), 0)),
                  chunk(D_INNER), chunk(LANES), const((CONV_W, CONV_DIM)), const((1, CONV_DIM)),
                  const((1, LANES)), const((1, LANES)), const((1, D_INNER)), const((1, D_INNER))],
        out_specs=[chunk(D_INNER), pl.BlockSpec((1, D_STATE, D_INNER), lambda c: (c, 0, 0))],
        out_shape=[jax.ShapeDtypeStruct((t, D_INNER), bf16), jax.ShapeDtypeStruct((nc, D_STATE, D_INNER), f32)],
        scratch_shapes=[pltpu.VMEM((D_STATE, D_INNER), f32), pltpu.VMEM((CHUNK, CONV_DIM), f32)],
        compiler_params=_params(("arbitrary",)),
    )(xbc, xbc, z, dtr, cw, cb, dtb, alog, dsk_x, gs)


def _ssd_bwd(xbc, z, dtr, hprev, dyb, cw, cb, dtb, alog, dsk_x, gs, seg, *, name):
    t = xbc.shape[0]
    nc = t // CHUNK
    tiles = CHUNK // _TAIL

    def body(cur_ref, tail_ref, z_ref, dtr_ref, hp_ref, dyb_ref, cw_ref, cb_ref, dtb_ref, alog_ref, dsk_ref, gs_ref,
             seg_ref, dz_ref, dxbc_ref, ddt_ref, dcw_ref, dcb_ref, ddtb_ref, dalog_ref, ddsk_ref, dgs_ref,
             dh_ref, dcnext_ref, xc_ref, cv_ref, dxc_ref, x13_ref, x2_ref, rows_ref):
        i = pl.program_id(0)
        cc = nc - 1 - i

        @pl.when(i == 0)
        def _():
            for ref in (dh_ref, dcnext_ref, dcw_ref, dcb_ref, ddtb_ref, dalog_ref, ddsk_ref, dgs_ref, rows_ref):
                ref[...] = jnp.zeros_like(ref)

        _conv_silu(cur_ref, tail_ref, cw_ref, cb_ref, cc > 0, xc_ref, cv_ref)
        sc = _ssd_chunk_scalars(dtr_ref[...], dtb_ref[...], alog_ref[...])
        tril = _iota((CHUNK, CHUNK), 0) >= _iota((CHUNK, CHUNK), 1)
        triu = _iota((CHUNK, CHUNK), 0) <= _iota((CHUNK, CHUNK), 1)
        masks = _head_masks()
        rowh = _iota((N_HEADS, CHUNK), 0)
        dcs_t = jnp.zeros((N_HEADS, CHUNK), f32)
        for g in range(N_GROUPS):
            gsl = slice(g * GROUP_W, (g + 1) * GROUP_W)
            xs_g = xc_ref[:, gsl]
            bg = xc_ref[:, _B0 + g * D_STATE:_B0 + (g + 1) * D_STATE]
            cg = xc_ref[:, _C0 + g * D_STATE:_C0 + (g + 1) * D_STATE]
            dt_g, e_g, dec_g, dk_g = sc["dt_x"][:, gsl], sc["e_x"][:, gsl], sc["dec_x"][:, gsl], sc["dk_x"][:, gsl]
            dsk_g = dsk_ref[:, gsl]
            xdt_g = xs_g * dt_g
            xdt_stack = _stack_heads(xdt_g, masks)
            cbm = _dot(cg, bg, _NT)
            cbt = _dot(bg, cg, _NT)
            heads = range(g * HEADS_PER_GROUP, (g + 1) * HEADS_PER_GROUP)
            lmats = [jnp.exp(jnp.where(tril, sc["csb"][h] - sc["cs_t"][h:h + 1, :], -1e30)) for h in heads]
            mw = jnp.concatenate([cbm * lm for lm in lmats], axis=1)
            mtw = jnp.concatenate(
                [cbt * jnp.exp(jnp.where(triu, sc["cs_t"][h:h + 1, :] - sc["csb"][h], -1e30)) for h in heads], axis=1)
            ht_g = hp_ref[0, :, gsl]
            dhn_g = dh_ref[:, gsl]
            yoff = e_g * _dot(cg, ht_g)
            y_g = _dot(mw, xdt_stack) + yoff + dsk_g * xs_g
            zg = z_ref[:, gsl]
            sz = _sigmoid(zg)
            silu = zg * sz
            yg = y_g * silu
            rs = lax.rsqrt(jnp.mean(yg * yg, axis=1, keepdims=True) + NORM_EPS)
            yn = yg * rs
            dyb = dyb_ref[:, gsl]
            dgs_ref[:, gsl] += jnp.sum(dyb * yn, axis=0, keepdims=True)
            dyn = dyb * gs_ref[:, gsl]
            dyg = rs * (dyn - yn * jnp.mean(dyn * yn, axis=1, keepdims=True))
            dy_g = dyg * silu
            dz_ref[:, gsl] = (dyg * y_g * (sz * (1.0 + zg * (1.0 - sz)))).astype(bf16)
            dy_stack = _stack_heads(dy_g, masks)
            dm_w = _dot(dy_g, xdt_stack, _NT)
            dmt_w = _dot(xdt_g, dy_stack, _NT)
            dxdt = _dot(mtw, dy_stack)
            dcb_acc = jnp.zeros((CHUNK, CHUNK), f32)
            for r, h in enumerate(heads):
                hs = slice(r * CHUNK, (r + 1) * CHUNK)
                dml = dm_w[:, hs] * lmats[r]
                dcb_acc = dcb_acc + dml
                col = jnp.sum(dml * cbm, axis=0, keepdims=True)
                row = jnp.sum(dmt_w[:, hs] * mtw[:, hs], axis=0, keepdims=True)
                dcs_t = dcs_t + jnp.where(rowh == h, row - col, 0.0)
            w = _dot(bg, dhn_g)
            dxdt = dxdt + dec_g * w
            decx3 = dec_g * (xdt_g * w)
            dg_g = e_g * dy_g
            d_c = _dot(dg_g, ht_g, _NT) + _dot(dcb_acc, bg)
            d_b = _dot(dcb_acc, cg, _TN) + _dot(xdt_g * dec_g, dhn_g, _NT)
            dh_ref[:, gsl] = dhn_g * dk_g + _dot(cg, dg_g, _TN)
            dxc_ref[:, gsl] = dsk_g * dy_g + dxdt * dt_g
            dxc_ref[:, _B0 + g * D_STATE:_B0 + (g + 1) * D_STATE] = d_b
            dxc_ref[:, _C0 + g * D_STATE:_C0 + (g + 1) * D_STATE] = d_c
            x13_ref[:, gsl] = dy_g * yoff - decx3
            x2_ref[:, gsl] = dxdt * xs_g
            rows_ref[0:1, gsl] = jnp.sum(dhn_g * ht_g, axis=0, keepdims=True)
            rows_ref[1:2, gsl] = jnp.sum(decx3, axis=0, keepdims=True)
            rows_ref[2:3, gsl] = jnp.sum(dy_g * xs_g, axis=0, keepdims=True)
        segm = seg_ref[...]
        r13 = _seg_sum(x13_ref[...], segm)
        r2 = _seg_sum(x2_ref[...], segm)
        small = _seg_sum(rows_ref[...], segm)
        lane = _iota((CHUNK, LANES), 1)
        rowi = _iota((CHUNK, LANES), 0)
        dcl_row = small[0:1, :] * jnp.exp(sc["cs"][CHUNK - 1:CHUNK, :]) + small[1:2, :]
        dcs = r13 + jnp.where(rowi == CHUNK - 1, dcl_row, 0.0)
        dcs_t_all = dcs.T + jnp.concatenate([dcs_t, jnp.zeros((LANES - N_HEADS, CHUNK), f32)], axis=0)
        dda = _dot32(dcs_t_all, tril.astype(f32)).T
        a = sc["a"]
        ddt_total = r2 + dda * a
        dalog_ref[...] += jnp.sum(dda * sc["dtv"], axis=0, keepdims=True) * a
        ddtr = jnp.where(lane < N_HEADS, ddt_total * _sigmoid(sc["xdt_pre"]), 0.0)
        ddtb_ref[...] += jnp.sum(ddtr, axis=0, keepdims=True)
        ddt_ref[...] = ddtr.astype(bf16)
        ddsk_ref[...] += small[2:3, :]
        row8 = _iota((_TAIL, _CONV_COLS), 0)
        for j in range(CONV_DIM // _CONV_COLS):
            sl = slice(j * _CONV_COLS, (j + 1) * _CONV_COLS)
            cvv = cv_ref[:, sl]
            sg = _sigmoid(cvv)
            dconv = dxc_ref[:, sl] * (sg * (1.0 + cvv * (1.0 - sg)))
            nxt = dcnext_ref[:, sl]
            cur = cur_ref[:, sl]
            dxin = dconv * cw_ref[CONV_W - 1:CONV_W, sl]
            dcw_ref[CONV_W - 1:CONV_W, sl] += jnp.sum(dconv * cur, axis=0, keepdims=True)
            for s in range(1, CONV_W):
                rolled = pltpu.roll(dconv, CHUNK - s, 0)
                bot = jnp.where(row8 < _TAIL - s, rolled[CHUNK - _TAIL:], pltpu.roll(nxt, _TAIL - s, 0))
                up = jnp.concatenate([rolled[:CHUNK - _TAIL], bot], axis=0)
                dxin = dxin + up * cw_ref[CONV_W - 1 - s:CONV_W - s, sl]
                dcw_ref[CONV_W - 1 - s:CONV_W - s, sl] += jnp.sum(up * cur, axis=0, keepdims=True)
            dcb_ref[:, sl] += jnp.sum(dconv, axis=0, keepdims=True)
            dxbc_ref[:, sl] = dxin.astype(bf16)
            dcnext_ref[:, sl] = dconv[:_TAIL]

    def chunk(w):
        return pl.BlockSpec((CHUNK, w), lambda i: (nc - 1 - i, 0))

    def const(shape):
        return pl.BlockSpec(shape, lambda i: (0,) * len(shape))

    return pl.pallas_call(
        body, name=name, grid=(nc,),
        in_specs=[chunk(CONV_DIM),
                  pl.BlockSpec((_TAIL, CONV_DIM), lambda i: (jnp.maximum((nc - 1 - i) * tiles - 1, 0), 0)),
                  chunk(D_INNER), chunk(LANES), pl.BlockSpec((1, D_STATE, D_INNER), lambda i: (nc - 1 - i, 0, 0)),
                  chunk(D_INNER), const((CONV_W, CONV_DIM)), const((1, CONV_DIM)),
                  const((1, LANES)), const((1, LANES)), const((1, D_INNER)), const((1, D_INNER)),
                  const((D_INNER, LANES))],
        out_specs=[chunk(D_INNER), chunk(CONV_DIM), chunk(LANES), const((CONV_W, CONV_DIM)), const((1, CONV_DIM)),
                   const((1, LANES)), const((1, LANES)), const((1, LANES)), const((1, D_INNER))],
        out_shape=[jax.ShapeDtypeStruct((t, D_INNER), bf16), jax.ShapeDtypeStruct((t, CONV_DIM), bf16),
                   jax.ShapeDtypeStruct((t, LANES), bf16), jax.ShapeDtypeStruct((CONV_W, CONV_DIM), f32),
                   jax.ShapeDtypeStruct((1, CONV_DIM), f32), jax.ShapeDtypeStruct((1, LANES), f32),
                   jax.ShapeDtypeStruct((1, LANES), f32), jax.ShapeDtypeStruct((1, LANES), f32),
                   jax.ShapeDtypeStruct((1, D_INNER), f32)],
        scratch_shapes=[pltpu.VMEM((D_STATE, D_INNER), f32), pltpu.VMEM((_TAIL, CONV_DIM), f32),
                        pltpu.VMEM((CHUNK, CONV_DIM), f32), pltpu.VMEM((CHUNK, CONV_DIM), f32),
                        pltpu.VMEM((CHUNK, CONV_DIM), f32), pltpu.VMEM((CHUNK, D_INNER), f32),
                        pltpu.VMEM((CHUNK, D_INNER), f32), pltpu.VMEM((_TAIL, D_INNER), f32)],
        compiler_params=_params(("arbitrary",)),
    )(xbc, xbc, z, dtr, hprev, dyb, cw, cb, dtb, alog, dsk_x, gs, seg)


def _adamw(w, g, m, v, *, name):
    r, c = w.shape
    tr = r
    while tr * c * 4 > _MB and tr % 16 == 0:
        tr //= 2

    def body(w_ref, g_ref, m_ref, v_ref, d_ref, m2_ref, v2_ref):
        gv = g_ref[...]
        m2 = ADAM_B1 * m_ref[...] + (1.0 - ADAM_B1) * gv
        v2 = ADAM_B2 * v_ref[...] + (1.0 - ADAM_B2) * (gv * gv)
        m_hat = m2 / (1.0 - ADAM_B1 ** ADAM_STEP)
        v_hat = v2 / (1.0 - ADAM_B2 ** ADAM_STEP)
        d_ref[...] = -ADAM_LR * (m_hat / (jnp.sqrt(v_hat) + ADAM_EPS) + ADAM_WD * w_ref[...])
        m2_ref[...] = m2
        v2_ref[...] = v2

    blk = pl.BlockSpec((tr, c), lambda i: (i, 0))
    return pl.pallas_call(
        body, name=name, grid=(r // tr,),
        in_specs=[blk] * 4, out_specs=[blk] * 3,
        out_shape=[jax.ShapeDtypeStruct((r, c), f32)] * 3,
        compiler_params=_params(("parallel",)),
    )(w, g, m, v)


def _cast_bf16(a, *, name):
    r, c = a.shape
    tr = 2256 if r % 2256 == 0 else r

    def body(a_ref, o_ref):
        o_ref[...] = a_ref[...].astype(bf16)

    blk = pl.BlockSpec((tr, c), lambda i: (i, 0))
    return pl.pallas_call(
        body, name=name, grid=(r // tr,), in_specs=[blk], out_specs=blk,
        out_shape=jax.ShapeDtypeStruct((r, c), bf16), compiler_params=_params(("parallel",)),
    )(a)


_ANY = pl.BlockSpec(memory_space=pl.ANY)


def _place():
    x, y, c = lax.axis_index("x"), lax.axis_index("y"), lax.axis_index("c")
    other_chips = [(1 - x, y), (x, 1 - y), (1 - x, 1 - y)]
    return x, y, c, other_chips


def _gather_shards(shard, *, name):
    _, rh, lanes = shard.shape

    def body(in_ref, out_ref, send_sems, recv_sems):
        x, y, c, chips = _place()
        me = 2 * x + y
        sibling = (x, y, 1 - c)

        def cp(k, chip, half, to, src=None):
            dst = out_ref.at[chip, half]
            return pltpu.make_async_remote_copy(
                src_ref=dst if src is None else src, dst_ref=dst, send_sem=send_sems.at[k], recv_sem=recv_sems.at[k],
                device_id=to, device_id_type=MESH)

        first = [cp(j, me, c, (cx, cy, c), src=in_ref.at[c]) for j, (cx, cy) in enumerate(chips)]
        for f in first:
            f.start()
        passed = []
        for j, (cx, cy) in enumerate(chips):
            cp(j, 2 * cx + cy, c, sibling).wait_recv()
            p = cp(3 + j, 2 * cx + cy, c, sibling)
            p.start()
            passed.append(p)
        for j, (cx, cy) in enumerate(chips):
            cp(3 + j, 2 * cx + cy, 1 - c, sibling).wait_recv()
        for f in first + passed:
            f.wait_send()

    return pl.pallas_call(
        body, name=name, in_specs=[_ANY], out_specs=_ANY,
        out_shape=jax.ShapeDtypeStruct((N_CHIPS, 2, rh, lanes), shard.dtype),
        scratch_shapes=[pltpu.SemaphoreType.DMA((6,)), pltpu.SemaphoreType.DMA((6,))],
    )(shard)


def _rs_swap_halves(g, *, name):
    nch, _, rh, lanes = g.shape

    def body(g_ref, out_ref, send_sems, recv_sems):
        x, y, c, _ = _place()
        copies = [pltpu.make_async_remote_copy(
            src_ref=g_ref.at[k, 1 - c], dst_ref=out_ref.at[k], send_sem=send_sems.at[k], recv_sem=recv_sems.at[k],
            device_id=(x, y, 1 - c), device_id_type=MESH) for k in range(nch)]
        for cpy in copies:
            cpy.start()
        for cpy in copies:
            cpy.wait()

    return pl.pallas_call(
        body, name=name, in_specs=[_ANY], out_specs=_ANY,
        out_shape=jax.ShapeDtypeStruct((nch, rh, lanes), g.dtype),
        scratch_shapes=[pltpu.SemaphoreType.DMA((nch,)), pltpu.SemaphoreType.DMA((nch,))],
    )(g)


def _rs_add_pair(g, got, c_idx, *, name):
    nch, _, rh, lanes = g.shape
    tr = rh // 10 if rh % 160 == 0 else rh

    def body(c_ref, g_ref, got_ref, p32_ref, p16_ref):
        s = g_ref[...] + got_ref[...]
        p32_ref[...] = s
        p16_ref[...] = s.astype(bf16)

    blk = pl.BlockSpec((None, tr, lanes), lambda k, i, c_ref: (k, i, 0))
    return pl.pallas_call(
        body, name=name,
        grid_spec=pltpu.PrefetchScalarGridSpec(
            num_scalar_prefetch=1, grid=(nch, rh // tr),
            in_specs=[pl.BlockSpec((None, None, tr, lanes), lambda k, i, c_ref: (k, c_ref[0], i, 0)), blk],
            out_specs=[blk, blk]),
        out_shape=[jax.ShapeDtypeStruct((nch, rh, lanes), f32), jax.ShapeDtypeStruct((nch, rh, lanes), bf16)],
        compiler_params=_params(("parallel", "parallel")),
    )(c_idx, g, got)


def _rs_scatter_chips(p16, *, name):
    _, rh, lanes = p16.shape

    def body(p_ref, out_ref, send_sems, recv_sems):
        x, y, c, chips = _place()
        copies = [pltpu.make_async_remote_copy(
            src_ref=p_ref.at[2 * cx + cy], dst_ref=out_ref.at[j], send_sem=send_sems.at[j], recv_sem=recv_sems.at[j],
            device_id=(cx, cy, c), device_id_type=MESH) for j, (cx, cy) in enumerate(chips)]
        for cpy in copies:
            cpy.start()
        for cpy in copies:
            cpy.wait()

    return pl.pallas_call(
        body, name=name, in_specs=[_ANY], out_specs=_ANY,
        out_shape=jax.ShapeDtypeStruct((3, rh, lanes), p16.dtype),
        scratch_shapes=[pltpu.SemaphoreType.DMA((3,)), pltpu.SemaphoreType.DMA((3,))],
    )(p16)


def _rs_add_chips(p32, got, me_idx, *, name):
    _, rh, lanes = p32.shape
    tr = rh // 10 if rh % 160 == 0 else rh

    def body(me_ref, p_ref, got_ref, o_ref):
        o_ref[...] = ((p_ref[...] + got_ref[0].astype(f32)) + got_ref[1].astype(f32)) + got_ref[2].astype(f32)

    return pl.pallas_call(
        body, name=name,
        grid_spec=pltpu.PrefetchScalarGridSpec(
            num_scalar_prefetch=1, grid=(rh // tr,),
            in_specs=[pl.BlockSpec((None, tr, lanes), lambda i, me_ref: (me_ref[0], i, 0)),
                      pl.BlockSpec((3, tr, lanes), lambda i, me_ref: (0, i, 0))],
            out_specs=pl.BlockSpec((tr, lanes), lambda i, me_ref: (i, 0))),
        out_shape=jax.ShapeDtypeStruct((rh, lanes), f32),
        compiler_params=_params(("parallel",)),
    )(me_idx, p32, got)


def _rs_join_halves(half, *, name):
    rh, lanes = half.shape

    def body(h_ref, out_ref, send_sem, recv_sem):
        x, y, c, _ = _place()
        cpy = pltpu.make_async_remote_copy(
            src_ref=h_ref, dst_ref=out_ref, send_sem=send_sem, recv_sem=recv_sem,
            device_id=(x, y, 1 - c), device_id_type=MESH)
        cpy.start()
        cpy.wait()

    return pl.pallas_call(
        body, name=name, in_specs=[_ANY], out_specs=_ANY,
        out_shape=jax.ShapeDtypeStruct((rh, lanes), half.dtype),
        scratch_shapes=[pltpu.SemaphoreType.DMA, pltpu.SemaphoreType.DMA],
    )(half)


def _all_reduce_small(s, *, name):
    rs, lanes = s.shape

    def body(s_ref, o_ref, buf_ref, send_sems, recv_sems):
        x, y, c, _ = _place()
        me = 4 * x + 2 * y + c
        peers = []
        for k in range(1, N_DEV):
            px = 1 - x if (k >> 2) & 1 else x
            py = 1 - y if (k >> 1) & 1 else y
            pc = 1 - c if k & 1 else c
            peers.append((px, py, pc))
        copies = [pltpu.make_async_remote_copy(
            src_ref=s_ref, dst_ref=buf_ref.at[me], send_sem=send_sems.at[k], recv_sem=recv_sems.at[k],
            device_id=peer, device_id_type=MESH) for k, peer in enumerate(peers)]
        for cpy in copies:
            cpy.start()
        buf_ref[me] = s_ref[...]
        for k, (px, py, pc) in enumerate(peers):
            pltpu.make_async_remote_copy(
                src_ref=s_ref, dst_ref=buf_ref.at[4 * px + 2 * py + pc], send_sem=send_sems.at[k],
                recv_sem=recv_sems.at[k], device_id=(px, py, pc), device_id_type=MESH).wait_recv()
        for cpy in copies:
            cpy.wait_send()
        acc = buf_ref[0]
        for d in range(1, N_DEV):
            acc = acc + buf_ref[d]
        o_ref[...] = acc

    vm = pl.BlockSpec(memory_space=pltpu.VMEM)
    return pl.pallas_call(
        body, name=name, in_specs=[vm], out_specs=vm,
        out_shape=jax.ShapeDtypeStruct((rs, lanes), f32),
        scratch_shapes=[pltpu.VMEM((N_DEV, rs, lanes), f32), pltpu.SemaphoreType.DMA((N_DEV - 1,)),
                        pltpu.SemaphoreType.DMA((N_DEV - 1,))],
        compiler_params=pltpu.CompilerParams(vmem_limit_bytes=32 * _MB),
    )(s)


def _pad_lanes(a, width=LANES):
    return jnp.pad(a, ((0, 0), (0, width - a.shape[1])))


def _local_grads(x, tgt, wts, small):
    t = x.shape[0]
    tm = min(t, 1024)
    d = D_MODEL
    mm = functools.partial(_matmul, tm=tm)

    dtb = _pad_lanes(small["dt_bias"])
    alog = _pad_lanes(small["a_log"])
    dsk = jnp.repeat(small["d_skip"], HEAD_DIM, axis=1)
    bsp_t = _pad_lanes(small["b_spatial"].T)
    wsp = small["w_spatial"]

    h = _rms_fwd(x, small["norm_mix_g"], name="rms_mix")
    uv = mm(h, wts["uv"], tn=1024, tk=d, out_dtypes=[f32], name="proj_uv")
    z = mm(h, wts["z"], tn=1024, tk=d, out_dtypes=[f32], name="proj_z")
    xbc = mm(h, wts["xbc"], tn=1024, tk=d, out_dtypes=[f32], name="proj_xbc")
    dtr = mm(h, wts["dt"], tn=LANES, tk=d, out_dtypes=[f32], name="proj_dt")
    gl = mm(h, wts["gate"], tn=1024, tk=d, out_dtypes=[f32], name="proj_gate")
    ya = _gmlp_fwd(uv, small["v_norm_g"], small["v_norm_b"], wsp, bsp_t, name="gmlp_fwd")
    yb, hprev = _ssd_fwd(xbc, z, dtr, small["conv_w"], small["conv_b"], dtb, alog, dsk, small["ssm_norm_g"],
                         name="ssd_fwd")
    pa = mm(ya, wts["pa"], tn=1024, tk=1024, out_dtypes=[f32], name="proj_a")
    pb = mm(yb, wts["pb"], tn=1024, tk=1024, out_dtypes=[f32], name="proj_b")
    merged = _merge_fwd(pa, pb, gl, small["b_gates"], name="merge_fwd")
    x1 = mm(merged, wts["out"], tn=1024, tk=1024, out_dtypes=[f32], extras=[x],
            epilogue=lambda acc, res: (res + acc,), name="out_proj")
    h2 = _rms_fwd(x1, small["norm_mlp_g"], name="rms_mlp")
    up, act = mm(h2, wts["up"], tn=1024, tk=d, out_dtypes=[f32, bf16],
                 epilogue=lambda acc: (acc, jnp.square(jnp.maximum(acc, 0.0))), name="mlp_up")
    x2 = mm(act, wts["down"], tn=1024, tk=1024, out_dtypes=[f32], extras=[x1],
            epilogue=lambda acc, res: (res + acc,), name="mlp_down")

    dx2, dx2b, dgf, loss = _loss_head(x2, tgt, small["norm_final_g"], name="loss_head")
    tt = min(t, 1024)
    tn_mm = functools.partial(_matmul_tn, tt=tt)
    dw = {}
    dw["down"] = tn_mm(act, dx2b, tka=1024, tn=1024, name="dw_down")
    dup = mm(dx2b, wts["down"], nt=True, tn=1024, tk=1024, out_dtypes=[bf16], extras=[up],
             epilogue=lambda acc, u: (acc * (2.0 * jnp.maximum(u, 0.0)),), name="d_act")
    dw["up"] = tn_mm(h2, dup, tka=1024, tn=1024, name="dw_up")
    dh2 = mm(dup, wts["up"], nt=True, tn=1024, tk=1024, out_dtypes=[f32], name="d_h2")
    dx1, dx1b, dg_mlp = _rms_bwd(x1, small["norm_mlp_g"], dh2, dx2, want_bf16=True, name="rms_mlp_bwd")
    dw["out"] = tn_mm(merged, dx1b, tka=1024, tn=1024, name="dw_out")
    dmerged = mm(dx1b, wts["out"], nt=True, tn=1024, tk=1024, out_dtypes=[f32], name="d_merged")
    dpa, dpb, dgl, dbg = _merge_bwd(dmerged, pa, pb, gl, small["b_gates"], name="merge_bwd")
    dw["pa"] = tn_mm(ya, dpa, tka=1024, tn=1024, name="dw_pa")
    dw["pb"] = tn_mm(yb, dpb, tka=1024, tn=1024, name="dw_pb")
    dya = mm(dpa, wts["pa"], nt=True, tn=1024, tk=1024, out_dtypes=[f32], name="d_ya")
    dyb = mm(dpb, wts["pb"], nt=True, tn=1024, tk=1024, out_dtypes=[f32], name="d_yb")
    duv, dwsp, dbsp_t, dvg, dvb = _gmlp_bwd(uv, dya, small["v_norm_g"], small["v_norm_b"], wsp, bsp_t,
                                            name="gmlp_bwd")
    dz, dxbc, ddt, dcw, dcb, ddtb, dalog, ddsk, dgs = _ssd_bwd(
        xbc, z, dtr, hprev, dyb, small["conv_w"], small["conv_b"], dtb, alog, dsk, small["ssm_norm_g"],
        _head_seg_matrix(), name="ssd_bwd")
    dw["uv"] = tn_mm(h, duv, tka=1024, tn=1024, name="dw_uv")
    dw["z"] = tn_mm(h, dz, tka=1024, tn=1024, name="dw_z")
    dw["xbc"] = tn_mm(h, dxbc, tka=1024, tn=1024, name="dw_xbc")
    dw["dt"] = tn_mm(h, ddt, tka=1024, tn=LANES, name="dw_dt")
    dw["gate"] = tn_mm(h, dgl, tka=1024, tn=1024, name="dw_gate")
    dh = mm(ddt, wts["dt"], nt=True, tn=1024, tk=LANES, out_dtypes=[f32], name="d_h_dt")
    dh = _matmul_nt_sum([(duv, wts["uv"]), (dz, wts["z"]), (dxbc, wts["xbc"]), (dgl, wts["gate"])], dh,
                        tm=tm, tk=512, name="d_h")
    dx, dg_mix = _rms_bwd(x, small["norm_mix_g"], dh, dx1, want_bf16=False, name="rms_mix_bwd")

    dsmall = {
        "norm_mix_g": dg_mix, "conv_w": dcw, "conv_b": dcb, "dt_bias": ddtb[:, :N_HEADS], "a_log": dalog[:, :N_HEADS],
        "d_skip": ddsk[:, :N_HEADS], "ssm_norm_g": dgs, "v_norm_g": dvg, "v_norm_b": dvb, "w_spatial": dwsp,
        "b_spatial": dbsp_t[:, :GMLP_GROUPS].T, "b_gates": dbg, "norm_mlp_g": dg_mlp, "norm_final_g": dgf,
    }
    return loss, dx, dw, dsmall


_IN_SHARD = IN_PROJ // N_CHIPS
_DENSE = ("w_in", "w_proj_a", "w_proj_b", "w_out", "w_mlp_up", "w_mlp_down")
_DENSE_SHARD_SHAPES = {"w_in": (D_MODEL, _IN_SHARD), "w_proj_a": (GMLP_WIDTH // N_CHIPS, D_MODEL),
                       "w_proj_b": (D_INNER // N_CHIPS, D_MODEL), "w_out": (D_MODEL // N_CHIPS, D_MODEL),
                       "w_mlp_up": (D_MODEL, D_FF // N_CHIPS), "w_mlp_down": (D_FF // N_CHIPS, D_MODEL)}
_DENSE_ROWS = {k: s[0] * s[1] // LANES for k, s in _DENSE_SHARD_SHAPES.items()}
_DENSE_TOTAL = sum(_DENSE_ROWS.values())
_CONV_ROWS = CONV_W * (CONV_DIM // N_CHIPS) * 2 // LANES


def _dense_offsets():
    off, out = 0, {}
    for k in _DENSE:
        out[k] = off
        off += _DENSE_ROWS[k]
    return out


_DENSE_OFF = _dense_offsets()

_SMALL = ("norm_mix_g", "conv_w", "conv_b", "dt_bias", "a_log", "d_skip", "ssm_norm_g", "v_norm_g", "v_norm_b",
          "w_spatial", "b_spatial", "b_gates", "norm_mlp_g", "norm_final_g")


def _pack_small(parts):
    flat = jnp.concatenate([parts[k].reshape(-1) for k in _SMALL])
    rows = -(-flat.shape[0] // (8 * LANES)) * 8
    return jnp.pad(flat, (0, rows * LANES - flat.shape[0])).reshape(rows, LANES)


def _unpack_small(packed, shapes):
    flat = packed.reshape(-1)
    out, off = {}, 0
    for k in _SMALL:
        n = math.prod(shapes[k])
        out[k] = flat[off:off + n].reshape(shapes[k])
        off += n
    return out


def _from_chip_columns(stacked, rows, cols):
    return stacked.reshape(N_CHIPS, rows, cols).transpose(1, 0, 2).reshape(rows, N_CHIPS * cols)


def _to_chip_columns(full, cols):
    rows = full.shape[0]
    return full.reshape(rows, N_CHIPS, cols).transpose(1, 0, 2).reshape(N_CHIPS, rows * cols // LANES, LANES)


def kernel(x, norm_mix_g, w_in, conv_w, conv_b, dt_bias, a_log, d_skip, ssm_norm_g, v_norm_g, v_norm_b, w_spatial, b_spatial, b_gates, w_proj_a, w_proj_b, w_out, norm_mlp_g, w_mlp_up, w_mlp_down, norm_final_g, loss_target, m_norm_mix_g, m_w_in, m_conv_w, m_conv_b, m_dt_bias, m_a_log, m_d_skip, m_ssm_norm_g, m_v_norm_g, m_v_norm_b, m_w_spatial, m_b_spatial, m_b_gates, m_w_proj_a, m_w_proj_b, m_w_out, m_norm_mlp_g, m_w_mlp_up, m_w_mlp_down, m_norm_final_g, v_norm_mix_g, v_w_in, v_conv_w, v_conv_b, v_dt_bias, v_a_log, v_d_skip, v_ssm_norm_g, v_v_norm_g, v_v_norm_b, v_w_spatial, v_b_spatial, v_b_gates, v_w_proj_a, v_w_proj_b, v_w_out, v_norm_mlp_g, v_w_mlp_up, v_w_mlp_down, v_norm_final_g):
    given = dict(locals())
    names = ("norm_mix_g", "w_in", "conv_w", "conv_b", "dt_bias", "a_log", "d_skip", "ssm_norm_g", "v_norm_g",
             "v_norm_b", "w_spatial", "b_spatial", "b_gates", "w_proj_a", "w_proj_b", "w_out", "norm_mlp_g",
             "w_mlp_up", "w_mlp_down", "norm_final_g")
    xi, yi, ci = lax.axis_index("x"), lax.axis_index("y"), lax.axis_index("c")
    me_chip = (2 * xi + yi).astype(jnp.int32)

    dense_f32 = jnp.concatenate([given[k][0].reshape(-1, LANES) for k in _DENSE])
    dense_b16 = _cast_bf16(dense_f32, name="cast_weights")
    conv_shard = conv_w.reshape(CONV_W, CONV_DIM // N_CHIPS)
    conv_bits = lax.bitcast_convert_type(conv_shard.reshape(-1, LANES), bf16).reshape(_CONV_ROWS, LANES)
    shard = jnp.concatenate([dense_b16, conv_bits]).reshape(2, (_DENSE_TOTAL + _CONV_ROWS) // 2, LANES)
    gathered = lax.dynamic_update_slice(_gather_shards(shard, name="gather_weights"), shard[None], (me_chip, 0, 0, 0))
    gathered = gathered.reshape(N_CHIPS, _DENSE_TOTAL + _CONV_ROWS, LANES)

    def rows_of(k):
        return gathered[:, _DENSE_OFF[k]:_DENSE_OFF[k] + _DENSE_ROWS[k]]

    w_in_full = _from_chip_columns(rows_of("w_in"), D_MODEL, _IN_SHARD)
    o_dt, o_gate = 2 * GMLP_WIDTH + D_INNER + CONV_DIM, 2 * GMLP_WIDTH + D_INNER + CONV_DIM + N_HEADS
    wts = {
        "uv": w_in_full[:, :2 * GMLP_WIDTH], "z": w_in_full[:, 2 * GMLP_WIDTH:2 * GMLP_WIDTH + D_INNER],
        "xbc": w_in_full[:, 2 * GMLP_WIDTH + D_INNER:o_dt], "dt": _pad_lanes(w_in_full[:, o_dt:o_gate]),
        "gate": w_in_full[:, o_gate:],
        "pa": rows_of("w_proj_a").reshape(GMLP_WIDTH, D_MODEL), "pb": rows_of("w_proj_b").reshape(D_INNER, D_MODEL),
        "out": rows_of("w_out").reshape(D_MODEL, D_MODEL),
        "up": _from_chip_columns(rows_of("w_mlp_up"), D_MODEL, D_FF // N_CHIPS),
        "down": rows_of("w_mlp_down").reshape(D_FF, D_MODEL),
    }
    conv_all = lax.bitcast_convert_type(
        gathered[:, _DENSE_TOTAL:].reshape(N_CHIPS, _CONV_ROWS // 2, LANES, 2), f32)
    conv_full = conv_all.reshape(N_CHIPS, CONV_W, CONV_DIM // N_CHIPS).transpose(1, 0, 2).reshape(CONV_W, CONV_DIM)

    small = {
        "norm_mix_g": norm_mix_g, "conv_w": conv_full, "conv_b": conv_b, "dt_bias": dt_bias, "a_log": a_log,
        "d_skip": d_skip, "ssm_norm_g": ssm_norm_g, "v_norm_g": v_norm_g, "v_norm_b": v_norm_b,
        "w_spatial": w_spatial[0], "b_spatial": b_spatial[0], "b_gates": b_gates, "norm_mlp_g": norm_mlp_g,
        "norm_final_g": norm_final_g.reshape(1, D_MODEL),
    }

    loss_part, grad_x, dw, dsmall = _local_grads(x[0], loss_target[0], wts, small)
    loss = lax.psum(loss_part[0, 0], ("x", "y", "c"))

    dw_in = jnp.concatenate([dw["uv"], dw["z"], dw["xbc"], dw["dt"][:, :N_HEADS], dw["gate"]], axis=1)
    per_chip = {
        "w_in": _to_chip_columns(dw_in, _IN_SHARD), "w_proj_a": dw["pa"].reshape(N_CHIPS, -1, LANES),
        "w_proj_b": dw["pb"].reshape(N_CHIPS, -1, LANES), "w_out": dw["out"].reshape(N_CHIPS, -1, LANES),
        "w_mlp_up": _to_chip_columns(dw["up"], D_FF // N_CHIPS), "w_mlp_down": dw["down"].reshape(N_CHIPS, -1, LANES),
    }
    g_all = jnp.concatenate([per_chip[k] for k in _DENSE], axis=1).reshape(N_CHIPS, 2, _DENSE_TOTAL // 2, LANES)
    c_idx = ci.astype(jnp.int32).reshape(1)
    got_pair = _rs_swap_halves(g_all, name="rs_swap_halves")
    p32, p16 = _rs_add_pair(g_all, got_pair, c_idx, name="rs_add_pair")
    got_chips = _rs_scatter_chips(p16, name="rs_scatter_chips")
    half = _rs_add_chips(p32, got_chips, me_chip.reshape(1), name="rs_add_chips")
    other_half = _rs_join_halves(half, name="rs_join_halves")
    g_shard = jnp.where(ci == 0, jnp.concatenate([half, other_half]), jnp.concatenate([other_half, half]))

    small_shapes = {k: dsmall[k].shape for k in _SMALL}
    red = _unpack_small(_all_reduce_small(_pack_small(dsmall), name="all_reduce_small"), small_shapes)
    conv_cols = CONV_DIM // N_CHIPS
    red["conv_w"] = lax.dynamic_slice_in_dim(red["conv_w"], me_chip * conv_cols, conv_cols, axis=1)

    grads, deltas, new_m, new_v = {}, {}, {}, {}
    for k in _DENSE:
        shp = _DENSE_SHARD_SHAPES[k]
        g2 = g_shard[_DENSE_OFF[k]:_DENSE_OFF[k] + _DENSE_ROWS[k]].reshape(shp)
        dlt, m2, v2 = _adamw(given[k][0], g2, given["m_" + k][0], given["v_" + k][0], name="adamw_" + k)
        grads[k], deltas[k], new_m[k], new_v[k] = g2, dlt, m2, v2
    adam_shapes = dict(small_shapes)
    adam_shapes["conv_w"] = (CONV_W, conv_cols)

    def small_pack_of(prefix):
        return _pack_small({k: given[prefix + k].reshape(adam_shapes[k]) for k in _SMALL})

    dlt_s, m_s, v_s = _adamw(small_pack_of(""), _pack_small(red), small_pack_of("m_"), small_pack_of("v_"),
                             name="adamw_small")
    for dst, packed in ((deltas, dlt_s), (new_m, m_s), (new_v, v_s)):
        dst.update(_unpack_small(packed, adam_shapes))
    grads.update(red)

    def shaped(dct):
        return [dct[k].reshape(given[k].shape) for k in names]

    return (loss, grad_x[None], *shaped(grads), *shaped(deltas), *shaped(new_m), *shaped(new_v))
```

```python
import functools
import math

import jax
import jax.numpy as jnp
from jax import lax
from jax.experimental import pallas as pl
from jax.experimental.pallas import tpu as pltpu

f32 = jnp.float32
bf16 = jnp.bfloat16

D_MODEL = 1024
CHUNK = 128
GMLP_WIDTH = 1024
GMLP_GROUPS = 8
D_INNER = 2048
HEAD_DIM = 64
N_HEADS = 32
N_GROUPS = 8
HEADS_PER_GROUP = 4
GROUP_W = HEADS_PER_GROUP * HEAD_DIM
D_STATE = 128
CONV_W = 4
CONV_DIM = 4096
D_FF = 4096
IN_PROJ = 10272
NORM_EPS = 1e-6
N_CHIPS = 4
N_DEV = 8
LANES = 128

ADAM_LR = 0.001
ADAM_B1 = 0.9
ADAM_B2 = 0.999
ADAM_EPS = 1e-08
ADAM_WD = 0.01
ADAM_STEP = 10

MESH = pl.DeviceIdType.MESH
_NT = (((1,), (1,)), ((), ()))
_NN = (((1,), (0,)), ((), ()))
_TN = (((0,), (0,)), ((), ()))
_MB = 2 ** 20


def _params(sem, vmem_mb=48):
    return pltpu.CompilerParams(dimension_semantics=sem, vmem_limit_bytes=vmem_mb * _MB)


def _dot(a, b, dims=_NN):
    return lax.dot_general(a.astype(bf16), b.astype(bf16), dims, preferred_element_type=f32)


def _dot32(a, b):
    return jnp.dot(a, b, preferred_element_type=f32, precision=lax.Precision.HIGHEST)


def _sigmoid(x):
    return 1.0 / (1.0 + jnp.exp(-x))


def _sum_all(a):
    return jnp.sum(jnp.sum(a, axis=1, keepdims=True), axis=0, keepdims=True)


def _iota(shape, dim):
    return lax.broadcasted_iota(jnp.int32, shape, dim)


def _matmul(a, b, *, nt=False, tm, tn, tk, out_dtypes, epilogue=None, extras=(), name):
    m, k_dim = a.shape
    n = b.shape[0] if nt else b.shape[1]
    nk = k_dim // tk
    ne, no = len(extras), len(out_dtypes)
    dims = _NT if nt else _NN

    def body(*refs):
        a_ref, b_ref = refs[0], refs[1]
        ex = refs[2:2 + ne]
        outs = refs[2 + ne:2 + ne + no]

        def finish(acc):
            vals = epilogue(acc, *[e[...] for e in ex]) if epilogue is not None else (acc,)
            for o, v in zip(outs, vals):
                o[...] = v.astype(o.dtype)

        part = lax.dot_general(a_ref[...], b_ref[...], dims, preferred_element_type=f32)
        if nk == 1:
            finish(part)
        else:
            acc_ref = refs[-1]
            kk = pl.program_id(2)

            @pl.when(kk == 0)
            def _():
                acc_ref[...] = part

            @pl.when(kk > 0)
            def _():
                acc_ref[...] += part

            @pl.when(kk == nk - 1)
            def _():
                finish(acc_ref[...])

    b_spec = pl.BlockSpec((tn, tk), lambda i, j, k: (j, k)) if nt else pl.BlockSpec((tk, tn), lambda i, j, k: (k, j))
    tile = pl.BlockSpec((tm, tn), lambda i, j, k: (i, j))
    outs = pl.pallas_call(
        body, name=name, grid=(m // tm, n // tn, nk),
        in_specs=[pl.BlockSpec((tm, tk), lambda i, j, k: (i, k)), b_spec] + [tile] * ne,
        out_specs=[tile] * no,
        out_shape=[jax.ShapeDtypeStruct((m, n), dt) for dt in out_dtypes],
        scratch_shapes=[pltpu.VMEM((tm, tn), f32)] if nk > 1 else [],
        compiler_params=_params(("parallel", "parallel", "arbitrary")),
    )(a, b, *extras)
    return outs if no > 1 else outs[0]


def _matmul_nt_sum(pairs, extra, *, tm, tk, name):
    m = pairs[0][0].shape[0]
    n = pairs[0][1].shape[0]
    nblk = [a.shape[1] // tk for a, _ in pairs]
    starts = [sum(nblk[:p]) for p in range(len(pairs))]
    nk = sum(nblk)
    npairs = len(pairs)

    def body(*refs):
        extra_ref, o_ref, acc_ref = refs[2 * npairs], refs[2 * npairs + 1], refs[2 * npairs + 2]
        kk = pl.program_id(1)

        @pl.when(kk == 0)
        def _():
            acc_ref[...] = extra_ref[...]

        for p in range(npairs):
            @pl.when((kk >= starts[p]) & (kk < starts[p] + nblk[p]))
            def _(p=p):
                acc_ref[...] += lax.dot_general(refs[2 * p][...], refs[2 * p + 1][...], _NT, preferred_element_type=f32)

        @pl.when(kk == nk - 1)
        def _():
            o_ref[...] = acc_ref[...]

    in_specs, args = [], []
    for p, (a, b) in enumerate(pairs):
        def kblock(k, s=starts[p], nb=nblk[p]):
            return jnp.clip(k - s, 0, nb - 1)
        in_specs.append(pl.BlockSpec((tm, tk), lambda i, k, kb=kblock: (i, kb(k))))
        in_specs.append(pl.BlockSpec((n, tk), lambda i, k, kb=kblock: (0, kb(k))))
        args += [a, b]
    tile = pl.BlockSpec((tm, n), lambda i, k: (i, 0))
    return pl.pallas_call(
        body, name=name, grid=(m // tm, nk), in_specs=in_specs + [tile], out_specs=tile,
        out_shape=jax.ShapeDtypeStruct((m, n), f32), scratch_shapes=[pltpu.VMEM((tm, n), f32)],
        compiler_params=_params(("parallel", "arbitrary")),
    )(*args, extra)


def _matmul_tn(a, b, *, tka, tn, tt, name):
    t, ka = a.shape
    n = b.shape[1]

    def body(a_ref, b_ref, o_ref):
        part = lax.dot_general(a_ref[...], b_ref[...], _TN, preferred_element_type=f32)
        kk = pl.program_id(2)

        @pl.when(kk == 0)
        def _():
            o_ref[...] = part

        @pl.when(kk > 0)
        def _():
            o_ref[...] += part

    return pl.pallas_call(
        body, name=name, grid=(ka // tka, n // tn, t // tt),
        in_specs=[pl.BlockSpec((tt, tka), lambda i, j, k: (k, i)), pl.BlockSpec((tt, tn), lambda i, j, k: (k, j))],
        out_specs=pl.BlockSpec((tka, tn), lambda i, j, k: (i, j)),
        out_shape=jax.ShapeDtypeStruct((ka, n), f32),
        compiler_params=_params(("parallel", "parallel", "arbitrary")),
    )(a, b)


def _row_tile(t):
    return min(t, 512)


def _rms_fwd(x, g, *, name):
    t, d = x.shape
    tr = _row_tile(t)

    def body(x_ref, g_ref, h_ref):
        xv = x_ref[...]
        r = lax.rsqrt(jnp.mean(xv * xv, axis=1, keepdims=True) + NORM_EPS)
        h_ref[...] = (xv * r * g_ref[...]).astype(bf16)

    return pl.pallas_call(
        body, name=name, grid=(t // tr,),
        in_specs=[pl.BlockSpec((tr, d), lambda i: (i, 0)), pl.BlockSpec((1, d), lambda i: (0, 0))],
        out_specs=pl.BlockSpec((tr, d), lambda i: (i, 0)),
        out_shape=jax.ShapeDtypeStruct((t, d), bf16),
        compiler_params=_params(("parallel",)),
    )(x, g)


def _rms_bwd(xin, g, dh, dres, *, want_bf16, name):
    t, d = xin.shape
    tr = _row_tile(t)

    def body(x_ref, g_ref, dh_ref, dres_ref, dx_ref, *rest):
        dg_ref = rest[-1]
        xv = x_ref[...]
        r = lax.rsqrt(jnp.mean(xv * xv, axis=1, keepdims=True) + NORM_EPS)
        xn = xv * r
        dhv = dh_ref[...]
        dxn = dhv * g_ref[...]
        dx = dres_ref[...] + r * (dxn - xn * jnp.mean(dxn * xn, axis=1, keepdims=True))
        dx_ref[...] = dx
        if want_bf16:
            rest[0][...] = dx.astype(bf16)
        part = jnp.sum(dhv * xn, axis=0, keepdims=True)

        @pl.when(pl.program_id(0) == 0)
        def _():
            dg_ref[...] = part

        @pl.when(pl.program_id(0) > 0)
        def _():
            dg_ref[...] += part

    row = pl.BlockSpec((tr, d), lambda i: (i, 0))
    vec = pl.BlockSpec((1, d), lambda i: (0, 0))
    out_shape = [jax.ShapeDtypeStruct((t, d), f32)] + ([jax.ShapeDtypeStruct((t, d), bf16)] if want_bf16 else []) \
        + [jax.ShapeDtypeStruct((1, d), f32)]
    return pl.pallas_call(
        body, name=name, grid=(t // tr,),
        in_specs=[row, vec, row, row],
        out_specs=[row] + ([row] if want_bf16 else []) + [vec],
        out_shape=out_shape,
        compiler_params=_params(("arbitrary",)),
    )(xin, g, dh, dres)


def _loss_head(x2, tgt, g, *, name):
    t, d = x2.shape
    tr = _row_tile(t)

    def body(x_ref, t_ref, g_ref, dx_ref, dxb_ref, dg_ref, loss_ref):
        xv = x_ref[...]
        gv = g_ref[...]
        r = lax.rsqrt(jnp.mean(xv * xv, axis=1, keepdims=True) + NORM_EPS)
        xn = xv * r
        e = xn * gv - t_ref[...]
        lpart = jnp.zeros((1, LANES), f32) + 0.5 * _sum_all(jnp.mean(e * e, axis=1, keepdims=True))
        dy = e * (1.0 / d)
        dxn = dy * gv
        dx = r * (dxn - xn * jnp.mean(dxn * xn, axis=1, keepdims=True))
        dx_ref[...] = dx
        dxb_ref[...] = dx.astype(bf16)
        gpart = jnp.sum(dy * xn, axis=0, keepdims=True)

        @pl.when(pl.program_id(0) == 0)
        def _():
            dg_ref[...] = gpart
            loss_ref[...] = lpart

        @pl.when(pl.program_id(0) > 0)
        def _():
            dg_ref[...] += gpart
            loss_ref[...] += lpart

    row = pl.BlockSpec((tr, d), lambda i: (i, 0))
    vec = pl.BlockSpec((1, d), lambda i: (0, 0))
    return pl.pallas_call(
        body, name=name, grid=(t // tr,),
        in_specs=[row, row, vec],
        out_specs=[row, row, vec, pl.BlockSpec((1, LANES), lambda i: (0, 0))],
        out_shape=[jax.ShapeDtypeStruct((t, d), f32), jax.ShapeDtypeStruct((t, d), bf16),
                   jax.ShapeDtypeStruct((1, d), f32), jax.ShapeDtypeStruct((1, LANES), f32)],
        compiler_params=_params(("arbitrary",)),
    )(x2, tgt, g)


def _merge_fwd(pa, pb, gl, bg, *, name):
    t, d = pa.shape
    tr = _row_tile(t)

    def body(pa_ref, pb_ref, gla_ref, glb_ref, bga_ref, bgb_ref, o_ref):
        ga = _sigmoid(gla_ref[...] + bga_ref[...])
        gb = _sigmoid(glb_ref[...] + bgb_ref[...])
        o_ref[...] = (ga * pa_ref[...] + gb * pb_ref[...]).astype(bf16)

    row = pl.BlockSpec((tr, d), lambda i: (i, 0))
    return pl.pallas_call(
        body, name=name, grid=(t // tr,),
        in_specs=[row, row, row, pl.BlockSpec((tr, d), lambda i: (i, 1)),
                  pl.BlockSpec((1, d), lambda i: (0, 0)), pl.BlockSpec((1, d), lambda i: (0, 1))],
        out_specs=row,
        out_shape=jax.ShapeDtypeStruct((t, d), bf16),
        compiler_params=_params(("parallel",)),
    )(pa, pb, gl, gl, bg, bg)


def _merge_bwd(dm, pa, pb, gl, bg, *, name):
    t, d = pa.shape
    tr = _row_tile(t)

    def body(dm_ref, pa_ref, pb_ref, gla_ref, glb_ref, bga_ref, bgb_ref, dpa_ref, dpb_ref, dgl_ref, dbg_ref):
        dmv = dm_ref[...]
        ga = _sigmoid(gla_ref[...] + bga_ref[...])
        gb = _sigmoid(glb_ref[...] + bgb_ref[...])
        dpa_ref[...] = (dmv * ga).astype(bf16)
        dpb_ref[...] = (dmv * gb).astype(bf16)
        dla = dmv * pa_ref[...] * ga * (1.0 - ga)
        dlb = dmv * pb_ref[...] * gb * (1.0 - gb)
        dgl_ref[:, :d] = dla.astype(bf16)
        dgl_ref[:, d:] = dlb.astype(bf16)
        sa = jnp.sum(dla, axis=0, keepdims=True)
        sb = jnp.sum(dlb, axis=0, keepdims=True)

        @pl.when(pl.program_id(0) == 0)
        def _():
            dbg_ref[:, :d] = sa
            dbg_ref[:, d:] = sb

        @pl.when(pl.program_id(0) > 0)
        def _():
            dbg_ref[:, :d] += sa
            dbg_ref[:, d:] += sb

    row = pl.BlockSpec((tr, d), lambda i: (i, 0))
    return pl.pallas_call(
        body, name=name, grid=(t // tr,),
        in_specs=[row, row, row, row, pl.BlockSpec((tr, d), lambda i: (i, 1)),
                  pl.BlockSpec((1, d), lambda i: (0, 0)), pl.BlockSpec((1, d), lambda i: (0, 1))],
        out_specs=[row, row, pl.BlockSpec((tr, 2 * d), lambda i: (i, 0)), pl.BlockSpec((1, 2 * d), lambda i: (0, 0))],
        out_shape=[jax.ShapeDtypeStruct((t, d), bf16), jax.ShapeDtypeStruct((t, d), bf16),
                   jax.ShapeDtypeStruct((t, 2 * d), bf16), jax.ShapeDtypeStruct((1, 2 * d), f32)],
        compiler_params=_params(("arbitrary",)),
    )(dm, pa, pb, gl, gl, bg, bg)


_INV_SQRT2 = 1.0 / math.sqrt(2.0)
_INV_SQRT2PI = 1.0 / math.sqrt(2.0 * math.pi)


def _gelu(x):
    return 0.5 * x * (1.0 + lax.erf(x * _INV_SQRT2))


def _gelu_grad(x):
    return 0.5 * (1.0 + lax.erf(x * _INV_SQRT2)) + x * jnp.exp(-0.5 * x * x) * _INV_SQRT2PI


def _gmlp_common(uv, vg, vb):
    zz = _gelu(uv)
    u = zz[:, :GMLP_WIDTH]
    v = zz[:, GMLP_WIDTH:]
    mu = jnp.mean(v, axis=1, keepdims=True)
    vc = v - mu
    rstd = lax.rsqrt(jnp.mean(vc * vc, axis=1, keepdims=True) + NORM_EPS)
    vhat = vc * rstd
    vn = vhat * vg + vb
    return u, vhat, rstd, vn


def _gmlp_fwd(uv, vg, vb, wsp, bsp_t, *, name):
    t = uv.shape[0]
    nc = t // CHUNK

    def body(uv_ref, vg_ref, vb_ref, w_ref, b_ref, y_ref):
        u, _, _, vn = _gmlp_common(uv_ref[...], vg_ref[...], vb_ref[...])
        tril = _iota((CHUNK, CHUNK), 0) >= _iota((CHUNK, CHUNK), 1)
        bt = b_ref[...]
        for g in range(GMLP_GROUPS):
            sl = slice(g * CHUNK, (g + 1) * CHUNK)
            w = jnp.where(tril, w_ref[g], 0.0)
            s = _dot(w, vn[:, sl]) + bt[:, g:g + 1]
            y_ref[:, sl] = (u[:, sl] * s).astype(bf16)

    return pl.pallas_call(
        body, name=name, grid=(nc,),
        in_specs=[pl.BlockSpec((CHUNK, 2 * GMLP_WIDTH), lambda c: (c, 0)),
                  pl.BlockSpec((1, GMLP_WIDTH), lambda c: (0, 0)), pl.BlockSpec((1, GMLP_WIDTH), lambda c: (0, 0)),
                  pl.BlockSpec((GMLP_GROUPS, CHUNK, CHUNK), lambda c: (0, 0, 0)),
                  pl.BlockSpec((CHUNK, LANES), lambda c: (0, 0))],
        out_specs=pl.BlockSpec((CHUNK, GMLP_WIDTH), lambda c: (c, 0)),
        out_shape=jax.ShapeDtypeStruct((t, GMLP_WIDTH), bf16),
        compiler_params=_params(("parallel",)),
    )(uv, vg, vb, wsp, bsp_t)


def _gmlp_bwd(uv, dya, vg, vb, wsp, bsp_t, *, name):
    t = uv.shape[0]
    nc = t // CHUNK

    def body(uv_ref, dy_ref, vg_ref, vb_ref, w_ref, b_ref, duv_ref, dw_ref, db_ref, dvg_ref, dvb_ref):
        first = pl.program_id(0) == 0

        @pl.when(first)
        def _():
            dw_ref[...] = jnp.zeros_like(dw_ref)
            db_ref[...] = jnp.zeros_like(db_ref)
            dvg_ref[...] = jnp.zeros_like(dvg_ref)
            dvb_ref[...] = jnp.zeros_like(dvb_ref)

        uvv = uv_ref[...]
        vgv = vg_ref[...]
        u, vhat, rstd, vn = _gmlp_common(uvv, vgv, vb_ref[...])
        dy = dy_ref[...]
        tril = _iota((CHUNK, CHUNK), 0) >= _iota((CHUNK, CHUNK), 1)
        lane = _iota((CHUNK, LANES), 1)
        bt = b_ref[...]
        ds_all = dy * u
        dbacc = jnp.zeros((CHUNK, LANES), f32)
        dvh_parts = []
        for g in range(GMLP_GROUPS):
            sl = slice(g * CHUNK, (g + 1) * CHUNK)
            w = jnp.where(tril, w_ref[g], 0.0)
            vng = vn[:, sl]
            s = _dot(w, vng) + bt[:, g:g + 1]
            ds = ds_all[:, sl]
            duv_ref[:, sl] = (dy[:, sl] * s * _gelu_grad(uvv[:, sl])).astype(bf16)
            dw_ref[g] += jnp.where(tril, _dot(ds, vng, _NT), 0.0)
            dbacc = dbacc + jnp.where(lane == g, jnp.sum(ds, axis=1, keepdims=True), 0.0)
            dvn = _dot(w, ds, _TN)
            vh = vhat[:, sl]
            dvg_ref[:, sl] += jnp.sum(dvn * vh, axis=0, keepdims=True)
            dvb_ref[:, sl] += jnp.sum(dvn, axis=0, keepdims=True)
            dvh_parts.append(dvn * vgv[:, sl])
        db_ref[...] += dbacc
        dvhat = jnp.concatenate(dvh_parts, axis=1)
        m1 = jnp.mean(dvhat, axis=1, keepdims=True)
        m2 = jnp.mean(dvhat * vhat, axis=1, keepdims=True)
        dv = rstd * (dvhat - m1 - vhat * m2)
        duv_ref[:, GMLP_WIDTH:] = (dv * _gelu_grad(uvv[:, GMLP_WIDTH:])).astype(bf16)

    vec = pl.BlockSpec((1, GMLP_WIDTH), lambda c: (0, 0))
    return pl.pallas_call(
        body, name=name, grid=(nc,),
        in_specs=[pl.BlockSpec((CHUNK, 2 * GMLP_WIDTH), lambda c: (c, 0)),
                  pl.BlockSpec((CHUNK, GMLP_WIDTH), lambda c: (c, 0)), vec, vec,
                  pl.BlockSpec((GMLP_GROUPS, CHUNK, CHUNK), lambda c: (0, 0, 0)),
                  pl.BlockSpec((CHUNK, LANES), lambda c: (0, 0))],
        out_specs=[pl.BlockSpec((CHUNK, 2 * GMLP_WIDTH), lambda c: (c, 0)),
                   pl.BlockSpec((GMLP_GROUPS, CHUNK, CHUNK), lambda c: (0, 0, 0)),
                   pl.BlockSpec((CHUNK, LANES), lambda c: (0, 0)), vec, vec],
        out_shape=[jax.ShapeDtypeStruct((t, 2 * GMLP_WIDTH), bf16),
                   jax.ShapeDtypeStruct((GMLP_GROUPS, CHUNK, CHUNK), f32),
                   jax.ShapeDtypeStruct((CHUNK, LANES), f32),
                   jax.ShapeDtypeStruct((1, GMLP_WIDTH), f32), jax.ShapeDtypeStruct((1, GMLP_WIDTH), f32)],
        compiler_params=_params(("arbitrary",)),
    )(uv, dya, vg, vb, wsp, bsp_t)


_CONV_COLS = 512
_XS0, _B0, _C0 = 0, D_INNER, D_INNER + N_GROUPS * D_STATE


_TAIL = 8


def _conv_silu(cur_ref, tail_ref, w_ref, b_ref, has_prev, xc_ref, cv_ref):
    row = _iota((_TAIL, _CONV_COLS), 0)
    for j in range(CONV_DIM // _CONV_COLS):
        sl = slice(j * _CONV_COLS, (j + 1) * _CONV_COLS)
        cur = cur_ref[:, sl]
        tail = jnp.where(has_prev, tail_ref[:, sl], 0.0)
        acc = cur * w_ref[CONV_W - 1:CONV_W, sl] + b_ref[:, sl]
        for s in range(1, CONV_W):
            rolled = pltpu.roll(cur, s, 0)
            top = jnp.where(row >= s, rolled[:_TAIL], pltpu.roll(tail, s, 0))
            sh = jnp.concatenate([top, rolled[_TAIL:]], axis=0)
            acc = acc + sh * w_ref[CONV_W - 1 - s:CONV_W - s, sl]
        if cv_ref is not None:
            cv_ref[:, sl] = acc
        xc_ref[:, sl] = acc * _sigmoid(acc)


def _col_bcast(mat, h):
    return jnp.broadcast_to(mat[:, h:h + 1], (CHUNK, LANES))


def _head_expand(cols):
    lo = _iota((CHUNK, LANES), 1) < HEAD_DIM
    return jnp.concatenate([jnp.where(lo, cols[2 * j], cols[2 * j + 1]) for j in range(N_HEADS // 2)], axis=1)


def _ssd_chunk_scalars(dtr, dtb, alog):
    xdt_pre = dtr + dtb
    dtv = jnp.maximum(xdt_pre, 0.0) + jnp.log(1.0 + jnp.exp(-jnp.abs(xdt_pre)))
    a = -jnp.exp(alog)
    ltri = (_iota((CHUNK, CHUNK), 0) >= _iota((CHUNK, CHUNK), 1)).astype(f32)
    cs = _dot32(ltri, dtv * a)
    csb = [_col_bcast(cs, h) for h in range(N_HEADS)]
    cs_x = _head_expand(csb)
    dt_x = _head_expand([_col_bcast(dtv, h) for h in range(N_HEADS)])
    cl_x = cs_x[CHUNK - 1:CHUNK, :]
    return dict(xdt_pre=xdt_pre, dtv=dtv, a=a, cs=cs, cs_t=cs.T, csb=csb, dt_x=dt_x, e_x=jnp.exp(cs_x),
                dec_x=jnp.exp(cl_x - cs_x), dk_x=jnp.exp(cl_x))


def _head_masks():
    lane = _iota((CHUNK, GROUP_W), 1)
    return [(lane >= r * HEAD_DIM) & (lane < (r + 1) * HEAD_DIM) for r in range(HEADS_PER_GROUP)]


def _stack_heads(a, masks):
    return jnp.concatenate([jnp.where(m, a, 0.0) for m in masks], axis=0).astype(bf16)


def _seg_sum(a, seg):
    hi = a.astype(jnp.bfloat16)
    lo = (a - hi.astype(f32)).astype(jnp.bfloat16)
    return (lax.dot_general(hi, seg, _NN, preferred_element_type=f32)
            + lax.dot_general(lo, seg, _NN, preferred_element_type=f32))


def _head_seg_matrix():
    return (_iota((D_INNER, LANES), 0) // HEAD_DIM == _iota((D_INNER, LANES), 1)).astype(jnp.bfloat16)


def _ssd_fwd(xbc, z, dtr, cw, cb, dtb, alog, dsk_x, gs, *, ride=None, name):
    t = xbc.shape[0]
    nc = t // CHUNK
    tiles = CHUNK // _TAIL

    def body(*refs):
        cur_ref, tail_ref, z_ref, dtr_ref, cw_ref, cb_ref, dtb_ref, alog_ref, dsk_ref, gs_ref = refs[:10]
        if ride is None:
            yb_ref, hp_ref, state_ref, xc_ref = refs[10:]
        else:
            ride_ref, yb_ref, hp_ref, got_ref, state_ref, xc_ref, send_sems, recv_sems = refs[10:]
        c = pl.program_id(0)
        if ride is not None:
            start, relay, finish = _gather_protocol(ride_ref, got_ref, send_sems, recv_sems)
            pl.when(c == 0)(start)
            pl.when(c == nc // 2)(relay)

        @pl.when(c == 0)
        def _():
            state_ref[...] = jnp.zeros_like(state_ref)

        _conv_silu(cur_ref, tail_ref, cw_ref, cb_ref, c > 0, xc_ref, None)
        sc = _ssd_chunk_scalars(dtr_ref[...], dtb_ref[...], alog_ref[...])
        tril = _iota((CHUNK, CHUNK), 0) >= _iota((CHUNK, CHUNK), 1)
        masks = _head_masks()
        hp_ref[0] = state_ref[...]
        for g in range(N_GROUPS):
            gsl = slice(g * GROUP_W, (g + 1) * GROUP_W)
            xs_g = xc_ref[:, gsl]
            bg = xc_ref[:, _B0 + g * D_STATE:_B0 + (g + 1) * D_STATE]
            cg = xc_ref[:, _C0 + g * D_STATE:_C0 + (g + 1) * D_STATE]
            xdt_g = xs_g * sc["dt_x"][:, gsl]
            cbm = _dot(cg, bg, _NT)
            mw = jnp.concatenate(
                [cbm * jnp.exp(jnp.where(tril, sc["csb"][h] - sc["cs_t"][h:h + 1, :], -1e30))
                 for h in range(g * HEADS_PER_GROUP, (g + 1) * HEADS_PER_GROUP)], axis=1)
            ht_g = state_ref[:, gsl]
            y_g = _dot(mw, _stack_heads(xdt_g, masks)) + sc["e_x"][:, gsl] * _dot(cg, ht_g) + dsk_ref[:, gsl] * xs_g
            state_ref[:, gsl] = ht_g * sc["dk_x"][:, gsl] + _dot(bg, xdt_g * sc["dec_x"][:, gsl], _TN)
            zg = z_ref[:, gsl]
            yg = y_g * zg * _sigmoid(zg)
            rs = lax.rsqrt(jnp.mean(yg * yg, axis=1, keepdims=True) + NORM_EPS)
            yb_ref[:, gsl] = (yg * rs * gs_ref[:, gsl]).astype(bf16)
        if ride is not None:
            pl.when(c == nc - 1)(finish)

    def chunk(w):
        return pl.BlockSpec((CHUNK, w), lambda c: (c, 0))

    def const(shape):
        return pl.BlockSpec(shape, lambda c: (0,) * len(shape))

    riding = ride is not None
    return pl.pallas_call(
        body, name=name, grid=(nc,),
        in_specs=[chunk(CONV_DIM), pl.BlockSpec((_TAIL, CONV_DIM), lambda c: (jnp.maximum(c * tiles - 1, 0), 0)),
                  chunk(D_INNER), chunk(LANES), const((CONV_W, CONV_DIM)), const((1, CONV_DIM)),
                  const((1, LANES)), const((1, LANES)), const((1, D_INNER)), const((1, D_INNER))] + [_ANY] * riding,
        out_specs=[chunk(D_INNER), pl.BlockSpec((1, D_STATE, D_INNER), lambda c: (c, 0, 0))] + [_ANY] * riding,
        out_shape=[jax.ShapeDtypeStruct((t, D_INNER), bf16), jax.ShapeDtypeStruct((nc, D_STATE, D_INNER), f32)]
        + ([jax.ShapeDtypeStruct((N_CHIPS,) + ride.shape, ride.dtype)] if riding else []),
        scratch_shapes=[pltpu.VMEM((D_STATE, D_INNER), f32), pltpu.VMEM((CHUNK, CONV_DIM), f32)]
        + (list(_GATHER_SCRATCH) if riding else []),
        compiler_params=_params(("arbitrary",)),
    )(xbc, xbc, z, dtr, cw, cb, dtb, alog, dsk_x, gs, *([ride] if riding else []))


def _ssd_bwd(xbc, z, dtr, hprev, dyb, cw, cb, dtb, alog, dsk_x, gs, seg, *, ride=None, name):
    t = xbc.shape[0]
    nc = t // CHUNK
    tiles = CHUNK // _TAIL

    def body(*refs):
        (cur_ref, tail_ref, z_ref, dtr_ref, hp_ref, dyb_ref, cw_ref, cb_ref, dtb_ref, alog_ref, dsk_ref, gs_ref,
         seg_ref) = refs[:13]
        rest = refs[13:]
        if ride is not None:
            ride_ref, got_ref, send_sems, recv_sems = rest[0], rest[10], rest[-2], rest[-1]
            rest = rest[1:10] + rest[11:-2]
        (dz_ref, dxbc_ref, ddt_ref, dcw_ref, dcb_ref, ddtb_ref, dalog_ref, ddsk_ref, dgs_ref,
         dh_ref, dcnext_ref, xc_ref, cv_ref, dxc_ref, x13_ref, x2_ref, rows_ref) = rest
        i = pl.program_id(0)
        cc = nc - 1 - i
        if ride is not None:
            start, finish = _scatter_protocol(ride_ref, got_ref, send_sems, recv_sems)
            pl.when(i == 0)(start)

        @pl.when(i == 0)
        def _():
            for ref in (dh_ref, dcnext_ref, dcw_ref, dcb_ref, ddtb_ref, dalog_ref, ddsk_ref, dgs_ref, rows_ref):
                ref[...] = jnp.zeros_like(ref)

        _conv_silu(cur_ref, tail_ref, cw_ref, cb_ref, cc > 0, xc_ref, cv_ref)
        sc = _ssd_chunk_scalars(dtr_ref[...], dtb_ref[...], alog_ref[...])
        tril = _iota((CHUNK, CHUNK), 0) >= _iota((CHUNK, CHUNK), 1)
        triu = _iota((CHUNK, CHUNK), 0) <= _iota((CHUNK, CHUNK), 1)
        masks = _head_masks()
        rowh = _iota((N_HEADS, CHUNK), 0)
        dcs_t = jnp.zeros((N_HEADS, CHUNK), f32)
        for g in range(N_GROUPS):
            gsl = slice(g * GROUP_W, (g + 1) * GROUP_W)
            xs_g = xc_ref[:, gsl]
            bg = xc_ref[:, _B0 + g * D_STATE:_B0 + (g + 1) * D_STATE]
            cg = xc_ref[:, _C0 + g * D_STATE:_C0 + (g + 1) * D_STATE]
            dt_g, e_g, dec_g, dk_g = sc["dt_x"][:, gsl], sc["e_x"][:, gsl], sc["dec_x"][:, gsl], sc["dk_x"][:, gsl]
            dsk_g = dsk_ref[:, gsl]
            xdt_g = xs_g * dt_g
            xdt_stack = _stack_heads(xdt_g, masks)
            cbm = _dot(cg, bg, _NT)
            cbt = _dot(bg, cg, _NT)
            heads = range(g * HEADS_PER_GROUP, (g + 1) * HEADS_PER_GROUP)
            lmats = [jnp.exp(jnp.where(tril, sc["csb"][h] - sc["cs_t"][h:h + 1, :], -1e30)) for h in heads]
            mw = jnp.concatenate([cbm * lm for lm in lmats], axis=1)
            mtw = jnp.concatenate(
                [cbt * jnp.exp(jnp.where(triu, sc["cs_t"][h:h + 1, :] - sc["csb"][h], -1e30)) for h in heads], axis=1)
            ht_g = hp_ref[0, :, gsl]
            dhn_g = dh_ref[:, gsl]
            yoff = e_g * _dot(cg, ht_g)
            y_g = _dot(mw, xdt_stack) + yoff + dsk_g * xs_g
            zg = z_ref[:, gsl]
            sz = _sigmoid(zg)
            silu = zg * sz
            yg = y_g * silu
            rs = lax.rsqrt(jnp.mean(yg * yg, axis=1, keepdims=True) + NORM_EPS)
            yn = yg * rs
            dyb = dyb_ref[:, gsl]
            dgs_ref[:, gsl] += jnp.sum(dyb * yn, axis=0, keepdims=True)
            dyn = dyb * gs_ref[:, gsl]
            dyg = rs * (dyn - yn * jnp.mean(dyn * yn, axis=1, keepdims=True))
            dy_g = dyg * silu
            dz_ref[:, gsl] = (dyg * y_g * (sz * (1.0 + zg * (1.0 - sz)))).astype(bf16)
            dy_stack = _stack_heads(dy_g, masks)
            dm_w = _dot(dy_g, xdt_stack, _NT)
            dmt_w = _dot(xdt_g, dy_stack, _NT)
            dxdt = _dot(mtw, dy_stack)
            dcb_acc = jnp.zeros((CHUNK, CHUNK), f32)
            for r, h in enumerate(heads):
                hs = slice(r * CHUNK, (r + 1) * CHUNK)
                dml = dm_w[:, hs] * lmats[r]
                dcb_acc = dcb_acc + dml
                col = jnp.sum(dml * cbm, axis=0, keepdims=True)
                row = jnp.sum(dmt_w[:, hs] * mtw[:, hs], axis=0, keepdims=True)
                dcs_t = dcs_t + jnp.where(rowh == h, row - col, 0.0)
            w = _dot(bg, dhn_g)
            dxdt = dxdt + dec_g * w
            decx3 = dec_g * (xdt_g * w)
            dg_g = e_g * dy_g
            d_c = _dot(dg_g, ht_g, _NT) + _dot(dcb_acc, bg)
            d_b = _dot(dcb_acc, cg, _TN) + _dot(xdt_g * dec_g, dhn_g, _NT)
            dh_ref[:, gsl] = dhn_g * dk_g + _dot(cg, dg_g, _TN)
            dxc_ref[:, gsl] = dsk_g * dy_g + dxdt * dt_g
            dxc_ref[:, _B0 + g * D_STATE:_B0 + (g + 1) * D_STATE] = d_b
            dxc_ref[:, _C0 + g * D_STATE:_C0 + (g + 1) * D_STATE] = d_c
            x13_ref[:, gsl] = dy_g * yoff - decx3
            x2_ref[:, gsl] = dxdt * xs_g
            rows_ref[0:1, gsl] = jnp.sum(dhn_g * ht_g, axis=0, keepdims=True)
            rows_ref[1:2, gsl] = jnp.sum(decx3, axis=0, keepdims=True)
            rows_ref[2:3, gsl] = jnp.sum(dy_g * xs_g, axis=0, keepdims=True)
        segm = seg_ref[...]
        r13 = _seg_sum(x13_ref[...], segm)
        r2 = _seg_sum(x2_ref[...], segm)
        small = _seg_sum(rows_ref[...], segm)
        lane = _iota((CHUNK, LANES), 1)
        rowi = _iota((CHUNK, LANES), 0)
        dcl_row = small[0:1, :] * jnp.exp(sc["cs"][CHUNK - 1:CHUNK, :]) + small[1:2, :]
        dcs = r13 + jnp.where(rowi == CHUNK - 1, dcl_row, 0.0)
        dcs_t_all = dcs.T + jnp.concatenate([dcs_t, jnp.zeros((LANES - N_HEADS, CHUNK), f32)], axis=0)
        dda = _dot32(dcs_t_all, tril.astype(f32)).T
        a = sc["a"]
        ddt_total = r2 + dda * a
        dalog_ref[...] += jnp.sum(dda * sc["dtv"], axis=0, keepdims=True) * a
        ddtr = jnp.where(lane < N_HEADS, ddt_total * _sigmoid(sc["xdt_pre"]), 0.0)
        ddtb_ref[...] += jnp.sum(ddtr, axis=0, keepdims=True)
        ddt_ref[...] = ddtr.astype(bf16)
        ddsk_ref[...] += small[2:3, :]
        row8 = _iota((_TAIL, _CONV_COLS), 0)
        for j in range(CONV_DIM // _CONV_COLS):
            sl = slice(j * _CONV_COLS, (j + 1) * _CONV_COLS)
            cvv = cv_ref[:, sl]
            sg = _sigmoid(cvv)
            dconv = dxc_ref[:, sl] * (sg * (1.0 + cvv * (1.0 - sg)))
            nxt = dcnext_ref[:, sl]
            cur = cur_ref[:, sl]
            dxin = dconv * cw_ref[CONV_W - 1:CONV_W, sl]
            dcw_ref[CONV_W - 1:CONV_W, sl] += jnp.sum(dconv * cur, axis=0, keepdims=True)
            for s in range(1, CONV_W):
                rolled = pltpu.roll(dconv, CHUNK - s, 0)
                bot = jnp.where(row8 < _TAIL - s, rolled[CHUNK - _TAIL:], pltpu.roll(nxt, _TAIL - s, 0))
                up = jnp.concatenate([rolled[:CHUNK - _TAIL], bot], axis=0)
                dxin = dxin + up * cw_ref[CONV_W - 1 - s:CONV_W - s, sl]
                dcw_ref[CONV_W - 1 - s:CONV_W - s, sl] += jnp.sum(up * cur, axis=0, keepdims=True)
            dcb_ref[:, sl] += jnp.sum(dconv, axis=0, keepdims=True)
            dxbc_ref[:, sl] = dxin.astype(bf16)
            dcnext_ref[:, sl] = dconv[:_TAIL]
        if ride is not None:
            pl.when(i == nc - 1)(finish)

    def chunk(w):
        return pl.BlockSpec((CHUNK, w), lambda i: (nc - 1 - i, 0))

    def const(shape):
        return pl.BlockSpec(shape, lambda i: (0,) * len(shape))

    riding = ride is not None
    return pl.pallas_call(
        body, name=name, grid=(nc,),
        in_specs=[chunk(CONV_DIM),
                  pl.BlockSpec((_TAIL, CONV_DIM), lambda i: (jnp.maximum((nc - 1 - i) * tiles - 1, 0), 0)),
                  chunk(D_INNER), chunk(LANES), pl.BlockSpec((1, D_STATE, D_INNER), lambda i: (nc - 1 - i, 0, 0)),
                  chunk(D_INNER), const((CONV_W, CONV_DIM)), const((1, CONV_DIM)),
                  const((1, LANES)), const((1, LANES)), const((1, D_INNER)), const((1, D_INNER)),
                  const((D_INNER, LANES))] + [_ANY] * riding,
        out_specs=[chunk(D_INNER), chunk(CONV_DIM), chunk(LANES), const((CONV_W, CONV_DIM)), const((1, CONV_DIM)),
                   const((1, LANES)), const((1, LANES)), const((1, LANES)), const((1, D_INNER))] + [_ANY] * riding,
        out_shape=[jax.ShapeDtypeStruct((t, D_INNER), bf16), jax.ShapeDtypeStruct((t, CONV_DIM), bf16),
                   jax.ShapeDtypeStruct((t, LANES), bf16), jax.ShapeDtypeStruct((CONV_W, CONV_DIM), f32),
                   jax.ShapeDtypeStruct((1, CONV_DIM), f32), jax.ShapeDtypeStruct((1, LANES), f32),
                   jax.ShapeDtypeStruct((1, LANES), f32), jax.ShapeDtypeStruct((1, LANES), f32),
                   jax.ShapeDtypeStruct((1, D_INNER), f32)]
        + ([jax.ShapeDtypeStruct((N_CHIPS - 1,) + ride.shape[1:], ride.dtype)] if riding else []),
        scratch_shapes=[pltpu.VMEM((D_STATE, D_INNER), f32), pltpu.VMEM((_TAIL, CONV_DIM), f32),
                        pltpu.VMEM((CHUNK, CONV_DIM), f32), pltpu.VMEM((CHUNK, CONV_DIM), f32),
                        pltpu.VMEM((CHUNK, CONV_DIM), f32), pltpu.VMEM((CHUNK, D_INNER), f32),
                        pltpu.VMEM((CHUNK, D_INNER), f32), pltpu.VMEM((_TAIL, D_INNER), f32)]
        + (list(_SCATTER_SCRATCH) if riding else []),
        compiler_params=_params(("arbitrary",)),
    )(xbc, xbc, z, dtr, hprev, dyb, cw, cb, dtb, alog, dsk_x, gs, seg, *([ride] if riding else []))


def _adamw(w, g, m, v, *, name):
    r, c = w.shape
    tr = r
    while tr * c * 4 > _MB and tr % 16 == 0:
        tr //= 2

    def body(w_ref, g_ref, m_ref, v_ref, d_ref, m2_ref, v2_ref):
        gv = g_ref[...]
        m2 = ADAM_B1 * m_ref[...] + (1.0 - ADAM_B1) * gv
        v2 = ADAM_B2 * v_ref[...] + (1.0 - ADAM_B2) * (gv * gv)
        m_hat = m2 / (1.0 - ADAM_B1 ** ADAM_STEP)
        v_hat = v2 / (1.0 - ADAM_B2 ** ADAM_STEP)
        d_ref[...] = -ADAM_LR * (m_hat / (jnp.sqrt(v_hat) + ADAM_EPS) + ADAM_WD * w_ref[...])
        m2_ref[...] = m2
        v2_ref[...] = v2

    blk = pl.BlockSpec((tr, c), lambda i: (i, 0))
    return pl.pallas_call(
        body, name=name, grid=(r // tr,),
        in_specs=[blk] * 4, out_specs=[blk] * 3,
        out_shape=[jax.ShapeDtypeStruct((r, c), f32)] * 3,
        compiler_params=_params(("parallel",)),
    )(w, g, m, v)


def _row_block(rows, cap=2304):
    return max(tr for tr in range(16, cap + 1, 16) if rows % tr == 0)


def _cast_bf16(a, *, name):
    r, c = a.shape
    tr = _row_block(r)

    def body(a_ref, o_ref):
        o_ref[...] = a_ref[...].astype(bf16)

    blk = pl.BlockSpec((tr, c), lambda i: (i, 0))
    return pl.pallas_call(
        body, name=name, grid=(r // tr,), in_specs=[blk], out_specs=blk,
        out_shape=jax.ShapeDtypeStruct((r, c), bf16), compiler_params=_params(("parallel",)),
    )(a)


_ANY = pl.BlockSpec(memory_space=pl.ANY)


def _place():
    x, y, c = lax.axis_index("x"), lax.axis_index("y"), lax.axis_index("c")
    other_chips = [(1 - x, y), (x, 1 - y), (1 - x, 1 - y)]
    return x, y, c, other_chips


def _gather_protocol(in_ref, out_ref, send_sems, recv_sems):
    x, y, c, chips = _place()
    me = 2 * x + y
    sibling = (x, y, 1 - c)

    def cp(k, chip, half, to, src=None):
        dst = out_ref.at[chip, half]
        return pltpu.make_async_remote_copy(
            src_ref=dst if src is None else src, dst_ref=dst, send_sem=send_sems.at[k], recv_sem=recv_sems.at[k],
            device_id=to, device_id_type=MESH)

    def sends():
        return [cp(j, me, c, (cx, cy, c), src=in_ref.at[c]) for j, (cx, cy) in enumerate(chips)]

    def relays():
        return [cp(3 + j, 2 * cx + cy, c, sibling) for j, (cx, cy) in enumerate(chips)]

    def start():
        for f in sends():
            f.start()

    def relay():
        onward = relays()
        for j, (cx, cy) in enumerate(chips):
            cp(j, 2 * cx + cy, c, sibling).wait_recv()
            onward[j].start()

    def finish():
        for j, (cx, cy) in enumerate(chips):
            cp(3 + j, 2 * cx + cy, 1 - c, sibling).wait_recv()
        for f in sends() + relays():
            f.wait_send()

    return start, relay, finish


_GATHER_SCRATCH = [pltpu.SemaphoreType.DMA((6,)), pltpu.SemaphoreType.DMA((6,))]


def _gather_shards(shard, *, name):
    _, rh, lanes = shard.shape

    def body(in_ref, out_ref, send_sems, recv_sems):
        start, relay, finish = _gather_protocol(in_ref, out_ref, send_sems, recv_sems)
        start()
        relay()
        finish()

    return pl.pallas_call(
        body, name=name, in_specs=[_ANY], out_specs=_ANY,
        out_shape=jax.ShapeDtypeStruct((N_CHIPS, 2, rh, lanes), shard.dtype),
        scratch_shapes=list(_GATHER_SCRATCH),
    )(shard)


def _scatter_protocol(p_ref, out_ref, send_sems, recv_sems):
    x, y, c, chips = _place()

    def copies():
        return [pltpu.make_async_remote_copy(
            src_ref=p_ref.at[2 * cx + cy], dst_ref=out_ref.at[j], send_sem=send_sems.at[j], recv_sem=recv_sems.at[j],
            device_id=(cx, cy, c), device_id_type=MESH) for j, (cx, cy) in enumerate(chips)]

    def start():
        for cpy in copies():
            cpy.start()

    def finish():
        for cpy in copies():
            cpy.wait()

    return start, finish


_SCATTER_SCRATCH = [pltpu.SemaphoreType.DMA((3,)), pltpu.SemaphoreType.DMA((3,))]


def _rs_swap_halves(g, *, name):
    nch, _, rh, lanes = g.shape

    def body(g_ref, out_ref, send_sems, recv_sems):
        x, y, c, _ = _place()
        copies = [pltpu.make_async_remote_copy(
            src_ref=g_ref.at[k, 1 - c], dst_ref=out_ref.at[k], send_sem=send_sems.at[k], recv_sem=recv_sems.at[k],
            device_id=(x, y, 1 - c), device_id_type=MESH) for k in range(nch)]
        for cpy in copies:
            cpy.start()
        for cpy in copies:
            cpy.wait()

    return pl.pallas_call(
        body, name=name, in_specs=[_ANY], out_specs=_ANY,
        out_shape=jax.ShapeDtypeStruct((nch, rh, lanes), g.dtype),
        scratch_shapes=[pltpu.SemaphoreType.DMA((nch,)), pltpu.SemaphoreType.DMA((nch,))],
    )(g)


def _rs_add_pair(g, got, c_idx, *, name):
    nch, _, rh, lanes = g.shape
    tr = _row_block(rh)

    def body(c_ref, g_ref, got_ref, p32_ref, p16_ref):
        s = g_ref[...] + got_ref[...]
        p32_ref[...] = s
        p16_ref[...] = s.astype(bf16)

    blk = pl.BlockSpec((None, tr, lanes), lambda k, i, c_ref: (k, i, 0))
    return pl.pallas_call(
        body, name=name,
        grid_spec=pltpu.PrefetchScalarGridSpec(
            num_scalar_prefetch=1, grid=(nch, rh // tr),
            in_specs=[pl.BlockSpec((None, None, tr, lanes), lambda k, i, c_ref: (k, c_ref[0], i, 0)), blk],
            out_specs=[blk, blk]),
        out_shape=[jax.ShapeDtypeStruct((nch, rh, lanes), f32), jax.ShapeDtypeStruct((nch, rh, lanes), bf16)],
        compiler_params=_params(("parallel", "parallel")),
    )(c_idx, g, got)


def _rs_scatter_chips(p16, *, name):
    _, rh, lanes = p16.shape

    def body(p_ref, out_ref, send_sems, recv_sems):
        start, finish = _scatter_protocol(p_ref, out_ref, send_sems, recv_sems)
        start()
        finish()

    return pl.pallas_call(
        body, name=name, in_specs=[_ANY], out_specs=_ANY,
        out_shape=jax.ShapeDtypeStruct((3, rh, lanes), p16.dtype),
        scratch_shapes=list(_SCATTER_SCRATCH),
    )(p16)


def _rs_add_chips(p32, got, me_idx, *, name):
    _, rh, lanes = p32.shape
    tr = _row_block(rh)

    def body(me_ref, p_ref, got_ref, o_ref):
        o_ref[...] = ((p_ref[...] + got_ref[0].astype(f32)) + got_ref[1].astype(f32)) + got_ref[2].astype(f32)

    return pl.pallas_call(
        body, name=name,
        grid_spec=pltpu.PrefetchScalarGridSpec(
            num_scalar_prefetch=1, grid=(rh // tr,),
            in_specs=[pl.BlockSpec((None, tr, lanes), lambda i, me_ref: (me_ref[0], i, 0)),
                      pl.BlockSpec((3, tr, lanes), lambda i, me_ref: (0, i, 0))],
            out_specs=pl.BlockSpec((tr, lanes), lambda i, me_ref: (i, 0))),
        out_shape=jax.ShapeDtypeStruct((rh, lanes), f32),
        compiler_params=_params(("parallel",)),
    )(me_idx, p32, got)


def _rs_join_halves(half, *, name):
    rh, lanes = half.shape

    def body(h_ref, out_ref, send_sem, recv_sem):
        x, y, c, _ = _place()
        cpy = pltpu.make_async_remote_copy(
            src_ref=h_ref, dst_ref=out_ref, send_sem=send_sem, recv_sem=recv_sem,
            device_id=(x, y, 1 - c), device_id_type=MESH)
        cpy.start()
        cpy.wait()

    return pl.pallas_call(
        body, name=name, in_specs=[_ANY], out_specs=_ANY,
        out_shape=jax.ShapeDtypeStruct((rh, lanes), half.dtype),
        scratch_shapes=[pltpu.SemaphoreType.DMA, pltpu.SemaphoreType.DMA],
    )(half)


def _all_reduce_small(s, *, name):
    rs, lanes = s.shape

    def body(s_ref, o_ref, buf_ref, send_sems, recv_sems):
        x, y, c, _ = _place()
        me = 4 * x + 2 * y + c
        peers = []
        for k in range(1, N_DEV):
            px = 1 - x if (k >> 2) & 1 else x
            py = 1 - y if (k >> 1) & 1 else y
            pc = 1 - c if k & 1 else c
            peers.append((px, py, pc))
        copies = [pltpu.make_async_remote_copy(
            src_ref=s_ref, dst_ref=buf_ref.at[me], send_sem=send_sems.at[k], recv_sem=recv_sems.at[k],
            device_id=peer, device_id_type=MESH) for k, peer in enumerate(peers)]
        for cpy in copies:
            cpy.start()
        buf_ref[me] = s_ref[...]
        for k, (px, py, pc) in enumerate(peers):
            pltpu.make_async_remote_copy(
                src_ref=s_ref, dst_ref=buf_ref.at[4 * px + 2 * py + pc], send_sem=send_sems.at[k],
                recv_sem=recv_sems.at[k], device_id=(px, py, pc), device_id_type=MESH).wait_recv()
        for cpy in copies:
            cpy.wait_send()
        acc = buf_ref[0]
        for d in range(1, N_DEV):
            acc = acc + buf_ref[d]
        o_ref[...] = acc

    vm = pl.BlockSpec(memory_space=pltpu.VMEM)
    return pl.pallas_call(
        body, name=name, in_specs=[vm], out_specs=vm,
        out_shape=jax.ShapeDtypeStruct((rs, lanes), f32),
        scratch_shapes=[pltpu.VMEM((N_DEV, rs, lanes), f32), pltpu.SemaphoreType.DMA((N_DEV - 1,)),
                        pltpu.SemaphoreType.DMA((N_DEV - 1,))],
        compiler_params=pltpu.CompilerParams(vmem_limit_bytes=32 * _MB),
    )(s)


def _pad_lanes(a, width=LANES):
    return jnp.pad(a, ((0, 0), (0, width - a.shape[1])))


def _local_grads(x, tgt, wts, small, *, fwd_ride=None, late_weights=None, bwd_ride=None):
    t = x.shape[0]
    tm = min(t, 1024)
    d = D_MODEL
    mm = functools.partial(_matmul, tm=tm)

    dtb = _pad_lanes(small["dt_bias"])
    alog = _pad_lanes(small["a_log"])
    dsk = jnp.repeat(small["d_skip"], HEAD_DIM, axis=1)
    bsp_t = _pad_lanes(small["b_spatial"].T)
    wsp = small["w_spatial"]

    h = _rms_fwd(x, small["norm_mix_g"], name="rms_mix")
    uv = mm(h, wts["uv"], tn=1024, tk=d, out_dtypes=[f32], name="proj_uv")
    z = mm(h, wts["z"], tn=1024, tk=d, out_dtypes=[f32], name="proj_z")
    xbc = mm(h, wts["xbc"], tn=1024, tk=d, out_dtypes=[f32], name="proj_xbc")
    dtr = mm(h, wts["dt"], tn=LANES, tk=d, out_dtypes=[f32], name="proj_dt")
    gl = mm(h, wts["gate"], tn=1024, tk=d, out_dtypes=[f32], name="proj_gate")
    ya = _gmlp_fwd(uv, small["v_norm_g"], small["v_norm_b"], wsp, bsp_t, name="gmlp_fwd")
    yb, hprev, *gathered = _ssd_fwd(xbc, z, dtr, small["conv_w"], small["conv_b"], dtb, alog, dsk,
                                    small["ssm_norm_g"], ride=fwd_ride, name="ssd_fwd")
    if fwd_ride is not None:
        wts = {**wts, **late_weights(gathered[0])}
    pa = mm(ya, wts["pa"], tn=1024, tk=1024, out_dtypes=[f32], name="proj_a")
    pb = mm(yb, wts["pb"], tn=1024, tk=1024, out_dtypes=[f32], name="proj_b")
    merged = _merge_fwd(pa, pb, gl, small["b_gates"], name="merge_fwd")
    x1 = mm(merged, wts["out"], tn=1024, tk=1024, out_dtypes=[f32], extras=[x],
            epilogue=lambda acc, res: (res + acc,), name="out_proj")
    h2 = _rms_fwd(x1, small["norm_mlp_g"], name="rms_mlp")
    act = mm(h2, wts["up"], tn=1024, tk=d, out_dtypes=[bf16],
             epilogue=lambda acc: (jnp.square(jnp.maximum(acc, 0.0)),), name="mlp_up")
    x2 = mm(act, wts["down"], tn=1024, tk=1024, out_dtypes=[f32], extras=[x1],
            epilogue=lambda acc, res: (res + acc,), name="mlp_down")

    dx2, dx2b, dgf, loss = _loss_head(x2, tgt, small["norm_final_g"], name="loss_head")
    tt = min(t, 1024)
    tn_mm = functools.partial(_matmul_tn, tt=tt)
    dw = {}
    dw["down"] = tn_mm(act, dx2b, tka=1024, tn=1024, name="dw_down")
    dup = mm(dx2b, wts["down"], nt=True, tn=1024, tk=1024, out_dtypes=[bf16], extras=[act],
             epilogue=lambda acc, a2: (acc * (2.0 * jnp.sqrt(a2.astype(f32))),), name="d_act")
    dw["up"] = tn_mm(h2, dup, tka=1024, tn=1024, name="dw_up")
    dh2 = mm(dup, wts["up"], nt=True, tn=1024, tk=1024, out_dtypes=[f32], name="d_h2")
    dx1, dx1b, dg_mlp = _rms_bwd(x1, small["norm_mlp_g"], dh2, dx2, want_bf16=True, name="rms_mlp_bwd")
    dw["out"] = tn_mm(merged, dx1b, tka=1024, tn=1024, name="dw_out")
    dmerged = mm(dx1b, wts["out"], nt=True, tn=1024, tk=1024, out_dtypes=[f32], name="d_merged")
    dpa, dpb, dgl, dbg = _merge_bwd(dmerged, pa, pb, gl, small["b_gates"], name="merge_bwd")
    dw["pa"] = tn_mm(ya, dpa, tka=1024, tn=1024, name="dw_pa")
    dw["pb"] = tn_mm(yb, dpb, tka=1024, tn=1024, name="dw_pb")
    dya = mm(dpa, wts["pa"], nt=True, tn=1024, tk=1024, out_dtypes=[f32], name="d_ya")
    dyb = mm(dpb, wts["pb"], nt=True, tn=1024, tk=1024, out_dtypes=[f32], name="d_yb")
    duv, dwsp, dbsp_t, dvg, dvb = _gmlp_bwd(uv, dya, small["v_norm_g"], small["v_norm_b"], wsp, bsp_t,
                                            name="gmlp_bwd")
    ride = bwd_ride(dw) if bwd_ride is not None else None
    dz, dxbc, ddt, dcw, dcb, ddtb, dalog, ddsk, dgs, *got = _ssd_bwd(
        xbc, z, dtr, hprev, dyb, small["conv_w"], small["conv_b"], dtb, alog, dsk, small["ssm_norm_g"],
        _head_seg_matrix(), ride=ride, name="ssd_bwd")
    dw["uv"] = tn_mm(h, duv, tka=1024, tn=1024, name="dw_uv")
    dw["z"] = tn_mm(h, dz, tka=1024, tn=1024, name="dw_z")
    dw["xbc"] = tn_mm(h, dxbc, tka=1024, tn=1024, name="dw_xbc")
    dw["dt"] = tn_mm(h, ddt, tka=1024, tn=LANES, name="dw_dt")
    dw["gate"] = tn_mm(h, dgl, tka=1024, tn=1024, name="dw_gate")
    dh = mm(ddt, wts["dt"], nt=True, tn=1024, tk=LANES, out_dtypes=[f32], name="d_h_dt")
    dh = _matmul_nt_sum([(duv, wts["uv"]), (dz, wts["z"]), (dxbc, wts["xbc"]), (dgl, wts["gate"])], dh,
                        tm=tm, tk=512, name="d_h")
    dx, dg_mix = _rms_bwd(x, small["norm_mix_g"], dh, dx1, want_bf16=False, name="rms_mix_bwd")

    dsmall = {
        "norm_mix_g": dg_mix, "conv_w": dcw, "conv_b": dcb, "dt_bias": ddtb[:, :N_HEADS], "a_log": dalog[:, :N_HEADS],
        "d_skip": ddsk[:, :N_HEADS], "ssm_norm_g": dgs, "v_norm_g": dvg, "v_norm_b": dvb, "w_spatial": dwsp,
        "b_spatial": dbsp_t[:, :GMLP_GROUPS].T, "b_gates": dbg, "norm_mlp_g": dg_mlp, "norm_final_g": dgf,
    }
    return loss, dx, dw, dsmall, (got[0] if got else None)


_IN_SHARD = IN_PROJ // N_CHIPS
_DENSE = ("w_in", "w_proj_a", "w_proj_b", "w_out", "w_mlp_up", "w_mlp_down")
_DENSE_SHARD_SHAPES = {"w_in": (D_MODEL, _IN_SHARD), "w_proj_a": (GMLP_WIDTH // N_CHIPS, D_MODEL),
                       "w_proj_b": (D_INNER // N_CHIPS, D_MODEL), "w_out": (D_MODEL // N_CHIPS, D_MODEL),
                       "w_mlp_up": (D_MODEL, D_FF // N_CHIPS), "w_mlp_down": (D_FF // N_CHIPS, D_MODEL)}
_DENSE_ROWS = {k: s[0] * s[1] // LANES for k, s in _DENSE_SHARD_SHAPES.items()}
_DENSE_TOTAL = sum(_DENSE_ROWS.values())
_CONV_ROWS = CONV_W * (CONV_DIM // N_CHIPS) * 2 // LANES


def _dense_offsets():
    off, out = 0, {}
    for k in _DENSE:
        out[k] = off
        off += _DENSE_ROWS[k]
    return out


_DENSE_OFF = _dense_offsets()

_SMALL = ("norm_mix_g", "conv_w", "conv_b", "dt_bias", "a_log", "d_skip", "ssm_norm_g", "v_norm_g", "v_norm_b",
          "w_spatial", "b_spatial", "b_gates", "norm_mlp_g", "norm_final_g")


def _pack_small(parts):
    flat = jnp.concatenate([parts[k].reshape(-1) for k in _SMALL])
    rows = -(-flat.shape[0] // (8 * LANES)) * 8
    return jnp.pad(flat, (0, rows * LANES - flat.shape[0])).reshape(rows, LANES)


def _unpack_small(packed, shapes):
    flat = packed.reshape(-1)
    out, off = {}, 0
    for k in _SMALL:
        n = math.prod(shapes[k])
        out[k] = flat[off:off + n].reshape(shapes[k])
        off += n
    return out


def _from_chip_columns(stacked, rows, cols):
    return stacked.reshape(N_CHIPS, rows, cols).transpose(1, 0, 2).reshape(rows, N_CHIPS * cols)


def _to_chip_columns(full, cols):
    rows = full.shape[0]
    return full.reshape(rows, N_CHIPS, cols).transpose(1, 0, 2).reshape(N_CHIPS, rows * cols // LANES, LANES)


def kernel(x, norm_mix_g, w_in, conv_w, conv_b, dt_bias, a_log, d_skip, ssm_norm_g, v_norm_g, v_norm_b, w_spatial, b_spatial, b_gates, w_proj_a, w_proj_b, w_out, norm_mlp_g, w_mlp_up, w_mlp_down, norm_final_g, loss_target, m_norm_mix_g, m_w_in, m_conv_w, m_conv_b, m_dt_bias, m_a_log, m_d_skip, m_ssm_norm_g, m_v_norm_g, m_v_norm_b, m_w_spatial, m_b_spatial, m_b_gates, m_w_proj_a, m_w_proj_b, m_w_out, m_norm_mlp_g, m_w_mlp_up, m_w_mlp_down, m_norm_final_g, v_norm_mix_g, v_w_in, v_conv_w, v_conv_b, v_dt_bias, v_a_log, v_d_skip, v_ssm_norm_g, v_v_norm_g, v_v_norm_b, v_w_spatial, v_b_spatial, v_b_gates, v_w_proj_a, v_w_proj_b, v_w_out, v_norm_mlp_g, v_w_mlp_up, v_w_mlp_down, v_norm_final_g):
    given = dict(locals())
    names = ("norm_mix_g", "w_in", "conv_w", "conv_b", "dt_bias", "a_log", "d_skip", "ssm_norm_g", "v_norm_g",
             "v_norm_b", "w_spatial", "b_spatial", "b_gates", "w_proj_a", "w_proj_b", "w_out", "norm_mlp_g",
             "w_mlp_up", "w_mlp_down", "norm_final_g")
    xi, yi, ci = lax.axis_index("x"), lax.axis_index("y"), lax.axis_index("c")
    me_chip = (2 * xi + yi).astype(jnp.int32)

    in_rows = _DENSE_ROWS["w_in"]
    late_rows = _DENSE_TOTAL - in_rows
    w_in_b16 = _cast_bf16(w_in[0].reshape(in_rows, LANES), name="cast_w_in")
    late_b16 = _cast_bf16(jnp.concatenate([given[k][0].reshape(-1, LANES) for k in _DENSE[1:]]), name="cast_w_late")
    conv_shard = conv_w.reshape(CONV_W, CONV_DIM // N_CHIPS)
    conv_bits = lax.bitcast_convert_type(conv_shard.reshape(-1, LANES), bf16).reshape(_CONV_ROWS, LANES)
    shard_in = jnp.concatenate([w_in_b16, conv_bits]).reshape(2, (in_rows + _CONV_ROWS) // 2, LANES)
    shard_late = late_b16.reshape(2, late_rows // 2, LANES)

    def with_own(got, shard):
        return lax.dynamic_update_slice(got, shard[None], (me_chip, 0, 0, 0)).reshape(N_CHIPS, -1, LANES)

    g_in = with_own(_gather_shards(shard_in, name="gather_w_in"), shard_in)
    w_in_full = _from_chip_columns(g_in[:, :in_rows], D_MODEL, _IN_SHARD)
    o_dt, o_gate = 2 * GMLP_WIDTH + D_INNER + CONV_DIM, 2 * GMLP_WIDTH + D_INNER + CONV_DIM + N_HEADS
    wts = {
        "uv": w_in_full[:, :2 * GMLP_WIDTH], "z": w_in_full[:, 2 * GMLP_WIDTH:2 * GMLP_WIDTH + D_INNER],
        "xbc": w_in_full[:, 2 * GMLP_WIDTH + D_INNER:o_dt], "dt": _pad_lanes(w_in_full[:, o_dt:o_gate]),
        "gate": w_in_full[:, o_gate:],
    }
    conv_all = lax.bitcast_convert_type(g_in[:, in_rows:].reshape(N_CHIPS, _CONV_ROWS // 2, LANES, 2), f32)
    conv_full = conv_all.reshape(N_CHIPS, CONV_W, CONV_DIM // N_CHIPS).transpose(1, 0, 2).reshape(CONV_W, CONV_DIM)

    def late_weights(got):
        g_late = with_own(got, shard_late)

        def rows_of(k):
            off = _DENSE_OFF[k] - in_rows
            return g_late[:, off:off + _DENSE_ROWS[k]]

        return {
            "pa": rows_of("w_proj_a").reshape(GMLP_WIDTH, D_MODEL),
            "pb": rows_of("w_proj_b").reshape(D_INNER, D_MODEL), "out": rows_of("w_out").reshape(D_MODEL, D_MODEL),
            "up": _from_chip_columns(rows_of("w_mlp_up"), D_MODEL, D_FF // N_CHIPS),
            "down": rows_of("w_mlp_down").reshape(D_FF, D_MODEL),
        }

    small = {
        "norm_mix_g": norm_mix_g, "conv_w": conv_full, "conv_b": conv_b, "dt_bias": dt_bias, "a_log": a_log,
        "d_skip": d_skip, "ssm_norm_g": ssm_norm_g, "v_norm_g": v_norm_g, "v_norm_b": v_norm_b,
        "w_spatial": w_spatial[0], "b_spatial": b_spatial[0], "b_gates": b_gates, "norm_mlp_g": norm_mlp_g,
        "norm_final_g": norm_final_g.reshape(1, D_MODEL),
    }

    c_idx = ci.astype(jnp.int32).reshape(1)
    me_idx = me_chip.reshape(1)
    partials = {}

    def pair_sums(g, tag):
        g = g.reshape(N_CHIPS, 2, -1, LANES)
        p32, p16 = _rs_add_pair(g, _rs_swap_halves(g, name="rs_swap_" + tag), c_idx, name="rs_add_pair_" + tag)
        partials[tag] = p32
        return p16

    def reduced_shard(tag, got_chips):
        half = _rs_add_chips(partials[tag], got_chips, me_idx, name="rs_add_chips_" + tag)
        other_half = _rs_join_halves(half, name="rs_join_" + tag)
        return jnp.where(ci == 0, jnp.concatenate([half, other_half]), jnp.concatenate([other_half, half]))

    def late_partials(dw):
        return pair_sums(jnp.concatenate(
            [dw["pa"].reshape(N_CHIPS, -1, LANES), dw["pb"].reshape(N_CHIPS, -1, LANES),
             dw["out"].reshape(N_CHIPS, -1, LANES), _to_chip_columns(dw["up"], D_FF // N_CHIPS),
             dw["down"].reshape(N_CHIPS, -1, LANES)], axis=1), "late")

    loss_part, grad_x, dw, dsmall, got_late = _local_grads(
        x[0], loss_target[0], wts, small, fwd_ride=shard_late, late_weights=late_weights, bwd_ride=late_partials)
    loss = lax.psum(loss_part[0, 0], ("x", "y", "c"))
    g_late = reduced_shard("late", got_late)
    dw_in = jnp.concatenate([dw["uv"], dw["z"], dw["xbc"], dw["dt"][:, :N_HEADS], dw["gate"]], axis=1)
    p16_in = pair_sums(_to_chip_columns(dw_in, _IN_SHARD), "in")
    g_in_shard = reduced_shard("in", _rs_scatter_chips(p16_in, name="rs_scatter_in"))

    small_shapes = {k: dsmall[k].shape for k in _SMALL}
    red = _unpack_small(_all_reduce_small(_pack_small(dsmall), name="all_reduce_small"), small_shapes)
    conv_cols = CONV_DIM // N_CHIPS
    red["conv_w"] = lax.dynamic_slice_in_dim(red["conv_w"], me_chip * conv_cols, conv_cols, axis=1)

    grads, deltas, new_m, new_v = {}, {}, {}, {}
    for k in _DENSE:
        shp = _DENSE_SHARD_SHAPES[k]
        src, off = (g_in_shard, 0) if k == "w_in" else (g_late, _DENSE_OFF[k] - in_rows)
        g2 = src[off:off + _DENSE_ROWS[k]].reshape(shp)
        dlt, m2, v2 = _adamw(given[k][0], g2, given["m_" + k][0], given["v_" + k][0], name="adamw_" + k)
        grads[k], deltas[k], new_m[k], new_v[k] = g2, dlt, m2, v2
    adam_shapes = dict(small_shapes)
    adam_shapes["conv_w"] = (CONV_W, conv_cols)

    def small_pack_of(prefix):
        return _pack_small({k: given[prefix + k].reshape(adam_shapes[k]) for k in _SMALL})

    dlt_s, m_s, v_s = _adamw(small_pack_of(""), _pack_small(red), small_pack_of("m_"), small_pack_of("v_"),
                             name="adamw_small")
    for dst, packed in ((deltas, dlt_s), (new_m, m_s), (new_v, v_s)):
        dst.update(_unpack_small(packed, adam_shapes))
    grads.update(red)

    def shaped(dct):
        return [dct[k].reshape(given[k].shape) for k in names]

    return (loss, grad_x[None], *shaped(grads), *shaped(deltas), *shaped(new_m), *shaped(new_v))
```

```python
import functools
import math

import jax
import jax.numpy as jnp
from jax import lax
from jax.experimental import pallas as pl
from jax.experimental.pallas import tpu as pltpu

f32 = jnp.float32
bf16 = jnp.bfloat16

D_MODEL = 1024
CHUNK = 128
GMLP_WIDTH = 1024
GMLP_GROUPS = 8
D_INNER = 2048
HEAD_DIM = 64
N_HEADS = 32
N_GROUPS = 8
HEADS_PER_GROUP = 4
GROUP_W = HEADS_PER_GROUP * HEAD_DIM
D_STATE = 128
CONV_W = 4
CONV_DIM = 4096
D_FF = 4096
IN_PROJ = 10272
NORM_EPS = 1e-6
N_CHIPS = 4
N_DEV = 8
LANES = 128

ADAM_LR = 0.001
ADAM_B1 = 0.9
ADAM_B2 = 0.999
ADAM_EPS = 1e-08
ADAM_WD = 0.01
ADAM_STEP = 10

MESH = pl.DeviceIdType.MESH
_NT = (((1,), (1,)), ((), ()))
_NN = (((1,), (0,)), ((), ()))
_TN = (((0,), (0,)), ((), ()))
_MB = 2 ** 20


def _params(sem, vmem_mb=48):
    return pltpu.CompilerParams(dimension_semantics=sem, vmem_limit_bytes=vmem_mb * _MB)


def _dot(a, b, dims=_NN):
    return lax.dot_general(a.astype(bf16), b.astype(bf16), dims, preferred_element_type=f32)


def _dot32(a, b):
    return jnp.dot(a, b, preferred_element_type=f32, precision=lax.Precision.HIGHEST)


def _sigmoid(x):
    return 1.0 / (1.0 + jnp.exp(-x))


def _sum_all(a):
    return jnp.sum(jnp.sum(a, axis=1, keepdims=True), axis=0, keepdims=True)


def _iota(shape, dim):
    return lax.broadcasted_iota(jnp.int32, shape, dim)


def _matmul(a, b, *, nt=False, tm, tn, tk, out_dtypes, epilogue=None, extras=(), name):
    m, k_dim = a.shape
    n = b.shape[0] if nt else b.shape[1]
    nk = k_dim // tk
    ne, no = len(extras), len(out_dtypes)
    dims = _NT if nt else _NN

    def body(*refs):
        a_ref, b_ref = refs[0], refs[1]
        ex = refs[2:2 + ne]
        outs = refs[2 + ne:2 + ne + no]

        def finish(acc):
            vals = epilogue(acc, *[e[...] for e in ex]) if epilogue is not None else (acc,)
            for o, v in zip(outs, vals):
                o[...] = v.astype(o.dtype)

        part = lax.dot_general(a_ref[...], b_ref[...], dims, preferred_element_type=f32)
        if nk == 1:
            finish(part)
        else:
            acc_ref = refs[-1]
            kk = pl.program_id(2)

            @pl.when(kk == 0)
            def _():
                acc_ref[...] = part

            @pl.when(kk > 0)
            def _():
                acc_ref[...] += part

            @pl.when(kk == nk - 1)
            def _():
                finish(acc_ref[...])

    b_spec = pl.BlockSpec((tn, tk), lambda i, j, k: (j, k)) if nt else pl.BlockSpec((tk, tn), lambda i, j, k: (k, j))
    tile = pl.BlockSpec((tm, tn), lambda i, j, k: (i, j))
    outs = pl.pallas_call(
        body, name=name, grid=(m // tm, n // tn, nk),
        in_specs=[pl.BlockSpec((tm, tk), lambda i, j, k: (i, k)), b_spec] + [tile] * ne,
        out_specs=[tile] * no,
        out_shape=[jax.ShapeDtypeStruct((m, n), dt) for dt in out_dtypes],
        scratch_shapes=[pltpu.VMEM((tm, tn), f32)] if nk > 1 else [],
        compiler_params=_params(("parallel", "parallel", "arbitrary")),
    )(a, b, *extras)
    return outs if no > 1 else outs[0]


def _matmul_nt_sum(pairs, *, tm, tks, ride=None, name):
    m = pairs[0][0].shape[0]
    n = pairs[0][1].shape[0]
    nblk = [a.shape[1] // tk for (a, _), tk in zip(pairs, tks)]
    starts = [sum(nblk[:p]) for p in range(len(pairs))]
    nk = sum(nblk)
    npairs = len(pairs)
    ni = m // tm
    riding = ride is not None

    def body(*refs):
        rest = refs[2 * npairs:]
        if riding:
            ride_ref, o_ref, got_ref, acc_ref, send_sems, recv_sems = rest
        else:
            o_ref, acc_ref = rest
        i, kk = pl.program_id(0), pl.program_id(1)
        if riding:
            start, finish = _scatter_protocol(ride_ref, got_ref, send_sems, recv_sems)
            pl.when((i == 0) & (kk == 0))(start)

        @pl.when(kk == 0)
        def _():
            acc_ref[...] = jnp.zeros_like(acc_ref)

        for p in range(npairs):
            @pl.when((kk >= starts[p]) & (kk < starts[p] + nblk[p]))
            def _(p=p):
                acc_ref[...] += lax.dot_general(refs[2 * p][...], refs[2 * p + 1][...], _NT, preferred_element_type=f32)

        @pl.when(kk == nk - 1)
        def _():
            o_ref[...] = acc_ref[...]

        if riding:
            pl.when((i == ni - 1) & (kk == nk - 1))(finish)

    in_specs, args = [], []
    for p, (a, b) in enumerate(pairs):
        def kblock(k, s=starts[p], nb=nblk[p]):
            return jnp.clip(k - s, 0, nb - 1)
        in_specs.append(pl.BlockSpec((tm, tks[p]), lambda i, k, kb=kblock: (i, kb(k))))
        in_specs.append(pl.BlockSpec((n, tks[p]), lambda i, k, kb=kblock: (0, kb(k))))
        args += [a, b]
    tile = pl.BlockSpec((tm, n), lambda i, k: (i, 0))
    outs = pl.pallas_call(
        body, name=name, grid=(ni, nk), in_specs=in_specs + [_ANY] * riding, out_specs=[tile] + [_ANY] * riding,
        out_shape=[jax.ShapeDtypeStruct((m, n), f32)]
        + ([jax.ShapeDtypeStruct((N_CHIPS - 1,) + ride.shape[1:], ride.dtype)] if riding else []),
        scratch_shapes=[pltpu.VMEM((tm, n), f32)] + (list(_SCATTER_SCRATCH) if riding else []),
        compiler_params=_params(("arbitrary", "arbitrary"), vmem_mb=56),
    )(*args, *([ride] if riding else []))
    return outs if riding else outs[0]


def _matmul_tn(a, b, *, tka, tn, tt, name):
    t, ka = a.shape
    n = b.shape[1]

    def body(a_ref, b_ref, o_ref):
        part = lax.dot_general(a_ref[...], b_ref[...], _TN, preferred_element_type=f32)
        kk = pl.program_id(2)

        @pl.when(kk == 0)
        def _():
            o_ref[...] = part

        @pl.when(kk > 0)
        def _():
            o_ref[...] += part

    return pl.pallas_call(
        body, name=name, grid=(ka // tka, n // tn, t // tt),
        in_specs=[pl.BlockSpec((tt, tka), lambda i, j, k: (k, i)), pl.BlockSpec((tt, tn), lambda i, j, k: (k, j))],
        out_specs=pl.BlockSpec((tka, tn), lambda i, j, k: (i, j)),
        out_shape=jax.ShapeDtypeStruct((ka, n), f32),
        compiler_params=_params(("parallel", "parallel", "arbitrary")),
    )(a, b)


def _row_tile(t):
    return min(t, 512)


def _rms_fwd(x, g, *, name):
    t, d = x.shape
    tr = _row_tile(t)

    def body(x_ref, g_ref, h_ref):
        xv = x_ref[...]
        r = lax.rsqrt(jnp.mean(xv * xv, axis=1, keepdims=True) + NORM_EPS)
        h_ref[...] = (xv * r * g_ref[...]).astype(bf16)

    return pl.pallas_call(
        body, name=name, grid=(t // tr,),
        in_specs=[pl.BlockSpec((tr, d), lambda i: (i, 0)), pl.BlockSpec((1, d), lambda i: (0, 0))],
        out_specs=pl.BlockSpec((tr, d), lambda i: (i, 0)),
        out_shape=jax.ShapeDtypeStruct((t, d), bf16),
        compiler_params=_params(("parallel",)),
    )(x, g)


def _rms_bwd(xin, g, dh, dres, *, want_bf16, name):
    t, d = xin.shape
    tr = _row_tile(t)

    def body(x_ref, g_ref, dh_ref, dres_ref, dx_ref, *rest):
        dg_ref = rest[-1]
        xv = x_ref[...]
        r = lax.rsqrt(jnp.mean(xv * xv, axis=1, keepdims=True) + NORM_EPS)
        xn = xv * r
        dhv = dh_ref[...]
        dxn = dhv * g_ref[...]
        dx = dres_ref[...] + r * (dxn - xn * jnp.mean(dxn * xn, axis=1, keepdims=True))
        dx_ref[...] = dx
        if want_bf16:
            rest[0][...] = dx.astype(bf16)
        part = jnp.sum(dhv * xn, axis=0, keepdims=True)

        @pl.when(pl.program_id(0) == 0)
        def _():
            dg_ref[...] = part

        @pl.when(pl.program_id(0) > 0)
        def _():
            dg_ref[...] += part

    row = pl.BlockSpec((tr, d), lambda i: (i, 0))
    vec = pl.BlockSpec((1, d), lambda i: (0, 0))
    out_shape = [jax.ShapeDtypeStruct((t, d), f32)] + ([jax.ShapeDtypeStruct((t, d), bf16)] if want_bf16 else []) \
        + [jax.ShapeDtypeStruct((1, d), f32)]
    return pl.pallas_call(
        body, name=name, grid=(t // tr,),
        in_specs=[row, vec, row, row],
        out_specs=[row] + ([row] if want_bf16 else []) + [vec],
        out_shape=out_shape,
        compiler_params=_params(("arbitrary",)),
    )(xin, g, dh, dres)


def _loss_head(x2, tgt, g, *, name):
    t, d = x2.shape
    tr = _row_tile(t)

    def body(x_ref, t_ref, g_ref, dx_ref, dxb_ref, dg_ref, loss_ref):
        xv = x_ref[...]
        gv = g_ref[...]
        r = lax.rsqrt(jnp.mean(xv * xv, axis=1, keepdims=True) + NORM_EPS)
        xn = xv * r
        e = xn * gv - t_ref[...]
        lpart = jnp.zeros((1, LANES), f32) + 0.5 * _sum_all(jnp.mean(e * e, axis=1, keepdims=True))
        dy = e * (1.0 / d)
        dxn = dy * gv
        dx = r * (dxn - xn * jnp.mean(dxn * xn, axis=1, keepdims=True))
        dx_ref[...] = dx
        dxb_ref[...] = dx.astype(bf16)
        gpart = jnp.sum(dy * xn, axis=0, keepdims=True)

        @pl.when(pl.program_id(0) == 0)
        def _():
            dg_ref[...] = gpart
            loss_ref[...] = lpart

        @pl.when(pl.program_id(0) > 0)
        def _():
            dg_ref[...] += gpart
            loss_ref[...] += lpart

    row = pl.BlockSpec((tr, d), lambda i: (i, 0))
    vec = pl.BlockSpec((1, d), lambda i: (0, 0))
    return pl.pallas_call(
        body, name=name, grid=(t // tr,),
        in_specs=[row, row, vec],
        out_specs=[row, row, vec, pl.BlockSpec((1, LANES), lambda i: (0, 0))],
        out_shape=[jax.ShapeDtypeStruct((t, d), f32), jax.ShapeDtypeStruct((t, d), bf16),
                   jax.ShapeDtypeStruct((1, d), f32), jax.ShapeDtypeStruct((1, LANES), f32)],
        compiler_params=_params(("arbitrary",)),
    )(x2, tgt, g)


def _merge_fwd(pa, pb, gl, bg, *, name):
    t, d = pa.shape
    tr = _row_tile(t)

    def body(pa_ref, pb_ref, gla_ref, glb_ref, bga_ref, bgb_ref, o_ref):
        ga = _sigmoid(gla_ref[...] + bga_ref[...])
        gb = _sigmoid(glb_ref[...] + bgb_ref[...])
        o_ref[...] = (ga * pa_ref[...] + gb * pb_ref[...]).astype(bf16)

    row = pl.BlockSpec((tr, d), lambda i: (i, 0))
    return pl.pallas_call(
        body, name=name, grid=(t // tr,),
        in_specs=[row, row, row, pl.BlockSpec((tr, d), lambda i: (i, 1)),
                  pl.BlockSpec((1, d), lambda i: (0, 0)), pl.BlockSpec((1, d), lambda i: (0, 1))],
        out_specs=row,
        out_shape=jax.ShapeDtypeStruct((t, d), bf16),
        compiler_params=_params(("parallel",)),
    )(pa, pb, gl, gl, bg, bg)


def _merge_bwd(dm, pa, pb, gl, bg, *, name):
    t, d = pa.shape
    tr = _row_tile(t)

    def body(dm_ref, pa_ref, pb_ref, gla_ref, glb_ref, bga_ref, bgb_ref, dpa_ref, dpb_ref, dgl_ref, dbg_ref):
        dmv = dm_ref[...]
        ga = _sigmoid(gla_ref[...] + bga_ref[...])
        gb = _sigmoid(glb_ref[...] + bgb_ref[...])
        dpa_ref[...] = (dmv * ga).astype(bf16)
        dpb_ref[...] = (dmv * gb).astype(bf16)
        dla = dmv * pa_ref[...] * ga * (1.0 - ga)
        dlb = dmv * pb_ref[...] * gb * (1.0 - gb)
        dgl_ref[:, :d] = dla.astype(bf16)
        dgl_ref[:, d:] = dlb.astype(bf16)
        sa = jnp.sum(dla, axis=0, keepdims=True)
        sb = jnp.sum(dlb, axis=0, keepdims=True)

        @pl.when(pl.program_id(0) == 0)
        def _():
            dbg_ref[:, :d] = sa
            dbg_ref[:, d:] = sb

        @pl.when(pl.program_id(0) > 0)
        def _():
            dbg_ref[:, :d] += sa
            dbg_ref[:, d:] += sb

    row = pl.BlockSpec((tr, d), lambda i: (i, 0))
    return pl.pallas_call(
        body, name=name, grid=(t // tr,),
        in_specs=[row, row, row, row, pl.BlockSpec((tr, d), lambda i: (i, 1)),
                  pl.BlockSpec((1, d), lambda i: (0, 0)), pl.BlockSpec((1, d), lambda i: (0, 1))],
        out_specs=[row, row, pl.BlockSpec((tr, 2 * d), lambda i: (i, 0)), pl.BlockSpec((1, 2 * d), lambda i: (0, 0))],
        out_shape=[jax.ShapeDtypeStruct((t, d), bf16), jax.ShapeDtypeStruct((t, d), bf16),
                   jax.ShapeDtypeStruct((t, 2 * d), bf16), jax.ShapeDtypeStruct((1, 2 * d), f32)],
        compiler_params=_params(("arbitrary",)),
    )(dm, pa, pb, gl, gl, bg, bg)


_INV_SQRT2 = 1.0 / math.sqrt(2.0)
_INV_SQRT2PI = 1.0 / math.sqrt(2.0 * math.pi)


def _gelu(x):
    return 0.5 * x * (1.0 + lax.erf(x * _INV_SQRT2))


def _gelu_grad(x):
    return 0.5 * (1.0 + lax.erf(x * _INV_SQRT2)) + x * jnp.exp(-0.5 * x * x) * _INV_SQRT2PI


def _gmlp_common(uv, vg, vb):
    zz = _gelu(uv)
    u = zz[:, :GMLP_WIDTH]
    v = zz[:, GMLP_WIDTH:]
    mu = jnp.mean(v, axis=1, keepdims=True)
    vc = v - mu
    rstd = lax.rsqrt(jnp.mean(vc * vc, axis=1, keepdims=True) + NORM_EPS)
    vhat = vc * rstd
    vn = vhat * vg + vb
    return u, vhat, rstd, vn


def _gmlp_fwd(uv, vg, vb, wsp, bsp_t, *, name):
    t = uv.shape[0]
    nc = t // CHUNK

    def body(uv_ref, vg_ref, vb_ref, w_ref, b_ref, y_ref):
        u, _, _, vn = _gmlp_common(uv_ref[...], vg_ref[...], vb_ref[...])
        tril = _iota((CHUNK, CHUNK), 0) >= _iota((CHUNK, CHUNK), 1)
        bt = b_ref[...]
        for g in range(GMLP_GROUPS):
            sl = slice(g * CHUNK, (g + 1) * CHUNK)
            w = jnp.where(tril, w_ref[g], 0.0)
            s = _dot(w, vn[:, sl]) + bt[:, g:g + 1]
            y_ref[:, sl] = (u[:, sl] * s).astype(bf16)

    return pl.pallas_call(
        body, name=name, grid=(nc,),
        in_specs=[pl.BlockSpec((CHUNK, 2 * GMLP_WIDTH), lambda c: (c, 0)),
                  pl.BlockSpec((1, GMLP_WIDTH), lambda c: (0, 0)), pl.BlockSpec((1, GMLP_WIDTH), lambda c: (0, 0)),
                  pl.BlockSpec((GMLP_GROUPS, CHUNK, CHUNK), lambda c: (0, 0, 0)),
                  pl.BlockSpec((CHUNK, LANES), lambda c: (0, 0))],
        out_specs=pl.BlockSpec((CHUNK, GMLP_WIDTH), lambda c: (c, 0)),
        out_shape=jax.ShapeDtypeStruct((t, GMLP_WIDTH), bf16),
        compiler_params=_params(("parallel",)),
    )(uv, vg, vb, wsp, bsp_t)


def _gmlp_bwd(uv, dya, vg, vb, wsp, bsp_t, *, name):
    t = uv.shape[0]
    nc = t // CHUNK

    def body(uv_ref, dy_ref, vg_ref, vb_ref, w_ref, b_ref, duv_ref, dw_ref, db_ref, dvg_ref, dvb_ref):
        first = pl.program_id(0) == 0

        @pl.when(first)
        def _():
            dw_ref[...] = jnp.zeros_like(dw_ref)
            db_ref[...] = jnp.zeros_like(db_ref)
            dvg_ref[...] = jnp.zeros_like(dvg_ref)
            dvb_ref[...] = jnp.zeros_like(dvb_ref)

        uvv = uv_ref[...]
        vgv = vg_ref[...]
        u, vhat, rstd, vn = _gmlp_common(uvv, vgv, vb_ref[...])
        dy = dy_ref[...]
        tril = _iota((CHUNK, CHUNK), 0) >= _iota((CHUNK, CHUNK), 1)
        lane = _iota((CHUNK, LANES), 1)
        bt = b_ref[...]
        ds_all = dy * u
        dbacc = jnp.zeros((CHUNK, LANES), f32)
        dvh_parts = []
        for g in range(GMLP_GROUPS):
            sl = slice(g * CHUNK, (g + 1) * CHUNK)
            w = jnp.where(tril, w_ref[g], 0.0)
            vng = vn[:, sl]
            s = _dot(w, vng) + bt[:, g:g + 1]
            ds = ds_all[:, sl]
            duv_ref[:, sl] = (dy[:, sl] * s * _gelu_grad(uvv[:, sl])).astype(bf16)
            dw_ref[g] += jnp.where(tril, _dot(ds, vng, _NT), 0.0)
            dbacc = dbacc + jnp.where(lane == g, jnp.sum(ds, axis=1, keepdims=True), 0.0)
            dvn = _dot(w, ds, _TN)
            vh = vhat[:, sl]
            dvg_ref[:, sl] += jnp.sum(dvn * vh, axis=0, keepdims=True)
            dvb_ref[:, sl] += jnp.sum(dvn, axis=0, keepdims=True)
            dvh_parts.append(dvn * vgv[:, sl])
        db_ref[...] += dbacc
        dvhat = jnp.concatenate(dvh_parts, axis=1)
        m1 = jnp.mean(dvhat, axis=1, keepdims=True)
        m2 = jnp.mean(dvhat * vhat, axis=1, keepdims=True)
        dv = rstd * (dvhat - m1 - vhat * m2)
        duv_ref[:, GMLP_WIDTH:] = (dv * _gelu_grad(uvv[:, GMLP_WIDTH:])).astype(bf16)

    vec = pl.BlockSpec((1, GMLP_WIDTH), lambda c: (0, 0))
    return pl.pallas_call(
        body, name=name, grid=(nc,),
        in_specs=[pl.BlockSpec((CHUNK, 2 * GMLP_WIDTH), lambda c: (c, 0)),
                  pl.BlockSpec((CHUNK, GMLP_WIDTH), lambda c: (c, 0)), vec, vec,
                  pl.BlockSpec((GMLP_GROUPS, CHUNK, CHUNK), lambda c: (0, 0, 0)),
                  pl.BlockSpec((CHUNK, LANES), lambda c: (0, 0))],
        out_specs=[pl.BlockSpec((CHUNK, 2 * GMLP_WIDTH), lambda c: (c, 0)),
                   pl.BlockSpec((GMLP_GROUPS, CHUNK, CHUNK), lambda c: (0, 0, 0)),
                   pl.BlockSpec((CHUNK, LANES), lambda c: (0, 0)), vec, vec],
        out_shape=[jax.ShapeDtypeStruct((t, 2 * GMLP_WIDTH), bf16),
                   jax.ShapeDtypeStruct((GMLP_GROUPS, CHUNK, CHUNK), f32),
                   jax.ShapeDtypeStruct((CHUNK, LANES), f32),
                   jax.ShapeDtypeStruct((1, GMLP_WIDTH), f32), jax.ShapeDtypeStruct((1, GMLP_WIDTH), f32)],
        compiler_params=_params(("arbitrary",)),
    )(uv, dya, vg, vb, wsp, bsp_t)


_CONV_COLS = 512
_XS0, _B0, _C0 = 0, D_INNER, D_INNER + N_GROUPS * D_STATE


_TAIL = 8


def _conv_silu(cur_ref, tail_ref, w_ref, b_ref, has_prev, xc_ref, cv_ref):
    row = _iota((_TAIL, _CONV_COLS), 0)
    for j in range(CONV_DIM // _CONV_COLS):
        sl = slice(j * _CONV_COLS, (j + 1) * _CONV_COLS)
        cur = cur_ref[:, sl]
        tail = jnp.where(has_prev, tail_ref[:, sl], 0.0)
        acc = cur * w_ref[CONV_W - 1:CONV_W, sl] + b_ref[:, sl]
        for s in range(1, CONV_W):
            rolled = pltpu.roll(cur, s, 0)
            top = jnp.where(row >= s, rolled[:_TAIL], pltpu.roll(tail, s, 0))
            sh = jnp.concatenate([top, rolled[_TAIL:]], axis=0)
            acc = acc + sh * w_ref[CONV_W - 1 - s:CONV_W - s, sl]
        if cv_ref is not None:
            cv_ref[:, sl] = acc
        xc_ref[:, sl] = acc * _sigmoid(acc)


def _col_bcast(mat, h):
    return jnp.broadcast_to(mat[:, h:h + 1], (CHUNK, LANES))


def _head_expand(cols):
    lo = _iota((CHUNK, LANES), 1) < HEAD_DIM
    return jnp.concatenate([jnp.where(lo, cols[2 * j], cols[2 * j + 1]) for j in range(N_HEADS // 2)], axis=1)


def _ssd_chunk_scalars(dtr, dtb, alog):
    xdt_pre = dtr + dtb
    dtv = jnp.maximum(xdt_pre, 0.0) + jnp.log(1.0 + jnp.exp(-jnp.abs(xdt_pre)))
    a = -jnp.exp(alog)
    ltri = (_iota((CHUNK, CHUNK), 0) >= _iota((CHUNK, CHUNK), 1)).astype(f32)
    cs = _dot32(ltri, dtv * a)
    csb = [_col_bcast(cs, h) for h in range(N_HEADS)]
    cs_x = _head_expand(csb)
    dt_x = _head_expand([_col_bcast(dtv, h) for h in range(N_HEADS)])
    cl_x = cs_x[CHUNK - 1:CHUNK, :]
    return dict(xdt_pre=xdt_pre, dtv=dtv, a=a, cs=cs, cs_t=cs.T, csb=csb, dt_x=dt_x, e_x=jnp.exp(cs_x),
                dec_x=jnp.exp(cl_x - cs_x), dk_x=jnp.exp(cl_x))


def _head_masks():
    lane = _iota((CHUNK, GROUP_W), 1)
    return [(lane >= r * HEAD_DIM) & (lane < (r + 1) * HEAD_DIM) for r in range(HEADS_PER_GROUP)]


def _stack_heads(a, masks):
    return jnp.concatenate([jnp.where(m, a, 0.0) for m in masks], axis=0).astype(bf16)


def _seg_sum(a, seg):
    hi = a.astype(jnp.bfloat16)
    lo = (a - hi.astype(f32)).astype(jnp.bfloat16)
    return (lax.dot_general(hi, seg, _NN, preferred_element_type=f32)
            + lax.dot_general(lo, seg, _NN, preferred_element_type=f32))


def _head_seg_matrix():
    return (_iota((D_INNER, LANES), 0) // HEAD_DIM == _iota((D_INNER, LANES), 1)).astype(jnp.bfloat16)


def _ssd_fwd(xbc, z, dtr, cw, cb, dtb, alog, dsk_x, gs, *, ride=None, name):
    t = xbc.shape[0]
    nc = t // CHUNK
    tiles = CHUNK // _TAIL

    def body(*refs):
        cur_ref, tail_ref, z_ref, dtr_ref, cw_ref, cb_ref, dtb_ref, alog_ref, dsk_ref, gs_ref = refs[:10]
        if ride is None:
            yb_ref, hp_ref, state_ref, xc_ref = refs[10:]
        else:
            ride_ref, yb_ref, hp_ref, got_ref, state_ref, xc_ref, send_sems, recv_sems = refs[10:]
        c = pl.program_id(0)
        if ride is not None:
            start, relay, finish = _gather_protocol(ride_ref, got_ref, send_sems, recv_sems)
            pl.when(c == 0)(start)
            pl.when(c == nc // 2)(relay)

        @pl.when(c == 0)
        def _():
            state_ref[...] = jnp.zeros_like(state_ref)

        _conv_silu(cur_ref, tail_ref, cw_ref, cb_ref, c > 0, xc_ref, None)
        sc = _ssd_chunk_scalars(dtr_ref[...], dtb_ref[...], alog_ref[...])
        tril = _iota((CHUNK, CHUNK), 0) >= _iota((CHUNK, CHUNK), 1)
        masks = _head_masks()
        hp_ref[0] = state_ref[...]
        for g in range(N_GROUPS):
            gsl = slice(g * GROUP_W, (g + 1) * GROUP_W)
            xs_g = xc_ref[:, gsl]
            bg = xc_ref[:, _B0 + g * D_STATE:_B0 + (g + 1) * D_STATE]
            cg = xc_ref[:, _C0 + g * D_STATE:_C0 + (g + 1) * D_STATE]
            xdt_g = xs_g * sc["dt_x"][:, gsl]
            cbm = _dot(cg, bg, _NT)
            mw = jnp.concatenate(
                [cbm * jnp.exp(jnp.where(tril, sc["csb"][h] - sc["cs_t"][h:h + 1, :], -1e30))
                 for h in range(g * HEADS_PER_GROUP, (g + 1) * HEADS_PER_GROUP)], axis=1)
            ht_g = state_ref[:, gsl]
            y_g = _dot(mw, _stack_heads(xdt_g, masks)) + sc["e_x"][:, gsl] * _dot(cg, ht_g) + dsk_ref[:, gsl] * xs_g
            state_ref[:, gsl] = ht_g * sc["dk_x"][:, gsl] + _dot(bg, xdt_g * sc["dec_x"][:, gsl], _TN)
            zg = z_ref[:, gsl]
            yg = y_g * zg * _sigmoid(zg)
            rs = lax.rsqrt(jnp.mean(yg * yg, axis=1, keepdims=True) + NORM_EPS)
            yb_ref[:, gsl] = (yg * rs * gs_ref[:, gsl]).astype(bf16)
        if ride is not None:
            pl.when(c == nc - 1)(finish)

    def chunk(w):
        return pl.BlockSpec((CHUNK, w), lambda c: (c, 0))

    def const(shape):
        return pl.BlockSpec(shape, lambda c: (0,) * len(shape))

    riding = ride is not None
    return pl.pallas_call(
        body, name=name, grid=(nc,),
        in_specs=[chunk(CONV_DIM), pl.BlockSpec((_TAIL, CONV_DIM), lambda c: (jnp.maximum(c * tiles - 1, 0), 0)),
                  chunk(D_INNER), chunk(LANES), const((CONV_W, CONV_DIM)), const((1, CONV_DIM)),
                  const((1, LANES)), const((1, LANES)), const((1, D_INNER)), const((1, D_INNER))] + [_ANY] * riding,
        out_specs=[chunk(D_INNER), pl.BlockSpec((1, D_STATE, D_INNER), lambda c: (c, 0, 0))] + [_ANY] * riding,
        out_shape=[jax.ShapeDtypeStruct((t, D_INNER), bf16), jax.ShapeDtypeStruct((nc, D_STATE, D_INNER), f32)]
        + ([jax.ShapeDtypeStruct((N_CHIPS,) + ride.shape, ride.dtype)] if riding else []),
        scratch_shapes=[pltpu.VMEM((D_STATE, D_INNER), f32), pltpu.VMEM((CHUNK, CONV_DIM), f32)]
        + (list(_GATHER_SCRATCH) if riding else []),
        compiler_params=_params(("arbitrary",)),
    )(xbc, xbc, z, dtr, cw, cb, dtb, alog, dsk_x, gs, *([ride] if riding else []))


def _ssd_bwd(xbc, z, dtr, hprev, dyb, cw, cb, dtb, alog, dsk_x, gs, seg, *, ride=None, name):
    t = xbc.shape[0]
    nc = t // CHUNK
    tiles = CHUNK // _TAIL

    def body(*refs):
        (cur_ref, tail_ref, z_ref, dtr_ref, hp_ref, dyb_ref, cw_ref, cb_ref, dtb_ref, alog_ref, dsk_ref, gs_ref,
         seg_ref) = refs[:13]
        rest = refs[13:]
        if ride is not None:
            ride_ref, got_ref, send_sems, recv_sems = rest[0], rest[10], rest[-2], rest[-1]
            rest = rest[1:10] + rest[11:-2]
        (dz_ref, dxbc_ref, ddt_ref, dcw_ref, dcb_ref, ddtb_ref, dalog_ref, ddsk_ref, dgs_ref,
         dh_ref, dcnext_ref, xc_ref, cv_ref, dxc_ref, x13_ref, x2_ref, rows_ref) = rest
        i = pl.program_id(0)
        cc = nc - 1 - i
        if ride is not None:
            start, finish = _scatter_protocol(ride_ref, got_ref, send_sems, recv_sems)
            pl.when(i == 0)(start)

        @pl.when(i == 0)
        def _():
            for ref in (dh_ref, dcnext_ref, dcw_ref, dcb_ref, ddtb_ref, dalog_ref, ddsk_ref, dgs_ref, rows_ref):
                ref[...] = jnp.zeros_like(ref)

        _conv_silu(cur_ref, tail_ref, cw_ref, cb_ref, cc > 0, xc_ref, cv_ref)
        sc = _ssd_chunk_scalars(dtr_ref[...], dtb_ref[...], alog_ref[...])
        tril = _iota((CHUNK, CHUNK), 0) >= _iota((CHUNK, CHUNK), 1)
        triu = _iota((CHUNK, CHUNK), 0) <= _iota((CHUNK, CHUNK), 1)
        masks = _head_masks()
        rowh = _iota((N_HEADS, CHUNK), 0)
        dcs_t = jnp.zeros((N_HEADS, CHUNK), f32)
        for g in range(N_GROUPS):
            gsl = slice(g * GROUP_W, (g + 1) * GROUP_W)
            xs_g = xc_ref[:, gsl]
            bg = xc_ref[:, _B0 + g * D_STATE:_B0 + (g + 1) * D_STATE]
            cg = xc_ref[:, _C0 + g * D_STATE:_C0 + (g + 1) * D_STATE]
            dt_g, e_g, dec_g, dk_g = sc["dt_x"][:, gsl], sc["e_x"][:, gsl], sc["dec_x"][:, gsl], sc["dk_x"][:, gsl]
            dsk_g = dsk_ref[:, gsl]
            xdt_g = xs_g * dt_g
            xdt_stack = _stack_heads(xdt_g, masks)
            cbm = _dot(cg, bg, _NT)
            cbt = _dot(bg, cg, _NT)
            heads = range(g * HEADS_PER_GROUP, (g + 1) * HEADS_PER_GROUP)
            lmats = [jnp.exp(jnp.where(tril, sc["csb"][h] - sc["cs_t"][h:h + 1, :], -1e30)) for h in heads]
            mw = jnp.concatenate([cbm * lm for lm in lmats], axis=1)
            mtw = jnp.concatenate(
                [cbt * jnp.exp(jnp.where(triu, sc["cs_t"][h:h + 1, :] - sc["csb"][h], -1e30)) for h in heads], axis=1)
            ht_g = hp_ref[0, :, gsl]
            dhn_g = dh_ref[:, gsl]
            yoff = e_g * _dot(cg, ht_g)
            y_g = _dot(mw, xdt_stack) + yoff + dsk_g * xs_g
            zg = z_ref[:, gsl]
            sz = _sigmoid(zg)
            silu = zg * sz
            yg = y_g * silu
            rs = lax.rsqrt(jnp.mean(yg * yg, axis=1, keepdims=True) + NORM_EPS)
            yn = yg * rs
            dyb = dyb_ref[:, gsl]
            dgs_ref[:, gsl] += jnp.sum(dyb * yn, axis=0, keepdims=True)
            dyn = dyb * gs_ref[:, gsl]
            dyg = rs * (dyn - yn * jnp.mean(dyn * yn, axis=1, keepdims=True))
            dy_g = dyg * silu
            dz_ref[:, gsl] = (dyg * y_g * (sz * (1.0 + zg * (1.0 - sz)))).astype(bf16)
            dy_stack = _stack_heads(dy_g, masks)
            dm_w = _dot(dy_g, xdt_stack, _NT)
            dmt_w = _dot(xdt_g, dy_stack, _NT)
            dxdt = _dot(mtw, dy_stack)
            dcb_acc = jnp.zeros((CHUNK, CHUNK), f32)
            for r, h in enumerate(heads):
                hs = slice(r * CHUNK, (r + 1) * CHUNK)
                dml = dm_w[:, hs] * lmats[r]
                dcb_acc = dcb_acc + dml
                col = jnp.sum(dml * cbm, axis=0, keepdims=True)
                row = jnp.sum(dmt_w[:, hs] * mtw[:, hs], axis=0, keepdims=True)
                dcs_t = dcs_t + jnp.where(rowh == h, row - col, 0.0)
            w = _dot(bg, dhn_g)
            dxdt = dxdt + dec_g * w
            decx3 = dec_g * (xdt_g * w)
            dg_g = e_g * dy_g
            d_c = _dot(dg_g, ht_g, _NT) + _dot(dcb_acc, bg)
            d_b = _dot(dcb_acc, cg, _TN) + _dot(xdt_g * dec_g, dhn_g, _NT)
            dh_ref[:, gsl] = dhn_g * dk_g + _dot(cg, dg_g, _TN)
            dxc_ref[:, gsl] = dsk_g * dy_g + dxdt * dt_g
            dxc_ref[:, _B0 + g * D_STATE:_B0 + (g + 1) * D_STATE] = d_b
            dxc_ref[:, _C0 + g * D_STATE:_C0 + (g + 1) * D_STATE] = d_c
            x13_ref[:, gsl] = dy_g * yoff - decx3
            x2_ref[:, gsl] = dxdt * xs_g
            rows_ref[0:1, gsl] = jnp.sum(dhn_g * ht_g, axis=0, keepdims=True)
            rows_ref[1:2, gsl] = jnp.sum(decx3, axis=0, keepdims=True)
            rows_ref[2:3, gsl] = jnp.sum(dy_g * xs_g, axis=0, keepdims=True)
        segm = seg_ref[...]
        r13 = _seg_sum(x13_ref[...], segm)
        r2 = _seg_sum(x2_ref[...], segm)
        small = _seg_sum(rows_ref[...], segm)
        lane = _iota((CHUNK, LANES), 1)
        rowi = _iota((CHUNK, LANES), 0)
        dcl_row = small[0:1, :] * jnp.exp(sc["cs"][CHUNK - 1:CHUNK, :]) + small[1:2, :]
        dcs = r13 + jnp.where(rowi == CHUNK - 1, dcl_row, 0.0)
        dcs_t_all = dcs.T + jnp.concatenate([dcs_t, jnp.zeros((LANES - N_HEADS, CHUNK), f32)], axis=0)
        dda = _dot32(dcs_t_all, tril.astype(f32)).T
        a = sc["a"]
        ddt_total = r2 + dda * a
        dalog_ref[...] += jnp.sum(dda * sc["dtv"], axis=0, keepdims=True) * a
        ddtr = jnp.where(lane < N_HEADS, ddt_total * _sigmoid(sc["xdt_pre"]), 0.0)
        ddtb_ref[...] += jnp.sum(ddtr, axis=0, keepdims=True)
        ddt_ref[...] = ddtr.astype(bf16)
        ddsk_ref[...] += small[2:3, :]
        row8 = _iota((_TAIL, _CONV_COLS), 0)
        for j in range(CONV_DIM // _CONV_COLS):
            sl = slice(j * _CONV_COLS, (j + 1) * _CONV_COLS)
            cvv = cv_ref[:, sl]
            sg = _sigmoid(cvv)
            dconv = dxc_ref[:, sl] * (sg * (1.0 + cvv * (1.0 - sg)))
            nxt = dcnext_ref[:, sl]
            cur = cur_ref[:, sl]
            dxin = dconv * cw_ref[CONV_W - 1:CONV_W, sl]
            dcw_ref[CONV_W - 1:CONV_W, sl] += jnp.sum(dconv * cur, axis=0, keepdims=True)
            for s in range(1, CONV_W):
                rolled = pltpu.roll(dconv, CHUNK - s, 0)
                bot = jnp.where(row8 < _TAIL - s, rolled[CHUNK - _TAIL:], pltpu.roll(nxt, _TAIL - s, 0))
                up = jnp.concatenate([rolled[:CHUNK - _TAIL], bot], axis=0)
                dxin = dxin + up * cw_ref[CONV_W - 1 - s:CONV_W - s, sl]
                dcw_ref[CONV_W - 1 - s:CONV_W - s, sl] += jnp.sum(up * cur, axis=0, keepdims=True)
            dcb_ref[:, sl] += jnp.sum(dconv, axis=0, keepdims=True)
            dxbc_ref[:, sl] = dxin.astype(bf16)
            dcnext_ref[:, sl] = dconv[:_TAIL]
        if ride is not None:
            pl.when(i == nc - 1)(finish)

    def chunk(w):
        return pl.BlockSpec((CHUNK, w), lambda i: (nc - 1 - i, 0))

    def const(shape):
        return pl.BlockSpec(shape, lambda i: (0,) * len(shape))

    riding = ride is not None
    return pl.pallas_call(
        body, name=name, grid=(nc,),
        in_specs=[chunk(CONV_DIM),
                  pl.BlockSpec((_TAIL, CONV_DIM), lambda i: (jnp.maximum((nc - 1 - i) * tiles - 1, 0), 0)),
                  chunk(D_INNER), chunk(LANES), pl.BlockSpec((1, D_STATE, D_INNER), lambda i: (nc - 1 - i, 0, 0)),
                  chunk(D_INNER), const((CONV_W, CONV_DIM)), const((1, CONV_DIM)),
                  const((1, LANES)), const((1, LANES)), const((1, D_INNER)), const((1, D_INNER)),
                  const((D_INNER, LANES))] + [_ANY] * riding,
        out_specs=[chunk(D_INNER), chunk(CONV_DIM), chunk(LANES), const((CONV_W, CONV_DIM)), const((1, CONV_DIM)),
                   const((1, LANES)), const((1, LANES)), const((1, LANES)), const((1, D_INNER))] + [_ANY] * riding,
        out_shape=[jax.ShapeDtypeStruct((t, D_INNER), bf16), jax.ShapeDtypeStruct((t, CONV_DIM), bf16),
                   jax.ShapeDtypeStruct((t, LANES), bf16), jax.ShapeDtypeStruct((CONV_W, CONV_DIM), f32),
                   jax.ShapeDtypeStruct((1, CONV_DIM), f32), jax.ShapeDtypeStruct((1, LANES), f32),
                   jax.ShapeDtypeStruct((1, LANES), f32), jax.ShapeDtypeStruct((1, LANES), f32),
                   jax.ShapeDtypeStruct((1, D_INNER), f32)]
        + ([jax.ShapeDtypeStruct((N_CHIPS - 1,) + ride.shape[1:], ride.dtype)] if riding else []),
        scratch_shapes=[pltpu.VMEM((D_STATE, D_INNER), f32), pltpu.VMEM((_TAIL, CONV_DIM), f32),
                        pltpu.VMEM((CHUNK, CONV_DIM), f32), pltpu.VMEM((CHUNK, CONV_DIM), f32),
                        pltpu.VMEM((CHUNK, CONV_DIM), f32), pltpu.VMEM((CHUNK, D_INNER), f32),
                        pltpu.VMEM((CHUNK, D_INNER), f32), pltpu.VMEM((_TAIL, D_INNER), f32)]
        + (list(_SCATTER_SCRATCH) if riding else []),
        compiler_params=_params(("arbitrary",)),
    )(xbc, xbc, z, dtr, hprev, dyb, cw, cb, dtb, alog, dsk_x, gs, seg, *([ride] if riding else []))


def _adamw(w, g, m, v, *, name):
    r, c = w.shape
    tr = r
    while tr * c * 4 > _MB and tr % 16 == 0:
        tr //= 2

    def body(w_ref, g_ref, m_ref, v_ref, d_ref, m2_ref, v2_ref):
        gv = g_ref[...]
        m2 = ADAM_B1 * m_ref[...] + (1.0 - ADAM_B1) * gv
        v2 = ADAM_B2 * v_ref[...] + (1.0 - ADAM_B2) * (gv * gv)
        m_hat = m2 / (1.0 - ADAM_B1 ** ADAM_STEP)
        v_hat = v2 / (1.0 - ADAM_B2 ** ADAM_STEP)
        d_ref[...] = -ADAM_LR * (m_hat / (jnp.sqrt(v_hat) + ADAM_EPS) + ADAM_WD * w_ref[...])
        m2_ref[...] = m2
        v2_ref[...] = v2

    blk = pl.BlockSpec((tr, c), lambda i: (i, 0))
    return pl.pallas_call(
        body, name=name, grid=(r // tr,),
        in_specs=[blk] * 4, out_specs=[blk] * 3,
        out_shape=[jax.ShapeDtypeStruct((r, c), f32)] * 3,
        compiler_params=_params(("parallel",)),
    )(w, g, m, v)


def _row_block(rows, cap=2304):
    return max(tr for tr in range(16, cap + 1, 16) if rows % tr == 0)


def _cast_bf16(a, *, name):
    r, c = a.shape
    tr = _row_block(r)

    def body(a_ref, o_ref):
        o_ref[...] = a_ref[...].astype(bf16)

    blk = pl.BlockSpec((tr, c), lambda i: (i, 0))
    return pl.pallas_call(
        body, name=name, grid=(r // tr,), in_specs=[blk], out_specs=blk,
        out_shape=jax.ShapeDtypeStruct((r, c), bf16), compiler_params=_params(("parallel",)),
    )(a)


_ANY = pl.BlockSpec(memory_space=pl.ANY)


def _place():
    x, y, c = lax.axis_index("x"), lax.axis_index("y"), lax.axis_index("c")
    other_chips = [(1 - x, y), (x, 1 - y), (1 - x, 1 - y)]
    return x, y, c, other_chips


def _gather_protocol(in_ref, out_ref, send_sems, recv_sems):
    x, y, c, chips = _place()
    me = 2 * x + y
    sibling = (x, y, 1 - c)

    def cp(k, chip, half, to, src=None):
        dst = out_ref.at[chip, half]
        return pltpu.make_async_remote_copy(
            src_ref=dst if src is None else src, dst_ref=dst, send_sem=send_sems.at[k], recv_sem=recv_sems.at[k],
            device_id=to, device_id_type=MESH)

    def sends():
        return [cp(j, me, c, (cx, cy, c), src=in_ref.at[c]) for j, (cx, cy) in enumerate(chips)]

    def relays():
        return [cp(3 + j, 2 * cx + cy, c, sibling) for j, (cx, cy) in enumerate(chips)]

    def start():
        for f in sends():
            f.start()

    def relay():
        onward = relays()
        for j, (cx, cy) in enumerate(chips):
            cp(j, 2 * cx + cy, c, sibling).wait_recv()
            onward[j].start()

    def finish():
        for j, (cx, cy) in enumerate(chips):
            cp(3 + j, 2 * cx + cy, 1 - c, sibling).wait_recv()
        for f in sends() + relays():
            f.wait_send()

    return start, relay, finish


_GATHER_SCRATCH = [pltpu.SemaphoreType.DMA((6,)), pltpu.SemaphoreType.DMA((6,))]


def _gather_shards(shard, *, name):
    _, rh, lanes = shard.shape

    def body(in_ref, out_ref, send_sems, recv_sems):
        start, relay, finish = _gather_protocol(in_ref, out_ref, send_sems, recv_sems)
        start()
        relay()
        finish()

    return pl.pallas_call(
        body, name=name, in_specs=[_ANY], out_specs=_ANY,
        out_shape=jax.ShapeDtypeStruct((N_CHIPS, 2, rh, lanes), shard.dtype),
        scratch_shapes=list(_GATHER_SCRATCH),
    )(shard)


def _scatter_protocol(p_ref, out_ref, send_sems, recv_sems):
    x, y, c, chips = _place()

    def copies():
        return [pltpu.make_async_remote_copy(
            src_ref=p_ref.at[2 * cx + cy], dst_ref=out_ref.at[j], send_sem=send_sems.at[j], recv_sem=recv_sems.at[j],
            device_id=(cx, cy, c), device_id_type=MESH) for j, (cx, cy) in enumerate(chips)]

    def start():
        for cpy in copies():
            cpy.start()

    def finish():
        for cpy in copies():
            cpy.wait()

    return start, finish


_SCATTER_SCRATCH = [pltpu.SemaphoreType.DMA((3,)), pltpu.SemaphoreType.DMA((3,))]


def _rs_swap_halves(g, *, name):
    nch, _, rh, lanes = g.shape

    def body(g_ref, out_ref, send_sems, recv_sems):
        x, y, c, _ = _place()
        copies = [pltpu.make_async_remote_copy(
            src_ref=g_ref.at[k, 1 - c], dst_ref=out_ref.at[k], send_sem=send_sems.at[k], recv_sem=recv_sems.at[k],
            device_id=(x, y, 1 - c), device_id_type=MESH) for k in range(nch)]
        for cpy in copies:
            cpy.start()
        for cpy in copies:
            cpy.wait()

    return pl.pallas_call(
        body, name=name, in_specs=[_ANY], out_specs=_ANY,
        out_shape=jax.ShapeDtypeStruct((nch, rh, lanes), g.dtype),
        scratch_shapes=[pltpu.SemaphoreType.DMA((nch,)), pltpu.SemaphoreType.DMA((nch,))],
    )(g)


def _rs_add_pair(g, got, c_idx, *, name):
    nch, _, rh, lanes = g.shape
    tr = _row_block(rh)

    def body(c_ref, g_ref, got_ref, p32_ref, p16_ref):
        s = g_ref[...] + got_ref[...]
        p32_ref[...] = s
        p16_ref[...] = s.astype(bf16)

    blk = pl.BlockSpec((None, tr, lanes), lambda k, i, c_ref: (k, i, 0))
    return pl.pallas_call(
        body, name=name,
        grid_spec=pltpu.PrefetchScalarGridSpec(
            num_scalar_prefetch=1, grid=(nch, rh // tr),
            in_specs=[pl.BlockSpec((None, None, tr, lanes), lambda k, i, c_ref: (k, c_ref[0], i, 0)), blk],
            out_specs=[blk, blk]),
        out_shape=[jax.ShapeDtypeStruct((nch, rh, lanes), f32), jax.ShapeDtypeStruct((nch, rh, lanes), bf16)],
        compiler_params=_params(("parallel", "parallel")),
    )(c_idx, g, got)


def _rs_add_chips(p32, got, me_idx, *, name):
    _, rh, lanes = p32.shape
    tr = _row_block(rh)

    def body(me_ref, p_ref, got_ref, o_ref):
        o_ref[...] = ((p_ref[...] + got_ref[0].astype(f32)) + got_ref[1].astype(f32)) + got_ref[2].astype(f32)

    return pl.pallas_call(
        body, name=name,
        grid_spec=pltpu.PrefetchScalarGridSpec(
            num_scalar_prefetch=1, grid=(rh // tr,),
            in_specs=[pl.BlockSpec((None, tr, lanes), lambda i, me_ref: (me_ref[0], i, 0)),
                      pl.BlockSpec((3, tr, lanes), lambda i, me_ref: (0, i, 0))],
            out_specs=pl.BlockSpec((tr, lanes), lambda i, me_ref: (i, 0))),
        out_shape=jax.ShapeDtypeStruct((rh, lanes), f32),
        compiler_params=_params(("parallel",)),
    )(me_idx, p32, got)


def _rs_join_halves(half, *, name):
    rh, lanes = half.shape

    def body(h_ref, out_ref, send_sem, recv_sem):
        x, y, c, _ = _place()
        cpy = pltpu.make_async_remote_copy(
            src_ref=h_ref, dst_ref=out_ref, send_sem=send_sem, recv_sem=recv_sem,
            device_id=(x, y, 1 - c), device_id_type=MESH)
        cpy.start()
        cpy.wait()

    return pl.pallas_call(
        body, name=name, in_specs=[_ANY], out_specs=_ANY,
        out_shape=jax.ShapeDtypeStruct((rh, lanes), half.dtype),
        scratch_shapes=[pltpu.SemaphoreType.DMA, pltpu.SemaphoreType.DMA],
    )(half)


def _all_reduce_small(s, *, name):
    rs, lanes = s.shape

    def body(s_ref, o_ref, buf_ref, send_sems, recv_sems):
        x, y, c, _ = _place()
        me = 4 * x + 2 * y + c
        peers = []
        for k in range(1, N_DEV):
            px = 1 - x if (k >> 2) & 1 else x
            py = 1 - y if (k >> 1) & 1 else y
            pc = 1 - c if k & 1 else c
            peers.append((px, py, pc))
        copies = [pltpu.make_async_remote_copy(
            src_ref=s_ref, dst_ref=buf_ref.at[me], send_sem=send_sems.at[k], recv_sem=recv_sems.at[k],
            device_id=peer, device_id_type=MESH) for k, peer in enumerate(peers)]
        for cpy in copies:
            cpy.start()
        buf_ref[me] = s_ref[...]
        for k, (px, py, pc) in enumerate(peers):
            pltpu.make_async_remote_copy(
                src_ref=s_ref, dst_ref=buf_ref.at[4 * px + 2 * py + pc], send_sem=send_sems.at[k],
                recv_sem=recv_sems.at[k], device_id=(px, py, pc), device_id_type=MESH).wait_recv()
        for cpy in copies:
            cpy.wait_send()
        acc = buf_ref[0]
        for d in range(1, N_DEV):
            acc = acc + buf_ref[d]
        o_ref[...] = acc

    vm = pl.BlockSpec(memory_space=pltpu.VMEM)
    return pl.pallas_call(
        body, name=name, in_specs=[vm], out_specs=vm,
        out_shape=jax.ShapeDtypeStruct((rs, lanes), f32),
        scratch_shapes=[pltpu.VMEM((N_DEV, rs, lanes), f32), pltpu.SemaphoreType.DMA((N_DEV - 1,)),
                        pltpu.SemaphoreType.DMA((N_DEV - 1,))],
        compiler_params=pltpu.CompilerParams(vmem_limit_bytes=32 * _MB),
    )(s)


def _pad_lanes(a, width=LANES):
    return jnp.pad(a, ((0, 0), (0, width - a.shape[1])))


def _local_grads(x, tgt, wts, small, *, fwd_ride=None, late_weights=None, bwd_ride=None, last_ride=None):
    t = x.shape[0]
    tm = min(t, 1024)
    d = D_MODEL
    mm = functools.partial(_matmul, tm=tm)

    dtb = _pad_lanes(small["dt_bias"])
    alog = _pad_lanes(small["a_log"])
    dsk = jnp.repeat(small["d_skip"], HEAD_DIM, axis=1)
    bsp_t = _pad_lanes(small["b_spatial"].T)
    wsp = small["w_spatial"]

    h = _rms_fwd(x, small["norm_mix_g"], name="rms_mix")
    uv = mm(h, wts["uv"], tn=1024, tk=d, out_dtypes=[f32], name="proj_uv")
    z = mm(h, wts["z"], tn=1024, tk=d, out_dtypes=[f32], name="proj_z")
    xbc = mm(h, wts["xbc"], tn=1024, tk=d, out_dtypes=[f32], name="proj_xbc")
    dtr = mm(h, wts["dt"], tn=LANES, tk=d, out_dtypes=[f32], name="proj_dt")
    gl = mm(h, wts["gate"], tn=1024, tk=d, out_dtypes=[f32], name="proj_gate")
    ya = _gmlp_fwd(uv, small["v_norm_g"], small["v_norm_b"], wsp, bsp_t, name="gmlp_fwd")
    yb, hprev, *gathered = _ssd_fwd(xbc, z, dtr, small["conv_w"], small["conv_b"], dtb, alog, dsk,
                                    small["ssm_norm_g"], ride=fwd_ride, name="ssd_fwd")
    if fwd_ride is not None:
        wts = {**wts, **late_weights(gathered[0])}
    pa = mm(ya, wts["pa"], tn=1024, tk=1024, out_dtypes=[f32], name="proj_a")
    pb = mm(yb, wts["pb"], tn=1024, tk=1024, out_dtypes=[f32], name="proj_b")
    merged = _merge_fwd(pa, pb, gl, small["b_gates"], name="merge_fwd")
    x1 = mm(merged, wts["out"], tn=1024, tk=1024, out_dtypes=[f32], extras=[x],
            epilogue=lambda acc, res: (res + acc,), name="out_proj")
    h2 = _rms_fwd(x1, small["norm_mlp_g"], name="rms_mlp")
    act = mm(h2, wts["up"], tn=1024, tk=d, out_dtypes=[bf16],
             epilogue=lambda acc: (jnp.square(jnp.maximum(acc, 0.0)),), name="mlp_up")
    x2 = mm(act, wts["down"], tn=1024, tk=1024, out_dtypes=[f32], extras=[x1],
            epilogue=lambda acc, res: (res + acc,), name="mlp_down")

    dx2, dx2b, dgf, loss = _loss_head(x2, tgt, small["norm_final_g"], name="loss_head")
    tt = min(t, 1024)
    tn_mm = functools.partial(_matmul_tn, tt=tt)
    dw = {}
    dw["down"] = tn_mm(act, dx2b, tka=1024, tn=1024, name="dw_down")
    dup = mm(dx2b, wts["down"], nt=True, tn=1024, tk=1024, out_dtypes=[bf16], extras=[act],
             epilogue=lambda acc, a2: (acc * (2.0 * jnp.sqrt(a2.astype(f32))),), name="d_act")
    dw["up"] = tn_mm(h2, dup, tka=1024, tn=1024, name="dw_up")
    dh2 = mm(dup, wts["up"], nt=True, tn=1024, tk=1024, out_dtypes=[f32], name="d_h2")
    dx1, dx1b, dg_mlp = _rms_bwd(x1, small["norm_mlp_g"], dh2, dx2, want_bf16=True, name="rms_mlp_bwd")
    dw["out"] = tn_mm(merged, dx1b, tka=1024, tn=1024, name="dw_out")
    dmerged = mm(dx1b, wts["out"], nt=True, tn=1024, tk=1024, out_dtypes=[f32], name="d_merged")
    dpa, dpb, dgl, dbg = _merge_bwd(dmerged, pa, pb, gl, small["b_gates"], name="merge_bwd")
    dw["pa"] = tn_mm(ya, dpa, tka=1024, tn=1024, name="dw_pa")
    dw["pb"] = tn_mm(yb, dpb, tka=1024, tn=1024, name="dw_pb")
    dya = mm(dpa, wts["pa"], nt=True, tn=1024, tk=1024, out_dtypes=[f32], name="d_ya")
    dyb = mm(dpb, wts["pb"], nt=True, tn=1024, tk=1024, out_dtypes=[f32], name="d_yb")
    duv, dwsp, dbsp_t, dvg, dvb = _gmlp_bwd(uv, dya, small["v_norm_g"], small["v_norm_b"], wsp, bsp_t,
                                            name="gmlp_bwd")
    ride = bwd_ride(dw) if bwd_ride is not None else None
    dz, dxbc, ddt, dcw, dcb, ddtb, dalog, ddsk, dgs, *got = _ssd_bwd(
        xbc, z, dtr, hprev, dyb, small["conv_w"], small["conv_b"], dtb, alog, dsk, small["ssm_norm_g"],
        _head_seg_matrix(), ride=ride, name="ssd_bwd")
    dw["uv"] = tn_mm(h, duv, tka=1024, tn=1024, name="dw_uv")
    dw["z"] = tn_mm(h, dz, tka=1024, tn=1024, name="dw_z")
    dw["xbc"] = tn_mm(h, dxbc, tka=1024, tn=1024, name="dw_xbc")
    dw["dt"] = tn_mm(h, ddt, tka=1024, tn=LANES, name="dw_dt")
    dw["gate"] = tn_mm(h, dgl, tka=1024, tn=1024, name="dw_gate")
    last = last_ride(dw) if last_ride is not None else None
    res = _matmul_nt_sum(
        [(duv, wts["uv"]), (dz, wts["z"]), (dxbc, wts["xbc"]), (dgl, wts["gate"]), (ddt, wts["dt"])],
        tm=tm, tks=[1024] * 4 + [LANES], ride=last, name="d_h")
    dh, got_last = (res[0], res[1]) if last is not None else (res, None)
    dx, dg_mix = _rms_bwd(x, small["norm_mix_g"], dh, dx1, want_bf16=False, name="rms_mix_bwd")

    dsmall = {
        "norm_mix_g": dg_mix, "conv_w": dcw, "conv_b": dcb, "dt_bias": ddtb[:, :N_HEADS], "a_log": dalog[:, :N_HEADS],
        "d_skip": ddsk[:, :N_HEADS], "ssm_norm_g": dgs, "v_norm_g": dvg, "v_norm_b": dvb, "w_spatial": dwsp,
        "b_spatial": dbsp_t[:, :GMLP_GROUPS].T, "b_gates": dbg, "norm_mlp_g": dg_mlp, "norm_final_g": dgf,
    }
    return loss, dx, dw, dsmall, (got[0] if got else None), got_last


_IN_SHARD = IN_PROJ // N_CHIPS
_DENSE = ("w_in", "w_proj_a", "w_proj_b", "w_out", "w_mlp_up", "w_mlp_down")
_DENSE_SHARD_SHAPES = {"w_in": (D_MODEL, _IN_SHARD), "w_proj_a": (GMLP_WIDTH // N_CHIPS, D_MODEL),
                       "w_proj_b": (D_INNER // N_CHIPS, D_MODEL), "w_out": (D_MODEL // N_CHIPS, D_MODEL),
                       "w_mlp_up": (D_MODEL, D_FF // N_CHIPS), "w_mlp_down": (D_FF // N_CHIPS, D_MODEL)}
_DENSE_ROWS = {k: s[0] * s[1] // LANES for k, s in _DENSE_SHARD_SHAPES.items()}
_DENSE_TOTAL = sum(_DENSE_ROWS.values())
_CONV_ROWS = CONV_W * (CONV_DIM // N_CHIPS) * 2 // LANES


def _dense_offsets():
    off, out = 0, {}
    for k in _DENSE:
        out[k] = off
        off += _DENSE_ROWS[k]
    return out


_DENSE_OFF = _dense_offsets()

_SMALL = ("norm_mix_g", "conv_w", "conv_b", "dt_bias", "a_log", "d_skip", "ssm_norm_g", "v_norm_g", "v_norm_b",
          "w_spatial", "b_spatial", "b_gates", "norm_mlp_g", "norm_final_g")


def _pack_small(parts):
    flat = jnp.concatenate([parts[k].reshape(-1) for k in _SMALL])
    rows = -(-flat.shape[0] // (8 * LANES)) * 8
    return jnp.pad(flat, (0, rows * LANES - flat.shape[0])).reshape(rows, LANES)


def _unpack_small(packed, shapes):
    flat = packed.reshape(-1)
    out, off = {}, 0
    for k in _SMALL:
        n = math.prod(shapes[k])
        out[k] = flat[off:off + n].reshape(shapes[k])
        off += n
    return out


def _from_chip_columns(stacked, rows, cols):
    return stacked.reshape(N_CHIPS, rows, cols).transpose(1, 0, 2).reshape(rows, N_CHIPS * cols)


def _to_chip_columns(full, cols):
    rows = full.shape[0]
    return full.reshape(rows, N_CHIPS, cols).transpose(1, 0, 2).reshape(N_CHIPS, rows * cols // LANES, LANES)


def kernel(x, norm_mix_g, w_in, conv_w, conv_b, dt_bias, a_log, d_skip, ssm_norm_g, v_norm_g, v_norm_b, w_spatial, b_spatial, b_gates, w_proj_a, w_proj_b, w_out, norm_mlp_g, w_mlp_up, w_mlp_down, norm_final_g, loss_target, m_norm_mix_g, m_w_in, m_conv_w, m_conv_b, m_dt_bias, m_a_log, m_d_skip, m_ssm_norm_g, m_v_norm_g, m_v_norm_b, m_w_spatial, m_b_spatial, m_b_gates, m_w_proj_a, m_w_proj_b, m_w_out, m_norm_mlp_g, m_w_mlp_up, m_w_mlp_down, m_norm_final_g, v_norm_mix_g, v_w_in, v_conv_w, v_conv_b, v_dt_bias, v_a_log, v_d_skip, v_ssm_norm_g, v_v_norm_g, v_v_norm_b, v_w_spatial, v_b_spatial, v_b_gates, v_w_proj_a, v_w_proj_b, v_w_out, v_norm_mlp_g, v_w_mlp_up, v_w_mlp_down, v_norm_final_g):
    given = dict(locals())
    names = ("norm_mix_g", "w_in", "conv_w", "conv_b", "dt_bias", "a_log", "d_skip", "ssm_norm_g", "v_norm_g",
             "v_norm_b", "w_spatial", "b_spatial", "b_gates", "w_proj_a", "w_proj_b", "w_out", "norm_mlp_g",
             "w_mlp_up", "w_mlp_down", "norm_final_g")
    xi, yi, ci = lax.axis_index("x"), lax.axis_index("y"), lax.axis_index("c")
    me_chip = (2 * xi + yi).astype(jnp.int32)

    in_rows = _DENSE_ROWS["w_in"]
    late_rows = _DENSE_TOTAL - in_rows
    w_in_b16 = _cast_bf16(w_in[0].reshape(in_rows, LANES), name="cast_w_in")
    late_b16 = _cast_bf16(jnp.concatenate([given[k][0].reshape(-1, LANES) for k in _DENSE[1:]]), name="cast_w_late")
    conv_shard = conv_w.reshape(CONV_W, CONV_DIM // N_CHIPS)
    conv_bits = lax.bitcast_convert_type(conv_shard.reshape(-1, LANES), bf16).reshape(_CONV_ROWS, LANES)
    shard_in = jnp.concatenate([w_in_b16, conv_bits]).reshape(2, (in_rows + _CONV_ROWS) // 2, LANES)
    shard_late = late_b16.reshape(2, late_rows // 2, LANES)

    def with_own(got, shard):
        return lax.dynamic_update_slice(got, shard[None], (me_chip, 0, 0, 0)).reshape(N_CHIPS, -1, LANES)

    g_in = with_own(_gather_shards(shard_in, name="gather_w_in"), shard_in)
    w_in_full = _from_chip_columns(g_in[:, :in_rows], D_MODEL, _IN_SHARD)
    o_dt, o_gate = 2 * GMLP_WIDTH + D_INNER + CONV_DIM, 2 * GMLP_WIDTH + D_INNER + CONV_DIM + N_HEADS
    wts = {
        "uv": w_in_full[:, :2 * GMLP_WIDTH], "z": w_in_full[:, 2 * GMLP_WIDTH:2 * GMLP_WIDTH + D_INNER],
        "xbc": w_in_full[:, 2 * GMLP_WIDTH + D_INNER:o_dt], "dt": _pad_lanes(w_in_full[:, o_dt:o_gate]),
        "gate": w_in_full[:, o_gate:],
    }
    conv_all = lax.bitcast_convert_type(g_in[:, in_rows:].reshape(N_CHIPS, _CONV_ROWS // 2, LANES, 2), f32)
    conv_full = conv_all.reshape(N_CHIPS, CONV_W, CONV_DIM // N_CHIPS).transpose(1, 0, 2).reshape(CONV_W, CONV_DIM)

    def late_weights(got):
        g_late = with_own(got, shard_late)

        def rows_of(k):
            off = _DENSE_OFF[k] - in_rows
            return g_late[:, off:off + _DENSE_ROWS[k]]

        return {
            "pa": rows_of("w_proj_a").reshape(GMLP_WIDTH, D_MODEL),
            "pb": rows_of("w_proj_b").reshape(D_INNER, D_MODEL), "out": rows_of("w_out").reshape(D_MODEL, D_MODEL),
            "up": _from_chip_columns(rows_of("w_mlp_up"), D_MODEL, D_FF // N_CHIPS),
            "down": rows_of("w_mlp_down").reshape(D_FF, D_MODEL),
        }

    small = {
        "norm_mix_g": norm_mix_g, "conv_w": conv_full, "conv_b": conv_b, "dt_bias": dt_bias, "a_log": a_log,
        "d_skip": d_skip, "ssm_norm_g": ssm_norm_g, "v_norm_g": v_norm_g, "v_norm_b": v_norm_b,
        "w_spatial": w_spatial[0], "b_spatial": b_spatial[0], "b_gates": b_gates, "norm_mlp_g": norm_mlp_g,
        "norm_final_g": norm_final_g.reshape(1, D_MODEL),
    }

    c_idx = ci.astype(jnp.int32).reshape(1)
    me_idx = me_chip.reshape(1)
    partials = {}

    def pair_sums(g, tag):
        g = g.reshape(N_CHIPS, 2, -1, LANES)
        p32, p16 = _rs_add_pair(g, _rs_swap_halves(g, name="rs_swap_" + tag), c_idx, name="rs_add_pair_" + tag)
        partials[tag] = p32
        return p16

    def reduced_shard(tag, got_chips):
        half = _rs_add_chips(partials[tag], got_chips, me_idx, name="rs_add_chips_" + tag)
        other_half = _rs_join_halves(half, name="rs_join_" + tag)
        return jnp.where(ci == 0, jnp.concatenate([half, other_half]), jnp.concatenate([other_half, half]))

    def late_partials(dw):
        return pair_sums(jnp.concatenate(
            [dw["pa"].reshape(N_CHIPS, -1, LANES), dw["pb"].reshape(N_CHIPS, -1, LANES),
             dw["out"].reshape(N_CHIPS, -1, LANES), _to_chip_columns(dw["up"], D_FF // N_CHIPS),
             dw["down"].reshape(N_CHIPS, -1, LANES)], axis=1), "late")

    def in_partials(dw):
        dw_in = jnp.concatenate([dw["uv"], dw["z"], dw["xbc"], dw["dt"][:, :N_HEADS], dw["gate"]], axis=1)
        return pair_sums(_to_chip_columns(dw_in, _IN_SHARD), "in")

    loss_part, grad_x, dw, dsmall, got_late, got_in = _local_grads(
        x[0], loss_target[0], wts, small, fwd_ride=shard_late, late_weights=late_weights, bwd_ride=late_partials,
        last_ride=in_partials)
    loss = lax.psum(loss_part[0, 0], ("x", "y", "c"))
    g_late = reduced_shard("late", got_late)
    g_in_shard = reduced_shard("in", got_in)

    small_shapes = {k: dsmall[k].shape for k in _SMALL}
    red = _unpack_small(_all_reduce_small(_pack_small(dsmall), name="all_reduce_small"), small_shapes)
    conv_cols = CONV_DIM // N_CHIPS
    red["conv_w"] = lax.dynamic_slice_in_dim(red["conv_w"], me_chip * conv_cols, conv_cols, axis=1)

    grads, deltas, new_m, new_v = {}, {}, {}, {}
    for k in _DENSE:
        shp = _DENSE_SHARD_SHAPES[k]
        src, off = (g_in_shard, 0) if k == "w_in" else (g_late, _DENSE_OFF[k] - in_rows)
        g2 = src[off:off + _DENSE_ROWS[k]].reshape(shp)
        dlt, m2, v2 = _adamw(given[k][0], g2, given["m_" + k][0], given["v_" + k][0], name="adamw_" + k)
        grads[k], deltas[k], new_m[k], new_v[k] = g2, dlt, m2, v2
    adam_shapes = dict(small_shapes)
    adam_shapes["conv_w"] = (CONV_W, conv_cols)

    def small_pack_of(prefix):
        return _pack_small({k: given[prefix + k].reshape(adam_shapes[k]) for k in _SMALL})

    dlt_s, m_s, v_s = _adamw(small_pack_of(""), _pack_small(red), small_pack_of("m_"), small_pack_of("v_"),
                             name="adamw_small")
    for dst, packed in ((deltas, dlt_s), (new_m, m_s), (new_v, v_s)):
        dst.update(_unpack_small(packed, adam_shapes))
    grads.update(red)

    def shaped(dct):
        return [dct[k].reshape(given[k].shape) for k in names]

    return (loss, grad_x[None], *shaped(grads), *shaped(deltas), *shaped(new_m), *shaped(new_v))
```

```python
import functools
import math

import jax
import jax.numpy as jnp
from jax import lax
from jax.experimental import pallas as pl
from jax.experimental.pallas import tpu as pltpu

f32 = jnp.float32
bf16 = jnp.bfloat16

D_MODEL = 1024
CHUNK = 128
GMLP_WIDTH = 1024
GMLP_GROUPS = 8
D_INNER = 2048
HEAD_DIM = 64
N_HEADS = 32
N_GROUPS = 8
HEADS_PER_GROUP = 4
GROUP_W = HEADS_PER_GROUP * HEAD_DIM
D_STATE = 128
CONV_W = 4
CONV_DIM = 4096
D_FF = 4096
IN_PROJ = 10272
NORM_EPS = 1e-6
N_CHIPS = 4
N_DEV = 8
LANES = 128

ADAM_LR = 0.001
ADAM_B1 = 0.9
ADAM_B2 = 0.999
ADAM_EPS = 1e-08
ADAM_WD = 0.01
ADAM_STEP = 10

MESH = pl.DeviceIdType.MESH
_NT = (((1,), (1,)), ((), ()))
_NN = (((1,), (0,)), ((), ()))
_TN = (((0,), (0,)), ((), ()))
_MB = 2 ** 20


def _params(sem, vmem_mb=48):
    return pltpu.CompilerParams(dimension_semantics=sem, vmem_limit_bytes=vmem_mb * _MB)


def _dot(a, b, dims=_NN):
    return lax.dot_general(a.astype(bf16), b.astype(bf16), dims, preferred_element_type=f32)


def _dot32(a, b):
    return jnp.dot(a, b, preferred_element_type=f32, precision=lax.Precision.HIGHEST)


def _sigmoid(x):
    return 1.0 / (1.0 + jnp.exp(-x))


def _sum_all(a):
    return jnp.sum(jnp.sum(a, axis=1, keepdims=True), axis=0, keepdims=True)


def _iota(shape, dim):
    return lax.broadcasted_iota(jnp.int32, shape, dim)


def _matmul(a, b, *, nt=False, tm, tn, tk, out_dtypes, epilogue=None, extras=(), name):
    m, k_dim = a.shape
    n = b.shape[0] if nt else b.shape[1]
    nk = k_dim // tk
    ne, no = len(extras), len(out_dtypes)
    dims = _NT if nt else _NN

    def body(*refs):
        a_ref, b_ref = refs[0], refs[1]
        ex = refs[2:2 + ne]
        outs = refs[2 + ne:2 + ne + no]

        def finish(acc):
            vals = epilogue(acc, *[e[...] for e in ex]) if epilogue is not None else (acc,)
            for o, v in zip(outs, vals):
                o[...] = v.astype(o.dtype)

        part = lax.dot_general(a_ref[...], b_ref[...], dims, preferred_element_type=f32)
        if nk == 1:
            finish(part)
        else:
            acc_ref = refs[-1]
            kk = pl.program_id(2)

            @pl.when(kk == 0)
            def _():
                acc_ref[...] = part

            @pl.when(kk > 0)
            def _():
                acc_ref[...] += part

            @pl.when(kk == nk - 1)
            def _():
                finish(acc_ref[...])

    b_spec = pl.BlockSpec((tn, tk), lambda i, j, k: (j, k)) if nt else pl.BlockSpec((tk, tn), lambda i, j, k: (k, j))
    tile = pl.BlockSpec((tm, tn), lambda i, j, k: (i, j))
    outs = pl.pallas_call(
        body, name=name, grid=(m // tm, n // tn, nk),
        in_specs=[pl.BlockSpec((tm, tk), lambda i, j, k: (i, k)), b_spec] + [tile] * ne,
        out_specs=[tile] * no,
        out_shape=[jax.ShapeDtypeStruct((m, n), dt) for dt in out_dtypes],
        scratch_shapes=[pltpu.VMEM((tm, tn), f32)] if nk > 1 else [],
        compiler_params=_params(("parallel", "parallel", "arbitrary")),
    )(a, b, *extras)
    return outs if no > 1 else outs[0]


def _matmul_nt_sum(pairs, *, tm, tks, ride=None, name):
    m = pairs[0][0].shape[0]
    n = pairs[0][1].shape[0]
    nblk = [a.shape[1] // tk for (a, _), tk in zip(pairs, tks)]
    starts = [sum(nblk[:p]) for p in range(len(pairs))]
    nk = sum(nblk)
    npairs = len(pairs)
    ni = m // tm
    riding = ride is not None

    def body(*refs):
        rest = refs[2 * npairs:]
        if riding:
            ride_ref, o_ref, got_ref, acc_ref, send_sems, recv_sems = rest
        else:
            o_ref, acc_ref = rest
        i, kk = pl.program_id(0), pl.program_id(1)
        if riding:
            start, finish = _scatter_protocol(ride_ref, got_ref, send_sems, recv_sems)
            pl.when((i == 0) & (kk == 0))(start)

        @pl.when(kk == 0)
        def _():
            acc_ref[...] = jnp.zeros_like(acc_ref)

        for p in range(npairs):
            @pl.when((kk >= starts[p]) & (kk < starts[p] + nblk[p]))
            def _(p=p):
                acc_ref[...] += lax.dot_general(refs[2 * p][...], refs[2 * p + 1][...], _NT, preferred_element_type=f32)

        @pl.when(kk == nk - 1)
        def _():
            o_ref[...] = acc_ref[...]

        if riding:
            pl.when((i == ni - 1) & (kk == nk - 1))(finish)

    in_specs, args = [], []
    for p, (a, b) in enumerate(pairs):
        def kblock(k, s=starts[p], nb=nblk[p]):
            return jnp.clip(k - s, 0, nb - 1)
        in_specs.append(pl.BlockSpec((tm, tks[p]), lambda i, k, kb=kblock: (i, kb(k))))
        in_specs.append(pl.BlockSpec((n, tks[p]), lambda i, k, kb=kblock: (0, kb(k))))
        args += [a, b]
    tile = pl.BlockSpec((tm, n), lambda i, k: (i, 0))
    outs = pl.pallas_call(
        body, name=name, grid=(ni, nk), in_specs=in_specs + [_ANY] * riding, out_specs=[tile] + [_ANY] * riding,
        out_shape=[jax.ShapeDtypeStruct((m, n), f32)]
        + ([jax.ShapeDtypeStruct((N_CHIPS - 1,) + ride.shape[1:], ride.dtype)] if riding else []),
        scratch_shapes=[pltpu.VMEM((tm, n), f32)] + (list(_SCATTER_SCRATCH) if riding else []),
        compiler_params=_params(("arbitrary", "arbitrary"), vmem_mb=56),
    )(*args, *([ride] if riding else []))
    return outs if riding else outs[0]


def _matmul_tn(a, b, *, tka, tn, tt, name):
    t, ka = a.shape
    n = b.shape[1]

    def body(a_ref, b_ref, o_ref):
        part = lax.dot_general(a_ref[...], b_ref[...], _TN, preferred_element_type=f32)
        kk = pl.program_id(2)

        @pl.when(kk == 0)
        def _():
            o_ref[...] = part

        @pl.when(kk > 0)
        def _():
            o_ref[...] += part

    return pl.pallas_call(
        body, name=name, grid=(ka // tka, n // tn, t // tt),
        in_specs=[pl.BlockSpec((tt, tka), lambda i, j, k: (k, i)), pl.BlockSpec((tt, tn), lambda i, j, k: (k, j))],
        out_specs=pl.BlockSpec((tka, tn), lambda i, j, k: (i, j)),
        out_shape=jax.ShapeDtypeStruct((ka, n), f32),
        compiler_params=_params(("parallel", "parallel", "arbitrary")),
    )(a, b)


def _row_tile(t):
    return min(t, 512)


def _rms_fwd(x, g, *, name):
    t, d = x.shape
    tr = _row_tile(t)

    def body(x_ref, g_ref, h_ref):
        xv = x_ref[...]
        r = lax.rsqrt(jnp.mean(xv * xv, axis=1, keepdims=True) + NORM_EPS)
        h_ref[...] = (xv * r * g_ref[...]).astype(bf16)

    return pl.pallas_call(
        body, name=name, grid=(t // tr,),
        in_specs=[pl.BlockSpec((tr, d), lambda i: (i, 0)), pl.BlockSpec((1, d), lambda i: (0, 0))],
        out_specs=pl.BlockSpec((tr, d), lambda i: (i, 0)),
        out_shape=jax.ShapeDtypeStruct((t, d), bf16),
        compiler_params=_params(("parallel",)),
    )(x, g)


def _rms_bwd(xin, g, dh, dres, *, want_bf16, name):
    t, d = xin.shape
    tr = _row_tile(t)

    def body(x_ref, g_ref, dh_ref, dres_ref, dx_ref, *rest):
        dg_ref = rest[-1]
        xv = x_ref[...]
        r = lax.rsqrt(jnp.mean(xv * xv, axis=1, keepdims=True) + NORM_EPS)
        xn = xv * r
        dhv = dh_ref[...]
        dxn = dhv * g_ref[...]
        dx = dres_ref[...] + r * (dxn - xn * jnp.mean(dxn * xn, axis=1, keepdims=True))
        dx_ref[...] = dx
        if want_bf16:
            rest[0][...] = dx.astype(bf16)
        part = jnp.sum(dhv * xn, axis=0, keepdims=True)

        @pl.when(pl.program_id(0) == 0)
        def _():
            dg_ref[...] = part

        @pl.when(pl.program_id(0) > 0)
        def _():
            dg_ref[...] += part

    row = pl.BlockSpec((tr, d), lambda i: (i, 0))
    vec = pl.BlockSpec((1, d), lambda i: (0, 0))
    out_shape = [jax.ShapeDtypeStruct((t, d), f32)] + ([jax.ShapeDtypeStruct((t, d), bf16)] if want_bf16 else []) \
        + [jax.ShapeDtypeStruct((1, d), f32)]
    return pl.pallas_call(
        body, name=name, grid=(t // tr,),
        in_specs=[row, vec, row, row],
        out_specs=[row] + ([row] if want_bf16 else []) + [vec],
        out_shape=out_shape,
        compiler_params=_params(("arbitrary",)),
    )(xin, g, dh, dres)


def _loss_head(x2, tgt, g, *, name):
    t, d = x2.shape
    tr = _row_tile(t)

    def body(x_ref, t_ref, g_ref, dx_ref, dxb_ref, dg_ref, loss_ref):
        xv = x_ref[...]
        gv = g_ref[...]
        r = lax.rsqrt(jnp.mean(xv * xv, axis=1, keepdims=True) + NORM_EPS)
        xn = xv * r
        e = xn * gv - t_ref[...]
        lpart = jnp.zeros((1, LANES), f32) + 0.5 * _sum_all(jnp.mean(e * e, axis=1, keepdims=True))
        dy = e * (1.0 / d)
        dxn = dy * gv
        dx = r * (dxn - xn * jnp.mean(dxn * xn, axis=1, keepdims=True))
        dx_ref[...] = dx
        dxb_ref[...] = dx.astype(bf16)
        gpart = jnp.sum(dy * xn, axis=0, keepdims=True)

        @pl.when(pl.program_id(0) == 0)
        def _():
            dg_ref[...] = gpart
            loss_ref[...] = lpart

        @pl.when(pl.program_id(0) > 0)
        def _():
            dg_ref[...] += gpart
            loss_ref[...] += lpart

    row = pl.BlockSpec((tr, d), lambda i: (i, 0))
    vec = pl.BlockSpec((1, d), lambda i: (0, 0))
    return pl.pallas_call(
        body, name=name, grid=(t // tr,),
        in_specs=[row, row, vec],
        out_specs=[row, row, vec, pl.BlockSpec((1, LANES), lambda i: (0, 0))],
        out_shape=[jax.ShapeDtypeStruct((t, d), f32), jax.ShapeDtypeStruct((t, d), bf16),
                   jax.ShapeDtypeStruct((1, d), f32), jax.ShapeDtypeStruct((1, LANES), f32)],
        compiler_params=_params(("arbitrary",)),
    )(x2, tgt, g)


def _merge_fwd(pa, pb, gl, bg, *, name):
    t, d = pa.shape
    tr = _row_tile(t)

    def body(pa_ref, pb_ref, gla_ref, glb_ref, bga_ref, bgb_ref, o_ref):
        ga = _sigmoid(gla_ref[...] + bga_ref[...])
        gb = _sigmoid(glb_ref[...] + bgb_ref[...])
        o_ref[...] = (ga * pa_ref[...] + gb * pb_ref[...]).astype(bf16)

    row = pl.BlockSpec((tr, d), lambda i: (i, 0))
    return pl.pallas_call(
        body, name=name, grid=(t // tr,),
        in_specs=[row, row, row, pl.BlockSpec((tr, d), lambda i: (i, 1)),
                  pl.BlockSpec((1, d), lambda i: (0, 0)), pl.BlockSpec((1, d), lambda i: (0, 1))],
        out_specs=row,
        out_shape=jax.ShapeDtypeStruct((t, d), bf16),
        compiler_params=_params(("parallel",)),
    )(pa, pb, gl, gl, bg, bg)


def _merge_bwd(dm, pa, pb, gl, bg, *, name):
    t, d = pa.shape
    tr = _row_tile(t)

    def body(dm_ref, pa_ref, pb_ref, gla_ref, glb_ref, bga_ref, bgb_ref, dpa_ref, dpb_ref, dgl_ref, dbg_ref):
        dmv = dm_ref[...]
        ga = _sigmoid(gla_ref[...] + bga_ref[...])
        gb = _sigmoid(glb_ref[...] + bgb_ref[...])
        dpa_ref[...] = (dmv * ga).astype(bf16)
        dpb_ref[...] = (dmv * gb).astype(bf16)
        dla = dmv * pa_ref[...] * ga * (1.0 - ga)
        dlb = dmv * pb_ref[...] * gb * (1.0 - gb)
        dgl_ref[:, :d] = dla.astype(bf16)
        dgl_ref[:, d:] = dlb.astype(bf16)
        sa = jnp.sum(dla, axis=0, keepdims=True)
        sb = jnp.sum(dlb, axis=0, keepdims=True)

        @pl.when(pl.program_id(0) == 0)
        def _():
            dbg_ref[:, :d] = sa
            dbg_ref[:, d:] = sb

        @pl.when(pl.program_id(0) > 0)
        def _():
            dbg_ref[:, :d] += sa
            dbg_ref[:, d:] += sb

    row = pl.BlockSpec((tr, d), lambda i: (i, 0))
    return pl.pallas_call(
        body, name=name, grid=(t // tr,),
        in_specs=[row, row, row, row, pl.BlockSpec((tr, d), lambda i: (i, 1)),
                  pl.BlockSpec((1, d), lambda i: (0, 0)), pl.BlockSpec((1, d), lambda i: (0, 1))],
        out_specs=[row, row, pl.BlockSpec((tr, 2 * d), lambda i: (i, 0)), pl.BlockSpec((1, 2 * d), lambda i: (0, 0))],
        out_shape=[jax.ShapeDtypeStruct((t, d), bf16), jax.ShapeDtypeStruct((t, d), bf16),
                   jax.ShapeDtypeStruct((t, 2 * d), bf16), jax.ShapeDtypeStruct((1, 2 * d), f32)],
        compiler_params=_params(("arbitrary",)),
    )(dm, pa, pb, gl, gl, bg, bg)


_INV_SQRT2 = 1.0 / math.sqrt(2.0)
_INV_SQRT2PI = 1.0 / math.sqrt(2.0 * math.pi)


def _gelu(x):
    return 0.5 * x * (1.0 + lax.erf(x * _INV_SQRT2))


def _gelu_grad(x):
    return 0.5 * (1.0 + lax.erf(x * _INV_SQRT2)) + x * jnp.exp(-0.5 * x * x) * _INV_SQRT2PI


def _gmlp_common(uv, vg, vb):
    zz = _gelu(uv)
    u = zz[:, :GMLP_WIDTH]
    v = zz[:, GMLP_WIDTH:]
    mu = jnp.mean(v, axis=1, keepdims=True)
    vc = v - mu
    rstd = lax.rsqrt(jnp.mean(vc * vc, axis=1, keepdims=True) + NORM_EPS)
    vhat = vc * rstd
    vn = vhat * vg + vb
    return u, vhat, rstd, vn


def _gmlp_fwd(uv, vg, vb, wsp, bsp_t, *, name):
    t = uv.shape[0]
    nc = t // CHUNK

    def body(uv_ref, vg_ref, vb_ref, w_ref, b_ref, y_ref):
        u, _, _, vn = _gmlp_common(uv_ref[...], vg_ref[...], vb_ref[...])
        tril = _iota((CHUNK, CHUNK), 0) >= _iota((CHUNK, CHUNK), 1)
        bt = b_ref[...]
        for g in range(GMLP_GROUPS):
            sl = slice(g * CHUNK, (g + 1) * CHUNK)
            w = jnp.where(tril, w_ref[g], 0.0)
            s = _dot(w, vn[:, sl]) + bt[:, g:g + 1]
            y_ref[:, sl] = (u[:, sl] * s).astype(bf16)

    return pl.pallas_call(
        body, name=name, grid=(nc,),
        in_specs=[pl.BlockSpec((CHUNK, 2 * GMLP_WIDTH), lambda c: (c, 0)),
                  pl.BlockSpec((1, GMLP_WIDTH), lambda c: (0, 0)), pl.BlockSpec((1, GMLP_WIDTH), lambda c: (0, 0)),
                  pl.BlockSpec((GMLP_GROUPS, CHUNK, CHUNK), lambda c: (0, 0, 0)),
                  pl.BlockSpec((CHUNK, LANES), lambda c: (0, 0))],
        out_specs=pl.BlockSpec((CHUNK, GMLP_WIDTH), lambda c: (c, 0)),
        out_shape=jax.ShapeDtypeStruct((t, GMLP_WIDTH), bf16),
        compiler_params=_params(("parallel",)),
    )(uv, vg, vb, wsp, bsp_t)


def _gmlp_bwd(uv, dya, vg, vb, wsp, bsp_t, *, name):
    t = uv.shape[0]
    nc = t // CHUNK

    def body(uv_ref, dy_ref, vg_ref, vb_ref, w_ref, b_ref, duv_ref, dw_ref, db_ref, dvg_ref, dvb_ref):
        first = pl.program_id(0) == 0

        @pl.when(first)
        def _():
            dw_ref[...] = jnp.zeros_like(dw_ref)
            db_ref[...] = jnp.zeros_like(db_ref)
            dvg_ref[...] = jnp.zeros_like(dvg_ref)
            dvb_ref[...] = jnp.zeros_like(dvb_ref)

        uvv = uv_ref[...]
        vgv = vg_ref[...]
        u, vhat, rstd, vn = _gmlp_common(uvv, vgv, vb_ref[...])
        dy = dy_ref[...]
        tril = _iota((CHUNK, CHUNK), 0) >= _iota((CHUNK, CHUNK), 1)
        lane = _iota((CHUNK, LANES), 1)
        bt = b_ref[...]
        ds_all = dy * u
        dbacc = jnp.zeros((CHUNK, LANES), f32)
        dvh_parts = []
        for g in range(GMLP_GROUPS):
            sl = slice(g * CHUNK, (g + 1) * CHUNK)
            w = jnp.where(tril, w_ref[g], 0.0)
            vng = vn[:, sl]
            s = _dot(w, vng) + bt[:, g:g + 1]
            ds = ds_all[:, sl]
            duv_ref[:, sl] = (dy[:, sl] * s * _gelu_grad(uvv[:, sl])).astype(bf16)
            dw_ref[g] += jnp.where(tril, _dot(ds, vng, _NT), 0.0)
            dbacc = dbacc + jnp.where(lane == g, jnp.sum(ds, axis=1, keepdims=True), 0.0)
            dvn = _dot(w, ds, _TN)
            vh = vhat[:, sl]
            dvg_ref[:, sl] += jnp.sum(dvn * vh, axis=0, keepdims=True)
            dvb_ref[:, sl] += jnp.sum(dvn, axis=0, keepdims=True)
            dvh_parts.append(dvn * vgv[:, sl])
        db_ref[...] += dbacc
        dvhat = jnp.concatenate(dvh_parts, axis=1)
        m1 = jnp.mean(dvhat, axis=1, keepdims=True)
        m2 = jnp.mean(dvhat * vhat, axis=1, keepdims=True)
        dv = rstd * (dvhat - m1 - vhat * m2)
        duv_ref[:, GMLP_WIDTH:] = (dv * _gelu_grad(uvv[:, GMLP_WIDTH:])).astype(bf16)

    vec = pl.BlockSpec((1, GMLP_WIDTH), lambda c: (0, 0))
    return pl.pallas_call(
        body, name=name, grid=(nc,),
        in_specs=[pl.BlockSpec((CHUNK, 2 * GMLP_WIDTH), lambda c: (c, 0)),
                  pl.BlockSpec((CHUNK, GMLP_WIDTH), lambda c: (c, 0)), vec, vec,
                  pl.BlockSpec((GMLP_GROUPS, CHUNK, CHUNK), lambda c: (0, 0, 0)),
                  pl.BlockSpec((CHUNK, LANES), lambda c: (0, 0))],
        out_specs=[pl.BlockSpec((CHUNK, 2 * GMLP_WIDTH), lambda c: (c, 0)),
                   pl.BlockSpec((GMLP_GROUPS, CHUNK, CHUNK), lambda c: (0, 0, 0)),
                   pl.BlockSpec((CHUNK, LANES), lambda c: (0, 0)), vec, vec],
        out_shape=[jax.ShapeDtypeStruct((t, 2 * GMLP_WIDTH), bf16),
                   jax.ShapeDtypeStruct((GMLP_GROUPS, CHUNK, CHUNK), f32),
                   jax.ShapeDtypeStruct((CHUNK, LANES), f32),
                   jax.ShapeDtypeStruct((1, GMLP_WIDTH), f32), jax.ShapeDtypeStruct((1, GMLP_WIDTH), f32)],
        compiler_params=_params(("arbitrary",)),
    )(uv, dya, vg, vb, wsp, bsp_t)


_CONV_COLS = 512
_XS0, _B0, _C0 = 0, D_INNER, D_INNER + N_GROUPS * D_STATE


_TAIL = 8


def _conv_silu(cur_ref, tail_ref, w_ref, b_ref, has_prev, xc_ref, cv_ref):
    row = _iota((_TAIL, _CONV_COLS), 0)
    for j in range(CONV_DIM // _CONV_COLS):
        sl = slice(j * _CONV_COLS, (j + 1) * _CONV_COLS)
        cur = cur_ref[:, sl]
        tail = jnp.where(has_prev, tail_ref[:, sl], 0.0)
        acc = cur * w_ref[CONV_W - 1:CONV_W, sl] + b_ref[:, sl]
        for s in range(1, CONV_W):
            rolled = pltpu.roll(cur, s, 0)
            top = jnp.where(row >= s, rolled[:_TAIL], pltpu.roll(tail, s, 0))
            sh = jnp.concatenate([top, rolled[_TAIL:]], axis=0)
            acc = acc + sh * w_ref[CONV_W - 1 - s:CONV_W - s, sl]
        cv_ref[:, sl] = acc
        xc_ref[:, sl] = acc * _sigmoid(acc)


def _col_bcast(mat, h):
    return jnp.broadcast_to(mat[:, h:h + 1], (CHUNK, LANES))


def _head_expand(cols):
    lo = _iota((CHUNK, LANES), 1) < HEAD_DIM
    return jnp.concatenate([jnp.where(lo, cols[2 * j], cols[2 * j + 1]) for j in range(N_HEADS // 2)], axis=1)


def _ssd_chunk_scalars(dtr, dtb, alog):
    xdt_pre = dtr + dtb
    dtv = jnp.maximum(xdt_pre, 0.0) + jnp.log(1.0 + jnp.exp(-jnp.abs(xdt_pre)))
    a = -jnp.exp(alog)
    ltri = (_iota((CHUNK, CHUNK), 0) >= _iota((CHUNK, CHUNK), 1)).astype(f32)
    cs = _dot32(ltri, dtv * a)
    csb = [_col_bcast(cs, h) for h in range(N_HEADS)]
    cs_x = _head_expand(csb)
    dt_x = _head_expand([_col_bcast(dtv, h) for h in range(N_HEADS)])
    cl_x = cs_x[CHUNK - 1:CHUNK, :]
    return dict(xdt_pre=xdt_pre, dtv=dtv, a=a, cs=cs, cs_t=cs.T, csb=csb, dt_x=dt_x, e_x=jnp.exp(cs_x),
                dec_x=jnp.exp(cl_x - cs_x), dk_x=jnp.exp(cl_x))


def _head_masks():
    lane = _iota((CHUNK, GROUP_W), 1)
    return [(lane >= r * HEAD_DIM) & (lane < (r + 1) * HEAD_DIM) for r in range(HEADS_PER_GROUP)]


def _stack_heads(a, masks):
    return jnp.concatenate([jnp.where(m, a, 0.0) for m in masks], axis=0).astype(bf16)


def _seg_sum(a, seg):
    hi = a.astype(jnp.bfloat16)
    lo = (a - hi.astype(f32)).astype(jnp.bfloat16)
    return (lax.dot_general(hi, seg, _NN, preferred_element_type=f32)
            + lax.dot_general(lo, seg, _NN, preferred_element_type=f32))


def _head_seg_matrix():
    return (_iota((D_INNER, LANES), 0) // HEAD_DIM == _iota((D_INNER, LANES), 1)).astype(jnp.bfloat16)


def _ssd_fwd(xbc, z, dtr, cw, cb, dtb, alog, dsk_x, gs, *, ride=None, name):
    t = xbc.shape[0]
    nc = t // CHUNK
    tiles = CHUNK // _TAIL

    def body(*refs):
        cur_ref, tail_ref, z_ref, dtr_ref, cw_ref, cb_ref, dtb_ref, alog_ref, dsk_ref, gs_ref = refs[:10]
        if ride is None:
            yb_ref, hp_ref, cv_ref, state_ref, xc_ref = refs[10:]
        else:
            ride_ref, yb_ref, hp_ref, cv_ref, got_ref, state_ref, xc_ref, send_sems, recv_sems = refs[10:]
        c = pl.program_id(0)
        if ride is not None:
            start, relay, finish = _gather_protocol(ride_ref, got_ref, send_sems, recv_sems)
            pl.when(c == 0)(start)
            pl.when(c == nc // 2)(relay)

        @pl.when(c == 0)
        def _():
            state_ref[...] = jnp.zeros_like(state_ref)

        _conv_silu(cur_ref, tail_ref, cw_ref, cb_ref, c > 0, xc_ref, cv_ref)
        sc = _ssd_chunk_scalars(dtr_ref[...], dtb_ref[...], alog_ref[...])
        tril = _iota((CHUNK, CHUNK), 0) >= _iota((CHUNK, CHUNK), 1)
        masks = _head_masks()
        hp_ref[0] = state_ref[...]
        for g in range(N_GROUPS):
            gsl = slice(g * GROUP_W, (g + 1) * GROUP_W)
            xs_g = xc_ref[:, gsl]
            bg = xc_ref[:, _B0 + g * D_STATE:_B0 + (g + 1) * D_STATE]
            cg = xc_ref[:, _C0 + g * D_STATE:_C0 + (g + 1) * D_STATE]
            xdt_g = xs_g * sc["dt_x"][:, gsl]
            cbm = _dot(cg, bg, _NT)
            mw = jnp.concatenate(
                [cbm * jnp.exp(jnp.where(tril, sc["csb"][h] - sc["cs_t"][h:h + 1, :], -1e30))
                 for h in range(g * HEADS_PER_GROUP, (g + 1) * HEADS_PER_GROUP)], axis=1)
            ht_g = state_ref[:, gsl]
            y_g = _dot(mw, _stack_heads(xdt_g, masks)) + sc["e_x"][:, gsl] * _dot(cg, ht_g) + dsk_ref[:, gsl] * xs_g
            state_ref[:, gsl] = ht_g * sc["dk_x"][:, gsl] + _dot(bg, xdt_g * sc["dec_x"][:, gsl], _TN)
            zg = z_ref[:, gsl]
            yg = y_g * zg * _sigmoid(zg)
            rs = lax.rsqrt(jnp.mean(yg * yg, axis=1, keepdims=True) + NORM_EPS)
            yb_ref[:, gsl] = (yg * rs * gs_ref[:, gsl]).astype(bf16)
        if ride is not None:
            pl.when(c == nc - 1)(finish)

    def chunk(w):
        return pl.BlockSpec((CHUNK, w), lambda c: (c, 0))

    def const(shape):
        return pl.BlockSpec(shape, lambda c: (0,) * len(shape))

    riding = ride is not None
    return pl.pallas_call(
        body, name=name, grid=(nc,),
        in_specs=[chunk(CONV_DIM), pl.BlockSpec((_TAIL, CONV_DIM), lambda c: (jnp.maximum(c * tiles - 1, 0), 0)),
                  chunk(D_INNER), chunk(LANES), const((CONV_W, CONV_DIM)), const((1, CONV_DIM)),
                  const((1, LANES)), const((1, LANES)), const((1, D_INNER)), const((1, D_INNER))] + [_ANY] * riding,
        out_specs=[chunk(D_INNER), pl.BlockSpec((1, D_STATE, D_INNER), lambda c: (c, 0, 0)), chunk(CONV_DIM)]
        + [_ANY] * riding,
        out_shape=[jax.ShapeDtypeStruct((t, D_INNER), bf16), jax.ShapeDtypeStruct((nc, D_STATE, D_INNER), f32),
                   jax.ShapeDtypeStruct((t, CONV_DIM), f32)]
        + ([jax.ShapeDtypeStruct((N_CHIPS,) + ride.shape, ride.dtype)] if riding else []),
        scratch_shapes=[pltpu.VMEM((D_STATE, D_INNER), f32), pltpu.VMEM((CHUNK, CONV_DIM), f32)]
        + (list(_GATHER_SCRATCH) if riding else []),
        compiler_params=_params(("arbitrary",)),
    )(xbc, xbc, z, dtr, cw, cb, dtb, alog, dsk_x, gs, *([ride] if riding else []))


def _ssd_bwd(xbc, cv, z, dtr, hprev, dyb, cw, dtb, alog, dsk_x, gs, seg, *, ride=None, name):
    t = xbc.shape[0]
    nc = t // CHUNK

    def body(*refs):
        (cur_ref, cv_ref, z_ref, dtr_ref, hp_ref, dyb_ref, cw_ref, dtb_ref, alog_ref, dsk_ref, gs_ref,
         seg_ref) = refs[:12]
        rest = refs[12:]
        if ride is not None:
            ride_ref, got_ref, send_sems, recv_sems = rest[0], rest[10], rest[-2], rest[-1]
            rest = rest[1:10] + rest[11:-2]
        (dz_ref, dxbc_ref, ddt_ref, dcw_ref, dcb_ref, ddtb_ref, dalog_ref, ddsk_ref, dgs_ref,
         dh_ref, dcnext_ref, xc_ref, dxc_ref, x13_ref, x2_ref, rows_ref) = rest
        i = pl.program_id(0)
        cc = nc - 1 - i
        if ride is not None:
            start, finish = _scatter_protocol(ride_ref, got_ref, send_sems, recv_sems)
            pl.when(i == 0)(start)

        @pl.when(i == 0)
        def _():
            for ref in (dh_ref, dcnext_ref, dcw_ref, dcb_ref, ddtb_ref, dalog_ref, ddsk_ref, dgs_ref, rows_ref):
                ref[...] = jnp.zeros_like(ref)

        for j in range(CONV_DIM // _CONV_COLS):
            sl = slice(j * _CONV_COLS, (j + 1) * _CONV_COLS)
            cvv = cv_ref[:, sl]
            xc_ref[:, sl] = cvv * _sigmoid(cvv)
        sc = _ssd_chunk_scalars(dtr_ref[...], dtb_ref[...], alog_ref[...])
        tril = _iota((CHUNK, CHUNK), 0) >= _iota((CHUNK, CHUNK), 1)
        triu = _iota((CHUNK, CHUNK), 0) <= _iota((CHUNK, CHUNK), 1)
        masks = _head_masks()
        rowh = _iota((N_HEADS, CHUNK), 0)
        dcs_t = jnp.zeros((N_HEADS, CHUNK), f32)
        for g in range(N_GROUPS):
            gsl = slice(g * GROUP_W, (g + 1) * GROUP_W)
            xs_g = xc_ref[:, gsl]
            bg = xc_ref[:, _B0 + g * D_STATE:_B0 + (g + 1) * D_STATE]
            cg = xc_ref[:, _C0 + g * D_STATE:_C0 + (g + 1) * D_STATE]
            dt_g, e_g, dec_g, dk_g = sc["dt_x"][:, gsl], sc["e_x"][:, gsl], sc["dec_x"][:, gsl], sc["dk_x"][:, gsl]
            dsk_g = dsk_ref[:, gsl]
            xdt_g = xs_g * dt_g
            xdt_stack = _stack_heads(xdt_g, masks)
            cbm = _dot(cg, bg, _NT)
            cbt = _dot(bg, cg, _NT)
            heads = range(g * HEADS_PER_GROUP, (g + 1) * HEADS_PER_GROUP)
            lmats = [jnp.exp(jnp.where(tril, sc["csb"][h] - sc["cs_t"][h:h + 1, :], -1e30)) for h in heads]
            mw = jnp.concatenate([cbm * lm for lm in lmats], axis=1)
            mtw = jnp.concatenate(
                [cbt * jnp.exp(jnp.where(triu, sc["cs_t"][h:h + 1, :] - sc["csb"][h], -1e30)) for h in heads], axis=1)
            ht_g = hp_ref[0, :, gsl]
            dhn_g = dh_ref[:, gsl]
            yoff = e_g * _dot(cg, ht_g)
            y_g = _dot(mw, xdt_stack) + yoff + dsk_g * xs_g
            zg = z_ref[:, gsl]
            sz = _sigmoid(zg)
            silu = zg * sz
            yg = y_g * silu
            rs = lax.rsqrt(jnp.mean(yg * yg, axis=1, keepdims=True) + NORM_EPS)
            yn = yg * rs
            dyb = dyb_ref[:, gsl]
            dgs_ref[:, gsl] += jnp.sum(dyb * yn, axis=0, keepdims=True)
            dyn = dyb * gs_ref[:, gsl]
            dyg = rs * (dyn - yn * jnp.mean(dyn * yn, axis=1, keepdims=True))
            dy_g = dyg * silu
            dz_ref[:, gsl] = (dyg * y_g * (sz * (1.0 + zg * (1.0 - sz)))).astype(bf16)
            dy_stack = _stack_heads(dy_g, masks)
            dm_w = _dot(dy_g, xdt_stack, _NT)
            dmt_w = _dot(xdt_g, dy_stack, _NT)
            dxdt = _dot(mtw, dy_stack)
            dcb_acc = jnp.zeros((CHUNK, CHUNK), f32)
            for r, h in enumerate(heads):
                hs = slice(r * CHUNK, (r + 1) * CHUNK)
                dml = dm_w[:, hs] * lmats[r]
                dcb_acc = dcb_acc + dml
                col = jnp.sum(dml * cbm, axis=0, keepdims=True)
                row = jnp.sum(dmt_w[:, hs] * mtw[:, hs], axis=0, keepdims=True)
                dcs_t = dcs_t + jnp.where(rowh == h, row - col, 0.0)
            w = _dot(bg, dhn_g)
            dxdt = dxdt + dec_g * w
            decx3 = dec_g * (xdt_g * w)
            dg_g = e_g * dy_g
            d_c = _dot(dg_g, ht_g, _NT) + _dot(dcb_acc, bg)
            d_b = _dot(dcb_acc, cg, _TN) + _dot(xdt_g * dec_g, dhn_g, _NT)
            dh_ref[:, gsl] = dhn_g * dk_g + _dot(cg, dg_g, _TN)
            dxc_ref[:, gsl] = dsk_g * dy_g + dxdt * dt_g
            dxc_ref[:, _B0 + g * D_STATE:_B0 + (g + 1) * D_STATE] = d_b
            dxc_ref[:, _C0 + g * D_STATE:_C0 + (g + 1) * D_STATE] = d_c
            x13_ref[:, gsl] = dy_g * yoff - decx3
            x2_ref[:, gsl] = dxdt * xs_g
            rows_ref[0:1, gsl] = jnp.sum(dhn_g * ht_g, axis=0, keepdims=True)
            rows_ref[1:2, gsl] = jnp.sum(decx3, axis=0, keepdims=True)
            rows_ref[2:3, gsl] = jnp.sum(dy_g * xs_g, axis=0, keepdims=True)
        segm = seg_ref[...]
        r13 = _seg_sum(x13_ref[...], segm)
        r2 = _seg_sum(x2_ref[...], segm)
        small = _seg_sum(rows_ref[...], segm)
        lane = _iota((CHUNK, LANES), 1)
        rowi = _iota((CHUNK, LANES), 0)
        dcl_row = small[0:1, :] * jnp.exp(sc["cs"][CHUNK - 1:CHUNK, :]) + small[1:2, :]
        dcs = r13 + jnp.where(rowi == CHUNK - 1, dcl_row, 0.0)
        dcs_t_all = dcs.T + jnp.concatenate([dcs_t, jnp.zeros((LANES - N_HEADS, CHUNK), f32)], axis=0)
        dda = _dot32(dcs_t_all, tril.astype(f32)).T
        a = sc["a"]
        ddt_total = r2 + dda * a
        dalog_ref[...] += jnp.sum(dda * sc["dtv"], axis=0, keepdims=True) * a
        ddtr = jnp.where(lane < N_HEADS, ddt_total * _sigmoid(sc["xdt_pre"]), 0.0)
        ddtb_ref[...] += jnp.sum(ddtr, axis=0, keepdims=True)
        ddt_ref[...] = ddtr.astype(bf16)
        ddsk_ref[...] += small[2:3, :]
        row8 = _iota((_TAIL, _CONV_COLS), 0)
        for j in range(CONV_DIM // _CONV_COLS):
            sl = slice(j * _CONV_COLS, (j + 1) * _CONV_COLS)
            cvv = cv_ref[:, sl]
            sg = _sigmoid(cvv)
            dconv = dxc_ref[:, sl] * (sg * (1.0 + cvv * (1.0 - sg)))
            nxt = dcnext_ref[:, sl]
            cur = cur_ref[:, sl]
            dxin = dconv * cw_ref[CONV_W - 1:CONV_W, sl]
            dcw_ref[CONV_W - 1:CONV_W, sl] += jnp.sum(dconv * cur, axis=0, keepdims=True)
            for s in range(1, CONV_W):
                rolled = pltpu.roll(dconv, CHUNK - s, 0)
                bot = jnp.where(row8 < _TAIL - s, rolled[CHUNK - _TAIL:], pltpu.roll(nxt, _TAIL - s, 0))
                up = jnp.concatenate([rolled[:CHUNK - _TAIL], bot], axis=0)
                dxin = dxin + up * cw_ref[CONV_W - 1 - s:CONV_W - s, sl]
                dcw_ref[CONV_W - 1 - s:CONV_W - s, sl] += jnp.sum(up * cur, axis=0, keepdims=True)
            dcb_ref[:, sl] += jnp.sum(dconv, axis=0, keepdims=True)
            dxbc_ref[:, sl] = dxin.astype(bf16)
            dcnext_ref[:, sl] = dconv[:_TAIL]
        if ride is not None:
            pl.when(i == nc - 1)(finish)

    def chunk(w):
        return pl.BlockSpec((CHUNK, w), lambda i: (nc - 1 - i, 0))

    def const(shape):
        return pl.BlockSpec(shape, lambda i: (0,) * len(shape))

    riding = ride is not None
    return pl.pallas_call(
        body, name=name, grid=(nc,),
        in_specs=[chunk(CONV_DIM), chunk(CONV_DIM),
                  chunk(D_INNER), chunk(LANES), pl.BlockSpec((1, D_STATE, D_INNER), lambda i: (nc - 1 - i, 0, 0)),
                  chunk(D_INNER), const((CONV_W, CONV_DIM)),
                  const((1, LANES)), const((1, LANES)), const((1, D_INNER)), const((1, D_INNER)),
                  const((D_INNER, LANES))] + [_ANY] * riding,
        out_specs=[chunk(D_INNER), chunk(CONV_DIM), chunk(LANES), const((CONV_W, CONV_DIM)), const((1, CONV_DIM)),
                   const((1, LANES)), const((1, LANES)), const((1, LANES)), const((1, D_INNER))] + [_ANY] * riding,
        out_shape=[jax.ShapeDtypeStruct((t, D_INNER), bf16), jax.ShapeDtypeStruct((t, CONV_DIM), bf16),
                   jax.ShapeDtypeStruct((t, LANES), bf16), jax.ShapeDtypeStruct((CONV_W, CONV_DIM), f32),
                   jax.ShapeDtypeStruct((1, CONV_DIM), f32), jax.ShapeDtypeStruct((1, LANES), f32),
                   jax.ShapeDtypeStruct((1, LANES), f32), jax.ShapeDtypeStruct((1, LANES), f32),
                   jax.ShapeDtypeStruct((1, D_INNER), f32)]
        + ([jax.ShapeDtypeStruct((N_CHIPS - 1,) + ride.shape[1:], ride.dtype)] if riding else []),
        scratch_shapes=[pltpu.VMEM((D_STATE, D_INNER), f32), pltpu.VMEM((_TAIL, CONV_DIM), f32),
                        pltpu.VMEM((CHUNK, CONV_DIM), f32), pltpu.VMEM((CHUNK, CONV_DIM), f32),
                        pltpu.VMEM((CHUNK, D_INNER), f32), pltpu.VMEM((CHUNK, D_INNER), f32),
                        pltpu.VMEM((_TAIL, D_INNER), f32)]
        + (list(_SCATTER_SCRATCH) if riding else []),
        compiler_params=_params(("arbitrary",)),
    )(xbc, cv, z, dtr, hprev, dyb, cw, dtb, alog, dsk_x, gs, seg, *([ride] if riding else []))


def _adamw(w, g, m, v, *, name):
    r, c = w.shape
    tr = r
    while tr * c * 4 > _MB and tr % 16 == 0:
        tr //= 2

    def body(w_ref, g_ref, m_ref, v_ref, d_ref, m2_ref, v2_ref):
        gv = g_ref[...]
        m2 = ADAM_B1 * m_ref[...] + (1.0 - ADAM_B1) * gv
        v2 = ADAM_B2 * v_ref[...] + (1.0 - ADAM_B2) * (gv * gv)
        m_hat = m2 / (1.0 - ADAM_B1 ** ADAM_STEP)
        v_hat = v2 / (1.0 - ADAM_B2 ** ADAM_STEP)
        d_ref[...] = -ADAM_LR * (m_hat / (jnp.sqrt(v_hat) + ADAM_EPS) + ADAM_WD * w_ref[...])
        m2_ref[...] = m2
        v2_ref[...] = v2

    blk = pl.BlockSpec((tr, c), lambda i: (i, 0))
    return pl.pallas_call(
        body, name=name, grid=(r // tr,),
        in_specs=[blk] * 4, out_specs=[blk] * 3,
        out_shape=[jax.ShapeDtypeStruct((r, c), f32)] * 3,
        compiler_params=_params(("parallel",)),
    )(w, g, m, v)


def _row_block(rows, cap=2304):
    return max(tr for tr in range(16, cap + 1, 16) if rows % tr == 0)


def _cast_bf16(a, *, name):
    r, c = a.shape
    tr = _row_block(r)

    def body(a_ref, o_ref):
        o_ref[...] = a_ref[...].astype(bf16)

    blk = pl.BlockSpec((tr, c), lambda i: (i, 0))
    return pl.pallas_call(
        body, name=name, grid=(r // tr,), in_specs=[blk], out_specs=blk,
        out_shape=jax.ShapeDtypeStruct((r, c), bf16), compiler_params=_params(("parallel",)),
    )(a)


_ANY = pl.BlockSpec(memory_space=pl.ANY)


def _place():
    x, y, c = lax.axis_index("x"), lax.axis_index("y"), lax.axis_index("c")
    other_chips = [(1 - x, y), (x, 1 - y), (1 - x, 1 - y)]
    return x, y, c, other_chips


def _gather_protocol(in_ref, out_ref, send_sems, recv_sems):
    x, y, c, chips = _place()
    me = 2 * x + y
    sibling = (x, y, 1 - c)

    def cp(k, chip, half, to, src=None):
        dst = out_ref.at[chip, half]
        return pltpu.make_async_remote_copy(
            src_ref=dst if src is None else src, dst_ref=dst, send_sem=send_sems.at[k], recv_sem=recv_sems.at[k],
            device_id=to, device_id_type=MESH)

    def sends():
        return [cp(j, me, c, (cx, cy, c), src=in_ref.at[c]) for j, (cx, cy) in enumerate(chips)]

    def relays():
        return [cp(3 + j, 2 * cx + cy, c, sibling) for j, (cx, cy) in enumerate(chips)]

    def start():
        for f in sends():
            f.start()

    def relay():
        onward = relays()
        for j, (cx, cy) in enumerate(chips):
            cp(j, 2 * cx + cy, c, sibling).wait_recv()
            onward[j].start()

    def finish():
        for j, (cx, cy) in enumerate(chips):
            cp(3 + j, 2 * cx + cy, 1 - c, sibling).wait_recv()
        for f in sends() + relays():
            f.wait_send()

    return start, relay, finish


_GATHER_SCRATCH = [pltpu.SemaphoreType.DMA((6,)), pltpu.SemaphoreType.DMA((6,))]


def _gather_shards(shard, *, name):
    _, rh, lanes = shard.shape

    def body(in_ref, out_ref, send_sems, recv_sems):
        start, relay, finish = _gather_protocol(in_ref, out_ref, send_sems, recv_sems)
        start()
        relay()
        finish()

    return pl.pallas_call(
        body, name=name, in_specs=[_ANY], out_specs=_ANY,
        out_shape=jax.ShapeDtypeStruct((N_CHIPS, 2, rh, lanes), shard.dtype),
        scratch_shapes=list(_GATHER_SCRATCH),
    )(shard)


def _scatter_protocol(p_ref, out_ref, send_sems, recv_sems):
    x, y, c, chips = _place()

    def copies():
        return [pltpu.make_async_remote_copy(
            src_ref=p_ref.at[2 * cx + cy], dst_ref=out_ref.at[j], send_sem=send_sems.at[j], recv_sem=recv_sems.at[j],
            device_id=(cx, cy, c), device_id_type=MESH) for j, (cx, cy) in enumerate(chips)]

    def start():
        for cpy in copies():
            cpy.start()

    def finish():
        for cpy in copies():
            cpy.wait()

    return start, finish


_SCATTER_SCRATCH = [pltpu.SemaphoreType.DMA((3,)), pltpu.SemaphoreType.DMA((3,))]


def _rs_swap_halves(g, *, name):
    nch, _, rh, lanes = g.shape

    def body(g_ref, out_ref, send_sems, recv_sems):
        x, y, c, _ = _place()
        copies = [pltpu.make_async_remote_copy(
            src_ref=g_ref.at[k, 1 - c], dst_ref=out_ref.at[k], send_sem=send_sems.at[k], recv_sem=recv_sems.at[k],
            device_id=(x, y, 1 - c), device_id_type=MESH) for k in range(nch)]
        for cpy in copies:
            cpy.start()
        for cpy in copies:
            cpy.wait()

    return pl.pallas_call(
        body, name=name, in_specs=[_ANY], out_specs=_ANY,
        out_shape=jax.ShapeDtypeStruct((nch, rh, lanes), g.dtype),
        scratch_shapes=[pltpu.SemaphoreType.DMA((nch,)), pltpu.SemaphoreType.DMA((nch,))],
    )(g)


def _rs_add_pair(g, got, c_idx, *, name):
    nch, _, rh, lanes = g.shape
    tr = _row_block(rh)

    def body(c_ref, g_ref, got_ref, p32_ref, p16_ref):
        s = g_ref[...] + got_ref[...]
        p32_ref[...] = s
        p16_ref[...] = s.astype(bf16)

    blk = pl.BlockSpec((None, tr, lanes), lambda k, i, c_ref: (k, i, 0))
    return pl.pallas_call(
        body, name=name,
        grid_spec=pltpu.PrefetchScalarGridSpec(
            num_scalar_prefetch=1, grid=(nch, rh // tr),
            in_specs=[pl.BlockSpec((None, None, tr, lanes), lambda k, i, c_ref: (k, c_ref[0], i, 0)), blk],
            out_specs=[blk, blk]),
        out_shape=[jax.ShapeDtypeStruct((nch, rh, lanes), f32), jax.ShapeDtypeStruct((nch, rh, lanes), bf16)],
        compiler_params=_params(("parallel", "parallel")),
    )(c_idx, g, got)


def _rs_add_chips(p32, got, me_idx, *, name):
    _, rh, lanes = p32.shape
    tr = _row_block(rh)

    def body(me_ref, p_ref, got_ref, o_ref):
        o_ref[...] = ((p_ref[...] + got_ref[0].astype(f32)) + got_ref[1].astype(f32)) + got_ref[2].astype(f32)

    return pl.pallas_call(
        body, name=name,
        grid_spec=pltpu.PrefetchScalarGridSpec(
            num_scalar_prefetch=1, grid=(rh // tr,),
            in_specs=[pl.BlockSpec((None, tr, lanes), lambda i, me_ref: (me_ref[0], i, 0)),
                      pl.BlockSpec((3, tr, lanes), lambda i, me_ref: (0, i, 0))],
            out_specs=pl.BlockSpec((tr, lanes), lambda i, me_ref: (i, 0))),
        out_shape=jax.ShapeDtypeStruct((rh, lanes), f32),
        compiler_params=_params(("parallel",)),
    )(me_idx, p32, got)


def _rs_join_halves(half, *, name):
    rh, lanes = half.shape

    def body(h_ref, out_ref, send_sem, recv_sem):
        x, y, c, _ = _place()
        cpy = pltpu.make_async_remote_copy(
            src_ref=h_ref, dst_ref=out_ref, send_sem=send_sem, recv_sem=recv_sem,
            device_id=(x, y, 1 - c), device_id_type=MESH)
        cpy.start()
        cpy.wait()

    return pl.pallas_call(
        body, name=name, in_specs=[_ANY], out_specs=_ANY,
        out_shape=jax.ShapeDtypeStruct((rh, lanes), half.dtype),
        scratch_shapes=[pltpu.SemaphoreType.DMA, pltpu.SemaphoreType.DMA],
    )(half)


def _all_reduce_small(s, *, name):
    rs, lanes = s.shape

    def body(s_ref, o_ref, buf_ref, send_sems, recv_sems):
        x, y, c, _ = _place()
        me = 4 * x + 2 * y + c
        peers = []
        for k in range(1, N_DEV):
            px = 1 - x if (k >> 2) & 1 else x
            py = 1 - y if (k >> 1) & 1 else y
            pc = 1 - c if k & 1 else c
            peers.append((px, py, pc))
        copies = [pltpu.make_async_remote_copy(
            src_ref=s_ref, dst_ref=buf_ref.at[me], send_sem=send_sems.at[k], recv_sem=recv_sems.at[k],
            device_id=peer, device_id_type=MESH) for k, peer in enumerate(peers)]
        for cpy in copies:
            cpy.start()
        buf_ref[me] = s_ref[...]
        for k, (px, py, pc) in enumerate(peers):
            pltpu.make_async_remote_copy(
                src_ref=s_ref, dst_ref=buf_ref.at[4 * px + 2 * py + pc], send_sem=send_sems.at[k],
                recv_sem=recv_sems.at[k], device_id=(px, py, pc), device_id_type=MESH).wait_recv()
        for cpy in copies:
            cpy.wait_send()
        acc = buf_ref[0]
        for d in range(1, N_DEV):
            acc = acc + buf_ref[d]
        o_ref[...] = acc

    vm = pl.BlockSpec(memory_space=pltpu.VMEM)
    return pl.pallas_call(
        body, name=name, in_specs=[vm], out_specs=vm,
        out_shape=jax.ShapeDtypeStruct((rs, lanes), f32),
        scratch_shapes=[pltpu.VMEM((N_DEV, rs, lanes), f32), pltpu.SemaphoreType.DMA((N_DEV - 1,)),
                        pltpu.SemaphoreType.DMA((N_DEV - 1,))],
        compiler_params=pltpu.CompilerParams(vmem_limit_bytes=32 * _MB),
    )(s)


def _pad_lanes(a, width=LANES):
    return jnp.pad(a, ((0, 0), (0, width - a.shape[1])))


def _local_grads(x, tgt, wts, small, *, fwd_ride=None, late_weights=None, bwd_ride=None, last_ride=None):
    t = x.shape[0]
    tm = min(t, 1024)
    d = D_MODEL
    mm = functools.partial(_matmul, tm=tm)

    dtb = _pad_lanes(small["dt_bias"])
    alog = _pad_lanes(small["a_log"])
    dsk = jnp.repeat(small["d_skip"], HEAD_DIM, axis=1)
    bsp_t = _pad_lanes(small["b_spatial"].T)
    wsp = small["w_spatial"]

    h = _rms_fwd(x, small["norm_mix_g"], name="rms_mix")
    uv = mm(h, wts["uv"], tn=1024, tk=d, out_dtypes=[f32], name="proj_uv")
    z = mm(h, wts["z"], tn=1024, tk=d, out_dtypes=[f32], name="proj_z")
    xbc = mm(h, wts["xbc"], tn=1024, tk=d, out_dtypes=[f32], name="proj_xbc")
    dtr = mm(h, wts["dt"], tn=LANES, tk=d, out_dtypes=[f32], name="proj_dt")
    gl = mm(h, wts["gate"], tn=1024, tk=d, out_dtypes=[f32], name="proj_gate")
    ya = _gmlp_fwd(uv, small["v_norm_g"], small["v_norm_b"], wsp, bsp_t, name="gmlp_fwd")
    yb, hprev, cv, *gathered = _ssd_fwd(xbc, z, dtr, small["conv_w"], small["conv_b"], dtb, alog, dsk,
                                    small["ssm_norm_g"], ride=fwd_ride, name="ssd_fwd")
    if fwd_ride is not None:
        wts = {**wts, **late_weights(gathered[0])}
    pa = mm(ya, wts["pa"], tn=1024, tk=1024, out_dtypes=[f32], name="proj_a")
    pb = mm(yb, wts["pb"], tn=1024, tk=1024, out_dtypes=[f32], name="proj_b")
    merged = _merge_fwd(pa, pb, gl, small["b_gates"], name="merge_fwd")
    x1 = mm(merged, wts["out"], tn=1024, tk=1024, out_dtypes=[f32], extras=[x],
            epilogue=lambda acc, res: (res + acc,), name="out_proj")
    h2 = _rms_fwd(x1, small["norm_mlp_g"], name="rms_mlp")
    act = mm(h2, wts["up"], tn=1024, tk=d, out_dtypes=[bf16],
             epilogue=lambda acc: (jnp.square(jnp.maximum(acc, 0.0)),), name="mlp_up")
    x2 = mm(act, wts["down"], tn=1024, tk=2048, out_dtypes=[f32], extras=[x1],
            epilogue=lambda acc, res: (res + acc,), name="mlp_down")

    dx2, dx2b, dgf, loss = _loss_head(x2, tgt, small["norm_final_g"], name="loss_head")
    tt = min(t, 2048)
    tn_mm = functools.partial(_matmul_tn, tt=tt)
    dw = {}
    dw["down"] = tn_mm(act, dx2b, tka=1024, tn=1024, name="dw_down")
    dup = mm(dx2b, wts["down"], nt=True, tn=1024, tk=1024, out_dtypes=[bf16], extras=[act],
             epilogue=lambda acc, a2: (acc * (2.0 * jnp.sqrt(a2).astype(f32)),), name="d_act")
    dw["up"] = tn_mm(h2, dup, tka=1024, tn=1024, name="dw_up")
    dh2 = mm(dup, wts["up"], nt=True, tn=1024, tk=2048, out_dtypes=[f32], name="d_h2")
    dx1, dx1b, dg_mlp = _rms_bwd(x1, small["norm_mlp_g"], dh2, dx2, want_bf16=True, name="rms_mlp_bwd")
    dw["out"] = tn_mm(merged, dx1b, tka=1024, tn=1024, name="dw_out")
    dmerged = mm(dx1b, wts["out"], nt=True, tn=1024, tk=1024, out_dtypes=[f32], name="d_merged")
    dpa, dpb, dgl, dbg = _merge_bwd(dmerged, pa, pb, gl, small["b_gates"], name="merge_bwd")
    dw["pa"] = tn_mm(ya, dpa, tka=1024, tn=1024, name="dw_pa")
    dw["pb"] = tn_mm(yb, dpb, tka=1024, tn=1024, name="dw_pb")
    dya = mm(dpa, wts["pa"], nt=True, tn=1024, tk=1024, out_dtypes=[f32], name="d_ya")
    dyb = mm(dpb, wts["pb"], nt=True, tn=1024, tk=1024, out_dtypes=[f32], name="d_yb")
    duv, dwsp, dbsp_t, dvg, dvb = _gmlp_bwd(uv, dya, small["v_norm_g"], small["v_norm_b"], wsp, bsp_t,
                                            name="gmlp_bwd")
    ride = bwd_ride(dw) if bwd_ride is not None else None
    dz, dxbc, ddt, dcw, dcb, ddtb, dalog, ddsk, dgs, *got = _ssd_bwd(
        xbc, cv, z, dtr, hprev, dyb, small["conv_w"], dtb, alog, dsk, small["ssm_norm_g"],
        _head_seg_matrix(), ride=ride, name="ssd_bwd")
    dw["uv"] = tn_mm(h, duv, tka=1024, tn=1024, name="dw_uv")
    dw["z"] = tn_mm(h, dz, tka=1024, tn=1024, name="dw_z")
    dw["xbc"] = tn_mm(h, dxbc, tka=1024, tn=1024, name="dw_xbc")
    dw["dt"] = tn_mm(h, ddt, tka=1024, tn=LANES, name="dw_dt")
    dw["gate"] = tn_mm(h, dgl, tka=1024, tn=1024, name="dw_gate")
    last = last_ride(dw) if last_ride is not None else None
    res = _matmul_nt_sum(
        [(duv, wts["uv"]), (dz, wts["z"]), (dxbc, wts["xbc"]), (dgl, wts["gate"]), (ddt, wts["dt"])],
        tm=tm, tks=[1024] * 4 + [LANES], ride=last, name="d_h")
    dh, got_last = (res[0], res[1]) if last is not None else (res, None)
    dx, dg_mix = _rms_bwd(x, small["norm_mix_g"], dh, dx1, want_bf16=False, name="rms_mix_bwd")

    dsmall = {
        "norm_mix_g": dg_mix, "conv_w": dcw, "conv_b": dcb, "dt_bias": ddtb[:, :N_HEADS], "a_log": dalog[:, :N_HEADS],
        "d_skip": ddsk[:, :N_HEADS], "ssm_norm_g": dgs, "v_norm_g": dvg, "v_norm_b": dvb, "w_spatial": dwsp,
        "b_spatial": dbsp_t[:, :GMLP_GROUPS].T, "b_gates": dbg, "norm_mlp_g": dg_mlp, "norm_final_g": dgf,
    }
    return loss, dx, dw, dsmall, (got[0] if got else None), got_last


_IN_SHARD = IN_PROJ // N_CHIPS
_DENSE = ("w_in", "w_proj_a", "w_proj_b", "w_out", "w_mlp_up", "w_mlp_down")
_DENSE_SHARD_SHAPES = {"w_in": (D_MODEL, _IN_SHARD), "w_proj_a": (GMLP_WIDTH // N_CHIPS, D_MODEL),
                       "w_proj_b": (D_INNER // N_CHIPS, D_MODEL), "w_out": (D_MODEL // N_CHIPS, D_MODEL),
                       "w_mlp_up": (D_MODEL, D_FF // N_CHIPS), "w_mlp_down": (D_FF // N_CHIPS, D_MODEL)}
_DENSE_ROWS = {k: s[0] * s[1] // LANES for k, s in _DENSE_SHARD_SHAPES.items()}
_DENSE_TOTAL = sum(_DENSE_ROWS.values())
_CONV_ROWS = CONV_W * (CONV_DIM // N_CHIPS) * 2 // LANES


def _dense_offsets():
    off, out = 0, {}
    for k in _DENSE:
        out[k] = off
        off += _DENSE_ROWS[k]
    return out


_DENSE_OFF = _dense_offsets()

_SMALL = ("norm_mix_g", "conv_w", "conv_b", "dt_bias", "a_log", "d_skip", "ssm_norm_g", "v_norm_g", "v_norm_b",
          "w_spatial", "b_spatial", "b_gates", "norm_mlp_g", "norm_final_g")


def _pack_small(parts):
    flat = jnp.concatenate([parts[k].reshape(-1) for k in _SMALL])
    rows = -(-flat.shape[0] // (8 * LANES)) * 8
    return jnp.pad(flat, (0, rows * LANES - flat.shape[0])).reshape(rows, LANES)


def _unpack_small(packed, shapes):
    flat = packed.reshape(-1)
    out, off = {}, 0
    for k in _SMALL:
        n = math.prod(shapes[k])
        out[k] = flat[off:off + n].reshape(shapes[k])
        off += n
    return out


def _from_chip_columns(stacked, rows, cols):
    return stacked.reshape(N_CHIPS, rows, cols).transpose(1, 0, 2).reshape(rows, N_CHIPS * cols)


def _to_chip_columns(full, cols):
    rows = full.shape[0]
    return full.reshape(rows, N_CHIPS, cols).transpose(1, 0, 2).reshape(N_CHIPS, rows * cols // LANES, LANES)


def kernel(x, norm_mix_g, w_in, conv_w, conv_b, dt_bias, a_log, d_skip, ssm_norm_g, v_norm_g, v_norm_b, w_spatial, b_spatial, b_gates, w_proj_a, w_proj_b, w_out, norm_mlp_g, w_mlp_up, w_mlp_down, norm_final_g, loss_target, m_norm_mix_g, m_w_in, m_conv_w, m_conv_b, m_dt_bias, m_a_log, m_d_skip, m_ssm_norm_g, m_v_norm_g, m_v_norm_b, m_w_spatial, m_b_spatial, m_b_gates, m_w_proj_a, m_w_proj_b, m_w_out, m_norm_mlp_g, m_w_mlp_up, m_w_mlp_down, m_norm_final_g, v_norm_mix_g, v_w_in, v_conv_w, v_conv_b, v_dt_bias, v_a_log, v_d_skip, v_ssm_norm_g, v_v_norm_g, v_v_norm_b, v_w_spatial, v_b_spatial, v_b_gates, v_w_proj_a, v_w_proj_b, v_w_out, v_norm_mlp_g, v_w_mlp_up, v_w_mlp_down, v_norm_final_g):
    given = dict(locals())
    names = ("norm_mix_g", "w_in", "conv_w", "conv_b", "dt_bias", "a_log", "d_skip", "ssm_norm_g", "v_norm_g",
             "v_norm_b", "w_spatial", "b_spatial", "b_gates", "w_proj_a", "w_proj_b", "w_out", "norm_mlp_g",
             "w_mlp_up", "w_mlp_down", "norm_final_g")
    xi, yi, ci = lax.axis_index("x"), lax.axis_index("y"), lax.axis_index("c")
    me_chip = (2 * xi + yi).astype(jnp.int32)

    in_rows = _DENSE_ROWS["w_in"]
    late_rows = _DENSE_TOTAL - in_rows
    w_in_b16 = _cast_bf16(w_in[0].reshape(in_rows, LANES), name="cast_w_in")
    late_b16 = _cast_bf16(jnp.concatenate([given[k][0].reshape(-1, LANES) for k in _DENSE[1:]]), name="cast_w_late")
    conv_shard = conv_w.reshape(CONV_W, CONV_DIM // N_CHIPS)
    conv_bits = lax.bitcast_convert_type(conv_shard.reshape(-1, LANES), bf16).reshape(_CONV_ROWS, LANES)
    shard_in = jnp.concatenate([w_in_b16, conv_bits]).reshape(2, (in_rows + _CONV_ROWS) // 2, LANES)
    shard_late = late_b16.reshape(2, late_rows // 2, LANES)

    def with_own(got, shard):
        return lax.dynamic_update_slice(got, shard[None], (me_chip, 0, 0, 0)).reshape(N_CHIPS, -1, LANES)

    g_in = with_own(_gather_shards(shard_in, name="gather_w_in"), shard_in)
    w_in_full = _from_chip_columns(g_in[:, :in_rows], D_MODEL, _IN_SHARD)
    o_dt, o_gate = 2 * GMLP_WIDTH + D_INNER + CONV_DIM, 2 * GMLP_WIDTH + D_INNER + CONV_DIM + N_HEADS
    wts = {
        "uv": w_in_full[:, :2 * GMLP_WIDTH], "z": w_in_full[:, 2 * GMLP_WIDTH:2 * GMLP_WIDTH + D_INNER],
        "xbc": w_in_full[:, 2 * GMLP_WIDTH + D_INNER:o_dt], "dt": _pad_lanes(w_in_full[:, o_dt:o_gate]),
        "gate": w_in_full[:, o_gate:],
    }
    conv_all = lax.bitcast_convert_type(g_in[:, in_rows:].reshape(N_CHIPS, _CONV_ROWS // 2, LANES, 2), f32)
    conv_full = conv_all.reshape(N_CHIPS, CONV_W, CONV_DIM // N_CHIPS).transpose(1, 0, 2).reshape(CONV_W, CONV_DIM)

    def late_weights(got):
        g_late = with_own(got, shard_late)

        def rows_of(k):
            off = _DENSE_OFF[k] - in_rows
            return g_late[:, off:off + _DENSE_ROWS[k]]

        return {
            "pa": rows_of("w_proj_a").reshape(GMLP_WIDTH, D_MODEL),
            "pb": rows_of("w_proj_b").reshape(D_INNER, D_MODEL), "out": rows_of("w_out").reshape(D_MODEL, D_MODEL),
            "up": _from_chip_columns(rows_of("w_mlp_up"), D_MODEL, D_FF // N_CHIPS),
            "down": rows_of("w_mlp_down").reshape(D_FF, D_MODEL),
        }

    small = {
        "norm_mix_g": norm_mix_g, "conv_w": conv_full, "conv_b": conv_b, "dt_bias": dt_bias, "a_log": a_log,
        "d_skip": d_skip, "ssm_norm_g": ssm_norm_g, "v_norm_g": v_norm_g, "v_norm_b": v_norm_b,
        "w_spatial": w_spatial[0], "b_spatial": b_spatial[0], "b_gates": b_gates, "norm_mlp_g": norm_mlp_g,
        "norm_final_g": norm_final_g.reshape(1, D_MODEL),
    }

    c_idx = ci.astype(jnp.int32).reshape(1)
    me_idx = me_chip.reshape(1)
    partials = {}

    def pair_sums(g, tag):
        g = g.reshape(N_CHIPS, 2, -1, LANES)
        p32, p16 = _rs_add_pair(g, _rs_swap_halves(g, name="rs_swap_" + tag), c_idx, name="rs_add_pair_" + tag)
        partials[tag] = p32
        return p16

    def reduced_shard(tag, got_chips):
        half = _rs_add_chips(partials[tag], got_chips, me_idx, name="rs_add_chips_" + tag)
        other_half = _rs_join_halves(half, name="rs_join_" + tag)
        return jnp.where(ci == 0, jnp.concatenate([half, other_half]), jnp.concatenate([other_half, half]))

    def late_partials(dw):
        return pair_sums(jnp.concatenate(
            [dw["pa"].reshape(N_CHIPS, -1, LANES), dw["pb"].reshape(N_CHIPS, -1, LANES),
             dw["out"].reshape(N_CHIPS, -1, LANES), _to_chip_columns(dw["up"], D_FF // N_CHIPS),
             dw["down"].reshape(N_CHIPS, -1, LANES)], axis=1), "late")

    def in_partials(dw):
        dw_in = jnp.concatenate([dw["uv"], dw["z"], dw["xbc"], dw["dt"][:, :N_HEADS], dw["gate"]], axis=1)
        return pair_sums(_to_chip_columns(dw_in, _IN_SHARD), "in")

    loss_part, grad_x, dw, dsmall, got_late, got_in = _local_grads(
        x[0], loss_target[0], wts, small, fwd_ride=shard_late, late_weights=late_weights, bwd_ride=late_partials,
        last_ride=in_partials)
    loss = lax.psum(loss_part[0, 0], ("x", "y", "c"))
    g_late = reduced_shard("late", got_late)
    g_in_shard = reduced_shard("in", got_in)

    small_shapes = {k: dsmall[k].shape for k in _SMALL}
    red = _unpack_small(_all_reduce_small(_pack_small(dsmall), name="all_reduce_small"), small_shapes)
    conv_cols = CONV_DIM // N_CHIPS
    red["conv_w"] = lax.dynamic_slice_in_dim(red["conv_w"], me_chip * conv_cols, conv_cols, axis=1)

    grads, deltas, new_m, new_v = {}, {}, {}, {}
    for k in _DENSE:
        shp = _DENSE_SHARD_SHAPES[k]
        src, off = (g_in_shard, 0) if k == "w_in" else (g_late, _DENSE_OFF[k] - in_rows)
        g2 = src[off:off + _DENSE_ROWS[k]].reshape(shp)
        dlt, m2, v2 = _adamw(given[k][0], g2, given["m_" + k][0], given["v_" + k][0], name="adamw_" + k)
        grads[k], deltas[k], new_m[k], new_v[k] = g2, dlt, m2, v2
    adam_shapes = dict(small_shapes)
    adam_shapes["conv_w"] = (CONV_W, conv_cols)

    def small_pack_of(prefix):
        return _pack_small({k: given[prefix + k].reshape(adam_shapes[k]) for k in _SMALL})

    dlt_s, m_s, v_s = _adamw(small_pack_of(""), _pack_small(red), small_pack_of("m_"), small_pack_of("v_"),
                             name="adamw_small")
    for dst, packed in ((deltas, dlt_s), (new_m, m_s), (new_v, v_s)):
        dst.update(_unpack_small(packed, adam_shapes))
    grads.update(red)

    def shaped(dct):
        return [dct[k].reshape(given[k].shape) for k in names]

    return (loss, grad_x[None], *shaped(grads), *shaped(deltas), *shaped(new_m), *shaped(new_v))
```

```python
import functools
import math

import jax
import jax.numpy as jnp
from jax import lax
from jax.experimental import pallas as pl
from jax.experimental.pallas import tpu as pltpu

f32 = jnp.float32
bf16 = jnp.bfloat16

D_MODEL = 1024
CHUNK = 128
GMLP_WIDTH = 1024
GMLP_GROUPS = 8
D_INNER = 2048
HEAD_DIM = 64
N_HEADS = 32
N_GROUPS = 8
HEADS_PER_GROUP = 4
GROUP_W = HEADS_PER_GROUP * HEAD_DIM
D_STATE = 128
CONV_W = 4
CONV_DIM = 4096
D_FF = 4096
IN_PROJ = 10272
NORM_EPS = 1e-6
N_CHIPS = 4
N_DEV = 8
LANES = 128

ADAM_LR = 0.001
ADAM_B1 = 0.9
ADAM_B2 = 0.999
ADAM_EPS = 1e-08
ADAM_WD = 0.01
ADAM_STEP = 10

MESH = pl.DeviceIdType.MESH
_NT = (((1,), (1,)), ((), ()))
_NN = (((1,), (0,)), ((), ()))
_TN = (((0,), (0,)), ((), ()))
_MB = 2 ** 20


def _params(sem, vmem_mb=48):
    return pltpu.CompilerParams(dimension_semantics=sem, vmem_limit_bytes=vmem_mb * _MB)


def _dot(a, b, dims=_NN):
    return lax.dot_general(a.astype(bf16), b.astype(bf16), dims, preferred_element_type=f32)


def _dot32(a, b):
    return jnp.dot(a, b, preferred_element_type=f32, precision=lax.Precision.HIGHEST)


def _sigmoid(x):
    return 1.0 / (1.0 + jnp.exp(-x))


def _sum_all(a):
    return jnp.sum(jnp.sum(a, axis=1, keepdims=True), axis=0, keepdims=True)


def _iota(shape, dim):
    return lax.broadcasted_iota(jnp.int32, shape, dim)


def _matmul(a, b, *, nt=False, tm, tn, tk, out_dtypes, epilogue=None, extras=(), name):
    m, k_dim = a.shape
    n = b.shape[0] if nt else b.shape[1]
    nk = k_dim // tk
    ne, no = len(extras), len(out_dtypes)
    dims = _NT if nt else _NN

    def body(*refs):
        a_ref, b_ref = refs[0], refs[1]
        ex = refs[2:2 + ne]
        outs = refs[2 + ne:2 + ne + no]

        def finish(acc):
            vals = epilogue(acc, *[e[...] for e in ex]) if epilogue is not None else (acc,)
            for o, v in zip(outs, vals):
                o[...] = v.astype(o.dtype)

        part = lax.dot_general(a_ref[...], b_ref[...], dims, preferred_element_type=f32)
        if nk == 1:
            finish(part)
        else:
            acc_ref = refs[-1]
            kk = pl.program_id(2)

            @pl.when(kk == 0)
            def _():
                acc_ref[...] = part

            @pl.when(kk > 0)
            def _():
                acc_ref[...] += part

            @pl.when(kk == nk - 1)
            def _():
                finish(acc_ref[...])

    b_spec = pl.BlockSpec((tn, tk), lambda i, j, k: (j, k)) if nt else pl.BlockSpec((tk, tn), lambda i, j, k: (k, j))
    tile = pl.BlockSpec((tm, tn), lambda i, j, k: (i, j))
    outs = pl.pallas_call(
        body, name=name, grid=(m // tm, n // tn, nk),
        in_specs=[pl.BlockSpec((tm, tk), lambda i, j, k: (i, k)), b_spec] + [tile] * ne,
        out_specs=[tile] * no,
        out_shape=[jax.ShapeDtypeStruct((m, n), dt) for dt in out_dtypes],
        scratch_shapes=[pltpu.VMEM((tm, tn), f32)] if nk > 1 else [],
        compiler_params=_params(("parallel", "parallel", "arbitrary")),
    )(a, b, *extras)
    return outs if no > 1 else outs[0]


def _matmul_nt_sum(pairs, *, tm, tks, ride=None, name):
    m = pairs[0][0].shape[0]
    n = pairs[0][1].shape[0]
    nblk = [a.shape[1] // tk for (a, _), tk in zip(pairs, tks)]
    starts = [sum(nblk[:p]) for p in range(len(pairs))]
    nk = sum(nblk)
    npairs = len(pairs)
    ni = m // tm
    riding = ride is not None

    def body(*refs):
        rest = refs[2 * npairs:]
        if riding:
            ride_ref, o_ref, got_ref, acc_ref, send_sems, recv_sems = rest
        else:
            o_ref, acc_ref = rest
        i, kk = pl.program_id(0), pl.program_id(1)
        if riding:
            start, finish = _scatter_protocol(ride_ref, got_ref, send_sems, recv_sems)
            pl.when((i == 0) & (kk == 0))(start)

        @pl.when(kk == 0)
        def _():
            acc_ref[...] = jnp.zeros_like(acc_ref)

        for p in range(npairs):
            @pl.when((kk >= starts[p]) & (kk < starts[p] + nblk[p]))
            def _(p=p):
                acc_ref[...] += lax.dot_general(refs[2 * p][...], refs[2 * p + 1][...], _NT, preferred_element_type=f32)

        @pl.when(kk == nk - 1)
        def _():
            o_ref[...] = acc_ref[...]

        if riding:
            pl.when((i == ni - 1) & (kk == nk - 1))(finish)

    in_specs, args = [], []
    for p, (a, b) in enumerate(pairs):
        def kblock(k, s=starts[p], nb=nblk[p]):
            return jnp.clip(k - s, 0, nb - 1)
        in_specs.append(pl.BlockSpec((tm, tks[p]), lambda i, k, kb=kblock: (i, kb(k))))
        in_specs.append(pl.BlockSpec((n, tks[p]), lambda i, k, kb=kblock: (0, kb(k))))
        args += [a, b]
    tile = pl.BlockSpec((tm, n), lambda i, k: (i, 0))
    outs = pl.pallas_call(
        body, name=name, grid=(ni, nk), in_specs=in_specs + [_ANY] * riding, out_specs=[tile] + [_ANY] * riding,
        out_shape=[jax.ShapeDtypeStruct((m, n), f32)]
        + ([jax.ShapeDtypeStruct((N_CHIPS - 1,) + ride.shape[1:], ride.dtype)] if riding else []),
        scratch_shapes=[pltpu.VMEM((tm, n), f32)] + (list(_SCATTER_SCRATCH) if riding else []),
        compiler_params=_params(("arbitrary", "arbitrary"), vmem_mb=56),
    )(*args, *([ride] if riding else []))
    return outs if riding else outs[0]


def _matmul_tn(a, b, *, tka, tn, tt, name):
    t, ka = a.shape
    n = b.shape[1]

    def body(a_ref, b_ref, o_ref):
        part = lax.dot_general(a_ref[...], b_ref[...], _TN, preferred_element_type=f32)
        kk = pl.program_id(2)

        @pl.when(kk == 0)
        def _():
            o_ref[...] = part

        @pl.when(kk > 0)
        def _():
            o_ref[...] += part

    return pl.pallas_call(
        body, name=name, grid=(ka // tka, n // tn, t // tt),
        in_specs=[pl.BlockSpec((tt, tka), lambda i, j, k: (k, i)), pl.BlockSpec((tt, tn), lambda i, j, k: (k, j))],
        out_specs=pl.BlockSpec((tka, tn), lambda i, j, k: (i, j)),
        out_shape=jax.ShapeDtypeStruct((ka, n), f32),
        compiler_params=_params(("parallel", "parallel", "arbitrary")),
    )(a, b)


def _row_tile(t):
    return min(t, 512)


def _rms_fwd(x, g, *, name):
    t, d = x.shape
    tr = _row_tile(t)

    def body(x_ref, g_ref, h_ref):
        xv = x_ref[...]
        r = lax.rsqrt(jnp.mean(xv * xv, axis=1, keepdims=True) + NORM_EPS)
        h_ref[...] = (xv * r * g_ref[...]).astype(bf16)

    return pl.pallas_call(
        body, name=name, grid=(t // tr,),
        in_specs=[pl.BlockSpec((tr, d), lambda i: (i, 0)), pl.BlockSpec((1, d), lambda i: (0, 0))],
        out_specs=pl.BlockSpec((tr, d), lambda i: (i, 0)),
        out_shape=jax.ShapeDtypeStruct((t, d), bf16),
        compiler_params=_params(("parallel",)),
    )(x, g)


def _rms_bwd(xin, g, dh, dres, *, want_bf16, name):
    t, d = xin.shape
    tr = _row_tile(t)

    def body(x_ref, g_ref, dh_ref, dres_ref, dx_ref, *rest):
        dg_ref = rest[-1]
        xv = x_ref[...]
        r = lax.rsqrt(jnp.mean(xv * xv, axis=1, keepdims=True) + NORM_EPS)
        xn = xv * r
        dhv = dh_ref[...]
        dxn = dhv * g_ref[...]
        dx = dres_ref[...] + r * (dxn - xn * jnp.mean(dxn * xn, axis=1, keepdims=True))
        dx_ref[...] = dx
        if want_bf16:
            rest[0][...] = dx.astype(bf16)
        part = jnp.sum(dhv * xn, axis=0, keepdims=True)

        @pl.when(pl.program_id(0) == 0)
        def _():
            dg_ref[...] = part

        @pl.when(pl.program_id(0) > 0)
        def _():
            dg_ref[...] += part

    row = pl.BlockSpec((tr, d), lambda i: (i, 0))
    vec = pl.BlockSpec((1, d), lambda i: (0, 0))
    out_shape = [jax.ShapeDtypeStruct((t, d), f32)] + ([jax.ShapeDtypeStruct((t, d), bf16)] if want_bf16 else []) \
        + [jax.ShapeDtypeStruct((1, d), f32)]
    return pl.pallas_call(
        body, name=name, grid=(t // tr,),
        in_specs=[row, vec, row, row],
        out_specs=[row] + ([row] if want_bf16 else []) + [vec],
        out_shape=out_shape,
        compiler_params=_params(("arbitrary",)),
    )(xin, g, dh, dres)


def _loss_head(x2, tgt, g, *, name):
    t, d = x2.shape
    tr = _row_tile(t)

    def body(x_ref, t_ref, g_ref, dx_ref, dxb_ref, dg_ref, loss_ref):
        xv = x_ref[...]
        gv = g_ref[...]
        r = lax.rsqrt(jnp.mean(xv * xv, axis=1, keepdims=True) + NORM_EPS)
        xn = xv * r
        e = xn * gv - t_ref[...]
        lpart = jnp.zeros((1, LANES), f32) + 0.5 * _sum_all(jnp.mean(e * e, axis=1, keepdims=True))
        dy = e * (1.0 / d)
        dxn = dy * gv
        dx = r * (dxn - xn * jnp.mean(dxn * xn, axis=1, keepdims=True))
        dx_ref[...] = dx
        dxb_ref[...] = dx.astype(bf16)
        gpart = jnp.sum(dy * xn, axis=0, keepdims=True)

        @pl.when(pl.program_id(0) == 0)
        def _():
            dg_ref[...] = gpart
            loss_ref[...] = lpart

        @pl.when(pl.program_id(0) > 0)
        def _():
            dg_ref[...] += gpart
            loss_ref[...] += lpart

    row = pl.BlockSpec((tr, d), lambda i: (i, 0))
    vec = pl.BlockSpec((1, d), lambda i: (0, 0))
    return pl.pallas_call(
        body, name=name, grid=(t // tr,),
        in_specs=[row, row, vec],
        out_specs=[row, row, vec, pl.BlockSpec((1, LANES), lambda i: (0, 0))],
        out_shape=[jax.ShapeDtypeStruct((t, d), f32), jax.ShapeDtypeStruct((t, d), bf16),
                   jax.ShapeDtypeStruct((1, d), f32), jax.ShapeDtypeStruct((1, LANES), f32)],
        compiler_params=_params(("arbitrary",)),
    )(x2, tgt, g)


def _merge_fwd(pa, pb, gl, bg, *, name):
    t, d = pa.shape
    tr = _row_tile(t)

    def body(pa_ref, pb_ref, gla_ref, glb_ref, bga_ref, bgb_ref, o_ref):
        ga = _sigmoid(gla_ref[...] + bga_ref[...])
        gb = _sigmoid(glb_ref[...] + bgb_ref[...])
        o_ref[...] = (ga * pa_ref[...] + gb * pb_ref[...]).astype(bf16)

    row = pl.BlockSpec((tr, d), lambda i: (i, 0))
    return pl.pallas_call(
        body, name=name, grid=(t // tr,),
        in_specs=[row, row, row, pl.BlockSpec((tr, d), lambda i: (i, 1)),
                  pl.BlockSpec((1, d), lambda i: (0, 0)), pl.BlockSpec((1, d), lambda i: (0, 1))],
        out_specs=row,
        out_shape=jax.ShapeDtypeStruct((t, d), bf16),
        compiler_params=_params(("parallel",)),
    )(pa, pb, gl, gl, bg, bg)


def _merge_bwd(dm, pa, pb, gl, bg, *, name):
    t, d = pa.shape
    tr = _row_tile(t)

    def body(dm_ref, pa_ref, pb_ref, gla_ref, glb_ref, bga_ref, bgb_ref, dpa_ref, dpb_ref, dgl_ref, dbg_ref):
        dmv = dm_ref[...]
        ga = _sigmoid(gla_ref[...] + bga_ref[...])
        gb = _sigmoid(glb_ref[...] + bgb_ref[...])
        dpa_ref[...] = (dmv * ga).astype(bf16)
        dpb_ref[...] = (dmv * gb).astype(bf16)
        dla = dmv * pa_ref[...] * ga * (1.0 - ga)
        dlb = dmv * pb_ref[...] * gb * (1.0 - gb)
        dgl_ref[:, :d] = dla.astype(bf16)
        dgl_ref[:, d:] = dlb.astype(bf16)
        sa = jnp.sum(dla, axis=0, keepdims=True)
        sb = jnp.sum(dlb, axis=0, keepdims=True)

        @pl.when(pl.program_id(0) == 0)
        def _():
            dbg_ref[:, :d] = sa
            dbg_ref[:, d:] = sb

        @pl.when(pl.program_id(0) > 0)
        def _():
            dbg_ref[:, :d] += sa
            dbg_ref[:, d:] += sb

    row = pl.BlockSpec((tr, d), lambda i: (i, 0))
    return pl.pallas_call(
        body, name=name, grid=(t // tr,),
        in_specs=[row, row, row, row, pl.BlockSpec((tr, d), lambda i: (i, 1)),
                  pl.BlockSpec((1, d), lambda i: (0, 0)), pl.BlockSpec((1, d), lambda i: (0, 1))],
        out_specs=[row, row, pl.BlockSpec((tr, 2 * d), lambda i: (i, 0)), pl.BlockSpec((1, 2 * d), lambda i: (0, 0))],
        out_shape=[jax.ShapeDtypeStruct((t, d), bf16), jax.ShapeDtypeStruct((t, d), bf16),
                   jax.ShapeDtypeStruct((t, 2 * d), bf16), jax.ShapeDtypeStruct((1, 2 * d), f32)],
        compiler_params=_params(("arbitrary",)),
    )(dm, pa, pb, gl, gl, bg, bg)


_INV_SQRT2 = 1.0 / math.sqrt(2.0)
_INV_SQRT2PI = 1.0 / math.sqrt(2.0 * math.pi)


def _gelu(x):
    return 0.5 * x * (1.0 + lax.erf(x * _INV_SQRT2))


def _gelu_grad(x):
    return 0.5 * (1.0 + lax.erf(x * _INV_SQRT2)) + x * jnp.exp(-0.5 * x * x) * _INV_SQRT2PI


def _gmlp_common(uv, vg, vb):
    zz = _gelu(uv)
    u = zz[:, :GMLP_WIDTH]
    v = zz[:, GMLP_WIDTH:]
    mu = jnp.mean(v, axis=1, keepdims=True)
    vc = v - mu
    rstd = lax.rsqrt(jnp.mean(vc * vc, axis=1, keepdims=True) + NORM_EPS)
    vhat = vc * rstd
    vn = vhat * vg + vb
    return u, vhat, rstd, vn


def _gmlp_fwd(uv, vg, vb, wsp, bsp_t, *, name):
    t = uv.shape[0]
    nc = t // CHUNK

    def body(uv_ref, vg_ref, vb_ref, w_ref, b_ref, y_ref):
        u, _, _, vn = _gmlp_common(uv_ref[...], vg_ref[...], vb_ref[...])
        tril = _iota((CHUNK, CHUNK), 0) >= _iota((CHUNK, CHUNK), 1)
        bt = b_ref[...]
        for g in range(GMLP_GROUPS):
            sl = slice(g * CHUNK, (g + 1) * CHUNK)
            w = jnp.where(tril, w_ref[g], 0.0)
            s = _dot(w, vn[:, sl]) + bt[:, g:g + 1]
            y_ref[:, sl] = (u[:, sl] * s).astype(bf16)

    return pl.pallas_call(
        body, name=name, grid=(nc,),
        in_specs=[pl.BlockSpec((CHUNK, 2 * GMLP_WIDTH), lambda c: (c, 0)),
                  pl.BlockSpec((1, GMLP_WIDTH), lambda c: (0, 0)), pl.BlockSpec((1, GMLP_WIDTH), lambda c: (0, 0)),
                  pl.BlockSpec((GMLP_GROUPS, CHUNK, CHUNK), lambda c: (0, 0, 0)),
                  pl.BlockSpec((CHUNK, LANES), lambda c: (0, 0))],
        out_specs=pl.BlockSpec((CHUNK, GMLP_WIDTH), lambda c: (c, 0)),
        out_shape=jax.ShapeDtypeStruct((t, GMLP_WIDTH), bf16),
        compiler_params=_params(("parallel",)),
    )(uv, vg, vb, wsp, bsp_t)


def _gmlp_bwd(uv, dya, vg, vb, wsp, bsp_t, *, name):
    t = uv.shape[0]
    nc = t // CHUNK

    def body(uv_ref, dy_ref, vg_ref, vb_ref, w_ref, b_ref, duv_ref, dw_ref, db_ref, dvg_ref, dvb_ref):
        first = pl.program_id(0) == 0

        @pl.when(first)
        def _():
            dw_ref[...] = jnp.zeros_like(dw_ref)
            db_ref[...] = jnp.zeros_like(db_ref)
            dvg_ref[...] = jnp.zeros_like(dvg_ref)
            dvb_ref[...] = jnp.zeros_like(dvb_ref)

        uvv = uv_ref[...]
        vgv = vg_ref[...]
        u, vhat, rstd, vn = _gmlp_common(uvv, vgv, vb_ref[...])
        dy = dy_ref[...]
        tril = _iota((CHUNK, CHUNK), 0) >= _iota((CHUNK, CHUNK), 1)
        lane = _iota((CHUNK, LANES), 1)
        bt = b_ref[...]
        ds_all = dy * u
        dbacc = jnp.zeros((CHUNK, LANES), f32)
        dvh_parts = []
        for g in range(GMLP_GROUPS):
            sl = slice(g * CHUNK, (g + 1) * CHUNK)
            w = jnp.where(tril, w_ref[g], 0.0)
            vng = vn[:, sl]
            s = _dot(w, vng) + bt[:, g:g + 1]
            ds = ds_all[:, sl]
            duv_ref[:, sl] = (dy[:, sl] * s * _gelu_grad(uvv[:, sl])).astype(bf16)
            dw_ref[g] += jnp.where(tril, _dot(ds, vng, _NT), 0.0)
            dbacc = dbacc + jnp.where(lane == g, jnp.sum(ds, axis=1, keepdims=True), 0.0)
            dvn = _dot(w, ds, _TN)
            vh = vhat[:, sl]
            dvg_ref[:, sl] += jnp.sum(dvn * vh, axis=0, keepdims=True)
            dvb_ref[:, sl] += jnp.sum(dvn, axis=0, keepdims=True)
            dvh_parts.append(dvn * vgv[:, sl])
        db_ref[...] += dbacc
        dvhat = jnp.concatenate(dvh_parts, axis=1)
        m1 = jnp.mean(dvhat, axis=1, keepdims=True)
        m2 = jnp.mean(dvhat * vhat, axis=1, keepdims=True)
        dv = rstd * (dvhat - m1 - vhat * m2)
        duv_ref[:, GMLP_WIDTH:] = (dv * _gelu_grad(uvv[:, GMLP_WIDTH:])).astype(bf16)

    vec = pl.BlockSpec((1, GMLP_WIDTH), lambda c: (0, 0))
    return pl.pallas_call(
        body, name=name, grid=(nc,),
        in_specs=[pl.BlockSpec((CHUNK, 2 * GMLP_WIDTH), lambda c: (c, 0)),
                  pl.BlockSpec((CHUNK, GMLP_WIDTH), lambda c: (c, 0)), vec, vec,
                  pl.BlockSpec((GMLP_GROUPS, CHUNK, CHUNK), lambda c: (0, 0, 0)),
                  pl.BlockSpec((CHUNK, LANES), lambda c: (0, 0))],
        out_specs=[pl.BlockSpec((CHUNK, 2 * GMLP_WIDTH), lambda c: (c, 0)),
                   pl.BlockSpec((GMLP_GROUPS, CHUNK, CHUNK), lambda c: (0, 0, 0)),
                   pl.BlockSpec((CHUNK, LANES), lambda c: (0, 0)), vec, vec],
        out_shape=[jax.ShapeDtypeStruct((t, 2 * GMLP_WIDTH), bf16),
                   jax.ShapeDtypeStruct((GMLP_GROUPS, CHUNK, CHUNK), f32),
                   jax.ShapeDtypeStruct((CHUNK, LANES), f32),
                   jax.ShapeDtypeStruct((1, GMLP_WIDTH), f32), jax.ShapeDtypeStruct((1, GMLP_WIDTH), f32)],
        compiler_params=_params(("arbitrary",)),
    )(uv, dya, vg, vb, wsp, bsp_t)


_CONV_COLS = 512
_XS0, _B0, _C0 = 0, D_INNER, D_INNER + N_GROUPS * D_STATE


_TAIL = 8


def _conv_silu(cur_ref, tail_ref, w_ref, b_ref, has_prev, xc_ref, cv_ref):
    row = _iota((_TAIL, _CONV_COLS), 0)
    for j in range(CONV_DIM // _CONV_COLS):
        sl = slice(j * _CONV_COLS, (j + 1) * _CONV_COLS)
        cur = cur_ref[:, sl]
        tail = jnp.where(has_prev, tail_ref[:, sl], 0.0)
        acc = cur * w_ref[CONV_W - 1:CONV_W, sl] + b_ref[:, sl]
        for s in range(1, CONV_W):
            rolled = pltpu.roll(cur, s, 0)
            top = jnp.where(row >= s, rolled[:_TAIL], pltpu.roll(tail, s, 0))
            sh = jnp.concatenate([top, rolled[_TAIL:]], axis=0)
            acc = acc + sh * w_ref[CONV_W - 1 - s:CONV_W - s, sl]
        cv_ref[:, sl] = acc
        xc_ref[:, sl] = acc * _sigmoid(acc)


def _col_bcast(mat, h):
    return jnp.broadcast_to(mat[:, h:h + 1], (CHUNK, LANES))


def _head_expand(cols):
    lo = _iota((CHUNK, LANES), 1) < HEAD_DIM
    return jnp.concatenate([jnp.where(lo, cols[2 * j], cols[2 * j + 1]) for j in range(N_HEADS // 2)], axis=1)


def _ssd_chunk_scalars(dtr, dtb, alog):
    xdt_pre = dtr + dtb
    dtv = jnp.maximum(xdt_pre, 0.0) + jnp.log(1.0 + jnp.exp(-jnp.abs(xdt_pre)))
    a = -jnp.exp(alog)
    ltri = (_iota((CHUNK, CHUNK), 0) >= _iota((CHUNK, CHUNK), 1)).astype(f32)
    cs = _dot32(ltri, dtv * a)
    csb = [_col_bcast(cs, h) for h in range(N_HEADS)]
    cs_x = _head_expand(csb)
    dt_x = _head_expand([_col_bcast(dtv, h) for h in range(N_HEADS)])
    cl_x = cs_x[CHUNK - 1:CHUNK, :]
    return dict(xdt_pre=xdt_pre, dtv=dtv, a=a, cs=cs, cs_t=cs.T, csb=csb, dt_x=dt_x, e_x=jnp.exp(cs_x),
                dec_x=jnp.exp(cl_x - cs_x), dk_x=jnp.exp(cl_x))


def _head_masks():
    lane = _iota((CHUNK, GROUP_W), 1)
    return [(lane >= r * HEAD_DIM) & (lane < (r + 1) * HEAD_DIM) for r in range(HEADS_PER_GROUP)]


def _stack_heads(a, masks):
    return jnp.concatenate([jnp.where(m, a, 0.0) for m in masks], axis=0).astype(bf16)


def _seg_sum(a, seg):
    hi = a.astype(jnp.bfloat16)
    lo = (a - hi.astype(f32)).astype(jnp.bfloat16)
    return (lax.dot_general(hi, seg, _NN, preferred_element_type=f32)
            + lax.dot_general(lo, seg, _NN, preferred_element_type=f32))


def _head_seg_matrix():
    return (_iota((D_INNER, LANES), 0) // HEAD_DIM == _iota((D_INNER, LANES), 1)).astype(jnp.bfloat16)


def _ssd_fwd(xbc, z, dtr, cw, cb, dtb, alog, dsk_x, gs, *, ride=None, name):
    t = xbc.shape[0]
    nc = t // CHUNK
    tiles = CHUNK // _TAIL

    def body(*refs):
        cur_ref, tail_ref, z_ref, dtr_ref, cw_ref, cb_ref, dtb_ref, alog_ref, dsk_ref, gs_ref = refs[:10]
        if ride is None:
            yb_ref, hp_ref, cv_ref, state_ref, xc_ref = refs[10:]
        else:
            ride_ref, yb_ref, hp_ref, cv_ref, got_ref, state_ref, xc_ref, send_sems, recv_sems = refs[10:]
        c = pl.program_id(0)
        if ride is not None:
            start, relay, finish = _gather_protocol(ride_ref, got_ref, send_sems, recv_sems)
            pl.when(c == 0)(start)
            pl.when(c == nc // 2)(relay)

        @pl.when(c == 0)
        def _():
            state_ref[...] = jnp.zeros_like(state_ref)

        _conv_silu(cur_ref, tail_ref, cw_ref, cb_ref, c > 0, xc_ref, cv_ref)
        sc = _ssd_chunk_scalars(dtr_ref[...], dtb_ref[...], alog_ref[...])
        tril = _iota((CHUNK, CHUNK), 0) >= _iota((CHUNK, CHUNK), 1)
        masks = _head_masks()
        hp_ref[0] = state_ref[...]
        for g in range(N_GROUPS):
            gsl = slice(g * GROUP_W, (g + 1) * GROUP_W)
            xs_g = xc_ref[:, gsl]
            bg = xc_ref[:, _B0 + g * D_STATE:_B0 + (g + 1) * D_STATE]
            cg = xc_ref[:, _C0 + g * D_STATE:_C0 + (g + 1) * D_STATE]
            xdt_g = xs_g * sc["dt_x"][:, gsl]
            cbm = _dot(cg, bg, _NT)
            mw = jnp.concatenate(
                [cbm * jnp.exp(jnp.where(tril, sc["csb"][h] - sc["cs_t"][h:h + 1, :], -1e30))
                 for h in range(g * HEADS_PER_GROUP, (g + 1) * HEADS_PER_GROUP)], axis=1)
            ht_g = state_ref[:, gsl]
            y_g = _dot(mw, _stack_heads(xdt_g, masks)) + sc["e_x"][:, gsl] * _dot(cg, ht_g) + dsk_ref[:, gsl] * xs_g
            state_ref[:, gsl] = ht_g * sc["dk_x"][:, gsl] + _dot(bg, xdt_g * sc["dec_x"][:, gsl], _TN)
            zg = z_ref[:, gsl]
            yg = y_g * zg * _sigmoid(zg)
            rs = lax.rsqrt(jnp.mean(yg * yg, axis=1, keepdims=True) + NORM_EPS)
            yb_ref[:, gsl] = (yg * rs * gs_ref[:, gsl]).astype(bf16)
        if ride is not None:
            pl.when(c == nc - 1)(finish)

    def chunk(w):
        return pl.BlockSpec((CHUNK, w), lambda c: (c, 0))

    def const(shape):
        return pl.BlockSpec(shape, lambda c: (0,) * len(shape))

    riding = ride is not None
    return pl.pallas_call(
        body, name=name, grid=(nc,),
        in_specs=[chunk(CONV_DIM), pl.BlockSpec((_TAIL, CONV_DIM), lambda c: (jnp.maximum(c * tiles - 1, 0), 0)),
                  chunk(D_INNER), chunk(LANES), const((CONV_W, CONV_DIM)), const((1, CONV_DIM)),
                  const((1, LANES)), const((1, LANES)), const((1, D_INNER)), const((1, D_INNER))] + [_ANY] * riding,
        out_specs=[chunk(D_INNER), pl.BlockSpec((1, D_STATE, D_INNER), lambda c: (c, 0, 0)), chunk(CONV_DIM)]
        + [_ANY] * riding,
        out_shape=[jax.ShapeDtypeStruct((t, D_INNER), bf16), jax.ShapeDtypeStruct((nc, D_STATE, D_INNER), f32),
                   jax.ShapeDtypeStruct((t, CONV_DIM), f32)]
        + ([jax.ShapeDtypeStruct((N_CHIPS,) + ride.shape, ride.dtype)] if riding else []),
        scratch_shapes=[pltpu.VMEM((D_STATE, D_INNER), f32), pltpu.VMEM((CHUNK, CONV_DIM), f32)]
        + (list(_GATHER_SCRATCH) if riding else []),
        compiler_params=_params(("arbitrary",)),
    )(xbc, xbc, z, dtr, cw, cb, dtb, alog, dsk_x, gs, *([ride] if riding else []))


def _ssd_bwd(xbc, cv, z, dtr, hprev, dyb, cw, dtb, alog, dsk_x, gs, seg, *, ride=None, name):
    t = xbc.shape[0]
    nc = t // CHUNK

    def body(*refs):
        (cur_ref, cv_ref, z_ref, dtr_ref, hp_ref, dyb_ref, cw_ref, dtb_ref, alog_ref, dsk_ref, gs_ref,
         seg_ref) = refs[:12]
        rest = refs[12:]
        if ride is not None:
            ride_ref, got_ref, send_sems, recv_sems = rest[0], rest[10], rest[-2], rest[-1]
            rest = rest[1:10] + rest[11:-2]
        (dz_ref, dxbc_ref, ddt_ref, dcw_ref, dcb_ref, ddtb_ref, dalog_ref, ddsk_ref, dgs_ref,
         dh_ref, dcnext_ref, xc_ref, dxc_ref, x13_ref, x2_ref, rows_ref) = rest
        i = pl.program_id(0)
        cc = nc - 1 - i
        if ride is not None:
            start, finish = _scatter_protocol(ride_ref, got_ref, send_sems, recv_sems)
            pl.when(i == 0)(start)

        @pl.when(i == 0)
        def _():
            for ref in (dh_ref, dcnext_ref, dcw_ref, dcb_ref, ddtb_ref, dalog_ref, ddsk_ref, dgs_ref, rows_ref):
                ref[...] = jnp.zeros_like(ref)

        for j in range(CONV_DIM // _CONV_COLS):
            sl = slice(j * _CONV_COLS, (j + 1) * _CONV_COLS)
            cvv = cv_ref[:, sl]
            xc_ref[:, sl] = cvv * _sigmoid(cvv)
        sc = _ssd_chunk_scalars(dtr_ref[...], dtb_ref[...], alog_ref[...])
        tril = _iota((CHUNK, CHUNK), 0) >= _iota((CHUNK, CHUNK), 1)
        triu = _iota((CHUNK, CHUNK), 0) <= _iota((CHUNK, CHUNK), 1)
        masks = _head_masks()
        rowh = _iota((N_HEADS, CHUNK), 0)
        dcs_t = jnp.zeros((N_HEADS, CHUNK), f32)
        for g in range(N_GROUPS):
            gsl = slice(g * GROUP_W, (g + 1) * GROUP_W)
            xs_g = xc_ref[:, gsl]
            bg = xc_ref[:, _B0 + g * D_STATE:_B0 + (g + 1) * D_STATE]
            cg = xc_ref[:, _C0 + g * D_STATE:_C0 + (g + 1) * D_STATE]
            dt_g, e_g, dec_g, dk_g = sc["dt_x"][:, gsl], sc["e_x"][:, gsl], sc["dec_x"][:, gsl], sc["dk_x"][:, gsl]
            dsk_g = dsk_ref[:, gsl]
            xdt_g = xs_g * dt_g
            xdt_stack = _stack_heads(xdt_g, masks)
            cbm = _dot(cg, bg, _NT)
            cbt = _dot(bg, cg, _NT)
            heads = range(g * HEADS_PER_GROUP, (g + 1) * HEADS_PER_GROUP)
            lmats = [jnp.exp(jnp.where(tril, sc["csb"][h] - sc["cs_t"][h:h + 1, :], -1e30)) for h in heads]
            mw = jnp.concatenate([cbm * lm for lm in lmats], axis=1)
            mtw = jnp.concatenate(
                [cbt * jnp.exp(jnp.where(triu, sc["cs_t"][h:h + 1, :] - sc["csb"][h], -1e30)) for h in heads], axis=1)
            ht_g = hp_ref[0, :, gsl]
            dhn_g = dh_ref[:, gsl]
            yoff = e_g * _dot(cg, ht_g)
            y_g = _dot(mw, xdt_stack) + yoff + dsk_g * xs_g
            zg = z_ref[:, gsl]
            sz = _sigmoid(zg)
            silu = zg * sz
            yg = y_g * silu
            rs = lax.rsqrt(jnp.mean(yg * yg, axis=1, keepdims=True) + NORM_EPS)
            yn = yg * rs
            dyb = dyb_ref[:, gsl]
            dgs_ref[:, gsl] += jnp.sum(dyb * yn, axis=0, keepdims=True)
            dyn = dyb * gs_ref[:, gsl]
            dyg = rs * (dyn - yn * jnp.mean(dyn * yn, axis=1, keepdims=True))
            dy_g = dyg * silu
            dz_ref[:, gsl] = (dyg * y_g * (sz * (1.0 + zg * (1.0 - sz)))).astype(bf16)
            dy_stack = _stack_heads(dy_g, masks)
            dm_w = _dot(dy_g, xdt_stack, _NT)
            dmt_w = _dot(xdt_g, dy_stack, _NT)
            dxdt = _dot(mtw, dy_stack)
            dcb_acc = jnp.zeros((CHUNK, CHUNK), f32)
            for r, h in enumerate(heads):
                hs = slice(r * CHUNK, (r + 1) * CHUNK)
                dml = dm_w[:, hs] * lmats[r]
                dcb_acc = dcb_acc + dml
                col = jnp.sum(dml * cbm, axis=0, keepdims=True)
                row = jnp.sum(dmt_w[:, hs] * mtw[:, hs], axis=0, keepdims=True)
                dcs_t = dcs_t + jnp.where(rowh == h, row - col, 0.0)
            w = _dot(bg, dhn_g)
            dxdt = dxdt + dec_g * w
            decx3 = dec_g * (xdt_g * w)
            dg_g = e_g * dy_g
            d_c = _dot(dg_g, ht_g, _NT) + _dot(dcb_acc, bg)
            d_b = _dot(dcb_acc, cg, _TN) + _dot(xdt_g * dec_g, dhn_g, _NT)
            dh_ref[:, gsl] = dhn_g * dk_g + _dot(cg, dg_g, _TN)
            dxc_ref[:, gsl] = dsk_g * dy_g + dxdt * dt_g
            dxc_ref[:, _B0 + g * D_STATE:_B0 + (g + 1) * D_STATE] = d_b
            dxc_ref[:, _C0 + g * D_STATE:_C0 + (g + 1) * D_STATE] = d_c
            x13_ref[:, gsl] = dy_g * yoff - decx3
            x2_ref[:, gsl] = dxdt * xs_g
            rows_ref[0:1, gsl] = jnp.sum(dhn_g * ht_g, axis=0, keepdims=True)
            rows_ref[1:2, gsl] = jnp.sum(decx3, axis=0, keepdims=True)
            rows_ref[2:3, gsl] = jnp.sum(dy_g * xs_g, axis=0, keepdims=True)
        segm = seg_ref[...]
        r13 = _seg_sum(x13_ref[...], segm)
        r2 = _seg_sum(x2_ref[...], segm)
        small = _seg_sum(rows_ref[...], segm)
        lane = _iota((CHUNK, LANES), 1)
        rowi = _iota((CHUNK, LANES), 0)
        dcl_row = small[0:1, :] * jnp.exp(sc["cs"][CHUNK - 1:CHUNK, :]) + small[1:2, :]
        dcs = r13 + jnp.where(rowi == CHUNK - 1, dcl_row, 0.0)
        dcs_t_all = dcs.T + jnp.concatenate([dcs_t, jnp.zeros((LANES - N_HEADS, CHUNK), f32)], axis=0)
        dda = _dot32(dcs_t_all, tril.astype(f32)).T
        a = sc["a"]
        ddt_total = r2 + dda * a
        dalog_ref[...] += jnp.sum(dda * sc["dtv"], axis=0, keepdims=True) * a
        ddtr = jnp.where(lane < N_HEADS, ddt_total * _sigmoid(sc["xdt_pre"]), 0.0)
        ddtb_ref[...] += jnp.sum(ddtr, axis=0, keepdims=True)
        ddt_ref[...] = ddtr.astype(bf16)
        ddsk_ref[...] += small[2:3, :]
        row8 = _iota((_TAIL, _CONV_COLS), 0)
        for j in range(CONV_DIM // _CONV_COLS):
            sl = slice(j * _CONV_COLS, (j + 1) * _CONV_COLS)
            cvv = cv_ref[:, sl]
            sg = _sigmoid(cvv)
            dconv = dxc_ref[:, sl] * (sg * (1.0 + cvv * (1.0 - sg)))
            nxt = dcnext_ref[:, sl]
            cur = cur_ref[:, sl]
            dxin = dconv * cw_ref[CONV_W - 1:CONV_W, sl]
            dcw_ref[CONV_W - 1:CONV_W, sl] += jnp.sum(dconv * cur, axis=0, keepdims=True)
            for s in range(1, CONV_W):
                rolled = pltpu.roll(dconv, CHUNK - s, 0)
                bot = jnp.where(row8 < _TAIL - s, rolled[CHUNK - _TAIL:], pltpu.roll(nxt, _TAIL - s, 0))
                up = jnp.concatenate([rolled[:CHUNK - _TAIL], bot], axis=0)
                dxin = dxin + up * cw_ref[CONV_W - 1 - s:CONV_W - s, sl]
                dcw_ref[CONV_W - 1 - s:CONV_W - s, sl] += jnp.sum(up * cur, axis=0, keepdims=True)
            dcb_ref[:, sl] += jnp.sum(dconv, axis=0, keepdims=True)
            dxbc_ref[:, sl] = dxin.astype(bf16)
            dcnext_ref[:, sl] = dconv[:_TAIL]
        if ride is not None:
            pl.when(i == nc - 1)(finish)

    def chunk(w):
        return pl.BlockSpec((CHUNK, w), lambda i: (nc - 1 - i, 0))

    def const(shape):
        return pl.BlockSpec(shape, lambda i: (0,) * len(shape))

    riding = ride is not None
    return pl.pallas_call(
        body, name=name, grid=(nc,),
        in_specs=[chunk(CONV_DIM), chunk(CONV_DIM),
                  chunk(D_INNER), chunk(LANES), pl.BlockSpec((1, D_STATE, D_INNER), lambda i: (nc - 1 - i, 0, 0)),
                  chunk(D_INNER), const((CONV_W, CONV_DIM)),
                  const((1, LANES)), const((1, LANES)), const((1, D_INNER)), const((1, D_INNER)),
                  const((D_INNER, LANES))] + [_ANY] * riding,
        out_specs=[chunk(D_INNER), chunk(CONV_DIM), chunk(LANES), const((CONV_W, CONV_DIM)), const((1, CONV_DIM)),
                   const((1, LANES)), const((1, LANES)), const((1, LANES)), const((1, D_INNER))] + [_ANY] * riding,
        out_shape=[jax.ShapeDtypeStruct((t, D_INNER), bf16), jax.ShapeDtypeStruct((t, CONV_DIM), bf16),
                   jax.ShapeDtypeStruct((t, LANES), bf16), jax.ShapeDtypeStruct((CONV_W, CONV_DIM), f32),
                   jax.ShapeDtypeStruct((1, CONV_DIM), f32), jax.ShapeDtypeStruct((1, LANES), f32),
                   jax.ShapeDtypeStruct((1, LANES), f32), jax.ShapeDtypeStruct((1, LANES), f32),
                   jax.ShapeDtypeStruct((1, D_INNER), f32)]
        + ([jax.ShapeDtypeStruct((N_CHIPS - 1,) + ride.shape[1:], ride.dtype)] if riding else []),
        scratch_shapes=[pltpu.VMEM((D_STATE, D_INNER), f32), pltpu.VMEM((_TAIL, CONV_DIM), f32),
                        pltpu.VMEM((CHUNK, CONV_DIM), f32), pltpu.VMEM((CHUNK, CONV_DIM), f32),
                        pltpu.VMEM((CHUNK, D_INNER), f32), pltpu.VMEM((CHUNK, D_INNER), f32),
                        pltpu.VMEM((_TAIL, D_INNER), f32)]
        + (list(_SCATTER_SCRATCH) if riding else []),
        compiler_params=_params(("arbitrary",)),
    )(xbc, cv, z, dtr, hprev, dyb, cw, dtb, alog, dsk_x, gs, seg, *([ride] if riding else []))


def _adamw(w, g, m, v, *, name):
    r, c = w.shape
    tr = r
    while tr * c * 4 > _MB and tr % 16 == 0:
        tr //= 2

    def body(w_ref, g_ref, m_ref, v_ref, d_ref, m2_ref, v2_ref):
        gv = g_ref[...]
        m2 = ADAM_B1 * m_ref[...] + (1.0 - ADAM_B1) * gv
        v2 = ADAM_B2 * v_ref[...] + (1.0 - ADAM_B2) * (gv * gv)
        m_hat = m2 / (1.0 - ADAM_B1 ** ADAM_STEP)
        v_hat = v2 / (1.0 - ADAM_B2 ** ADAM_STEP)
        d_ref[...] = -ADAM_LR * (m_hat / (jnp.sqrt(v_hat) + ADAM_EPS) + ADAM_WD * w_ref[...])
        m2_ref[...] = m2
        v2_ref[...] = v2

    blk = pl.BlockSpec((tr, c), lambda i: (i, 0))
    return pl.pallas_call(
        body, name=name, grid=(r // tr,),
        in_specs=[blk] * 4, out_specs=[blk] * 3,
        out_shape=[jax.ShapeDtypeStruct((r, c), f32)] * 3,
        compiler_params=_params(("parallel",)),
    )(w, g, m, v)


def _row_block(rows, cols):
    cap = max(16, _MB // (4 * cols))
    return max(tr for tr in range(16, min(cap, rows) + 1, 16) if rows % tr == 0)


def _cast_bf16(a, *, name):
    r, c = a.shape
    tr = _row_block(r, c)

    def body(a_ref, o_ref):
        o_ref[...] = a_ref[...].astype(bf16)

    blk = pl.BlockSpec((tr, c), lambda i: (i, 0))
    return pl.pallas_call(
        body, name=name, grid=(r // tr,), in_specs=[blk], out_specs=blk,
        out_shape=jax.ShapeDtypeStruct((r, c), bf16), compiler_params=_params(("parallel",)),
    )(a)


_ANY = pl.BlockSpec(memory_space=pl.ANY)


def _place():
    x, y, c = lax.axis_index("x"), lax.axis_index("y"), lax.axis_index("c")
    other_chips = [(1 - x, y), (x, 1 - y), (1 - x, 1 - y)]
    return x, y, c, other_chips


def _gather_protocol(in_ref, out_ref, send_sems, recv_sems):
    x, y, c, chips = _place()
    me = 2 * x + y
    sibling = (x, y, 1 - c)

    def cp(k, chip, half, to, src=None):
        dst = out_ref.at[chip, half]
        return pltpu.make_async_remote_copy(
            src_ref=dst if src is None else src, dst_ref=dst, send_sem=send_sems.at[k], recv_sem=recv_sems.at[k],
            device_id=to, device_id_type=MESH)

    def sends():
        return [cp(j, me, c, (cx, cy, c), src=in_ref.at[c]) for j, (cx, cy) in enumerate(chips)]

    def relays():
        return [cp(3 + j, 2 * cx + cy, c, sibling) for j, (cx, cy) in enumerate(chips)]

    def start():
        for f in sends():
            f.start()

    def relay():
        onward = relays()
        for j, (cx, cy) in enumerate(chips):
            cp(j, 2 * cx + cy, c, sibling).wait_recv()
            onward[j].start()

    def finish():
        for j, (cx, cy) in enumerate(chips):
            cp(3 + j, 2 * cx + cy, 1 - c, sibling).wait_recv()
        for f in sends() + relays():
            f.wait_send()

    return start, relay, finish


_GATHER_SCRATCH = [pltpu.SemaphoreType.DMA((6,)), pltpu.SemaphoreType.DMA((6,))]


def _gather_shards(shard, *, name):
    _, rh, lanes = shard.shape

    def body(in_ref, out_ref, send_sems, recv_sems):
        start, relay, finish = _gather_protocol(in_ref, out_ref, send_sems, recv_sems)
        start()
        relay()
        finish()

    return pl.pallas_call(
        body, name=name, in_specs=[_ANY], out_specs=_ANY,
        out_shape=jax.ShapeDtypeStruct((N_CHIPS, 2, rh, lanes), shard.dtype),
        scratch_shapes=list(_GATHER_SCRATCH),
    )(shard)


def _scatter_protocol(p_ref, out_ref, send_sems, recv_sems):
    x, y, c, chips = _place()

    def copies():
        return [pltpu.make_async_remote_copy(
            src_ref=p_ref.at[2 * cx + cy], dst_ref=out_ref.at[j], send_sem=send_sems.at[j], recv_sem=recv_sems.at[j],
            device_id=(cx, cy, c), device_id_type=MESH) for j, (cx, cy) in enumerate(chips)]

    def start():
        for cpy in copies():
            cpy.start()

    def finish():
        for cpy in copies():
            cpy.wait()

    return start, finish


_SCATTER_SCRATCH = [pltpu.SemaphoreType.DMA((3,)), pltpu.SemaphoreType.DMA((3,))]


def _rs_swap_halves(g, *, name):
    nch, _, rh, lanes = g.shape

    def body(g_ref, out_ref, send_sems, recv_sems):
        x, y, c, _ = _place()
        copies = [pltpu.make_async_remote_copy(
            src_ref=g_ref.at[k, 1 - c], dst_ref=out_ref.at[k], send_sem=send_sems.at[k], recv_sem=recv_sems.at[k],
            device_id=(x, y, 1 - c), device_id_type=MESH) for k in range(nch)]
        for cpy in copies:
            cpy.start()
        for cpy in copies:
            cpy.wait()

    return pl.pallas_call(
        body, name=name, in_specs=[_ANY], out_specs=_ANY,
        out_shape=jax.ShapeDtypeStruct((nch, rh, lanes), g.dtype),
        scratch_shapes=[pltpu.SemaphoreType.DMA((nch,)), pltpu.SemaphoreType.DMA((nch,))],
    )(g)


def _rs_add_pair(g, got, c_idx, *, name):
    nch, _, rh, lanes = g.shape
    tr = _row_block(rh, lanes)

    def body(c_ref, g_ref, got_ref, p32_ref, p16_ref):
        s = g_ref[...] + got_ref[...]
        p32_ref[...] = s
        p16_ref[...] = s.astype(bf16)

    blk = pl.BlockSpec((None, tr, lanes), lambda k, i, c_ref: (k, i, 0))
    return pl.pallas_call(
        body, name=name,
        grid_spec=pltpu.PrefetchScalarGridSpec(
            num_scalar_prefetch=1, grid=(nch, rh // tr),
            in_specs=[pl.BlockSpec((None, None, tr, lanes), lambda k, i, c_ref: (k, c_ref[0], i, 0)), blk],
            out_specs=[blk, blk]),
        out_shape=[jax.ShapeDtypeStruct((nch, rh, lanes), f32), jax.ShapeDtypeStruct((nch, rh, lanes), bf16)],
        compiler_params=_params(("parallel", "parallel")),
    )(c_idx, g, got)


def _rs_add_chips(p32, got, place, *, name):
    _, rh, lanes = p32.shape
    tr = _row_block(rh, lanes)

    def body(place_ref, p_ref, got_ref, o_ref):
        o_ref[...] = ((p_ref[...] + got_ref[0].astype(f32)) + got_ref[1].astype(f32)) + got_ref[2].astype(f32)

    return pl.pallas_call(
        body, name=name,
        grid_spec=pltpu.PrefetchScalarGridSpec(
            num_scalar_prefetch=1, grid=(rh // tr,),
            in_specs=[pl.BlockSpec((None, tr, lanes), lambda i, place_ref: (place_ref[0], i, 0)),
                      pl.BlockSpec((3, tr, lanes), lambda i, place_ref: (0, i, 0))],
            out_specs=pl.BlockSpec((None, tr, lanes), lambda i, place_ref: (place_ref[1], i, 0))),
        out_shape=jax.ShapeDtypeStruct((2, rh, lanes), f32),
        compiler_params=_params(("parallel",)),
    )(place, p32, got)


def _rs_join_halves(halves, *, name):
    def body(h_ref, out_ref, send_sem, recv_sem):
        x, y, c, _ = _place()
        cpy = pltpu.make_async_remote_copy(
            src_ref=h_ref.at[c], dst_ref=out_ref.at[c], send_sem=send_sem, recv_sem=recv_sem,
            device_id=(x, y, 1 - c), device_id_type=MESH)
        cpy.start()
        cpy.wait()

    return pl.pallas_call(
        body, name=name, in_specs=[_ANY], out_specs=_ANY,
        out_shape=jax.ShapeDtypeStruct(halves.shape, halves.dtype), input_output_aliases={0: 0},
        scratch_shapes=[pltpu.SemaphoreType.DMA, pltpu.SemaphoreType.DMA],
    )(halves)


def _all_reduce_small(s, *, name):
    rs, lanes = s.shape

    def body(s_ref, o_ref, buf_ref, send_sems, recv_sems):
        x, y, c, _ = _place()
        me = 4 * x + 2 * y + c
        peers = []
        for k in range(1, N_DEV):
            px = 1 - x if (k >> 2) & 1 else x
            py = 1 - y if (k >> 1) & 1 else y
            pc = 1 - c if k & 1 else c
            peers.append((px, py, pc))
        copies = [pltpu.make_async_remote_copy(
            src_ref=s_ref, dst_ref=buf_ref.at[me], send_sem=send_sems.at[k], recv_sem=recv_sems.at[k],
            device_id=peer, device_id_type=MESH) for k, peer in enumerate(peers)]
        for cpy in copies:
            cpy.start()
        buf_ref[me] = s_ref[...]
        for k, (px, py, pc) in enumerate(peers):
            pltpu.make_async_remote_copy(
                src_ref=s_ref, dst_ref=buf_ref.at[4 * px + 2 * py + pc], send_sem=send_sems.at[k],
                recv_sem=recv_sems.at[k], device_id=(px, py, pc), device_id_type=MESH).wait_recv()
        for cpy in copies:
            cpy.wait_send()
        acc = buf_ref[0]
        for d in range(1, N_DEV):
            acc = acc + buf_ref[d]
        o_ref[...] = acc

    vm = pl.BlockSpec(memory_space=pltpu.VMEM)
    return pl.pallas_call(
        body, name=name, in_specs=[vm], out_specs=vm,
        out_shape=jax.ShapeDtypeStruct((rs, lanes), f32),
        scratch_shapes=[pltpu.VMEM((N_DEV, rs, lanes), f32), pltpu.SemaphoreType.DMA((N_DEV - 1,)),
                        pltpu.SemaphoreType.DMA((N_DEV - 1,))],
        compiler_params=pltpu.CompilerParams(vmem_limit_bytes=32 * _MB),
    )(s)


def _pad_lanes(a, width=LANES):
    return jnp.pad(a, ((0, 0), (0, width - a.shape[1])))


def _local_grads(x, tgt, wts, small, *, fwd_ride=None, late_weights=None, bwd_ride=None, last_ride=None):
    t = x.shape[0]
    tm = min(t, 1024)
    d = D_MODEL
    mm = functools.partial(_matmul, tm=tm)

    dtb = _pad_lanes(small["dt_bias"])
    alog = _pad_lanes(small["a_log"])
    dsk = jnp.repeat(small["d_skip"], HEAD_DIM, axis=1)
    bsp_t = _pad_lanes(small["b_spatial"].T)
    wsp = small["w_spatial"]

    h = _rms_fwd(x, small["norm_mix_g"], name="rms_mix")
    uv = mm(h, wts["uv"], tn=1024, tk=d, out_dtypes=[f32], name="proj_uv")
    z = mm(h, wts["z"], tn=1024, tk=d, out_dtypes=[f32], name="proj_z")
    xbc = mm(h, wts["xbc"], tn=1024, tk=d, out_dtypes=[f32], name="proj_xbc")
    dtr = mm(h, wts["dt"], tn=LANES, tk=d, out_dtypes=[f32], name="proj_dt")
    gl = mm(h, wts["gate"], tn=1024, tk=d, out_dtypes=[f32], name="proj_gate")
    ya = _gmlp_fwd(uv, small["v_norm_g"], small["v_norm_b"], wsp, bsp_t, name="gmlp_fwd")
    yb, hprev, cv, *gathered = _ssd_fwd(xbc, z, dtr, small["conv_w"], small["conv_b"], dtb, alog, dsk,
                                    small["ssm_norm_g"], ride=fwd_ride, name="ssd_fwd")
    if fwd_ride is not None:
        wts = {**wts, **late_weights(gathered[0])}
    pa = mm(ya, wts["pa"], tn=1024, tk=1024, out_dtypes=[f32], name="proj_a")
    pb = mm(yb, wts["pb"], tn=1024, tk=1024, out_dtypes=[f32], name="proj_b")
    merged = _merge_fwd(pa, pb, gl, small["b_gates"], name="merge_fwd")
    x1 = mm(merged, wts["out"], tn=1024, tk=1024, out_dtypes=[f32], extras=[x],
            epilogue=lambda acc, res: (res + acc,), name="out_proj")
    h2 = _rms_fwd(x1, small["norm_mlp_g"], name="rms_mlp")
    act = mm(h2, wts["up"], tn=1024, tk=d, out_dtypes=[bf16],
             epilogue=lambda acc: (jnp.square(jnp.maximum(acc, 0.0)),), name="mlp_up")
    x2 = mm(act, wts["down"], tn=1024, tk=2048, out_dtypes=[f32], extras=[x1],
            epilogue=lambda acc, res: (res + acc,), name="mlp_down")

    dx2, dx2b, dgf, loss = _loss_head(x2, tgt, small["norm_final_g"], name="loss_head")
    tt = min(t, 2048)
    tn_mm = functools.partial(_matmul_tn, tt=tt)
    dw = {}
    dw["down"] = tn_mm(act, dx2b, tka=1024, tn=1024, name="dw_down")
    dup = mm(dx2b, wts["down"], nt=True, tn=1024, tk=1024, out_dtypes=[bf16], extras=[act],
             epilogue=lambda acc, a2: (acc * (2.0 * jnp.sqrt(a2).astype(f32)),), name="d_act")
    dw["up"] = tn_mm(h2, dup, tka=1024, tn=1024, name="dw_up")
    dh2 = mm(dup, wts["up"], nt=True, tn=1024, tk=2048, out_dtypes=[f32], name="d_h2")
    dx1, dx1b, dg_mlp = _rms_bwd(x1, small["norm_mlp_g"], dh2, dx2, want_bf16=True, name="rms_mlp_bwd")
    dw["out"] = tn_mm(merged, dx1b, tka=1024, tn=1024, name="dw_out")
    dmerged = mm(dx1b, wts["out"], nt=True, tn=1024, tk=1024, out_dtypes=[f32], name="d_merged")
    dpa, dpb, dgl, dbg = _merge_bwd(dmerged, pa, pb, gl, small["b_gates"], name="merge_bwd")
    dw["pa"] = tn_mm(ya, dpa, tka=1024, tn=1024, name="dw_pa")
    dw["pb"] = tn_mm(yb, dpb, tka=1024, tn=1024, name="dw_pb")
    dya = mm(dpa, wts["pa"], nt=True, tn=1024, tk=1024, out_dtypes=[f32], name="d_ya")
    dyb = mm(dpb, wts["pb"], nt=True, tn=1024, tk=1024, out_dtypes=[f32], name="d_yb")
    duv, dwsp, dbsp_t, dvg, dvb = _gmlp_bwd(uv, dya, small["v_norm_g"], small["v_norm_b"], wsp, bsp_t,
                                            name="gmlp_bwd")
    ride = bwd_ride(dw) if bwd_ride is not None else None
    dz, dxbc, ddt, dcw, dcb, ddtb, dalog, ddsk, dgs, *got = _ssd_bwd(
        xbc, cv, z, dtr, hprev, dyb, small["conv_w"], dtb, alog, dsk, small["ssm_norm_g"],
        _head_seg_matrix(), ride=ride, name="ssd_bwd")
    dw["uv"] = tn_mm(h, duv, tka=1024, tn=1024, name="dw_uv")
    dw["z"] = tn_mm(h, dz, tka=1024, tn=1024, name="dw_z")
    dw["xbc"] = tn_mm(h, dxbc, tka=1024, tn=1024, name="dw_xbc")
    dw["dt"] = tn_mm(h, ddt, tka=1024, tn=LANES, name="dw_dt")
    dw["gate"] = tn_mm(h, dgl, tka=1024, tn=1024, name="dw_gate")
    last = last_ride(dw) if last_ride is not None else None
    res = _matmul_nt_sum(
        [(duv, wts["uv"]), (dz, wts["z"]), (dxbc, wts["xbc"]), (dgl, wts["gate"]), (ddt, wts["dt"])],
        tm=tm, tks=[1024] * 4 + [LANES], ride=last, name="d_h")
    dh, got_last = (res[0], res[1]) if last is not None else (res, None)
    dx, dg_mix = _rms_bwd(x, small["norm_mix_g"], dh, dx1, want_bf16=False, name="rms_mix_bwd")

    dsmall = {
        "norm_mix_g": dg_mix, "conv_w": dcw, "conv_b": dcb, "dt_bias": ddtb[:, :N_HEADS], "a_log": dalog[:, :N_HEADS],
        "d_skip": ddsk[:, :N_HEADS], "ssm_norm_g": dgs, "v_norm_g": dvg, "v_norm_b": dvb, "w_spatial": dwsp,
        "b_spatial": dbsp_t[:, :GMLP_GROUPS].T, "b_gates": dbg, "norm_mlp_g": dg_mlp, "norm_final_g": dgf,
    }
    return loss, dx, dw, dsmall, (got[0] if got else None), got_last


_IN_SHARD = IN_PROJ // N_CHIPS
_LATE = ("w_proj_a", "w_proj_b", "w_out", "w_mlp_up", "w_mlp_down")
_LATE_ROWS = {"w_proj_a": GMLP_WIDTH // N_CHIPS, "w_proj_b": D_INNER // N_CHIPS, "w_out": D_MODEL // N_CHIPS,
              "w_mlp_up": D_MODEL, "w_mlp_down": D_FF // N_CHIPS}
_LATE_TOTAL = sum(_LATE_ROWS.values())


def _late_offsets():
    off, out = 0, {}
    for k in _LATE:
        out[k] = off
        off += _LATE_ROWS[k]
    return out


_LATE_OFF = _late_offsets()

_SMALL = ("norm_mix_g", "conv_w", "conv_b", "dt_bias", "a_log", "d_skip", "ssm_norm_g", "v_norm_g", "v_norm_b",
          "w_spatial", "b_spatial", "b_gates", "norm_mlp_g", "norm_final_g")


def _pack_small(parts):
    flat = jnp.concatenate([parts[k].reshape(-1) for k in _SMALL])
    rows = -(-flat.shape[0] // (8 * LANES)) * 8
    return jnp.pad(flat, (0, rows * LANES - flat.shape[0])).reshape(rows, LANES)


def _unpack_small(packed, shapes):
    flat = packed.reshape(-1)
    out, off = {}, 0
    for k in _SMALL:
        n = math.prod(shapes[k])
        out[k] = flat[off:off + n].reshape(shapes[k])
        off += n
    return out


def _from_chip_columns(stacked):
    _, rows, cols = stacked.shape
    return stacked.transpose(1, 0, 2).reshape(rows, N_CHIPS * cols)


def _to_chip_columns(full):
    rows, cols = full.shape
    return full.reshape(rows, N_CHIPS, cols // N_CHIPS).transpose(1, 0, 2)


def _w_in_grad_by_chip(dw):
    pieces = [dw["uv"], dw["z"], dw["xbc"], dw["dt"][:, :N_HEADS], dw["gate"]]
    bounds = [0]
    for p in pieces:
        bounds.append(bounds[-1] + p.shape[1])
    chips = []
    for k in range(N_CHIPS):
        lo, hi = k * _IN_SHARD, (k + 1) * _IN_SHARD
        parts = [p[:, max(lo, b0) - b0:min(hi, b1) - b0]
                 for p, b0, b1 in zip(pieces, bounds[:-1], bounds[1:]) if min(hi, b1) > max(lo, b0)]
        chips.append(jnp.concatenate(parts, axis=1))
    return jnp.stack(chips)


def kernel(x, norm_mix_g, w_in, conv_w, conv_b, dt_bias, a_log, d_skip, ssm_norm_g, v_norm_g, v_norm_b, w_spatial, b_spatial, b_gates, w_proj_a, w_proj_b, w_out, norm_mlp_g, w_mlp_up, w_mlp_down, norm_final_g, loss_target, m_norm_mix_g, m_w_in, m_conv_w, m_conv_b, m_dt_bias, m_a_log, m_d_skip, m_ssm_norm_g, m_v_norm_g, m_v_norm_b, m_w_spatial, m_b_spatial, m_b_gates, m_w_proj_a, m_w_proj_b, m_w_out, m_norm_mlp_g, m_w_mlp_up, m_w_mlp_down, m_norm_final_g, v_norm_mix_g, v_w_in, v_conv_w, v_conv_b, v_dt_bias, v_a_log, v_d_skip, v_ssm_norm_g, v_v_norm_g, v_v_norm_b, v_w_spatial, v_b_spatial, v_b_gates, v_w_proj_a, v_w_proj_b, v_w_out, v_norm_mlp_g, v_w_mlp_up, v_w_mlp_down, v_norm_final_g):
    given = dict(locals())
    names = ("norm_mix_g", "w_in", "conv_w", "conv_b", "dt_bias", "a_log", "d_skip", "ssm_norm_g", "v_norm_g",
             "v_norm_b", "w_spatial", "b_spatial", "b_gates", "w_proj_a", "w_proj_b", "w_out", "norm_mlp_g",
             "w_mlp_up", "w_mlp_down", "norm_final_g")
    xi, yi, ci = lax.axis_index("x"), lax.axis_index("y"), lax.axis_index("c")
    me_chip = (2 * xi + yi).astype(jnp.int32)

    def halves(a):
        return a.reshape(2, a.shape[0] // 2, a.shape[1])

    def with_own(got, shard):
        whole = lax.dynamic_update_slice(got, shard[None], (me_chip, 0, 0, 0))
        return whole.reshape(N_CHIPS, 2 * shard.shape[1], shard.shape[2])

    shard_in = halves(_cast_bf16(w_in[0], name="cast_w_in"))
    shard_late = halves(_cast_bf16(jnp.concatenate([given[k][0] for k in _LATE]), name="cast_w_late"))
    shard_conv = halves(conv_w.reshape(2 * _TAIL, -1))
    w_in_full = _from_chip_columns(with_own(_gather_shards(shard_in, name="gather_w_in"), shard_in))
    o_dt, o_gate = 2 * GMLP_WIDTH + D_INNER + CONV_DIM, 2 * GMLP_WIDTH + D_INNER + CONV_DIM + N_HEADS
    wts = {
        "uv": w_in_full[:, :2 * GMLP_WIDTH], "z": w_in_full[:, 2 * GMLP_WIDTH:2 * GMLP_WIDTH + D_INNER],
        "xbc": w_in_full[:, 2 * GMLP_WIDTH + D_INNER:o_dt], "dt": _pad_lanes(w_in_full[:, o_dt:o_gate]),
        "gate": w_in_full[:, o_gate:],
    }
    conv_all = with_own(_gather_shards(shard_conv, name="gather_conv_w"), shard_conv)
    conv_full = _from_chip_columns(conv_all.reshape(N_CHIPS, CONV_W, CONV_DIM // N_CHIPS))

    def late_weights(got):
        g_late = with_own(got, shard_late)

        def rows_of(k):
            return g_late[:, _LATE_OFF[k]:_LATE_OFF[k] + _LATE_ROWS[k]]

        return {
            "pa": rows_of("w_proj_a").reshape(GMLP_WIDTH, D_MODEL),
            "pb": rows_of("w_proj_b").reshape(D_INNER, D_MODEL), "out": rows_of("w_out").reshape(D_MODEL, D_MODEL),
            "up": _from_chip_columns(rows_of("w_mlp_up")), "down": rows_of("w_mlp_down").reshape(D_FF, D_MODEL),
        }

    small = {
        "norm_mix_g": norm_mix_g, "conv_w": conv_full, "conv_b": conv_b, "dt_bias": dt_bias, "a_log": a_log,
        "d_skip": d_skip, "ssm_norm_g": ssm_norm_g, "v_norm_g": v_norm_g, "v_norm_b": v_norm_b,
        "w_spatial": w_spatial[0], "b_spatial": b_spatial[0], "b_gates": b_gates, "norm_mlp_g": norm_mlp_g,
        "norm_final_g": norm_final_g.reshape(1, D_MODEL),
    }

    c_idx = ci.astype(jnp.int32).reshape(1)
    place = jnp.stack([me_chip, ci.astype(jnp.int32)])
    partials = {}

    def pair_sums(g, tag):
        g = g.reshape(N_CHIPS, 2, g.shape[1] // 2, g.shape[2])
        p32, p16 = _rs_add_pair(g, _rs_swap_halves(g, name="rs_swap_" + tag), c_idx, name="rs_add_pair_" + tag)
        partials[tag] = p32
        return p16

    def reduced_shard(tag, got_chips):
        own = _rs_add_chips(partials[tag], got_chips, place, name="rs_add_chips_" + tag)
        both = _rs_join_halves(own, name="rs_join_" + tag)
        return both.reshape(2 * both.shape[1], both.shape[2])

    def late_partials(dw):
        def by_rows(a):
            return a.reshape(N_CHIPS, a.shape[0] // N_CHIPS, a.shape[1])

        return pair_sums(jnp.concatenate([by_rows(dw["pa"]), by_rows(dw["pb"]), by_rows(dw["out"]),
                                          _to_chip_columns(dw["up"]), by_rows(dw["down"])], axis=1), "late")

    def in_partials(dw):
        return pair_sums(_w_in_grad_by_chip(dw), "in")

    loss_part, grad_x, dw, dsmall, got_late, got_in = _local_grads(
        x[0], loss_target[0], wts, small, fwd_ride=shard_late, late_weights=late_weights, bwd_ride=late_partials,
        last_ride=in_partials)
    loss = lax.psum(loss_part[0, 0], ("x", "y", "c"))
    g_late = reduced_shard("late", got_late)
    g_in_shard = reduced_shard("in", got_in)

    small_shapes = {k: dsmall[k].shape for k in _SMALL}
    red = _unpack_small(_all_reduce_small(_pack_small(dsmall), name="all_reduce_small"), small_shapes)
    conv_cols = CONV_DIM // N_CHIPS
    red["conv_w"] = lax.dynamic_slice_in_dim(red["conv_w"], me_chip * conv_cols, conv_cols, axis=1)

    grads, deltas, new_m, new_v = {}, {}, {}, {}
    for k in ("w_in",) + _LATE:
        g2 = g_in_shard if k == "w_in" else g_late[_LATE_OFF[k]:_LATE_OFF[k] + _LATE_ROWS[k]]
        dlt, m2, v2 = _adamw(given[k][0], g2, given["m_" + k][0], given["v_" + k][0], name="adamw_" + k)
        grads[k], deltas[k], new_m[k], new_v[k] = g2, dlt, m2, v2
    adam_shapes = dict(small_shapes)
    adam_shapes["conv_w"] = (CONV_W, conv_cols)

    def small_pack_of(prefix):
        return _pack_small({k: given[prefix + k].reshape(adam_shapes[k]) for k in _SMALL})

    dlt_s, m_s, v_s = _adamw(small_pack_of(""), _pack_small(red), small_pack_of("m_"), small_pack_of("v_"),
                             name="adamw_small")
    for dst, packed in ((deltas, dlt_s), (new_m, m_s), (new_v, v_s)):
        dst.update(_unpack_small(packed, adam_shapes))
    grads.update(red)

    def shaped(dct):
        return [dct[k].reshape(given[k].shape) for k in names]

    return (loss, grad_x[None], *shaped(grads), *shaped(deltas), *shaped(new_m), *shaped(new_v))
```

```python
import functools
import math

import jax
import jax.numpy as jnp
from jax import lax
from jax.experimental import pallas as pl
from jax.experimental.pallas import tpu as pltpu

f32 = jnp.float32
bf16 = jnp.bfloat16

D_MODEL = 1024
CHUNK = 128
GMLP_WIDTH = 1024
GMLP_GROUPS = 8
D_INNER = 2048
HEAD_DIM = 64
N_HEADS = 32
N_GROUPS = 8
HEADS_PER_GROUP = 4
GROUP_W = HEADS_PER_GROUP * HEAD_DIM
D_STATE = 128
CONV_W = 4
CONV_DIM = 4096
D_FF = 4096
IN_PROJ = 10272
NORM_EPS = 1e-6
N_CHIPS = 4
N_DEV = 8
LANES = 128

ADAM_LR = 0.001
ADAM_B1 = 0.9
ADAM_B2 = 0.999
ADAM_EPS = 1e-08
ADAM_WD = 0.01
ADAM_STEP = 10

MESH = pl.DeviceIdType.MESH
_NT = (((1,), (1,)), ((), ()))
_NN = (((1,), (0,)), ((), ()))
_TN = (((0,), (0,)), ((), ()))
_MB = 2 ** 20


def _params(sem, vmem_mb=48):
    return pltpu.CompilerParams(dimension_semantics=sem, vmem_limit_bytes=vmem_mb * _MB)


def _dot(a, b, dims=_NN):
    return lax.dot_general(a.astype(bf16), b.astype(bf16), dims, preferred_element_type=f32)


def _dot32(a, b):
    return jnp.dot(a, b, preferred_element_type=f32, precision=lax.Precision.HIGHEST)


def _sigmoid(x):
    return 1.0 / (1.0 + jnp.exp(-x))


def _sum_all(a):
    return jnp.sum(jnp.sum(a, axis=1, keepdims=True), axis=0, keepdims=True)


def _iota(shape, dim):
    return lax.broadcasted_iota(jnp.int32, shape, dim)


def _matmul(a, b, *, nt=False, tm, tn, tk, out_dtypes, epilogue=None, extras=(), name):
    m, k_dim = a.shape
    n = b.shape[0] if nt else b.shape[1]
    nk = k_dim // tk
    ne, no = len(extras), len(out_dtypes)
    dims = _NT if nt else _NN

    def body(*refs):
        a_ref, b_ref = refs[0], refs[1]
        ex = refs[2:2 + ne]
        outs = refs[2 + ne:2 + ne + no]

        def finish(acc):
            vals = epilogue(acc, *[e[...] for e in ex]) if epilogue is not None else (acc,)
            for o, v in zip(outs, vals):
                o[...] = v.astype(o.dtype)

        part = lax.dot_general(a_ref[...], b_ref[...], dims, preferred_element_type=f32)
        if nk == 1:
            finish(part)
        else:
            acc_ref = refs[-1]
            kk = pl.program_id(2)

            @pl.when(kk == 0)
            def _():
                acc_ref[...] = part

            @pl.when(kk > 0)
            def _():
                acc_ref[...] += part

            @pl.when(kk == nk - 1)
            def _():
                finish(acc_ref[...])

    b_spec = pl.BlockSpec((tn, tk), lambda i, j, k: (j, k)) if nt else pl.BlockSpec((tk, tn), lambda i, j, k: (k, j))
    tile = pl.BlockSpec((tm, tn), lambda i, j, k: (i, j))
    outs = pl.pallas_call(
        body, name=name, grid=(m // tm, n // tn, nk),
        in_specs=[pl.BlockSpec((tm, tk), lambda i, j, k: (i, k)), b_spec] + [tile] * ne,
        out_specs=[tile] * no,
        out_shape=[jax.ShapeDtypeStruct((m, n), dt) for dt in out_dtypes],
        scratch_shapes=[pltpu.VMEM((tm, tn), f32)] if nk > 1 else [],
        compiler_params=_params(("parallel", "parallel", "arbitrary")),
    )(a, b, *extras)
    return outs if no > 1 else outs[0]


def _matmul_nt_sum(pairs, *, tm, tks, ride=None, name):
    m = pairs[0][0].shape[0]
    n = pairs[0][1].shape[0]
    nblk = [a.shape[1] // tk for (a, _), tk in zip(pairs, tks)]
    starts = [sum(nblk[:p]) for p in range(len(pairs))]
    nk = sum(nblk)
    npairs = len(pairs)
    ni = m // tm
    riding = ride is not None

    def body(*refs):
        rest = refs[2 * npairs:]
        if riding:
            ride_ref, o_ref, got_ref, acc_ref, send_sems, recv_sems = rest
        else:
            o_ref, acc_ref = rest
        i, kk = pl.program_id(0), pl.program_id(1)
        if riding:
            start, finish = _scatter_protocol(ride_ref, got_ref, send_sems, recv_sems)
            pl.when((i == 0) & (kk == 0))(start)

        @pl.when(kk == 0)
        def _():
            acc_ref[...] = jnp.zeros_like(acc_ref)

        for p in range(npairs):
            @pl.when((kk >= starts[p]) & (kk < starts[p] + nblk[p]))
            def _(p=p):
                acc_ref[...] += lax.dot_general(refs[2 * p][...], refs[2 * p + 1][...], _NT, preferred_element_type=f32)

        @pl.when(kk == nk - 1)
        def _():
            o_ref[...] = acc_ref[...]

        if riding:
            pl.when((i == ni - 1) & (kk == nk - 1))(finish)

    in_specs, args = [], []
    for p, (a, b) in enumerate(pairs):
        def kblock(k, s=starts[p], nb=nblk[p]):
            return jnp.clip(k - s, 0, nb - 1)
        in_specs.append(pl.BlockSpec((tm, tks[p]), lambda i, k, kb=kblock: (i, kb(k))))
        in_specs.append(pl.BlockSpec((n, tks[p]), lambda i, k, kb=kblock: (0, kb(k))))
        args += [a, b]
    tile = pl.BlockSpec((tm, n), lambda i, k: (i, 0))
    outs = pl.pallas_call(
        body, name=name, grid=(ni, nk), in_specs=in_specs + [_ANY] * riding, out_specs=[tile] + [_ANY] * riding,
        out_shape=[jax.ShapeDtypeStruct((m, n), f32)]
        + ([jax.ShapeDtypeStruct((N_CHIPS - 1,) + ride.shape[1:], ride.dtype)] if riding else []),
        scratch_shapes=[pltpu.VMEM((tm, n), f32)] + (list(_SCATTER_SCRATCH) if riding else []),
        compiler_params=_params(("arbitrary", "arbitrary"), vmem_mb=56),
    )(*args, *([ride] if riding else []))
    return outs if riding else outs[0]


def _matmul_tn(a, b, *, tka, tn, tt, name):
    t, ka = a.shape
    n = b.shape[1]

    def body(a_ref, b_ref, o_ref):
        part = lax.dot_general(a_ref[...], b_ref[...], _TN, preferred_element_type=f32)
        kk = pl.program_id(2)

        @pl.when(kk == 0)
        def _():
            o_ref[...] = part

        @pl.when(kk > 0)
        def _():
            o_ref[...] += part

    return pl.pallas_call(
        body, name=name, grid=(ka // tka, n // tn, t // tt),
        in_specs=[pl.BlockSpec((tt, tka), lambda i, j, k: (k, i)), pl.BlockSpec((tt, tn), lambda i, j, k: (k, j))],
        out_specs=pl.BlockSpec((tka, tn), lambda i, j, k: (i, j)),
        out_shape=jax.ShapeDtypeStruct((ka, n), f32),
        compiler_params=_params(("parallel", "parallel", "arbitrary")),
    )(a, b)


def _row_tile(t):
    return min(t, 512)


def _rms_fwd(x, g, *, name):
    t, d = x.shape
    tr = _row_tile(t)

    def body(x_ref, g_ref, h_ref):
        xv = x_ref[...]
        r = lax.rsqrt(jnp.mean(xv * xv, axis=1, keepdims=True) + NORM_EPS)
        h_ref[...] = (xv * r * g_ref[...]).astype(bf16)

    return pl.pallas_call(
        body, name=name, grid=(t // tr,),
        in_specs=[pl.BlockSpec((tr, d), lambda i: (i, 0)), pl.BlockSpec((1, d), lambda i: (0, 0))],
        out_specs=pl.BlockSpec((tr, d), lambda i: (i, 0)),
        out_shape=jax.ShapeDtypeStruct((t, d), bf16),
        compiler_params=_params(("parallel",)),
    )(x, g)


def _rms_bwd(xin, g, dh, dres, *, want_bf16, name):
    t, d = xin.shape
    tr = _row_tile(t)

    def body(x_ref, g_ref, dh_ref, dres_ref, dx_ref, *rest):
        dg_ref = rest[-1]
        xv = x_ref[...]
        r = lax.rsqrt(jnp.mean(xv * xv, axis=1, keepdims=True) + NORM_EPS)
        xn = xv * r
        dhv = dh_ref[...]
        dxn = dhv * g_ref[...]
        dx = dres_ref[...] + r * (dxn - xn * jnp.mean(dxn * xn, axis=1, keepdims=True))
        dx_ref[...] = dx
        if want_bf16:
            rest[0][...] = dx.astype(bf16)
        part = jnp.sum(dhv * xn, axis=0, keepdims=True)

        @pl.when(pl.program_id(0) == 0)
        def _():
            dg_ref[...] = part

        @pl.when(pl.program_id(0) > 0)
        def _():
            dg_ref[...] += part

    row = pl.BlockSpec((tr, d), lambda i: (i, 0))
    vec = pl.BlockSpec((1, d), lambda i: (0, 0))
    out_shape = [jax.ShapeDtypeStruct((t, d), f32)] + ([jax.ShapeDtypeStruct((t, d), bf16)] if want_bf16 else []) \
        + [jax.ShapeDtypeStruct((1, d), f32)]
    return pl.pallas_call(
        body, name=name, grid=(t // tr,),
        in_specs=[row, vec, row, row],
        out_specs=[row] + ([row] if want_bf16 else []) + [vec],
        out_shape=out_shape,
        compiler_params=_params(("arbitrary",)),
    )(xin, g, dh, dres)


def _loss_head(x2, tgt, g, *, name):
    t, d = x2.shape
    tr = _row_tile(t)

    def body(x_ref, t_ref, g_ref, dx_ref, dxb_ref, dg_ref, loss_ref):
        xv = x_ref[...]
        gv = g_ref[...]
        r = lax.rsqrt(jnp.mean(xv * xv, axis=1, keepdims=True) + NORM_EPS)
        xn = xv * r
        e = xn * gv - t_ref[...]
        lpart = jnp.zeros((1, LANES), f32) + 0.5 * _sum_all(jnp.mean(e * e, axis=1, keepdims=True))
        dy = e * (1.0 / d)
        dxn = dy * gv
        dx = r * (dxn - xn * jnp.mean(dxn * xn, axis=1, keepdims=True))
        dx_ref[...] = dx
        dxb_ref[...] = dx.astype(bf16)
        gpart = jnp.sum(dy * xn, axis=0, keepdims=True)

        @pl.when(pl.program_id(0) == 0)
        def _():
            dg_ref[...] = gpart
            loss_ref[...] = lpart

        @pl.when(pl.program_id(0) > 0)
        def _():
            dg_ref[...] += gpart
            loss_ref[...] += lpart

    row = pl.BlockSpec((tr, d), lambda i: (i, 0))
    vec = pl.BlockSpec((1, d), lambda i: (0, 0))
    return pl.pallas_call(
        body, name=name, grid=(t // tr,),
        in_specs=[row, row, vec],
        out_specs=[row, row, vec, pl.BlockSpec((1, LANES), lambda i: (0, 0))],
        out_shape=[jax.ShapeDtypeStruct((t, d), f32), jax.ShapeDtypeStruct((t, d), bf16),
                   jax.ShapeDtypeStruct((1, d), f32), jax.ShapeDtypeStruct((1, LANES), f32)],
        compiler_params=_params(("arbitrary",)),
    )(x2, tgt, g)


def _merge_fwd(pa, pb, gl, bg, *, name):
    t, d = pa.shape
    tr = _row_tile(t)

    def body(pa_ref, pb_ref, gla_ref, glb_ref, bga_ref, bgb_ref, o_ref):
        ga = _sigmoid(gla_ref[...] + bga_ref[...])
        gb = _sigmoid(glb_ref[...] + bgb_ref[...])
        o_ref[...] = (ga * pa_ref[...] + gb * pb_ref[...]).astype(bf16)

    row = pl.BlockSpec((tr, d), lambda i: (i, 0))
    return pl.pallas_call(
        body, name=name, grid=(t // tr,),
        in_specs=[row, row, row, pl.BlockSpec((tr, d), lambda i: (i, 1)),
                  pl.BlockSpec((1, d), lambda i: (0, 0)), pl.BlockSpec((1, d), lambda i: (0, 1))],
        out_specs=row,
        out_shape=jax.ShapeDtypeStruct((t, d), bf16),
        compiler_params=_params(("parallel",)),
    )(pa, pb, gl, gl, bg, bg)


def _merge_bwd(dm, pa, pb, gl, bg, *, name):
    t, d = pa.shape
    tr = _row_tile(t)

    def body(dm_ref, pa_ref, pb_ref, gla_ref, glb_ref, bga_ref, bgb_ref, dpa_ref, dpb_ref, dgl_ref, dbg_ref):
        dmv = dm_ref[...]
        ga = _sigmoid(gla_ref[...] + bga_ref[...])
        gb = _sigmoid(glb_ref[...] + bgb_ref[...])
        dpa_ref[...] = (dmv * ga).astype(bf16)
        dpb_ref[...] = (dmv * gb).astype(bf16)
        dla = dmv * pa_ref[...] * ga * (1.0 - ga)
        dlb = dmv * pb_ref[...] * gb * (1.0 - gb)
        dgl_ref[:, :d] = dla.astype(bf16)
        dgl_ref[:, d:] = dlb.astype(bf16)
        sa = jnp.sum(dla, axis=0, keepdims=True)
        sb = jnp.sum(dlb, axis=0, keepdims=True)

        @pl.when(pl.program_id(0) == 0)
        def _():
            dbg_ref[:, :d] = sa
            dbg_ref[:, d:] = sb

        @pl.when(pl.program_id(0) > 0)
        def _():
            dbg_ref[:, :d] += sa
            dbg_ref[:, d:] += sb

    row = pl.BlockSpec((tr, d), lambda i: (i, 0))
    return pl.pallas_call(
        body, name=name, grid=(t // tr,),
        in_specs=[row, row, row, row, pl.BlockSpec((tr, d), lambda i: (i, 1)),
                  pl.BlockSpec((1, d), lambda i: (0, 0)), pl.BlockSpec((1, d), lambda i: (0, 1))],
        out_specs=[row, row, pl.BlockSpec((tr, 2 * d), lambda i: (i, 0)), pl.BlockSpec((1, 2 * d), lambda i: (0, 0))],
        out_shape=[jax.ShapeDtypeStruct((t, d), bf16), jax.ShapeDtypeStruct((t, d), bf16),
                   jax.ShapeDtypeStruct((t, 2 * d), bf16), jax.ShapeDtypeStruct((1, 2 * d), f32)],
        compiler_params=_params(("arbitrary",)),
    )(dm, pa, pb, gl, gl, bg, bg)


_INV_SQRT2 = 1.0 / math.sqrt(2.0)
_INV_SQRT2PI = 1.0 / math.sqrt(2.0 * math.pi)


def _gmlp_common(uv, vg, vb, with_grad=False):
    cdf = 0.5 * (1.0 + lax.erf(uv * _INV_SQRT2))
    zz = uv * cdf
    u, vhat, rstd, vn = _gmlp_norm(zz, vg, vb)
    if not with_grad:
        return u, vhat, rstd, vn
    return u, vhat, rstd, vn, cdf + uv * jnp.exp(-0.5 * uv * uv) * _INV_SQRT2PI


def _gmlp_norm(zz, vg, vb):
    u = zz[:, :GMLP_WIDTH]
    v = zz[:, GMLP_WIDTH:]
    mu = jnp.mean(v, axis=1, keepdims=True)
    vc = v - mu
    rstd = lax.rsqrt(jnp.mean(vc * vc, axis=1, keepdims=True) + NORM_EPS)
    vhat = vc * rstd
    vn = vhat * vg + vb
    return u, vhat, rstd, vn


def _gmlp_fwd(uv, vg, vb, wsp, bsp_t, *, name):
    t = uv.shape[0]
    nc = t // CHUNK

    def body(uv_ref, vg_ref, vb_ref, w_ref, b_ref, y_ref):
        u, _, _, vn = _gmlp_common(uv_ref[...], vg_ref[...], vb_ref[...])
        tril = _iota((CHUNK, CHUNK), 0) >= _iota((CHUNK, CHUNK), 1)
        bt = b_ref[...]
        for g in range(GMLP_GROUPS):
            sl = slice(g * CHUNK, (g + 1) * CHUNK)
            w = jnp.where(tril, w_ref[g], 0.0)
            s = _dot(w, vn[:, sl]) + bt[:, g:g + 1]
            y_ref[:, sl] = (u[:, sl] * s).astype(bf16)

    return pl.pallas_call(
        body, name=name, grid=(nc,),
        in_specs=[pl.BlockSpec((CHUNK, 2 * GMLP_WIDTH), lambda c: (c, 0)),
                  pl.BlockSpec((1, GMLP_WIDTH), lambda c: (0, 0)), pl.BlockSpec((1, GMLP_WIDTH), lambda c: (0, 0)),
                  pl.BlockSpec((GMLP_GROUPS, CHUNK, CHUNK), lambda c: (0, 0, 0)),
                  pl.BlockSpec((CHUNK, LANES), lambda c: (0, 0))],
        out_specs=pl.BlockSpec((CHUNK, GMLP_WIDTH), lambda c: (c, 0)),
        out_shape=jax.ShapeDtypeStruct((t, GMLP_WIDTH), bf16),
        compiler_params=_params(("parallel",)),
    )(uv, vg, vb, wsp, bsp_t)


def _gmlp_bwd(uv, dya, vg, vb, wsp, bsp_t, *, ride=None, name):
    t = uv.shape[0]
    nc = t // CHUNK
    riding = ride is not None

    def body(*refs):
        uv_ref, dy_ref, vg_ref, vb_ref, w_ref, b_ref = refs[:6]
        duv_ref, dw_ref, db_ref, dvg_ref, dvb_ref = refs[6 + riding:11 + riding]
        first = pl.program_id(0) == 0
        if riding:
            start, finish = _swap_protocol(refs[6], refs[12], refs[13], refs[14])
            pl.when(first)(start)

        @pl.when(first)
        def _():
            dw_ref[...] = jnp.zeros_like(dw_ref)
            db_ref[...] = jnp.zeros_like(db_ref)
            dvg_ref[...] = jnp.zeros_like(dvg_ref)
            dvb_ref[...] = jnp.zeros_like(dvb_ref)

        uvv = uv_ref[...]
        vgv = vg_ref[...]
        u, vhat, rstd, vn, gelu_grad = _gmlp_common(uvv, vgv, vb_ref[...], with_grad=True)
        dy = dy_ref[...]
        tril = _iota((CHUNK, CHUNK), 0) >= _iota((CHUNK, CHUNK), 1)
        lane = _iota((CHUNK, LANES), 1)
        bt = b_ref[...]
        ds_all = dy * u
        dbacc = jnp.zeros((CHUNK, LANES), f32)
        dvh_parts = []
        for g in range(GMLP_GROUPS):
            sl = slice(g * CHUNK, (g + 1) * CHUNK)
            w = jnp.where(tril, w_ref[g], 0.0)
            vng = vn[:, sl]
            s = _dot(w, vng) + bt[:, g:g + 1]
            ds = ds_all[:, sl]
            duv_ref[:, sl] = (dy[:, sl] * s * gelu_grad[:, sl]).astype(bf16)
            dw_ref[g] += jnp.where(tril, _dot(ds, vng, _NT), 0.0)
            dbacc = dbacc + jnp.where(lane == g, jnp.sum(ds, axis=1, keepdims=True), 0.0)
            dvn = _dot(w, ds, _TN)
            vh = vhat[:, sl]
            dvg_ref[:, sl] += jnp.sum(dvn * vh, axis=0, keepdims=True)
            dvb_ref[:, sl] += jnp.sum(dvn, axis=0, keepdims=True)
            dvh_parts.append(dvn * vgv[:, sl])
        db_ref[...] += dbacc
        dvhat = jnp.concatenate(dvh_parts, axis=1)
        m1 = jnp.mean(dvhat, axis=1, keepdims=True)
        m2 = jnp.mean(dvhat * vhat, axis=1, keepdims=True)
        dv = rstd * (dvhat - m1 - vhat * m2)
        duv_ref[:, GMLP_WIDTH:] = (dv * gelu_grad[:, GMLP_WIDTH:]).astype(bf16)
        if riding:
            pl.when(pl.program_id(0) == nc - 1)(finish)

    vec = pl.BlockSpec((1, GMLP_WIDTH), lambda c: (0, 0))
    return pl.pallas_call(
        body, name=name, grid=(nc,),
        in_specs=[pl.BlockSpec((CHUNK, 2 * GMLP_WIDTH), lambda c: (c, 0)),
                  pl.BlockSpec((CHUNK, GMLP_WIDTH), lambda c: (c, 0)), vec, vec,
                  pl.BlockSpec((GMLP_GROUPS, CHUNK, CHUNK), lambda c: (0, 0, 0)),
                  pl.BlockSpec((CHUNK, LANES), lambda c: (0, 0))] + [_ANY] * riding,
        out_specs=[pl.BlockSpec((CHUNK, 2 * GMLP_WIDTH), lambda c: (c, 0)),
                   pl.BlockSpec((GMLP_GROUPS, CHUNK, CHUNK), lambda c: (0, 0, 0)),
                   pl.BlockSpec((CHUNK, LANES), lambda c: (0, 0)), vec, vec] + [_ANY] * riding,
        out_shape=[jax.ShapeDtypeStruct((t, 2 * GMLP_WIDTH), bf16),
                   jax.ShapeDtypeStruct((GMLP_GROUPS, CHUNK, CHUNK), f32),
                   jax.ShapeDtypeStruct((CHUNK, LANES), f32),
                   jax.ShapeDtypeStruct((1, GMLP_WIDTH), f32), jax.ShapeDtypeStruct((1, GMLP_WIDTH), f32)]
        + ([jax.ShapeDtypeStruct(ride.shape[:1] + ride.shape[2:], ride.dtype)] if riding else []),
        scratch_shapes=list(_SWAP_SCRATCH) if riding else [],
        compiler_params=_params(("arbitrary",)),
    )(uv, dya, vg, vb, wsp, bsp_t, *([ride] if riding else []))


_CONV_COLS = 512
_XS0, _B0, _C0 = 0, D_INNER, D_INNER + N_GROUPS * D_STATE


_TAIL = 8


def _conv_silu(cur_ref, tail_ref, w_ref, b_ref, has_prev, xc_ref, cv_ref):
    row = _iota((_TAIL, _CONV_COLS), 0)
    for j in range(CONV_DIM // _CONV_COLS):
        sl = slice(j * _CONV_COLS, (j + 1) * _CONV_COLS)
        cur = cur_ref[:, sl]
        tail = jnp.where(has_prev, tail_ref[:, sl], 0.0)
        acc = cur * w_ref[CONV_W - 1:CONV_W, sl] + b_ref[:, sl]
        for s in range(1, CONV_W):
            rolled = pltpu.roll(cur, s, 0)
            top = jnp.where(row >= s, rolled[:_TAIL], pltpu.roll(tail, s, 0))
            sh = jnp.concatenate([top, rolled[_TAIL:]], axis=0)
            acc = acc + sh * w_ref[CONV_W - 1 - s:CONV_W - s, sl]
        cv_ref[:, sl] = acc
        xc_ref[:, sl] = acc * _sigmoid(acc)


def _col_bcast(mat, h):
    return jnp.broadcast_to(mat[:, h:h + 1], (CHUNK, LANES))


def _head_expand(cols):
    lo = _iota((CHUNK, LANES), 1) < HEAD_DIM
    return jnp.concatenate([jnp.where(lo, cols[2 * j], cols[2 * j + 1]) for j in range(N_HEADS // 2)], axis=1)


def _ssd_chunk_scalars(dtr, dtb, alog):
    xdt_pre = dtr + dtb
    dtv = jnp.maximum(xdt_pre, 0.0) + jnp.log(1.0 + jnp.exp(-jnp.abs(xdt_pre)))
    a = -jnp.exp(alog)
    ltri = (_iota((CHUNK, CHUNK), 0) >= _iota((CHUNK, CHUNK), 1)).astype(f32)
    cs = _dot32(ltri, dtv * a)
    csb = [_col_bcast(cs, h) for h in range(N_HEADS)]
    cs_x = _head_expand(csb)
    dt_x = _head_expand([_col_bcast(dtv, h) for h in range(N_HEADS)])
    cl_x = cs_x[CHUNK - 1:CHUNK, :]
    return dict(xdt_pre=xdt_pre, dtv=dtv, a=a, cs=cs, cs_t=cs.T, csb=csb, dt_x=dt_x, e_x=jnp.exp(cs_x),
                dec_x=jnp.exp(cl_x - cs_x), dk_x=jnp.exp(cl_x))


def _head_masks():
    lane = _iota((CHUNK, GROUP_W), 1)
    return [(lane >= r * HEAD_DIM) & (lane < (r + 1) * HEAD_DIM) for r in range(HEADS_PER_GROUP)]


def _stack_heads(a, masks):
    return jnp.concatenate([jnp.where(m, a, 0.0) for m in masks], axis=0).astype(bf16)


def _seg_sum(a, seg):
    hi = a.astype(jnp.bfloat16)
    lo = (a - hi.astype(f32)).astype(jnp.bfloat16)
    return (lax.dot_general(hi, seg, _NN, preferred_element_type=f32)
            + lax.dot_general(lo, seg, _NN, preferred_element_type=f32))


def _head_seg_matrix():
    return (_iota((D_INNER, LANES), 0) // HEAD_DIM == _iota((D_INNER, LANES), 1)).astype(jnp.bfloat16)


def _ssd_fwd(xbc, z, dtr, cw, cb, dtb, alog, dsk_x, gs, *, ride=None, name):
    t = xbc.shape[0]
    nc = t // CHUNK
    tiles = CHUNK // _TAIL

    def body(*refs):
        cur_ref, tail_ref, z_ref, dtr_ref, cw_ref, cb_ref, dtb_ref, alog_ref, dsk_ref, gs_ref = refs[:10]
        if ride is None:
            yb_ref, hp_ref, cv_ref, state_ref, xc_ref = refs[10:]
        else:
            ride_ref, yb_ref, hp_ref, cv_ref, got_ref, state_ref, xc_ref, send_sems, recv_sems = refs[10:]
        c = pl.program_id(0)
        if ride is not None:
            start, relay, finish = _gather_protocol(ride_ref, got_ref, send_sems, recv_sems)
            pl.when(c == 0)(start)
            pl.when(c == nc // 2)(relay)

        @pl.when(c == 0)
        def _():
            state_ref[...] = jnp.zeros_like(state_ref)

        _conv_silu(cur_ref, tail_ref, cw_ref, cb_ref, c > 0, xc_ref, cv_ref)
        sc = _ssd_chunk_scalars(dtr_ref[...], dtb_ref[...], alog_ref[...])
        tril = _iota((CHUNK, CHUNK), 0) >= _iota((CHUNK, CHUNK), 1)
        masks = _head_masks()
        hp_ref[0] = state_ref[...]
        for g in range(N_GROUPS):
            gsl = slice(g * GROUP_W, (g + 1) * GROUP_W)
            xs_g = xc_ref[:, gsl]
            bg = xc_ref[:, _B0 + g * D_STATE:_B0 + (g + 1) * D_STATE]
            cg = xc_ref[:, _C0 + g * D_STATE:_C0 + (g + 1) * D_STATE]
            xdt_g = xs_g * sc["dt_x"][:, gsl]
            cbm = _dot(cg, bg, _NT)
            mw = jnp.concatenate(
                [cbm * jnp.exp(jnp.where(tril, sc["csb"][h] - sc["cs_t"][h:h + 1, :], -1e30))
                 for h in range(g * HEADS_PER_GROUP, (g + 1) * HEADS_PER_GROUP)], axis=1)
            ht_g = state_ref[:, gsl]
            y_g = _dot(mw, _stack_heads(xdt_g, masks)) + sc["e_x"][:, gsl] * _dot(cg, ht_g) + dsk_ref[:, gsl] * xs_g
            state_ref[:, gsl] = ht_g * sc["dk_x"][:, gsl] + _dot(bg, xdt_g * sc["dec_x"][:, gsl], _TN)
            zg = z_ref[:, gsl]
            yg = y_g * zg * _sigmoid(zg)
            rs = lax.rsqrt(jnp.mean(yg * yg, axis=1, keepdims=True) + NORM_EPS)
            yb_ref[:, gsl] = (yg * rs * gs_ref[:, gsl]).astype(bf16)
        if ride is not None:
            pl.when(c == nc - 1)(finish)

    def chunk(w):
        return pl.BlockSpec((CHUNK, w), lambda c: (c, 0))

    def const(shape):
        return pl.BlockSpec(shape, lambda c: (0,) * len(shape))

    riding = ride is not None
    return pl.pallas_call(
        body, name=name, grid=(nc,),
        in_specs=[chunk(CONV_DIM), pl.BlockSpec((_TAIL, CONV_DIM), lambda c: (jnp.maximum(c * tiles - 1, 0), 0)),
                  chunk(D_INNER), chunk(LANES), const((CONV_W, CONV_DIM)), const((1, CONV_DIM)),
                  const((1, LANES)), const((1, LANES)), const((1, D_INNER)), const((1, D_INNER))] + [_ANY] * riding,
        out_specs=[chunk(D_INNER), pl.BlockSpec((1, D_STATE, D_INNER), lambda c: (c, 0, 0)), chunk(CONV_DIM)]
        + [_ANY] * riding,
        out_shape=[jax.ShapeDtypeStruct((t, D_INNER), bf16), jax.ShapeDtypeStruct((nc, D_STATE, D_INNER), f32),
                   jax.ShapeDtypeStruct((t, CONV_DIM), f32)]
        + ([jax.ShapeDtypeStruct((N_CHIPS,) + ride.shape, ride.dtype)] if riding else []),
        scratch_shapes=[pltpu.VMEM((D_STATE, D_INNER), f32), pltpu.VMEM((CHUNK, CONV_DIM), f32)]
        + (list(_GATHER_SCRATCH) if riding else []),
        compiler_params=_params(("arbitrary",)),
    )(xbc, xbc, z, dtr, cw, cb, dtb, alog, dsk_x, gs, *([ride] if riding else []))


def _ssd_bwd(xbc, cv, z, dtr, hprev, dyb, cw, dtb, alog, dsk_x, gs, seg, *, ride=None, name):
    t = xbc.shape[0]
    nc = t // CHUNK

    def body(*refs):
        (cur_ref, cv_ref, z_ref, dtr_ref, hp_ref, dyb_ref, cw_ref, dtb_ref, alog_ref, dsk_ref, gs_ref,
         seg_ref) = refs[:12]
        rest = refs[12:]
        if ride is not None:
            ride_ref, got_ref, send_sems, recv_sems = rest[0], rest[10], rest[-2], rest[-1]
            rest = rest[1:10] + rest[11:-2]
        (dz_ref, dxbc_ref, ddt_ref, dcw_ref, dcb_ref, ddtb_ref, dalog_ref, ddsk_ref, dgs_ref,
         dh_ref, dcnext_ref, xc_ref, dxc_ref, x13_ref, x2_ref, rows_ref) = rest
        i = pl.program_id(0)
        cc = nc - 1 - i
        if ride is not None:
            start, finish = _scatter_protocol(ride_ref, got_ref, send_sems, recv_sems)
            pl.when(i == 0)(start)

        @pl.when(i == 0)
        def _():
            for ref in (dh_ref, dcnext_ref, dcw_ref, dcb_ref, ddtb_ref, dalog_ref, ddsk_ref, dgs_ref, rows_ref):
                ref[...] = jnp.zeros_like(ref)

        for j in range(CONV_DIM // _CONV_COLS):
            sl = slice(j * _CONV_COLS, (j + 1) * _CONV_COLS)
            cvv = cv_ref[:, sl]
            xc_ref[:, sl] = cvv * _sigmoid(cvv)
        sc = _ssd_chunk_scalars(dtr_ref[...], dtb_ref[...], alog_ref[...])
        tril = _iota((CHUNK, CHUNK), 0) >= _iota((CHUNK, CHUNK), 1)
        triu = _iota((CHUNK, CHUNK), 0) <= _iota((CHUNK, CHUNK), 1)
        masks = _head_masks()
        rowh = _iota((N_HEADS, CHUNK), 0)
        dcs_t = jnp.zeros((N_HEADS, CHUNK), f32)
        for g in range(N_GROUPS):
            gsl = slice(g * GROUP_W, (g + 1) * GROUP_W)
            xs_g = xc_ref[:, gsl]
            bg = xc_ref[:, _B0 + g * D_STATE:_B0 + (g + 1) * D_STATE]
            cg = xc_ref[:, _C0 + g * D_STATE:_C0 + (g + 1) * D_STATE]
            dt_g, e_g, dec_g, dk_g = sc["dt_x"][:, gsl], sc["e_x"][:, gsl], sc["dec_x"][:, gsl], sc["dk_x"][:, gsl]
            dsk_g = dsk_ref[:, gsl]
            xdt_g = xs_g * dt_g
            xdt_stack = _stack_heads(xdt_g, masks)
            cbm = _dot(cg, bg, _NT)
            cbt = _dot(bg, cg, _NT)
            heads = range(g * HEADS_PER_GROUP, (g + 1) * HEADS_PER_GROUP)
            lmats = [jnp.exp(jnp.where(tril, sc["csb"][h] - sc["cs_t"][h:h + 1, :], -1e30)) for h in heads]
            mw = jnp.concatenate([cbm * lm for lm in lmats], axis=1)
            mtw = jnp.concatenate(
                [cbt * jnp.exp(jnp.where(triu, sc["cs_t"][h:h + 1, :] - sc["csb"][h], -1e30)) for h in heads], axis=1)
            ht_g = hp_ref[0, :, gsl]
            dhn_g = dh_ref[:, gsl]
            yoff = e_g * _dot(cg, ht_g)
            y_g = _dot(mw, xdt_stack) + yoff + dsk_g * xs_g
            zg = z_ref[:, gsl]
            sz = _sigmoid(zg)
            silu = zg * sz
            yg = y_g * silu
            rs = lax.rsqrt(jnp.mean(yg * yg, axis=1, keepdims=True) + NORM_EPS)
            yn = yg * rs
            dyb = dyb_ref[:, gsl]
            dgs_ref[:, gsl] += jnp.sum(dyb * yn, axis=0, keepdims=True)
            dyn = dyb * gs_ref[:, gsl]
            dyg = rs * (dyn - yn * jnp.mean(dyn * yn, axis=1, keepdims=True))
            dy_g = dyg * silu
            dz_ref[:, gsl] = (dyg * y_g * (sz * (1.0 + zg * (1.0 - sz)))).astype(bf16)
            dy_stack = _stack_heads(dy_g, masks)
            dm_w = _dot(dy_g, xdt_stack, _NT)
            dmt_w = _dot(xdt_g, dy_stack, _NT)
            dxdt = _dot(mtw, dy_stack)
            dcb_acc = jnp.zeros((CHUNK, CHUNK), f32)
            for r, h in enumerate(heads):
                hs = slice(r * CHUNK, (r + 1) * CHUNK)
                dml = dm_w[:, hs] * lmats[r]
                dcb_acc = dcb_acc + dml
                col = jnp.sum(dml * cbm, axis=0, keepdims=True)
                row = jnp.sum(dmt_w[:, hs] * mtw[:, hs], axis=0, keepdims=True)
                dcs_t = dcs_t + jnp.where(rowh == h, row - col, 0.0)
            w = _dot(bg, dhn_g)
            dxdt = dxdt + dec_g * w
            decx3 = dec_g * (xdt_g * w)
            dg_g = e_g * dy_g
            d_c = _dot(dg_g, ht_g, _NT) + _dot(dcb_acc, bg)
            d_b = _dot(dcb_acc, cg, _TN) + _dot(xdt_g * dec_g, dhn_g, _NT)
            dh_ref[:, gsl] = dhn_g * dk_g + _dot(cg, dg_g, _TN)
            dxc_ref[:, gsl] = dsk_g * dy_g + dxdt * dt_g
            dxc_ref[:, _B0 + g * D_STATE:_B0 + (g + 1) * D_STATE] = d_b
            dxc_ref[:, _C0 + g * D_STATE:_C0 + (g + 1) * D_STATE] = d_c
            x13_ref[:, gsl] = dy_g * yoff - decx3
            x2_ref[:, gsl] = dxdt * xs_g
            rows_ref[0:1, gsl] = jnp.sum(dhn_g * ht_g, axis=0, keepdims=True)
            rows_ref[1:2, gsl] = jnp.sum(decx3, axis=0, keepdims=True)
            rows_ref[2:3, gsl] = jnp.sum(dy_g * xs_g, axis=0, keepdims=True)
        segm = seg_ref[...]
        r13 = _seg_sum(x13_ref[...], segm)
        r2 = _seg_sum(x2_ref[...], segm)
        small = _seg_sum(rows_ref[...], segm)
        lane = _iota((CHUNK, LANES), 1)
        rowi = _iota((CHUNK, LANES), 0)
        dcl_row = small[0:1, :] * jnp.exp(sc["cs"][CHUNK - 1:CHUNK, :]) + small[1:2, :]
        dcs = r13 + jnp.where(rowi == CHUNK - 1, dcl_row, 0.0)
        dcs_t_all = dcs.T + jnp.concatenate([dcs_t, jnp.zeros((LANES - N_HEADS, CHUNK), f32)], axis=0)
        dda = _dot32(dcs_t_all, tril.astype(f32)).T
        a = sc["a"]
        ddt_total = r2 + dda * a
        dalog_ref[...] += jnp.sum(dda * sc["dtv"], axis=0, keepdims=True) * a
        ddtr = jnp.where(lane < N_HEADS, ddt_total * _sigmoid(sc["xdt_pre"]), 0.0)
        ddtb_ref[...] += jnp.sum(ddtr, axis=0, keepdims=True)
        ddt_ref[...] = ddtr.astype(bf16)
        ddsk_ref[...] += small[2:3, :]
        row8 = _iota((_TAIL, _CONV_COLS), 0)
        for j in range(CONV_DIM // _CONV_COLS):
            sl = slice(j * _CONV_COLS, (j + 1) * _CONV_COLS)
            cvv = cv_ref[:, sl]
            sg = _sigmoid(cvv)
            dconv = dxc_ref[:, sl] * (sg * (1.0 + cvv * (1.0 - sg)))
            nxt = dcnext_ref[:, sl]
            cur = cur_ref[:, sl]
            dxin = dconv * cw_ref[CONV_W - 1:CONV_W, sl]
            dcw_ref[CONV_W - 1:CONV_W, sl] += jnp.sum(dconv * cur, axis=0, keepdims=True)
            for s in range(1, CONV_W):
                rolled = pltpu.roll(dconv, CHUNK - s, 0)
                bot = jnp.where(row8 < _TAIL - s, rolled[CHUNK - _TAIL:], pltpu.roll(nxt, _TAIL - s, 0))
                up = jnp.concatenate([rolled[:CHUNK - _TAIL], bot], axis=0)
                dxin = dxin + up * cw_ref[CONV_W - 1 - s:CONV_W - s, sl]
                dcw_ref[CONV_W - 1 - s:CONV_W - s, sl] += jnp.sum(up * cur, axis=0, keepdims=True)
            dcb_ref[:, sl] += jnp.sum(dconv, axis=0, keepdims=True)
            dxbc_ref[:, sl] = dxin.astype(bf16)
            dcnext_ref[:, sl] = dconv[:_TAIL]
        if ride is not None:
            pl.when(i == nc - 1)(finish)

    def chunk(w):
        return pl.BlockSpec((CHUNK, w), lambda i: (nc - 1 - i, 0))

    def const(shape):
        return pl.BlockSpec(shape, lambda i: (0,) * len(shape))

    riding = ride is not None
    return pl.pallas_call(
        body, name=name, grid=(nc,),
        in_specs=[chunk(CONV_DIM), chunk(CONV_DIM),
                  chunk(D_INNER), chunk(LANES), pl.BlockSpec((1, D_STATE, D_INNER), lambda i: (nc - 1 - i, 0, 0)),
                  chunk(D_INNER), const((CONV_W, CONV_DIM)),
                  const((1, LANES)), const((1, LANES)), const((1, D_INNER)), const((1, D_INNER)),
                  const((D_INNER, LANES))] + [_ANY] * riding,
        out_specs=[chunk(D_INNER), chunk(CONV_DIM), chunk(LANES), const((CONV_W, CONV_DIM)), const((1, CONV_DIM)),
                   const((1, LANES)), const((1, LANES)), const((1, LANES)), const((1, D_INNER))] + [_ANY] * riding,
        out_shape=[jax.ShapeDtypeStruct((t, D_INNER), bf16), jax.ShapeDtypeStruct((t, CONV_DIM), bf16),
                   jax.ShapeDtypeStruct((t, LANES), bf16), jax.ShapeDtypeStruct((CONV_W, CONV_DIM), f32),
                   jax.ShapeDtypeStruct((1, CONV_DIM), f32), jax.ShapeDtypeStruct((1, LANES), f32),
                   jax.ShapeDtypeStruct((1, LANES), f32), jax.ShapeDtypeStruct((1, LANES), f32),
                   jax.ShapeDtypeStruct((1, D_INNER), f32)]
        + ([jax.ShapeDtypeStruct((N_CHIPS - 1,) + ride.shape[1:], ride.dtype)] if riding else []),
        scratch_shapes=[pltpu.VMEM((D_STATE, D_INNER), f32), pltpu.VMEM((_TAIL, CONV_DIM), f32),
                        pltpu.VMEM((CHUNK, CONV_DIM), f32), pltpu.VMEM((CHUNK, CONV_DIM), f32),
                        pltpu.VMEM((CHUNK, D_INNER), f32), pltpu.VMEM((CHUNK, D_INNER), f32),
                        pltpu.VMEM((_TAIL, D_INNER), f32)]
        + (list(_SCATTER_SCRATCH) if riding else []),
        compiler_params=_params(("arbitrary",)),
    )(xbc, cv, z, dtr, hprev, dyb, cw, dtb, alog, dsk_x, gs, seg, *([ride] if riding else []))


def _adamw(w, g, m, v, *, name):
    r, c = w.shape
    tr = r
    while tr * c * 4 > _MB and tr % 16 == 0:
        tr //= 2

    def body(w_ref, g_ref, m_ref, v_ref, d_ref, m2_ref, v2_ref):
        gv = g_ref[...]
        m2 = ADAM_B1 * m_ref[...] + (1.0 - ADAM_B1) * gv
        v2 = ADAM_B2 * v_ref[...] + (1.0 - ADAM_B2) * (gv * gv)
        m_hat = m2 / (1.0 - ADAM_B1 ** ADAM_STEP)
        v_hat = v2 / (1.0 - ADAM_B2 ** ADAM_STEP)
        d_ref[...] = -ADAM_LR * (m_hat / (jnp.sqrt(v_hat) + ADAM_EPS) + ADAM_WD * w_ref[...])
        m2_ref[...] = m2
        v2_ref[...] = v2

    blk = pl.BlockSpec((tr, c), lambda i: (i, 0))
    return pl.pallas_call(
        body, name=name, grid=(r // tr,),
        in_specs=[blk] * 4, out_specs=[blk] * 3,
        out_shape=[jax.ShapeDtypeStruct((r, c), f32)] * 3,
        compiler_params=_params(("parallel",)),
    )(w, g, m, v)


def _row_block(rows, cols):
    cap = max(16, _MB // (4 * cols))
    return max(tr for tr in range(16, min(cap, rows) + 1, 16) if rows % tr == 0)


def _cast_bf16(a, *, name):
    r, c = a.shape
    tr = _row_block(r, c)

    def body(a_ref, o_ref):
        o_ref[...] = a_ref[...].astype(bf16)

    blk = pl.BlockSpec((tr, c), lambda i: (i, 0))
    return pl.pallas_call(
        body, name=name, grid=(r // tr,), in_specs=[blk], out_specs=blk,
        out_shape=jax.ShapeDtypeStruct((r, c), bf16), compiler_params=_params(("parallel",)),
    )(a)


_ANY = pl.BlockSpec(memory_space=pl.ANY)


def _place():
    x, y, c = lax.axis_index("x"), lax.axis_index("y"), lax.axis_index("c")
    other_chips = [(1 - x, y), (x, 1 - y), (1 - x, 1 - y)]
    return x, y, c, other_chips


def _gather_protocol(in_ref, out_ref, send_sems, recv_sems):
    x, y, c, chips = _place()
    me = 2 * x + y
    sibling = (x, y, 1 - c)

    def cp(k, chip, half, to, src=None):
        dst = out_ref.at[chip, half]
        return pltpu.make_async_remote_copy(
            src_ref=dst if src is None else src, dst_ref=dst, send_sem=send_sems.at[k], recv_sem=recv_sems.at[k],
            device_id=to, device_id_type=MESH)

    def sends():
        return [cp(j, me, c, (cx, cy, c), src=in_ref.at[c]) for j, (cx, cy) in enumerate(chips)]

    def relays():
        return [cp(3 + j, 2 * cx + cy, c, sibling) for j, (cx, cy) in enumerate(chips)]

    def start():
        for f in sends():
            f.start()

    def relay():
        onward = relays()
        for j, (cx, cy) in enumerate(chips):
            cp(j, 2 * cx + cy, c, sibling).wait_recv()
            onward[j].start()

    def finish():
        for j, (cx, cy) in enumerate(chips):
            cp(3 + j, 2 * cx + cy, 1 - c, sibling).wait_recv()
        for f in sends() + relays():
            f.wait_send()

    return start, relay, finish


_GATHER_SCRATCH = [pltpu.SemaphoreType.DMA((6,)), pltpu.SemaphoreType.DMA((6,))]


def _gather_shards(shard, *, name):
    _, rh, lanes = shard.shape

    def body(in_ref, out_ref, send_sems, recv_sems):
        start, relay, finish = _gather_protocol(in_ref, out_ref, send_sems, recv_sems)
        start()
        relay()
        finish()

    return pl.pallas_call(
        body, name=name, in_specs=[_ANY], out_specs=_ANY,
        out_shape=jax.ShapeDtypeStruct((N_CHIPS, 2, rh, lanes), shard.dtype),
        scratch_shapes=list(_GATHER_SCRATCH),
    )(shard)


def _scatter_protocol(p_ref, out_ref, send_sems, recv_sems):
    x, y, c, chips = _place()

    def copies():
        return [pltpu.make_async_remote_copy(
            src_ref=p_ref.at[2 * cx + cy], dst_ref=out_ref.at[j], send_sem=send_sems.at[j], recv_sem=recv_sems.at[j],
            device_id=(cx, cy, c), device_id_type=MESH) for j, (cx, cy) in enumerate(chips)]

    def start():
        for cpy in copies():
            cpy.start()

    def finish():
        for cpy in copies():
            cpy.wait()

    return start, finish


_SCATTER_SCRATCH = [pltpu.SemaphoreType.DMA((3,)), pltpu.SemaphoreType.DMA((3,))]


def _swap_protocol(g_ref, out_ref, send_sems, recv_sems):
    x, y, c, _ = _place()

    def copies():
        return [pltpu.make_async_remote_copy(
            src_ref=g_ref.at[k, 1 - c], dst_ref=out_ref.at[k], send_sem=send_sems.at[k], recv_sem=recv_sems.at[k],
            device_id=(x, y, 1 - c), device_id_type=MESH) for k in range(N_CHIPS)]

    def start():
        for cpy in copies():
            cpy.start()

    def finish():
        for cpy in copies():
            cpy.wait()

    return start, finish


_SWAP_SCRATCH = [pltpu.SemaphoreType.DMA((N_CHIPS,)), pltpu.SemaphoreType.DMA((N_CHIPS,))]


def _rs_swap_halves(g, *, name):
    nch, _, rh, lanes = g.shape

    def body(g_ref, out_ref, send_sems, recv_sems):
        start, finish = _swap_protocol(g_ref, out_ref, send_sems, recv_sems)
        start()
        finish()

    return pl.pallas_call(
        body, name=name, in_specs=[_ANY], out_specs=_ANY,
        out_shape=jax.ShapeDtypeStruct((nch, rh, lanes), g.dtype),
        scratch_shapes=list(_SWAP_SCRATCH),
    )(g)


def _rs_add_pair(g, got, c_idx, *, name):
    nch, _, rh, lanes = g.shape
    tr = _row_block(rh, lanes)

    def body(c_ref, g_ref, got_ref, p32_ref, p16_ref):
        s = g_ref[...] + got_ref[...]
        p32_ref[...] = s
        p16_ref[...] = s.astype(bf16)

    blk = pl.BlockSpec((None, tr, lanes), lambda k, i, c_ref: (k, i, 0))
    return pl.pallas_call(
        body, name=name,
        grid_spec=pltpu.PrefetchScalarGridSpec(
            num_scalar_prefetch=1, grid=(nch, rh // tr),
            in_specs=[pl.BlockSpec((None, None, tr, lanes), lambda k, i, c_ref: (k, c_ref[0], i, 0)), blk],
            out_specs=[blk, blk]),
        out_shape=[jax.ShapeDtypeStruct((nch, rh, lanes), f32), jax.ShapeDtypeStruct((nch, rh, lanes), bf16)],
        compiler_params=_params(("parallel", "parallel")),
    )(c_idx, g, got)


def _rs_add_chips(p32, got, place, *, name):
    _, rh, lanes = p32.shape
    tr = _row_block(rh, lanes)

    def body(place_ref, p_ref, got_ref, o_ref):
        o_ref[...] = ((p_ref[...] + got_ref[0].astype(f32)) + got_ref[1].astype(f32)) + got_ref[2].astype(f32)

    return pl.pallas_call(
        body, name=name,
        grid_spec=pltpu.PrefetchScalarGridSpec(
            num_scalar_prefetch=1, grid=(rh // tr,),
            in_specs=[pl.BlockSpec((None, tr, lanes), lambda i, place_ref: (place_ref[0], i, 0)),
                      pl.BlockSpec((3, tr, lanes), lambda i, place_ref: (0, i, 0))],
            out_specs=pl.BlockSpec((None, tr, lanes), lambda i, place_ref: (place_ref[1], i, 0))),
        out_shape=jax.ShapeDtypeStruct((2, rh, lanes), f32),
        compiler_params=_params(("parallel",)),
    )(place, p32, got)


def _rs_join_halves(halves, *, name):
    def body(h_ref, out_ref, send_sem, recv_sem):
        x, y, c, _ = _place()
        cpy = pltpu.make_async_remote_copy(
            src_ref=h_ref.at[c], dst_ref=out_ref.at[c], send_sem=send_sem, recv_sem=recv_sem,
            device_id=(x, y, 1 - c), device_id_type=MESH)
        cpy.start()
        cpy.wait()

    return pl.pallas_call(
        body, name=name, in_specs=[_ANY], out_specs=_ANY,
        out_shape=jax.ShapeDtypeStruct(halves.shape, halves.dtype), input_output_aliases={0: 0},
        scratch_shapes=[pltpu.SemaphoreType.DMA, pltpu.SemaphoreType.DMA],
    )(halves)


def _all_reduce_small(s, *, name):
    rs, lanes = s.shape
    rh = rs // 2

    def body(s_ref, o_ref, sib_ref, mine_ref, chips_ref, send_sems, recv_sems):
        x, y, c, chips = _place()
        me = 2 * x + y
        sibling = (x, y, 1 - c)
        rows = pl.ds(pl.multiple_of(c * rh, 8), rh)

        def cp(k, src, dst, to):
            return pltpu.make_async_remote_copy(src_ref=src, dst_ref=dst, send_sem=send_sems.at[k],
                                                recv_sem=recv_sems.at[k], device_id=to, device_id_type=MESH)

        swap = cp(0, s_ref, sib_ref, sibling)
        swap.start()
        swap.wait()
        mine_ref[...] = s_ref[rows, :] + sib_ref[rows, :]
        sends = [cp(1 + j, mine_ref, chips_ref.at[j], (cx, cy, c)) for j, (cx, cy) in enumerate(chips)]
        for cpy in sends:
            cpy.start()
        for cpy in sends:
            cpy.wait()
        where = [2 * cx + cy for cx, cy in chips]
        total = None
        for q in range(N_CHIPS):
            term = jnp.where(q == me, mine_ref[...], jnp.where(
                q == where[0], chips_ref[0], jnp.where(q == where[1], chips_ref[1], chips_ref[2])))
            total = term if total is None else total + term
        o_ref[rows, :] = total
        push = cp(4, o_ref.at[rows, :], o_ref.at[rows, :], sibling)
        push.start()
        push.wait()

    vm = pl.BlockSpec(memory_space=pltpu.VMEM)
    return pl.pallas_call(
        body, name=name, in_specs=[vm], out_specs=vm,
        out_shape=jax.ShapeDtypeStruct((rs, lanes), f32),
        scratch_shapes=[pltpu.VMEM((rs, lanes), f32), pltpu.VMEM((rh, lanes), f32),
                        pltpu.VMEM((N_CHIPS - 1, rh, lanes), f32), pltpu.SemaphoreType.DMA((5,)),
                        pltpu.SemaphoreType.DMA((5,))],
        compiler_params=pltpu.CompilerParams(vmem_limit_bytes=32 * _MB),
    )(s)


def _pad_lanes(a, width=LANES):
    return jnp.pad(a, ((0, 0), (0, width - a.shape[1])))


def _local_grads(x, tgt, wts, small, *, fwd_ride=None, late_weights=None, swap_ride=None, bwd_ride=None,
                 last_ride=None):
    t = x.shape[0]
    tm = min(t, 1024)
    d = D_MODEL
    mm = functools.partial(_matmul, tm=tm)

    dtb = _pad_lanes(small["dt_bias"])
    alog = _pad_lanes(small["a_log"])
    dsk = jnp.repeat(small["d_skip"], HEAD_DIM, axis=1)
    bsp_t = _pad_lanes(small["b_spatial"].T)
    wsp = small["w_spatial"]

    h = _rms_fwd(x, small["norm_mix_g"], name="rms_mix")
    uv = mm(h, wts["uv"], tn=1024, tk=d, out_dtypes=[f32], name="proj_uv")
    z = mm(h, wts["z"], tn=1024, tk=d, out_dtypes=[f32], name="proj_z")
    xbc = mm(h, wts["xbc"], tn=1024, tk=d, out_dtypes=[f32], name="proj_xbc")
    dtr = mm(h, wts["dt"], tn=LANES, tk=d, out_dtypes=[f32], name="proj_dt")
    gl = mm(h, wts["gate"], tn=1024, tk=d, out_dtypes=[f32], name="proj_gate")
    ya = _gmlp_fwd(uv, small["v_norm_g"], small["v_norm_b"], wsp, bsp_t, name="gmlp_fwd")
    yb, hprev, cv, *gathered = _ssd_fwd(xbc, z, dtr, small["conv_w"], small["conv_b"], dtb, alog, dsk,
                                    small["ssm_norm_g"], ride=fwd_ride, name="ssd_fwd")
    if fwd_ride is not None:
        wts = {**wts, **late_weights(gathered[0])}
    pa = mm(ya, wts["pa"], tn=1024, tk=1024, out_dtypes=[f32], name="proj_a")
    pb = mm(yb, wts["pb"], tn=1024, tk=1024, out_dtypes=[f32], name="proj_b")
    merged = _merge_fwd(pa, pb, gl, small["b_gates"], name="merge_fwd")
    x1 = mm(merged, wts["out"], tn=1024, tk=1024, out_dtypes=[f32], extras=[x],
            epilogue=lambda acc, res: (res + acc,), name="out_proj")
    h2 = _rms_fwd(x1, small["norm_mlp_g"], name="rms_mlp")
    act = mm(h2, wts["up"], tn=1024, tk=d, out_dtypes=[bf16],
             epilogue=lambda acc: (jnp.square(jnp.maximum(acc, 0.0)),), name="mlp_up")
    x2 = mm(act, wts["down"], tn=1024, tk=2048, out_dtypes=[f32], extras=[x1],
            epilogue=lambda acc, res: (res + acc,), name="mlp_down")

    dx2, dx2b, dgf, loss = _loss_head(x2, tgt, small["norm_final_g"], name="loss_head")
    tt = min(t, 2048)
    tn_mm = functools.partial(_matmul_tn, tt=tt)
    dw = {}
    dw["down"] = tn_mm(act, dx2b, tka=1024, tn=1024, name="dw_down")
    dup = mm(dx2b, wts["down"], nt=True, tn=1024, tk=1024, out_dtypes=[bf16], extras=[act],
             epilogue=lambda acc, a2: (acc * (2.0 * jnp.sqrt(a2).astype(f32)),), name="d_act")
    dw["up"] = tn_mm(h2, dup, tka=1024, tn=1024, name="dw_up")
    dh2 = mm(dup, wts["up"], nt=True, tn=1024, tk=2048, out_dtypes=[f32], name="d_h2")
    dx1, dx1b, dg_mlp = _rms_bwd(x1, small["norm_mlp_g"], dh2, dx2, want_bf16=True, name="rms_mlp_bwd")
    dw["out"] = tn_mm(merged, dx1b, tka=1024, tn=1024, name="dw_out")
    dmerged = mm(dx1b, wts["out"], nt=True, tn=1024, tk=1024, out_dtypes=[f32], name="d_merged")
    dpa, dpb, dgl, dbg = _merge_bwd(dmerged, pa, pb, gl, small["b_gates"], name="merge_bwd")
    dw["pa"] = tn_mm(ya, dpa, tka=1024, tn=1024, name="dw_pa")
    dw["pb"] = tn_mm(yb, dpb, tka=1024, tn=1024, name="dw_pb")
    dya = mm(dpa, wts["pa"], nt=True, tn=1024, tk=1024, out_dtypes=[f32], name="d_ya")
    dyb = mm(dpb, wts["pb"], nt=True, tn=1024, tk=1024, out_dtypes=[f32], name="d_yb")
    swapped = swap_ride(dw) if swap_ride is not None else None
    duv, dwsp, dbsp_t, dvg, dvb, *got_pair = _gmlp_bwd(uv, dya, small["v_norm_g"], small["v_norm_b"], wsp, bsp_t,
                                                       ride=swapped, name="gmlp_bwd")
    ride = bwd_ride(swapped, got_pair[0]) if bwd_ride is not None else None
    dz, dxbc, ddt, dcw, dcb, ddtb, dalog, ddsk, dgs, *got = _ssd_bwd(
        xbc, cv, z, dtr, hprev, dyb, small["conv_w"], dtb, alog, dsk, small["ssm_norm_g"],
        _head_seg_matrix(), ride=ride, name="ssd_bwd")
    dw["uv"] = tn_mm(h, duv, tka=1024, tn=1024, name="dw_uv")
    dw["z"] = tn_mm(h, dz, tka=1024, tn=1024, name="dw_z")
    dw["xbc"] = tn_mm(h, dxbc, tka=1024, tn=1024, name="dw_xbc")
    dw["dt"] = tn_mm(h, ddt, tka=1024, tn=LANES, name="dw_dt")
    dw["gate"] = tn_mm(h, dgl, tka=1024, tn=1024, name="dw_gate")
    last = last_ride(dw) if last_ride is not None else None
    res = _matmul_nt_sum(
        [(duv, wts["uv"]), (dz, wts["z"]), (dxbc, wts["xbc"]), (dgl, wts["gate"]), (ddt, wts["dt"])],
        tm=tm, tks=[1024] * 4 + [LANES], ride=last, name="d_h")
    dh, got_last = (res[0], res[1]) if last is not None else (res, None)
    dx, dg_mix = _rms_bwd(x, small["norm_mix_g"], dh, dx1, want_bf16=False, name="rms_mix_bwd")

    dsmall = {
        "norm_mix_g": dg_mix, "conv_w": dcw, "conv_b": dcb, "dt_bias": ddtb[:, :N_HEADS], "a_log": dalog[:, :N_HEADS],
        "d_skip": ddsk[:, :N_HEADS], "ssm_norm_g": dgs, "v_norm_g": dvg, "v_norm_b": dvb, "w_spatial": dwsp,
        "b_spatial": dbsp_t[:, :GMLP_GROUPS].T, "b_gates": dbg, "norm_mlp_g": dg_mlp, "norm_final_g": dgf,
    }
    return loss, dx, dw, dsmall, (got[0] if got else None), got_last


_IN_SHARD = IN_PROJ // N_CHIPS
_LATE = ("w_proj_a", "w_proj_b", "w_out", "w_mlp_up", "w_mlp_down")
_LATE_ROWS = {"w_proj_a": GMLP_WIDTH // N_CHIPS, "w_proj_b": D_INNER // N_CHIPS, "w_out": D_MODEL // N_CHIPS,
              "w_mlp_up": D_MODEL, "w_mlp_down": D_FF // N_CHIPS}
_LATE_TOTAL = sum(_LATE_ROWS.values())


def _late_offsets():
    off, out = 0, {}
    for k in _LATE:
        out[k] = off
        off += _LATE_ROWS[k]
    return out


_LATE_OFF = _late_offsets()

_SMALL = ("norm_mix_g", "conv_w", "conv_b", "dt_bias", "a_log", "d_skip", "ssm_norm_g", "v_norm_g", "v_norm_b",
          "w_spatial", "b_spatial", "b_gates", "norm_mlp_g", "norm_final_g")


def _pack_small(parts):
    flat = jnp.concatenate([parts[k].reshape(-1) for k in _SMALL])
    rows = -(-flat.shape[0] // (16 * LANES)) * 16
    return jnp.pad(flat, (0, rows * LANES - flat.shape[0])).reshape(rows, LANES)


def _unpack_small(packed, shapes):
    flat = packed.reshape(-1)
    out, off = {}, 0
    for k in _SMALL:
        n = math.prod(shapes[k])
        out[k] = flat[off:off + n].reshape(shapes[k])
        off += n
    return out


def _from_chip_columns(stacked):
    _, rows, cols = stacked.shape
    return stacked.transpose(1, 0, 2).reshape(rows, N_CHIPS * cols)


def _to_chip_columns(full):
    rows, cols = full.shape
    return full.reshape(rows, N_CHIPS, cols // N_CHIPS).transpose(1, 0, 2)


def _w_in_grad_by_chip(dw):
    pieces = [dw["uv"], dw["z"], dw["xbc"], dw["dt"][:, :N_HEADS], dw["gate"]]
    bounds = [0]
    for p in pieces:
        bounds.append(bounds[-1] + p.shape[1])
    chips = []
    for k in range(N_CHIPS):
        lo, hi = k * _IN_SHARD, (k + 1) * _IN_SHARD
        parts = [p[:, max(lo, b0) - b0:min(hi, b1) - b0]
                 for p, b0, b1 in zip(pieces, bounds[:-1], bounds[1:]) if min(hi, b1) > max(lo, b0)]
        chips.append(jnp.concatenate(parts, axis=1))
    return jnp.stack(chips)


def kernel(x, norm_mix_g, w_in, conv_w, conv_b, dt_bias, a_log, d_skip, ssm_norm_g, v_norm_g, v_norm_b, w_spatial, b_spatial, b_gates, w_proj_a, w_proj_b, w_out, norm_mlp_g, w_mlp_up, w_mlp_down, norm_final_g, loss_target, m_norm_mix_g, m_w_in, m_conv_w, m_conv_b, m_dt_bias, m_a_log, m_d_skip, m_ssm_norm_g, m_v_norm_g, m_v_norm_b, m_w_spatial, m_b_spatial, m_b_gates, m_w_proj_a, m_w_proj_b, m_w_out, m_norm_mlp_g, m_w_mlp_up, m_w_mlp_down, m_norm_final_g, v_norm_mix_g, v_w_in, v_conv_w, v_conv_b, v_dt_bias, v_a_log, v_d_skip, v_ssm_norm_g, v_v_norm_g, v_v_norm_b, v_w_spatial, v_b_spatial, v_b_gates, v_w_proj_a, v_w_proj_b, v_w_out, v_norm_mlp_g, v_w_mlp_up, v_w_mlp_down, v_norm_final_g):
    given = dict(locals())
    names = ("norm_mix_g", "w_in", "conv_w", "conv_b", "dt_bias", "a_log", "d_skip", "ssm_norm_g", "v_norm_g",
             "v_norm_b", "w_spatial", "b_spatial", "b_gates", "w_proj_a", "w_proj_b", "w_out", "norm_mlp_g",
             "w_mlp_up", "w_mlp_down", "norm_final_g")
    xi, yi, ci = lax.axis_index("x"), lax.axis_index("y"), lax.axis_index("c")
    me_chip = (2 * xi + yi).astype(jnp.int32)

    def halves(a):
        return a.reshape(2, a.shape[0] // 2, a.shape[1])

    def with_own(got, shard):
        whole = lax.dynamic_update_slice(got, shard[None], (me_chip, 0, 0, 0))
        return whole.reshape(N_CHIPS, 2 * shard.shape[1], shard.shape[2])

    shard_in = halves(_cast_bf16(w_in[0], name="cast_w_in"))
    shard_late = halves(_cast_bf16(jnp.concatenate([given[k][0] for k in _LATE]), name="cast_w_late"))
    shard_conv = halves(conv_w.reshape(2 * _TAIL, -1))
    w_in_full = _from_chip_columns(with_own(_gather_shards(shard_in, name="gather_w_in"), shard_in))
    o_dt, o_gate = 2 * GMLP_WIDTH + D_INNER + CONV_DIM, 2 * GMLP_WIDTH + D_INNER + CONV_DIM + N_HEADS
    wts = {
        "uv": w_in_full[:, :2 * GMLP_WIDTH], "z": w_in_full[:, 2 * GMLP_WIDTH:2 * GMLP_WIDTH + D_INNER],
        "xbc": w_in_full[:, 2 * GMLP_WIDTH + D_INNER:o_dt], "dt": _pad_lanes(w_in_full[:, o_dt:o_gate]),
        "gate": w_in_full[:, o_gate:],
    }
    conv_all = with_own(_gather_shards(shard_conv, name="gather_conv_w"), shard_conv)
    conv_full = _from_chip_columns(conv_all.reshape(N_CHIPS, CONV_W, CONV_DIM // N_CHIPS))

    def late_weights(got):
        g_late = with_own(got, shard_late)

        def rows_of(k):
            return g_late[:, _LATE_OFF[k]:_LATE_OFF[k] + _LATE_ROWS[k]]

        return {
            "pa": rows_of("w_proj_a").reshape(GMLP_WIDTH, D_MODEL),
            "pb": rows_of("w_proj_b").reshape(D_INNER, D_MODEL), "out": rows_of("w_out").reshape(D_MODEL, D_MODEL),
            "up": _from_chip_columns(rows_of("w_mlp_up")), "down": rows_of("w_mlp_down").reshape(D_FF, D_MODEL),
        }

    small = {
        "norm_mix_g": norm_mix_g, "conv_w": conv_full, "conv_b": conv_b, "dt_bias": dt_bias, "a_log": a_log,
        "d_skip": d_skip, "ssm_norm_g": ssm_norm_g, "v_norm_g": v_norm_g, "v_norm_b": v_norm_b,
        "w_spatial": w_spatial[0], "b_spatial": b_spatial[0], "b_gates": b_gates, "norm_mlp_g": norm_mlp_g,
        "norm_final_g": norm_final_g.reshape(1, D_MODEL),
    }

    c_idx = ci.astype(jnp.int32).reshape(1)
    place = jnp.stack([me_chip, ci.astype(jnp.int32)])
    partials = {}

    def reduced_shard(tag, got_chips):
        own = _rs_add_chips(partials[tag], got_chips, place, name="rs_add_chips_" + tag)
        both = _rs_join_halves(own, name="rs_join_" + tag)
        return both.reshape(2 * both.shape[1], both.shape[2])

    def late_grads(dw):
        def by_rows(a):
            return a.reshape(N_CHIPS, a.shape[0] // N_CHIPS, a.shape[1])

        g = jnp.concatenate([by_rows(dw["pa"]), by_rows(dw["pb"]), by_rows(dw["out"]), _to_chip_columns(dw["up"]),
                             by_rows(dw["down"])], axis=1)
        return g.reshape(N_CHIPS, 2, g.shape[1] // 2, g.shape[2])

    def late_partials(g, got_pair):
        partials["late"], p16 = _rs_add_pair(g, got_pair, c_idx, name="rs_add_pair_late")
        return p16

    def in_partials(dw):
        g = _w_in_grad_by_chip(dw).reshape(N_CHIPS, 2, D_MODEL // 2, _IN_SHARD)
        partials["in"], p16 = _rs_add_pair(g, _rs_swap_halves(g, name="rs_swap_in"), c_idx, name="rs_add_pair_in")
        return p16

    loss_part, grad_x, dw, dsmall, got_late, got_in = _local_grads(
        x[0], loss_target[0], wts, small, fwd_ride=shard_late, late_weights=late_weights, swap_ride=late_grads,
        bwd_ride=late_partials, last_ride=in_partials)
    loss = lax.psum(loss_part[0, 0], ("x", "y", "c"))
    g_late = reduced_shard("late", got_late)
    g_in_shard = reduced_shard("in", got_in)

    small_shapes = {k: dsmall[k].shape for k in _SMALL}
    red = _unpack_small(_all_reduce_small(_pack_small(dsmall), name="all_reduce_small"), small_shapes)
    conv_cols = CONV_DIM // N_CHIPS
    red["conv_w"] = lax.dynamic_slice_in_dim(red["conv_w"], me_chip * conv_cols, conv_cols, axis=1)

    grads, deltas, new_m, new_v = {}, {}, {}, {}
    for k in ("w_in",) + _LATE:
        g2 = g_in_shard if k == "w_in" else g_late[_LATE_OFF[k]:_LATE_OFF[k] + _LATE_ROWS[k]]
        dlt, m2, v2 = _adamw(given[k][0], g2, given["m_" + k][0], given["v_" + k][0], name="adamw_" + k)
        grads[k], deltas[k], new_m[k], new_v[k] = g2, dlt, m2, v2
    adam_shapes = dict(small_shapes)
    adam_shapes["conv_w"] = (CONV_W, conv_cols)

    def small_pack_of(prefix):
        return _pack_small({k: given[prefix + k].reshape(adam_shapes[k]) for k in _SMALL})

    dlt_s, m_s, v_s = _adamw(small_pack_of(""), _pack_small(red), small_pack_of("m_"), small_pack_of("v_"),
                             name="adamw_small")
    for dst, packed in ((deltas, dlt_s), (new_m, m_s), (new_v, v_s)):
        dst.update(_unpack_small(packed, adam_shapes))
    grads.update(red)

    def shaped(dct):
        return [dct[k].reshape(given[k].shape) for k in names]

    return (loss, grad_x[None], *shaped(grads), *shaped(deltas), *shaped(new_m), *shaped(new_v))
```

```python
import functools
import math

import jax
import jax.numpy as jnp
from jax import lax
from jax.experimental import pallas as pl
from jax.experimental.pallas import tpu as pltpu

f32 = jnp.float32
bf16 = jnp.bfloat16

D_MODEL = 1024
CHUNK = 128
GMLP_WIDTH = 1024
GMLP_GROUPS = 8
D_INNER = 2048
HEAD_DIM = 64
N_HEADS = 32
N_GROUPS = 8
HEADS_PER_GROUP = 4
GROUP_W = HEADS_PER_GROUP * HEAD_DIM
D_STATE = 128
CONV_W = 4
CONV_DIM = 4096
D_FF = 4096
IN_PROJ = 10272
NORM_EPS = 1e-6
N_CHIPS = 4
N_DEV = 8
LANES = 128

ADAM_LR = 0.001
ADAM_B1 = 0.9
ADAM_B2 = 0.999
ADAM_EPS = 1e-08
ADAM_WD = 0.01
ADAM_STEP = 10

MESH = pl.DeviceIdType.MESH
_NT = (((1,), (1,)), ((), ()))
_NN = (((1,), (0,)), ((), ()))
_TN = (((0,), (0,)), ((), ()))
_MB = 2 ** 20


def _params(sem, vmem_mb=48):
    return pltpu.CompilerParams(dimension_semantics=sem, vmem_limit_bytes=vmem_mb * _MB)


def _dot(a, b, dims=_NN):
    return lax.dot_general(a.astype(bf16), b.astype(bf16), dims, preferred_element_type=f32)


def _dot32(a, b):
    return jnp.dot(a, b, preferred_element_type=f32, precision=lax.Precision.HIGHEST)


def _sigmoid(x):
    return 1.0 / (1.0 + jnp.exp(-x))


def _sum_all(a):
    return jnp.sum(jnp.sum(a, axis=1, keepdims=True), axis=0, keepdims=True)


def _iota(shape, dim):
    return lax.broadcasted_iota(jnp.int32, shape, dim)


def _matmul(a, b, *, nt=False, tm, tn, tk, out_dtypes, epilogue=None, extras=(), name):
    m, k_dim = a.shape
    n = b.shape[0] if nt else b.shape[1]
    nk = k_dim // tk
    ne, no = len(extras), len(out_dtypes)
    dims = _NT if nt else _NN

    def body(*refs):
        a_ref, b_ref = refs[0], refs[1]
        ex = refs[2:2 + ne]
        outs = refs[2 + ne:2 + ne + no]

        def finish(acc):
            vals = epilogue(acc, *[e[...] for e in ex]) if epilogue is not None else (acc,)
            for o, v in zip(outs, vals):
                o[...] = v.astype(o.dtype)

        part = lax.dot_general(a_ref[...], b_ref[...], dims, preferred_element_type=f32)
        if nk == 1:
            finish(part)
        else:
            acc_ref = refs[-1]
            kk = pl.program_id(2)

            @pl.when(kk == 0)
            def _():
                acc_ref[...] = part

            @pl.when(kk > 0)
            def _():
                acc_ref[...] += part

            @pl.when(kk == nk - 1)
            def _():
                finish(acc_ref[...])

    b_spec = pl.BlockSpec((tn, tk), lambda i, j, k: (j, k)) if nt else pl.BlockSpec((tk, tn), lambda i, j, k: (k, j))
    tile = pl.BlockSpec((tm, tn), lambda i, j, k: (i, j))
    outs = pl.pallas_call(
        body, name=name, grid=(m // tm, n // tn, nk),
        in_specs=[pl.BlockSpec((tm, tk), lambda i, j, k: (i, k)), b_spec] + [tile] * ne,
        out_specs=[tile] * no,
        out_shape=[jax.ShapeDtypeStruct((m, n), dt) for dt in out_dtypes],
        scratch_shapes=[pltpu.VMEM((tm, tn), f32)] if nk > 1 else [],
        compiler_params=_params(("parallel", "parallel", "arbitrary")),
    )(a, b, *extras)
    return outs if no > 1 else outs[0]


def _matmul_nt_sum(pairs, *, tm, tks, ride=None, name):
    m = pairs[0][0].shape[0]
    n = pairs[0][1].shape[0]
    nblk = [a.shape[1] // tk for (a, _), tk in zip(pairs, tks)]
    starts = [sum(nblk[:p]) for p in range(len(pairs))]
    nk = sum(nblk)
    npairs = len(pairs)
    ni = m // tm
    riding = ride is not None

    def body(*refs):
        rest = refs[2 * npairs:]
        if riding:
            ride_ref, o_ref, got_ref, acc_ref, send_sems, recv_sems = rest
        else:
            o_ref, acc_ref = rest
        i, kk = pl.program_id(0), pl.program_id(1)
        if riding:
            start, finish = _scatter_protocol(ride_ref, got_ref, send_sems, recv_sems)
            pl.when((i == 0) & (kk == 0))(start)

        @pl.when(kk == 0)
        def _():
            acc_ref[...] = jnp.zeros_like(acc_ref)

        for p in range(npairs):
            @pl.when((kk >= starts[p]) & (kk < starts[p] + nblk[p]))
            def _(p=p):
                acc_ref[...] += lax.dot_general(refs[2 * p][...], refs[2 * p + 1][...], _NT, preferred_element_type=f32)

        @pl.when(kk == nk - 1)
        def _():
            o_ref[...] = acc_ref[...]

        if riding:
            pl.when((i == ni - 1) & (kk == nk - 1))(finish)

    in_specs, args = [], []
    for p, (a, b) in enumerate(pairs):
        def kblock(k, s=starts[p], nb=nblk[p]):
            return jnp.clip(k - s, 0, nb - 1)
        in_specs.append(pl.BlockSpec((tm, tks[p]), lambda i, k, kb=kblock: (i, kb(k))))
        in_specs.append(pl.BlockSpec((n, tks[p]), lambda i, k, kb=kblock: (0, kb(k))))
        args += [a, b]
    tile = pl.BlockSpec((tm, n), lambda i, k: (i, 0))
    outs = pl.pallas_call(
        body, name=name, grid=(ni, nk), in_specs=in_specs + [_ANY] * riding, out_specs=[tile] + [_ANY] * riding,
        out_shape=[jax.ShapeDtypeStruct((m, n), f32)]
        + ([jax.ShapeDtypeStruct((N_CHIPS - 1,) + ride.shape[1:], ride.dtype)] if riding else []),
        scratch_shapes=[pltpu.VMEM((tm, n), f32)] + (list(_SCATTER_SCRATCH) if riding else []),
        compiler_params=_params(("arbitrary", "arbitrary"), vmem_mb=56),
    )(*args, *([ride] if riding else []))
    return outs if riding else outs[0]


def _matmul_tn(a, b, *, tka, tn, tt, name):
    t, ka = a.shape
    n = b.shape[1]

    def body(a_ref, b_ref, o_ref):
        part = lax.dot_general(a_ref[...], b_ref[...], _TN, preferred_element_type=f32)
        kk = pl.program_id(2)

        @pl.when(kk == 0)
        def _():
            o_ref[...] = part

        @pl.when(kk > 0)
        def _():
            o_ref[...] += part

    return pl.pallas_call(
        body, name=name, grid=(ka // tka, n // tn, t // tt),
        in_specs=[pl.BlockSpec((tt, tka), lambda i, j, k: (k, i)), pl.BlockSpec((tt, tn), lambda i, j, k: (k, j))],
        out_specs=pl.BlockSpec((tka, tn), lambda i, j, k: (i, j)),
        out_shape=jax.ShapeDtypeStruct((ka, n), f32),
        compiler_params=_params(("parallel", "parallel", "arbitrary")),
    )(a, b)


def _row_tile(t):
    return min(t, 512)


def _rms_fwd(x, g, *, name):
    t, d = x.shape
    tr = _row_tile(t)

    def body(x_ref, g_ref, h_ref):
        xv = x_ref[...]
        r = lax.rsqrt(jnp.mean(xv * xv, axis=1, keepdims=True) + NORM_EPS)
        h_ref[...] = (xv * r * g_ref[...]).astype(bf16)

    return pl.pallas_call(
        body, name=name, grid=(t // tr,),
        in_specs=[pl.BlockSpec((tr, d), lambda i: (i, 0)), pl.BlockSpec((1, d), lambda i: (0, 0))],
        out_specs=pl.BlockSpec((tr, d), lambda i: (i, 0)),
        out_shape=jax.ShapeDtypeStruct((t, d), bf16),
        compiler_params=_params(("parallel",)),
    )(x, g)


def _rms_bwd(xin, g, dh, dres, *, want_bf16, name):
    t, d = xin.shape
    tr = _row_tile(t)

    def body(x_ref, g_ref, dh_ref, dres_ref, dx_ref, *rest):
        dg_ref = rest[-1]
        xv = x_ref[...]
        r = lax.rsqrt(jnp.mean(xv * xv, axis=1, keepdims=True) + NORM_EPS)
        xn = xv * r
        dhv = dh_ref[...]
        dxn = dhv * g_ref[...]
        dx = dres_ref[...] + r * (dxn - xn * jnp.mean(dxn * xn, axis=1, keepdims=True))
        dx_ref[...] = dx
        if want_bf16:
            rest[0][...] = dx.astype(bf16)
        part = jnp.sum(dhv * xn, axis=0, keepdims=True)

        @pl.when(pl.program_id(0) == 0)
        def _():
            dg_ref[...] = part

        @pl.when(pl.program_id(0) > 0)
        def _():
            dg_ref[...] += part

    row = pl.BlockSpec((tr, d), lambda i: (i, 0))
    vec = pl.BlockSpec((1, d), lambda i: (0, 0))
    out_shape = [jax.ShapeDtypeStruct((t, d), f32)] + ([jax.ShapeDtypeStruct((t, d), bf16)] if want_bf16 else []) \
        + [jax.ShapeDtypeStruct((1, d), f32)]
    return pl.pallas_call(
        body, name=name, grid=(t // tr,),
        in_specs=[row, vec, row, row],
        out_specs=[row] + ([row] if want_bf16 else []) + [vec],
        out_shape=out_shape,
        compiler_params=_params(("arbitrary",)),
    )(xin, g, dh, dres)


def _loss_head(x2, tgt, g, *, name):
    t, d = x2.shape
    tr = _row_tile(t)

    def body(x_ref, t_ref, g_ref, dx_ref, dxb_ref, dg_ref, loss_ref):
        xv = x_ref[...]
        gv = g_ref[...]
        r = lax.rsqrt(jnp.mean(xv * xv, axis=1, keepdims=True) + NORM_EPS)
        xn = xv * r
        e = xn * gv - t_ref[...]
        lpart = jnp.zeros((1, LANES), f32) + 0.5 * _sum_all(jnp.mean(e * e, axis=1, keepdims=True))
        dy = e * (1.0 / d)
        dxn = dy * gv
        dx = r * (dxn - xn * jnp.mean(dxn * xn, axis=1, keepdims=True))
        dx_ref[...] = dx
        dxb_ref[...] = dx.astype(bf16)
        gpart = jnp.sum(dy * xn, axis=0, keepdims=True)

        @pl.when(pl.program_id(0) == 0)
        def _():
            dg_ref[...] = gpart
            loss_ref[...] = lpart

        @pl.when(pl.program_id(0) > 0)
        def _():
            dg_ref[...] += gpart
            loss_ref[...] += lpart

    row = pl.BlockSpec((tr, d), lambda i: (i, 0))
    vec = pl.BlockSpec((1, d), lambda i: (0, 0))
    return pl.pallas_call(
        body, name=name, grid=(t // tr,),
        in_specs=[row, row, vec],
        out_specs=[row, row, vec, pl.BlockSpec((1, LANES), lambda i: (0, 0))],
        out_shape=[jax.ShapeDtypeStruct((t, d), f32), jax.ShapeDtypeStruct((t, d), bf16),
                   jax.ShapeDtypeStruct((1, d), f32), jax.ShapeDtypeStruct((1, LANES), f32)],
        compiler_params=_params(("arbitrary",)),
    )(x2, tgt, g)


def _merge_fwd(pa, pb, gl, bg, *, name):
    t, d = pa.shape
    tr = _row_tile(t)

    def body(pa_ref, pb_ref, gla_ref, glb_ref, bga_ref, bgb_ref, o_ref):
        ga = _sigmoid(gla_ref[...] + bga_ref[...])
        gb = _sigmoid(glb_ref[...] + bgb_ref[...])
        o_ref[...] = (ga * pa_ref[...] + gb * pb_ref[...]).astype(bf16)

    row = pl.BlockSpec((tr, d), lambda i: (i, 0))
    return pl.pallas_call(
        body, name=name, grid=(t // tr,),
        in_specs=[row, row, row, pl.BlockSpec((tr, d), lambda i: (i, 1)),
                  pl.BlockSpec((1, d), lambda i: (0, 0)), pl.BlockSpec((1, d), lambda i: (0, 1))],
        out_specs=row,
        out_shape=jax.ShapeDtypeStruct((t, d), bf16),
        compiler_params=_params(("parallel",)),
    )(pa, pb, gl, gl, bg, bg)


def _merge_bwd(dm, pa, pb, gl, bg, *, name):
    t, d = pa.shape
    tr = _row_tile(t)

    def body(dm_ref, pa_ref, pb_ref, gla_ref, glb_ref, bga_ref, bgb_ref, dpa_ref, dpb_ref, dgl_ref, dbg_ref):
        dmv = dm_ref[...]
        ga = _sigmoid(gla_ref[...] + bga_ref[...])
        gb = _sigmoid(glb_ref[...] + bgb_ref[...])
        dpa_ref[...] = (dmv * ga).astype(bf16)
        dpb_ref[...] = (dmv * gb).astype(bf16)
        dla = dmv * pa_ref[...] * ga * (1.0 - ga)
        dlb = dmv * pb_ref[...] * gb * (1.0 - gb)
        dgl_ref[:, :d] = dla.astype(bf16)
        dgl_ref[:, d:] = dlb.astype(bf16)
        sa = jnp.sum(dla, axis=0, keepdims=True)
        sb = jnp.sum(dlb, axis=0, keepdims=True)

        @pl.when(pl.program_id(0) == 0)
        def _():
            dbg_ref[:, :d] = sa
            dbg_ref[:, d:] = sb

        @pl.when(pl.program_id(0) > 0)
        def _():
            dbg_ref[:, :d] += sa
            dbg_ref[:, d:] += sb

    row = pl.BlockSpec((tr, d), lambda i: (i, 0))
    return pl.pallas_call(
        body, name=name, grid=(t // tr,),
        in_specs=[row, row, row, row, pl.BlockSpec((tr, d), lambda i: (i, 1)),
                  pl.BlockSpec((1, d), lambda i: (0, 0)), pl.BlockSpec((1, d), lambda i: (0, 1))],
        out_specs=[row, row, pl.BlockSpec((tr, 2 * d), lambda i: (i, 0)), pl.BlockSpec((1, 2 * d), lambda i: (0, 0))],
        out_shape=[jax.ShapeDtypeStruct((t, d), bf16), jax.ShapeDtypeStruct((t, d), bf16),
                   jax.ShapeDtypeStruct((t, 2 * d), bf16), jax.ShapeDtypeStruct((1, 2 * d), f32)],
        compiler_params=_params(("arbitrary",)),
    )(dm, pa, pb, gl, gl, bg, bg)


_INV_SQRT2 = 1.0 / math.sqrt(2.0)
_INV_SQRT2PI = 1.0 / math.sqrt(2.0 * math.pi)


def _gmlp_common(uv, vg, vb, with_grad=False):
    cdf = 0.5 * (1.0 + lax.erf(uv * _INV_SQRT2))
    zz = uv * cdf
    u, vhat, rstd, vn = _gmlp_norm(zz, vg, vb)
    if not with_grad:
        return u, vhat, rstd, vn
    return u, vhat, rstd, vn, cdf + uv * jnp.exp(-0.5 * uv * uv) * _INV_SQRT2PI


def _gmlp_norm(zz, vg, vb):
    u = zz[:, :GMLP_WIDTH]
    v = zz[:, GMLP_WIDTH:]
    mu = jnp.mean(v, axis=1, keepdims=True)
    vc = v - mu
    rstd = lax.rsqrt(jnp.mean(vc * vc, axis=1, keepdims=True) + NORM_EPS)
    vhat = vc * rstd
    vn = vhat * vg + vb
    return u, vhat, rstd, vn


def _gmlp_fwd(uv, vg, vb, wsp, bsp_t, *, name):
    t = uv.shape[0]
    nc = t // CHUNK

    def body(uv_ref, vg_ref, vb_ref, w_ref, b_ref, y_ref):
        u, _, _, vn = _gmlp_common(uv_ref[...], vg_ref[...], vb_ref[...])
        tril = _iota((CHUNK, CHUNK), 0) >= _iota((CHUNK, CHUNK), 1)
        bt = b_ref[...]
        for g in range(GMLP_GROUPS):
            sl = slice(g * CHUNK, (g + 1) * CHUNK)
            w = jnp.where(tril, w_ref[g], 0.0)
            s = _dot(w, vn[:, sl]) + bt[:, g:g + 1]
            y_ref[:, sl] = (u[:, sl] * s).astype(bf16)

    return pl.pallas_call(
        body, name=name, grid=(nc,),
        in_specs=[pl.BlockSpec((CHUNK, 2 * GMLP_WIDTH), lambda c: (c, 0)),
                  pl.BlockSpec((1, GMLP_WIDTH), lambda c: (0, 0)), pl.BlockSpec((1, GMLP_WIDTH), lambda c: (0, 0)),
                  pl.BlockSpec((GMLP_GROUPS, CHUNK, CHUNK), lambda c: (0, 0, 0)),
                  pl.BlockSpec((CHUNK, LANES), lambda c: (0, 0))],
        out_specs=pl.BlockSpec((CHUNK, GMLP_WIDTH), lambda c: (c, 0)),
        out_shape=jax.ShapeDtypeStruct((t, GMLP_WIDTH), bf16),
        compiler_params=_params(("parallel",)),
    )(uv, vg, vb, wsp, bsp_t)


def _gmlp_bwd(uv, dya, vg, vb, wsp, bsp_t, *, ride=None, name):
    t = uv.shape[0]
    nc = t // CHUNK
    riding = ride is not None

    def body(*refs):
        uv_ref, dy_ref, vg_ref, vb_ref, w_ref, b_ref = refs[:6]
        duv_ref, dw_ref, db_ref, dvg_ref, dvb_ref = refs[6 + riding:11 + riding]
        first = pl.program_id(0) == 0
        if riding:
            start, finish = _swap_protocol(refs[6], refs[12], refs[13], refs[14])
            pl.when(first)(start)

        @pl.when(first)
        def _():
            dw_ref[...] = jnp.zeros_like(dw_ref)
            db_ref[...] = jnp.zeros_like(db_ref)
            dvg_ref[...] = jnp.zeros_like(dvg_ref)
            dvb_ref[...] = jnp.zeros_like(dvb_ref)

        uvv = uv_ref[...]
        vgv = vg_ref[...]
        u, vhat, rstd, vn, gelu_grad = _gmlp_common(uvv, vgv, vb_ref[...], with_grad=True)
        dy = dy_ref[...]
        tril = _iota((CHUNK, CHUNK), 0) >= _iota((CHUNK, CHUNK), 1)
        lane = _iota((CHUNK, LANES), 1)
        bt = b_ref[...]
        ds_all = dy * u
        dbacc = jnp.zeros((CHUNK, LANES), f32)
        dvh_parts = []
        for g in range(GMLP_GROUPS):
            sl = slice(g * CHUNK, (g + 1) * CHUNK)
            w = jnp.where(tril, w_ref[g], 0.0)
            vng = vn[:, sl]
            s = _dot(w, vng) + bt[:, g:g + 1]
            ds = ds_all[:, sl]
            duv_ref[:, sl] = (dy[:, sl] * s * gelu_grad[:, sl]).astype(bf16)
            dw_ref[g] += jnp.where(tril, _dot(ds, vng, _NT), 0.0)
            dbacc = dbacc + jnp.where(lane == g, jnp.sum(ds, axis=1, keepdims=True), 0.0)
            dvn = _dot(w, ds, _TN)
            vh = vhat[:, sl]
            dvg_ref[:, sl] += jnp.sum(dvn * vh, axis=0, keepdims=True)
            dvb_ref[:, sl] += jnp.sum(dvn, axis=0, keepdims=True)
            dvh_parts.append(dvn * vgv[:, sl])
        db_ref[...] += dbacc
        dvhat = jnp.concatenate(dvh_parts, axis=1)
        m1 = jnp.mean(dvhat, axis=1, keepdims=True)
        m2 = jnp.mean(dvhat * vhat, axis=1, keepdims=True)
        dv = rstd * (dvhat - m1 - vhat * m2)
        duv_ref[:, GMLP_WIDTH:] = (dv * gelu_grad[:, GMLP_WIDTH:]).astype(bf16)
        if riding:
            pl.when(pl.program_id(0) == nc - 1)(finish)

    vec = pl.BlockSpec((1, GMLP_WIDTH), lambda c: (0, 0))
    return pl.pallas_call(
        body, name=name, grid=(nc,),
        in_specs=[pl.BlockSpec((CHUNK, 2 * GMLP_WIDTH), lambda c: (c, 0)),
                  pl.BlockSpec((CHUNK, GMLP_WIDTH), lambda c: (c, 0)), vec, vec,
                  pl.BlockSpec((GMLP_GROUPS, CHUNK, CHUNK), lambda c: (0, 0, 0)),
                  pl.BlockSpec((CHUNK, LANES), lambda c: (0, 0))] + [_ANY] * riding,
        out_specs=[pl.BlockSpec((CHUNK, 2 * GMLP_WIDTH), lambda c: (c, 0)),
                   pl.BlockSpec((GMLP_GROUPS, CHUNK, CHUNK), lambda c: (0, 0, 0)),
                   pl.BlockSpec((CHUNK, LANES), lambda c: (0, 0)), vec, vec] + [_ANY] * riding,
        out_shape=[jax.ShapeDtypeStruct((t, 2 * GMLP_WIDTH), bf16),
                   jax.ShapeDtypeStruct((GMLP_GROUPS, CHUNK, CHUNK), f32),
                   jax.ShapeDtypeStruct((CHUNK, LANES), f32),
                   jax.ShapeDtypeStruct((1, GMLP_WIDTH), f32), jax.ShapeDtypeStruct((1, GMLP_WIDTH), f32)]
        + ([jax.ShapeDtypeStruct(ride.shape[:1] + ride.shape[2:], ride.dtype)] if riding else []),
        scratch_shapes=list(_SWAP_SCRATCH) if riding else [],
        compiler_params=_params(("arbitrary",)),
    )(uv, dya, vg, vb, wsp, bsp_t, *([ride] if riding else []))


_CONV_COLS = 512
_XS0, _B0, _C0 = 0, D_INNER, D_INNER + N_GROUPS * D_STATE


_TAIL = 8


def _col_bcast(mat, h):
    return jnp.broadcast_to(mat[:, h:h + 1], (CHUNK, LANES))


def _head_expand(cols):
    lo = _iota((CHUNK, LANES), 1) < HEAD_DIM
    return jnp.concatenate([jnp.where(lo, cols[2 * j], cols[2 * j + 1]) for j in range(N_HEADS // 2)], axis=1)


def _ssd_chunk_scalars(dtr, dtb, alog):
    xdt_pre = dtr + dtb
    dtv = jnp.maximum(xdt_pre, 0.0) + jnp.log(1.0 + jnp.exp(-jnp.abs(xdt_pre)))
    a = -jnp.exp(alog)
    ltri = (_iota((CHUNK, CHUNK), 0) >= _iota((CHUNK, CHUNK), 1)).astype(f32)
    cs = _dot32(ltri, dtv * a)
    csb = [_col_bcast(cs, h) for h in range(N_HEADS)]
    cs_x = _head_expand(csb)
    dt_x = _head_expand([_col_bcast(dtv, h) for h in range(N_HEADS)])
    cl_x = cs_x[CHUNK - 1:CHUNK, :]
    return dict(xdt_pre=xdt_pre, dtv=dtv, a=a, cs=cs, cs_t=cs.T, csb=csb, dt_x=dt_x, e_x=jnp.exp(cs_x),
                dec_x=jnp.exp(cl_x - cs_x), dk_x=jnp.exp(cl_x))


def _head_masks():
    lane = _iota((CHUNK, GROUP_W), 1)
    return [(lane >= r * HEAD_DIM) & (lane < (r + 1) * HEAD_DIM) for r in range(HEADS_PER_GROUP)]


def _stack_heads(a, masks):
    return jnp.concatenate([jnp.where(m, a, 0.0) for m in masks], axis=0).astype(bf16)


def _seg_sum(a, seg):
    hi = a.astype(jnp.bfloat16)
    lo = (a - hi.astype(f32)).astype(jnp.bfloat16)
    return (lax.dot_general(hi, seg, _NN, preferred_element_type=f32)
            + lax.dot_general(lo, seg, _NN, preferred_element_type=f32))


def _head_seg_matrix():
    return (_iota((D_INNER, LANES), 0) // HEAD_DIM == _iota((D_INNER, LANES), 1)).astype(jnp.bfloat16)


def _ssd_fwd(cv, z, dtr, dtb, alog, dsk_x, gs, *, ride=None, name):
    t = cv.shape[0]
    nc = t // CHUNK

    def body(*refs):
        cv_ref, z_ref, dtr_ref, dtb_ref, alog_ref, dsk_ref, gs_ref = refs[:7]
        if ride is None:
            yb_ref, hp_ref, state_ref, xc_ref = refs[7:]
        else:
            ride_ref, yb_ref, hp_ref, got_ref, state_ref, xc_ref, send_sems, recv_sems = refs[7:]
        c = pl.program_id(0)
        if ride is not None:
            start, relay, finish = _gather_protocol(ride_ref, got_ref, send_sems, recv_sems)
            pl.when(c == 0)(start)
            pl.when(c == nc // 2)(relay)

        @pl.when(c == 0)
        def _():
            state_ref[...] = jnp.zeros_like(state_ref)

        for j in range(CONV_DIM // _CONV_COLS):
            sl = slice(j * _CONV_COLS, (j + 1) * _CONV_COLS)
            cvv = cv_ref[:, sl]
            xc_ref[:, sl] = cvv * _sigmoid(cvv)
        sc = _ssd_chunk_scalars(dtr_ref[...], dtb_ref[...], alog_ref[...])
        tril = _iota((CHUNK, CHUNK), 0) >= _iota((CHUNK, CHUNK), 1)
        masks = _head_masks()
        hp_ref[0] = state_ref[...]
        for g in range(N_GROUPS):
            gsl = slice(g * GROUP_W, (g + 1) * GROUP_W)
            xs_g = xc_ref[:, gsl]
            bg = xc_ref[:, _B0 + g * D_STATE:_B0 + (g + 1) * D_STATE]
            cg = xc_ref[:, _C0 + g * D_STATE:_C0 + (g + 1) * D_STATE]
            xdt_g = xs_g * sc["dt_x"][:, gsl]
            cbm = _dot(cg, bg, _NT)
            mw = jnp.concatenate(
                [cbm * jnp.exp(jnp.where(tril, sc["csb"][h] - sc["cs_t"][h:h + 1, :], -1e30))
                 for h in range(g * HEADS_PER_GROUP, (g + 1) * HEADS_PER_GROUP)], axis=1)
            ht_g = state_ref[:, gsl]
            y_g = _dot(mw, _stack_heads(xdt_g, masks)) + sc["e_x"][:, gsl] * _dot(cg, ht_g) + dsk_ref[:, gsl] * xs_g
            state_ref[:, gsl] = ht_g * sc["dk_x"][:, gsl] + _dot(bg, xdt_g * sc["dec_x"][:, gsl], _TN)
            zg = z_ref[:, gsl]
            yg = y_g * zg * _sigmoid(zg)
            rs = lax.rsqrt(jnp.mean(yg * yg, axis=1, keepdims=True) + NORM_EPS)
            yb_ref[:, gsl] = (yg * rs * gs_ref[:, gsl]).astype(bf16)
        if ride is not None:
            pl.when(c == nc - 1)(finish)

    def chunk(w):
        return pl.BlockSpec((CHUNK, w), lambda c: (c, 0))

    def const(shape):
        return pl.BlockSpec(shape, lambda c: (0,) * len(shape))

    riding = ride is not None
    return pl.pallas_call(
        body, name=name, grid=(nc,),
        in_specs=[chunk(CONV_DIM), chunk(D_INNER), chunk(LANES),
                  const((1, LANES)), const((1, LANES)), const((1, D_INNER)), const((1, D_INNER))] + [_ANY] * riding,
        out_specs=[chunk(D_INNER), pl.BlockSpec((1, D_STATE, D_INNER), lambda c: (c, 0, 0))] + [_ANY] * riding,
        out_shape=[jax.ShapeDtypeStruct((t, D_INNER), bf16), jax.ShapeDtypeStruct((nc, D_STATE, D_INNER), f32)]
        + ([jax.ShapeDtypeStruct((N_CHIPS,) + ride.shape, ride.dtype)] if riding else []),
        scratch_shapes=[pltpu.VMEM((D_STATE, D_INNER), f32), pltpu.VMEM((CHUNK, CONV_DIM), f32)]
        + (list(_GATHER_SCRATCH) if riding else []),
        compiler_params=_params(("arbitrary",)),
    )(cv, z, dtr, dtb, alog, dsk_x, gs, *([ride] if riding else []))


def _proj_conv(h, w, cw, cb, *, tm, tn, name):
    t, k_dim = h.shape
    c_dim = w.shape[1]
    cols = 256

    def body(h_ref, w_ref, cw_ref, cb_ref, x_ref, cv_ref, carry_ref):
        i = pl.program_id(1)
        row8 = _iota((_TAIL, cols), 0)
        def matmul_block(j):
            sl = slice(j * cols, (j + 1) * cols)
            x_ref[:, sl] = lax.dot_general(h_ref[...], w_ref[:, sl], _NN, preferred_element_type=f32)

        matmul_block(0)
        for j in range(tn // cols):
            sl = slice(j * cols, (j + 1) * cols)
            if j + 1 < tn // cols:
                matmul_block(j + 1)
            for r in range(tm // CHUNK):
                rows = slice(r * CHUNK, (r + 1) * CHUNK)
                cur = x_ref[rows, sl]
                if r == 0:
                    tail = jnp.where(i > 0, carry_ref[:, sl], 0.0)
                else:
                    tail = x_ref[r * CHUNK - _TAIL:r * CHUNK, sl]
                acc = cur * cw_ref[CONV_W - 1:CONV_W, sl] + cb_ref[:, sl]
                for s in range(1, CONV_W):
                    rolled = pltpu.roll(cur, s, 0)
                    top = jnp.where(row8 >= s, rolled[:_TAIL], pltpu.roll(tail, s, 0))
                    acc = acc + jnp.concatenate([top, rolled[_TAIL:]], axis=0) * cw_ref[CONV_W - 1 - s:CONV_W - s, sl]
                cv_ref[rows, sl] = acc
            carry_ref[:, sl] = x_ref[tm - _TAIL:tm, sl]

    tile = pl.BlockSpec((tm, tn), lambda j, i: (i, j))
    return pl.pallas_call(
        body, name=name, grid=(c_dim // tn, t // tm),
        in_specs=[pl.BlockSpec((tm, k_dim), lambda j, i: (i, 0)), pl.BlockSpec((k_dim, tn), lambda j, i: (0, j)),
                  pl.BlockSpec((CONV_W, tn), lambda j, i: (0, j)), pl.BlockSpec((1, tn), lambda j, i: (0, j))],
        out_specs=[tile, tile],
        out_shape=[jax.ShapeDtypeStruct((t, c_dim), f32), jax.ShapeDtypeStruct((t, c_dim), f32)],
        scratch_shapes=[pltpu.VMEM((_TAIL, tn), f32)],
        compiler_params=_params(("parallel", "arbitrary")),
    )(h, w, cw, cb)


def _ssd_bwd(cv, z, dtr, hprev, dyb, dtb, alog, dsk_x, gs, seg, *, ride=None, name):
    t = cv.shape[0]
    nc = t // CHUNK

    def body(*refs):
        cv_ref, z_ref, dtr_ref, hp_ref, dyb_ref, dtb_ref, alog_ref, dsk_ref, gs_ref, seg_ref = refs[:10]
        rest = refs[10:]
        if ride is not None:
            ride_ref, got_ref, send_sems, recv_sems = rest[0], rest[8], rest[-2], rest[-1]
            rest = rest[1:8] + rest[9:-2]
        (dz_ref, dcv_ref, ddt_ref, ddtb_ref, dalog_ref, ddsk_ref, dgs_ref,
         dh_ref, xc_ref, dxc_ref, x13_ref, x2_ref, rows_ref) = rest
        i = pl.program_id(0)
        if ride is not None:
            start, finish = _scatter_protocol(ride_ref, got_ref, send_sems, recv_sems)
            pl.when(i == 0)(start)

        @pl.when(i == 0)
        def _():
            for ref in (dh_ref, ddtb_ref, dalog_ref, ddsk_ref, dgs_ref, rows_ref):
                ref[...] = jnp.zeros_like(ref)

        for j in range(CONV_DIM // _CONV_COLS):
            sl = slice(j * _CONV_COLS, (j + 1) * _CONV_COLS)
            cvv = cv_ref[:, sl]
            xc_ref[:, sl] = cvv * _sigmoid(cvv)
        sc = _ssd_chunk_scalars(dtr_ref[...], dtb_ref[...], alog_ref[...])
        tril = _iota((CHUNK, CHUNK), 0) >= _iota((CHUNK, CHUNK), 1)
        triu = _iota((CHUNK, CHUNK), 0) <= _iota((CHUNK, CHUNK), 1)
        masks = _head_masks()
        rowh = _iota((N_HEADS, CHUNK), 0)
        dcs_t = jnp.zeros((N_HEADS, CHUNK), f32)
        for g in range(N_GROUPS):
            gsl = slice(g * GROUP_W, (g + 1) * GROUP_W)
            xs_g = xc_ref[:, gsl]
            bg = xc_ref[:, _B0 + g * D_STATE:_B0 + (g + 1) * D_STATE]
            cg = xc_ref[:, _C0 + g * D_STATE:_C0 + (g + 1) * D_STATE]
            dt_g, e_g, dec_g, dk_g = sc["dt_x"][:, gsl], sc["e_x"][:, gsl], sc["dec_x"][:, gsl], sc["dk_x"][:, gsl]
            dsk_g = dsk_ref[:, gsl]
            xdt_g = xs_g * dt_g
            xdt_stack = _stack_heads(xdt_g, masks)
            cbm = _dot(cg, bg, _NT)
            cbt = _dot(bg, cg, _NT)
            heads = range(g * HEADS_PER_GROUP, (g + 1) * HEADS_PER_GROUP)
            lmats = [jnp.exp(jnp.where(tril, sc["csb"][h] - sc["cs_t"][h:h + 1, :], -1e30)) for h in heads]
            mw = jnp.concatenate([cbm * lm for lm in lmats], axis=1)
            mtw = jnp.concatenate(
                [cbt * jnp.exp(jnp.where(triu, sc["cs_t"][h:h + 1, :] - sc["csb"][h], -1e30)) for h in heads], axis=1)
            ht_g = hp_ref[0, :, gsl]
            dhn_g = dh_ref[:, gsl]
            yoff = e_g * _dot(cg, ht_g)
            y_g = _dot(mw, xdt_stack) + yoff + dsk_g * xs_g
            zg = z_ref[:, gsl]
            sz = _sigmoid(zg)
            silu = zg * sz
            yg = y_g * silu
            rs = lax.rsqrt(jnp.mean(yg * yg, axis=1, keepdims=True) + NORM_EPS)
            yn = yg * rs
            dyb = dyb_ref[:, gsl]
            dgs_ref[:, gsl] += jnp.sum(dyb * yn, axis=0, keepdims=True)
            dyn = dyb * gs_ref[:, gsl]
            dyg = rs * (dyn - yn * jnp.mean(dyn * yn, axis=1, keepdims=True))
            dy_g = dyg * silu
            dz_ref[:, gsl] = (dyg * y_g * (sz * (1.0 + zg * (1.0 - sz)))).astype(bf16)
            dy_stack = _stack_heads(dy_g, masks)
            dm_w = _dot(dy_g, xdt_stack, _NT)
            dmt_w = _dot(xdt_g, dy_stack, _NT)
            dxdt = _dot(mtw, dy_stack)
            dcb_acc = jnp.zeros((CHUNK, CHUNK), f32)
            for r, h in enumerate(heads):
                hs = slice(r * CHUNK, (r + 1) * CHUNK)
                dml = dm_w[:, hs] * lmats[r]
                dcb_acc = dcb_acc + dml
                col = jnp.sum(dml * cbm, axis=0, keepdims=True)
                row = jnp.sum(dmt_w[:, hs] * mtw[:, hs], axis=0, keepdims=True)
                dcs_t = dcs_t + jnp.where(rowh == h, row - col, 0.0)
            w = _dot(bg, dhn_g)
            dxdt = dxdt + dec_g * w
            decx3 = dec_g * (xdt_g * w)
            dg_g = e_g * dy_g
            d_c = _dot(dg_g, ht_g, _NT) + _dot(dcb_acc, bg)
            d_b = _dot(dcb_acc, cg, _TN) + _dot(xdt_g * dec_g, dhn_g, _NT)
            dh_ref[:, gsl] = dhn_g * dk_g + _dot(cg, dg_g, _TN)
            dxc_ref[:, gsl] = dsk_g * dy_g + dxdt * dt_g
            dxc_ref[:, _B0 + g * D_STATE:_B0 + (g + 1) * D_STATE] = d_b
            dxc_ref[:, _C0 + g * D_STATE:_C0 + (g + 1) * D_STATE] = d_c
            x13_ref[:, gsl] = dy_g * yoff - decx3
            x2_ref[:, gsl] = dxdt * xs_g
            rows_ref[0:1, gsl] = jnp.sum(dhn_g * ht_g, axis=0, keepdims=True)
            rows_ref[1:2, gsl] = jnp.sum(decx3, axis=0, keepdims=True)
            rows_ref[2:3, gsl] = jnp.sum(dy_g * xs_g, axis=0, keepdims=True)
        segm = seg_ref[...]
        r13 = _seg_sum(x13_ref[...], segm)
        r2 = _seg_sum(x2_ref[...], segm)
        small = _seg_sum(rows_ref[...], segm)
        lane = _iota((CHUNK, LANES), 1)
        rowi = _iota((CHUNK, LANES), 0)
        dcl_row = small[0:1, :] * jnp.exp(sc["cs"][CHUNK - 1:CHUNK, :]) + small[1:2, :]
        dcs = r13 + jnp.where(rowi == CHUNK - 1, dcl_row, 0.0)
        dcs_t_all = dcs.T + jnp.concatenate([dcs_t, jnp.zeros((LANES - N_HEADS, CHUNK), f32)], axis=0)
        dda = _dot32(dcs_t_all, tril.astype(f32)).T
        a = sc["a"]
        ddt_total = r2 + dda * a
        dalog_ref[...] += jnp.sum(dda * sc["dtv"], axis=0, keepdims=True) * a
        ddtr = jnp.where(lane < N_HEADS, ddt_total * _sigmoid(sc["xdt_pre"]), 0.0)
        ddtb_ref[...] += jnp.sum(ddtr, axis=0, keepdims=True)
        ddt_ref[...] = ddtr.astype(bf16)
        ddsk_ref[...] += small[2:3, :]
        for j in range(CONV_DIM // _CONV_COLS):
            sl = slice(j * _CONV_COLS, (j + 1) * _CONV_COLS)
            cvv = cv_ref[:, sl]
            sg = _sigmoid(cvv)
            dcv_ref[:, sl] = dxc_ref[:, sl] * (sg * (1.0 + cvv * (1.0 - sg)))
        if ride is not None:
            pl.when(i == nc - 1)(finish)

    def chunk(w):
        return pl.BlockSpec((CHUNK, w), lambda i: (nc - 1 - i, 0))

    def const(shape):
        return pl.BlockSpec(shape, lambda i: (0,) * len(shape))

    riding = ride is not None
    return pl.pallas_call(
        body, name=name, grid=(nc,),
        in_specs=[chunk(CONV_DIM), chunk(D_INNER), chunk(LANES),
                  pl.BlockSpec((1, D_STATE, D_INNER), lambda i: (nc - 1 - i, 0, 0)), chunk(D_INNER),
                  const((1, LANES)), const((1, LANES)), const((1, D_INNER)), const((1, D_INNER)),
                  const((D_INNER, LANES))] + [_ANY] * riding,
        out_specs=[chunk(D_INNER), chunk(CONV_DIM), chunk(LANES),
                   const((1, LANES)), const((1, LANES)), const((1, LANES)), const((1, D_INNER))] + [_ANY] * riding,
        out_shape=[jax.ShapeDtypeStruct((t, D_INNER), bf16), jax.ShapeDtypeStruct((t, CONV_DIM), f32),
                   jax.ShapeDtypeStruct((t, LANES), bf16), jax.ShapeDtypeStruct((1, LANES), f32),
                   jax.ShapeDtypeStruct((1, LANES), f32), jax.ShapeDtypeStruct((1, LANES), f32),
                   jax.ShapeDtypeStruct((1, D_INNER), f32)]
        + ([jax.ShapeDtypeStruct((N_CHIPS - 1,) + ride.shape[1:], ride.dtype)] if riding else []),
        scratch_shapes=[pltpu.VMEM((D_STATE, D_INNER), f32), pltpu.VMEM((CHUNK, CONV_DIM), f32),
                        pltpu.VMEM((CHUNK, CONV_DIM), f32), pltpu.VMEM((CHUNK, D_INNER), f32),
                        pltpu.VMEM((CHUNK, D_INNER), f32), pltpu.VMEM((_TAIL, D_INNER), f32)]
        + (list(_SCATTER_SCRATCH) if riding else []),
        compiler_params=_params(("arbitrary",)),
    )(cv, z, dtr, hprev, dyb, dtb, alog, dsk_x, gs, seg, *([ride] if riding else []))


def _conv_bwd_dw(h, dcv, xbc, cw, *, tt, tn, name):
    t, d = h.shape
    c_dim = dcv.shape[1]
    nk = t // tt
    heads = tt // _TAIL

    def body(h_ref, dc_ref, nxt_ref, x_ref, w_ref, dw_ref, dxbc_ref, dcw_ref, dcb_ref):
        k = pl.program_id(1)
        cols = 256
        row8 = _iota((_TAIL, cols), 0)

        @pl.when(k == 0)
        def _():
            dw_ref[...] = jnp.zeros_like(dw_ref)
            dcw_ref[...] = jnp.zeros_like(dcw_ref)
            dcb_ref[...] = jnp.zeros_like(dcb_ref)

        for j in range(tn // cols):
            sl = slice(j * cols, (j + 1) * cols)
            taps = [jnp.zeros((1, cols), f32) for _ in range(CONV_W)]
            bsum = jnp.zeros((1, cols), f32)
            for r in range(tt // CHUNK):
                rows = slice(r * CHUNK, (r + 1) * CHUNK)
                dconv = dc_ref[rows, sl]
                if r + 1 < tt // CHUNK:
                    nxt = dc_ref[(r + 1) * CHUNK:(r + 1) * CHUNK + _TAIL, sl]
                else:
                    nxt = jnp.where(k < nk - 1, nxt_ref[:, sl], 0.0)
                cur = x_ref[rows, sl]
                dxin = dconv * w_ref[CONV_W - 1:CONV_W, sl]
                taps[CONV_W - 1] = taps[CONV_W - 1] + jnp.sum(dconv * cur, axis=0, keepdims=True)
                for s in range(1, CONV_W):
                    rolled = pltpu.roll(dconv, CHUNK - s, 0)
                    bot = jnp.where(row8 < _TAIL - s, rolled[CHUNK - _TAIL:], pltpu.roll(nxt, _TAIL - s, 0))
                    up = jnp.concatenate([rolled[:CHUNK - _TAIL], bot], axis=0)
                    dxin = dxin + up * w_ref[CONV_W - 1 - s:CONV_W - s, sl]
                    taps[CONV_W - 1 - s] = taps[CONV_W - 1 - s] + jnp.sum(up * cur, axis=0, keepdims=True)
                bsum = bsum + jnp.sum(dconv, axis=0, keepdims=True)
                dxbc_ref[rows, sl] = dxin.astype(bf16)
            dcw_ref[:, sl] += jnp.concatenate(taps, axis=0)
            dcb_ref[:, sl] += bsum
            dw_ref[:, sl] += lax.dot_general(h_ref[...], dxbc_ref[:, sl], _TN, preferred_element_type=f32)

    tile = pl.BlockSpec((tt, tn), lambda j, k: (k, j))
    last_head = t // _TAIL - 1
    return pl.pallas_call(
        body, name=name, grid=(c_dim // tn, nk),
        in_specs=[pl.BlockSpec((tt, d), lambda j, k: (k, 0)), tile,
                  pl.BlockSpec((_TAIL, tn), lambda j, k: (jnp.minimum((k + 1) * heads, last_head), j)), tile,
                  pl.BlockSpec((CONV_W, tn), lambda j, k: (0, j))],
        out_specs=[pl.BlockSpec((d, tn), lambda j, k: (0, j)), tile, pl.BlockSpec((CONV_W, tn), lambda j, k: (0, j)),
                   pl.BlockSpec((1, tn), lambda j, k: (0, j))],
        out_shape=[jax.ShapeDtypeStruct((d, c_dim), f32), jax.ShapeDtypeStruct((t, c_dim), bf16),
                   jax.ShapeDtypeStruct((CONV_W, c_dim), f32), jax.ShapeDtypeStruct((1, c_dim), f32)],
        compiler_params=_params(("parallel", "arbitrary")),
    )(h, dcv, dcv, xbc, cw)


def _adamw(w, g, m, v, *, name):
    r, c = w.shape
    tr = r
    while tr * c * 4 > _MB and tr % 16 == 0:
        tr //= 2

    def body(w_ref, g_ref, m_ref, v_ref, d_ref, m2_ref, v2_ref):
        gv = g_ref[...]
        m2 = ADAM_B1 * m_ref[...] + (1.0 - ADAM_B1) * gv
        v2 = ADAM_B2 * v_ref[...] + (1.0 - ADAM_B2) * (gv * gv)
        m_hat = m2 / (1.0 - ADAM_B1 ** ADAM_STEP)
        v_hat = v2 / (1.0 - ADAM_B2 ** ADAM_STEP)
        d_ref[...] = -ADAM_LR * (m_hat / (jnp.sqrt(v_hat) + ADAM_EPS) + ADAM_WD * w_ref[...])
        m2_ref[...] = m2
        v2_ref[...] = v2

    blk = pl.BlockSpec((tr, c), lambda i: (i, 0))
    return pl.pallas_call(
        body, name=name, grid=(r // tr,),
        in_specs=[blk] * 4, out_specs=[blk] * 3,
        out_shape=[jax.ShapeDtypeStruct((r, c), f32)] * 3,
        compiler_params=_params(("parallel",)),
    )(w, g, m, v)


def _row_block(rows, cols):
    cap = max(16, _MB // (4 * cols))
    return max(tr for tr in range(16, min(cap, rows) + 1, 16) if rows % tr == 0)


def _cast_bf16(a, *, name):
    r, c = a.shape
    tr = _row_block(r, c)

    def body(a_ref, o_ref):
        o_ref[...] = a_ref[...].astype(bf16)

    blk = pl.BlockSpec((tr, c), lambda i: (i, 0))
    return pl.pallas_call(
        body, name=name, grid=(r // tr,), in_specs=[blk], out_specs=blk,
        out_shape=jax.ShapeDtypeStruct((r, c), bf16), compiler_params=_params(("parallel",)),
    )(a)


_ANY = pl.BlockSpec(memory_space=pl.ANY)


def _place():
    x, y, c = lax.axis_index("x"), lax.axis_index("y"), lax.axis_index("c")
    other_chips = [(1 - x, y), (x, 1 - y), (1 - x, 1 - y)]
    return x, y, c, other_chips


def _gather_protocol(in_ref, out_ref, send_sems, recv_sems):
    x, y, c, chips = _place()
    me = 2 * x + y
    sibling = (x, y, 1 - c)

    def cp(k, chip, half, to, src=None):
        dst = out_ref.at[chip, half]
        return pltpu.make_async_remote_copy(
            src_ref=dst if src is None else src, dst_ref=dst, send_sem=send_sems.at[k], recv_sem=recv_sems.at[k],
            device_id=to, device_id_type=MESH)

    def sends():
        return [cp(j, me, c, (cx, cy, c), src=in_ref.at[c]) for j, (cx, cy) in enumerate(chips)]

    def relays():
        return [cp(3 + j, 2 * cx + cy, c, sibling) for j, (cx, cy) in enumerate(chips)]

    def start():
        for f in sends():
            f.start()

    def relay():
        onward = relays()
        for j, (cx, cy) in enumerate(chips):
            cp(j, 2 * cx + cy, c, sibling).wait_recv()
            onward[j].start()

    def finish():
        for j, (cx, cy) in enumerate(chips):
            cp(3 + j, 2 * cx + cy, 1 - c, sibling).wait_recv()
        for f in sends() + relays():
            f.wait_send()

    return start, relay, finish


_GATHER_SCRATCH = [pltpu.SemaphoreType.DMA((6,)), pltpu.SemaphoreType.DMA((6,))]


def _gather_shards(shard, *, name):
    _, rh, lanes = shard.shape

    def body(in_ref, out_ref, send_sems, recv_sems):
        start, relay, finish = _gather_protocol(in_ref, out_ref, send_sems, recv_sems)
        start()
        relay()
        finish()

    return pl.pallas_call(
        body, name=name, in_specs=[_ANY], out_specs=_ANY,
        out_shape=jax.ShapeDtypeStruct((N_CHIPS, 2, rh, lanes), shard.dtype),
        scratch_shapes=list(_GATHER_SCRATCH),
    )(shard)


def _scatter_protocol(p_ref, out_ref, send_sems, recv_sems):
    x, y, c, chips = _place()

    def copies():
        return [pltpu.make_async_remote_copy(
            src_ref=p_ref.at[2 * cx + cy], dst_ref=out_ref.at[j], send_sem=send_sems.at[j], recv_sem=recv_sems.at[j],
            device_id=(cx, cy, c), device_id_type=MESH) for j, (cx, cy) in enumerate(chips)]

    def start():
        for cpy in copies():
            cpy.start()

    def finish():
        for cpy in copies():
            cpy.wait()

    return start, finish


_SCATTER_SCRATCH = [pltpu.SemaphoreType.DMA((3,)), pltpu.SemaphoreType.DMA((3,))]


def _swap_protocol(g_ref, out_ref, send_sems, recv_sems):
    x, y, c, _ = _place()

    def copies():
        return [pltpu.make_async_remote_copy(
            src_ref=g_ref.at[k, 1 - c], dst_ref=out_ref.at[k], send_sem=send_sems.at[k], recv_sem=recv_sems.at[k],
            device_id=(x, y, 1 - c), device_id_type=MESH) for k in range(N_CHIPS)]

    def start():
        for cpy in copies():
            cpy.start()

    def finish():
        for cpy in copies():
            cpy.wait()

    return start, finish


_SWAP_SCRATCH = [pltpu.SemaphoreType.DMA((N_CHIPS,)), pltpu.SemaphoreType.DMA((N_CHIPS,))]


def _rs_swap_halves(g, *, name):
    nch, _, rh, lanes = g.shape

    def body(g_ref, out_ref, send_sems, recv_sems):
        start, finish = _swap_protocol(g_ref, out_ref, send_sems, recv_sems)
        start()
        finish()

    return pl.pallas_call(
        body, name=name, in_specs=[_ANY], out_specs=_ANY,
        out_shape=jax.ShapeDtypeStruct((nch, rh, lanes), g.dtype),
        scratch_shapes=list(_SWAP_SCRATCH),
    )(g)


def _rs_add_pair(g, got, c_idx, *, name):
    nch, _, rh, lanes = g.shape
    tr = _row_block(rh, lanes)

    def body(c_ref, g_ref, got_ref, p16_ref):
        p16_ref[...] = (g_ref[...] + got_ref[...]).astype(bf16)

    blk = pl.BlockSpec((None, tr, lanes), lambda k, i, c_ref: (k, i, 0))
    return pl.pallas_call(
        body, name=name,
        grid_spec=pltpu.PrefetchScalarGridSpec(
            num_scalar_prefetch=1, grid=(nch, rh // tr),
            in_specs=[pl.BlockSpec((None, None, tr, lanes), lambda k, i, c_ref: (k, c_ref[0], i, 0)), blk],
            out_specs=blk),
        out_shape=jax.ShapeDtypeStruct((nch, rh, lanes), bf16),
        compiler_params=_params(("parallel", "parallel")),
    )(c_idx, g, got)


def _rs_add_chips(g, got_pair, got, place, *, name):
    _, _, rh, lanes = g.shape
    tr = _row_block(rh, lanes)

    def body(place_ref, g_ref, pair_ref, got_ref, o_ref):
        own = g_ref[...] + pair_ref[...]
        o_ref[...] = ((own + got_ref[0].astype(f32)) + got_ref[1].astype(f32)) + got_ref[2].astype(f32)

    return pl.pallas_call(
        body, name=name,
        grid_spec=pltpu.PrefetchScalarGridSpec(
            num_scalar_prefetch=1, grid=(rh // tr,),
            in_specs=[pl.BlockSpec((None, None, tr, lanes), lambda i, place_ref: (place_ref[0], place_ref[1], i, 0)),
                      pl.BlockSpec((None, tr, lanes), lambda i, place_ref: (place_ref[0], i, 0)),
                      pl.BlockSpec((3, tr, lanes), lambda i, place_ref: (0, i, 0))],
            out_specs=pl.BlockSpec((None, tr, lanes), lambda i, place_ref: (place_ref[1], i, 0))),
        out_shape=jax.ShapeDtypeStruct((2, rh, lanes), f32),
        compiler_params=_params(("parallel",)),
    )(place, g, got_pair, got)


def _rs_join_halves(halves, *, name):
    def body(h_ref, out_ref, send_sem, recv_sem):
        x, y, c, _ = _place()
        cpy = pltpu.make_async_remote_copy(
            src_ref=h_ref.at[c], dst_ref=out_ref.at[c], send_sem=send_sem, recv_sem=recv_sem,
            device_id=(x, y, 1 - c), device_id_type=MESH)
        cpy.start()
        cpy.wait()

    return pl.pallas_call(
        body, name=name, in_specs=[_ANY], out_specs=_ANY,
        out_shape=jax.ShapeDtypeStruct(halves.shape, halves.dtype), input_output_aliases={0: 0},
        scratch_shapes=[pltpu.SemaphoreType.DMA, pltpu.SemaphoreType.DMA],
    )(halves)


def _all_reduce_small(s, *, name):
    rs, lanes = s.shape
    rh = rs // 2

    def body(s_ref, o_ref, sib_ref, mine_ref, chips_ref, send_sems, recv_sems):
        x, y, c, chips = _place()
        me = 2 * x + y
        sibling = (x, y, 1 - c)
        rows = pl.ds(pl.multiple_of(c * rh, 8), rh)

        def cp(k, src, dst, to):
            return pltpu.make_async_remote_copy(src_ref=src, dst_ref=dst, send_sem=send_sems.at[k],
                                                recv_sem=recv_sems.at[k], device_id=to, device_id_type=MESH)

        swap = cp(0, s_ref, sib_ref, sibling)
        swap.start()
        swap.wait()
        mine_ref[...] = s_ref[rows, :] + sib_ref[rows, :]
        sends = [cp(1 + j, mine_ref, chips_ref.at[j], (cx, cy, c)) for j, (cx, cy) in enumerate(chips)]
        for cpy in sends:
            cpy.start()
        for cpy in sends:
            cpy.wait()
        where = [2 * cx + cy for cx, cy in chips]
        total = None
        for q in range(N_CHIPS):
            term = jnp.where(q == me, mine_ref[...], jnp.where(
                q == where[0], chips_ref[0], jnp.where(q == where[1], chips_ref[1], chips_ref[2])))
            total = term if total is None else total + term
        o_ref[rows, :] = total
        push = cp(4, o_ref.at[rows, :], o_ref.at[rows, :], sibling)
        push.start()
        push.wait()

    vm = pl.BlockSpec(memory_space=pltpu.VMEM)
    return pl.pallas_call(
        body, name=name, in_specs=[vm], out_specs=vm,
        out_shape=jax.ShapeDtypeStruct((rs, lanes), f32),
        scratch_shapes=[pltpu.VMEM((rs, lanes), f32), pltpu.VMEM((rh, lanes), f32),
                        pltpu.VMEM((N_CHIPS - 1, rh, lanes), f32), pltpu.SemaphoreType.DMA((5,)),
                        pltpu.SemaphoreType.DMA((5,))],
        compiler_params=pltpu.CompilerParams(vmem_limit_bytes=32 * _MB),
    )(s)


def _pad_lanes(a, width=LANES):
    return jnp.pad(a, ((0, 0), (0, width - a.shape[1])))


def _local_grads(x, tgt, wts, small, *, fwd_ride=None, late_weights=None, swap_ride=None, bwd_ride=None,
                 last_ride=None):
    t = x.shape[0]
    tm = min(t, 1024)
    d = D_MODEL
    mm = functools.partial(_matmul, tm=tm)

    dtb = _pad_lanes(small["dt_bias"])
    alog = _pad_lanes(small["a_log"])
    dsk = jnp.repeat(small["d_skip"], HEAD_DIM, axis=1)
    bsp_t = _pad_lanes(small["b_spatial"].T)
    wsp = small["w_spatial"]

    h = _rms_fwd(x, small["norm_mix_g"], name="rms_mix")
    uv = mm(h, wts["uv"], tn=1024, tk=d, out_dtypes=[f32], name="proj_uv")
    z = mm(h, wts["z"], tn=1024, tk=d, out_dtypes=[f32], name="proj_z")
    xbc, cv = _proj_conv(h, wts["xbc"], small["conv_w"], small["conv_b"], tm=tm, tn=1024, name="proj_xbc")
    dtr = mm(h, wts["dt"], tn=LANES, tk=d, out_dtypes=[f32], name="proj_dt")
    gl = mm(h, wts["gate"], tn=1024, tk=d, out_dtypes=[f32], name="proj_gate")
    ya = _gmlp_fwd(uv, small["v_norm_g"], small["v_norm_b"], wsp, bsp_t, name="gmlp_fwd")
    yb, hprev, *gathered = _ssd_fwd(cv, z, dtr, dtb, alog, dsk, small["ssm_norm_g"], ride=fwd_ride, name="ssd_fwd")
    if fwd_ride is not None:
        wts = {**wts, **late_weights(gathered[0])}
    pa = mm(ya, wts["pa"], tn=1024, tk=1024, out_dtypes=[f32], name="proj_a")
    pb = mm(yb, wts["pb"], tn=1024, tk=1024, out_dtypes=[f32], name="proj_b")
    merged = _merge_fwd(pa, pb, gl, small["b_gates"], name="merge_fwd")
    x1 = mm(merged, wts["out"], tn=1024, tk=1024, out_dtypes=[f32], extras=[x],
            epilogue=lambda acc, res: (res + acc,), name="out_proj")
    h2 = _rms_fwd(x1, small["norm_mlp_g"], name="rms_mlp")
    act = mm(h2, wts["up"], tn=1024, tk=d, out_dtypes=[bf16],
             epilogue=lambda acc: (jnp.square(jnp.maximum(acc, 0.0)),), name="mlp_up")
    x2 = mm(act, wts["down"], tn=1024, tk=2048, out_dtypes=[f32], extras=[x1],
            epilogue=lambda acc, res: (res + acc,), name="mlp_down")

    dx2, dx2b, dgf, loss = _loss_head(x2, tgt, small["norm_final_g"], name="loss_head")
    tt = min(t, 2048)
    tn_mm = functools.partial(_matmul_tn, tt=tt)
    dw = {}
    dw["down"] = tn_mm(act, dx2b, tka=1024, tn=1024, name="dw_down")
    dup = mm(dx2b, wts["down"], nt=True, tn=1024, tk=1024, out_dtypes=[bf16], extras=[act],
             epilogue=lambda acc, a2: (acc * (2.0 * jnp.sqrt(a2).astype(f32)),), name="d_act")
    dw["up"] = tn_mm(h2, dup, tka=1024, tn=1024, name="dw_up")
    dh2 = mm(dup, wts["up"], nt=True, tn=1024, tk=2048, out_dtypes=[f32], name="d_h2")
    dx1, dx1b, dg_mlp = _rms_bwd(x1, small["norm_mlp_g"], dh2, dx2, want_bf16=True, name="rms_mlp_bwd")
    dw["out"] = tn_mm(merged, dx1b, tka=1024, tn=1024, name="dw_out")
    dmerged = mm(dx1b, wts["out"], nt=True, tn=1024, tk=1024, out_dtypes=[f32], name="d_merged")
    dpa, dpb, dgl, dbg = _merge_bwd(dmerged, pa, pb, gl, small["b_gates"], name="merge_bwd")
    dw["pa"] = tn_mm(ya, dpa, tka=1024, tn=1024, name="dw_pa")
    dw["pb"] = tn_mm(yb, dpb, tka=1024, tn=1024, name="dw_pb")
    dya = mm(dpa, wts["pa"], nt=True, tn=1024, tk=1024, out_dtypes=[f32], name="d_ya")
    dyb = mm(dpb, wts["pb"], nt=True, tn=1024, tk=1024, out_dtypes=[f32], name="d_yb")
    swapped = swap_ride(dw) if swap_ride is not None else None
    duv, dwsp, dbsp_t, dvg, dvb, *got_pair = _gmlp_bwd(uv, dya, small["v_norm_g"], small["v_norm_b"], wsp, bsp_t,
                                                       ride=swapped, name="gmlp_bwd")
    ride = bwd_ride(swapped, got_pair[0]) if bwd_ride is not None else None
    dz, dcv, ddt, ddtb, dalog, ddsk, dgs, *got = _ssd_bwd(
        cv, z, dtr, hprev, dyb, dtb, alog, dsk, small["ssm_norm_g"], _head_seg_matrix(), ride=ride, name="ssd_bwd")
    dw["uv"] = tn_mm(h, duv, tka=1024, tn=1024, name="dw_uv")
    dw["z"] = tn_mm(h, dz, tka=1024, tn=1024, name="dw_z")
    dw["xbc"], dxbc, dcw, dcb = _conv_bwd_dw(h, dcv, xbc, small["conv_w"], tt=min(t, 1024), tn=1024, name="dw_xbc")
    dw["dt"] = tn_mm(h, ddt, tka=1024, tn=LANES, name="dw_dt")
    dw["gate"] = tn_mm(h, dgl, tka=1024, tn=1024, name="dw_gate")
    last = last_ride(dw) if last_ride is not None else None
    res = _matmul_nt_sum(
        [(duv, wts["uv"]), (dz, wts["z"]), (dxbc, wts["xbc"]), (dgl, wts["gate"]), (ddt, wts["dt"])],
        tm=tm, tks=[1024] * 4 + [LANES], ride=last, name="d_h")
    dh, got_last = (res[0], res[1]) if last is not None else (res, None)
    dx, dg_mix = _rms_bwd(x, small["norm_mix_g"], dh, dx1, want_bf16=False, name="rms_mix_bwd")

    dsmall = {
        "norm_mix_g": dg_mix, "conv_w": dcw, "conv_b": dcb, "dt_bias": ddtb[:, :N_HEADS], "a_log": dalog[:, :N_HEADS],
        "d_skip": ddsk[:, :N_HEADS], "ssm_norm_g": dgs, "v_norm_g": dvg, "v_norm_b": dvb, "w_spatial": dwsp,
        "b_spatial": dbsp_t[:, :GMLP_GROUPS].T, "b_gates": dbg, "norm_mlp_g": dg_mlp, "norm_final_g": dgf,
    }
    return loss, dx, dw, dsmall, (got[0] if got else None), got_last


_IN_SHARD = IN_PROJ // N_CHIPS
_LATE = ("w_proj_a", "w_proj_b", "w_out", "w_mlp_up", "w_mlp_down")
_LATE_ROWS = {"w_proj_a": GMLP_WIDTH // N_CHIPS, "w_proj_b": D_INNER // N_CHIPS, "w_out": D_MODEL // N_CHIPS,
              "w_mlp_up": D_MODEL, "w_mlp_down": D_FF // N_CHIPS}
_LATE_TOTAL = sum(_LATE_ROWS.values())


def _late_offsets():
    off, out = 0, {}
    for k in _LATE:
        out[k] = off
        off += _LATE_ROWS[k]
    return out


_LATE_OFF = _late_offsets()

_SMALL = ("norm_mix_g", "conv_w", "conv_b", "dt_bias", "a_log", "d_skip", "ssm_norm_g", "v_norm_g", "v_norm_b",
          "w_spatial", "b_spatial", "b_gates", "norm_mlp_g", "norm_final_g")


def _pack_small(parts):
    flat = jnp.concatenate([parts[k].reshape(-1) for k in _SMALL])
    rows = -(-flat.shape[0] // (16 * LANES)) * 16
    return jnp.pad(flat, (0, rows * LANES - flat.shape[0])).reshape(rows, LANES)


def _unpack_small(packed, shapes):
    flat = packed.reshape(-1)
    out, off = {}, 0
    for k in _SMALL:
        n = math.prod(shapes[k])
        out[k] = flat[off:off + n].reshape(shapes[k])
        off += n
    return out


def _from_chip_columns(stacked):
    _, rows, cols = stacked.shape
    return stacked.transpose(1, 0, 2).reshape(rows, N_CHIPS * cols)


def _to_chip_columns(full):
    rows, cols = full.shape
    return full.reshape(rows, N_CHIPS, cols // N_CHIPS).transpose(1, 0, 2)


def _w_in_grad_by_chip(dw):
    pieces = [dw["uv"], dw["z"], dw["xbc"], dw["dt"][:, :N_HEADS], dw["gate"]]
    bounds = [0]
    for p in pieces:
        bounds.append(bounds[-1] + p.shape[1])
    chips = []
    for k in range(N_CHIPS):
        lo, hi = k * _IN_SHARD, (k + 1) * _IN_SHARD
        parts = [p[:, max(lo, b0) - b0:min(hi, b1) - b0]
                 for p, b0, b1 in zip(pieces, bounds[:-1], bounds[1:]) if min(hi, b1) > max(lo, b0)]
        chips.append(jnp.concatenate(parts, axis=1))
    return jnp.stack(chips)


def kernel(x, norm_mix_g, w_in, conv_w, conv_b, dt_bias, a_log, d_skip, ssm_norm_g, v_norm_g, v_norm_b, w_spatial, b_spatial, b_gates, w_proj_a, w_proj_b, w_out, norm_mlp_g, w_mlp_up, w_mlp_down, norm_final_g, loss_target, m_norm_mix_g, m_w_in, m_conv_w, m_conv_b, m_dt_bias, m_a_log, m_d_skip, m_ssm_norm_g, m_v_norm_g, m_v_norm_b, m_w_spatial, m_b_spatial, m_b_gates, m_w_proj_a, m_w_proj_b, m_w_out, m_norm_mlp_g, m_w_mlp_up, m_w_mlp_down, m_norm_final_g, v_norm_mix_g, v_w_in, v_conv_w, v_conv_b, v_dt_bias, v_a_log, v_d_skip, v_ssm_norm_g, v_v_norm_g, v_v_norm_b, v_w_spatial, v_b_spatial, v_b_gates, v_w_proj_a, v_w_proj_b, v_w_out, v_norm_mlp_g, v_w_mlp_up, v_w_mlp_down, v_norm_final_g):
    given = dict(locals())
    names = ("norm_mix_g", "w_in", "conv_w", "conv_b", "dt_bias", "a_log", "d_skip", "ssm_norm_g", "v_norm_g",
             "v_norm_b", "w_spatial", "b_spatial", "b_gates", "w_proj_a", "w_proj_b", "w_out", "norm_mlp_g",
             "w_mlp_up", "w_mlp_down", "norm_final_g")
    xi, yi, ci = lax.axis_index("x"), lax.axis_index("y"), lax.axis_index("c")
    me_chip = (2 * xi + yi).astype(jnp.int32)

    def halves(a):
        return a.reshape(2, a.shape[0] // 2, a.shape[1])

    def with_own(got, shard):
        whole = lax.dynamic_update_slice(got, shard[None], (me_chip, 0, 0, 0))
        return whole.reshape(N_CHIPS, 2 * shard.shape[1], shard.shape[2])

    shard_in = halves(_cast_bf16(w_in[0], name="cast_w_in"))
    shard_late = halves(_cast_bf16(jnp.concatenate([given[k][0] for k in _LATE]), name="cast_w_late"))
    shard_conv = halves(conv_w.reshape(2 * _TAIL, -1))
    w_in_full = _from_chip_columns(with_own(_gather_shards(shard_in, name="gather_w_in"), shard_in))
    o_dt, o_gate = 2 * GMLP_WIDTH + D_INNER + CONV_DIM, 2 * GMLP_WIDTH + D_INNER + CONV_DIM + N_HEADS
    wts = {
        "uv": w_in_full[:, :2 * GMLP_WIDTH], "z": w_in_full[:, 2 * GMLP_WIDTH:2 * GMLP_WIDTH + D_INNER],
        "xbc": w_in_full[:, 2 * GMLP_WIDTH + D_INNER:o_dt], "dt": _pad_lanes(w_in_full[:, o_dt:o_gate]),
        "gate": w_in_full[:, o_gate:],
    }
    conv_all = with_own(_gather_shards(shard_conv, name="gather_conv_w"), shard_conv)
    conv_full = _from_chip_columns(conv_all.reshape(N_CHIPS, CONV_W, CONV_DIM // N_CHIPS))

    def late_weights(got):
        g_late = with_own(got, shard_late)

        def rows_of(k):
            return g_late[:, _LATE_OFF[k]:_LATE_OFF[k] + _LATE_ROWS[k]]

        return {
            "pa": rows_of("w_proj_a").reshape(GMLP_WIDTH, D_MODEL),
            "pb": rows_of("w_proj_b").reshape(D_INNER, D_MODEL), "out": rows_of("w_out").reshape(D_MODEL, D_MODEL),
            "up": _from_chip_columns(rows_of("w_mlp_up")), "down": rows_of("w_mlp_down").reshape(D_FF, D_MODEL),
        }

    small = {
        "norm_mix_g": norm_mix_g, "conv_w": conv_full, "conv_b": conv_b, "dt_bias": dt_bias, "a_log": a_log,
        "d_skip": d_skip, "ssm_norm_g": ssm_norm_g, "v_norm_g": v_norm_g, "v_norm_b": v_norm_b,
        "w_spatial": w_spatial[0], "b_spatial": b_spatial[0], "b_gates": b_gates, "norm_mlp_g": norm_mlp_g,
        "norm_final_g": norm_final_g.reshape(1, D_MODEL),
    }

    c_idx = ci.astype(jnp.int32).reshape(1)
    place = jnp.stack([me_chip, ci.astype(jnp.int32)])
    partials = {}

    def reduced_shard(tag, got_chips):
        own = _rs_add_chips(*partials[tag], got_chips, place, name="rs_add_chips_" + tag)
        both = _rs_join_halves(own, name="rs_join_" + tag)
        return both.reshape(2 * both.shape[1], both.shape[2])

    def late_grads(dw):
        def by_rows(a):
            return a.reshape(N_CHIPS, a.shape[0] // N_CHIPS, a.shape[1])

        g = jnp.concatenate([by_rows(dw["pa"]), by_rows(dw["pb"]), by_rows(dw["out"]), _to_chip_columns(dw["up"]),
                             by_rows(dw["down"])], axis=1)
        return g.reshape(N_CHIPS, 2, g.shape[1] // 2, g.shape[2])

    def late_partials(g, got_pair):
        partials["late"] = (g, got_pair)
        return _rs_add_pair(g, got_pair, c_idx, name="rs_add_pair_late")

    def in_partials(dw):
        g = _w_in_grad_by_chip(dw).reshape(N_CHIPS, 2, D_MODEL // 2, _IN_SHARD)
        got_pair = _rs_swap_halves(g, name="rs_swap_in")
        partials["in"] = (g, got_pair)
        return _rs_add_pair(g, got_pair, c_idx, name="rs_add_pair_in")

    loss_part, grad_x, dw, dsmall, got_late, got_in = _local_grads(
        x[0], loss_target[0], wts, small, fwd_ride=shard_late, late_weights=late_weights, swap_ride=late_grads,
        bwd_ride=late_partials, last_ride=in_partials)
    loss = lax.psum(loss_part[0, 0], ("x", "y", "c"))
    g_late = reduced_shard("late", got_late)
    g_in_shard = reduced_shard("in", got_in)

    small_shapes = {k: dsmall[k].shape for k in _SMALL}
    red = _unpack_small(_all_reduce_small(_pack_small(dsmall), name="all_reduce_small"), small_shapes)
    conv_cols = CONV_DIM // N_CHIPS
    red["conv_w"] = lax.dynamic_slice_in_dim(red["conv_w"], me_chip * conv_cols, conv_cols, axis=1)

    grads, deltas, new_m, new_v = {}, {}, {}, {}
    for k in ("w_in",) + _LATE:
        g2 = g_in_shard if k == "w_in" else g_late[_LATE_OFF[k]:_LATE_OFF[k] + _LATE_ROWS[k]]
        dlt, m2, v2 = _adamw(given[k][0], g2, given["m_" + k][0], given["v_" + k][0], name="adamw_" + k)
        grads[k], deltas[k], new_m[k], new_v[k] = g2, dlt, m2, v2
    adam_shapes = dict(small_shapes)
    adam_shapes["conv_w"] = (CONV_W, conv_cols)

    def small_pack_of(prefix):
        return _pack_small({k: given[prefix + k].reshape(adam_shapes[k]) for k in _SMALL})

    dlt_s, m_s, v_s = _adamw(small_pack_of(""), _pack_small(red), small_pack_of("m_"), small_pack_of("v_"),
                             name="adamw_small")
    for dst, packed in ((deltas, dlt_s), (new_m, m_s), (new_v, v_s)):
        dst.update(_unpack_small(packed, adam_shapes))
    grads.update(red)

    def shaped(dct):
        return [dct[k].reshape(given[k].shape) for k in names]

    return (loss, grad_x[None], *shaped(grads), *shaped(deltas), *shaped(new_m), *shaped(new_v))
```

```python
import functools
import math

import jax
import jax.numpy as jnp
from jax import lax
from jax.experimental import pallas as pl
from jax.experimental.pallas import tpu as pltpu

f32 = jnp.float32
bf16 = jnp.bfloat16

D_MODEL = 1024
CHUNK = 128
GMLP_WIDTH = 1024
GMLP_GROUPS = 8
D_INNER = 2048
HEAD_DIM = 64
N_HEADS = 32
N_GROUPS = 8
HEADS_PER_GROUP = 4
GROUP_W = HEADS_PER_GROUP * HEAD_DIM
D_STATE = 128
CONV_W = 4
CONV_DIM = 4096
D_FF = 4096
IN_PROJ = 10272
NORM_EPS = 1e-6
N_CHIPS = 4
N_DEV = 8
LANES = 128

ADAM_LR = 0.001
ADAM_B1 = 0.9
ADAM_B2 = 0.999
ADAM_EPS = 1e-08
ADAM_WD = 0.01
ADAM_STEP = 10

MESH = pl.DeviceIdType.MESH
_NT = (((1,), (1,)), ((), ()))
_NN = (((1,), (0,)), ((), ()))
_TN = (((0,), (0,)), ((), ()))
_MB = 2 ** 20


def _params(sem, vmem_mb=48):
    return pltpu.CompilerParams(dimension_semantics=sem, vmem_limit_bytes=vmem_mb * _MB)


def _dot(a, b, dims=_NN):
    return lax.dot_general(a.astype(bf16), b.astype(bf16), dims, preferred_element_type=f32)


def _dot32(a, b):
    return jnp.dot(a, b, preferred_element_type=f32, precision=lax.Precision.HIGHEST)


def _sigmoid(x):
    return 1.0 / (1.0 + jnp.exp(-x))


def _sum_all(a):
    return jnp.sum(jnp.sum(a, axis=1, keepdims=True), axis=0, keepdims=True)


def _iota(shape, dim):
    return lax.broadcasted_iota(jnp.int32, shape, dim)


def _matmul(a, b, *, nt=False, tm, tn, tk, out_dtypes, epilogue=None, extras=(), extra_specs=None, name):
    m, k_dim = a.shape
    n = b.shape[0] if nt else b.shape[1]
    nk = k_dim // tk
    ne, no = len(extras), len(out_dtypes)
    dims = _NT if nt else _NN

    def body(*refs):
        a_ref, b_ref = refs[0], refs[1]
        ex = refs[2:2 + ne]
        outs = refs[2 + ne:2 + ne + no]

        def finish(acc):
            vals = epilogue(acc, *[e[...] for e in ex]) if epilogue is not None else (acc,)
            for o, v in zip(outs, vals):
                o[...] = v.astype(o.dtype)

        part = lax.dot_general(a_ref[...], b_ref[...], dims, preferred_element_type=f32)
        if nk == 1:
            finish(part)
        else:
            acc_ref = refs[-1]
            kk = pl.program_id(2)

            @pl.when(kk == 0)
            def _():
                acc_ref[...] = part

            @pl.when(kk > 0)
            def _():
                acc_ref[...] += part

            @pl.when(kk == nk - 1)
            def _():
                finish(acc_ref[...])

    b_spec = pl.BlockSpec((tn, tk), lambda i, j, k: (j, k)) if nt else pl.BlockSpec((tk, tn), lambda i, j, k: (k, j))
    tile = pl.BlockSpec((tm, tn), lambda i, j, k: (i, j))
    ex_specs = [tile if s is None else s for s in (extra_specs or [None] * ne)]
    outs = pl.pallas_call(
        body, name=name, grid=(m // tm, n // tn, nk),
        in_specs=[pl.BlockSpec((tm, tk), lambda i, j, k: (i, k)), b_spec] + ex_specs,
        out_specs=[tile] * no,
        out_shape=[jax.ShapeDtypeStruct((m, n), dt) for dt in out_dtypes],
        scratch_shapes=[pltpu.VMEM((tm, tn), f32)] if nk > 1 else [],
        compiler_params=_params(("parallel", "parallel", "arbitrary")),
    )(a, b, *extras)
    return outs if no > 1 else outs[0]


def _matmul_nt_sum(pairs, *, tm, tks, ride=None, name):
    m = pairs[0][0].shape[0]
    n = pairs[0][1].shape[0]
    nblk = [a.shape[1] // tk for (a, _), tk in zip(pairs, tks)]
    starts = [sum(nblk[:p]) for p in range(len(pairs))]
    nk = sum(nblk)
    npairs = len(pairs)
    ni = m // tm
    riding = ride is not None

    def body(*refs):
        rest = refs[2 * npairs:]
        if riding:
            ride_ref, o_ref, got_ref, acc_ref, send_sems, recv_sems = rest
        else:
            o_ref, acc_ref = rest
        i, kk = pl.program_id(0), pl.program_id(1)
        if riding:
            start, finish = _scatter_protocol(ride_ref, got_ref, send_sems, recv_sems)
            pl.when((i == 0) & (kk == 0))(start)

        @pl.when(kk == 0)
        def _():
            acc_ref[...] = jnp.zeros_like(acc_ref)

        for p in range(npairs):
            @pl.when((kk >= starts[p]) & (kk < starts[p] + nblk[p]))
            def _(p=p):
                acc_ref[...] += lax.dot_general(refs[2 * p][...], refs[2 * p + 1][...], _NT, preferred_element_type=f32)

        @pl.when(kk == nk - 1)
        def _():
            o_ref[...] = acc_ref[...]

        if riding:
            pl.when((i == ni - 1) & (kk == nk - 1))(finish)

    in_specs, args = [], []
    for p, (a, b) in enumerate(pairs):
        def kblock(k, s=starts[p], nb=nblk[p]):
            return jnp.clip(k - s, 0, nb - 1)
        in_specs.append(pl.BlockSpec((tm, tks[p]), lambda i, k, kb=kblock: (i, kb(k))))
        in_specs.append(pl.BlockSpec((n, tks[p]), lambda i, k, kb=kblock: (0, kb(k))))
        args += [a, b]
    tile = pl.BlockSpec((tm, n), lambda i, k: (i, 0))
    outs = pl.pallas_call(
        body, name=name, grid=(ni, nk), in_specs=in_specs + [_ANY] * riding, out_specs=[tile] + [_ANY] * riding,
        out_shape=[jax.ShapeDtypeStruct((m, n), f32)]
        + ([jax.ShapeDtypeStruct((N_CHIPS - 1,) + ride.shape[1:], ride.dtype)] if riding else []),
        scratch_shapes=[pltpu.VMEM((tm, n), f32)] + (list(_SCATTER_SCRATCH) if riding else []),
        compiler_params=_params(("arbitrary", "arbitrary"), vmem_mb=56),
    )(*args, *([ride] if riding else []))
    return outs if riding else outs[0]


def _matmul_tn(a, b, *, tka, tn, tt, name):
    t, ka = a.shape
    n = b.shape[1]

    def body(a_ref, b_ref, o_ref):
        part = lax.dot_general(a_ref[...], b_ref[...], _TN, preferred_element_type=f32)
        kk = pl.program_id(2)

        @pl.when(kk == 0)
        def _():
            o_ref[...] = part

        @pl.when(kk > 0)
        def _():
            o_ref[...] += part

    return pl.pallas_call(
        body, name=name, grid=(ka // tka, n // tn, t // tt),
        in_specs=[pl.BlockSpec((tt, tka), lambda i, j, k: (k, i)), pl.BlockSpec((tt, tn), lambda i, j, k: (k, j))],
        out_specs=pl.BlockSpec((tka, tn), lambda i, j, k: (i, j)),
        out_shape=jax.ShapeDtypeStruct((ka, n), f32),
        compiler_params=_params(("parallel", "parallel", "arbitrary")),
    )(a, b)


def _row_tile(t):
    return min(t, 512)


def _rms_fwd(x, g, *, name):
    t, d = x.shape
    tr = _row_tile(t)

    def body(x_ref, g_ref, h_ref):
        xv = x_ref[...]
        r = lax.rsqrt(jnp.mean(xv * xv, axis=1, keepdims=True) + NORM_EPS)
        h_ref[...] = (xv * r * g_ref[...]).astype(bf16)

    return pl.pallas_call(
        body, name=name, grid=(t // tr,),
        in_specs=[pl.BlockSpec((tr, d), lambda i: (i, 0)), pl.BlockSpec((1, d), lambda i: (0, 0))],
        out_specs=pl.BlockSpec((tr, d), lambda i: (i, 0)),
        out_shape=jax.ShapeDtypeStruct((t, d), bf16),
        compiler_params=_params(("parallel",)),
    )(x, g)


def _rms_bwd(xin, g, dh, dres, *, want_bf16, name):
    t, d = xin.shape
    tr = _row_tile(t)

    def body(x_ref, g_ref, dh_ref, dres_ref, dx_ref, *rest):
        dg_ref = rest[-1]
        xv = x_ref[...]
        r = lax.rsqrt(jnp.mean(xv * xv, axis=1, keepdims=True) + NORM_EPS)
        xn = xv * r
        dhv = dh_ref[...]
        dxn = dhv * g_ref[...]
        dx = dres_ref[...] + r * (dxn - xn * jnp.mean(dxn * xn, axis=1, keepdims=True))
        dx_ref[...] = dx
        if want_bf16:
            rest[0][...] = dx.astype(bf16)
        part = jnp.sum(dhv * xn, axis=0, keepdims=True)

        @pl.when(pl.program_id(0) == 0)
        def _():
            dg_ref[...] = part

        @pl.when(pl.program_id(0) > 0)
        def _():
            dg_ref[...] += part

    row = pl.BlockSpec((tr, d), lambda i: (i, 0))
    vec = pl.BlockSpec((1, d), lambda i: (0, 0))
    out_shape = [jax.ShapeDtypeStruct((t, d), f32)] + ([jax.ShapeDtypeStruct((t, d), bf16)] if want_bf16 else []) \
        + [jax.ShapeDtypeStruct((1, d), f32)]
    return pl.pallas_call(
        body, name=name, grid=(t // tr,),
        in_specs=[row, vec, row, row],
        out_specs=[row] + ([row] if want_bf16 else []) + [vec],
        out_shape=out_shape,
        compiler_params=_params(("arbitrary",)),
    )(xin, g, dh, dres)


def _loss_head(x2, tgt, g, *, name):
    t, d = x2.shape
    tr = _row_tile(t)

    def body(x_ref, t_ref, g_ref, dx_ref, dxb_ref, dg_ref, loss_ref):
        xv = x_ref[...]
        gv = g_ref[...]
        r = lax.rsqrt(jnp.mean(xv * xv, axis=1, keepdims=True) + NORM_EPS)
        xn = xv * r
        e = xn * gv - t_ref[...]
        lpart = jnp.zeros((1, LANES), f32) + 0.5 * _sum_all(jnp.mean(e * e, axis=1, keepdims=True))
        dy = e * (1.0 / d)
        dxn = dy * gv
        dx = r * (dxn - xn * jnp.mean(dxn * xn, axis=1, keepdims=True))
        dx_ref[...] = dx
        dxb_ref[...] = dx.astype(bf16)
        gpart = jnp.sum(dy * xn, axis=0, keepdims=True)

        @pl.when(pl.program_id(0) == 0)
        def _():
            dg_ref[...] = gpart
            loss_ref[...] = lpart

        @pl.when(pl.program_id(0) > 0)
        def _():
            dg_ref[...] += gpart
            loss_ref[...] += lpart

    row = pl.BlockSpec((tr, d), lambda i: (i, 0))
    vec = pl.BlockSpec((1, d), lambda i: (0, 0))
    return pl.pallas_call(
        body, name=name, grid=(t // tr,),
        in_specs=[row, row, vec],
        out_specs=[row, row, vec, pl.BlockSpec((1, LANES), lambda i: (0, 0))],
        out_shape=[jax.ShapeDtypeStruct((t, d), f32), jax.ShapeDtypeStruct((t, d), bf16),
                   jax.ShapeDtypeStruct((1, d), f32), jax.ShapeDtypeStruct((1, LANES), f32)],
        compiler_params=_params(("arbitrary",)),
    )(x2, tgt, g)


def _merge_bwd(dm, pa, pb, gl, bg, *, name):
    t, d = pa.shape
    tr = _row_tile(t)

    def body(dm_ref, pa_ref, pb_ref, gla_ref, glb_ref, bga_ref, bgb_ref, dpa_ref, dpb_ref, dgl_ref, dbg_ref):
        dmv = dm_ref[...]
        ga = _sigmoid(gla_ref[...] + bga_ref[...])
        gb = _sigmoid(glb_ref[...] + bgb_ref[...])
        dpa_ref[...] = (dmv * ga).astype(bf16)
        dpb_ref[...] = (dmv * gb).astype(bf16)
        dla = dmv * pa_ref[...] * ga * (1.0 - ga)
        dlb = dmv * pb_ref[...] * gb * (1.0 - gb)
        dgl_ref[:, :d] = dla.astype(bf16)
        dgl_ref[:, d:] = dlb.astype(bf16)
        sa = jnp.sum(dla, axis=0, keepdims=True)
        sb = jnp.sum(dlb, axis=0, keepdims=True)

        @pl.when(pl.program_id(0) == 0)
        def _():
            dbg_ref[:, :d] = sa
            dbg_ref[:, d:] = sb

        @pl.when(pl.program_id(0) > 0)
        def _():
            dbg_ref[:, :d] += sa
            dbg_ref[:, d:] += sb

    row = pl.BlockSpec((tr, d), lambda i: (i, 0))
    return pl.pallas_call(
        body, name=name, grid=(t // tr,),
        in_specs=[row, row, row, row, pl.BlockSpec((tr, d), lambda i: (i, 1)),
                  pl.BlockSpec((1, d), lambda i: (0, 0)), pl.BlockSpec((1, d), lambda i: (0, 1))],
        out_specs=[row, row, pl.BlockSpec((tr, 2 * d), lambda i: (i, 0)), pl.BlockSpec((1, 2 * d), lambda i: (0, 0))],
        out_shape=[jax.ShapeDtypeStruct((t, d), bf16), jax.ShapeDtypeStruct((t, d), bf16),
                   jax.ShapeDtypeStruct((t, 2 * d), bf16), jax.ShapeDtypeStruct((1, 2 * d), f32)],
        compiler_params=_params(("arbitrary",)),
    )(dm, pa, pb, gl, gl, bg, bg)


_INV_SQRT2 = 1.0 / math.sqrt(2.0)
_INV_SQRT2PI = 1.0 / math.sqrt(2.0 * math.pi)


def _gmlp_common(uv, vg, vb, with_grad=False):
    cdf = 0.5 * (1.0 + lax.erf(uv * _INV_SQRT2))
    zz = uv * cdf
    u, vhat, rstd, vn = _gmlp_norm(zz, vg, vb)
    if not with_grad:
        return u, vhat, rstd, vn
    return u, vhat, rstd, vn, cdf + uv * jnp.exp(-0.5 * uv * uv) * _INV_SQRT2PI


def _gmlp_norm(zz, vg, vb):
    u = zz[:, :GMLP_WIDTH]
    v = zz[:, GMLP_WIDTH:]
    mu = jnp.mean(v, axis=1, keepdims=True)
    vc = v - mu
    rstd = lax.rsqrt(jnp.mean(vc * vc, axis=1, keepdims=True) + NORM_EPS)
    vhat = vc * rstd
    vn = vhat * vg + vb
    return u, vhat, rstd, vn


def _gmlp_fwd(uv, vg, vb, wsp, bsp_t, *, name):
    t = uv.shape[0]
    nc = t // CHUNK

    def body(uv_ref, vg_ref, vb_ref, w_ref, b_ref, y_ref):
        u, _, _, vn = _gmlp_common(uv_ref[...], vg_ref[...], vb_ref[...])
        tril = _iota((CHUNK, CHUNK), 0) >= _iota((CHUNK, CHUNK), 1)
        bt = b_ref[...]
        for g in range(GMLP_GROUPS):
            sl = slice(g * CHUNK, (g + 1) * CHUNK)
            w = jnp.where(tril, w_ref[g], 0.0)
            s = _dot(w, vn[:, sl]) + bt[:, g:g + 1]
            y_ref[:, sl] = (u[:, sl] * s).astype(bf16)

    return pl.pallas_call(
        body, name=name, grid=(nc,),
        in_specs=[pl.BlockSpec((CHUNK, 2 * GMLP_WIDTH), lambda c: (c, 0)),
                  pl.BlockSpec((1, GMLP_WIDTH), lambda c: (0, 0)), pl.BlockSpec((1, GMLP_WIDTH), lambda c: (0, 0)),
                  pl.BlockSpec((GMLP_GROUPS, CHUNK, CHUNK), lambda c: (0, 0, 0)),
                  pl.BlockSpec((CHUNK, LANES), lambda c: (0, 0))],
        out_specs=pl.BlockSpec((CHUNK, GMLP_WIDTH), lambda c: (c, 0)),
        out_shape=jax.ShapeDtypeStruct((t, GMLP_WIDTH), bf16),
        compiler_params=_params(("parallel",)),
    )(uv, vg, vb, wsp, bsp_t)


def _gmlp_bwd(uv, dya, vg, vb, wsp, bsp_t, *, ride=None, name):
    t = uv.shape[0]
    nc = t // CHUNK
    riding = ride is not None

    def body(*refs):
        uv_ref, dy_ref, vg_ref, vb_ref, w_ref, b_ref = refs[:6]
        duv_ref, dw_ref, db_ref, dvg_ref, dvb_ref = refs[6 + riding:11 + riding]
        first = pl.program_id(0) == 0
        if riding:
            start, finish = _swap_protocol(refs[6], refs[12], refs[13], refs[14])
            pl.when(first)(start)

        @pl.when(first)
        def _():
            dw_ref[...] = jnp.zeros_like(dw_ref)
            db_ref[...] = jnp.zeros_like(db_ref)
            dvg_ref[...] = jnp.zeros_like(dvg_ref)
            dvb_ref[...] = jnp.zeros_like(dvb_ref)

        uvv = uv_ref[...]
        vgv = vg_ref[...]
        u, vhat, rstd, vn, gelu_grad = _gmlp_common(uvv, vgv, vb_ref[...], with_grad=True)
        dy = dy_ref[...]
        tril = _iota((CHUNK, CHUNK), 0) >= _iota((CHUNK, CHUNK), 1)
        lane = _iota((CHUNK, LANES), 1)
        bt = b_ref[...]
        ds_all = dy * u
        dbacc = jnp.zeros((CHUNK, LANES), f32)
        dvh_parts = []
        for g in range(GMLP_GROUPS):
            sl = slice(g * CHUNK, (g + 1) * CHUNK)
            w = jnp.where(tril, w_ref[g], 0.0)
            vng = vn[:, sl]
            s = _dot(w, vng) + bt[:, g:g + 1]
            ds = ds_all[:, sl]
            duv_ref[:, sl] = (dy[:, sl] * s * gelu_grad[:, sl]).astype(bf16)
            dw_ref[g] += jnp.where(tril, _dot(ds, vng, _NT), 0.0)
            dbacc = dbacc + jnp.where(lane == g, jnp.sum(ds, axis=1, keepdims=True), 0.0)
            dvn = _dot(w, ds, _TN)
            vh = vhat[:, sl]
            dvg_ref[:, sl] += jnp.sum(dvn * vh, axis=0, keepdims=True)
            dvb_ref[:, sl] += jnp.sum(dvn, axis=0, keepdims=True)
            dvh_parts.append(dvn * vgv[:, sl])
        db_ref[...] += dbacc
        dvhat = jnp.concatenate(dvh_parts, axis=1)
        m1 = jnp.mean(dvhat, axis=1, keepdims=True)
        m2 = jnp.mean(dvhat * vhat, axis=1, keepdims=True)
        dv = rstd * (dvhat - m1 - vhat * m2)
        duv_ref[:, GMLP_WIDTH:] = (dv * gelu_grad[:, GMLP_WIDTH:]).astype(bf16)
        if riding:
            pl.when(pl.program_id(0) == nc - 1)(finish)

    vec = pl.BlockSpec((1, GMLP_WIDTH), lambda c: (0, 0))
    return pl.pallas_call(
        body, name=name, grid=(nc,),
        in_specs=[pl.BlockSpec((CHUNK, 2 * GMLP_WIDTH), lambda c: (c, 0)),
                  pl.BlockSpec((CHUNK, GMLP_WIDTH), lambda c: (c, 0)), vec, vec,
                  pl.BlockSpec((GMLP_GROUPS, CHUNK, CHUNK), lambda c: (0, 0, 0)),
                  pl.BlockSpec((CHUNK, LANES), lambda c: (0, 0))] + [_ANY] * riding,
        out_specs=[pl.BlockSpec((CHUNK, 2 * GMLP_WIDTH), lambda c: (c, 0)),
                   pl.BlockSpec((GMLP_GROUPS, CHUNK, CHUNK), lambda c: (0, 0, 0)),
                   pl.BlockSpec((CHUNK, LANES), lambda c: (0, 0)), vec, vec] + [_ANY] * riding,
        out_shape=[jax.ShapeDtypeStruct((t, 2 * GMLP_WIDTH), bf16),
                   jax.ShapeDtypeStruct((GMLP_GROUPS, CHUNK, CHUNK), f32),
                   jax.ShapeDtypeStruct((CHUNK, LANES), f32),
                   jax.ShapeDtypeStruct((1, GMLP_WIDTH), f32), jax.ShapeDtypeStruct((1, GMLP_WIDTH), f32)]
        + ([jax.ShapeDtypeStruct(ride.shape[:1] + ride.shape[2:], ride.dtype)] if riding else []),
        scratch_shapes=list(_SWAP_SCRATCH) if riding else [],
        compiler_params=_params(("arbitrary",)),
    )(uv, dya, vg, vb, wsp, bsp_t, *([ride] if riding else []))


_CONV_COLS = 512
_XS0, _B0, _C0 = 0, D_INNER, D_INNER + N_GROUPS * D_STATE


_TAIL = 8


def _conv_silu(cur_ref, tail_ref, w_ref, b_ref, has_prev, xc_ref, cv_ref):
    row = _iota((_TAIL, _CONV_COLS), 0)
    for j in range(CONV_DIM // _CONV_COLS):
        sl = slice(j * _CONV_COLS, (j + 1) * _CONV_COLS)
        cur = cur_ref[:, sl]
        tail = jnp.where(has_prev, tail_ref[:, sl], 0.0)
        acc = cur * w_ref[CONV_W - 1:CONV_W, sl] + b_ref[:, sl]
        for s in range(1, CONV_W):
            rolled = pltpu.roll(cur, s, 0)
            top = jnp.where(row >= s, rolled[:_TAIL], pltpu.roll(tail, s, 0))
            sh = jnp.concatenate([top, rolled[_TAIL:]], axis=0)
            acc = acc + sh * w_ref[CONV_W - 1 - s:CONV_W - s, sl]
        cv_ref[:, sl] = acc
        xc_ref[:, sl] = acc * _sigmoid(acc)


def _col_bcast(mat, h):
    return jnp.broadcast_to(mat[:, h:h + 1], (CHUNK, LANES))


def _head_expand(cols):
    lo = _iota((CHUNK, LANES), 1) < HEAD_DIM
    return jnp.concatenate([jnp.where(lo, cols[2 * j], cols[2 * j + 1]) for j in range(N_HEADS // 2)], axis=1)


def _ssd_chunk_scalars(dtr, dtb, alog):
    xdt_pre = dtr + dtb
    dtv = jnp.maximum(xdt_pre, 0.0) + jnp.log(1.0 + jnp.exp(-jnp.abs(xdt_pre)))
    a = -jnp.exp(alog)
    ltri = (_iota((CHUNK, CHUNK), 0) >= _iota((CHUNK, CHUNK), 1)).astype(f32)
    cs = _dot32(ltri, dtv * a)
    csb = [_col_bcast(cs, h) for h in range(N_HEADS)]
    cs_x = _head_expand(csb)
    dt_x = _head_expand([_col_bcast(dtv, h) for h in range(N_HEADS)])
    cl_x = cs_x[CHUNK - 1:CHUNK, :]
    return dict(xdt_pre=xdt_pre, dtv=dtv, a=a, cs=cs, cs_t=cs.T, csb=csb, dt_x=dt_x, e_x=jnp.exp(cs_x),
                dec_x=jnp.exp(cl_x - cs_x), dk_x=jnp.exp(cl_x))


def _head_masks():
    lane = _iota((CHUNK, GROUP_W), 1)
    return [(lane >= r * HEAD_DIM) & (lane < (r + 1) * HEAD_DIM) for r in range(HEADS_PER_GROUP)]


def _stack_heads(a, masks):
    return jnp.concatenate([jnp.where(m, a, 0.0) for m in masks], axis=0).astype(bf16)


def _seg_sum(a, seg):
    hi = a.astype(jnp.bfloat16)
    lo = (a - hi.astype(f32)).astype(jnp.bfloat16)
    return (lax.dot_general(hi, seg, _NN, preferred_element_type=f32)
            + lax.dot_general(lo, seg, _NN, preferred_element_type=f32))


def _head_seg_matrix():
    return (_iota((D_INNER, LANES), 0) // HEAD_DIM == _iota((D_INNER, LANES), 1)).astype(jnp.bfloat16)


def _ssd_fwd(xbc, z, dtr, cw, cb, dtb, alog, dsk_x, gs, *, ride=None, name):
    t = xbc.shape[0]
    nc = t // CHUNK
    tiles = CHUNK // _TAIL

    def body(*refs):
        cur_ref, tail_ref, z_ref, dtr_ref, cw_ref, cb_ref, dtb_ref, alog_ref, dsk_ref, gs_ref = refs[:10]
        if ride is None:
            yb_ref, hp_ref, cv_ref, state_ref, xc_ref = refs[10:]
        else:
            ride_ref, yb_ref, hp_ref, cv_ref, got_ref, state_ref, xc_ref, send_sems, recv_sems = refs[10:]
        c = pl.program_id(0)
        if ride is not None:
            start, relay, finish = _gather_protocol(ride_ref, got_ref, send_sems, recv_sems)
            pl.when(c == 0)(start)
            pl.when(c == nc // 2)(relay)

        @pl.when(c == 0)
        def _():
            state_ref[...] = jnp.zeros_like(state_ref)

        _conv_silu(cur_ref, tail_ref, cw_ref, cb_ref, c > 0, xc_ref, cv_ref)
        sc = _ssd_chunk_scalars(dtr_ref[...], dtb_ref[...], alog_ref[...])
        tril = _iota((CHUNK, CHUNK), 0) >= _iota((CHUNK, CHUNK), 1)
        masks = _head_masks()
        hp_ref[0] = state_ref[...]
        for g in range(N_GROUPS):
            gsl = slice(g * GROUP_W, (g + 1) * GROUP_W)
            xs_g = xc_ref[:, gsl]
            bg = xc_ref[:, _B0 + g * D_STATE:_B0 + (g + 1) * D_STATE]
            cg = xc_ref[:, _C0 + g * D_STATE:_C0 + (g + 1) * D_STATE]
            xdt_g = xs_g * sc["dt_x"][:, gsl]
            cbm = _dot(cg, bg, _NT)
            mw = jnp.concatenate(
                [cbm * jnp.exp(jnp.where(tril, sc["csb"][h] - sc["cs_t"][h:h + 1, :], -1e30))
                 for h in range(g * HEADS_PER_GROUP, (g + 1) * HEADS_PER_GROUP)], axis=1)
            ht_g = state_ref[:, gsl]
            y_g = _dot(mw, _stack_heads(xdt_g, masks)) + sc["e_x"][:, gsl] * _dot(cg, ht_g) + dsk_ref[:, gsl] * xs_g
            state_ref[:, gsl] = ht_g * sc["dk_x"][:, gsl] + _dot(bg, xdt_g * sc["dec_x"][:, gsl], _TN)
            zg = z_ref[:, gsl]
            yg = y_g * zg * _sigmoid(zg)
            rs = lax.rsqrt(jnp.mean(yg * yg, axis=1, keepdims=True) + NORM_EPS)
            yb_ref[:, gsl] = (yg * rs * gs_ref[:, gsl]).astype(bf16)
        if ride is not None:
            pl.when(c == nc - 1)(finish)

    def chunk(w):
        return pl.BlockSpec((CHUNK, w), lambda c: (c, 0))

    def const(shape):
        return pl.BlockSpec(shape, lambda c: (0,) * len(shape))

    riding = ride is not None
    return pl.pallas_call(
        body, name=name, grid=(nc,),
        in_specs=[chunk(CONV_DIM), pl.BlockSpec((_TAIL, CONV_DIM), lambda c: (jnp.maximum(c * tiles - 1, 0), 0)),
                  chunk(D_INNER), chunk(LANES), const((CONV_W, CONV_DIM)), const((1, CONV_DIM)),
                  const((1, LANES)), const((1, LANES)), const((1, D_INNER)), const((1, D_INNER))] + [_ANY] * riding,
        out_specs=[chunk(D_INNER), pl.BlockSpec((1, D_STATE, D_INNER), lambda c: (c, 0, 0)), chunk(CONV_DIM)]
        + [_ANY] * riding,
        out_shape=[jax.ShapeDtypeStruct((t, D_INNER), bf16), jax.ShapeDtypeStruct((nc, D_STATE, D_INNER), f32),
                   jax.ShapeDtypeStruct((t, CONV_DIM), f32)]
        + ([jax.ShapeDtypeStruct((N_CHIPS,) + ride.shape, ride.dtype)] if riding else []),
        scratch_shapes=[pltpu.VMEM((D_STATE, D_INNER), f32), pltpu.VMEM((CHUNK, CONV_DIM), f32)]
        + (list(_GATHER_SCRATCH) if riding else []),
        compiler_params=_params(("arbitrary",)),
    )(xbc, xbc, z, dtr, cw, cb, dtb, alog, dsk_x, gs, *([ride] if riding else []))


def _ssd_bwd(xbc, cv, z, dtr, hprev, dyb, cw, dtb, alog, dsk_x, gs, seg, *, ride=None, name):
    t = xbc.shape[0]
    nc = t // CHUNK

    def body(*refs):
        (cur_ref, cv_ref, z_ref, dtr_ref, hp_ref, dyb_ref, cw_ref, dtb_ref, alog_ref, dsk_ref, gs_ref,
         seg_ref) = refs[:12]
        rest = refs[12:]
        if ride is not None:
            ride_ref, got_ref, send_sems, recv_sems = rest[0], rest[10], rest[-2], rest[-1]
            rest = rest[1:10] + rest[11:-2]
        (dz_ref, dxbc_ref, ddt_ref, dcw_ref, dcb_ref, ddtb_ref, dalog_ref, ddsk_ref, dgs_ref,
         dh_ref, dcnext_ref, xc_ref, dxc_ref, x13_ref, x2_ref, rows_ref) = rest
        i = pl.program_id(0)
        if ride is not None:
            start, finish = _scatter_protocol(ride_ref, got_ref, send_sems, recv_sems)
            pl.when(i == 0)(start)

        @pl.when(i == 0)
        def _():
            for ref in (dh_ref, dcnext_ref, dcw_ref, dcb_ref, ddtb_ref, dalog_ref, ddsk_ref, dgs_ref, rows_ref):
                ref[...] = jnp.zeros_like(ref)

        for j in range(CONV_DIM // _CONV_COLS):
            sl = slice(j * _CONV_COLS, (j + 1) * _CONV_COLS)
            cvv = cv_ref[:, sl]
            xc_ref[:, sl] = cvv * _sigmoid(cvv)
        sc = _ssd_chunk_scalars(dtr_ref[...], dtb_ref[...], alog_ref[...])
        tril = _iota((CHUNK, CHUNK), 0) >= _iota((CHUNK, CHUNK), 1)
        triu = _iota((CHUNK, CHUNK), 0) <= _iota((CHUNK, CHUNK), 1)
        masks = _head_masks()
        rowh = _iota((N_HEADS, CHUNK), 0)
        dcs_t = jnp.zeros((N_HEADS, CHUNK), f32)
        for g in range(N_GROUPS):
            gsl = slice(g * GROUP_W, (g + 1) * GROUP_W)
            xs_g = xc_ref[:, gsl]
            bg = xc_ref[:, _B0 + g * D_STATE:_B0 + (g + 1) * D_STATE]
            cg = xc_ref[:, _C0 + g * D_STATE:_C0 + (g + 1) * D_STATE]
            dt_g, e_g, dec_g, dk_g = sc["dt_x"][:, gsl], sc["e_x"][:, gsl], sc["dec_x"][:, gsl], sc["dk_x"][:, gsl]
            dsk_g = dsk_ref[:, gsl]
            xdt_g = xs_g * dt_g
            xdt_stack = _stack_heads(xdt_g, masks)
            cbm = _dot(cg, bg, _NT)
            cbt = _dot(bg, cg, _NT)
            heads = range(g * HEADS_PER_GROUP, (g + 1) * HEADS_PER_GROUP)
            lmats = [jnp.exp(jnp.where(tril, sc["csb"][h] - sc["cs_t"][h:h + 1, :], -1e30)) for h in heads]
            mw = jnp.concatenate([cbm * lm for lm in lmats], axis=1)
            mtw = jnp.concatenate(
                [cbt * jnp.exp(jnp.where(triu, sc["cs_t"][h:h + 1, :] - sc["csb"][h], -1e30)) for h in heads], axis=1)
            ht_g = hp_ref[0, :, gsl]
            dhn_g = dh_ref[:, gsl]
            yoff = e_g * _dot(cg, ht_g)
            y_g = _dot(mw, xdt_stack) + yoff + dsk_g * xs_g
            zg = z_ref[:, gsl]
            sz = _sigmoid(zg)
            silu = zg * sz
            yg = y_g * silu
            rs = lax.rsqrt(jnp.mean(yg * yg, axis=1, keepdims=True) + NORM_EPS)
            yn = yg * rs
            dyb = dyb_ref[:, gsl]
            dgs_ref[:, gsl] += jnp.sum(dyb * yn, axis=0, keepdims=True)
            dyn = dyb * gs_ref[:, gsl]
            dyg = rs * (dyn - yn * jnp.mean(dyn * yn, axis=1, keepdims=True))
            dy_g = dyg * silu
            dz_ref[:, gsl] = (dyg * y_g * (sz * (1.0 + zg * (1.0 - sz)))).astype(bf16)
            dy_stack = _stack_heads(dy_g, masks)
            dm_w = _dot(dy_g, xdt_stack, _NT)
            dmt_w = _dot(xdt_g, dy_stack, _NT)
            dxdt = _dot(mtw, dy_stack)
            dcb_acc = jnp.zeros((CHUNK, CHUNK), f32)
            for r, h in enumerate(heads):
                hs = slice(r * CHUNK, (r + 1) * CHUNK)
                dml = dm_w[:, hs] * lmats[r]
                dcb_acc = dcb_acc + dml
                col = jnp.sum(dml * cbm, axis=0, keepdims=True)
                row = jnp.sum(dmt_w[:, hs] * mtw[:, hs], axis=0, keepdims=True)
                dcs_t = dcs_t + jnp.where(rowh == h, row - col, 0.0)
            w = _dot(bg, dhn_g)
            dxdt = dxdt + dec_g * w
            decx3 = dec_g * (xdt_g * w)
            dg_g = e_g * dy_g
            d_c = _dot(dg_g, ht_g, _NT) + _dot(dcb_acc, bg)
            d_b = _dot(dcb_acc, cg, _TN) + _dot(xdt_g * dec_g, dhn_g, _NT)
            dh_ref[:, gsl] = dhn_g * dk_g + _dot(cg, dg_g, _TN)
            dxc_ref[:, gsl] = dsk_g * dy_g + dxdt * dt_g
            dxc_ref[:, _B0 + g * D_STATE:_B0 + (g + 1) * D_STATE] = d_b
            dxc_ref[:, _C0 + g * D_STATE:_C0 + (g + 1) * D_STATE] = d_c
            x13_ref[:, gsl] = dy_g * yoff - decx3
            x2_ref[:, gsl] = dxdt * xs_g
            rows_ref[0:1, gsl] = jnp.sum(dhn_g * ht_g, axis=0, keepdims=True)
            rows_ref[1:2, gsl] = jnp.sum(decx3, axis=0, keepdims=True)
            rows_ref[2:3, gsl] = jnp.sum(dy_g * xs_g, axis=0, keepdims=True)
        segm = seg_ref[...]
        r13 = _seg_sum(x13_ref[...], segm)
        r2 = _seg_sum(x2_ref[...], segm)
        small = _seg_sum(rows_ref[...], segm)
        lane = _iota((CHUNK, LANES), 1)
        rowi = _iota((CHUNK, LANES), 0)
        dcl_row = small[0:1, :] * jnp.exp(sc["cs"][CHUNK - 1:CHUNK, :]) + small[1:2, :]
        dcs = r13 + jnp.where(rowi == CHUNK - 1, dcl_row, 0.0)
        dcs_t_all = dcs.T + jnp.concatenate([dcs_t, jnp.zeros((LANES - N_HEADS, CHUNK), f32)], axis=0)
        dda = _dot32(dcs_t_all, tril.astype(f32)).T
        a = sc["a"]
        ddt_total = r2 + dda * a
        dalog_ref[...] += jnp.sum(dda * sc["dtv"], axis=0, keepdims=True) * a
        ddtr = jnp.where(lane < N_HEADS, ddt_total * _sigmoid(sc["xdt_pre"]), 0.0)
        ddtb_ref[...] += jnp.sum(ddtr, axis=0, keepdims=True)
        ddt_ref[...] = ddtr.astype(bf16)
        ddsk_ref[...] += small[2:3, :]
        row8 = _iota((_TAIL, _CONV_COLS), 0)
        for j in range(CONV_DIM // _CONV_COLS):
            sl = slice(j * _CONV_COLS, (j + 1) * _CONV_COLS)
            cvv = cv_ref[:, sl]
            sg = _sigmoid(cvv)
            dconv = dxc_ref[:, sl] * (sg * (1.0 + cvv * (1.0 - sg)))
            nxt = dcnext_ref[:, sl]
            cur = cur_ref[:, sl]
            dxin = dconv * cw_ref[CONV_W - 1:CONV_W, sl]
            dcw_ref[CONV_W - 1:CONV_W, sl] += jnp.sum(dconv * cur, axis=0, keepdims=True)
            for s in range(1, CONV_W):
                rolled = pltpu.roll(dconv, CHUNK - s, 0)
                bot = jnp.where(row8 < _TAIL - s, rolled[CHUNK - _TAIL:], pltpu.roll(nxt, _TAIL - s, 0))
                up = jnp.concatenate([rolled[:CHUNK - _TAIL], bot], axis=0)
                dxin = dxin + up * cw_ref[CONV_W - 1 - s:CONV_W - s, sl]
                dcw_ref[CONV_W - 1 - s:CONV_W - s, sl] += jnp.sum(up * cur, axis=0, keepdims=True)
            dcb_ref[:, sl] += jnp.sum(dconv, axis=0, keepdims=True)
            dxbc_ref[:, sl] = dxin.astype(bf16)
            dcnext_ref[:, sl] = dconv[:_TAIL]
        if ride is not None:
            pl.when(i == nc - 1)(finish)

    def chunk(w):
        return pl.BlockSpec((CHUNK, w), lambda i: (nc - 1 - i, 0))

    def const(shape):
        return pl.BlockSpec(shape, lambda i: (0,) * len(shape))

    riding = ride is not None
    return pl.pallas_call(
        body, name=name, grid=(nc,),
        in_specs=[chunk(CONV_DIM), chunk(CONV_DIM),
                  chunk(D_INNER), chunk(LANES), pl.BlockSpec((1, D_STATE, D_INNER), lambda i: (nc - 1 - i, 0, 0)),
                  chunk(D_INNER), const((CONV_W, CONV_DIM)),
                  const((1, LANES)), const((1, LANES)), const((1, D_INNER)), const((1, D_INNER)),
                  const((D_INNER, LANES))] + [_ANY] * riding,
        out_specs=[chunk(D_INNER), chunk(CONV_DIM), chunk(LANES), const((CONV_W, CONV_DIM)), const((1, CONV_DIM)),
                   const((1, LANES)), const((1, LANES)), const((1, LANES)), const((1, D_INNER))] + [_ANY] * riding,
        out_shape=[jax.ShapeDtypeStruct((t, D_INNER), bf16), jax.ShapeDtypeStruct((t, CONV_DIM), bf16),
                   jax.ShapeDtypeStruct((t, LANES), bf16), jax.ShapeDtypeStruct((CONV_W, CONV_DIM), f32),
                   jax.ShapeDtypeStruct((1, CONV_DIM), f32), jax.ShapeDtypeStruct((1, LANES), f32),
                   jax.ShapeDtypeStruct((1, LANES), f32), jax.ShapeDtypeStruct((1, LANES), f32),
                   jax.ShapeDtypeStruct((1, D_INNER), f32)]
        + ([jax.ShapeDtypeStruct((N_CHIPS - 1,) + ride.shape[1:], ride.dtype)] if riding else []),
        scratch_shapes=[pltpu.VMEM((D_STATE, D_INNER), f32), pltpu.VMEM((_TAIL, CONV_DIM), f32),
                        pltpu.VMEM((CHUNK, CONV_DIM), f32), pltpu.VMEM((CHUNK, CONV_DIM), f32),
                        pltpu.VMEM((CHUNK, D_INNER), f32), pltpu.VMEM((CHUNK, D_INNER), f32),
                        pltpu.VMEM((_TAIL, D_INNER), f32)]
        + (list(_SCATTER_SCRATCH) if riding else []),
        compiler_params=_params(("arbitrary",)),
    )(xbc, cv, z, dtr, hprev, dyb, cw, dtb, alog, dsk_x, gs, seg, *([ride] if riding else []))


def _adamw(w, g, m, v, *, name):
    r, c = w.shape
    tr = r
    while tr * c * 4 > _MB and tr % 16 == 0:
        tr //= 2

    def body(w_ref, g_ref, m_ref, v_ref, d_ref, m2_ref, v2_ref):
        gv = g_ref[...]
        m2 = ADAM_B1 * m_ref[...] + (1.0 - ADAM_B1) * gv
        v2 = ADAM_B2 * v_ref[...] + (1.0 - ADAM_B2) * (gv * gv)
        m_hat = m2 / (1.0 - ADAM_B1 ** ADAM_STEP)
        v_hat = v2 / (1.0 - ADAM_B2 ** ADAM_STEP)
        d_ref[...] = -ADAM_LR * (m_hat / (jnp.sqrt(v_hat) + ADAM_EPS) + ADAM_WD * w_ref[...])
        m2_ref[...] = m2
        v2_ref[...] = v2

    blk = pl.BlockSpec((tr, c), lambda i: (i, 0))
    return pl.pallas_call(
        body, name=name, grid=(r // tr,),
        in_specs=[blk] * 4, out_specs=[blk] * 3,
        out_shape=[jax.ShapeDtypeStruct((r, c), f32)] * 3,
        compiler_params=_params(("parallel",)),
    )(w, g, m, v)


def _row_block(rows, cols):
    cap = max(16, _MB // (4 * cols))
    return max(tr for tr in range(16, min(cap, rows) + 1, 16) if rows % tr == 0)


def _cast_bf16(a, *, name):
    r, c = a.shape
    tr = _row_block(r, c)

    def body(a_ref, o_ref):
        o_ref[...] = a_ref[...].astype(bf16)

    blk = pl.BlockSpec((tr, c), lambda i: (i, 0))
    return pl.pallas_call(
        body, name=name, grid=(r // tr,), in_specs=[blk], out_specs=blk,
        out_shape=jax.ShapeDtypeStruct((r, c), bf16), compiler_params=_params(("parallel",)),
    )(a)


_ANY = pl.BlockSpec(memory_space=pl.ANY)


def _place():
    x, y, c = lax.axis_index("x"), lax.axis_index("y"), lax.axis_index("c")
    other_chips = [(1 - x, y), (x, 1 - y), (1 - x, 1 - y)]
    return x, y, c, other_chips


def _gather_protocol(in_ref, out_ref, send_sems, recv_sems):
    x, y, c, chips = _place()
    me = 2 * x + y
    sibling = (x, y, 1 - c)

    def cp(k, chip, half, to, src=None):
        dst = out_ref.at[chip, half]
        return pltpu.make_async_remote_copy(
            src_ref=dst if src is None else src, dst_ref=dst, send_sem=send_sems.at[k], recv_sem=recv_sems.at[k],
            device_id=to, device_id_type=MESH)

    def sends():
        return [cp(j, me, c, (cx, cy, c), src=in_ref.at[c]) for j, (cx, cy) in enumerate(chips)]

    def relays():
        return [cp(3 + j, 2 * cx + cy, c, sibling) for j, (cx, cy) in enumerate(chips)]

    def start():
        for f in sends():
            f.start()

    def relay():
        onward = relays()
        for j, (cx, cy) in enumerate(chips):
            cp(j, 2 * cx + cy, c, sibling).wait_recv()
            onward[j].start()

    def finish():
        for j, (cx, cy) in enumerate(chips):
            cp(3 + j, 2 * cx + cy, 1 - c, sibling).wait_recv()
        for f in sends() + relays():
            f.wait_send()

    return start, relay, finish


_GATHER_SCRATCH = [pltpu.SemaphoreType.DMA((6,)), pltpu.SemaphoreType.DMA((6,))]


def _gather_shards(shard, *, name):
    _, rh, lanes = shard.shape

    def body(in_ref, out_ref, send_sems, recv_sems):
        start, relay, finish = _gather_protocol(in_ref, out_ref, send_sems, recv_sems)
        start()
        relay()
        finish()

    return pl.pallas_call(
        body, name=name, in_specs=[_ANY], out_specs=_ANY,
        out_shape=jax.ShapeDtypeStruct((N_CHIPS, 2, rh, lanes), shard.dtype),
        scratch_shapes=list(_GATHER_SCRATCH),
    )(shard)


def _scatter_protocol(p_ref, out_ref, send_sems, recv_sems):
    x, y, c, chips = _place()

    def copies():
        return [pltpu.make_async_remote_copy(
            src_ref=p_ref.at[2 * cx + cy], dst_ref=out_ref.at[j], send_sem=send_sems.at[j], recv_sem=recv_sems.at[j],
            device_id=(cx, cy, c), device_id_type=MESH) for j, (cx, cy) in enumerate(chips)]

    def start():
        for cpy in copies():
            cpy.start()

    def finish():
        for cpy in copies():
            cpy.wait()

    return start, finish


_SCATTER_SCRATCH = [pltpu.SemaphoreType.DMA((3,)), pltpu.SemaphoreType.DMA((3,))]


def _swap_protocol(g_ref, out_ref, send_sems, recv_sems):
    x, y, c, _ = _place()

    def copies():
        return [pltpu.make_async_remote_copy(
            src_ref=g_ref.at[k, 1 - c], dst_ref=out_ref.at[k], send_sem=send_sems.at[k], recv_sem=recv_sems.at[k],
            device_id=(x, y, 1 - c), device_id_type=MESH) for k in range(N_CHIPS)]

    def start():
        for cpy in copies():
            cpy.start()

    def finish():
        for cpy in copies():
            cpy.wait()

    return start, finish


_SWAP_SCRATCH = [pltpu.SemaphoreType.DMA((N_CHIPS,)), pltpu.SemaphoreType.DMA((N_CHIPS,))]


def _rs_swap_halves(g, *, name):
    nch, _, rh, lanes = g.shape

    def body(g_ref, out_ref, send_sems, recv_sems):
        start, finish = _swap_protocol(g_ref, out_ref, send_sems, recv_sems)
        start()
        finish()

    return pl.pallas_call(
        body, name=name, in_specs=[_ANY], out_specs=_ANY,
        out_shape=jax.ShapeDtypeStruct((nch, rh, lanes), g.dtype),
        scratch_shapes=list(_SWAP_SCRATCH),
    )(g)


def _rs_add_pair(g, got, c_idx, *, name):
    nch, _, rh, lanes = g.shape
    tr = _row_block(rh, lanes)

    def body(c_ref, g_ref, got_ref, p16_ref):
        p16_ref[...] = (g_ref[...] + got_ref[...]).astype(bf16)

    blk = pl.BlockSpec((None, tr, lanes), lambda k, i, c_ref: (k, i, 0))
    return pl.pallas_call(
        body, name=name,
        grid_spec=pltpu.PrefetchScalarGridSpec(
            num_scalar_prefetch=1, grid=(nch, rh // tr),
            in_specs=[pl.BlockSpec((None, None, tr, lanes), lambda k, i, c_ref: (k, c_ref[0], i, 0)), blk],
            out_specs=blk),
        out_shape=jax.ShapeDtypeStruct((nch, rh, lanes), bf16),
        compiler_params=_params(("parallel", "parallel")),
    )(c_idx, g, got)


def _rs_add_chips(g, got_pair, got, place, *, name):
    _, _, rh, lanes = g.shape
    tr = _row_block(rh, lanes)

    def body(place_ref, g_ref, pair_ref, got_ref, o_ref):
        own = g_ref[...] + pair_ref[...]
        o_ref[...] = ((own + got_ref[0].astype(f32)) + got_ref[1].astype(f32)) + got_ref[2].astype(f32)

    return pl.pallas_call(
        body, name=name,
        grid_spec=pltpu.PrefetchScalarGridSpec(
            num_scalar_prefetch=1, grid=(rh // tr,),
            in_specs=[pl.BlockSpec((None, None, tr, lanes), lambda i, place_ref: (place_ref[0], place_ref[1], i, 0)),
                      pl.BlockSpec((None, tr, lanes), lambda i, place_ref: (place_ref[0], i, 0)),
                      pl.BlockSpec((3, tr, lanes), lambda i, place_ref: (0, i, 0))],
            out_specs=pl.BlockSpec((None, tr, lanes), lambda i, place_ref: (place_ref[1], i, 0))),
        out_shape=jax.ShapeDtypeStruct((2, rh, lanes), f32),
        compiler_params=_params(("parallel",)),
    )(place, g, got_pair, got)


def _rs_join_halves(halves, *, name):
    def body(h_ref, out_ref, send_sem, recv_sem):
        x, y, c, _ = _place()
        cpy = pltpu.make_async_remote_copy(
            src_ref=h_ref.at[c], dst_ref=out_ref.at[c], send_sem=send_sem, recv_sem=recv_sem,
            device_id=(x, y, 1 - c), device_id_type=MESH)
        cpy.start()
        cpy.wait()

    return pl.pallas_call(
        body, name=name, in_specs=[_ANY], out_specs=_ANY,
        out_shape=jax.ShapeDtypeStruct(halves.shape, halves.dtype), input_output_aliases={0: 0},
        scratch_shapes=[pltpu.SemaphoreType.DMA, pltpu.SemaphoreType.DMA],
    )(halves)


def _all_reduce_small(s, *, name):
    rs, lanes = s.shape
    rh = rs // 2

    def body(s_ref, o_ref, sib_ref, mine_ref, chips_ref, send_sems, recv_sems):
        x, y, c, chips = _place()
        me = 2 * x + y
        sibling = (x, y, 1 - c)
        rows = pl.ds(pl.multiple_of(c * rh, 8), rh)

        def cp(k, src, dst, to):
            return pltpu.make_async_remote_copy(src_ref=src, dst_ref=dst, send_sem=send_sems.at[k],
                                                recv_sem=recv_sems.at[k], device_id=to, device_id_type=MESH)

        swap = cp(0, s_ref, sib_ref, sibling)
        swap.start()
        swap.wait()
        mine_ref[...] = s_ref[rows, :] + sib_ref[rows, :]
        sends = [cp(1 + j, mine_ref, chips_ref.at[j], (cx, cy, c)) for j, (cx, cy) in enumerate(chips)]
        for cpy in sends:
            cpy.start()
        for cpy in sends:
            cpy.wait()
        where = [2 * cx + cy for cx, cy in chips]
        total = None
        for q in range(N_CHIPS):
            term = jnp.where(q == me, mine_ref[...], jnp.where(
                q == where[0], chips_ref[0], jnp.where(q == where[1], chips_ref[1], chips_ref[2])))
            total = term if total is None else total + term
        o_ref[rows, :] = total
        push = cp(4, o_ref.at[rows, :], o_ref.at[rows, :], sibling)
        push.start()
        push.wait()

    vm = pl.BlockSpec(memory_space=pltpu.VMEM)
    return pl.pallas_call(
        body, name=name, in_specs=[vm], out_specs=vm,
        out_shape=jax.ShapeDtypeStruct((rs, lanes), f32),
        scratch_shapes=[pltpu.VMEM((rs, lanes), f32), pltpu.VMEM((rh, lanes), f32),
                        pltpu.VMEM((N_CHIPS - 1, rh, lanes), f32), pltpu.SemaphoreType.DMA((5,)),
                        pltpu.SemaphoreType.DMA((5,))],
        compiler_params=pltpu.CompilerParams(vmem_limit_bytes=32 * _MB),
    )(s)


def _pad_lanes(a, width=LANES):
    return jnp.pad(a, ((0, 0), (0, width - a.shape[1])))


def _local_grads(x, tgt, wts, small, *, fwd_ride=None, late_weights=None, swap_ride=None, bwd_ride=None,
                 last_ride=None):
    t = x.shape[0]
    tm = min(t, 1024)
    d = D_MODEL
    mm = functools.partial(_matmul, tm=tm)

    dtb = _pad_lanes(small["dt_bias"])
    alog = _pad_lanes(small["a_log"])
    dsk = jnp.repeat(small["d_skip"], HEAD_DIM, axis=1)
    bsp_t = _pad_lanes(small["b_spatial"].T)
    wsp = small["w_spatial"]

    h = _rms_fwd(x, small["norm_mix_g"], name="rms_mix")
    uv = mm(h, wts["uv"], tn=1024, tk=d, out_dtypes=[f32], name="proj_uv")
    z = mm(h, wts["z"], tn=1024, tk=d, out_dtypes=[f32], name="proj_z")
    xbc = mm(h, wts["xbc"], tn=1024, tk=d, out_dtypes=[f32], name="proj_xbc")
    dtr = mm(h, wts["dt"], tn=LANES, tk=d, out_dtypes=[f32], name="proj_dt")
    gl = mm(h, wts["gate"], tn=1024, tk=d, out_dtypes=[f32], name="proj_gate")
    ya = _gmlp_fwd(uv, small["v_norm_g"], small["v_norm_b"], wsp, bsp_t, name="gmlp_fwd")
    yb, hprev, cv, *gathered = _ssd_fwd(xbc, z, dtr, small["conv_w"], small["conv_b"], dtb, alog, dsk,
                                        small["ssm_norm_g"], ride=fwd_ride, name="ssd_fwd")
    if fwd_ride is not None:
        wts = {**wts, **late_weights(gathered[0])}
    pa = mm(ya, wts["pa"], tn=1024, tk=1024, out_dtypes=[f32], name="proj_a")
    tm_gate = min(t, 512)
    row_vec = [pl.BlockSpec((1, d), lambda i, j, k, half=half: (0, half)) for half in range(2)]
    gate_tiles = [pl.BlockSpec((tm_gate, d), lambda i, j, k, half=half: (i, half)) for half in range(2)]

    def merge(pb_acc, pa_t, gla, glb, bga, bgb):
        return pb_acc, _sigmoid(gla + bga) * pa_t + _sigmoid(glb + bgb) * pb_acc

    pb, merged = _matmul(yb, wts["pb"], tm=tm_gate, tn=d, tk=1024, out_dtypes=[f32, bf16], epilogue=merge,
                         extras=[pa, gl, gl, small["b_gates"], small["b_gates"]],
                         extra_specs=[None] + gate_tiles + row_vec, name="proj_b")

    def residual_norm(acc, res, g):
        x_new = res + acc
        r = lax.rsqrt(jnp.mean(x_new * x_new, axis=1, keepdims=True) + NORM_EPS)
        return x_new, x_new * r * g

    x1, h2 = mm(merged, wts["out"], tn=d, tk=1024, out_dtypes=[f32, bf16], epilogue=residual_norm,
                extras=[x, small["norm_mlp_g"]], extra_specs=[None, row_vec[0]], name="out_proj")
    act = mm(h2, wts["up"], tn=1024, tk=d, out_dtypes=[bf16],
             epilogue=lambda acc: (jnp.square(jnp.maximum(acc, 0.0)),), name="mlp_up")
    x2 = mm(act, wts["down"], tn=1024, tk=2048, out_dtypes=[f32], extras=[x1],
            epilogue=lambda acc, res: (res + acc,), name="mlp_down")

    dx2, dx2b, dgf, loss = _loss_head(x2, tgt, small["norm_final_g"], name="loss_head")
    tt = min(t, 2048)
    tn_mm = functools.partial(_matmul_tn, tt=tt)
    dw = {}
    dw["down"] = tn_mm(act, dx2b, tka=1024, tn=1024, name="dw_down")
    dup = mm(dx2b, wts["down"], nt=True, tn=1024, tk=1024, out_dtypes=[bf16], extras=[act],
             epilogue=lambda acc, a2: (acc * (2.0 * jnp.sqrt(a2).astype(f32)),), name="d_act")
    dw["up"] = tn_mm(h2, dup, tka=1024, tn=1024, name="dw_up")
    dh2 = mm(dup, wts["up"], nt=True, tn=1024, tk=2048, out_dtypes=[f32], name="d_h2")
    dx1, dx1b, dg_mlp = _rms_bwd(x1, small["norm_mlp_g"], dh2, dx2, want_bf16=True, name="rms_mlp_bwd")
    dw["out"] = tn_mm(merged, dx1b, tka=1024, tn=1024, name="dw_out")
    dmerged = mm(dx1b, wts["out"], nt=True, tn=1024, tk=1024, out_dtypes=[f32], name="d_merged")
    dpa, dpb, dgl, dbg = _merge_bwd(dmerged, pa, pb, gl, small["b_gates"], name="merge_bwd")
    dw["pa"] = tn_mm(ya, dpa, tka=1024, tn=1024, name="dw_pa")
    dw["pb"] = tn_mm(yb, dpb, tka=1024, tn=1024, name="dw_pb")
    dya = mm(dpa, wts["pa"], nt=True, tn=1024, tk=1024, out_dtypes=[f32], name="d_ya")
    dyb = mm(dpb, wts["pb"], nt=True, tn=1024, tk=1024, out_dtypes=[f32], name="d_yb")
    swapped = swap_ride(dw) if swap_ride is not None else None
    duv, dwsp, dbsp_t, dvg, dvb, *got_pair = _gmlp_bwd(uv, dya, small["v_norm_g"], small["v_norm_b"], wsp, bsp_t,
                                                       ride=swapped, name="gmlp_bwd")
    ride = bwd_ride(swapped, got_pair[0]) if bwd_ride is not None else None
    dz, dxbc, ddt, dcw, dcb, ddtb, dalog, ddsk, dgs, *got = _ssd_bwd(
        xbc, cv, z, dtr, hprev, dyb, small["conv_w"], dtb, alog, dsk, small["ssm_norm_g"],
        _head_seg_matrix(), ride=ride, name="ssd_bwd")
    dw["uv"] = tn_mm(h, duv, tka=1024, tn=1024, name="dw_uv")
    dw["z"] = tn_mm(h, dz, tka=1024, tn=1024, name="dw_z")
    dw["xbc"] = tn_mm(h, dxbc, tka=1024, tn=1024, name="dw_xbc")
    dw["dt"] = tn_mm(h, ddt, tka=1024, tn=LANES, name="dw_dt")
    dw["gate"] = tn_mm(h, dgl, tka=1024, tn=1024, name="dw_gate")
    last = last_ride(dw) if last_ride is not None else None
    res = _matmul_nt_sum(
        [(duv, wts["uv"]), (dz, wts["z"]), (dxbc, wts["xbc"]), (dgl, wts["gate"]), (ddt, wts["dt"])],
        tm=tm, tks=[1024] * 4 + [LANES], ride=last, name="d_h")
    dh, got_last = (res[0], res[1]) if last is not None else (res, None)
    dx, dg_mix = _rms_bwd(x, small["norm_mix_g"], dh, dx1, want_bf16=False, name="rms_mix_bwd")

    dsmall = {
        "norm_mix_g": dg_mix, "conv_w": dcw, "conv_b": dcb, "dt_bias": ddtb[:, :N_HEADS], "a_log": dalog[:, :N_HEADS],
        "d_skip": ddsk[:, :N_HEADS], "ssm_norm_g": dgs, "v_norm_g": dvg, "v_norm_b": dvb, "w_spatial": dwsp,
        "b_spatial": dbsp_t[:, :GMLP_GROUPS].T, "b_gates": dbg, "norm_mlp_g": dg_mlp, "norm_final_g": dgf,
    }
    return loss, dx, dw, dsmall, (got[0] if got else None), got_last


_IN_SHARD = IN_PROJ // N_CHIPS
_LATE = ("w_proj_a", "w_proj_b", "w_out", "w_mlp_up", "w_mlp_down")
_LATE_ROWS = {"w_proj_a": GMLP_WIDTH // N_CHIPS, "w_proj_b": D_INNER // N_CHIPS, "w_out": D_MODEL // N_CHIPS,
              "w_mlp_up": D_MODEL, "w_mlp_down": D_FF // N_CHIPS}
_LATE_TOTAL = sum(_LATE_ROWS.values())


def _late_offsets():
    off, out = 0, {}
    for k in _LATE:
        out[k] = off
        off += _LATE_ROWS[k]
    return out


_LATE_OFF = _late_offsets()

_SMALL = ("norm_mix_g", "conv_w", "conv_b", "dt_bias", "a_log", "d_skip", "ssm_norm_g", "v_norm_g", "v_norm_b",
          "w_spatial", "b_spatial", "b_gates", "norm_mlp_g", "norm_final_g")


def _pack_small(parts):
    flat = jnp.concatenate([parts[k].reshape(-1) for k in _SMALL])
    rows = -(-flat.shape[0] // (16 * LANES)) * 16
    return jnp.pad(flat, (0, rows * LANES - flat.shape[0])).reshape(rows, LANES)


def _unpack_small(packed, shapes):
    flat = packed.reshape(-1)
    out, off = {}, 0
    for k in _SMALL:
        n = math.prod(shapes[k])
        out[k] = flat[off:off + n].reshape(shapes[k])
        off += n
    return out


def _from_chip_columns(stacked):
    _, rows, cols = stacked.shape
    return stacked.transpose(1, 0, 2).reshape(rows, N_CHIPS * cols)


def _to_chip_columns(full):
    rows, cols = full.shape
    return full.reshape(rows, N_CHIPS, cols // N_CHIPS).transpose(1, 0, 2)


def _w_in_grad_by_chip(dw):
    pieces = [dw["uv"], dw["z"], dw["xbc"], dw["dt"][:, :N_HEADS], dw["gate"]]
    bounds = [0]
    for p in pieces:
        bounds.append(bounds[-1] + p.shape[1])
    chips = []
    for k in range(N_CHIPS):
        lo, hi = k * _IN_SHARD, (k + 1) * _IN_SHARD
        parts = [p[:, max(lo, b0) - b0:min(hi, b1) - b0]
                 for p, b0, b1 in zip(pieces, bounds[:-1], bounds[1:]) if min(hi, b1) > max(lo, b0)]
        chips.append(jnp.concatenate(parts, axis=1))
    return jnp.stack(chips)


def kernel(x, norm_mix_g, w_in, conv_w, conv_b, dt_bias, a_log, d_skip, ssm_norm_g, v_norm_g, v_norm_b, w_spatial, b_spatial, b_gates, w_proj_a, w_proj_b, w_out, norm_mlp_g, w_mlp_up, w_mlp_down, norm_final_g, loss_target, m_norm_mix_g, m_w_in, m_conv_w, m_conv_b, m_dt_bias, m_a_log, m_d_skip, m_ssm_norm_g, m_v_norm_g, m_v_norm_b, m_w_spatial, m_b_spatial, m_b_gates, m_w_proj_a, m_w_proj_b, m_w_out, m_norm_mlp_g, m_w_mlp_up, m_w_mlp_down, m_norm_final_g, v_norm_mix_g, v_w_in, v_conv_w, v_conv_b, v_dt_bias, v_a_log, v_d_skip, v_ssm_norm_g, v_v_norm_g, v_v_norm_b, v_w_spatial, v_b_spatial, v_b_gates, v_w_proj_a, v_w_proj_b, v_w_out, v_norm_mlp_g, v_w_mlp_up, v_w_mlp_down, v_norm_final_g):
    given = dict(locals())
    names = ("norm_mix_g", "w_in", "conv_w", "conv_b", "dt_bias", "a_log", "d_skip", "ssm_norm_g", "v_norm_g",
             "v_norm_b", "w_spatial", "b_spatial", "b_gates", "w_proj_a", "w_proj_b", "w_out", "norm_mlp_g",
             "w_mlp_up", "w_mlp_down", "norm_final_g")
    xi, yi, ci = lax.axis_index("x"), lax.axis_index("y"), lax.axis_index("c")
    me_chip = (2 * xi + yi).astype(jnp.int32)

    def halves(a):
        return a.reshape(2, a.shape[0] // 2, a.shape[1])

    def with_own(got, shard):
        whole = lax.dynamic_update_slice(got, shard[None], (me_chip, 0, 0, 0))
        return whole.reshape(N_CHIPS, 2 * shard.shape[1], shard.shape[2])

    shard_in = halves(_cast_bf16(w_in[0], name="cast_w_in"))
    shard_late = halves(_cast_bf16(jnp.concatenate([given[k][0] for k in _LATE]), name="cast_w_late"))
    shard_conv = halves(conv_w.reshape(2 * _TAIL, -1))
    w_in_full = _from_chip_columns(with_own(_gather_shards(shard_in, name="gather_w_in"), shard_in))
    o_dt, o_gate = 2 * GMLP_WIDTH + D_INNER + CONV_DIM, 2 * GMLP_WIDTH + D_INNER + CONV_DIM + N_HEADS
    wts = {
        "uv": w_in_full[:, :2 * GMLP_WIDTH], "z": w_in_full[:, 2 * GMLP_WIDTH:2 * GMLP_WIDTH + D_INNER],
        "xbc": w_in_full[:, 2 * GMLP_WIDTH + D_INNER:o_dt], "dt": _pad_lanes(w_in_full[:, o_dt:o_gate]),
        "gate": w_in_full[:, o_gate:],
    }
    conv_all = with_own(_gather_shards(shard_conv, name="gather_conv_w"), shard_conv)
    conv_full = _from_chip_columns(conv_all.reshape(N_CHIPS, CONV_W, CONV_DIM // N_CHIPS))

    def late_weights(got):
        g_late = with_own(got, shard_late)

        def rows_of(k):
            return g_late[:, _LATE_OFF[k]:_LATE_OFF[k] + _LATE_ROWS[k]]

        return {
            "pa": rows_of("w_proj_a").reshape(GMLP_WIDTH, D_MODEL),
            "pb": rows_of("w_proj_b").reshape(D_INNER, D_MODEL), "out": rows_of("w_out").reshape(D_MODEL, D_MODEL),
            "up": _from_chip_columns(rows_of("w_mlp_up")), "down": rows_of("w_mlp_down").reshape(D_FF, D_MODEL),
        }

    small = {
        "norm_mix_g": norm_mix_g, "conv_w": conv_full, "conv_b": conv_b, "dt_bias": dt_bias, "a_log": a_log,
        "d_skip": d_skip, "ssm_norm_g": ssm_norm_g, "v_norm_g": v_norm_g, "v_norm_b": v_norm_b,
        "w_spatial": w_spatial[0], "b_spatial": b_spatial[0], "b_gates": b_gates, "norm_mlp_g": norm_mlp_g,
        "norm_final_g": norm_final_g.reshape(1, D_MODEL),
    }

    c_idx = ci.astype(jnp.int32).reshape(1)
    place = jnp.stack([me_chip, ci.astype(jnp.int32)])
    partials = {}

    def reduced_shard(tag, got_chips):
        own = _rs_add_chips(*partials[tag], got_chips, place, name="rs_add_chips_" + tag)
        both = _rs_join_halves(own, name="rs_join_" + tag)
        return both.reshape(2 * both.shape[1], both.shape[2])

    def late_grads(dw):
        def by_rows(a):
            return a.reshape(N_CHIPS, a.shape[0] // N_CHIPS, a.shape[1])

        g = jnp.concatenate([by_rows(dw["pa"]), by_rows(dw["pb"]), by_rows(dw["out"]), _to_chip_columns(dw["up"]),
                             by_rows(dw["down"])], axis=1)
        return g.reshape(N_CHIPS, 2, g.shape[1] // 2, g.shape[2])

    def late_partials(g, got_pair):
        partials["late"] = (g, got_pair)
        return _rs_add_pair(g, got_pair, c_idx, name="rs_add_pair_late")

    def in_partials(dw):
        g = _w_in_grad_by_chip(dw).reshape(N_CHIPS, 2, D_MODEL // 2, _IN_SHARD)
        got_pair = _rs_swap_halves(g, name="rs_swap_in")
        partials["in"] = (g, got_pair)
        return _rs_add_pair(g, got_pair, c_idx, name="rs_add_pair_in")

    loss_part, grad_x, dw, dsmall, got_late, got_in = _local_grads(
        x[0], loss_target[0], wts, small, fwd_ride=shard_late, late_weights=late_weights, swap_ride=late_grads,
        bwd_ride=late_partials, last_ride=in_partials)
    loss = lax.psum(loss_part[0, 0], ("x", "y", "c"))
    g_late = reduced_shard("late", got_late)
    g_in_shard = reduced_shard("in", got_in)

    small_shapes = {k: dsmall[k].shape for k in _SMALL}
    red = _unpack_small(_all_reduce_small(_pack_small(dsmall), name="all_reduce_small"), small_shapes)
    conv_cols = CONV_DIM // N_CHIPS
    red["conv_w"] = lax.dynamic_slice_in_dim(red["conv_w"], me_chip * conv_cols, conv_cols, axis=1)

    grads, deltas, new_m, new_v = {}, {}, {}, {}
    for k in ("w_in",) + _LATE:
        g2 = g_in_shard if k == "w_in" else g_late[_LATE_OFF[k]:_LATE_OFF[k] + _LATE_ROWS[k]]
        dlt, m2, v2 = _adamw(given[k][0], g2, given["m_" + k][0], given["v_" + k][0], name="adamw_" + k)
        grads[k], deltas[k], new_m[k], new_v[k] = g2, dlt, m2, v2
    adam_shapes = dict(small_shapes)
    adam_shapes["conv_w"] = (CONV_W, conv_cols)

    def small_pack_of(prefix):
        return _pack_small({k: given[prefix + k].reshape(adam_shapes[k]) for k in _SMALL})

    dlt_s, m_s, v_s = _adamw(small_pack_of(""), _pack_small(red), small_pack_of("m_"), small_pack_of("v_"),
                             name="adamw_small")
    for dst, packed in ((deltas, dlt_s), (new_m, m_s), (new_v, v_s)):
        dst.update(_unpack_small(packed, adam_shapes))
    grads.update(red)

    def shaped(dct):
        return [dct[k].reshape(given[k].shape) for k in names]

    return (loss, grad_x[None], *shaped(grads), *shaped(deltas), *shaped(new_m), *shaped(new_v))
```

```python
import functools
import math

import jax
import jax.numpy as jnp
from jax import lax
from jax.experimental import pallas as pl
from jax.experimental.pallas import tpu as pltpu

f32 = jnp.float32
bf16 = jnp.bfloat16

D_MODEL = 1024
CHUNK = 128
GMLP_WIDTH = 1024
GMLP_GROUPS = 8
D_INNER = 2048
HEAD_DIM = 64
N_HEADS = 32
N_GROUPS = 8
HEADS_PER_GROUP = 4
GROUP_W = HEADS_PER_GROUP * HEAD_DIM
D_STATE = 128
CONV_W = 4
CONV_DIM = 4096
D_FF = 4096
IN_PROJ = 10272
NORM_EPS = 1e-6
N_CHIPS = 4
N_DEV = 8
LANES = 128

ADAM_LR = 0.001
ADAM_B1 = 0.9
ADAM_B2 = 0.999
ADAM_EPS = 1e-08
ADAM_WD = 0.01
ADAM_STEP = 10

MESH = pl.DeviceIdType.MESH
_NT = (((1,), (1,)), ((), ()))
_NN = (((1,), (0,)), ((), ()))
_TN = (((0,), (0,)), ((), ()))
_MB = 2 ** 20


def _params(sem, vmem_mb=48):
    return pltpu.CompilerParams(dimension_semantics=sem, vmem_limit_bytes=vmem_mb * _MB)


def _dot(a, b, dims=_NN):
    return lax.dot_general(a.astype(bf16), b.astype(bf16), dims, preferred_element_type=f32)


def _dot32(a, b):
    return jnp.dot(a, b, preferred_element_type=f32, precision=lax.Precision.HIGHEST)


def _sigmoid(x):
    return 1.0 / (1.0 + jnp.exp(-x))


def _sum_all(a):
    return jnp.sum(jnp.sum(a, axis=1, keepdims=True), axis=0, keepdims=True)


def _iota(shape, dim):
    return lax.broadcasted_iota(jnp.int32, shape, dim)


def _matmul(a, b, *, nt=False, tm, tn, tk, out_dtypes, epilogue=None, extras=(), extra_specs=None, n_sums=0, name):
    m, k_dim = a.shape
    n = b.shape[0] if nt else b.shape[1]
    nk = k_dim // tk
    ne, no = len(extras), len(out_dtypes)
    dims = _NT if nt else _NN
    assert n_sums == 0 or n == tn, "row-vector sums stay resident only when one tile spans N"

    def body(*refs):
        a_ref, b_ref = refs[0], refs[1]
        ex = refs[2:2 + ne]
        outs = refs[2 + ne:2 + ne + no]
        sums =refs[2 + ne + no:2 + ne + no + n_sums]

        def finish(acc):
            vals = epilogue(acc, *[e[...] for e in ex]) if epilogue is not None else (acc,)
            for o, v in zip(outs, vals[:no]):
                o[...] = v.astype(o.dtype)
            for s_ref, v in zip(sums, vals[no:]):
                @pl.when(pl.program_id(0) == 0)
                def _(s_ref=s_ref, v=v):
                    s_ref[...] = v

                @pl.when(pl.program_id(0) > 0)
                def _(s_ref=s_ref, v=v):
                    s_ref[...] += v

        part = lax.dot_general(a_ref[...], b_ref[...], dims, preferred_element_type=f32)
        if nk == 1:
            finish(part)
        else:
            acc_ref = refs[-1]
            kk = pl.program_id(2)

            @pl.when(kk == 0)
            def _():
                acc_ref[...] = part

            @pl.when(kk > 0)
            def _():
                acc_ref[...] += part

            @pl.when(kk == nk - 1)
            def _():
                finish(acc_ref[...])

    b_spec = pl.BlockSpec((tn, tk), lambda i, j, k: (j, k)) if nt else pl.BlockSpec((tk, tn), lambda i, j, k: (k, j))
    tile = pl.BlockSpec((tm, tn), lambda i, j, k: (i, j))
    ex_specs = [tile if s is None else s for s in (extra_specs or [None] * ne)]
    outs = pl.pallas_call(
        body, name=name, grid=(m // tm, n // tn, nk),
        in_specs=[pl.BlockSpec((tm, tk), lambda i, j, k: (i, k)), b_spec] + ex_specs,
        out_specs=[tile] * no + [pl.BlockSpec((1, tn), lambda i, j, k: (0, j))] * n_sums,
        out_shape=[jax.ShapeDtypeStruct((m, n), dt) for dt in out_dtypes]
        + [jax.ShapeDtypeStruct((1, n), f32)] * n_sums,
        scratch_shapes=[pltpu.VMEM((tm, tn), f32)] if nk > 1 else [],
        compiler_params=_params(("arbitrary",) * 3 if n_sums else ("parallel", "parallel", "arbitrary")),
    )(a, b, *extras)
    return outs if no + n_sums > 1 else outs[0]


def _matmul_nt_sum(pairs, *, tm, tks, ride=None, name):
    m = pairs[0][0].shape[0]
    n = pairs[0][1].shape[0]
    nblk = [a.shape[1] // tk for (a, _), tk in zip(pairs, tks)]
    starts = [sum(nblk[:p]) for p in range(len(pairs))]
    nk = sum(nblk)
    npairs = len(pairs)
    ni = m // tm
    riding = ride is not None

    def body(*refs):
        rest = refs[2 * npairs:]
        if riding:
            ride_ref, o_ref, got_ref, acc_ref, send_sems, recv_sems = rest
        else:
            o_ref, acc_ref = rest
        i, kk = pl.program_id(0), pl.program_id(1)
        if riding:
            start, finish = _scatter_protocol(ride_ref, got_ref, send_sems, recv_sems)
            pl.when((i == 0) & (kk == 0))(start)

        @pl.when(kk == 0)
        def _():
            acc_ref[...] = jnp.zeros_like(acc_ref)

        for p in range(npairs):
            @pl.when((kk >= starts[p]) & (kk < starts[p] + nblk[p]))
            def _(p=p):
                acc_ref[...] += lax.dot_general(refs[2 * p][...], refs[2 * p + 1][...], _NT, preferred_element_type=f32)

        @pl.when(kk == nk - 1)
        def _():
            o_ref[...] = acc_ref[...]

        if riding:
            pl.when((i == ni - 1) & (kk == nk - 1))(finish)

    in_specs, args = [], []
    for p, (a, b) in enumerate(pairs):
        def kblock(k, s=starts[p], nb=nblk[p]):
            return jnp.clip(k - s, 0, nb - 1)
        in_specs.append(pl.BlockSpec((tm, tks[p]), lambda i, k, kb=kblock: (i, kb(k))))
        in_specs.append(pl.BlockSpec((n, tks[p]), lambda i, k, kb=kblock: (0, kb(k))))
        args += [a, b]
    tile = pl.BlockSpec((tm, n), lambda i, k: (i, 0))
    outs = pl.pallas_call(
        body, name=name, grid=(ni, nk), in_specs=in_specs + [_ANY] * riding, out_specs=[tile] + [_ANY] * riding,
        out_shape=[jax.ShapeDtypeStruct((m, n), f32)]
        + ([jax.ShapeDtypeStruct((N_CHIPS - 1,) + ride.shape[1:], ride.dtype)] if riding else []),
        scratch_shapes=[pltpu.VMEM((tm, n), f32)] + (list(_SCATTER_SCRATCH) if riding else []),
        compiler_params=_params(("arbitrary", "arbitrary"), vmem_mb=56),
    )(*args, *([ride] if riding else []))
    return outs if riding else outs[0]


def _matmul_tn(a, b, *, tka, tn, tt, name):
    t, ka = a.shape
    n = b.shape[1]

    def body(a_ref, b_ref, o_ref):
        part = lax.dot_general(a_ref[...], b_ref[...], _TN, preferred_element_type=f32)
        kk = pl.program_id(2)

        @pl.when(kk == 0)
        def _():
            o_ref[...] = part

        @pl.when(kk > 0)
        def _():
            o_ref[...] += part

    return pl.pallas_call(
        body, name=name, grid=(ka // tka, n // tn, t // tt),
        in_specs=[pl.BlockSpec((tt, tka), lambda i, j, k: (k, i)), pl.BlockSpec((tt, tn), lambda i, j, k: (k, j))],
        out_specs=pl.BlockSpec((tka, tn), lambda i, j, k: (i, j)),
        out_shape=jax.ShapeDtypeStruct((ka, n), f32),
        compiler_params=_params(("parallel", "parallel", "arbitrary")),
    )(a, b)


def _row_tile(t):
    return min(t, 512)


def _rms_fwd(x, g, *, name):
    t, d = x.shape
    tr = _row_tile(t)

    def body(x_ref, g_ref, h_ref):
        xv = x_ref[...]
        r = lax.rsqrt(jnp.mean(xv * xv, axis=1, keepdims=True) + NORM_EPS)
        h_ref[...] = (xv * r * g_ref[...]).astype(bf16)

    return pl.pallas_call(
        body, name=name, grid=(t // tr,),
        in_specs=[pl.BlockSpec((tr, d), lambda i: (i, 0)), pl.BlockSpec((1, d), lambda i: (0, 0))],
        out_specs=pl.BlockSpec((tr, d), lambda i: (i, 0)),
        out_shape=jax.ShapeDtypeStruct((t, d), bf16),
        compiler_params=_params(("parallel",)),
    )(x, g)


def _rms_bwd(xin, g, dh, dres, *, want_bf16, name):
    t, d = xin.shape
    tr = _row_tile(t)

    def body(x_ref, g_ref, dh_ref, dres_ref, dx_ref, *rest):
        dg_ref = rest[-1]
        xv = x_ref[...]
        r = lax.rsqrt(jnp.mean(xv * xv, axis=1, keepdims=True) + NORM_EPS)
        xn = xv * r
        dhv = dh_ref[...]
        dxn = dhv * g_ref[...]
        dx = dres_ref[...] + r * (dxn - xn * jnp.mean(dxn * xn, axis=1, keepdims=True))
        dx_ref[...] = dx
        if want_bf16:
            rest[0][...] = dx.astype(bf16)
        part = jnp.sum(dhv * xn, axis=0, keepdims=True)

        @pl.when(pl.program_id(0) == 0)
        def _():
            dg_ref[...] = part

        @pl.when(pl.program_id(0) > 0)
        def _():
            dg_ref[...] += part

    row = pl.BlockSpec((tr, d), lambda i: (i, 0))
    vec = pl.BlockSpec((1, d), lambda i: (0, 0))
    out_shape = [jax.ShapeDtypeStruct((t, d), f32)] + ([jax.ShapeDtypeStruct((t, d), bf16)] if want_bf16 else []) \
        + [jax.ShapeDtypeStruct((1, d), f32)]
    return pl.pallas_call(
        body, name=name, grid=(t // tr,),
        in_specs=[row, vec, row, row],
        out_specs=[row] + ([row] if want_bf16 else []) + [vec],
        out_shape=out_shape,
        compiler_params=_params(("arbitrary",)),
    )(xin, g, dh, dres)


def _merge_bwd(dm, pa, pb, gl, bg, *, name):
    t, d = pa.shape
    tr = _row_tile(t)

    def body(dm_ref, pa_ref, pb_ref, gla_ref, glb_ref, bga_ref, bgb_ref, dpa_ref, dpb_ref, dgl_ref, dbg_ref):
        dmv = dm_ref[...]
        ga = _sigmoid(gla_ref[...] + bga_ref[...])
        gb = _sigmoid(glb_ref[...] + bgb_ref[...])
        dpa_ref[...] = (dmv * ga).astype(bf16)
        dpb_ref[...] = (dmv * gb).astype(bf16)
        dla = dmv * pa_ref[...] * ga * (1.0 - ga)
        dlb = dmv * pb_ref[...] * gb * (1.0 - gb)
        dgl_ref[:, :d] = dla.astype(bf16)
        dgl_ref[:, d:] = dlb.astype(bf16)
        sa = jnp.sum(dla, axis=0, keepdims=True)
        sb = jnp.sum(dlb, axis=0, keepdims=True)

        @pl.when(pl.program_id(0) == 0)
        def _():
            dbg_ref[:, :d] = sa
            dbg_ref[:, d:] = sb

        @pl.when(pl.program_id(0) > 0)
        def _():
            dbg_ref[:, :d] += sa
            dbg_ref[:, d:] += sb

    row = pl.BlockSpec((tr, d), lambda i: (i, 0))
    return pl.pallas_call(
        body, name=name, grid=(t // tr,),
        in_specs=[row, row, row, row, pl.BlockSpec((tr, d), lambda i: (i, 1)),
                  pl.BlockSpec((1, d), lambda i: (0, 0)), pl.BlockSpec((1, d), lambda i: (0, 1))],
        out_specs=[row, row, pl.BlockSpec((tr, 2 * d), lambda i: (i, 0)), pl.BlockSpec((1, 2 * d), lambda i: (0, 0))],
        out_shape=[jax.ShapeDtypeStruct((t, d), bf16), jax.ShapeDtypeStruct((t, d), bf16),
                   jax.ShapeDtypeStruct((t, 2 * d), bf16), jax.ShapeDtypeStruct((1, 2 * d), f32)],
        compiler_params=_params(("arbitrary",)),
    )(dm, pa, pb, gl, gl, bg, bg)


_INV_SQRT2 = 1.0 / math.sqrt(2.0)
_INV_SQRT2PI = 1.0 / math.sqrt(2.0 * math.pi)


def _normal_cdf(x):
    return 0.5 * (1.0 + lax.erf(x * _INV_SQRT2))


def _gmlp_norm(uv, cdf, vg, vb):
    zz = uv * cdf
    u = zz[:, :GMLP_WIDTH]
    v = zz[:, GMLP_WIDTH:]
    mu = jnp.mean(v, axis=1, keepdims=True)
    vc = v - mu
    rstd = lax.rsqrt(jnp.mean(vc * vc, axis=1, keepdims=True) + NORM_EPS)
    vhat = vc * rstd
    vn = vhat * vg + vb
    return u, vhat, rstd, vn


def _gmlp_fwd(uv, vg, vb, wsp, bsp_t, *, name):
    t = uv.shape[0]
    nc = t // CHUNK

    def body(uv_ref, vg_ref, vb_ref, w_ref, b_ref, y_ref, cdf_ref):
        uvv = uv_ref[...]
        cdf = _normal_cdf(uvv)
        cdf_ref[...] = cdf
        u, _, _, vn = _gmlp_norm(uvv, cdf, vg_ref[...], vb_ref[...])
        tril = _iota((CHUNK, CHUNK), 0) >= _iota((CHUNK, CHUNK), 1)
        bt = b_ref[...]
        for g in range(GMLP_GROUPS):
            sl = slice(g * CHUNK, (g + 1) * CHUNK)
            w = jnp.where(tril, w_ref[g], 0.0)
            s = _dot(w, vn[:, sl]) + bt[:, g:g + 1]
            y_ref[:, sl] = (u[:, sl] * s).astype(bf16)

    return pl.pallas_call(
        body, name=name, grid=(nc,),
        in_specs=[pl.BlockSpec((CHUNK, 2 * GMLP_WIDTH), lambda c: (c, 0)),
                  pl.BlockSpec((1, GMLP_WIDTH), lambda c: (0, 0)), pl.BlockSpec((1, GMLP_WIDTH), lambda c: (0, 0)),
                  pl.BlockSpec((GMLP_GROUPS, CHUNK, CHUNK), lambda c: (0, 0, 0)),
                  pl.BlockSpec((CHUNK, LANES), lambda c: (0, 0))],
        out_specs=[pl.BlockSpec((CHUNK, GMLP_WIDTH), lambda c: (c, 0)),
                   pl.BlockSpec((CHUNK, 2 * GMLP_WIDTH), lambda c: (c, 0))],
        out_shape=[jax.ShapeDtypeStruct((t, GMLP_WIDTH), bf16), jax.ShapeDtypeStruct((t, 2 * GMLP_WIDTH), f32)],
        compiler_params=_params(("parallel",)),
    )(uv, vg, vb, wsp, bsp_t)


def _gmlp_bwd(uv, cdf, dya, vg, vb, wsp, bsp_t, *, ride=None, name):
    t = uv.shape[0]
    nc = t // CHUNK
    riding = ride is not None

    def body(*refs):
        uv_ref, cdf_ref, dy_ref, vg_ref, vb_ref, w_ref, b_ref = refs[:7]
        duv_ref, dw_ref, db_ref, dvg_ref, dvb_ref = refs[7 + riding:12 + riding]
        first = pl.program_id(0) == 0
        if riding:
            start, finish = _swap_protocol(refs[7], refs[13], refs[14], refs[15])
            pl.when(first)(start)

        @pl.when(first)
        def _():
            dw_ref[...] = jnp.zeros_like(dw_ref)
            db_ref[...] = jnp.zeros_like(db_ref)
            dvg_ref[...] = jnp.zeros_like(dvg_ref)
            dvb_ref[...] = jnp.zeros_like(dvb_ref)

        uvv = uv_ref[...]
        vgv = vg_ref[...]
        cdf = cdf_ref[...]
        u, vhat, rstd, vn = _gmlp_norm(uvv, cdf, vgv, vb_ref[...])
        gelu_grad = cdf + uvv * jnp.exp(-0.5 * uvv * uvv) * _INV_SQRT2PI
        dy = dy_ref[...]
        tril = _iota((CHUNK, CHUNK), 0) >= _iota((CHUNK, CHUNK), 1)
        lane = _iota((CHUNK, LANES), 1)
        bt = b_ref[...]
        ds_all = dy * u
        dbacc = jnp.zeros((CHUNK, LANES), f32)
        dvh_parts = []
        for g in range(GMLP_GROUPS):
            sl = slice(g * CHUNK, (g + 1) * CHUNK)
            w = jnp.where(tril, w_ref[g], 0.0)
            vng = vn[:, sl]
            s = _dot(w, vng) + bt[:, g:g + 1]
            ds = ds_all[:, sl]
            duv_ref[:, sl] = (dy[:, sl] * s * gelu_grad[:, sl]).astype(bf16)
            dw_ref[g] += jnp.where(tril, _dot(ds, vng, _NT), 0.0)
            dbacc = dbacc + jnp.where(lane == g, jnp.sum(ds, axis=1, keepdims=True), 0.0)
            dvn = _dot(w, ds, _TN)
            vh = vhat[:, sl]
            dvg_ref[:, sl] += jnp.sum(dvn * vh, axis=0, keepdims=True)
            dvb_ref[:, sl] += jnp.sum(dvn, axis=0, keepdims=True)
            dvh_parts.append(dvn * vgv[:, sl])
        db_ref[...] += dbacc
        dvhat = jnp.concatenate(dvh_parts, axis=1)
        m1 = jnp.mean(dvhat, axis=1, keepdims=True)
        m2 = jnp.mean(dvhat * vhat, axis=1, keepdims=True)
        dv = rstd * (dvhat - m1 - vhat * m2)
        duv_ref[:, GMLP_WIDTH:] = (dv * gelu_grad[:, GMLP_WIDTH:]).astype(bf16)
        if riding:
            pl.when(pl.program_id(0) == nc - 1)(finish)

    vec = pl.BlockSpec((1, GMLP_WIDTH), lambda c: (0, 0))
    return pl.pallas_call(
        body, name=name, grid=(nc,),
        in_specs=[pl.BlockSpec((CHUNK, 2 * GMLP_WIDTH), lambda c: (c, 0)),
                  pl.BlockSpec((CHUNK, 2 * GMLP_WIDTH), lambda c: (c, 0)),
                  pl.BlockSpec((CHUNK, GMLP_WIDTH), lambda c: (c, 0)), vec, vec,
                  pl.BlockSpec((GMLP_GROUPS, CHUNK, CHUNK), lambda c: (0, 0, 0)),
                  pl.BlockSpec((CHUNK, LANES), lambda c: (0, 0))] + [_ANY] * riding,
        out_specs=[pl.BlockSpec((CHUNK, 2 * GMLP_WIDTH), lambda c: (c, 0)),
                   pl.BlockSpec((GMLP_GROUPS, CHUNK, CHUNK), lambda c: (0, 0, 0)),
                   pl.BlockSpec((CHUNK, LANES), lambda c: (0, 0)), vec, vec] + [_ANY] * riding,
        out_shape=[jax.ShapeDtypeStruct((t, 2 * GMLP_WIDTH), bf16),
                   jax.ShapeDtypeStruct((GMLP_GROUPS, CHUNK, CHUNK), f32),
                   jax.ShapeDtypeStruct((CHUNK, LANES), f32),
                   jax.ShapeDtypeStruct((1, GMLP_WIDTH), f32), jax.ShapeDtypeStruct((1, GMLP_WIDTH), f32)]
        + ([jax.ShapeDtypeStruct(ride.shape[:1] + ride.shape[2:], ride.dtype)] if riding else []),
        scratch_shapes=list(_SWAP_SCRATCH) if riding else [],
        compiler_params=_params(("arbitrary",)),
    )(uv, cdf, dya, vg, vb, wsp, bsp_t, *([ride] if riding else []))


_CONV_COLS = 512
_XS0, _B0, _C0 = 0, D_INNER, D_INNER + N_GROUPS * D_STATE


_TAIL = 8


def _conv_silu(cur_ref, tail_ref, w_ref, b_ref, has_prev, xc_ref, cv_ref):
    row = _iota((_TAIL, _CONV_COLS), 0)
    for j in range(CONV_DIM // _CONV_COLS):
        sl = slice(j * _CONV_COLS, (j + 1) * _CONV_COLS)
        cur = cur_ref[:, sl]
        tail = jnp.where(has_prev, tail_ref[:, sl], 0.0)
        acc = cur * w_ref[CONV_W - 1:CONV_W, sl] + b_ref[:, sl]
        for s in range(1, CONV_W):
            rolled = pltpu.roll(cur, s, 0)
            top = jnp.where(row >= s, rolled[:_TAIL], pltpu.roll(tail, s, 0))
            sh = jnp.concatenate([top, rolled[_TAIL:]], axis=0)
            acc = acc + sh * w_ref[CONV_W - 1 - s:CONV_W - s, sl]
        cv_ref[:, sl] = acc
        xc_ref[:, sl] = acc * _sigmoid(acc)


def _col_bcast(mat, h):
    return jnp.broadcast_to(mat[:, h:h + 1], (CHUNK, LANES))


def _head_expand(cols):
    lo = _iota((CHUNK, LANES), 1) < HEAD_DIM
    return jnp.concatenate([jnp.where(lo, cols[2 * j], cols[2 * j + 1]) for j in range(N_HEADS // 2)], axis=1)


def _ssd_chunk_scalars(dtr, dtb, alog):
    xdt_pre = dtr + dtb
    dtv = jnp.maximum(xdt_pre, 0.0) + jnp.log(1.0 + jnp.exp(-jnp.abs(xdt_pre)))
    a = -jnp.exp(alog)
    ltri = (_iota((CHUNK, CHUNK), 0) >= _iota((CHUNK, CHUNK), 1)).astype(f32)
    cs = _dot32(ltri, dtv * a)
    csb = [_col_bcast(cs, h) for h in range(N_HEADS)]
    cs_x = _head_expand(csb)
    dt_x = _head_expand([_col_bcast(dtv, h) for h in range(N_HEADS)])
    cl_x = cs_x[CHUNK - 1:CHUNK, :]
    return dict(xdt_pre=xdt_pre, dtv=dtv, a=a, cs=cs, cs_t=cs.T, csb=csb, dt_x=dt_x, e_x=jnp.exp(cs_x),
                dec_x=jnp.exp(cl_x - cs_x), dk_x=jnp.exp(cl_x))


def _head_masks():
    lane = _iota((CHUNK, GROUP_W), 1)
    return [(lane >= r * HEAD_DIM) & (lane < (r + 1) * HEAD_DIM) for r in range(HEADS_PER_GROUP)]


def _stack_heads(a, masks):
    return jnp.concatenate([jnp.where(m, a, 0.0) for m in masks], axis=0).astype(bf16)


def _seg_sum(a, seg):
    hi = a.astype(jnp.bfloat16)
    lo = (a - hi.astype(f32)).astype(jnp.bfloat16)
    return (lax.dot_general(hi, seg, _NN, preferred_element_type=f32)
            + lax.dot_general(lo, seg, _NN, preferred_element_type=f32))


def _head_seg_matrix():
    return (_iota((D_INNER, LANES), 0) // HEAD_DIM == _iota((D_INNER, LANES), 1)).astype(jnp.bfloat16)


def _ssd_fwd(xbc, z, dtr, cw, cb, dtb, alog, dsk_x, gs, *, ride=None, name):
    t = xbc.shape[0]
    nc = t // CHUNK
    tiles = CHUNK // _TAIL

    def body(*refs):
        cur_ref, tail_ref, z_ref, dtr_ref, cw_ref, cb_ref, dtb_ref, alog_ref, dsk_ref, gs_ref = refs[:10]
        if ride is None:
            yb_ref, hp_ref, cv_ref, state_ref, xc_ref = refs[10:]
        else:
            ride_ref, yb_ref, hp_ref, cv_ref, got_ref, state_ref, xc_ref, send_sems, recv_sems = refs[10:]
        c = pl.program_id(0)
        if ride is not None:
            start, relay, finish = _gather_protocol(ride_ref, got_ref, send_sems, recv_sems)
            pl.when(c == 0)(start)
            pl.when(c == nc // 2)(relay)

        @pl.when(c == 0)
        def _():
            state_ref[...] = jnp.zeros_like(state_ref)

        _conv_silu(cur_ref, tail_ref, cw_ref, cb_ref, c > 0, xc_ref, cv_ref)
        sc = _ssd_chunk_scalars(dtr_ref[...], dtb_ref[...], alog_ref[...])
        tril = _iota((CHUNK, CHUNK), 0) >= _iota((CHUNK, CHUNK), 1)
        masks = _head_masks()
        hp_ref[0] = state_ref[...]
        for g in range(N_GROUPS):
            gsl = slice(g * GROUP_W, (g + 1) * GROUP_W)
            xs_g = xc_ref[:, gsl]
            bg = xc_ref[:, _B0 + g * D_STATE:_B0 + (g + 1) * D_STATE]
            cg = xc_ref[:, _C0 + g * D_STATE:_C0 + (g + 1) * D_STATE]
            xdt_g = xs_g * sc["dt_x"][:, gsl]
            cbm = _dot(cg, bg, _NT)
            mw = jnp.concatenate(
                [cbm * jnp.exp(jnp.where(tril, sc["csb"][h] - sc["cs_t"][h:h + 1, :], -1e30))
                 for h in range(g * HEADS_PER_GROUP, (g + 1) * HEADS_PER_GROUP)], axis=1)
            ht_g = state_ref[:, gsl]
            y_g = _dot(mw, _stack_heads(xdt_g, masks)) + sc["e_x"][:, gsl] * _dot(cg, ht_g) + dsk_ref[:, gsl] * xs_g
            state_ref[:, gsl] = ht_g * sc["dk_x"][:, gsl] + _dot(bg, xdt_g * sc["dec_x"][:, gsl], _TN)
            zg = z_ref[:, gsl]
            yg = y_g * zg * _sigmoid(zg)
            rs = lax.rsqrt(jnp.mean(yg * yg, axis=1, keepdims=True) + NORM_EPS)
            yb_ref[:, gsl] = (yg * rs * gs_ref[:, gsl]).astype(bf16)
        if ride is not None:
            pl.when(c == nc - 1)(finish)

    def chunk(w):
        return pl.BlockSpec((CHUNK, w), lambda c: (c, 0))

    def const(shape):
        return pl.BlockSpec(shape, lambda c: (0,) * len(shape))

    riding = ride is not None
    return pl.pallas_call(
        body, name=name, grid=(nc,),
        in_specs=[chunk(CONV_DIM), pl.BlockSpec((_TAIL, CONV_DIM), lambda c: (jnp.maximum(c * tiles - 1, 0), 0)),
                  chunk(D_INNER), chunk(LANES), const((CONV_W, CONV_DIM)), const((1, CONV_DIM)),
                  const((1, LANES)), const((1, LANES)), const((1, D_INNER)), const((1, D_INNER))] + [_ANY] * riding,
        out_specs=[chunk(D_INNER), pl.BlockSpec((1, D_STATE, D_INNER), lambda c: (c, 0, 0)), chunk(CONV_DIM)]
        + [_ANY] * riding,
        out_shape=[jax.ShapeDtypeStruct((t, D_INNER), bf16), jax.ShapeDtypeStruct((nc, D_STATE, D_INNER), f32),
                   jax.ShapeDtypeStruct((t, CONV_DIM), f32)]
        + ([jax.ShapeDtypeStruct((N_CHIPS,) + ride.shape, ride.dtype)] if riding else []),
        scratch_shapes=[pltpu.VMEM((D_STATE, D_INNER), f32), pltpu.VMEM((CHUNK, CONV_DIM), f32)]
        + (list(_GATHER_SCRATCH) if riding else []),
        compiler_params=_params(("arbitrary",)),
    )(xbc, xbc, z, dtr, cw, cb, dtb, alog, dsk_x, gs, *([ride] if riding else []))


def _ssd_bwd(xbc, cv, z, dtr, hprev, dyb, cw, dtb, alog, dsk_x, gs, seg, *, ride=None, name):
    t = xbc.shape[0]
    nc = t // CHUNK

    def body(*refs):
        (cur_ref, cv_ref, z_ref, dtr_ref, hp_ref, dyb_ref, cw_ref, dtb_ref, alog_ref, dsk_ref, gs_ref,
         seg_ref) = refs[:12]
        rest = refs[12:]
        if ride is not None:
            ride_ref, got_ref, send_sems, recv_sems = rest[0], rest[10], rest[-2], rest[-1]
            rest = rest[1:10] + rest[11:-2]
        (dz_ref, dxbc_ref, ddt_ref, dcw_ref, dcb_ref, ddtb_ref, dalog_ref, ddsk_ref, dgs_ref,
         dh_ref, dcnext_ref, xc_ref, dxc_ref, x13_ref, x2_ref, rows_ref) = rest
        i = pl.program_id(0)
        if ride is not None:
            start, finish = _scatter_protocol(ride_ref, got_ref, send_sems, recv_sems)
            pl.when(i == 0)(start)

        @pl.when(i == 0)
        def _():
            for ref in (dh_ref, dcnext_ref, dcw_ref, dcb_ref, ddtb_ref, dalog_ref, ddsk_ref, dgs_ref, rows_ref):
                ref[...] = jnp.zeros_like(ref)

        for j in range(CONV_DIM // _CONV_COLS):
            sl = slice(j * _CONV_COLS, (j + 1) * _CONV_COLS)
            cvv = cv_ref[:, sl]
            xc_ref[:, sl] = cvv * _sigmoid(cvv)
        sc = _ssd_chunk_scalars(dtr_ref[...], dtb_ref[...], alog_ref[...])
        tril = _iota((CHUNK, CHUNK), 0) >= _iota((CHUNK, CHUNK), 1)
        triu = _iota((CHUNK, CHUNK), 0) <= _iota((CHUNK, CHUNK), 1)
        masks = _head_masks()
        rowh = _iota((N_HEADS, CHUNK), 0)
        dcs_t = jnp.zeros((N_HEADS, CHUNK), f32)
        for g in range(N_GROUPS):
            gsl = slice(g * GROUP_W, (g + 1) * GROUP_W)
            xs_g = xc_ref[:, gsl]
            bg = xc_ref[:, _B0 + g * D_STATE:_B0 + (g + 1) * D_STATE]
            cg = xc_ref[:, _C0 + g * D_STATE:_C0 + (g + 1) * D_STATE]
            dt_g, e_g, dec_g, dk_g = sc["dt_x"][:, gsl], sc["e_x"][:, gsl], sc["dec_x"][:, gsl], sc["dk_x"][:, gsl]
            dsk_g = dsk_ref[:, gsl]
            xdt_g = xs_g * dt_g
            xdt_stack = _stack_heads(xdt_g, masks)
            cbm = _dot(cg, bg, _NT)
            cbt = _dot(bg, cg, _NT)
            heads = range(g * HEADS_PER_GROUP, (g + 1) * HEADS_PER_GROUP)
            lmats = [jnp.exp(jnp.where(tril, sc["csb"][h] - sc["cs_t"][h:h + 1, :], -1e30)) for h in heads]
            mw = jnp.concatenate([cbm * lm for lm in lmats], axis=1)
            mtw = jnp.concatenate(
                [cbt * jnp.exp(jnp.where(triu, sc["cs_t"][h:h + 1, :] - sc["csb"][h], -1e30)) for h in heads], axis=1)
            ht_g = hp_ref[0, :, gsl]
            dhn_g = dh_ref[:, gsl]
            yoff = e_g * _dot(cg, ht_g)
            y_g = _dot(mw, xdt_stack) + yoff + dsk_g * xs_g
            zg = z_ref[:, gsl]
            sz = _sigmoid(zg)
            silu = zg * sz
            yg = y_g * silu
            rs = lax.rsqrt(jnp.mean(yg * yg, axis=1, keepdims=True) + NORM_EPS)
            yn = yg * rs
            dyb = dyb_ref[:, gsl]
            dgs_ref[:, gsl] += jnp.sum(dyb * yn, axis=0, keepdims=True)
            dyn = dyb * gs_ref[:, gsl]
            dyg = rs * (dyn - yn * jnp.mean(dyn * yn, axis=1, keepdims=True))
            dy_g = dyg * silu
            dz_ref[:, gsl] = (dyg * y_g * (sz * (1.0 + zg * (1.0 - sz)))).astype(bf16)
            dy_stack = _stack_heads(dy_g, masks)
            dm_w = _dot(dy_g, xdt_stack, _NT)
            dmt_w = _dot(xdt_g, dy_stack, _NT)
            dxdt = _dot(mtw, dy_stack)
            dcb_acc = jnp.zeros((CHUNK, CHUNK), f32)
            for r, h in enumerate(heads):
                hs = slice(r * CHUNK, (r + 1) * CHUNK)
                dml = dm_w[:, hs] * lmats[r]
                dcb_acc = dcb_acc + dml
                col = jnp.sum(dml * cbm, axis=0, keepdims=True)
                row = jnp.sum(dmt_w[:, hs] * mtw[:, hs], axis=0, keepdims=True)
                dcs_t = dcs_t + jnp.where(rowh == h, row - col, 0.0)
            w = _dot(bg, dhn_g)
            dxdt = dxdt + dec_g * w
            decx3 = dec_g * (xdt_g * w)
            dg_g = e_g * dy_g
            d_c = _dot(dg_g, ht_g, _NT) + _dot(dcb_acc, bg)
            d_b = _dot(dcb_acc, cg, _TN) + _dot(xdt_g * dec_g, dhn_g, _NT)
            dh_ref[:, gsl] = dhn_g * dk_g + _dot(cg, dg_g, _TN)
            dxc_ref[:, gsl] = dsk_g * dy_g + dxdt * dt_g
            dxc_ref[:, _B0 + g * D_STATE:_B0 + (g + 1) * D_STATE] = d_b
            dxc_ref[:, _C0 + g * D_STATE:_C0 + (g + 1) * D_STATE] = d_c
            x13_ref[:, gsl] = dy_g * yoff - decx3
            x2_ref[:, gsl] = dxdt * xs_g
            rows_ref[0:1, gsl] = jnp.sum(dhn_g * ht_g, axis=0, keepdims=True)
            rows_ref[1:2, gsl] = jnp.sum(decx3, axis=0, keepdims=True)
            rows_ref[2:3, gsl] = jnp.sum(dy_g * xs_g, axis=0, keepdims=True)
        segm = seg_ref[...]
        r13 = _seg_sum(x13_ref[...], segm)
        r2 = _seg_sum(x2_ref[...], segm)
        small = _seg_sum(rows_ref[...], segm)
        lane = _iota((CHUNK, LANES), 1)
        rowi = _iota((CHUNK, LANES), 0)
        dcl_row = small[0:1, :] * jnp.exp(sc["cs"][CHUNK - 1:CHUNK, :]) + small[1:2, :]
        dcs = r13 + jnp.where(rowi == CHUNK - 1, dcl_row, 0.0)
        dcs_t_all = dcs.T + jnp.concatenate([dcs_t, jnp.zeros((LANES - N_HEADS, CHUNK), f32)], axis=0)
        dda = _dot32(dcs_t_all, tril.astype(f32)).T
        a = sc["a"]
        ddt_total = r2 + dda * a
        dalog_ref[...] += jnp.sum(dda * sc["dtv"], axis=0, keepdims=True) * a
        ddtr = jnp.where(lane < N_HEADS, ddt_total * _sigmoid(sc["xdt_pre"]), 0.0)
        ddtb_ref[...] += jnp.sum(ddtr, axis=0, keepdims=True)
        ddt_ref[...] = ddtr.astype(bf16)
        ddsk_ref[...] += small[2:3, :]
        row8 = _iota((_TAIL, _CONV_COLS), 0)
        for j in range(CONV_DIM // _CONV_COLS):
            sl = slice(j * _CONV_COLS, (j + 1) * _CONV_COLS)
            cvv = cv_ref[:, sl]
            sg = _sigmoid(cvv)
            dconv = dxc_ref[:, sl] * (sg * (1.0 + cvv * (1.0 - sg)))
            nxt = dcnext_ref[:, sl]
            cur = cur_ref[:, sl]
            dxin = dconv * cw_ref[CONV_W - 1:CONV_W, sl]
            dcw_ref[CONV_W - 1:CONV_W, sl] += jnp.sum(dconv * cur, axis=0, keepdims=True)
            for s in range(1, CONV_W):
                rolled = pltpu.roll(dconv, CHUNK - s, 0)
                bot = jnp.where(row8 < _TAIL - s, rolled[CHUNK - _TAIL:], pltpu.roll(nxt, _TAIL - s, 0))
                up = jnp.concatenate([rolled[:CHUNK - _TAIL], bot], axis=0)
                dxin = dxin + up * cw_ref[CONV_W - 1 - s:CONV_W - s, sl]
                dcw_ref[CONV_W - 1 - s:CONV_W - s, sl] += jnp.sum(up * cur, axis=0, keepdims=True)
            dcb_ref[:, sl] += jnp.sum(dconv, axis=0, keepdims=True)
            dxbc_ref[:, sl] = dxin.astype(bf16)
            dcnext_ref[:, sl] = dconv[:_TAIL]
        if ride is not None:
            pl.when(i == nc - 1)(finish)

    def chunk(w):
        return pl.BlockSpec((CHUNK, w), lambda i: (nc - 1 - i, 0))

    def const(shape):
        return pl.BlockSpec(shape, lambda i: (0,) * len(shape))

    riding = ride is not None
    return pl.pallas_call(
        body, name=name, grid=(nc,),
        in_specs=[chunk(CONV_DIM), chunk(CONV_DIM),
                  chunk(D_INNER), chunk(LANES), pl.BlockSpec((1, D_STATE, D_INNER), lambda i: (nc - 1 - i, 0, 0)),
                  chunk(D_INNER), const((CONV_W, CONV_DIM)),
                  const((1, LANES)), const((1, LANES)), const((1, D_INNER)), const((1, D_INNER)),
                  const((D_INNER, LANES))] + [_ANY] * riding,
        out_specs=[chunk(D_INNER), chunk(CONV_DIM), chunk(LANES), const((CONV_W, CONV_DIM)), const((1, CONV_DIM)),
                   const((1, LANES)), const((1, LANES)), const((1, LANES)), const((1, D_INNER))] + [_ANY] * riding,
        out_shape=[jax.ShapeDtypeStruct((t, D_INNER), bf16), jax.ShapeDtypeStruct((t, CONV_DIM), bf16),
                   jax.ShapeDtypeStruct((t, LANES), bf16), jax.ShapeDtypeStruct((CONV_W, CONV_DIM), f32),
                   jax.ShapeDtypeStruct((1, CONV_DIM), f32), jax.ShapeDtypeStruct((1, LANES), f32),
                   jax.ShapeDtypeStruct((1, LANES), f32), jax.ShapeDtypeStruct((1, LANES), f32),
                   jax.ShapeDtypeStruct((1, D_INNER), f32)]
        + ([jax.ShapeDtypeStruct((N_CHIPS - 1,) + ride.shape[1:], ride.dtype)] if riding else []),
        scratch_shapes=[pltpu.VMEM((D_STATE, D_INNER), f32), pltpu.VMEM((_TAIL, CONV_DIM), f32),
                        pltpu.VMEM((CHUNK, CONV_DIM), f32), pltpu.VMEM((CHUNK, CONV_DIM), f32),
                        pltpu.VMEM((CHUNK, D_INNER), f32), pltpu.VMEM((CHUNK, D_INNER), f32),
                        pltpu.VMEM((_TAIL, D_INNER), f32)]
        + (list(_SCATTER_SCRATCH) if riding else []),
        compiler_params=_params(("arbitrary",)),
    )(xbc, cv, z, dtr, hprev, dyb, cw, dtb, alog, dsk_x, gs, seg, *([ride] if riding else []))


def _adamw(w, g, m, v, *, name):
    r, c = w.shape
    tr = r
    while tr * c * 4 > _MB and tr % 16 == 0:
        tr //= 2

    def body(w_ref, g_ref, m_ref, v_ref, d_ref, m2_ref, v2_ref):
        gv = g_ref[...]
        m2 = ADAM_B1 * m_ref[...] + (1.0 - ADAM_B1) * gv
        v2 = ADAM_B2 * v_ref[...] + (1.0 - ADAM_B2) * (gv * gv)
        m_hat = m2 / (1.0 - ADAM_B1 ** ADAM_STEP)
        v_hat = v2 / (1.0 - ADAM_B2 ** ADAM_STEP)
        d_ref[...] = -ADAM_LR * (m_hat / (jnp.sqrt(v_hat) + ADAM_EPS) + ADAM_WD * w_ref[...])
        m2_ref[...] = m2
        v2_ref[...] = v2

    blk = pl.BlockSpec((tr, c), lambda i: (i, 0))
    return pl.pallas_call(
        body, name=name, grid=(r // tr,),
        in_specs=[blk] * 4, out_specs=[blk] * 3,
        out_shape=[jax.ShapeDtypeStruct((r, c), f32)] * 3,
        compiler_params=_params(("parallel",)),
    )(w, g, m, v)


def _row_block(rows, cols):
    cap = max(16, _MB // (4 * cols))
    return max(tr for tr in range(16, min(cap, rows) + 1, 16) if rows % tr == 0)


def _cast_bf16(a, *, name):
    r, c = a.shape
    tr = _row_block(r, c)

    def body(a_ref, o_ref):
        o_ref[...] = a_ref[...].astype(bf16)

    blk = pl.BlockSpec((tr, c), lambda i: (i, 0))
    return pl.pallas_call(
        body, name=name, grid=(r // tr,), in_specs=[blk], out_specs=blk,
        out_shape=jax.ShapeDtypeStruct((r, c), bf16), compiler_params=_params(("parallel",)),
    )(a)


_ANY = pl.BlockSpec(memory_space=pl.ANY)


def _place():
    x, y, c = lax.axis_index("x"), lax.axis_index("y"), lax.axis_index("c")
    other_chips = [(1 - x, y), (x, 1 - y), (1 - x, 1 - y)]
    return x, y, c, other_chips


def _gather_protocol(in_ref, out_ref, send_sems, recv_sems):
    x, y, c, chips = _place()
    me = 2 * x + y
    sibling = (x, y, 1 - c)

    def cp(k, chip, half, to, src=None):
        dst = out_ref.at[chip, half]
        return pltpu.make_async_remote_copy(
            src_ref=dst if src is None else src, dst_ref=dst, send_sem=send_sems.at[k], recv_sem=recv_sems.at[k],
            device_id=to, device_id_type=MESH)

    def sends():
        return [cp(j, me, c, (cx, cy, c), src=in_ref.at[c]) for j, (cx, cy) in enumerate(chips)]

    def relays():
        return [cp(3 + j, 2 * cx + cy, c, sibling) for j, (cx, cy) in enumerate(chips)]

    def start():
        for f in sends():
            f.start()

    def relay():
        onward = relays()
        for j, (cx, cy) in enumerate(chips):
            cp(j, 2 * cx + cy, c, sibling).wait_recv()
            onward[j].start()

    def finish():
        for j, (cx, cy) in enumerate(chips):
            cp(3 + j, 2 * cx + cy, 1 - c, sibling).wait_recv()
        for f in sends() + relays():
            f.wait_send()

    return start, relay, finish


_GATHER_SCRATCH = [pltpu.SemaphoreType.DMA((6,)), pltpu.SemaphoreType.DMA((6,))]


def _gather_shards(shard, *, name):
    _, rh, lanes = shard.shape

    def body(in_ref, out_ref, send_sems, recv_sems):
        start, relay, finish = _gather_protocol(in_ref, out_ref, send_sems, recv_sems)
        start()
        relay()
        finish()

    return pl.pallas_call(
        body, name=name, in_specs=[_ANY], out_specs=_ANY,
        out_shape=jax.ShapeDtypeStruct((N_CHIPS, 2, rh, lanes), shard.dtype),
        scratch_shapes=list(_GATHER_SCRATCH),
    )(shard)


def _scatter_protocol(p_ref, out_ref, send_sems, recv_sems):
    x, y, c, chips = _place()

    def copies():
        return [pltpu.make_async_remote_copy(
            src_ref=p_ref.at[2 * cx + cy], dst_ref=out_ref.at[j], send_sem=send_sems.at[j], recv_sem=recv_sems.at[j],
            device_id=(cx, cy, c), device_id_type=MESH) for j, (cx, cy) in enumerate(chips)]

    def start():
        for cpy in copies():
            cpy.start()

    def finish():
        for cpy in copies():
            cpy.wait()

    return start, finish


_SCATTER_SCRATCH = [pltpu.SemaphoreType.DMA((3,)), pltpu.SemaphoreType.DMA((3,))]


def _swap_protocol(g_ref, out_ref, send_sems, recv_sems):
    x, y, c, _ = _place()

    def copies():
        return [pltpu.make_async_remote_copy(
            src_ref=g_ref.at[k, 1 - c], dst_ref=out_ref.at[k], send_sem=send_sems.at[k], recv_sem=recv_sems.at[k],
            device_id=(x, y, 1 - c), device_id_type=MESH) for k in range(N_CHIPS)]

    def start():
        for cpy in copies():
            cpy.start()

    def finish():
        for cpy in copies():
            cpy.wait()

    return start, finish


_SWAP_SCRATCH = [pltpu.SemaphoreType.DMA((N_CHIPS,)), pltpu.SemaphoreType.DMA((N_CHIPS,))]


def _rs_swap_halves(g, *, name):
    nch, _, rh, lanes = g.shape

    def body(g_ref, out_ref, send_sems, recv_sems):
        start, finish = _swap_protocol(g_ref, out_ref, send_sems, recv_sems)
        start()
        finish()

    return pl.pallas_call(
        body, name=name, in_specs=[_ANY], out_specs=_ANY,
        out_shape=jax.ShapeDtypeStruct((nch, rh, lanes), g.dtype),
        scratch_shapes=list(_SWAP_SCRATCH),
    )(g)


def _rs_add_pair(g, got, c_idx, *, name):
    nch, _, rh, lanes = g.shape
    tr = _row_block(rh, lanes)

    def body(c_ref, g_ref, got_ref, p16_ref):
        p16_ref[...] = (g_ref[...] + got_ref[...]).astype(bf16)

    blk = pl.BlockSpec((None, tr, lanes), lambda k, i, c_ref: (k, i, 0))
    return pl.pallas_call(
        body, name=name,
        grid_spec=pltpu.PrefetchScalarGridSpec(
            num_scalar_prefetch=1, grid=(nch, rh // tr),
            in_specs=[pl.BlockSpec((None, None, tr, lanes), lambda k, i, c_ref: (k, c_ref[0], i, 0)), blk],
            out_specs=blk),
        out_shape=jax.ShapeDtypeStruct((nch, rh, lanes), bf16),
        compiler_params=_params(("parallel", "parallel")),
    )(c_idx, g, got)


def _rs_add_chips(g, got_pair, got, place, *, name):
    _, _, rh, lanes = g.shape
    tr = _row_block(rh, lanes)

    def body(place_ref, g_ref, pair_ref, got_ref, o_ref):
        own = g_ref[...] + pair_ref[...]
        o_ref[...] = ((own + got_ref[0].astype(f32)) + got_ref[1].astype(f32)) + got_ref[2].astype(f32)

    return pl.pallas_call(
        body, name=name,
        grid_spec=pltpu.PrefetchScalarGridSpec(
            num_scalar_prefetch=1, grid=(rh // tr,),
            in_specs=[pl.BlockSpec((None, None, tr, lanes), lambda i, place_ref: (place_ref[0], place_ref[1], i, 0)),
                      pl.BlockSpec((None, tr, lanes), lambda i, place_ref: (place_ref[0], i, 0)),
                      pl.BlockSpec((3, tr, lanes), lambda i, place_ref: (0, i, 0))],
            out_specs=pl.BlockSpec((None, tr, lanes), lambda i, place_ref: (place_ref[1], i, 0))),
        out_shape=jax.ShapeDtypeStruct((2, rh, lanes), f32),
        compiler_params=_params(("parallel",)),
    )(place, g, got_pair, got)


def _rs_join_halves(halves, *, name):
    def body(h_ref, out_ref, send_sem, recv_sem):
        x, y, c, _ = _place()
        cpy = pltpu.make_async_remote_copy(
            src_ref=h_ref.at[c], dst_ref=out_ref.at[c], send_sem=send_sem, recv_sem=recv_sem,
            device_id=(x, y, 1 - c), device_id_type=MESH)
        cpy.start()
        cpy.wait()

    return pl.pallas_call(
        body, name=name, in_specs=[_ANY], out_specs=_ANY,
        out_shape=jax.ShapeDtypeStruct(halves.shape, halves.dtype), input_output_aliases={0: 0},
        scratch_shapes=[pltpu.SemaphoreType.DMA, pltpu.SemaphoreType.DMA],
    )(halves)


def _all_reduce_small(s, *, name):
    rs, lanes = s.shape
    rh = rs // 2

    def body(s_ref, o_ref, sib_ref, mine_ref, chips_ref, send_sems, recv_sems):
        x, y, c, chips = _place()
        me = 2 * x + y
        sibling = (x, y, 1 - c)
        rows = pl.ds(pl.multiple_of(c * rh, 8), rh)

        def cp(k, src, dst, to):
            return pltpu.make_async_remote_copy(src_ref=src, dst_ref=dst, send_sem=send_sems.at[k],
                                                recv_sem=recv_sems.at[k], device_id=to, device_id_type=MESH)

        swap = cp(0, s_ref, sib_ref, sibling)
        swap.start()
        swap.wait()
        mine_ref[...] = s_ref[rows, :] + sib_ref[rows, :]
        sends = [cp(1 + j, mine_ref, chips_ref.at[j], (cx, cy, c)) for j, (cx, cy) in enumerate(chips)]
        for cpy in sends:
            cpy.start()
        for cpy in sends:
            cpy.wait()
        where = [2 * cx + cy for cx, cy in chips]
        total = None
        for q in range(N_CHIPS):
            term = jnp.where(q == me, mine_ref[...], jnp.where(
                q == where[0], chips_ref[0], jnp.where(q == where[1], chips_ref[1], chips_ref[2])))
            total = term if total is None else total + term
        o_ref[rows, :] = total
        push = cp(4, o_ref.at[rows, :], o_ref.at[rows, :], sibling)
        push.start()
        push.wait()

    vm = pl.BlockSpec(memory_space=pltpu.VMEM)
    return pl.pallas_call(
        body, name=name, in_specs=[vm], out_specs=vm,
        out_shape=jax.ShapeDtypeStruct((rs, lanes), f32),
        scratch_shapes=[pltpu.VMEM((rs, lanes), f32), pltpu.VMEM((rh, lanes), f32),
                        pltpu.VMEM((N_CHIPS - 1, rh, lanes), f32), pltpu.SemaphoreType.DMA((5,)),
                        pltpu.SemaphoreType.DMA((5,))],
        compiler_params=pltpu.CompilerParams(vmem_limit_bytes=32 * _MB),
    )(s)


def _pad_lanes(a, width=LANES):
    return jnp.pad(a, ((0, 0), (0, width - a.shape[1])))


def _local_grads(x, tgt, wts, small, *, fwd_ride=None, late_weights=None, swap_ride=None, bwd_ride=None,
                 last_ride=None):
    t = x.shape[0]
    tm = min(t, 1024)
    d = D_MODEL
    mm = functools.partial(_matmul, tm=tm)

    dtb = _pad_lanes(small["dt_bias"])
    alog = _pad_lanes(small["a_log"])
    dsk = jnp.repeat(small["d_skip"], HEAD_DIM, axis=1)
    bsp_t = _pad_lanes(small["b_spatial"].T)
    wsp = small["w_spatial"]

    h = _rms_fwd(x, small["norm_mix_g"], name="rms_mix")
    uv = mm(h, wts["uv"], tn=1024, tk=d, out_dtypes=[f32], name="proj_uv")
    z = mm(h, wts["z"], tn=1024, tk=d, out_dtypes=[f32], name="proj_z")
    xbc = mm(h, wts["xbc"], tn=1024, tk=d, out_dtypes=[f32], name="proj_xbc")
    dtr = mm(h, wts["dt"], tn=LANES, tk=d, out_dtypes=[f32], name="proj_dt")
    gl = mm(h, wts["gate"], tn=1024, tk=d, out_dtypes=[f32], name="proj_gate")
    ya, cdf = _gmlp_fwd(uv, small["v_norm_g"], small["v_norm_b"], wsp, bsp_t, name="gmlp_fwd")
    yb, hprev, cv, *gathered = _ssd_fwd(xbc, z, dtr, small["conv_w"], small["conv_b"], dtb, alog, dsk,
                                        small["ssm_norm_g"], ride=fwd_ride, name="ssd_fwd")
    if fwd_ride is not None:
        wts = {**wts, **late_weights(gathered[0])}
    pa = mm(ya, wts["pa"], tn=1024, tk=1024, out_dtypes=[f32], name="proj_a")
    tm_gate = min(t, 512)
    row_vec = [pl.BlockSpec((1, d), lambda i, j, k, half=half: (0, half)) for half in range(2)]
    gate_tiles = [pl.BlockSpec((tm_gate, d), lambda i, j, k, half=half: (i, half)) for half in range(2)]

    def merge(pb_acc, pa_t, gla, glb, bga, bgb):
        return pb_acc, _sigmoid(gla + bga) * pa_t + _sigmoid(glb + bgb) * pb_acc

    pb, merged = _matmul(yb, wts["pb"], tm=tm_gate, tn=d, tk=1024, out_dtypes=[f32, bf16], epilogue=merge,
                         extras=[pa, gl, gl, small["b_gates"], small["b_gates"]],
                         extra_specs=[None] + gate_tiles + row_vec, name="proj_b")

    def residual_norm(acc, res, g):
        x_new = res + acc
        r = lax.rsqrt(jnp.mean(x_new * x_new, axis=1, keepdims=True) + NORM_EPS)
        return x_new, x_new * r * g

    x1, h2 = mm(merged, wts["out"], tn=d, tk=1024, out_dtypes=[f32, bf16], epilogue=residual_norm,
                extras=[x, small["norm_mlp_g"]], extra_specs=[None, row_vec[0]], name="out_proj")
    act = mm(h2, wts["up"], tn=1024, tk=d, out_dtypes=[bf16],
             epilogue=lambda acc: (jnp.square(jnp.maximum(acc, 0.0)),), name="mlp_up")
    def loss_head(acc, res, target, g):
        x_new = res + acc
        r = lax.rsqrt(jnp.mean(x_new * x_new, axis=1, keepdims=True) + NORM_EPS)
        xn = x_new * r
        e = xn * g - target
        dy = e * (1.0 / d)
        dxn = dy * g
        dx = r * (dxn - xn * jnp.mean(dxn * xn, axis=1, keepdims=True))
        loss_part = jnp.zeros((1, d), f32) + 0.5 * _sum_all(jnp.mean(e * e, axis=1, keepdims=True))
        return dx, dx, jnp.sum(dy * xn, axis=0, keepdims=True), loss_part

    dx2, dx2b, dgf, loss = _matmul(
        act, wts["down"], tm=tm_gate, tn=d, tk=2048, out_dtypes=[f32, bf16], epilogue=loss_head,
        extras=[x1, tgt, small["norm_final_g"]], extra_specs=[None, None, row_vec[0]], n_sums=2, name="mlp_down")
    tt = min(t, 2048)
    tn_mm = functools.partial(_matmul_tn, tt=tt)
    dw = {}
    dw["down"] = tn_mm(act, dx2b, tka=1024, tn=1024, name="dw_down")
    dup = mm(dx2b, wts["down"], nt=True, tn=1024, tk=1024, out_dtypes=[bf16], extras=[act],
             epilogue=lambda acc, a2: (acc * (2.0 * jnp.sqrt(a2).astype(f32)),), name="d_act")
    dw["up"] = tn_mm(h2, dup, tka=1024, tn=1024, name="dw_up")
    def norm_bwd(dh_acc, x_in, d_res, g):
        r = lax.rsqrt(jnp.mean(x_in * x_in, axis=1, keepdims=True) + NORM_EPS)
        xn = x_in * r
        dxn = dh_acc * g
        dx = d_res + r * (dxn - xn * jnp.mean(dxn * xn, axis=1, keepdims=True))
        return dx, dx, jnp.sum(dh_acc * xn, axis=0, keepdims=True)

    dx1, dx1b, dg_mlp = _matmul(
        dup, wts["up"], nt=True, tm=tm_gate, tn=d, tk=2048, out_dtypes=[f32, bf16], epilogue=norm_bwd,
        extras=[x1, dx2, small["norm_mlp_g"]], extra_specs=[None, None, row_vec[0]], n_sums=1, name="d_h2")
    dw["out"] = tn_mm(merged, dx1b, tka=1024, tn=1024, name="dw_out")
    dmerged = mm(dx1b, wts["out"], nt=True, tn=1024, tk=1024, out_dtypes=[f32], name="d_merged")
    dpa, dpb, dgl, dbg = _merge_bwd(dmerged, pa, pb, gl, small["b_gates"], name="merge_bwd")
    dw["pa"] = tn_mm(ya, dpa, tka=1024, tn=1024, name="dw_pa")
    dw["pb"] = tn_mm(yb, dpb, tka=1024, tn=1024, name="dw_pb")
    dya = mm(dpa, wts["pa"], nt=True, tn=1024, tk=1024, out_dtypes=[f32], name="d_ya")
    dyb = mm(dpb, wts["pb"], nt=True, tn=1024, tk=1024, out_dtypes=[f32], name="d_yb")
    swapped = swap_ride(dw) if swap_ride is not None else None
    duv, dwsp, dbsp_t, dvg, dvb, *got_pair = _gmlp_bwd(uv, cdf, dya, small["v_norm_g"], small["v_norm_b"], wsp,
                                                       bsp_t, ride=swapped, name="gmlp_bwd")
    ride = bwd_ride(swapped, got_pair[0]) if bwd_ride is not None else None
    dz, dxbc, ddt, dcw, dcb, ddtb, dalog, ddsk, dgs, *got = _ssd_bwd(
        xbc, cv, z, dtr, hprev, dyb, small["conv_w"], dtb, alog, dsk, small["ssm_norm_g"],
        _head_seg_matrix(), ride=ride, name="ssd_bwd")
    dw["uv"] = tn_mm(h, duv, tka=1024, tn=1024, name="dw_uv")
    dw["z"] = tn_mm(h, dz, tka=1024, tn=1024, name="dw_z")
    dw["xbc"] = tn_mm(h, dxbc, tka=1024, tn=1024, name="dw_xbc")
    dw["dt"] = tn_mm(h, ddt, tka=1024, tn=LANES, name="dw_dt")
    dw["gate"] = tn_mm(h, dgl, tka=1024, tn=1024, name="dw_gate")
    last = last_ride(dw) if last_ride is not None else None
    res = _matmul_nt_sum(
        [(duv, wts["uv"]), (dz, wts["z"]), (dxbc, wts["xbc"]), (dgl, wts["gate"]), (ddt, wts["dt"])],
        tm=tm, tks=[1024] * 4 + [LANES], ride=last, name="d_h")
    dh, got_last = (res[0], res[1]) if last is not None else (res, None)
    dx, dg_mix = _rms_bwd(x, small["norm_mix_g"], dh, dx1, want_bf16=False, name="rms_mix_bwd")

    dsmall = {
        "norm_mix_g": dg_mix, "conv_w": dcw, "conv_b": dcb, "dt_bias": ddtb[:, :N_HEADS], "a_log": dalog[:, :N_HEADS],
        "d_skip": ddsk[:, :N_HEADS], "ssm_norm_g": dgs, "v_norm_g": dvg, "v_norm_b": dvb, "w_spatial": dwsp,
        "b_spatial": dbsp_t[:, :GMLP_GROUPS].T, "b_gates": dbg, "norm_mlp_g": dg_mlp, "norm_final_g": dgf,
    }
    return loss, dx, dw, dsmall, (got[0] if got else None), got_last


_IN_SHARD = IN_PROJ // N_CHIPS
_LATE = ("w_proj_a", "w_proj_b", "w_out", "w_mlp_up", "w_mlp_down")
_LATE_ROWS = {"w_proj_a": GMLP_WIDTH // N_CHIPS, "w_proj_b": D_INNER // N_CHIPS, "w_out": D_MODEL // N_CHIPS,
              "w_mlp_up": D_MODEL, "w_mlp_down": D_FF // N_CHIPS}
_LATE_TOTAL = sum(_LATE_ROWS.values())


def _late_offsets():
    off, out = 0, {}
    for k in _LATE:
        out[k] = off
        off += _LATE_ROWS[k]
    return out


_LATE_OFF = _late_offsets()

_SMALL = ("norm_mix_g", "conv_w", "conv_b", "dt_bias", "a_log", "d_skip", "ssm_norm_g", "v_norm_g", "v_norm_b",
          "w_spatial", "b_spatial", "b_gates", "norm_mlp_g", "norm_final_g")


def _pack_small(parts):
    flat = jnp.concatenate([parts[k].reshape(-1) for k in _SMALL])
    rows = -(-flat.shape[0] // (16 * LANES)) * 16
    return jnp.pad(flat, (0, rows * LANES - flat.shape[0])).reshape(rows, LANES)


def _unpack_small(packed, shapes):
    flat = packed.reshape(-1)
    out, off = {}, 0
    for k in _SMALL:
        n = math.prod(shapes[k])
        out[k] = flat[off:off + n].reshape(shapes[k])
        off += n
    return out


def _from_chip_columns(stacked):
    _, rows, cols = stacked.shape
    return stacked.transpose(1, 0, 2).reshape(rows, N_CHIPS * cols)


def _to_chip_columns(full):
    rows, cols = full.shape
    return full.reshape(rows, N_CHIPS, cols // N_CHIPS).transpose(1, 0, 2)


def _w_in_grad_by_chip(dw):
    pieces = [dw["uv"], dw["z"], dw["xbc"], dw["dt"][:, :N_HEADS], dw["gate"]]
    bounds = [0]
    for p in pieces:
        bounds.append(bounds[-1] + p.shape[1])
    chips = []
    for k in range(N_CHIPS):
        lo, hi = k * _IN_SHARD, (k + 1) * _IN_SHARD
        parts = [p[:, max(lo, b0) - b0:min(hi, b1) - b0]
                 for p, b0, b1 in zip(pieces, bounds[:-1], bounds[1:]) if min(hi, b1) > max(lo, b0)]
        chips.append(jnp.concatenate(parts, axis=1))
    return jnp.stack(chips)


def kernel(x, norm_mix_g, w_in, conv_w, conv_b, dt_bias, a_log, d_skip, ssm_norm_g, v_norm_g, v_norm_b, w_spatial, b_spatial, b_gates, w_proj_a, w_proj_b, w_out, norm_mlp_g, w_mlp_up, w_mlp_down, norm_final_g, loss_target, m_norm_mix_g, m_w_in, m_conv_w, m_conv_b, m_dt_bias, m_a_log, m_d_skip, m_ssm_norm_g, m_v_norm_g, m_v_norm_b, m_w_spatial, m_b_spatial, m_b_gates, m_w_proj_a, m_w_proj_b, m_w_out, m_norm_mlp_g, m_w_mlp_up, m_w_mlp_down, m_norm_final_g, v_norm_mix_g, v_w_in, v_conv_w, v_conv_b, v_dt_bias, v_a_log, v_d_skip, v_ssm_norm_g, v_v_norm_g, v_v_norm_b, v_w_spatial, v_b_spatial, v_b_gates, v_w_proj_a, v_w_proj_b, v_w_out, v_norm_mlp_g, v_w_mlp_up, v_w_mlp_down, v_norm_final_g):
    given = dict(locals())
    names = ("norm_mix_g", "w_in", "conv_w", "conv_b", "dt_bias", "a_log", "d_skip", "ssm_norm_g", "v_norm_g",
             "v_norm_b", "w_spatial", "b_spatial", "b_gates", "w_proj_a", "w_proj_b", "w_out", "norm_mlp_g",
             "w_mlp_up", "w_mlp_down", "norm_final_g")
    xi, yi, ci = lax.axis_index("x"), lax.axis_index("y"), lax.axis_index("c")
    me_chip = (2 * xi + yi).astype(jnp.int32)

    def halves(a):
        return a.reshape(2, a.shape[0] // 2, a.shape[1])

    def with_own(got, shard):
        whole = lax.dynamic_update_slice(got, shard[None], (me_chip, 0, 0, 0))
        return whole.reshape(N_CHIPS, 2 * shard.shape[1], shard.shape[2])

    shard_in = halves(_cast_bf16(w_in[0], name="cast_w_in"))
    shard_late = halves(_cast_bf16(jnp.concatenate([given[k][0] for k in _LATE]), name="cast_w_late"))
    shard_conv = halves(conv_w.reshape(2 * _TAIL, -1))
    w_in_full = _from_chip_columns(with_own(_gather_shards(shard_in, name="gather_w_in"), shard_in))
    o_dt, o_gate = 2 * GMLP_WIDTH + D_INNER + CONV_DIM, 2 * GMLP_WIDTH + D_INNER + CONV_DIM + N_HEADS
    wts = {
        "uv": w_in_full[:, :2 * GMLP_WIDTH], "z": w_in_full[:, 2 * GMLP_WIDTH:2 * GMLP_WIDTH + D_INNER],
        "xbc": w_in_full[:, 2 * GMLP_WIDTH + D_INNER:o_dt], "dt": _pad_lanes(w_in_full[:, o_dt:o_gate]),
        "gate": w_in_full[:, o_gate:],
    }
    conv_all = with_own(_gather_shards(shard_conv, name="gather_conv_w"), shard_conv)
    conv_full = _from_chip_columns(conv_all.reshape(N_CHIPS, CONV_W, CONV_DIM // N_CHIPS))

    def late_weights(got):
        g_late = with_own(got, shard_late)

        def rows_of(k):
            return g_late[:, _LATE_OFF[k]:_LATE_OFF[k] + _LATE_ROWS[k]]

        return {
            "pa": rows_of("w_proj_a").reshape(GMLP_WIDTH, D_MODEL),
            "pb": rows_of("w_proj_b").reshape(D_INNER, D_MODEL), "out": rows_of("w_out").reshape(D_MODEL, D_MODEL),
            "up": _from_chip_columns(rows_of("w_mlp_up")), "down": rows_of("w_mlp_down").reshape(D_FF, D_MODEL),
        }

    small = {
        "norm_mix_g": norm_mix_g, "conv_w": conv_full, "conv_b": conv_b, "dt_bias": dt_bias, "a_log": a_log,
        "d_skip": d_skip, "ssm_norm_g": ssm_norm_g, "v_norm_g": v_norm_g, "v_norm_b": v_norm_b,
        "w_spatial": w_spatial[0], "b_spatial": b_spatial[0], "b_gates": b_gates, "norm_mlp_g": norm_mlp_g,
        "norm_final_g": norm_final_g.reshape(1, D_MODEL),
    }

    c_idx = ci.astype(jnp.int32).reshape(1)
    place = jnp.stack([me_chip, ci.astype(jnp.int32)])
    partials = {}

    def reduced_shard(tag, got_chips):
        own = _rs_add_chips(*partials[tag], got_chips, place, name="rs_add_chips_" + tag)
        both = _rs_join_halves(own, name="rs_join_" + tag)
        return both.reshape(2 * both.shape[1], both.shape[2])

    def late_grads(dw):
        def by_rows(a):
            return a.reshape(N_CHIPS, a.shape[0] // N_CHIPS, a.shape[1])

        g = jnp.concatenate([by_rows(dw["pa"]), by_rows(dw["pb"]), by_rows(dw["out"]), _to_chip_columns(dw["up"]),
                             by_rows(dw["down"])], axis=1)
        return g.reshape(N_CHIPS, 2, g.shape[1] // 2, g.shape[2])

    def late_partials(g, got_pair):
        partials["late"] = (g, got_pair)
        return _rs_add_pair(g, got_pair, c_idx, name="rs_add_pair_late")

    def in_partials(dw):
        g = _w_in_grad_by_chip(dw).reshape(N_CHIPS, 2, D_MODEL // 2, _IN_SHARD)
        got_pair = _rs_swap_halves(g, name="rs_swap_in")
        partials["in"] = (g, got_pair)
        return _rs_add_pair(g, got_pair, c_idx, name="rs_add_pair_in")

    loss_part, grad_x, dw, dsmall, got_late, got_in = _local_grads(
        x[0], loss_target[0], wts, small, fwd_ride=shard_late, late_weights=late_weights, swap_ride=late_grads,
        bwd_ride=late_partials, last_ride=in_partials)
    loss = lax.psum(loss_part[0, 0], ("x", "y", "c"))
    g_late = reduced_shard("late", got_late)
    g_in_shard = reduced_shard("in", got_in)

    small_shapes = {k: dsmall[k].shape for k in _SMALL}
    red = _unpack_small(_all_reduce_small(_pack_small(dsmall), name="all_reduce_small"), small_shapes)
    conv_cols = CONV_DIM // N_CHIPS
    red["conv_w"] = lax.dynamic_slice_in_dim(red["conv_w"], me_chip * conv_cols, conv_cols, axis=1)

    grads, deltas, new_m, new_v = {}, {}, {}, {}
    for k in ("w_in",) + _LATE:
        g2 = g_in_shard if k == "w_in" else g_late[_LATE_OFF[k]:_LATE_OFF[k] + _LATE_ROWS[k]]
        dlt, m2, v2 = _adamw(given[k][0], g2, given["m_" + k][0], given["v_" + k][0], name="adamw_" + k)
        grads[k], deltas[k], new_m[k], new_v[k] = g2, dlt, m2, v2
    adam_shapes = dict(small_shapes)
    adam_shapes["conv_w"] = (CONV_W, conv_cols)

    def small_pack_of(prefix):
        return _pack_small({k: given[prefix + k].reshape(adam_shapes[k]) for k in _SMALL})

    dlt_s, m_s, v_s = _adamw(small_pack_of(""), _pack_small(red), small_pack_of("m_"), small_pack_of("v_"),
                             name="adamw_small")
    for dst, packed in ((deltas, dlt_s), (new_m, m_s), (new_v, v_s)):
        dst.update(_unpack_small(packed, adam_shapes))
    grads.update(red)

    def shaped(dct):
        return [dct[k].reshape(given[k].shape) for k in names]

    return (loss, grad_x[None], *shaped(grads), *shaped(deltas), *shaped(new_m), *shaped(new_v))
```

```python
import functools
import math

import jax
import jax.numpy as jnp
from jax import lax
from jax.experimental import pallas as pl
from jax.experimental.pallas import tpu as pltpu

f32 = jnp.float32
bf16 = jnp.bfloat16

D_MODEL = 1024
CHUNK = 128
GMLP_WIDTH = 1024
GMLP_GROUPS = 8
D_INNER = 2048
HEAD_DIM = 64
N_HEADS = 32
N_GROUPS = 8
HEADS_PER_GROUP = 4
GROUP_W = HEADS_PER_GROUP * HEAD_DIM
D_STATE = 128
CONV_W = 4
CONV_DIM = 4096
D_FF = 4096
IN_PROJ = 10272
NORM_EPS = 1e-6
N_CHIPS = 4
N_DEV = 8
LANES = 128

ADAM_LR = 0.001
ADAM_B1 = 0.9
ADAM_B2 = 0.999
ADAM_EPS = 1e-08
ADAM_WD = 0.01
ADAM_STEP = 10

MESH = pl.DeviceIdType.MESH
_NT = (((1,), (1,)), ((), ()))
_NN = (((1,), (0,)), ((), ()))
_TN = (((0,), (0,)), ((), ()))
_MB = 2 ** 20


def _params(sem, vmem_mb=48):
    return pltpu.CompilerParams(dimension_semantics=sem, vmem_limit_bytes=vmem_mb * _MB)


def _dot(a, b, dims=_NN):
    return lax.dot_general(a.astype(bf16), b.astype(bf16), dims, preferred_element_type=f32)


def _dot32(a, b):
    return jnp.dot(a, b, preferred_element_type=f32, precision=lax.Precision.HIGHEST)


def _sigmoid(x):
    return 1.0 / (1.0 + jnp.exp(-x))


def _sum_all(a):
    return jnp.sum(jnp.sum(a, axis=1, keepdims=True), axis=0, keepdims=True)


def _iota(shape, dim):
    return lax.broadcasted_iota(jnp.int32, shape, dim)


def _matmul(a, b, *, nt=False, tm, tn, tk, out_dtypes, epilogue=None, extras=(), extra_specs=None, name):
    m, k_dim = a.shape
    n = b.shape[0] if nt else b.shape[1]
    nk = k_dim // tk
    ne, no = len(extras), len(out_dtypes)
    dims = _NT if nt else _NN

    def body(*refs):
        a_ref, b_ref = refs[0], refs[1]
        ex = refs[2:2 + ne]
        outs = refs[2 + ne:2 + ne + no]

        def finish(acc):
            vals = epilogue(acc, *[e[...] for e in ex]) if epilogue is not None else (acc,)
            for o, v in zip(outs, vals):
                o[...] = v.astype(o.dtype)

        part = lax.dot_general(a_ref[...], b_ref[...], dims, preferred_element_type=f32)
        if nk == 1:
            finish(part)
        else:
            acc_ref = refs[-1]
            kk = pl.program_id(2)

            @pl.when(kk == 0)
            def _():
                acc_ref[...] = part

            @pl.when(kk > 0)
            def _():
                acc_ref[...] += part

            @pl.when(kk == nk - 1)
            def _():
                finish(acc_ref[...])

    b_spec = pl.BlockSpec((tn, tk), lambda i, j, k: (j, k)) if nt else pl.BlockSpec((tk, tn), lambda i, j, k: (k, j))
    tile = pl.BlockSpec((tm, tn), lambda i, j, k: (i, j))
    ex_specs = [tile if s is None else s for s in (extra_specs or [None] * ne)]
    outs = pl.pallas_call(
        body, name=name, grid=(m // tm, n // tn, nk),
        in_specs=[pl.BlockSpec((tm, tk), lambda i, j, k: (i, k)), b_spec] + ex_specs,
        out_specs=[tile] * no,
        out_shape=[jax.ShapeDtypeStruct((m, n), dt) for dt in out_dtypes],
        scratch_shapes=[pltpu.VMEM((tm, tn), f32)] if nk > 1 else [],
        compiler_params=_params(("parallel", "parallel", "arbitrary")),
    )(a, b, *extras)
    return outs if no > 1 else outs[0]


def _matmul_nt_sum(pairs, *, tm, tks, ride=None, name):
    m = pairs[0][0].shape[0]
    n = pairs[0][1].shape[0]
    nblk = [a.shape[1] // tk for (a, _), tk in zip(pairs, tks)]
    starts = [sum(nblk[:p]) for p in range(len(pairs))]
    nk = sum(nblk)
    npairs = len(pairs)
    ni = m // tm
    riding = ride is not None

    def body(*refs):
        rest = refs[2 * npairs:]
        if riding:
            ride_ref, o_ref, got_ref, acc_ref, send_sems, recv_sems = rest
        else:
            o_ref, acc_ref = rest
        i, kk = pl.program_id(0), pl.program_id(1)
        if riding:
            start, finish = _scatter_protocol(ride_ref, got_ref, send_sems, recv_sems)
            pl.when((i == 0) & (kk == 0))(start)

        @pl.when(kk == 0)
        def _():
            acc_ref[...] = jnp.zeros_like(acc_ref)

        for p in range(npairs):
            @pl.when((kk >= starts[p]) & (kk < starts[p] + nblk[p]))
            def _(p=p):
                acc_ref[...] += lax.dot_general(refs[2 * p][...], refs[2 * p + 1][...], _NT, preferred_element_type=f32)

        @pl.when(kk == nk - 1)
        def _():
            o_ref[...] = acc_ref[...]

        if riding:
            pl.when((i == ni - 1) & (kk == nk - 1))(finish)

    in_specs, args = [], []
    for p, (a, b) in enumerate(pairs):
        def kblock(k, s=starts[p], nb=nblk[p]):
            return jnp.clip(k - s, 0, nb - 1)
        in_specs.append(pl.BlockSpec((tm, tks[p]), lambda i, k, kb=kblock: (i, kb(k))))
        in_specs.append(pl.BlockSpec((n, tks[p]), lambda i, k, kb=kblock: (0, kb(k))))
        args += [a, b]
    tile = pl.BlockSpec((tm, n), lambda i, k: (i, 0))
    outs = pl.pallas_call(
        body, name=name, grid=(ni, nk), in_specs=in_specs + [_ANY] * riding, out_specs=[tile] + [_ANY] * riding,
        out_shape=[jax.ShapeDtypeStruct((m, n), f32)]
        + ([jax.ShapeDtypeStruct((N_CHIPS - 1,) + ride.shape[1:], ride.dtype)] if riding else []),
        scratch_shapes=[pltpu.VMEM((tm, n), f32)] + (list(_SCATTER_SCRATCH) if riding else []),
        compiler_params=_params(("arbitrary", "arbitrary"), vmem_mb=56),
    )(*args, *([ride] if riding else []))
    return outs if riding else outs[0]


def _matmul_tn(a, b, *, tka, tn, tt, name):
    t, ka = a.shape
    n = b.shape[1]

    def body(a_ref, b_ref, o_ref):
        part = lax.dot_general(a_ref[...], b_ref[...], _TN, preferred_element_type=f32)
        kk = pl.program_id(2)

        @pl.when(kk == 0)
        def _():
            o_ref[...] = part

        @pl.when(kk > 0)
        def _():
            o_ref[...] += part

    return pl.pallas_call(
        body, name=name, grid=(ka // tka, n // tn, t // tt),
        in_specs=[pl.BlockSpec((tt, tka), lambda i, j, k: (k, i)), pl.BlockSpec((tt, tn), lambda i, j, k: (k, j))],
        out_specs=pl.BlockSpec((tka, tn), lambda i, j, k: (i, j)),
        out_shape=jax.ShapeDtypeStruct((ka, n), f32),
        compiler_params=_params(("parallel", "parallel", "arbitrary")),
    )(a, b)


def _row_tile(t):
    return min(t, 512)


def _rms_fwd(x, g, *, name):
    t, d = x.shape
    tr = _row_tile(t)

    def body(x_ref, g_ref, h_ref):
        xv = x_ref[...]
        r = lax.rsqrt(jnp.mean(xv * xv, axis=1, keepdims=True) + NORM_EPS)
        h_ref[...] = (xv * r * g_ref[...]).astype(bf16)

    return pl.pallas_call(
        body, name=name, grid=(t // tr,),
        in_specs=[pl.BlockSpec((tr, d), lambda i: (i, 0)), pl.BlockSpec((1, d), lambda i: (0, 0))],
        out_specs=pl.BlockSpec((tr, d), lambda i: (i, 0)),
        out_shape=jax.ShapeDtypeStruct((t, d), bf16),
        compiler_params=_params(("parallel",)),
    )(x, g)


def _rms_bwd(xin, g, dh, dres, *, want_bf16, name):
    t, d = xin.shape
    tr = _row_tile(t)

    def body(x_ref, g_ref, dh_ref, dres_ref, dx_ref, *rest):
        dg_ref = rest[-1]
        xv = x_ref[...]
        r = lax.rsqrt(jnp.mean(xv * xv, axis=1, keepdims=True) + NORM_EPS)
        xn = xv * r
        dhv = dh_ref[...]
        dxn = dhv * g_ref[...]
        dx = dres_ref[...] + r * (dxn - xn * jnp.mean(dxn * xn, axis=1, keepdims=True))
        dx_ref[...] = dx
        if want_bf16:
            rest[0][...] = dx.astype(bf16)
        part = jnp.sum(dhv * xn, axis=0, keepdims=True)

        @pl.when(pl.program_id(0) == 0)
        def _():
            dg_ref[...] = part

        @pl.when(pl.program_id(0) > 0)
        def _():
            dg_ref[...] += part

    row = pl.BlockSpec((tr, d), lambda i: (i, 0))
    vec = pl.BlockSpec((1, d), lambda i: (0, 0))
    out_shape = [jax.ShapeDtypeStruct((t, d), f32)] + ([jax.ShapeDtypeStruct((t, d), bf16)] if want_bf16 else []) \
        + [jax.ShapeDtypeStruct((1, d), f32)]
    return pl.pallas_call(
        body, name=name, grid=(t // tr,),
        in_specs=[row, vec, row, row],
        out_specs=[row] + ([row] if want_bf16 else []) + [vec],
        out_shape=out_shape,
        compiler_params=_params(("arbitrary",)),
    )(xin, g, dh, dres)


def _loss_head(x2, tgt, g, *, name):
    t, d = x2.shape
    tr = _row_tile(t)

    def body(x_ref, t_ref, g_ref, dx_ref, dxb_ref, dg_ref, loss_ref):
        xv = x_ref[...]
        gv = g_ref[...]
        r = lax.rsqrt(jnp.mean(xv * xv, axis=1, keepdims=True) + NORM_EPS)
        xn = xv * r
        e = xn * gv - t_ref[...]
        lpart = jnp.zeros((1, LANES), f32) + 0.5 * _sum_all(jnp.mean(e * e, axis=1, keepdims=True))
        dy = e * (1.0 / d)
        dxn = dy * gv
        dx = r * (dxn - xn * jnp.mean(dxn * xn, axis=1, keepdims=True))
        dx_ref[...] = dx
        dxb_ref[...] = dx.astype(bf16)
        gpart = jnp.sum(dy * xn, axis=0, keepdims=True)

        @pl.when(pl.program_id(0) == 0)
        def _():
            dg_ref[...] = gpart
            loss_ref[...] = lpart

        @pl.when(pl.program_id(0) > 0)
        def _():
            dg_ref[...] += gpart
            loss_ref[...] += lpart

    row = pl.BlockSpec((tr, d), lambda i: (i, 0))
    vec = pl.BlockSpec((1, d), lambda i: (0, 0))
    return pl.pallas_call(
        body, name=name, grid=(t // tr,),
        in_specs=[row, row, vec],
        out_specs=[row, row, vec, pl.BlockSpec((1, LANES), lambda i: (0, 0))],
        out_shape=[jax.ShapeDtypeStruct((t, d), f32), jax.ShapeDtypeStruct((t, d), bf16),
                   jax.ShapeDtypeStruct((1, d), f32), jax.ShapeDtypeStruct((1, LANES), f32)],
        compiler_params=_params(("arbitrary",)),
    )(x2, tgt, g)


def _merge_bwd(dm, pa, pb, gl, bg, *, name):
    t, d = pa.shape
    tr = _row_tile(t)

    def body(dm_ref, pa_ref, pb_ref, gla_ref, glb_ref, bga_ref, bgb_ref, dpa_ref, dpb_ref, dgl_ref, dbg_ref):
        dmv = dm_ref[...]
        ga = _sigmoid(gla_ref[...] + bga_ref[...])
        gb = _sigmoid(glb_ref[...] + bgb_ref[...])
        dpa_ref[...] = (dmv * ga).astype(bf16)
        dpb_ref[...] = (dmv * gb).astype(bf16)
        dla = dmv * pa_ref[...] * ga * (1.0 - ga)
        dlb = dmv * pb_ref[...] * gb * (1.0 - gb)
        dgl_ref[:, :d] = dla.astype(bf16)
        dgl_ref[:, d:] = dlb.astype(bf16)
        sa = jnp.sum(dla, axis=0, keepdims=True)
        sb = jnp.sum(dlb, axis=0, keepdims=True)

        @pl.when(pl.program_id(0) == 0)
        def _():
            dbg_ref[:, :d] = sa
            dbg_ref[:, d:] = sb

        @pl.when(pl.program_id(0) > 0)
        def _():
            dbg_ref[:, :d] += sa
            dbg_ref[:, d:] += sb

    row = pl.BlockSpec((tr, d), lambda i: (i, 0))
    return pl.pallas_call(
        body, name=name, grid=(t // tr,),
        in_specs=[row, row, row, row, pl.BlockSpec((tr, d), lambda i: (i, 1)),
                  pl.BlockSpec((1, d), lambda i: (0, 0)), pl.BlockSpec((1, d), lambda i: (0, 1))],
        out_specs=[row, row, pl.BlockSpec((tr, 2 * d), lambda i: (i, 0)), pl.BlockSpec((1, 2 * d), lambda i: (0, 0))],
        out_shape=[jax.ShapeDtypeStruct((t, d), bf16), jax.ShapeDtypeStruct((t, d), bf16),
                   jax.ShapeDtypeStruct((t, 2 * d), bf16), jax.ShapeDtypeStruct((1, 2 * d), f32)],
        compiler_params=_params(("arbitrary",)),
    )(dm, pa, pb, gl, gl, bg, bg)


_INV_SQRT2 = 1.0 / math.sqrt(2.0)
_INV_SQRT2PI = 1.0 / math.sqrt(2.0 * math.pi)


def _gmlp_common(uv, vg, vb, with_grad=False):
    cdf = 0.5 * (1.0 + lax.erf(uv * _INV_SQRT2))
    zz = uv * cdf
    u, vhat, rstd, vn = _gmlp_norm(zz, vg, vb)
    if not with_grad:
        return u, vhat, rstd, vn
    return u, vhat, rstd, vn, cdf + uv * jnp.exp(-0.5 * uv * uv) * _INV_SQRT2PI


def _gmlp_norm(zz, vg, vb):
    u = zz[:, :GMLP_WIDTH]
    v = zz[:, GMLP_WIDTH:]
    mu = jnp.mean(v, axis=1, keepdims=True)
    vc = v - mu
    rstd = lax.rsqrt(jnp.mean(vc * vc, axis=1, keepdims=True) + NORM_EPS)
    vhat = vc * rstd
    vn = vhat * vg + vb
    return u, vhat, rstd, vn


def _gmlp_fwd(uv, vg, vb, wsp, bsp_t, *, name):
    t = uv.shape[0]
    nc = t // CHUNK

    def body(uv_ref, vg_ref, vb_ref, w_ref, b_ref, y_ref):
        u, _, _, vn = _gmlp_common(uv_ref[...], vg_ref[...], vb_ref[...])
        tril = _iota((CHUNK, CHUNK), 0) >= _iota((CHUNK, CHUNK), 1)
        bt = b_ref[...]
        for g in range(GMLP_GROUPS):
            sl = slice(g * CHUNK, (g + 1) * CHUNK)
            w = jnp.where(tril, w_ref[g], 0.0)
            s = _dot(w, vn[:, sl]) + bt[:, g:g + 1]
            y_ref[:, sl] = (u[:, sl] * s).astype(bf16)

    return pl.pallas_call(
        body, name=name, grid=(nc,),
        in_specs=[pl.BlockSpec((CHUNK, 2 * GMLP_WIDTH), lambda c: (c, 0)),
                  pl.BlockSpec((1, GMLP_WIDTH), lambda c: (0, 0)), pl.BlockSpec((1, GMLP_WIDTH), lambda c: (0, 0)),
                  pl.BlockSpec((GMLP_GROUPS, CHUNK, CHUNK), lambda c: (0, 0, 0)),
                  pl.BlockSpec((CHUNK, LANES), lambda c: (0, 0))],
        out_specs=pl.BlockSpec((CHUNK, GMLP_WIDTH), lambda c: (c, 0)),
        out_shape=jax.ShapeDtypeStruct((t, GMLP_WIDTH), bf16),
        compiler_params=_params(("parallel",)),
    )(uv, vg, vb, wsp, bsp_t)


def _gmlp_bwd(uv, dya, vg, vb, wsp, bsp_t, *, ride=None, name):
    t = uv.shape[0]
    nc = t // CHUNK
    riding = ride is not None

    def body(*refs):
        uv_ref, dy_ref, vg_ref, vb_ref, w_ref, b_ref = refs[:6]
        duv_ref, dw_ref, db_ref, dvg_ref, dvb_ref = refs[6 + riding:11 + riding]
        first = pl.program_id(0) == 0
        if riding:
            start, finish = _swap_protocol(refs[6], refs[12], refs[13], refs[14])
            pl.when(first)(start)

        @pl.when(first)
        def _():
            dw_ref[...] = jnp.zeros_like(dw_ref)
            db_ref[...] = jnp.zeros_like(db_ref)
            dvg_ref[...] = jnp.zeros_like(dvg_ref)
            dvb_ref[...] = jnp.zeros_like(dvb_ref)

        uvv = uv_ref[...]
        vgv = vg_ref[...]
        u, vhat, rstd, vn, gelu_grad = _gmlp_common(uvv, vgv, vb_ref[...], with_grad=True)
        dy = dy_ref[...]
        tril = _iota((CHUNK, CHUNK), 0) >= _iota((CHUNK, CHUNK), 1)
        lane = _iota((CHUNK, LANES), 1)
        bt = b_ref[...]
        ds_all = dy * u
        dbacc = jnp.zeros((CHUNK, LANES), f32)
        dvh_parts = []
        for g in range(GMLP_GROUPS):
            sl = slice(g * CHUNK, (g + 1) * CHUNK)
            w = jnp.where(tril, w_ref[g], 0.0)
            vng = vn[:, sl]
            s = _dot(w, vng) + bt[:, g:g + 1]
            ds = ds_all[:, sl]
            duv_ref[:, sl] = (dy[:, sl] * s * gelu_grad[:, sl]).astype(bf16)
            dw_ref[g] += jnp.where(tril, _dot(ds, vng, _NT), 0.0)
            dbacc = dbacc + jnp.where(lane == g, jnp.sum(ds, axis=1, keepdims=True), 0.0)
            dvn = _dot(w, ds, _TN)
            vh = vhat[:, sl]
            dvg_ref[:, sl] += jnp.sum(dvn * vh, axis=0, keepdims=True)
            dvb_ref[:, sl] += jnp.sum(dvn, axis=0, keepdims=True)
            dvh_parts.append(dvn * vgv[:, sl])
        db_ref[...] += dbacc
        dvhat = jnp.concatenate(dvh_parts, axis=1)
        m1 = jnp.mean(dvhat, axis=1, keepdims=True)
        m2 = jnp.mean(dvhat * vhat, axis=1, keepdims=True)
        dv = rstd * (dvhat - m1 - vhat * m2)
        duv_ref[:, GMLP_WIDTH:] = (dv * gelu_grad[:, GMLP_WIDTH:]).astype(bf16)
        if riding:
            pl.when(pl.program_id(0) == nc - 1)(finish)

    vec = pl.BlockSpec((1, GMLP_WIDTH), lambda c: (0, 0))
    return pl.pallas_call(
        body, name=name, grid=(nc,),
        in_specs=[pl.BlockSpec((CHUNK, 2 * GMLP_WIDTH), lambda c: (c, 0)),
                  pl.BlockSpec((CHUNK, GMLP_WIDTH), lambda c: (c, 0)), vec, vec,
                  pl.BlockSpec((GMLP_GROUPS, CHUNK, CHUNK), lambda c: (0, 0, 0)),
                  pl.BlockSpec((CHUNK, LANES), lambda c: (0, 0))] + [_ANY] * riding,
        out_specs=[pl.BlockSpec((CHUNK, 2 * GMLP_WIDTH), lambda c: (c, 0)),
                   pl.BlockSpec((GMLP_GROUPS, CHUNK, CHUNK), lambda c: (0, 0, 0)),
                   pl.BlockSpec((CHUNK, LANES), lambda c: (0, 0)), vec, vec] + [_ANY] * riding,
        out_shape=[jax.ShapeDtypeStruct((t, 2 * GMLP_WIDTH), bf16),
                   jax.ShapeDtypeStruct((GMLP_GROUPS, CHUNK, CHUNK), f32),
                   jax.ShapeDtypeStruct((CHUNK, LANES), f32),
                   jax.ShapeDtypeStruct((1, GMLP_WIDTH), f32), jax.ShapeDtypeStruct((1, GMLP_WIDTH), f32)]
        + ([jax.ShapeDtypeStruct(ride.shape[:1] + ride.shape[2:], ride.dtype)] if riding else []),
        scratch_shapes=list(_SWAP_SCRATCH) if riding else [],
        compiler_params=_params(("arbitrary",)),
    )(uv, dya, vg, vb, wsp, bsp_t, *([ride] if riding else []))


_CONV_COLS = 512
_XS0, _B0, _C0 = 0, D_INNER, D_INNER + N_GROUPS * D_STATE


_TAIL = 8


def _conv_silu(cur_ref, tail_ref, w_ref, b_ref, has_prev, xc_ref, cv_ref):
    row = _iota((_TAIL, _CONV_COLS), 0)
    for j in range(CONV_DIM // _CONV_COLS):
        sl = slice(j * _CONV_COLS, (j + 1) * _CONV_COLS)
        cur = cur_ref[:, sl]
        tail = jnp.where(has_prev, tail_ref[:, sl], 0.0)
        acc = cur * w_ref[CONV_W - 1:CONV_W, sl] + b_ref[:, sl]
        for s in range(1, CONV_W):
            rolled = pltpu.roll(cur, s, 0)
            top = jnp.where(row >= s, rolled[:_TAIL], pltpu.roll(tail, s, 0))
            sh = jnp.concatenate([top, rolled[_TAIL:]], axis=0)
            acc = acc + sh * w_ref[CONV_W - 1 - s:CONV_W - s, sl]
        cv_ref[:, sl] = acc
        xc_ref[:, sl] = acc * _sigmoid(acc)


def _col_bcast(mat, h):
    return jnp.broadcast_to(mat[:, h:h + 1], (CHUNK, LANES))


def _head_expand(cols):
    lo = _iota((CHUNK, LANES), 1) < HEAD_DIM
    return jnp.concatenate([jnp.where(lo, cols[2 * j], cols[2 * j + 1]) for j in range(N_HEADS // 2)], axis=1)


def _ssd_chunk_scalars(dtr, dtb, alog):
    xdt_pre = dtr + dtb
    dtv = jnp.maximum(xdt_pre, 0.0) + jnp.log(1.0 + jnp.exp(-jnp.abs(xdt_pre)))
    a = -jnp.exp(alog)
    ltri = (_iota((CHUNK, CHUNK), 0) >= _iota((CHUNK, CHUNK), 1)).astype(f32)
    cs = _dot32(ltri, dtv * a)
    csb = [_col_bcast(cs, h) for h in range(N_HEADS)]
    cs_x = _head_expand(csb)
    dt_x = _head_expand([_col_bcast(dtv, h) for h in range(N_HEADS)])
    cl_x = cs_x[CHUNK - 1:CHUNK, :]
    return dict(xdt_pre=xdt_pre, dtv=dtv, a=a, cs=cs, cs_t=cs.T, csb=csb, dt_x=dt_x, e_x=jnp.exp(cs_x),
                dec_x=jnp.exp(cl_x - cs_x), dk_x=jnp.exp(cl_x))


def _head_masks():
    lane = _iota((CHUNK, GROUP_W), 1)
    return [(lane >= r * HEAD_DIM) & (lane < (r + 1) * HEAD_DIM) for r in range(HEADS_PER_GROUP)]


def _stack_heads(a, masks):
    return jnp.concatenate([jnp.where(m, a, 0.0) for m in masks], axis=0).astype(bf16)


def _seg_sum(a, seg):
    hi = a.astype(jnp.bfloat16)
    lo = (a - hi.astype(f32)).astype(jnp.bfloat16)
    return (lax.dot_general(hi, seg, _NN, preferred_element_type=f32)
            + lax.dot_general(lo, seg, _NN, preferred_element_type=f32))


def _head_seg_matrix():
    return (_iota((D_INNER, LANES), 0) // HEAD_DIM == _iota((D_INNER, LANES), 1)).astype(jnp.bfloat16)


def _ssd_fwd(xbc, z, dtr, cw, cb, dtb, alog, dsk_x, gs, *, ride=None, name):
    t = xbc.shape[0]
    nc = t // CHUNK
    tiles = CHUNK // _TAIL

    def body(*refs):
        cur_ref, tail_ref, z_ref, dtr_ref, cw_ref, cb_ref, dtb_ref, alog_ref, dsk_ref, gs_ref = refs[:10]
        if ride is None:
            yb_ref, hp_ref, cv_ref, state_ref, xc_ref = refs[10:]
        else:
            ride_ref, yb_ref, hp_ref, cv_ref, got_ref, state_ref, xc_ref, send_sems, recv_sems = refs[10:]
        c = pl.program_id(0)
        if ride is not None:
            start, relay, finish = _gather_protocol(ride_ref, got_ref, send_sems, recv_sems)
            pl.when(c == 0)(start)
            pl.when(c == nc // 2)(relay)

        @pl.when(c == 0)
        def _():
            state_ref[...] = jnp.zeros_like(state_ref)

        _conv_silu(cur_ref, tail_ref, cw_ref, cb_ref, c > 0, xc_ref, cv_ref)
        sc = _ssd_chunk_scalars(dtr_ref[...], dtb_ref[...], alog_ref[...])
        tril = _iota((CHUNK, CHUNK), 0) >= _iota((CHUNK, CHUNK), 1)
        masks = _head_masks()
        hp_ref[0] = state_ref[...]
        for g in range(N_GROUPS):
            gsl = slice(g * GROUP_W, (g + 1) * GROUP_W)
            xs_g = xc_ref[:, gsl]
            bg = xc_ref[:, _B0 + g * D_STATE:_B0 + (g + 1) * D_STATE]
            cg = xc_ref[:, _C0 + g * D_STATE:_C0 + (g + 1) * D_STATE]
            xdt_g = xs_g * sc["dt_x"][:, gsl]
            cbm = _dot(cg, bg, _NT)
            mw = jnp.concatenate(
                [cbm * jnp.exp(jnp.where(tril, sc["csb"][h] - sc["cs_t"][h:h + 1, :], -1e30))
                 for h in range(g * HEADS_PER_GROUP, (g + 1) * HEADS_PER_GROUP)], axis=1)
            ht_g = state_ref[:, gsl]
            y_g = _dot(mw, _stack_heads(xdt_g, masks)) + sc["e_x"][:, gsl] * _dot(cg, ht_g) + dsk_ref[:, gsl] * xs_g
            state_ref[:, gsl] = ht_g * sc["dk_x"][:, gsl] + _dot(bg, xdt_g * sc["dec_x"][:, gsl], _TN)
            zg = z_ref[:, gsl]
            yg = y_g * zg * _sigmoid(zg)
            rs = lax.rsqrt(jnp.mean(yg * yg, axis=1, keepdims=True) + NORM_EPS)
            yb_ref[:, gsl] = (yg * rs * gs_ref[:, gsl]).astype(bf16)
        if ride is not None:
            pl.when(c == nc - 1)(finish)

    def chunk(w):
        return pl.BlockSpec((CHUNK, w), lambda c: (c, 0))

    def const(shape):
        return pl.BlockSpec(shape, lambda c: (0,) * len(shape))

    riding = ride is not None
    return pl.pallas_call(
        body, name=name, grid=(nc,),
        in_specs=[chunk(CONV_DIM), pl.BlockSpec((_TAIL, CONV_DIM), lambda c: (jnp.maximum(c * tiles - 1, 0), 0)),
                  chunk(D_INNER), chunk(LANES), const((CONV_W, CONV_DIM)), const((1, CONV_DIM)),
                  const((1, LANES)), const((1, LANES)), const((1, D_INNER)), const((1, D_INNER))] + [_ANY] * riding,
        out_specs=[chunk(D_INNER), pl.BlockSpec((1, D_STATE, D_INNER), lambda c: (c, 0, 0)), chunk(CONV_DIM)]
        + [_ANY] * riding,
        out_shape=[jax.ShapeDtypeStruct((t, D_INNER), bf16), jax.ShapeDtypeStruct((nc, D_STATE, D_INNER), f32),
                   jax.ShapeDtypeStruct((t, CONV_DIM), f32)]
        + ([jax.ShapeDtypeStruct((N_CHIPS,) + ride.shape, ride.dtype)] if riding else []),
        scratch_shapes=[pltpu.VMEM((D_STATE, D_INNER), f32), pltpu.VMEM((CHUNK, CONV_DIM), f32)]
        + (list(_GATHER_SCRATCH) if riding else []),
        compiler_params=_params(("arbitrary",)),
    )(xbc, xbc, z, dtr, cw, cb, dtb, alog, dsk_x, gs, *([ride] if riding else []))


def _ssd_bwd(xbc, cv, z, dtr, hprev, dyb, cw, dtb, alog, dsk_x, gs, seg, *, ride=None, name):
    t = xbc.shape[0]
    nc = t // CHUNK

    def body(*refs):
        (cur_ref, cv_ref, z_ref, dtr_ref, hp_ref, dyb_ref, cw_ref, dtb_ref, alog_ref, dsk_ref, gs_ref,
         seg_ref) = refs[:12]
        rest = refs[12:]
        if ride is not None:
            ride_ref, got_ref, send_sems, recv_sems = rest[0], rest[10], rest[-2], rest[-1]
            rest = rest[1:10] + rest[11:-2]
        (dz_ref, dxbc_ref, ddt_ref, dcw_ref, dcb_ref, ddtb_ref, dalog_ref, ddsk_ref, dgs_ref,
         dh_ref, dcnext_ref, xc_ref, dxc_ref, x13_ref, x2_ref, rows_ref) = rest
        i = pl.program_id(0)
        if ride is not None:
            start, finish = _scatter_protocol(ride_ref, got_ref, send_sems, recv_sems)
            pl.when(i == 0)(start)

        @pl.when(i == 0)
        def _():
            for ref in (dh_ref, dcnext_ref, dcw_ref, dcb_ref, ddtb_ref, dalog_ref, ddsk_ref, dgs_ref, rows_ref):
                ref[...] = jnp.zeros_like(ref)

        for j in range(CONV_DIM // _CONV_COLS):
            sl = slice(j * _CONV_COLS, (j + 1) * _CONV_COLS)
            cvv = cv_ref[:, sl]
            xc_ref[:, sl] = cvv * _sigmoid(cvv)
        sc = _ssd_chunk_scalars(dtr_ref[...], dtb_ref[...], alog_ref[...])
        tril = _iota((CHUNK, CHUNK), 0) >= _iota((CHUNK, CHUNK), 1)
        triu = _iota((CHUNK, CHUNK), 0) <= _iota((CHUNK, CHUNK), 1)
        masks = _head_masks()
        rowh = _iota((N_HEADS, CHUNK), 0)
        dcs_t = jnp.zeros((N_HEADS, CHUNK), f32)
        for g in range(N_GROUPS):
            gsl = slice(g * GROUP_W, (g + 1) * GROUP_W)
            xs_g = xc_ref[:, gsl]
            bg = xc_ref[:, _B0 + g * D_STATE:_B0 + (g + 1) * D_STATE]
            cg = xc_ref[:, _C0 + g * D_STATE:_C0 + (g + 1) * D_STATE]
            dt_g, e_g, dec_g, dk_g = sc["dt_x"][:, gsl], sc["e_x"][:, gsl], sc["dec_x"][:, gsl], sc["dk_x"][:, gsl]
            dsk_g = dsk_ref[:, gsl]
            xdt_g = xs_g * dt_g
            xdt_stack = _stack_heads(xdt_g, masks)
            cbm = _dot(cg, bg, _NT)
            cbt = _dot(bg, cg, _NT)
            heads = range(g * HEADS_PER_GROUP, (g + 1) * HEADS_PER_GROUP)
            lmats = [jnp.exp(jnp.where(tril, sc["csb"][h] - sc["cs_t"][h:h + 1, :], -1e30)) for h in heads]
            mw = jnp.concatenate([cbm * lm for lm in lmats], axis=1)
            mtw = jnp.concatenate(
                [cbt * jnp.exp(jnp.where(triu, sc["cs_t"][h:h + 1, :] - sc["csb"][h], -1e30)) for h in heads], axis=1)
            ht_g = hp_ref[0, :, gsl]
            dhn_g = dh_ref[:, gsl]
            yoff = e_g * _dot(cg, ht_g)
            y_g = _dot(mw, xdt_stack) + yoff + dsk_g * xs_g
            zg = z_ref[:, gsl]
            sz = _sigmoid(zg)
            silu = zg * sz
            yg = y_g * silu
            rs = lax.rsqrt(jnp.mean(yg * yg, axis=1, keepdims=True) + NORM_EPS)
            yn = yg * rs
            dyb = dyb_ref[:, gsl]
            dgs_ref[:, gsl] += jnp.sum(dyb * yn, axis=0, keepdims=True)
            dyn = dyb * gs_ref[:, gsl]
            dyg = rs * (dyn - yn * jnp.mean(dyn * yn, axis=1, keepdims=True))
            dy_g = dyg * silu
            dz_ref[:, gsl] = (dyg * y_g * (sz * (1.0 + zg * (1.0 - sz)))).astype(bf16)
            dy_stack = _stack_heads(dy_g, masks)
            dm_w = _dot(dy_g, xdt_stack, _NT)
            dmt_w = _dot(xdt_g, dy_stack, _NT)
            dxdt = _dot(mtw, dy_stack)
            dcb_acc = jnp.zeros((CHUNK, CHUNK), f32)
            for r, h in enumerate(heads):
                hs = slice(r * CHUNK, (r + 1) * CHUNK)
                dml = dm_w[:, hs] * lmats[r]
                dcb_acc = dcb_acc + dml
                col = jnp.sum(dml * cbm, axis=0, keepdims=True)
                row = jnp.sum(dmt_w[:, hs] * mtw[:, hs], axis=0, keepdims=True)
                dcs_t = dcs_t + jnp.where(rowh == h, row - col, 0.0)
            w = _dot(bg, dhn_g)
            dxdt = dxdt + dec_g * w
            decx3 = dec_g * (xdt_g * w)
            dg_g = e_g * dy_g
            d_c = _dot(dg_g, ht_g, _NT) + _dot(dcb_acc, bg)
            d_b = _dot(dcb_acc, cg, _TN) + _dot(xdt_g * dec_g, dhn_g, _NT)
            dh_ref[:, gsl] = dhn_g * dk_g + _dot(cg, dg_g, _TN)
            dxc_ref[:, gsl] = dsk_g * dy_g + dxdt * dt_g
            dxc_ref[:, _B0 + g * D_STATE:_B0 + (g + 1) * D_STATE] = d_b
            dxc_ref[:, _C0 + g * D_STATE:_C0 + (g + 1) * D_STATE] = d_c
            x13_ref[:, gsl] = dy_g * yoff - decx3
            x2_ref[:, gsl] = dxdt * xs_g
            rows_ref[0:1, gsl] = jnp.sum(dhn_g * ht_g, axis=0, keepdims=True)
            rows_ref[1:2, gsl] = jnp.sum(decx3, axis=0, keepdims=True)
            rows_ref[2:3, gsl] = jnp.sum(dy_g * xs_g, axis=0, keepdims=True)
        segm = seg_ref[...]
        r13 = _seg_sum(x13_ref[...], segm)
        r2 = _seg_sum(x2_ref[...], segm)
        small = _seg_sum(rows_ref[...], segm)
        lane = _iota((CHUNK, LANES), 1)
        rowi = _iota((CHUNK, LANES), 0)
        dcl_row = small[0:1, :] * jnp.exp(sc["cs"][CHUNK - 1:CHUNK, :]) + small[1:2, :]
        dcs = r13 + jnp.where(rowi == CHUNK - 1, dcl_row, 0.0)
        dcs_t_all = dcs.T + jnp.concatenate([dcs_t, jnp.zeros((LANES - N_HEADS, CHUNK), f32)], axis=0)
        dda = _dot32(dcs_t_all, tril.astype(f32)).T
        a = sc["a"]
        ddt_total = r2 + dda * a
        dalog_ref[...] += jnp.sum(dda * sc["dtv"], axis=0, keepdims=True) * a
        ddtr = jnp.where(lane < N_HEADS, ddt_total * _sigmoid(sc["xdt_pre"]), 0.0)
        ddtb_ref[...] += jnp.sum(ddtr, axis=0, keepdims=True)
        ddt_ref[...] = ddtr.astype(bf16)
        ddsk_ref[...] += small[2:3, :]
        row8 = _iota((_TAIL, _CONV_COLS), 0)
        for j in range(CONV_DIM // _CONV_COLS):
            sl = slice(j * _CONV_COLS, (j + 1) * _CONV_COLS)
            cvv = cv_ref[:, sl]
            sg = _sigmoid(cvv)
            dconv = dxc_ref[:, sl] * (sg * (1.0 + cvv * (1.0 - sg)))
            nxt = dcnext_ref[:, sl]
            cur = cur_ref[:, sl]
            dxin = dconv * cw_ref[CONV_W - 1:CONV_W, sl]
            dcw_ref[CONV_W - 1:CONV_W, sl] += jnp.sum(dconv * cur, axis=0, keepdims=True)
            for s in range(1, CONV_W):
                rolled = pltpu.roll(dconv, CHUNK - s, 0)
                bot = jnp.where(row8 < _TAIL - s, rolled[CHUNK - _TAIL:], pltpu.roll(nxt, _TAIL - s, 0))
                up = jnp.concatenate([rolled[:CHUNK - _TAIL], bot], axis=0)
                dxin = dxin + up * cw_ref[CONV_W - 1 - s:CONV_W - s, sl]
                dcw_ref[CONV_W - 1 - s:CONV_W - s, sl] += jnp.sum(up * cur, axis=0, keepdims=True)
            dcb_ref[:, sl] += jnp.sum(dconv, axis=0, keepdims=True)
            dxbc_ref[:, sl] = dxin.astype(bf16)
            dcnext_ref[:, sl] = dconv[:_TAIL]
        if ride is not None:
            pl.when(i == nc - 1)(finish)

    def chunk(w):
        return pl.BlockSpec((CHUNK, w), lambda i: (nc - 1 - i, 0))

    def const(shape):
        return pl.BlockSpec(shape, lambda i: (0,) * len(shape))

    riding = ride is not None
    return pl.pallas_call(
        body, name=name, grid=(nc,),
        in_specs=[chunk(CONV_DIM), chunk(CONV_DIM),
                  chunk(D_INNER), chunk(LANES), pl.BlockSpec((1, D_STATE, D_INNER), lambda i: (nc - 1 - i, 0, 0)),
                  chunk(D_INNER), const((CONV_W, CONV_DIM)),
                  const((1, LANES)), const((1, LANES)), const((1, D_INNER)), const((1, D_INNER)),
                  const((D_INNER, LANES))] + [_ANY] * riding,
        out_specs=[chunk(D_INNER), chunk(CONV_DIM), chunk(LANES), const((CONV_W, CONV_DIM)), const((1, CONV_DIM)),
                   const((1, LANES)), const((1, LANES)), const((1, LANES)), const((1, D_INNER))] + [_ANY] * riding,
        out_shape=[jax.ShapeDtypeStruct((t, D_INNER), bf16), jax.ShapeDtypeStruct((t, CONV_DIM), bf16),
                   jax.ShapeDtypeStruct((t, LANES), bf16), jax.ShapeDtypeStruct((CONV_W, CONV_DIM), f32),
                   jax.ShapeDtypeStruct((1, CONV_DIM), f32), jax.ShapeDtypeStruct((1, LANES), f32),
                   jax.ShapeDtypeStruct((1, LANES), f32), jax.ShapeDtypeStruct((1, LANES), f32),
                   jax.ShapeDtypeStruct((1, D_INNER), f32)]
        + ([jax.ShapeDtypeStruct((N_CHIPS - 1,) + ride.shape[1:], ride.dtype)] if riding else []),
        scratch_shapes=[pltpu.VMEM((D_STATE, D_INNER), f32), pltpu.VMEM((_TAIL, CONV_DIM), f32),
                        pltpu.VMEM((CHUNK, CONV_DIM), f32), pltpu.VMEM((CHUNK, CONV_DIM), f32),
                        pltpu.VMEM((CHUNK, D_INNER), f32), pltpu.VMEM((CHUNK, D_INNER), f32),
                        pltpu.VMEM((_TAIL, D_INNER), f32)]
        + (list(_SCATTER_SCRATCH) if riding else []),
        compiler_params=_params(("arbitrary",)),
    )(xbc, cv, z, dtr, hprev, dyb, cw, dtb, alog, dsk_x, gs, seg, *([ride] if riding else []))


def _adamw(w, g, m, v, *, name):
    r, c = w.shape
    tr = r
    while tr * c * 4 > _MB and tr % 16 == 0:
        tr //= 2

    def body(w_ref, g_ref, m_ref, v_ref, d_ref, m2_ref, v2_ref):
        gv = g_ref[...]
        m2 = ADAM_B1 * m_ref[...] + (1.0 - ADAM_B1) * gv
        v2 = ADAM_B2 * v_ref[...] + (1.0 - ADAM_B2) * (gv * gv)
        m_hat = m2 / (1.0 - ADAM_B1 ** ADAM_STEP)
        v_hat = v2 / (1.0 - ADAM_B2 ** ADAM_STEP)
        d_ref[...] = -ADAM_LR * (m_hat / (jnp.sqrt(v_hat) + ADAM_EPS) + ADAM_WD * w_ref[...])
        m2_ref[...] = m2
        v2_ref[...] = v2

    blk = pl.BlockSpec((tr, c), lambda i: (i, 0))
    return pl.pallas_call(
        body, name=name, grid=(r // tr,),
        in_specs=[blk] * 4, out_specs=[blk] * 3,
        out_shape=[jax.ShapeDtypeStruct((r, c), f32)] * 3,
        compiler_params=_params(("parallel",)),
    )(w, g, m, v)


def _row_block(rows, cols):
    cap = max(16, _MB // (4 * cols))
    return max(tr for tr in range(16, min(cap, rows) + 1, 16) if rows % tr == 0)


def _cast_bf16(a, *, name):
    r, c = a.shape
    tr = _row_block(r, c)

    def body(a_ref, o_ref):
        o_ref[...] = a_ref[...].astype(bf16)

    blk = pl.BlockSpec((tr, c), lambda i: (i, 0))
    return pl.pallas_call(
        body, name=name, grid=(r // tr,), in_specs=[blk], out_specs=blk,
        out_shape=jax.ShapeDtypeStruct((r, c), bf16), compiler_params=_params(("parallel",)),
    )(a)


_ANY = pl.BlockSpec(memory_space=pl.ANY)


def _place():
    x, y, c = lax.axis_index("x"), lax.axis_index("y"), lax.axis_index("c")
    other_chips = [(1 - x, y), (x, 1 - y), (1 - x, 1 - y)]
    return x, y, c, other_chips


def _gather_protocol(in_ref, out_ref, send_sems, recv_sems):
    x, y, c, chips = _place()
    me = 2 * x + y
    sibling = (x, y, 1 - c)
    where = [2 * cx + cy for cx, cy in chips]

    def cp(k, chip, half, to, src=None):
        dst = out_ref.at[chip, half]
        return pltpu.make_async_remote_copy(
            src_ref=dst if src is None else src, dst_ref=dst, send_sem=send_sems.at[k], recv_sem=recv_sems.at[k],
            device_id=to, device_id_type=MESH)

    def sends():
        return [cp(j, me, c, (*chips[j], c), src=in_ref.at[c]) for j in range(2)]

    def relays():
        return [cp(3 + j, where[j], c, sibling) for j in range(3)]

    def landed(j):
        return cp(j, where[j], c, sibling)

    def start():
        for f in sends():
            f.start()

    def relay():
        onward = relays()
        for first in range(2):
            @pl.when(c == first)
            def _(first=first):
                landed(first).wait_recv()
                cp(2, where[first], c, (*chips[1 - first], c)).start()
                onward[first].start()
                landed(1 - first).wait_recv()
                onward[1 - first].start()

    def finish():
        landed(2).wait_recv()
        relays()[2].start()
        for j in range(3):
            cp(3 + j, where[j], 1 - c, sibling).wait_recv()
        for f in sends() + [landed(2)] + relays():
            f.wait_send()

    return start, relay, finish


_GATHER_SCRATCH = [pltpu.SemaphoreType.DMA((6,)), pltpu.SemaphoreType.DMA((6,))]


def _gather_shards(shard, *, name):
    _, rh, lanes = shard.shape

    def body(in_ref, out_ref, send_sems, recv_sems):
        start, relay, finish = _gather_protocol(in_ref, out_ref, send_sems, recv_sems)
        start()
        relay()
        finish()

    return pl.pallas_call(
        body, name=name, in_specs=[_ANY], out_specs=_ANY,
        out_shape=jax.ShapeDtypeStruct((N_CHIPS, 2, rh, lanes), shard.dtype),
        scratch_shapes=list(_GATHER_SCRATCH),
    )(shard)


def _scatter_protocol(p_ref, out_ref, send_sems, recv_sems):
    x, y, c, chips = _place()

    def copies():
        return [pltpu.make_async_remote_copy(
            src_ref=p_ref.at[2 * cx + cy], dst_ref=out_ref.at[j], send_sem=send_sems.at[j], recv_sem=recv_sems.at[j],
            device_id=(cx, cy, c), device_id_type=MESH) for j, (cx, cy) in enumerate(chips)]

    def start():
        for cpy in copies():
            cpy.start()

    def finish():
        for cpy in copies():
            cpy.wait()

    return start, finish


_SCATTER_SCRATCH = [pltpu.SemaphoreType.DMA((3,)), pltpu.SemaphoreType.DMA((3,))]


def _swap_protocol(g_ref, out_ref, send_sems, recv_sems):
    x, y, c, _ = _place()

    def copies():
        return [pltpu.make_async_remote_copy(
            src_ref=g_ref.at[k, 1 - c], dst_ref=out_ref.at[k], send_sem=send_sems.at[k], recv_sem=recv_sems.at[k],
            device_id=(x, y, 1 - c), device_id_type=MESH) for k in range(N_CHIPS)]

    def start():
        for cpy in copies():
            cpy.start()

    def finish():
        for cpy in copies():
            cpy.wait()

    return start, finish


_SWAP_SCRATCH = [pltpu.SemaphoreType.DMA((N_CHIPS,)), pltpu.SemaphoreType.DMA((N_CHIPS,))]


def _rs_swap_halves(g, *, name):
    nch, _, rh, lanes = g.shape

    def body(g_ref, out_ref, send_sems, recv_sems):
        start, finish = _swap_protocol(g_ref, out_ref, send_sems, recv_sems)
        start()
        finish()

    return pl.pallas_call(
        body, name=name, in_specs=[_ANY], out_specs=_ANY,
        out_shape=jax.ShapeDtypeStruct((nch, rh, lanes), g.dtype),
        scratch_shapes=list(_SWAP_SCRATCH),
    )(g)


def _rs_add_pair(g, got, c_idx, *, name):
    nch, _, rh, lanes = g.shape
    tr = _row_block(rh, lanes)

    def body(c_ref, g_ref, got_ref, p16_ref):
        p16_ref[...] = (g_ref[...] + got_ref[...]).astype(bf16)

    blk = pl.BlockSpec((None, tr, lanes), lambda k, i, c_ref: (k, i, 0))
    return pl.pallas_call(
        body, name=name,
        grid_spec=pltpu.PrefetchScalarGridSpec(
            num_scalar_prefetch=1, grid=(nch, rh // tr),
            in_specs=[pl.BlockSpec((None, None, tr, lanes), lambda k, i, c_ref: (k, c_ref[0], i, 0)), blk],
            out_specs=blk),
        out_shape=jax.ShapeDtypeStruct((nch, rh, lanes), bf16),
        compiler_params=_params(("parallel", "parallel")),
    )(c_idx, g, got)


def _rs_add_chips(g, got_pair, got, place, *, name):
    _, _, rh, lanes = g.shape
    tr = _row_block(rh, lanes)

    def body(place_ref, g_ref, pair_ref, got_ref, o_ref):
        own = g_ref[...] + pair_ref[...]
        o_ref[...] = ((own + got_ref[0].astype(f32)) + got_ref[1].astype(f32)) + got_ref[2].astype(f32)

    return pl.pallas_call(
        body, name=name,
        grid_spec=pltpu.PrefetchScalarGridSpec(
            num_scalar_prefetch=1, grid=(rh // tr,),
            in_specs=[pl.BlockSpec((None, None, tr, lanes), lambda i, place_ref: (place_ref[0], place_ref[1], i, 0)),
                      pl.BlockSpec((None, tr, lanes), lambda i, place_ref: (place_ref[0], i, 0)),
                      pl.BlockSpec((3, tr, lanes), lambda i, place_ref: (0, i, 0))],
            out_specs=pl.BlockSpec((None, tr, lanes), lambda i, place_ref: (place_ref[1], i, 0))),
        out_shape=jax.ShapeDtypeStruct((2, rh, lanes), f32),
        compiler_params=_params(("parallel",)),
    )(place, g, got_pair, got)


def _rs_join_halves(halves, *, name):
    def body(h_ref, out_ref, send_sem, recv_sem):
        x, y, c, _ = _place()
        cpy = pltpu.make_async_remote_copy(
            src_ref=h_ref.at[c], dst_ref=out_ref.at[c], send_sem=send_sem, recv_sem=recv_sem,
            device_id=(x, y, 1 - c), device_id_type=MESH)
        cpy.start()
        cpy.wait()

    return pl.pallas_call(
        body, name=name, in_specs=[_ANY], out_specs=_ANY,
        out_shape=jax.ShapeDtypeStruct(halves.shape, halves.dtype), input_output_aliases={0: 0},
        scratch_shapes=[pltpu.SemaphoreType.DMA, pltpu.SemaphoreType.DMA],
    )(halves)


def _all_reduce_small(s, *, name):
    rs, lanes = s.shape
    rh = rs // 2

    def body(s_ref, o_ref, sib_ref, mine_ref, chips_ref, send_sems, recv_sems):
        x, y, c, chips = _place()
        me = 2 * x + y
        sibling = (x, y, 1 - c)
        rows = pl.ds(pl.multiple_of(c * rh, 8), rh)

        def cp(k, src, dst, to):
            return pltpu.make_async_remote_copy(src_ref=src, dst_ref=dst, send_sem=send_sems.at[k],
                                                recv_sem=recv_sems.at[k], device_id=to, device_id_type=MESH)

        swap = cp(0, s_ref, sib_ref, sibling)
        swap.start()
        swap.wait()
        mine_ref[...] = s_ref[rows, :] + sib_ref[rows, :]
        sends = [cp(1 + j, mine_ref, chips_ref.at[j], (cx, cy, c)) for j, (cx, cy) in enumerate(chips)]
        for cpy in sends:
            cpy.start()
        for cpy in sends:
            cpy.wait()
        where = [2 * cx + cy for cx, cy in chips]
        total = None
        for q in range(N_CHIPS):
            term = jnp.where(q == me, mine_ref[...], jnp.where(
                q == where[0], chips_ref[0], jnp.where(q == where[1], chips_ref[1], chips_ref[2])))
            total = term if total is None else total + term
        o_ref[rows, :] = total
        push = cp(4, o_ref.at[rows, :], o_ref.at[rows, :], sibling)
        push.start()
        push.wait()

    vm = pl.BlockSpec(memory_space=pltpu.VMEM)
    return pl.pallas_call(
        body, name=name, in_specs=[vm], out_specs=vm,
        out_shape=jax.ShapeDtypeStruct((rs, lanes), f32),
        scratch_shapes=[pltpu.VMEM((rs, lanes), f32), pltpu.VMEM((rh, lanes), f32),
                        pltpu.VMEM((N_CHIPS - 1, rh, lanes), f32), pltpu.SemaphoreType.DMA((5,)),
                        pltpu.SemaphoreType.DMA((5,))],
        compiler_params=pltpu.CompilerParams(vmem_limit_bytes=32 * _MB),
    )(s)


def _pad_lanes(a, width=LANES):
    return jnp.pad(a, ((0, 0), (0, width - a.shape[1])))


def _local_grads(x, tgt, wts, small, *, fwd_ride=None, late_weights=None, swap_ride=None, bwd_ride=None,
                 last_ride=None):
    t = x.shape[0]
    tm = min(t, 1024)
    d = D_MODEL
    mm = functools.partial(_matmul, tm=tm)

    dtb = _pad_lanes(small["dt_bias"])
    alog = _pad_lanes(small["a_log"])
    dsk = jnp.repeat(small["d_skip"], HEAD_DIM, axis=1)
    bsp_t = _pad_lanes(small["b_spatial"].T)
    wsp = small["w_spatial"]

    h = _rms_fwd(x, small["norm_mix_g"], name="rms_mix")
    uv = mm(h, wts["uv"], tn=1024, tk=d, out_dtypes=[f32], name="proj_uv")
    z = mm(h, wts["z"], tn=1024, tk=d, out_dtypes=[f32], name="proj_z")
    xbc = mm(h, wts["xbc"], tn=1024, tk=d, out_dtypes=[f32], name="proj_xbc")
    dtr = mm(h, wts["dt"], tn=LANES, tk=d, out_dtypes=[f32], name="proj_dt")
    gl = mm(h, wts["gate"], tn=1024, tk=d, out_dtypes=[f32], name="proj_gate")
    ya = _gmlp_fwd(uv, small["v_norm_g"], small["v_norm_b"], wsp, bsp_t, name="gmlp_fwd")
    yb, hprev, cv, *gathered = _ssd_fwd(xbc, z, dtr, small["conv_w"], small["conv_b"], dtb, alog, dsk,
                                        small["ssm_norm_g"], ride=fwd_ride, name="ssd_fwd")
    if fwd_ride is not None:
        wts = {**wts, **late_weights(gathered[0])}
    pa = mm(ya, wts["pa"], tn=1024, tk=1024, out_dtypes=[f32], name="proj_a")
    tm_gate = min(t, 512)
    row_vec = [pl.BlockSpec((1, d), lambda i, j, k, half=half: (0, half)) for half in range(2)]
    gate_tiles = [pl.BlockSpec((tm_gate, d), lambda i, j, k, half=half: (i, half)) for half in range(2)]

    def merge(pb_acc, pa_t, gla, glb, bga, bgb):
        return pb_acc, _sigmoid(gla + bga) * pa_t + _sigmoid(glb + bgb) * pb_acc

    pb, merged = _matmul(yb, wts["pb"], tm=tm_gate, tn=d, tk=1024, out_dtypes=[f32, bf16], epilogue=merge,
                         extras=[pa, gl, gl, small["b_gates"], small["b_gates"]],
                         extra_specs=[None] + gate_tiles + row_vec, name="proj_b")

    def residual_norm(acc, res, g):
        x_new = res + acc
        r = lax.rsqrt(jnp.mean(x_new * x_new, axis=1, keepdims=True) + NORM_EPS)
        return x_new, x_new * r * g

    x1, h2 = mm(merged, wts["out"], tn=d, tk=1024, out_dtypes=[f32, bf16], epilogue=residual_norm,
                extras=[x, small["norm_mlp_g"]], extra_specs=[None, row_vec[0]], name="out_proj")
    act = mm(h2, wts["up"], tn=1024, tk=d, out_dtypes=[bf16],
             epilogue=lambda acc: (jnp.square(jnp.maximum(acc, 0.0)),), name="mlp_up")
    x2 = mm(act, wts["down"], tn=1024, tk=2048, out_dtypes=[f32], extras=[x1],
            epilogue=lambda acc, res: (res + acc,), name="mlp_down")

    dx2, dx2b, dgf, loss = _loss_head(x2, tgt, small["norm_final_g"], name="loss_head")
    tt = min(t, 2048)
    tn_mm = functools.partial(_matmul_tn, tt=tt)
    dw = {}
    dw["down"] = tn_mm(act, dx2b, tka=1024, tn=1024, name="dw_down")
    dup = mm(dx2b, wts["down"], nt=True, tn=1024, tk=1024, out_dtypes=[bf16], extras=[act],
             epilogue=lambda acc, a2: (acc * (2.0 * jnp.sqrt(a2).astype(f32)),), name="d_act")
    dw["up"] = tn_mm(h2, dup, tka=1024, tn=1024, name="dw_up")
    dh2 = mm(dup, wts["up"], nt=True, tn=1024, tk=2048, out_dtypes=[f32], name="d_h2")
    dx1, dx1b, dg_mlp = _rms_bwd(x1, small["norm_mlp_g"], dh2, dx2, want_bf16=True, name="rms_mlp_bwd")
    dw["out"] = tn_mm(merged, dx1b, tka=1024, tn=1024, name="dw_out")
    dmerged = mm(dx1b, wts["out"], nt=True, tn=1024, tk=1024, out_dtypes=[f32], name="d_merged")
    dpa, dpb, dgl, dbg = _merge_bwd(dmerged, pa, pb, gl, small["b_gates"], name="merge_bwd")
    dw["pa"] = tn_mm(ya, dpa, tka=1024, tn=1024, name="dw_pa")
    dw["pb"] = tn_mm(yb, dpb, tka=1024, tn=1024, name="dw_pb")
    dya = mm(dpa, wts["pa"], nt=True, tn=1024, tk=1024, out_dtypes=[f32], name="d_ya")
    dyb = mm(dpb, wts["pb"], nt=True, tn=1024, tk=1024, out_dtypes=[f32], name="d_yb")
    swapped = swap_ride(dw) if swap_ride is not None else None
    duv, dwsp, dbsp_t, dvg, dvb, *got_pair = _gmlp_bwd(uv, dya, small["v_norm_g"], small["v_norm_b"], wsp, bsp_t,
                                                       ride=swapped, name="gmlp_bwd")
    ride = bwd_ride(swapped, got_pair[0]) if bwd_ride is not None else None
    dz, dxbc, ddt, dcw, dcb, ddtb, dalog, ddsk, dgs, *got = _ssd_bwd(
        xbc, cv, z, dtr, hprev, dyb, small["conv_w"], dtb, alog, dsk, small["ssm_norm_g"],
        _head_seg_matrix(), ride=ride, name="ssd_bwd")
    dw["uv"] = tn_mm(h, duv, tka=1024, tn=1024, name="dw_uv")
    dw["z"] = tn_mm(h, dz, tka=1024, tn=1024, name="dw_z")
    dw["xbc"] = tn_mm(h, dxbc, tka=1024, tn=1024, name="dw_xbc")
    dw["dt"] = tn_mm(h, ddt, tka=1024, tn=LANES, name="dw_dt")
    dw["gate"] = tn_mm(h, dgl, tka=1024, tn=1024, name="dw_gate")
    last = last_ride(dw) if last_ride is not None else None
    res = _matmul_nt_sum(
        [(duv, wts["uv"]), (dz, wts["z"]), (dxbc, wts["xbc"]), (dgl, wts["gate"]), (ddt, wts["dt"])],
        tm=tm, tks=[1024] * 4 + [LANES], ride=last, name="d_h")
    dh, got_last = (res[0], res[1]) if last is not None else (res, None)
    dx, dg_mix = _rms_bwd(x, small["norm_mix_g"], dh, dx1, want_bf16=False, name="rms_mix_bwd")

    dsmall = {
        "norm_mix_g": dg_mix, "conv_w": dcw, "conv_b": dcb, "dt_bias": ddtb[:, :N_HEADS], "a_log": dalog[:, :N_HEADS],
        "d_skip": ddsk[:, :N_HEADS], "ssm_norm_g": dgs, "v_norm_g": dvg, "v_norm_b": dvb, "w_spatial": dwsp,
        "b_spatial": dbsp_t[:, :GMLP_GROUPS].T, "b_gates": dbg, "norm_mlp_g": dg_mlp, "norm_final_g": dgf,
    }
    return loss, dx, dw, dsmall, (got[0] if got else None), got_last


_IN_SHARD = IN_PROJ // N_CHIPS
_LATE = ("w_proj_a", "w_proj_b", "w_out", "w_mlp_up", "w_mlp_down")
_LATE_ROWS = {"w_proj_a": GMLP_WIDTH // N_CHIPS, "w_proj_b": D_INNER // N_CHIPS, "w_out": D_MODEL // N_CHIPS,
              "w_mlp_up": D_MODEL, "w_mlp_down": D_FF // N_CHIPS}
_LATE_TOTAL = sum(_LATE_ROWS.values())


def _late_offsets():
    off, out = 0, {}
    for k in _LATE:
        out[k] = off
        off += _LATE_ROWS[k]
    return out


_LATE_OFF = _late_offsets()

_SMALL = ("norm_mix_g", "conv_w", "conv_b", "dt_bias", "a_log", "d_skip", "ssm_norm_g", "v_norm_g", "v_norm_b",
          "w_spatial", "b_spatial", "b_gates", "norm_mlp_g", "norm_final_g")


def _pack_small(parts):
    flat = jnp.concatenate([parts[k].reshape(-1) for k in _SMALL])
    rows = -(-flat.shape[0] // (16 * LANES)) * 16
    return jnp.pad(flat, (0, rows * LANES - flat.shape[0])).reshape(rows, LANES)


def _unpack_small(packed, shapes):
    flat = packed.reshape(-1)
    out, off = {}, 0
    for k in _SMALL:
        n = math.prod(shapes[k])
        out[k] = flat[off:off + n].reshape(shapes[k])
        off += n
    return out


def _from_chip_columns(stacked):
    _, rows, cols = stacked.shape
    return stacked.transpose(1, 0, 2).reshape(rows, N_CHIPS * cols)


def _to_chip_columns(full):
    rows, cols = full.shape
    return full.reshape(rows, N_CHIPS, cols // N_CHIPS).transpose(1, 0, 2)


def _w_in_grad_by_chip(dw):
    pieces = [dw["uv"], dw["z"], dw["xbc"], dw["dt"][:, :N_HEADS], dw["gate"]]
    bounds = [0]
    for p in pieces:
        bounds.append(bounds[-1] + p.shape[1])
    chips = []
    for k in range(N_CHIPS):
        lo, hi = k * _IN_SHARD, (k + 1) * _IN_SHARD
        parts = [p[:, max(lo, b0) - b0:min(hi, b1) - b0]
                 for p, b0, b1 in zip(pieces, bounds[:-1], bounds[1:]) if min(hi, b1) > max(lo, b0)]
        chips.append(jnp.concatenate(parts, axis=1))
    return jnp.stack(chips)


def kernel(x, norm_mix_g, w_in, conv_w, conv_b, dt_bias, a_log, d_skip, ssm_norm_g, v_norm_g, v_norm_b, w_spatial, b_spatial, b_gates, w_proj_a, w_proj_b, w_out, norm_mlp_g, w_mlp_up, w_mlp_down, norm_final_g, loss_target, m_norm_mix_g, m_w_in, m_conv_w, m_conv_b, m_dt_bias, m_a_log, m_d_skip, m_ssm_norm_g, m_v_norm_g, m_v_norm_b, m_w_spatial, m_b_spatial, m_b_gates, m_w_proj_a, m_w_proj_b, m_w_out, m_norm_mlp_g, m_w_mlp_up, m_w_mlp_down, m_norm_final_g, v_norm_mix_g, v_w_in, v_conv_w, v_conv_b, v_dt_bias, v_a_log, v_d_skip, v_ssm_norm_g, v_v_norm_g, v_v_norm_b, v_w_spatial, v_b_spatial, v_b_gates, v_w_proj_a, v_w_proj_b, v_w_out, v_norm_mlp_g, v_w_mlp_up, v_w_mlp_down, v_norm_final_g):
    given = dict(locals())
    names = ("norm_mix_g", "w_in", "conv_w", "conv_b", "dt_bias", "a_log", "d_skip", "ssm_norm_g", "v_norm_g",
             "v_norm_b", "w_spatial", "b_spatial", "b_gates", "w_proj_a", "w_proj_b", "w_out", "norm_mlp_g",
             "w_mlp_up", "w_mlp_down", "norm_final_g")
    xi, yi, ci = lax.axis_index("x"), lax.axis_index("y"), lax.axis_index("c")
    me_chip = (2 * xi + yi).astype(jnp.int32)

    def halves(a):
        return a.reshape(2, a.shape[0] // 2, a.shape[1])

    def with_own(got, shard):
        whole = lax.dynamic_update_slice(got, shard[None], (me_chip, 0, 0, 0))
        return whole.reshape(N_CHIPS, 2 * shard.shape[1], shard.shape[2])

    shard_in = halves(_cast_bf16(w_in[0], name="cast_w_in"))
    shard_late = halves(_cast_bf16(jnp.concatenate([given[k][0] for k in _LATE]), name="cast_w_late"))
    shard_conv = halves(conv_w.reshape(2 * _TAIL, -1))
    w_in_full = _from_chip_columns(with_own(_gather_shards(shard_in, name="gather_w_in"), shard_in))
    o_dt, o_gate = 2 * GMLP_WIDTH + D_INNER + CONV_DIM, 2 * GMLP_WIDTH + D_INNER + CONV_DIM + N_HEADS
    wts = {
        "uv": w_in_full[:, :2 * GMLP_WIDTH], "z": w_in_full[:, 2 * GMLP_WIDTH:2 * GMLP_WIDTH + D_INNER],
        "xbc": w_in_full[:, 2 * GMLP_WIDTH + D_INNER:o_dt], "dt": _pad_lanes(w_in_full[:, o_dt:o_gate]),
        "gate": w_in_full[:, o_gate:],
    }
    conv_all = with_own(_gather_shards(shard_conv, name="gather_conv_w"), shard_conv)
    conv_full = _from_chip_columns(conv_all.reshape(N_CHIPS, CONV_W, CONV_DIM // N_CHIPS))

    def late_weights(got):
        g_late = with_own(got, shard_late)

        def rows_of(k):
            return g_late[:, _LATE_OFF[k]:_LATE_OFF[k] + _LATE_ROWS[k]]

        return {
            "pa": rows_of("w_proj_a").reshape(GMLP_WIDTH, D_MODEL),
            "pb": rows_of("w_proj_b").reshape(D_INNER, D_MODEL), "out": rows_of("w_out").reshape(D_MODEL, D_MODEL),
            "up": _from_chip_columns(rows_of("w_mlp_up")), "down": rows_of("w_mlp_down").reshape(D_FF, D_MODEL),
        }

    small = {
        "norm_mix_g": norm_mix_g, "conv_w": conv_full, "conv_b": conv_b, "dt_bias": dt_bias, "a_log": a_log,
        "d_skip": d_skip, "ssm_norm_g": ssm_norm_g, "v_norm_g": v_norm_g, "v_norm_b": v_norm_b,
        "w_spatial": w_spatial[0], "b_spatial": b_spatial[0], "b_gates": b_gates, "norm_mlp_g": norm_mlp_g,
        "norm_final_g": norm_final_g.reshape(1, D_MODEL),
    }

    c_idx = ci.astype(jnp.int32).reshape(1)
    place = jnp.stack([me_chip, ci.astype(jnp.int32)])
    partials = {}

    def reduced_shard(tag, got_chips):
        own = _rs_add_chips(*partials[tag], got_chips, place, name="rs_add_chips_" + tag)
        both = _rs_join_halves(own, name="rs_join_" + tag)
        return both.reshape(2 * both.shape[1], both.shape[2])

    def late_grads(dw):
        def by_rows(a):
            return a.reshape(N_CHIPS, a.shape[0] // N_CHIPS, a.shape[1])

        g = jnp.concatenate([by_rows(dw["pa"]), by_rows(dw["pb"]), by_rows(dw["out"]), _to_chip_columns(dw["up"]),
                             by_rows(dw["down"])], axis=1)
        return g.reshape(N_CHIPS, 2, g.shape[1] // 2, g.shape[2])

    def late_partials(g, got_pair):
        partials["late"] = (g, got_pair)
        return _rs_add_pair(g, got_pair, c_idx, name="rs_add_pair_late")

    def in_partials(dw):
        g = _w_in_grad_by_chip(dw).reshape(N_CHIPS, 2, D_MODEL // 2, _IN_SHARD)
        got_pair = _rs_swap_halves(g, name="rs_swap_in")
        partials["in"] = (g, got_pair)
        return _rs_add_pair(g, got_pair, c_idx, name="rs_add_pair_in")

    loss_part, grad_x, dw, dsmall, got_late, got_in = _local_grads(
        x[0], loss_target[0], wts, small, fwd_ride=shard_late, late_weights=late_weights, swap_ride=late_grads,
        bwd_ride=late_partials, last_ride=in_partials)
    loss = lax.psum(loss_part[0, 0], ("x", "y", "c"))
    g_late = reduced_shard("late", got_late)
    g_in_shard = reduced_shard("in", got_in)

    small_shapes = {k: dsmall[k].shape for k in _SMALL}
    red = _unpack_small(_all_reduce_small(_pack_small(dsmall), name="all_reduce_small"), small_shapes)
    conv_cols = CONV_DIM // N_CHIPS
    red["conv_w"] = lax.dynamic_slice_in_dim(red["conv_w"], me_chip * conv_cols, conv_cols, axis=1)

    grads, deltas, new_m, new_v = {}, {}, {}, {}
    for k in ("w_in",) + _LATE:
        g2 = g_in_shard if k == "w_in" else g_late[_LATE_OFF[k]:_LATE_OFF[k] + _LATE_ROWS[k]]
        dlt, m2, v2 = _adamw(given[k][0], g2, given["m_" + k][0], given["v_" + k][0], name="adamw_" + k)
        grads[k], deltas[k], new_m[k], new_v[k] = g2, dlt, m2, v2
    adam_shapes = dict(small_shapes)
    adam_shapes["conv_w"] = (CONV_W, conv_cols)

    def small_pack_of(prefix):
        return _pack_small({k: given[prefix + k].reshape(adam_shapes[k]) for k in _SMALL})

    dlt_s, m_s, v_s = _adamw(small_pack_of(""), _pack_small(red), small_pack_of("m_"), small_pack_of("v_"),
                             name="adamw_small")
    for dst, packed in ((deltas, dlt_s), (new_m, m_s), (new_v, v_s)):
        dst.update(_unpack_small(packed, adam_shapes))
    grads.update(red)

    def shaped(dct):
        return [dct[k].reshape(given[k].shape) for k in names]

    return (loss, grad_x[None], *shaped(grads), *shaped(deltas), *shaped(new_m), *shaped(new_v))
```

```python
import functools
import math

import jax
import jax.numpy as jnp
from jax import lax
from jax.experimental import pallas as pl
from jax.experimental.pallas import tpu as pltpu

f32 = jnp.float32
bf16 = jnp.bfloat16

D_MODEL = 1024
CHUNK = 128
GMLP_WIDTH = 1024
GMLP_GROUPS = 8
D_INNER = 2048
HEAD_DIM = 64
N_HEADS = 32
N_GROUPS = 8
HEADS_PER_GROUP = 4
GROUP_W = HEADS_PER_GROUP * HEAD_DIM
D_STATE = 128
CONV_W = 4
CONV_DIM = 4096
D_FF = 4096
IN_PROJ = 10272
NORM_EPS = 1e-6
N_CHIPS = 4
N_DEV = 8
LANES = 128

ADAM_LR = 0.001
ADAM_B1 = 0.9
ADAM_B2 = 0.999
ADAM_EPS = 1e-08
ADAM_WD = 0.01
ADAM_STEP = 10

MESH = pl.DeviceIdType.MESH
_NT = (((1,), (1,)), ((), ()))
_NN = (((1,), (0,)), ((), ()))
_TN = (((0,), (0,)), ((), ()))
_MB = 2 ** 20


def _params(sem, vmem_mb=48):
    return pltpu.CompilerParams(dimension_semantics=sem, vmem_limit_bytes=vmem_mb * _MB)


def _dot(a, b, dims=_NN):
    return lax.dot_general(a.astype(bf16), b.astype(bf16), dims, preferred_element_type=f32)


def _dot32(a, b):
    return jnp.dot(a, b, preferred_element_type=f32, precision=lax.Precision.HIGHEST)


def _sigmoid(x):
    return 1.0 / (1.0 + jnp.exp(-x))


def _sum_all(a):
    return jnp.sum(jnp.sum(a, axis=1, keepdims=True), axis=0, keepdims=True)


def _iota(shape, dim):
    return lax.broadcasted_iota(jnp.int32, shape, dim)


def _matmul(a, b, *, nt=False, tm, tn, tk, out_dtypes, epilogue=None, extras=(), extra_specs=None, name):
    m, k_dim = a.shape
    n = b.shape[0] if nt else b.shape[1]
    nk = k_dim // tk
    ne, no = len(extras), len(out_dtypes)
    dims = _NT if nt else _NN

    def body(*refs):
        a_ref, b_ref = refs[0], refs[1]
        ex = refs[2:2 + ne]
        outs = refs[2 + ne:2 + ne + no]

        def finish(acc):
            vals = epilogue(acc, *[e[...] for e in ex]) if epilogue is not None else (acc,)
            for o, v in zip(outs, vals):
                o[...] = v.astype(o.dtype)

        part = lax.dot_general(a_ref[...], b_ref[...], dims, preferred_element_type=f32)
        if nk == 1:
            finish(part)
        else:
            acc_ref = refs[-1]
            kk = pl.program_id(2)

            @pl.when(kk == 0)
            def _():
                acc_ref[...] = part

            @pl.when(kk > 0)
            def _():
                acc_ref[...] += part

            @pl.when(kk == nk - 1)
            def _():
                finish(acc_ref[...])

    b_spec = pl.BlockSpec((tn, tk), lambda i, j, k: (j, k)) if nt else pl.BlockSpec((tk, tn), lambda i, j, k: (k, j))
    tile = pl.BlockSpec((tm, tn), lambda i, j, k: (i, j))
    ex_specs = [tile if s is None else s for s in (extra_specs or [None] * ne)]
    outs = pl.pallas_call(
        body, name=name, grid=(m // tm, n // tn, nk),
        in_specs=[pl.BlockSpec((tm, tk), lambda i, j, k: (i, k)), b_spec] + ex_specs,
        out_specs=[tile] * no,
        out_shape=[jax.ShapeDtypeStruct((m, n), dt) for dt in out_dtypes],
        scratch_shapes=[pltpu.VMEM((tm, tn), f32)] if nk > 1 else [],
        compiler_params=_params(("parallel", "parallel", "arbitrary")),
    )(a, b, *extras)
    return outs if no > 1 else outs[0]


def _matmul_nt_sum(pairs, *, tm, tks, ride=None, name):
    m = pairs[0][0].shape[0]
    n = pairs[0][1].shape[0]
    nblk = [a.shape[1] // tk for (a, _), tk in zip(pairs, tks)]
    starts = [sum(nblk[:p]) for p in range(len(pairs))]
    nk = sum(nblk)
    npairs = len(pairs)
    ni = m // tm
    riding = ride is not None

    def body(*refs):
        rest = refs[2 * npairs:]
        if riding:
            ride_ref, o_ref, got_ref, acc_ref, send_sems, recv_sems = rest
        else:
            o_ref, acc_ref = rest
        i, kk = pl.program_id(0), pl.program_id(1)
        if riding:
            start, finish = _scatter_protocol(ride_ref, got_ref, send_sems, recv_sems)
            pl.when((i == 0) & (kk == 0))(start)

        @pl.when(kk == 0)
        def _():
            acc_ref[...] = jnp.zeros_like(acc_ref)

        for p in range(npairs):
            @pl.when((kk >= starts[p]) & (kk < starts[p] + nblk[p]))
            def _(p=p):
                acc_ref[...] += lax.dot_general(refs[2 * p][...], refs[2 * p + 1][...], _NT, preferred_element_type=f32)

        @pl.when(kk == nk - 1)
        def _():
            o_ref[...] = acc_ref[...]

        if riding:
            pl.when((i == ni - 1) & (kk == nk - 1))(finish)

    in_specs, args = [], []
    for p, (a, b) in enumerate(pairs):
        def kblock(k, s=starts[p], nb=nblk[p]):
            return jnp.clip(k - s, 0, nb - 1)
        in_specs.append(pl.BlockSpec((tm, tks[p]), lambda i, k, kb=kblock: (i, kb(k))))
        in_specs.append(pl.BlockSpec((n, tks[p]), lambda i, k, kb=kblock: (0, kb(k))))
        args += [a, b]
    tile = pl.BlockSpec((tm, n), lambda i, k: (i, 0))
    outs = pl.pallas_call(
        body, name=name, grid=(ni, nk), in_specs=in_specs + [_ANY] * riding, out_specs=[tile] + [_ANY] * riding,
        out_shape=[jax.ShapeDtypeStruct((m, n), f32)]
        + ([jax.ShapeDtypeStruct((N_CHIPS - 1,) + ride.shape[1:], ride.dtype)] if riding else []),
        scratch_shapes=[pltpu.VMEM((tm, n), f32)] + (list(_SCATTER_SCRATCH) if riding else []),
        compiler_params=_params(("arbitrary", "arbitrary"), vmem_mb=56),
    )(*args, *([ride] if riding else []))
    return outs if riding else outs[0]


def _matmul_tn(a, b, *, tka, tn, tt, name):
    t, ka = a.shape
    n = b.shape[1]

    def body(a_ref, b_ref, o_ref):
        part = lax.dot_general(a_ref[...], b_ref[...], _TN, preferred_element_type=f32)
        kk = pl.program_id(2)

        @pl.when(kk == 0)
        def _():
            o_ref[...] = part

        @pl.when(kk > 0)
        def _():
            o_ref[...] += part

    return pl.pallas_call(
        body, name=name, grid=(ka // tka, n // tn, t // tt),
        in_specs=[pl.BlockSpec((tt, tka), lambda i, j, k: (k, i)), pl.BlockSpec((tt, tn), lambda i, j, k: (k, j))],
        out_specs=pl.BlockSpec((tka, tn), lambda i, j, k: (i, j)),
        out_shape=jax.ShapeDtypeStruct((ka, n), f32),
        compiler_params=_params(("parallel", "parallel", "arbitrary")),
    )(a, b)


def _row_tile(t):
    return min(t, 512)


def _rms_fwd(x, g, *, name):
    t, d = x.shape
    tr = _row_tile(t)

    def body(x_ref, g_ref, h_ref):
        xv = x_ref[...]
        r = lax.rsqrt(jnp.mean(xv * xv, axis=1, keepdims=True) + NORM_EPS)
        h_ref[...] = (xv * r * g_ref[...]).astype(bf16)

    return pl.pallas_call(
        body, name=name, grid=(t // tr,),
        in_specs=[pl.BlockSpec((tr, d), lambda i: (i, 0)), pl.BlockSpec((1, d), lambda i: (0, 0))],
        out_specs=pl.BlockSpec((tr, d), lambda i: (i, 0)),
        out_shape=jax.ShapeDtypeStruct((t, d), bf16),
        compiler_params=_params(("parallel",)),
    )(x, g)


def _rms_bwd(xin, g, dh, dres, *, want_bf16, name):
    t, d = xin.shape
    tr = _row_tile(t)

    def body(x_ref, g_ref, dh_ref, dres_ref, dx_ref, *rest):
        dg_ref = rest[-1]
        xv = x_ref[...]
        r = lax.rsqrt(jnp.mean(xv * xv, axis=1, keepdims=True) + NORM_EPS)
        xn = xv * r
        dhv = dh_ref[...]
        dxn = dhv * g_ref[...]
        dx = dres_ref[...] + r * (dxn - xn * jnp.mean(dxn * xn, axis=1, keepdims=True))
        dx_ref[...] = dx
        if want_bf16:
            rest[0][...] = dx.astype(bf16)
        part = jnp.sum(dhv * xn, axis=0, keepdims=True)

        @pl.when(pl.program_id(0) == 0)
        def _():
            dg_ref[...] = part

        @pl.when(pl.program_id(0) > 0)
        def _():
            dg_ref[...] += part

    row = pl.BlockSpec((tr, d), lambda i: (i, 0))
    vec = pl.BlockSpec((1, d), lambda i: (0, 0))
    out_shape = [jax.ShapeDtypeStruct((t, d), f32)] + ([jax.ShapeDtypeStruct((t, d), bf16)] if want_bf16 else []) \
        + [jax.ShapeDtypeStruct((1, d), f32)]
    return pl.pallas_call(
        body, name=name, grid=(t // tr,),
        in_specs=[row, vec, row, row],
        out_specs=[row] + ([row] if want_bf16 else []) + [vec],
        out_shape=out_shape,
        compiler_params=_params(("arbitrary",)),
    )(xin, g, dh, dres)


def _loss_head(x2, tgt, g, *, name):
    t, d = x2.shape
    tr = _row_tile(t)

    def body(x_ref, t_ref, g_ref, dx_ref, dxb_ref, dg_ref, loss_ref):
        xv = x_ref[...]
        gv = g_ref[...]
        r = lax.rsqrt(jnp.mean(xv * xv, axis=1, keepdims=True) + NORM_EPS)
        xn = xv * r
        e = xn * gv - t_ref[...]
        lpart = jnp.zeros((1, LANES), f32) + 0.5 * _sum_all(jnp.mean(e * e, axis=1, keepdims=True))
        dy = e * (1.0 / d)
        dxn = dy * gv
        dx = r * (dxn - xn * jnp.mean(dxn * xn, axis=1, keepdims=True))
        dx_ref[...] = dx
        dxb_ref[...] = dx.astype(bf16)
        gpart = jnp.sum(dy * xn, axis=0, keepdims=True)

        @pl.when(pl.program_id(0) == 0)
        def _():
            dg_ref[...] = gpart
            loss_ref[...] = lpart

        @pl.when(pl.program_id(0) > 0)
        def _():
            dg_ref[...] += gpart
            loss_ref[...] += lpart

    row = pl.BlockSpec((tr, d), lambda i: (i, 0))
    vec = pl.BlockSpec((1, d), lambda i: (0, 0))
    return pl.pallas_call(
        body, name=name, grid=(t // tr,),
        in_specs=[row, row, vec],
        out_specs=[row, row, vec, pl.BlockSpec((1, LANES), lambda i: (0, 0))],
        out_shape=[jax.ShapeDtypeStruct((t, d), f32), jax.ShapeDtypeStruct((t, d), bf16),
                   jax.ShapeDtypeStruct((1, d), f32), jax.ShapeDtypeStruct((1, LANES), f32)],
        compiler_params=_params(("arbitrary",)),
    )(x2, tgt, g)


def _merge_bwd(dm, pa, pb, gl, bg, *, name):
    t, d = pa.shape
    tr = _row_tile(t)

    def body(dm_ref, pa_ref, pb_ref, gla_ref, glb_ref, bga_ref, bgb_ref, dpa_ref, dpb_ref, dgl_ref, dbg_ref):
        dmv = dm_ref[...]
        ga = _sigmoid(gla_ref[...] + bga_ref[...])
        gb = _sigmoid(glb_ref[...] + bgb_ref[...])
        dpa_ref[...] = (dmv * ga).astype(bf16)
        dpb_ref[...] = (dmv * gb).astype(bf16)
        dla = dmv * pa_ref[...] * ga * (1.0 - ga)
        dlb = dmv * pb_ref[...] * gb * (1.0 - gb)
        dgl_ref[:, :d] = dla.astype(bf16)
        dgl_ref[:, d:] = dlb.astype(bf16)
        sa = jnp.sum(dla, axis=0, keepdims=True)
        sb = jnp.sum(dlb, axis=0, keepdims=True)

        @pl.when(pl.program_id(0) == 0)
        def _():
            dbg_ref[:, :d] = sa
            dbg_ref[:, d:] = sb

        @pl.when(pl.program_id(0) > 0)
        def _():
            dbg_ref[:, :d] += sa
            dbg_ref[:, d:] += sb

    row = pl.BlockSpec((tr, d), lambda i: (i, 0))
    return pl.pallas_call(
        body, name=name, grid=(t // tr,),
        in_specs=[row, row, row, row, pl.BlockSpec((tr, d), lambda i: (i, 1)),
                  pl.BlockSpec((1, d), lambda i: (0, 0)), pl.BlockSpec((1, d), lambda i: (0, 1))],
        out_specs=[row, row, pl.BlockSpec((tr, 2 * d), lambda i: (i, 0)), pl.BlockSpec((1, 2 * d), lambda i: (0, 0))],
        out_shape=[jax.ShapeDtypeStruct((t, d), bf16), jax.ShapeDtypeStruct((t, d), bf16),
                   jax.ShapeDtypeStruct((t, 2 * d), bf16), jax.ShapeDtypeStruct((1, 2 * d), f32)],
        compiler_params=_params(("arbitrary",)),
    )(dm, pa, pb, gl, gl, bg, bg)


_INV_SQRT2 = 1.0 / math.sqrt(2.0)
_INV_SQRT2PI = 1.0 / math.sqrt(2.0 * math.pi)


def _gmlp_common(uv, vg, vb, with_grad=False):
    cdf = 0.5 * (1.0 + lax.erf(uv * _INV_SQRT2))
    zz = uv * cdf
    u, vhat, rstd, vn = _gmlp_norm(zz, vg, vb)
    if not with_grad:
        return u, vhat, rstd, vn
    return u, vhat, rstd, vn, cdf + uv * jnp.exp(-0.5 * uv * uv) * _INV_SQRT2PI


def _gmlp_norm(zz, vg, vb):
    u = zz[:, :GMLP_WIDTH]
    v = zz[:, GMLP_WIDTH:]
    mu = jnp.mean(v, axis=1, keepdims=True)
    vc = v - mu
    rstd = lax.rsqrt(jnp.mean(vc * vc, axis=1, keepdims=True) + NORM_EPS)
    vhat = vc * rstd
    vn = vhat * vg + vb
    return u, vhat, rstd, vn


def _gmlp_fwd(uv, vg, vb, wsp, bsp_t, *, name):
    t = uv.shape[0]
    per_step = 4 if t % (4 * CHUNK) == 0 else 1
    rows = per_step * CHUNK

    def body(uv_ref, vg_ref, vb_ref, w_ref, b_ref, y_ref):
        tril = _iota((CHUNK, CHUNK), 0) >= _iota((CHUNK, CHUNK), 1)
        bt = b_ref[...]
        for q in range(per_step):
            qs = slice(q * CHUNK, (q + 1) * CHUNK)
            u, _, _, vn = _gmlp_common(uv_ref[qs, :], vg_ref[...], vb_ref[...])
            for g in range(GMLP_GROUPS):
                sl = slice(g * CHUNK, (g + 1) * CHUNK)
                w = jnp.where(tril, w_ref[g], 0.0)
                s = _dot(w, vn[:, sl]) + bt[:, g:g + 1]
                y_ref[qs, sl] = (u[:, sl] * s).astype(bf16)

    return pl.pallas_call(
        body, name=name, grid=(t // rows,),
        in_specs=[pl.BlockSpec((rows, 2 * GMLP_WIDTH), lambda c: (c, 0)),
                  pl.BlockSpec((1, GMLP_WIDTH), lambda c: (0, 0)), pl.BlockSpec((1, GMLP_WIDTH), lambda c: (0, 0)),
                  pl.BlockSpec((GMLP_GROUPS, CHUNK, CHUNK), lambda c: (0, 0, 0)),
                  pl.BlockSpec((CHUNK, LANES), lambda c: (0, 0))],
        out_specs=pl.BlockSpec((rows, GMLP_WIDTH), lambda c: (c, 0)),
        out_shape=jax.ShapeDtypeStruct((t, GMLP_WIDTH), bf16),
        compiler_params=_params(("parallel",)),
    )(uv, vg, vb, wsp, bsp_t)


def _gmlp_bwd(uv, dya, vg, vb, wsp, bsp_t, *, ride=None, name):
    t = uv.shape[0]
    nc = t // CHUNK
    riding = ride is not None

    def body(*refs):
        uv_ref, dy_ref, vg_ref, vb_ref, w_ref, b_ref = refs[:6]
        duv_ref, dw_ref, db_ref, dvg_ref, dvb_ref = refs[6 + riding:11 + riding]
        first = pl.program_id(0) == 0
        if riding:
            start, finish = _swap_protocol(refs[6], refs[12], refs[13], refs[14])
            pl.when(first)(start)

        @pl.when(first)
        def _():
            dw_ref[...] = jnp.zeros_like(dw_ref)
            db_ref[...] = jnp.zeros_like(db_ref)
            dvg_ref[...] = jnp.zeros_like(dvg_ref)
            dvb_ref[...] = jnp.zeros_like(dvb_ref)

        uvv = uv_ref[...]
        vgv = vg_ref[...]
        u, vhat, rstd, vn, gelu_grad = _gmlp_common(uvv, vgv, vb_ref[...], with_grad=True)
        dy = dy_ref[...]
        tril = _iota((CHUNK, CHUNK), 0) >= _iota((CHUNK, CHUNK), 1)
        lane = _iota((CHUNK, LANES), 1)
        bt = b_ref[...]
        ds_all = dy * u
        dbacc = jnp.zeros((CHUNK, LANES), f32)
        dvh_parts = []
        for g in range(GMLP_GROUPS):
            sl = slice(g * CHUNK, (g + 1) * CHUNK)
            w = jnp.where(tril, w_ref[g], 0.0)
            vng = vn[:, sl]
            s = _dot(w, vng) + bt[:, g:g + 1]
            ds = ds_all[:, sl]
            duv_ref[:, sl] = (dy[:, sl] * s * gelu_grad[:, sl]).astype(bf16)
            dw_ref[g] += jnp.where(tril, _dot(ds, vng, _NT), 0.0)
            dbacc = dbacc + jnp.where(lane == g, jnp.sum(ds, axis=1, keepdims=True), 0.0)
            dvn = _dot(w, ds, _TN)
            vh = vhat[:, sl]
            dvg_ref[:, sl] += jnp.sum(dvn * vh, axis=0, keepdims=True)
            dvb_ref[:, sl] += jnp.sum(dvn, axis=0, keepdims=True)
            dvh_parts.append(dvn * vgv[:, sl])
        db_ref[...] += dbacc
        dvhat = jnp.concatenate(dvh_parts, axis=1)
        m1 = jnp.mean(dvhat, axis=1, keepdims=True)
        m2 = jnp.mean(dvhat * vhat, axis=1, keepdims=True)
        dv = rstd * (dvhat - m1 - vhat * m2)
        duv_ref[:, GMLP_WIDTH:] = (dv * gelu_grad[:, GMLP_WIDTH:]).astype(bf16)
        if riding:
            pl.when(pl.program_id(0) == nc - 1)(finish)

    vec = pl.BlockSpec((1, GMLP_WIDTH), lambda c: (0, 0))
    return pl.pallas_call(
        body, name=name, grid=(nc,),
        in_specs=[pl.BlockSpec((CHUNK, 2 * GMLP_WIDTH), lambda c: (c, 0)),
                  pl.BlockSpec((CHUNK, GMLP_WIDTH), lambda c: (c, 0)), vec, vec,
                  pl.BlockSpec((GMLP_GROUPS, CHUNK, CHUNK), lambda c: (0, 0, 0)),
                  pl.BlockSpec((CHUNK, LANES), lambda c: (0, 0))] + [_ANY] * riding,
        out_specs=[pl.BlockSpec((CHUNK, 2 * GMLP_WIDTH), lambda c: (c, 0)),
                   pl.BlockSpec((GMLP_GROUPS, CHUNK, CHUNK), lambda c: (0, 0, 0)),
                   pl.BlockSpec((CHUNK, LANES), lambda c: (0, 0)), vec, vec] + [_ANY] * riding,
        out_shape=[jax.ShapeDtypeStruct((t, 2 * GMLP_WIDTH), bf16),
                   jax.ShapeDtypeStruct((GMLP_GROUPS, CHUNK, CHUNK), f32),
                   jax.ShapeDtypeStruct((CHUNK, LANES), f32),
                   jax.ShapeDtypeStruct((1, GMLP_WIDTH), f32), jax.ShapeDtypeStruct((1, GMLP_WIDTH), f32)]
        + ([jax.ShapeDtypeStruct(ride.shape[:1] + ride.shape[2:], ride.dtype)] if riding else []),
        scratch_shapes=list(_SWAP_SCRATCH) if riding else [],
        compiler_params=_params(("arbitrary",)),
    )(uv, dya, vg, vb, wsp, bsp_t, *([ride] if riding else []))


_CONV_COLS = 512
_XS0, _B0, _C0 = 0, D_INNER, D_INNER + N_GROUPS * D_STATE


_TAIL = 8


def _conv_silu(cur_ref, tail_ref, w_ref, b_ref, has_prev, xc_ref, cv_ref):
    row = _iota((_TAIL, _CONV_COLS), 0)
    for j in range(CONV_DIM // _CONV_COLS):
        sl = slice(j * _CONV_COLS, (j + 1) * _CONV_COLS)
        cur = cur_ref[:, sl]
        tail = jnp.where(has_prev, tail_ref[:, sl], 0.0)
        acc = cur * w_ref[CONV_W - 1:CONV_W, sl] + b_ref[:, sl]
        for s in range(1, CONV_W):
            rolled = pltpu.roll(cur, s, 0)
            top = jnp.where(row >= s, rolled[:_TAIL], pltpu.roll(tail, s, 0))
            sh = jnp.concatenate([top, rolled[_TAIL:]], axis=0)
            acc = acc + sh * w_ref[CONV_W - 1 - s:CONV_W - s, sl]
        cv_ref[:, sl] = acc
        xc_ref[:, sl] = acc * _sigmoid(acc)


def _col_bcast(mat, h):
    return jnp.broadcast_to(mat[:, h:h + 1], (CHUNK, LANES))


def _head_expand(cols):
    lo = _iota((CHUNK, LANES), 1) < HEAD_DIM
    return jnp.concatenate([jnp.where(lo, cols[2 * j], cols[2 * j + 1]) for j in range(N_HEADS // 2)], axis=1)


def _ssd_chunk_scalars(dtr, dtb, alog):
    xdt_pre = dtr + dtb
    dtv = jnp.maximum(xdt_pre, 0.0) + jnp.log(1.0 + jnp.exp(-jnp.abs(xdt_pre)))
    a = -jnp.exp(alog)
    ltri = (_iota((CHUNK, CHUNK), 0) >= _iota((CHUNK, CHUNK), 1)).astype(f32)
    cs = _dot32(ltri, dtv * a)
    csb = [_col_bcast(cs, h) for h in range(N_HEADS)]
    cs_x = _head_expand(csb)
    dt_x = _head_expand([_col_bcast(dtv, h) for h in range(N_HEADS)])
    cl_x = cs_x[CHUNK - 1:CHUNK, :]
    return dict(xdt_pre=xdt_pre, dtv=dtv, a=a, cs=cs, cs_t=cs.T, csb=csb, dt_x=dt_x, e_x=jnp.exp(cs_x),
                dec_x=jnp.exp(cl_x - cs_x), dk_x=jnp.exp(cl_x))


def _head_masks():
    lane = _iota((CHUNK, GROUP_W), 1)
    return [(lane >= r * HEAD_DIM) & (lane < (r + 1) * HEAD_DIM) for r in range(HEADS_PER_GROUP)]


def _stack_heads(a, masks):
    return jnp.concatenate([jnp.where(m, a, 0.0) for m in masks], axis=0).astype(bf16)


def _seg_sum(a, seg):
    hi = a.astype(jnp.bfloat16)
    lo = (a - hi.astype(f32)).astype(jnp.bfloat16)
    return (lax.dot_general(hi, seg, _NN, preferred_element_type=f32)
            + lax.dot_general(lo, seg, _NN, preferred_element_type=f32))


def _head_seg_matrix():
    return (_iota((D_INNER, LANES), 0) // HEAD_DIM == _iota((D_INNER, LANES), 1)).astype(jnp.bfloat16)


def _ssd_fwd(xbc, z, dtr, cw, cb, dtb, alog, dsk_x, gs, *, ride=None, name):
    t = xbc.shape[0]
    nc = t // CHUNK
    tiles = CHUNK // _TAIL

    def body(*refs):
        cur_ref, tail_ref, z_ref, dtr_ref, cw_ref, cb_ref, dtb_ref, alog_ref, dsk_ref, gs_ref = refs[:10]
        if ride is None:
            yb_ref, hp_ref, cv_ref, state_ref, xc_ref = refs[10:]
        else:
            ride_ref, yb_ref, hp_ref, cv_ref, got_ref, state_ref, xc_ref, send_sems, recv_sems = refs[10:]
        c = pl.program_id(0)
        if ride is not None:
            start, relay, finish = _gather_protocol(ride_ref, got_ref, send_sems, recv_sems)
            pl.when(c == 0)(start)
            pl.when(c == nc // 2)(relay)

        @pl.when(c == 0)
        def _():
            state_ref[...] = jnp.zeros_like(state_ref)

        _conv_silu(cur_ref, tail_ref, cw_ref, cb_ref, c > 0, xc_ref, cv_ref)
        sc = _ssd_chunk_scalars(dtr_ref[...], dtb_ref[...], alog_ref[...])
        tril = _iota((CHUNK, CHUNK), 0) >= _iota((CHUNK, CHUNK), 1)
        masks = _head_masks()
        hp_ref[0] = state_ref[...]
        for g in range(N_GROUPS):
            gsl = slice(g * GROUP_W, (g + 1) * GROUP_W)
            xs_g = xc_ref[:, gsl]
            bg = xc_ref[:, _B0 + g * D_STATE:_B0 + (g + 1) * D_STATE]
            cg = xc_ref[:, _C0 + g * D_STATE:_C0 + (g + 1) * D_STATE]
            xdt_g = xs_g * sc["dt_x"][:, gsl]
            cbm = _dot(cg, bg, _NT)
            mw = jnp.concatenate(
                [cbm * jnp.exp(jnp.where(tril, sc["csb"][h] - sc["cs_t"][h:h + 1, :], -1e30))
                 for h in range(g * HEADS_PER_GROUP, (g + 1) * HEADS_PER_GROUP)], axis=1)
            ht_g = state_ref[:, gsl]
            y_g = _dot(mw, _stack_heads(xdt_g, masks)) + sc["e_x"][:, gsl] * _dot(cg, ht_g) + dsk_ref[:, gsl] * xs_g
            state_ref[:, gsl] = ht_g * sc["dk_x"][:, gsl] + _dot(bg, xdt_g * sc["dec_x"][:, gsl], _TN)
            zg = z_ref[:, gsl]
            yg = y_g * zg * _sigmoid(zg)
            rs = lax.rsqrt(jnp.mean(yg * yg, axis=1, keepdims=True) + NORM_EPS)
            yb_ref[:, gsl] = (yg * rs * gs_ref[:, gsl]).astype(bf16)
        if ride is not None:
            pl.when(c == nc - 1)(finish)

    def chunk(w):
        return pl.BlockSpec((CHUNK, w), lambda c: (c, 0))

    def const(shape):
        return pl.BlockSpec(shape, lambda c: (0,) * len(shape))

    riding = ride is not None
    return pl.pallas_call(
        body, name=name, grid=(nc,),
        in_specs=[chunk(CONV_DIM), pl.BlockSpec((_TAIL, CONV_DIM), lambda c: (jnp.maximum(c * tiles - 1, 0), 0)),
                  chunk(D_INNER), chunk(LANES), const((CONV_W, CONV_DIM)), const((1, CONV_DIM)),
                  const((1, LANES)), const((1, LANES)), const((1, D_INNER)), const((1, D_INNER))] + [_ANY] * riding,
        out_specs=[chunk(D_INNER), pl.BlockSpec((1, D_STATE, D_INNER), lambda c: (c, 0, 0)), chunk(CONV_DIM)]
        + [_ANY] * riding,
        out_shape=[jax.ShapeDtypeStruct((t, D_INNER), bf16), jax.ShapeDtypeStruct((nc, D_STATE, D_INNER), f32),
                   jax.ShapeDtypeStruct((t, CONV_DIM), f32)]
        + ([jax.ShapeDtypeStruct((N_CHIPS,) + ride.shape, ride.dtype)] if riding else []),
        scratch_shapes=[pltpu.VMEM((D_STATE, D_INNER), f32), pltpu.VMEM((CHUNK, CONV_DIM), f32)]
        + (list(_GATHER_SCRATCH) if riding else []),
        compiler_params=_params(("arbitrary",)),
    )(xbc, xbc, z, dtr, cw, cb, dtb, alog, dsk_x, gs, *([ride] if riding else []))


def _ssd_bwd(xbc, cv, z, dtr, hprev, dyb, cw, dtb, alog, dsk_x, gs, seg, *, ride=None, name):
    t = xbc.shape[0]
    nc = t // CHUNK

    def body(*refs):
        (cur_ref, cv_ref, z_ref, dtr_ref, hp_ref, dyb_ref, cw_ref, dtb_ref, alog_ref, dsk_ref, gs_ref,
         seg_ref) = refs[:12]
        rest = refs[12:]
        if ride is not None:
            ride_ref, got_ref, send_sems, recv_sems = rest[0], rest[10], rest[-2], rest[-1]
            rest = rest[1:10] + rest[11:-2]
        (dz_ref, dxbc_ref, ddt_ref, dcw_ref, dcb_ref, ddtb_ref, dalog_ref, ddsk_ref, dgs_ref,
         dh_ref, dcnext_ref, xc_ref, dxc_ref, x13_ref, x2_ref, rows_ref) = rest
        i = pl.program_id(0)
        if ride is not None:
            start, finish = _scatter_protocol(ride_ref, got_ref, send_sems, recv_sems)
            pl.when(i == 0)(start)

        @pl.when(i == 0)
        def _():
            for ref in (dh_ref, dcnext_ref, dcw_ref, dcb_ref, ddtb_ref, dalog_ref, ddsk_ref, dgs_ref, rows_ref):
                ref[...] = jnp.zeros_like(ref)

        for j in range(CONV_DIM // _CONV_COLS):
            sl = slice(j * _CONV_COLS, (j + 1) * _CONV_COLS)
            cvv = cv_ref[:, sl]
            xc_ref[:, sl] = cvv * _sigmoid(cvv)
        sc = _ssd_chunk_scalars(dtr_ref[...], dtb_ref[...], alog_ref[...])
        tril = _iota((CHUNK, CHUNK), 0) >= _iota((CHUNK, CHUNK), 1)
        triu = _iota((CHUNK, CHUNK), 0) <= _iota((CHUNK, CHUNK), 1)
        masks = _head_masks()
        rowh = _iota((N_HEADS, CHUNK), 0)
        dcs_t = jnp.zeros((N_HEADS, CHUNK), f32)
        for g in range(N_GROUPS):
            gsl = slice(g * GROUP_W, (g + 1) * GROUP_W)
            xs_g = xc_ref[:, gsl]
            bg = xc_ref[:, _B0 + g * D_STATE:_B0 + (g + 1) * D_STATE]
            cg = xc_ref[:, _C0 + g * D_STATE:_C0 + (g + 1) * D_STATE]
            dt_g, e_g, dec_g, dk_g = sc["dt_x"][:, gsl], sc["e_x"][:, gsl], sc["dec_x"][:, gsl], sc["dk_x"][:, gsl]
            dsk_g = dsk_ref[:, gsl]
            xdt_g = xs_g * dt_g
            xdt_stack = _stack_heads(xdt_g, masks)
            cbm = _dot(cg, bg, _NT)
            cbt = _dot(bg, cg, _NT)
            heads = range(g * HEADS_PER_GROUP, (g + 1) * HEADS_PER_GROUP)
            lmats = [jnp.exp(jnp.where(tril, sc["csb"][h] - sc["cs_t"][h:h + 1, :], -1e30)) for h in heads]
            mw = jnp.concatenate([cbm * lm for lm in lmats], axis=1)
            mtw = jnp.concatenate(
                [cbt * jnp.exp(jnp.where(triu, sc["cs_t"][h:h + 1, :] - sc["csb"][h], -1e30)) for h in heads], axis=1)
            ht_g = hp_ref[0, :, gsl]
            dhn_g = dh_ref[:, gsl]
            yoff = e_g * _dot(cg, ht_g)
            y_g = _dot(mw, xdt_stack) + yoff + dsk_g * xs_g
            zg = z_ref[:, gsl]
            sz = _sigmoid(zg)
            silu = zg * sz
            yg = y_g * silu
            rs = lax.rsqrt(jnp.mean(yg * yg, axis=1, keepdims=True) + NORM_EPS)
            yn = yg * rs
            dyb = dyb_ref[:, gsl]
            dgs_ref[:, gsl] += jnp.sum(dyb * yn, axis=0, keepdims=True)
            dyn = dyb * gs_ref[:, gsl]
            dyg = rs * (dyn - yn * jnp.mean(dyn * yn, axis=1, keepdims=True))
            dy_g = dyg * silu
            dz_ref[:, gsl] = (dyg * y_g * (sz * (1.0 + zg * (1.0 - sz)))).astype(bf16)
            dy_stack = _stack_heads(dy_g, masks)
            dm_w = _dot(dy_g, xdt_stack, _NT)
            dmt_w = _dot(xdt_g, dy_stack, _NT)
            dxdt = _dot(mtw, dy_stack)
            dcb_acc = jnp.zeros((CHUNK, CHUNK), f32)
            for r, h in enumerate(heads):
                hs = slice(r * CHUNK, (r + 1) * CHUNK)
                dml = dm_w[:, hs] * lmats[r]
                dcb_acc = dcb_acc + dml
                col = jnp.sum(dml * cbm, axis=0, keepdims=True)
                row = jnp.sum(dmt_w[:, hs] * mtw[:, hs], axis=0, keepdims=True)
                dcs_t = dcs_t + jnp.where(rowh == h, row - col, 0.0)
            w = _dot(bg, dhn_g)
            dxdt = dxdt + dec_g * w
            decx3 = dec_g * (xdt_g * w)
            dg_g = e_g * dy_g
            d_c = _dot(dg_g, ht_g, _NT) + _dot(dcb_acc, bg)
            d_b = _dot(dcb_acc, cg, _TN) + _dot(xdt_g * dec_g, dhn_g, _NT)
            dh_ref[:, gsl] = dhn_g * dk_g + _dot(cg, dg_g, _TN)
            dxc_ref[:, gsl] = dsk_g * dy_g + dxdt * dt_g
            dxc_ref[:, _B0 + g * D_STATE:_B0 + (g + 1) * D_STATE] = d_b
            dxc_ref[:, _C0 + g * D_STATE:_C0 + (g + 1) * D_STATE] = d_c
            x13_ref[:, gsl] = dy_g * yoff - decx3
            x2_ref[:, gsl] = dxdt * xs_g
            rows_ref[0:1, gsl] = jnp.sum(dhn_g * ht_g, axis=0, keepdims=True)
            rows_ref[1:2, gsl] = jnp.sum(decx3, axis=0, keepdims=True)
            rows_ref[2:3, gsl] = jnp.sum(dy_g * xs_g, axis=0, keepdims=True)
        segm = seg_ref[...]
        r13 = _seg_sum(x13_ref[...], segm)
        r2 = _seg_sum(x2_ref[...], segm)
        small = _seg_sum(rows_ref[...], segm)
        lane = _iota((CHUNK, LANES), 1)
        rowi = _iota((CHUNK, LANES), 0)
        dcl_row = small[0:1, :] * jnp.exp(sc["cs"][CHUNK - 1:CHUNK, :]) + small[1:2, :]
        dcs = r13 + jnp.where(rowi == CHUNK - 1, dcl_row, 0.0)
        dcs_t_all = dcs.T + jnp.concatenate([dcs_t, jnp.zeros((LANES - N_HEADS, CHUNK), f32)], axis=0)
        dda = _dot32(dcs_t_all, tril.astype(f32)).T
        a = sc["a"]
        ddt_total = r2 + dda * a
        dalog_ref[...] += jnp.sum(dda * sc["dtv"], axis=0, keepdims=True) * a
        ddtr = jnp.where(lane < N_HEADS, ddt_total * _sigmoid(sc["xdt_pre"]), 0.0)
        ddtb_ref[...] += jnp.sum(ddtr, axis=0, keepdims=True)
        ddt_ref[...] = ddtr.astype(bf16)
        ddsk_ref[...] += small[2:3, :]
        row8 = _iota((_TAIL, _CONV_COLS), 0)
        for j in range(CONV_DIM // _CONV_COLS):
            sl = slice(j * _CONV_COLS, (j + 1) * _CONV_COLS)
            cvv = cv_ref[:, sl]
            sg = _sigmoid(cvv)
            dconv = dxc_ref[:, sl] * (sg * (1.0 + cvv * (1.0 - sg)))
            nxt = dcnext_ref[:, sl]
            cur = cur_ref[:, sl]
            dxin = dconv * cw_ref[CONV_W - 1:CONV_W, sl]
            dcw_ref[CONV_W - 1:CONV_W, sl] += jnp.sum(dconv * cur, axis=0, keepdims=True)
            for s in range(1, CONV_W):
                rolled = pltpu.roll(dconv, CHUNK - s, 0)
                bot = jnp.where(row8 < _TAIL - s, rolled[CHUNK - _TAIL:], pltpu.roll(nxt, _TAIL - s, 0))
                up = jnp.concatenate([rolled[:CHUNK - _TAIL], bot], axis=0)
                dxin = dxin + up * cw_ref[CONV_W - 1 - s:CONV_W - s, sl]
                dcw_ref[CONV_W - 1 - s:CONV_W - s, sl] += jnp.sum(up * cur, axis=0, keepdims=True)
            dcb_ref[:, sl] += jnp.sum(dconv, axis=0, keepdims=True)
            dxbc_ref[:, sl] = dxin.astype(bf16)
            dcnext_ref[:, sl] = dconv[:_TAIL]
        if ride is not None:
            pl.when(i == nc - 1)(finish)

    def chunk(w):
        return pl.BlockSpec((CHUNK, w), lambda i: (nc - 1 - i, 0))

    def const(shape):
        return pl.BlockSpec(shape, lambda i: (0,) * len(shape))

    riding = ride is not None
    return pl.pallas_call(
        body, name=name, grid=(nc,),
        in_specs=[chunk(CONV_DIM), chunk(CONV_DIM),
                  chunk(D_INNER), chunk(LANES), pl.BlockSpec((1, D_STATE, D_INNER), lambda i: (nc - 1 - i, 0, 0)),
                  chunk(D_INNER), const((CONV_W, CONV_DIM)),
                  const((1, LANES)), const((1, LANES)), const((1, D_INNER)), const((1, D_INNER)),
                  const((D_INNER, LANES))] + [_ANY] * riding,
        out_specs=[chunk(D_INNER), chunk(CONV_DIM), chunk(LANES), const((CONV_W, CONV_DIM)), const((1, CONV_DIM)),
                   const((1, LANES)), const((1, LANES)), const((1, LANES)), const((1, D_INNER))] + [_ANY] * riding,
        out_shape=[jax.ShapeDtypeStruct((t, D_INNER), bf16), jax.ShapeDtypeStruct((t, CONV_DIM), bf16),
                   jax.ShapeDtypeStruct((t, LANES), bf16), jax.ShapeDtypeStruct((CONV_W, CONV_DIM), f32),
                   jax.ShapeDtypeStruct((1, CONV_DIM), f32), jax.ShapeDtypeStruct((1, LANES), f32),
                   jax.ShapeDtypeStruct((1, LANES), f32), jax.ShapeDtypeStruct((1, LANES), f32),
                   jax.ShapeDtypeStruct((1, D_INNER), f32)]
        + ([jax.ShapeDtypeStruct((N_CHIPS - 1,) + ride.shape[1:], ride.dtype)] if riding else []),
        scratch_shapes=[pltpu.VMEM((D_STATE, D_INNER), f32), pltpu.VMEM((_TAIL, CONV_DIM), f32),
                        pltpu.VMEM((CHUNK, CONV_DIM), f32), pltpu.VMEM((CHUNK, CONV_DIM), f32),
                        pltpu.VMEM((CHUNK, D_INNER), f32), pltpu.VMEM((CHUNK, D_INNER), f32),
                        pltpu.VMEM((_TAIL, D_INNER), f32)]
        + (list(_SCATTER_SCRATCH) if riding else []),
        compiler_params=_params(("arbitrary",)),
    )(xbc, cv, z, dtr, hprev, dyb, cw, dtb, alog, dsk_x, gs, seg, *([ride] if riding else []))


def _adamw(w, g, m, v, *, name):
    r, c = w.shape
    tr = r
    while tr * c * 4 > _MB and tr % 16 == 0:
        tr //= 2

    def body(w_ref, g_ref, m_ref, v_ref, d_ref, m2_ref, v2_ref):
        gv = g_ref[...]
        m2 = ADAM_B1 * m_ref[...] + (1.0 - ADAM_B1) * gv
        v2 = ADAM_B2 * v_ref[...] + (1.0 - ADAM_B2) * (gv * gv)
        m_hat = m2 / (1.0 - ADAM_B1 ** ADAM_STEP)
        v_hat = v2 / (1.0 - ADAM_B2 ** ADAM_STEP)
        d_ref[...] = -ADAM_LR * (m_hat / (jnp.sqrt(v_hat) + ADAM_EPS) + ADAM_WD * w_ref[...])
        m2_ref[...] = m2
        v2_ref[...] = v2

    blk = pl.BlockSpec((tr, c), lambda i: (i, 0))
    return pl.pallas_call(
        body, name=name, grid=(r // tr,),
        in_specs=[blk] * 4, out_specs=[blk] * 3,
        out_shape=[jax.ShapeDtypeStruct((r, c), f32)] * 3,
        compiler_params=_params(("parallel",)),
    )(w, g, m, v)


def _row_block(rows, cols):
    cap = max(16, _MB // (4 * cols))
    return max(tr for tr in range(16, min(cap, rows) + 1, 16) if rows % tr == 0)


def _cast_bf16(a, *, name):
    r, c = a.shape
    tr = _row_block(r, c)

    def body(a_ref, o_ref):
        o_ref[...] = a_ref[...].astype(bf16)

    blk = pl.BlockSpec((tr, c), lambda i: (i, 0))
    return pl.pallas_call(
        body, name=name, grid=(r // tr,), in_specs=[blk], out_specs=blk,
        out_shape=jax.ShapeDtypeStruct((r, c), bf16), compiler_params=_params(("parallel",)),
    )(a)


_ANY = pl.BlockSpec(memory_space=pl.ANY)


def _place():
    x, y, c = lax.axis_index("x"), lax.axis_index("y"), lax.axis_index("c")
    other_chips = [(1 - x, y), (x, 1 - y), (1 - x, 1 - y)]
    return x, y, c, other_chips


def _gather_protocol(in_ref, out_ref, send_sems, recv_sems):
    x, y, c, chips = _place()
    me = 2 * x + y
    sibling = (x, y, 1 - c)
    where = [2 * cx + cy for cx, cy in chips]

    def cp(k, chip, half, to, src=None):
        dst = out_ref.at[chip, half]
        return pltpu.make_async_remote_copy(
            src_ref=dst if src is None else src, dst_ref=dst, send_sem=send_sems.at[k], recv_sem=recv_sems.at[k],
            device_id=to, device_id_type=MESH)

    def sends():
        return [cp(j, me, c, (*chips[j], c), src=in_ref.at[c]) for j in range(2)]

    def relays():
        return [cp(3 + j, where[j], c, sibling) for j in range(3)]

    def landed(j):
        return cp(j, where[j], c, sibling)

    def start():
        for f in sends():
            f.start()

    def relay():
        onward = relays()
        for first in range(2):
            @pl.when(c == first)
            def _(first=first):
                landed(first).wait_recv()
                cp(2, where[first], c, (*chips[1 - first], c)).start()
                onward[first].start()
                landed(1 - first).wait_recv()
                onward[1 - first].start()

    def finish():
        landed(2).wait_recv()
        relays()[2].start()
        for j in range(3):
            cp(3 + j, where[j], 1 - c, sibling).wait_recv()
        for f in sends() + [landed(2)] + relays():
            f.wait_send()

    return start, relay, finish


_GATHER_SCRATCH = [pltpu.SemaphoreType.DMA((6,)), pltpu.SemaphoreType.DMA((6,))]


def _gather_shards(shard, *, name):
    _, rh, lanes = shard.shape

    def body(in_ref, out_ref, send_sems, recv_sems):
        start, relay, finish = _gather_protocol(in_ref, out_ref, send_sems, recv_sems)
        start()
        relay()
        finish()

    return pl.pallas_call(
        body, name=name, in_specs=[_ANY], out_specs=_ANY,
        out_shape=jax.ShapeDtypeStruct((N_CHIPS, 2, rh, lanes), shard.dtype),
        scratch_shapes=list(_GATHER_SCRATCH),
    )(shard)


def _scatter_protocol(p_ref, out_ref, send_sems, recv_sems):
    x, y, c, chips = _place()

    def copies():
        return [pltpu.make_async_remote_copy(
            src_ref=p_ref.at[2 * cx + cy], dst_ref=out_ref.at[j], send_sem=send_sems.at[j], recv_sem=recv_sems.at[j],
            device_id=(cx, cy, c), device_id_type=MESH) for j, (cx, cy) in enumerate(chips)]

    def start():
        for cpy in copies():
            cpy.start()

    def finish():
        for cpy in copies():
            cpy.wait()

    return start, finish


_SCATTER_SCRATCH = [pltpu.SemaphoreType.DMA((3,)), pltpu.SemaphoreType.DMA((3,))]


def _swap_protocol(g_ref, out_ref, send_sems, recv_sems):
    x, y, c, _ = _place()

    def copies():
        return [pltpu.make_async_remote_copy(
            src_ref=g_ref.at[k, 1 - c], dst_ref=out_ref.at[k], send_sem=send_sems.at[k], recv_sem=recv_sems.at[k],
            device_id=(x, y, 1 - c), device_id_type=MESH) for k in range(N_CHIPS)]

    def start():
        for cpy in copies():
            cpy.start()

    def finish():
        for cpy in copies():
            cpy.wait()

    return start, finish


_SWAP_SCRATCH = [pltpu.SemaphoreType.DMA((N_CHIPS,)), pltpu.SemaphoreType.DMA((N_CHIPS,))]


def _rs_swap_halves(g, *, name):
    nch, _, rh, lanes = g.shape

    def body(g_ref, out_ref, send_sems, recv_sems):
        start, finish = _swap_protocol(g_ref, out_ref, send_sems, recv_sems)
        start()
        finish()

    return pl.pallas_call(
        body, name=name, in_specs=[_ANY], out_specs=_ANY,
        out_shape=jax.ShapeDtypeStruct((nch, rh, lanes), g.dtype),
        scratch_shapes=list(_SWAP_SCRATCH),
    )(g)


def _rs_add_pair(g, got, c_idx, *, name):
    nch, _, rh, lanes = g.shape
    tr = _row_block(rh, lanes)

    def body(c_ref, g_ref, got_ref, p16_ref):
        p16_ref[...] = (g_ref[...] + got_ref[...]).astype(bf16)

    blk = pl.BlockSpec((None, tr, lanes), lambda k, i, c_ref: (k, i, 0))
    return pl.pallas_call(
        body, name=name,
        grid_spec=pltpu.PrefetchScalarGridSpec(
            num_scalar_prefetch=1, grid=(nch, rh // tr),
            in_specs=[pl.BlockSpec((None, None, tr, lanes), lambda k, i, c_ref: (k, c_ref[0], i, 0)), blk],
            out_specs=blk),
        out_shape=jax.ShapeDtypeStruct((nch, rh, lanes), bf16),
        compiler_params=_params(("parallel", "parallel")),
    )(c_idx, g, got)


def _rs_add_chips(g, got_pair, got, place, *, name):
    _, _, rh, lanes = g.shape
    tr = _row_block(rh, lanes)

    def body(place_ref, g_ref, pair_ref, got_ref, o_ref):
        own = g_ref[...] + pair_ref[...]
        o_ref[...] = ((own + got_ref[0].astype(f32)) + got_ref[1].astype(f32)) + got_ref[2].astype(f32)

    return pl.pallas_call(
        body, name=name,
        grid_spec=pltpu.PrefetchScalarGridSpec(
            num_scalar_prefetch=1, grid=(rh // tr,),
            in_specs=[pl.BlockSpec((None, None, tr, lanes), lambda i, place_ref: (place_ref[0], place_ref[1], i, 0)),
                      pl.BlockSpec((None, tr, lanes), lambda i, place_ref: (place_ref[0], i, 0)),
                      pl.BlockSpec((3, tr, lanes), lambda i, place_ref: (0, i, 0))],
            out_specs=pl.BlockSpec((None, tr, lanes), lambda i, place_ref: (place_ref[1], i, 0))),
        out_shape=jax.ShapeDtypeStruct((2, rh, lanes), f32),
        compiler_params=_params(("parallel",)),
    )(place, g, got_pair, got)


def _rs_join_halves(halves, *, name):
    def body(h_ref, out_ref, send_sem, recv_sem):
        x, y, c, _ = _place()
        cpy = pltpu.make_async_remote_copy(
            src_ref=h_ref.at[c], dst_ref=out_ref.at[c], send_sem=send_sem, recv_sem=recv_sem,
            device_id=(x, y, 1 - c), device_id_type=MESH)
        cpy.start()
        cpy.wait()

    return pl.pallas_call(
        body, name=name, in_specs=[_ANY], out_specs=_ANY,
        out_shape=jax.ShapeDtypeStruct(halves.shape, halves.dtype), input_output_aliases={0: 0},
        scratch_shapes=[pltpu.SemaphoreType.DMA, pltpu.SemaphoreType.DMA],
    )(halves)


def _all_reduce_small(s, *, name):
    rs, lanes = s.shape
    rh = rs // 2

    def body(s_ref, o_ref, sib_ref, mine_ref, chips_ref, send_sems, recv_sems):
        x, y, c, chips = _place()
        me = 2 * x + y
        sibling = (x, y, 1 - c)
        rows = pl.ds(pl.multiple_of(c * rh, 8), rh)

        def cp(k, src, dst, to):
            return pltpu.make_async_remote_copy(src_ref=src, dst_ref=dst, send_sem=send_sems.at[k],
                                                recv_sem=recv_sems.at[k], device_id=to, device_id_type=MESH)

        swap = cp(0, s_ref, sib_ref, sibling)
        swap.start()
        swap.wait()
        mine_ref[...] = s_ref[rows, :] + sib_ref[rows, :]
        sends = [cp(1 + j, mine_ref, chips_ref.at[j], (cx, cy, c)) for j, (cx, cy) in enumerate(chips)]
        for cpy in sends:
            cpy.start()
        for cpy in sends:
            cpy.wait()
        where = [2 * cx + cy for cx, cy in chips]
        total = None
        for q in range(N_CHIPS):
            term = jnp.where(q == me, mine_ref[...], jnp.where(
                q == where[0], chips_ref[0], jnp.where(q == where[1], chips_ref[1], chips_ref[2])))
            total = term if total is None else total + term
        o_ref[rows, :] = total
        push = cp(4, o_ref.at[rows, :], o_ref.at[rows, :], sibling)
        push.start()
        push.wait()

    vm = pl.BlockSpec(memory_space=pltpu.VMEM)
    return pl.pallas_call(
        body, name=name, in_specs=[vm], out_specs=vm,
        out_shape=jax.ShapeDtypeStruct((rs, lanes), f32),
        scratch_shapes=[pltpu.VMEM((rs, lanes), f32), pltpu.VMEM((rh, lanes), f32),
                        pltpu.VMEM((N_CHIPS - 1, rh, lanes), f32), pltpu.SemaphoreType.DMA((5,)),
                        pltpu.SemaphoreType.DMA((5,))],
        compiler_params=pltpu.CompilerParams(vmem_limit_bytes=32 * _MB),
    )(s)


def _pad_lanes(a, width=LANES):
    return jnp.pad(a, ((0, 0), (0, width - a.shape[1])))


def _local_grads(x, tgt, wts, small, *, fwd_ride=None, late_weights=None, swap_ride=None, bwd_ride=None,
                 last_ride=None):
    t = x.shape[0]
    tm = min(t, 1024)
    d = D_MODEL
    mm = functools.partial(_matmul, tm=tm)

    dtb = _pad_lanes(small["dt_bias"])
    alog = _pad_lanes(small["a_log"])
    dsk = jnp.repeat(small["d_skip"], HEAD_DIM, axis=1)
    bsp_t = _pad_lanes(small["b_spatial"].T)
    wsp = small["w_spatial"]

    h = _rms_fwd(x, small["norm_mix_g"], name="rms_mix")
    uv = mm(h, wts["uv"], tn=1024, tk=d, out_dtypes=[f32], name="proj_uv")
    z = mm(h, wts["z"], tn=1024, tk=d, out_dtypes=[f32], name="proj_z")
    xbc = mm(h, wts["xbc"], tn=1024, tk=d, out_dtypes=[f32], name="proj_xbc")
    dtr = mm(h, wts["dt"], tn=LANES, tk=d, out_dtypes=[f32], name="proj_dt")
    gl = mm(h, wts["gate"], tn=1024, tk=d, out_dtypes=[f32], name="proj_gate")
    ya = _gmlp_fwd(uv, small["v_norm_g"], small["v_norm_b"], wsp, bsp_t, name="gmlp_fwd")
    yb, hprev, cv, *gathered = _ssd_fwd(xbc, z, dtr, small["conv_w"], small["conv_b"], dtb, alog, dsk,
                                        small["ssm_norm_g"], ride=fwd_ride, name="ssd_fwd")
    if fwd_ride is not None:
        wts = {**wts, **late_weights(gathered[0])}
    pa = mm(ya, wts["pa"], tn=1024, tk=1024, out_dtypes=[f32], name="proj_a")
    tm_gate = min(t, 512)
    row_vec = [pl.BlockSpec((1, d), lambda i, j, k, half=half: (0, half)) for half in range(2)]
    gate_tiles = [pl.BlockSpec((tm_gate, d), lambda i, j, k, half=half: (i, half)) for half in range(2)]

    def merge(pb_acc, pa_t, gla, glb, bga, bgb):
        return pb_acc, _sigmoid(gla + bga) * pa_t + _sigmoid(glb + bgb) * pb_acc

    pb, merged = _matmul(yb, wts["pb"], tm=tm_gate, tn=d, tk=1024, out_dtypes=[f32, bf16], epilogue=merge,
                         extras=[pa, gl, gl, small["b_gates"], small["b_gates"]],
                         extra_specs=[None] + gate_tiles + row_vec, name="proj_b")

    def residual_norm(acc, res, g):
        x_new = res + acc
        r = lax.rsqrt(jnp.mean(x_new * x_new, axis=1, keepdims=True) + NORM_EPS)
        return x_new, x_new * r * g

    x1, h2 = mm(merged, wts["out"], tn=d, tk=1024, out_dtypes=[f32, bf16], epilogue=residual_norm,
                extras=[x, small["norm_mlp_g"]], extra_specs=[None, row_vec[0]], name="out_proj")
    act = mm(h2, wts["up"], tn=1024, tk=d, out_dtypes=[bf16],
             epilogue=lambda acc: (jnp.square(jnp.maximum(acc, 0.0)),), name="mlp_up")
    x2 = mm(act, wts["down"], tn=1024, tk=2048, out_dtypes=[f32], extras=[x1],
            epilogue=lambda acc, res: (res + acc,), name="mlp_down")

    dx2, dx2b, dgf, loss = _loss_head(x2, tgt, small["norm_final_g"], name="loss_head")
    tt = min(t, 2048)
    tn_mm = functools.partial(_matmul_tn, tt=tt)
    dw = {}
    dw["down"] = tn_mm(act, dx2b, tka=1024, tn=1024, name="dw_down")
    dup = mm(dx2b, wts["down"], nt=True, tn=1024, tk=1024, out_dtypes=[bf16], extras=[act],
             epilogue=lambda acc, a2: (acc * (2.0 * jnp.sqrt(a2).astype(f32)),), name="d_act")
    dw["up"] = tn_mm(h2, dup, tka=1024, tn=1024, name="dw_up")
    dh2 = mm(dup, wts["up"], nt=True, tn=1024, tk=2048, out_dtypes=[f32], name="d_h2")
    dx1, dx1b, dg_mlp = _rms_bwd(x1, small["norm_mlp_g"], dh2, dx2, want_bf16=True, name="rms_mlp_bwd")
    dw["out"] = tn_mm(merged, dx1b, tka=1024, tn=1024, name="dw_out")
    dmerged = mm(dx1b, wts["out"], nt=True, tn=1024, tk=1024, out_dtypes=[f32], name="d_merged")
    dpa, dpb, dgl, dbg = _merge_bwd(dmerged, pa, pb, gl, small["b_gates"], name="merge_bwd")
    dw["pa"] = tn_mm(ya, dpa, tka=1024, tn=1024, name="dw_pa")
    dw["pb"] = tn_mm(yb, dpb, tka=1024, tn=1024, name="dw_pb")
    dya = mm(dpa, wts["pa"], nt=True, tn=1024, tk=1024, out_dtypes=[f32], name="d_ya")
    dyb = mm(dpb, wts["pb"], nt=True, tn=1024, tk=1024, out_dtypes=[f32], name="d_yb")
    swapped = swap_ride(dw) if swap_ride is not None else None
    duv, dwsp, dbsp_t, dvg, dvb, *got_pair = _gmlp_bwd(uv, dya, small["v_norm_g"], small["v_norm_b"], wsp, bsp_t,
                                                       ride=swapped, name="gmlp_bwd")
    ride = bwd_ride(swapped, got_pair[0]) if bwd_ride is not None else None
    dz, dxbc, ddt, dcw, dcb, ddtb, dalog, ddsk, dgs, *got = _ssd_bwd(
        xbc, cv, z, dtr, hprev, dyb, small["conv_w"], dtb, alog, dsk, small["ssm_norm_g"],
        _head_seg_matrix(), ride=ride, name="ssd_bwd")
    dw["uv"] = tn_mm(h, duv, tka=1024, tn=1024, name="dw_uv")
    dw["z"] = tn_mm(h, dz, tka=1024, tn=1024, name="dw_z")
    dw["xbc"] = tn_mm(h, dxbc, tka=1024, tn=1024, name="dw_xbc")
    dw["dt"] = tn_mm(h, ddt, tka=1024, tn=LANES, name="dw_dt")
    dw["gate"] = tn_mm(h, dgl, tka=1024, tn=1024, name="dw_gate")
    last = last_ride(dw) if last_ride is not None else None
    res = _matmul_nt_sum(
        [(duv, wts["uv"]), (dz, wts["z"]), (dxbc, wts["xbc"]), (dgl, wts["gate"]), (ddt, wts["dt"])],
        tm=tm, tks=[1024] * 4 + [LANES], ride=last, name="d_h")
    dh, got_last = (res[0], res[1]) if last is not None else (res, None)
    dx, dg_mix = _rms_bwd(x, small["norm_mix_g"], dh, dx1, want_bf16=False, name="rms_mix_bwd")

    dsmall = {
        "norm_mix_g": dg_mix, "conv_w": dcw, "conv_b": dcb, "dt_bias": ddtb[:, :N_HEADS], "a_log": dalog[:, :N_HEADS],
        "d_skip": ddsk[:, :N_HEADS], "ssm_norm_g": dgs, "v_norm_g": dvg, "v_norm_b": dvb, "w_spatial": dwsp,
        "b_spatial": dbsp_t[:, :GMLP_GROUPS].T, "b_gates": dbg, "norm_mlp_g": dg_mlp, "norm_final_g": dgf,
    }
    return loss, dx, dw, dsmall, (got[0] if got else None), got_last


_IN_SHARD = IN_PROJ // N_CHIPS
_LATE = ("w_proj_a", "w_proj_b", "w_out", "w_mlp_up", "w_mlp_down")
_LATE_ROWS = {"w_proj_a": GMLP_WIDTH // N_CHIPS, "w_proj_b": D_INNER // N_CHIPS, "w_out": D_MODEL // N_CHIPS,
              "w_mlp_up": D_MODEL, "w_mlp_down": D_FF // N_CHIPS}
_LATE_TOTAL = sum(_LATE_ROWS.values())


def _late_offsets():
    off, out = 0, {}
    for k in _LATE:
        out[k] = off
        off += _LATE_ROWS[k]
    return out


_LATE_OFF = _late_offsets()

_SMALL = ("norm_mix_g", "conv_w", "conv_b", "dt_bias", "a_log", "d_skip", "ssm_norm_g", "v_norm_g", "v_norm_b",
          "w_spatial", "b_spatial", "b_gates", "norm_mlp_g", "norm_final_g")


def _pack_small(parts):
    flat = jnp.concatenate([parts[k].reshape(-1) for k in _SMALL])
    rows = -(-flat.shape[0] // (16 * LANES)) * 16
    return jnp.pad(flat, (0, rows * LANES - flat.shape[0])).reshape(rows, LANES)


def _unpack_small(packed, shapes):
    flat = packed.reshape(-1)
    out, off = {}, 0
    for k in _SMALL:
        n = math.prod(shapes[k])
        out[k] = flat[off:off + n].reshape(shapes[k])
        off += n
    return out


def _from_chip_columns(stacked):
    _, rows, cols = stacked.shape
    return stacked.transpose(1, 0, 2).reshape(rows, N_CHIPS * cols)


def _to_chip_columns(full):
    rows, cols = full.shape
    return full.reshape(rows, N_CHIPS, cols // N_CHIPS).transpose(1, 0, 2)


def _w_in_grad_by_chip(dw):
    pieces = [dw["uv"], dw["z"], dw["xbc"], dw["dt"][:, :N_HEADS], dw["gate"]]
    bounds = [0]
    for p in pieces:
        bounds.append(bounds[-1] + p.shape[1])
    chips = []
    for k in range(N_CHIPS):
        lo, hi = k * _IN_SHARD, (k + 1) * _IN_SHARD
        parts = [p[:, max(lo, b0) - b0:min(hi, b1) - b0]
                 for p, b0, b1 in zip(pieces, bounds[:-1], bounds[1:]) if min(hi, b1) > max(lo, b0)]
        chips.append(jnp.concatenate(parts, axis=1))
    return jnp.stack(chips)


def kernel(x, norm_mix_g, w_in, conv_w, conv_b, dt_bias, a_log, d_skip, ssm_norm_g, v_norm_g, v_norm_b, w_spatial, b_spatial, b_gates, w_proj_a, w_proj_b, w_out, norm_mlp_g, w_mlp_up, w_mlp_down, norm_final_g, loss_target, m_norm_mix_g, m_w_in, m_conv_w, m_conv_b, m_dt_bias, m_a_log, m_d_skip, m_ssm_norm_g, m_v_norm_g, m_v_norm_b, m_w_spatial, m_b_spatial, m_b_gates, m_w_proj_a, m_w_proj_b, m_w_out, m_norm_mlp_g, m_w_mlp_up, m_w_mlp_down, m_norm_final_g, v_norm_mix_g, v_w_in, v_conv_w, v_conv_b, v_dt_bias, v_a_log, v_d_skip, v_ssm_norm_g, v_v_norm_g, v_v_norm_b, v_w_spatial, v_b_spatial, v_b_gates, v_w_proj_a, v_w_proj_b, v_w_out, v_norm_mlp_g, v_w_mlp_up, v_w_mlp_down, v_norm_final_g):
    given = dict(locals())
    names = ("norm_mix_g", "w_in", "conv_w", "conv_b", "dt_bias", "a_log", "d_skip", "ssm_norm_g", "v_norm_g",
             "v_norm_b", "w_spatial", "b_spatial", "b_gates", "w_proj_a", "w_proj_b", "w_out", "norm_mlp_g",
             "w_mlp_up", "w_mlp_down", "norm_final_g")
    xi, yi, ci = lax.axis_index("x"), lax.axis_index("y"), lax.axis_index("c")
    me_chip = (2 * xi + yi).astype(jnp.int32)

    def halves(a):
        return a.reshape(2, a.shape[0] // 2, a.shape[1])

    def with_own(got, shard):
        whole = lax.dynamic_update_slice(got, shard[None], (me_chip, 0, 0, 0))
        return whole.reshape(N_CHIPS, 2 * shard.shape[1], shard.shape[2])

    shard_in = halves(_cast_bf16(w_in[0], name="cast_w_in"))
    shard_late = halves(_cast_bf16(jnp.concatenate([given[k][0] for k in _LATE]), name="cast_w_late"))
    shard_conv = halves(conv_w.reshape(2 * _TAIL, -1))
    w_in_full = _from_chip_columns(with_own(_gather_shards(shard_in, name="gather_w_in"), shard_in))
    o_dt, o_gate = 2 * GMLP_WIDTH + D_INNER + CONV_DIM, 2 * GMLP_WIDTH + D_INNER + CONV_DIM + N_HEADS
    wts = {
        "uv": w_in_full[:, :2 * GMLP_WIDTH], "z": w_in_full[:, 2 * GMLP_WIDTH:2 * GMLP_WIDTH + D_INNER],
        "xbc": w_in_full[:, 2 * GMLP_WIDTH + D_INNER:o_dt], "dt": _pad_lanes(w_in_full[:, o_dt:o_gate]),
        "gate": w_in_full[:, o_gate:],
    }
    conv_all = with_own(_gather_shards(shard_conv, name="gather_conv_w"), shard_conv)
    conv_full = _from_chip_columns(conv_all.reshape(N_CHIPS, CONV_W, CONV_DIM // N_CHIPS))

    def late_weights(got):
        g_late = with_own(got, shard_late)

        def rows_of(k):
            return g_late[:, _LATE_OFF[k]:_LATE_OFF[k] + _LATE_ROWS[k]]

        return {
            "pa": rows_of("w_proj_a").reshape(GMLP_WIDTH, D_MODEL),
            "pb": rows_of("w_proj_b").reshape(D_INNER, D_MODEL), "out": rows_of("w_out").reshape(D_MODEL, D_MODEL),
            "up": _from_chip_columns(rows_of("w_mlp_up")), "down": rows_of("w_mlp_down").reshape(D_FF, D_MODEL),
        }

    small = {
        "norm_mix_g": norm_mix_g, "conv_w": conv_full, "conv_b": conv_b, "dt_bias": dt_bias, "a_log": a_log,
        "d_skip": d_skip, "ssm_norm_g": ssm_norm_g, "v_norm_g": v_norm_g, "v_norm_b": v_norm_b,
        "w_spatial": w_spatial[0], "b_spatial": b_spatial[0], "b_gates": b_gates, "norm_mlp_g": norm_mlp_g,
        "norm_final_g": norm_final_g.reshape(1, D_MODEL),
    }

    c_idx = ci.astype(jnp.int32).reshape(1)
    place = jnp.stack([me_chip, ci.astype(jnp.int32)])
    partials = {}

    def reduced_shard(tag, got_chips):
        own = _rs_add_chips(*partials[tag], got_chips, place, name="rs_add_chips_" + tag)
        both = _rs_join_halves(own, name="rs_join_" + tag)
        return both.reshape(2 * both.shape[1], both.shape[2])

    def late_grads(dw):
        def by_rows(a):
            return a.reshape(N_CHIPS, a.shape[0] // N_CHIPS, a.shape[1])

        g = jnp.concatenate([by_rows(dw["pa"]), by_rows(dw["pb"]), by_rows(dw["out"]), _to_chip_columns(dw["up"]),
                             by_rows(dw["down"])], axis=1)
        return g.reshape(N_CHIPS, 2, g.shape[1] // 2, g.shape[2])

    def late_partials(g, got_pair):
        partials["late"] = (g, got_pair)
        return _rs_add_pair(g, got_pair, c_idx, name="rs_add_pair_late")

    def in_partials(dw):
        g = _w_in_grad_by_chip(dw).reshape(N_CHIPS, 2, D_MODEL // 2, _IN_SHARD)
        got_pair = _rs_swap_halves(g, name="rs_swap_in")
        partials["in"] = (g, got_pair)
        return _rs_add_pair(g, got_pair, c_idx, name="rs_add_pair_in")

    loss_part, grad_x, dw, dsmall, got_late, got_in = _local_grads(
        x[0], loss_target[0], wts, small, fwd_ride=shard_late, late_weights=late_weights, swap_ride=late_grads,
        bwd_ride=late_partials, last_ride=in_partials)
    loss = lax.psum(loss_part[0, 0], ("x", "y", "c"))
    g_late = reduced_shard("late", got_late)
    g_in_shard = reduced_shard("in", got_in)

    small_shapes = {k: dsmall[k].shape for k in _SMALL}
    red = _unpack_small(_all_reduce_small(_pack_small(dsmall), name="all_reduce_small"), small_shapes)
    conv_cols = CONV_DIM // N_CHIPS
    red["conv_w"] = lax.dynamic_slice_in_dim(red["conv_w"], me_chip * conv_cols, conv_cols, axis=1)

    grads, deltas, new_m, new_v = {}, {}, {}, {}
    for k in ("w_in",) + _LATE:
        g2 = g_in_shard if k == "w_in" else g_late[_LATE_OFF[k]:_LATE_OFF[k] + _LATE_ROWS[k]]
        dlt, m2, v2 = _adamw(given[k][0], g2, given["m_" + k][0], given["v_" + k][0], name="adamw_" + k)
        grads[k], deltas[k], new_m[k], new_v[k] = g2, dlt, m2, v2
    adam_shapes = dict(small_shapes)
    adam_shapes["conv_w"] = (CONV_W, conv_cols)

    def small_pack_of(prefix):
        return _pack_small({k: given[prefix + k].reshape(adam_shapes[k]) for k in _SMALL})

    dlt_s, m_s, v_s = _adamw(small_pack_of(""), _pack_small(red), small_pack_of("m_"), small_pack_of("v_"),
                             name="adamw_small")
    for dst, packed in ((deltas, dlt_s), (new_m, m_s), (new_v, v_s)):
        dst.update(_unpack_small(packed, adam_shapes))
    grads.update(red)

    def shaped(dct):
        return [dct[k].reshape(given[k].shape) for k in names]

    return (loss, grad_x[None], *shaped(grads), *shaped(deltas), *shaped(new_m), *shaped(new_v))
```

```python
import functools
import math

import jax
import jax.numpy as jnp
from jax import lax
from jax.experimental import pallas as pl
from jax.experimental.pallas import tpu as pltpu

f32 = jnp.float32
bf16 = jnp.bfloat16

D_MODEL = 1024
CHUNK = 128
GMLP_WIDTH = 1024
GMLP_GROUPS = 8
D_INNER = 2048
HEAD_DIM = 64
N_HEADS = 32
N_GROUPS = 8
HEADS_PER_GROUP = 4
GROUP_W = HEADS_PER_GROUP * HEAD_DIM
D_STATE = 128
CONV_W = 4
CONV_DIM = 4096
D_FF = 4096
IN_PROJ = 10272
NORM_EPS = 1e-6
N_CHIPS = 4
N_DEV = 8
LANES = 128

ADAM_LR = 0.001
ADAM_B1 = 0.9
ADAM_B2 = 0.999
ADAM_EPS = 1e-08
ADAM_WD = 0.01
ADAM_STEP = 10

MESH = pl.DeviceIdType.MESH
_NT = (((1,), (1,)), ((), ()))
_NN = (((1,), (0,)), ((), ()))
_TN = (((0,), (0,)), ((), ()))
_MB = 2 ** 20


def _params(sem, vmem_mb=48):
    return pltpu.CompilerParams(dimension_semantics=sem, vmem_limit_bytes=vmem_mb * _MB)


def _dot(a, b, dims=_NN):
    return lax.dot_general(a.astype(bf16), b.astype(bf16), dims, preferred_element_type=f32)


def _dot32(a, b):
    return jnp.dot(a, b, preferred_element_type=f32, precision=lax.Precision.HIGHEST)


def _sigmoid(x):
    return 1.0 / (1.0 + jnp.exp(-x))


def _sum_all(a):
    return jnp.sum(jnp.sum(a, axis=1, keepdims=True), axis=0, keepdims=True)


def _iota(shape, dim):
    return lax.broadcasted_iota(jnp.int32, shape, dim)


def _matmul(a, b, *, nt=False, tm, tn, tk, out_dtypes, epilogue=None, extras=(), extra_specs=None, name):
    m, k_dim = a.shape
    n = b.shape[0] if nt else b.shape[1]
    nk = k_dim // tk
    ne, no = len(extras), len(out_dtypes)
    dims = _NT if nt else _NN

    def body(*refs):
        a_ref, b_ref = refs[0], refs[1]
        ex = refs[2:2 + ne]
        outs = refs[2 + ne:2 + ne + no]

        def finish(acc):
            vals = epilogue(acc, *[e[...] for e in ex]) if epilogue is not None else (acc,)
            for o, v in zip(outs, vals):
                o[...] = v.astype(o.dtype)

        part = lax.dot_general(a_ref[...], b_ref[...], dims, preferred_element_type=f32)
        if nk == 1:
            finish(part)
        else:
            acc_ref = refs[-1]
            kk = pl.program_id(2)

            @pl.when(kk == 0)
            def _():
                acc_ref[...] = part

            @pl.when(kk > 0)
            def _():
                acc_ref[...] += part

            @pl.when(kk == nk - 1)
            def _():
                finish(acc_ref[...])

    b_spec = pl.BlockSpec((tn, tk), lambda i, j, k: (j, k)) if nt else pl.BlockSpec((tk, tn), lambda i, j, k: (k, j))
    tile = pl.BlockSpec((tm, tn), lambda i, j, k: (i, j))
    ex_specs = [tile if s is None else s for s in (extra_specs or [None] * ne)]
    outs = pl.pallas_call(
        body, name=name, grid=(m // tm, n // tn, nk),
        in_specs=[pl.BlockSpec((tm, tk), lambda i, j, k: (i, k)), b_spec] + ex_specs,
        out_specs=[tile] * no,
        out_shape=[jax.ShapeDtypeStruct((m, n), dt) for dt in out_dtypes],
        scratch_shapes=[pltpu.VMEM((tm, tn), f32)] if nk > 1 else [],
        compiler_params=_params(("parallel", "parallel", "arbitrary")),
    )(a, b, *extras)
    return outs if no > 1 else outs[0]


def _matmul_nt_sum(pairs, *, tm, tks, ride=None, name):
    m = pairs[0][0].shape[0]
    n = pairs[0][1].shape[0]
    nblk = [a.shape[1] // tk for (a, _), tk in zip(pairs, tks)]
    starts = [sum(nblk[:p]) for p in range(len(pairs))]
    nk = sum(nblk)
    npairs = len(pairs)
    ni = m // tm
    riding = ride is not None

    def body(*refs):
        rest = refs[2 * npairs:]
        if riding:
            ride_ref, o_ref, got_ref, acc_ref, send_sems, recv_sems = rest
        else:
            o_ref, acc_ref = rest
        i, kk = pl.program_id(0), pl.program_id(1)
        if riding:
            start, finish = _scatter_protocol(ride_ref, got_ref, send_sems, recv_sems)
            pl.when((i == 0) & (kk == 0))(start)

        @pl.when(kk == 0)
        def _():
            acc_ref[...] = jnp.zeros_like(acc_ref)

        for p in range(npairs):
            @pl.when((kk >= starts[p]) & (kk < starts[p] + nblk[p]))
            def _(p=p):
                acc_ref[...] += lax.dot_general(refs[2 * p][...], refs[2 * p + 1][...], _NT, preferred_element_type=f32)

        @pl.when(kk == nk - 1)
        def _():
            o_ref[...] = acc_ref[...]

        if riding:
            pl.when((i == ni - 1) & (kk == nk - 1))(finish)

    in_specs, args = [], []
    for p, (a, b) in enumerate(pairs):
        def kblock(k, s=starts[p], nb=nblk[p]):
            return jnp.clip(k - s, 0, nb - 1)
        in_specs.append(pl.BlockSpec((tm, tks[p]), lambda i, k, kb=kblock: (i, kb(k))))
        in_specs.append(pl.BlockSpec((n, tks[p]), lambda i, k, kb=kblock: (0, kb(k))))
        args += [a, b]
    tile = pl.BlockSpec((tm, n), lambda i, k: (i, 0))
    outs = pl.pallas_call(
        body, name=name, grid=(ni, nk), in_specs=in_specs + [_ANY] * riding, out_specs=[tile] + [_ANY] * riding,
        out_shape=[jax.ShapeDtypeStruct((m, n), f32)]
        + ([jax.ShapeDtypeStruct((N_CHIPS - 1,) + ride.shape[1:], ride.dtype)] if riding else []),
        scratch_shapes=[pltpu.VMEM((tm, n), f32)] + (list(_SCATTER_SCRATCH) if riding else []),
        compiler_params=_params(("arbitrary", "arbitrary"), vmem_mb=56),
    )(*args, *([ride] if riding else []))
    return outs if riding else outs[0]


def _matmul_tn(a, b, *, tka, tn, tt, name):
    t, ka = a.shape
    n = b.shape[1]

    def body(a_ref, b_ref, o_ref):
        part = lax.dot_general(a_ref[...], b_ref[...], _TN, preferred_element_type=f32)
        kk = pl.program_id(2)

        @pl.when(kk == 0)
        def _():
            o_ref[...] = part

        @pl.when(kk > 0)
        def _():
            o_ref[...] += part

    return pl.pallas_call(
        body, name=name, grid=(ka // tka, n // tn, t // tt),
        in_specs=[pl.BlockSpec((tt, tka), lambda i, j, k: (k, i)), pl.BlockSpec((tt, tn), lambda i, j, k: (k, j))],
        out_specs=pl.BlockSpec((tka, tn), lambda i, j, k: (i, j)),
        out_shape=jax.ShapeDtypeStruct((ka, n), f32),
        compiler_params=_params(("parallel", "parallel", "arbitrary")),
    )(a, b)


def _row_tile(t):
    return min(t, 512)


def _rms_fwd(x, g, *, name):
    t, d = x.shape
    tr = _row_tile(t)

    def body(x_ref, g_ref, h_ref):
        xv = x_ref[...]
        r = lax.rsqrt(jnp.mean(xv * xv, axis=1, keepdims=True) + NORM_EPS)
        h_ref[...] = (xv * r * g_ref[...]).astype(bf16)

    return pl.pallas_call(
        body, name=name, grid=(t // tr,),
        in_specs=[pl.BlockSpec((tr, d), lambda i: (i, 0)), pl.BlockSpec((1, d), lambda i: (0, 0))],
        out_specs=pl.BlockSpec((tr, d), lambda i: (i, 0)),
        out_shape=jax.ShapeDtypeStruct((t, d), bf16),
        compiler_params=_params(("parallel",)),
    )(x, g)


def _rms_bwd(xin, g, dh, dres, *, want_bf16, name):
    t, d = xin.shape
    tr = _row_tile(t)

    def body(x_ref, g_ref, dh_ref, dres_ref, dx_ref, *rest):
        dg_ref = rest[-1]
        xv = x_ref[...]
        r = lax.rsqrt(jnp.mean(xv * xv, axis=1, keepdims=True) + NORM_EPS)
        xn = xv * r
        dhv = dh_ref[...]
        dxn = dhv * g_ref[...]
        dx = dres_ref[...] + r * (dxn - xn * jnp.mean(dxn * xn, axis=1, keepdims=True))
        dx_ref[...] = dx
        if want_bf16:
            rest[0][...] = dx.astype(bf16)
        part = jnp.sum(dhv * xn, axis=0, keepdims=True)

        @pl.when(pl.program_id(0) == 0)
        def _():
            dg_ref[...] = part

        @pl.when(pl.program_id(0) > 0)
        def _():
            dg_ref[...] += part

    row = pl.BlockSpec((tr, d), lambda i: (i, 0))
    vec = pl.BlockSpec((1, d), lambda i: (0, 0))
    out_shape = [jax.ShapeDtypeStruct((t, d), f32)] + ([jax.ShapeDtypeStruct((t, d), bf16)] if want_bf16 else []) \
        + [jax.ShapeDtypeStruct((1, d), f32)]
    return pl.pallas_call(
        body, name=name, grid=(t // tr,),
        in_specs=[row, vec, row, row],
        out_specs=[row] + ([row] if want_bf16 else []) + [vec],
        out_shape=out_shape,
        compiler_params=_params(("arbitrary",)),
    )(xin, g, dh, dres)


def _loss_head(x2, tgt, g, *, name):
    t, d = x2.shape
    tr = _row_tile(t)

    def body(x_ref, t_ref, g_ref, dx_ref, dxb_ref, dg_ref, loss_ref):
        xv = x_ref[...]
        gv = g_ref[...]
        r = lax.rsqrt(jnp.mean(xv * xv, axis=1, keepdims=True) + NORM_EPS)
        xn = xv * r
        e = xn * gv - t_ref[...]
        lpart = jnp.zeros((1, LANES), f32) + 0.5 * _sum_all(jnp.mean(e * e, axis=1, keepdims=True))
        dy = e * (1.0 / d)
        dxn = dy * gv
        dx = r * (dxn - xn * jnp.mean(dxn * xn, axis=1, keepdims=True))
        dx_ref[...] = dx
        dxb_ref[...] = dx.astype(bf16)
        gpart = jnp.sum(dy * xn, axis=0, keepdims=True)

        @pl.when(pl.program_id(0) == 0)
        def _():
            dg_ref[...] = gpart
            loss_ref[...] = lpart

        @pl.when(pl.program_id(0) > 0)
        def _():
            dg_ref[...] += gpart
            loss_ref[...] += lpart

    row = pl.BlockSpec((tr, d), lambda i: (i, 0))
    vec = pl.BlockSpec((1, d), lambda i: (0, 0))
    return pl.pallas_call(
        body, name=name, grid=(t // tr,),
        in_specs=[row, row, vec],
        out_specs=[row, row, vec, pl.BlockSpec((1, LANES), lambda i: (0, 0))],
        out_shape=[jax.ShapeDtypeStruct((t, d), f32), jax.ShapeDtypeStruct((t, d), bf16),
                   jax.ShapeDtypeStruct((1, d), f32), jax.ShapeDtypeStruct((1, LANES), f32)],
        compiler_params=_params(("arbitrary",)),
    )(x2, tgt, g)


def _merge_bwd(dm, pa, pb, gl, bg, *, name):
    t, d = pa.shape
    tr = _row_tile(t)

    def body(dm_ref, pa_ref, pb_ref, gla_ref, glb_ref, bga_ref, bgb_ref, dpa_ref, dpb_ref, dgl_ref, dbg_ref):
        dmv = dm_ref[...]
        ga = _sigmoid(gla_ref[...] + bga_ref[...])
        gb = _sigmoid(glb_ref[...] + bgb_ref[...])
        dpa_ref[...] = (dmv * ga).astype(bf16)
        dpb_ref[...] = (dmv * gb).astype(bf16)
        dla = dmv * pa_ref[...] * ga * (1.0 - ga)
        dlb = dmv * pb_ref[...] * gb * (1.0 - gb)
        dgl_ref[:, :d] = dla.astype(bf16)
        dgl_ref[:, d:] = dlb.astype(bf16)
        sa = jnp.sum(dla, axis=0, keepdims=True)
        sb = jnp.sum(dlb, axis=0, keepdims=True)

        @pl.when(pl.program_id(0) == 0)
        def _():
            dbg_ref[:, :d] = sa
            dbg_ref[:, d:] = sb

        @pl.when(pl.program_id(0) > 0)
        def _():
            dbg_ref[:, :d] += sa
            dbg_ref[:, d:] += sb

    row = pl.BlockSpec((tr, d), lambda i: (i, 0))
    return pl.pallas_call(
        body, name=name, grid=(t // tr,),
        in_specs=[row, row, row, row, pl.BlockSpec((tr, d), lambda i: (i, 1)),
                  pl.BlockSpec((1, d), lambda i: (0, 0)), pl.BlockSpec((1, d), lambda i: (0, 1))],
        out_specs=[row, row, pl.BlockSpec((tr, 2 * d), lambda i: (i, 0)), pl.BlockSpec((1, 2 * d), lambda i: (0, 0))],
        out_shape=[jax.ShapeDtypeStruct((t, d), bf16), jax.ShapeDtypeStruct((t, d), bf16),
                   jax.ShapeDtypeStruct((t, 2 * d), bf16), jax.ShapeDtypeStruct((1, 2 * d), f32)],
        compiler_params=_params(("arbitrary",)),
    )(dm, pa, pb, gl, gl, bg, bg)


_INV_SQRT2 = 1.0 / math.sqrt(2.0)
_INV_SQRT2PI = 1.0 / math.sqrt(2.0 * math.pi)


def _gmlp_common(uv, vg, vb, with_grad=False):
    cdf = 0.5 * (1.0 + lax.erf(uv * _INV_SQRT2))
    zz = uv * cdf
    u, vhat, rstd, vn = _gmlp_norm(zz, vg, vb)
    if not with_grad:
        return u, vhat, rstd, vn
    return u, vhat, rstd, vn, cdf + uv * jnp.exp(-0.5 * uv * uv) * _INV_SQRT2PI


def _gmlp_norm(zz, vg, vb):
    u = zz[:, :GMLP_WIDTH]
    v = zz[:, GMLP_WIDTH:]
    mu = jnp.mean(v, axis=1, keepdims=True)
    vc = v - mu
    rstd = lax.rsqrt(jnp.mean(vc * vc, axis=1, keepdims=True) + NORM_EPS)
    vhat = vc * rstd
    vn = vhat * vg + vb
    return u, vhat, rstd, vn


def _gmlp_fwd(uv, vg, vb, wsp, bsp_t, *, name):
    t = uv.shape[0]
    per_step = 4 if t % (4 * CHUNK) == 0 else 1
    rows = per_step * CHUNK

    def body(uv_ref, vg_ref, vb_ref, w_ref, b_ref, y_ref):
        tril = _iota((CHUNK, CHUNK), 0) >= _iota((CHUNK, CHUNK), 1)
        bt = b_ref[...]
        for q in range(per_step):
            qs = slice(q * CHUNK, (q + 1) * CHUNK)
            u, _, _, vn = _gmlp_common(uv_ref[qs, :], vg_ref[...], vb_ref[...])
            for g in range(GMLP_GROUPS):
                sl = slice(g * CHUNK, (g + 1) * CHUNK)
                w = jnp.where(tril, w_ref[g], 0.0)
                s = _dot(w, vn[:, sl]) + bt[:, g:g + 1]
                y_ref[qs, sl] = (u[:, sl] * s).astype(bf16)

    return pl.pallas_call(
        body, name=name, grid=(t // rows,),
        in_specs=[pl.BlockSpec((rows, 2 * GMLP_WIDTH), lambda c: (c, 0)),
                  pl.BlockSpec((1, GMLP_WIDTH), lambda c: (0, 0)), pl.BlockSpec((1, GMLP_WIDTH), lambda c: (0, 0)),
                  pl.BlockSpec((GMLP_GROUPS, CHUNK, CHUNK), lambda c: (0, 0, 0)),
                  pl.BlockSpec((CHUNK, LANES), lambda c: (0, 0))],
        out_specs=pl.BlockSpec((rows, GMLP_WIDTH), lambda c: (c, 0)),
        out_shape=jax.ShapeDtypeStruct((t, GMLP_WIDTH), bf16),
        compiler_params=_params(("parallel",)),
    )(uv, vg, vb, wsp, bsp_t)


def _gmlp_bwd(uv, dya, vg, vb, wsp, bsp_t, *, ride=None, name):
    t = uv.shape[0]
    per_step = 4 if t % (4 * CHUNK) == 0 else 1
    rows = per_step * CHUNK
    steps = t // rows
    riding = ride is not None

    def body(*refs):
        uv_ref, dy_ref, vg_ref, vb_ref, w_ref, b_ref = refs[:6]
        duv_ref, dw_ref, db_ref, dvg_ref, dvb_ref = refs[6 + riding:11 + riding]
        first = pl.program_id(0) == 0
        if riding:
            start, finish = _swap_protocol(refs[6], refs[12], refs[13], refs[14])
            pl.when(first)(start)

        @pl.when(first)
        def _():
            dw_ref[...] = jnp.zeros_like(dw_ref)
            db_ref[...] = jnp.zeros_like(db_ref)
            dvg_ref[...] = jnp.zeros_like(dvg_ref)
            dvb_ref[...] = jnp.zeros_like(dvb_ref)

        vgv = vg_ref[...]
        tril = _iota((CHUNK, CHUNK), 0) >= _iota((CHUNK, CHUNK), 1)
        lane = _iota((CHUNK, LANES), 1)
        bt = b_ref[...]
        for q in range(per_step):
            qs = slice(q * CHUNK, (q + 1) * CHUNK)
            u, vhat, rstd, vn, gelu_grad = _gmlp_common(uv_ref[qs, :], vgv, vb_ref[...], with_grad=True)
            dy = dy_ref[qs, :]
            ds_all = dy * u
            dbacc = jnp.zeros((CHUNK, LANES), f32)
            dvh_parts = []
            for g in range(GMLP_GROUPS):
                sl = slice(g * CHUNK, (g + 1) * CHUNK)
                w = jnp.where(tril, w_ref[g], 0.0)
                vng = vn[:, sl]
                s = _dot(w, vng) + bt[:, g:g + 1]
                ds = ds_all[:, sl]
                duv_ref[qs, sl] = (dy[:, sl] * s * gelu_grad[:, sl]).astype(bf16)
                dw_ref[g] += jnp.where(tril, _dot(ds, vng, _NT), 0.0)
                dbacc = dbacc + jnp.where(lane == g, jnp.sum(ds, axis=1, keepdims=True), 0.0)
                dvn = _dot(w, ds, _TN)
                vh = vhat[:, sl]
                dvg_ref[:, sl] += jnp.sum(dvn * vh, axis=0, keepdims=True)
                dvb_ref[:, sl] += jnp.sum(dvn, axis=0, keepdims=True)
                dvh_parts.append(dvn * vgv[:, sl])
            db_ref[...] += dbacc
            dvhat = jnp.concatenate(dvh_parts, axis=1)
            m1 = jnp.mean(dvhat, axis=1, keepdims=True)
            m2 = jnp.mean(dvhat * vhat, axis=1, keepdims=True)
            dv = rstd * (dvhat - m1 - vhat * m2)
            duv_ref[qs, GMLP_WIDTH:] = (dv * gelu_grad[:, GMLP_WIDTH:]).astype(bf16)
        if riding:
            pl.when(pl.program_id(0) == steps - 1)(finish)

    vec = pl.BlockSpec((1, GMLP_WIDTH), lambda c: (0, 0))
    return pl.pallas_call(
        body, name=name, grid=(steps,),
        in_specs=[pl.BlockSpec((rows, 2 * GMLP_WIDTH), lambda c: (c, 0)),
                  pl.BlockSpec((rows, GMLP_WIDTH), lambda c: (c, 0)), vec, vec,
                  pl.BlockSpec((GMLP_GROUPS, CHUNK, CHUNK), lambda c: (0, 0, 0)),
                  pl.BlockSpec((CHUNK, LANES), lambda c: (0, 0))] + [_ANY] * riding,
        out_specs=[pl.BlockSpec((rows, 2 * GMLP_WIDTH), lambda c: (c, 0)),
                   pl.BlockSpec((GMLP_GROUPS, CHUNK, CHUNK), lambda c: (0, 0, 0)),
                   pl.BlockSpec((CHUNK, LANES), lambda c: (0, 0)), vec, vec] + [_ANY] * riding,
        out_shape=[jax.ShapeDtypeStruct((t, 2 * GMLP_WIDTH), bf16),
                   jax.ShapeDtypeStruct((GMLP_GROUPS, CHUNK, CHUNK), f32),
                   jax.ShapeDtypeStruct((CHUNK, LANES), f32),
                   jax.ShapeDtypeStruct((1, GMLP_WIDTH), f32), jax.ShapeDtypeStruct((1, GMLP_WIDTH), f32)]
        + ([jax.ShapeDtypeStruct(ride.shape[:1] + ride.shape[2:], ride.dtype)] if riding else []),
        scratch_shapes=list(_SWAP_SCRATCH) if riding else [],
        compiler_params=_params(("arbitrary",)),
    )(uv, dya, vg, vb, wsp, bsp_t, *([ride] if riding else []))


_CONV_COLS = 512
_XS0, _B0, _C0 = 0, D_INNER, D_INNER + N_GROUPS * D_STATE


_TAIL = 8


def _conv_silu(cur_ref, tail_ref, w_ref, b_ref, has_prev, xc_ref, cv_ref):
    row = _iota((_TAIL, _CONV_COLS), 0)
    for j in range(CONV_DIM // _CONV_COLS):
        sl = slice(j * _CONV_COLS, (j + 1) * _CONV_COLS)
        cur = cur_ref[:, sl]
        tail = jnp.where(has_prev, tail_ref[:, sl], 0.0)
        acc = cur * w_ref[CONV_W - 1:CONV_W, sl] + b_ref[:, sl]
        for s in range(1, CONV_W):
            rolled = pltpu.roll(cur, s, 0)
            top = jnp.where(row >= s, rolled[:_TAIL], pltpu.roll(tail, s, 0))
            sh = jnp.concatenate([top, rolled[_TAIL:]], axis=0)
            acc = acc + sh * w_ref[CONV_W - 1 - s:CONV_W - s, sl]
        cv_ref[:, sl] = acc
        xc_ref[:, sl] = acc * _sigmoid(acc)


def _col_bcast(mat, h):
    return jnp.broadcast_to(mat[:, h:h + 1], (CHUNK, LANES))


def _head_expand(cols):
    lo = _iota((CHUNK, LANES), 1) < HEAD_DIM
    return jnp.concatenate([jnp.where(lo, cols[2 * j], cols[2 * j + 1]) for j in range(N_HEADS // 2)], axis=1)


def _ssd_chunk_scalars(dtr, dtb, alog):
    xdt_pre = dtr + dtb
    dtv = jnp.maximum(xdt_pre, 0.0) + jnp.log(1.0 + jnp.exp(-jnp.abs(xdt_pre)))
    a = -jnp.exp(alog)
    ltri = (_iota((CHUNK, CHUNK), 0) >= _iota((CHUNK, CHUNK), 1)).astype(f32)
    cs = _dot32(ltri, dtv * a)
    csb = [_col_bcast(cs, h) for h in range(N_HEADS)]
    cs_x = _head_expand(csb)
    dt_x = _head_expand([_col_bcast(dtv, h) for h in range(N_HEADS)])
    cl_x = cs_x[CHUNK - 1:CHUNK, :]
    return dict(xdt_pre=xdt_pre, dtv=dtv, a=a, cs=cs, cs_t=cs.T, csb=csb, dt_x=dt_x, e_x=jnp.exp(cs_x),
                dec_x=jnp.exp(cl_x - cs_x), dk_x=jnp.exp(cl_x))


def _head_masks():
    lane = _iota((CHUNK, GROUP_W), 1)
    return [(lane >= r * HEAD_DIM) & (lane < (r + 1) * HEAD_DIM) for r in range(HEADS_PER_GROUP)]


def _stack_heads(a, masks):
    return jnp.concatenate([jnp.where(m, a, 0.0) for m in masks], axis=0).astype(bf16)


def _seg_sum(a, seg):
    hi = a.astype(jnp.bfloat16)
    lo = (a - hi.astype(f32)).astype(jnp.bfloat16)
    return (lax.dot_general(hi, seg, _NN, preferred_element_type=f32)
            + lax.dot_general(lo, seg, _NN, preferred_element_type=f32))


def _head_seg_matrix():
    return (_iota((D_INNER, LANES), 0) // HEAD_DIM == _iota((D_INNER, LANES), 1)).astype(jnp.bfloat16)


def _ssd_fwd(xbc, z, dtr, cw, cb, dtb, alog, dsk_x, gs, *, ride=None, name):
    t = xbc.shape[0]
    nc = t // CHUNK
    tiles = CHUNK // _TAIL

    def body(*refs):
        cur_ref, tail_ref, z_ref, dtr_ref, cw_ref, cb_ref, dtb_ref, alog_ref, dsk_ref, gs_ref = refs[:10]
        if ride is None:
            yb_ref, hp_ref, cv_ref, state_ref, xc_ref = refs[10:]
        else:
            ride_ref, yb_ref, hp_ref, cv_ref, got_ref, state_ref, xc_ref, send_sems, recv_sems = refs[10:]
        c = pl.program_id(0)
        if ride is not None:
            start, relay, finish = _gather_protocol(ride_ref, got_ref, send_sems, recv_sems)
            pl.when(c == 0)(start)
            pl.when(c == nc // 2)(relay)

        @pl.when(c == 0)
        def _():
            state_ref[...] = jnp.zeros_like(state_ref)

        _conv_silu(cur_ref, tail_ref, cw_ref, cb_ref, c > 0, xc_ref, cv_ref)
        sc = _ssd_chunk_scalars(dtr_ref[...], dtb_ref[...], alog_ref[...])
        tril = _iota((CHUNK, CHUNK), 0) >= _iota((CHUNK, CHUNK), 1)
        masks = _head_masks()
        hp_ref[0] = state_ref[...]
        for g in range(N_GROUPS):
            gsl = slice(g * GROUP_W, (g + 1) * GROUP_W)
            xs_g = xc_ref[:, gsl]
            bg = xc_ref[:, _B0 + g * D_STATE:_B0 + (g + 1) * D_STATE]
            cg = xc_ref[:, _C0 + g * D_STATE:_C0 + (g + 1) * D_STATE]
            xdt_g = xs_g * sc["dt_x"][:, gsl]
            cbm = _dot(cg, bg, _NT)
            mw = jnp.concatenate(
                [cbm * jnp.exp(jnp.where(tril, sc["csb"][h] - sc["cs_t"][h:h + 1, :], -1e30))
                 for h in range(g * HEADS_PER_GROUP, (g + 1) * HEADS_PER_GROUP)], axis=1)
            ht_g = state_ref[:, gsl]
            y_g = _dot(mw, _stack_heads(xdt_g, masks)) + sc["e_x"][:, gsl] * _dot(cg, ht_g) + dsk_ref[:, gsl] * xs_g
            state_ref[:, gsl] = ht_g * sc["dk_x"][:, gsl] + _dot(bg, xdt_g * sc["dec_x"][:, gsl], _TN)
            zg = z_ref[:, gsl]
            yg = y_g * zg * _sigmoid(zg)
            rs = lax.rsqrt(jnp.mean(yg * yg, axis=1, keepdims=True) + NORM_EPS)
            yb_ref[:, gsl] = (yg * rs * gs_ref[:, gsl]).astype(bf16)
        if ride is not None:
            pl.when(c == nc - 1)(finish)

    def chunk(w):
        return pl.BlockSpec((CHUNK, w), lambda c: (c, 0))

    def const(shape):
        return pl.BlockSpec(shape, lambda c: (0,) * len(shape))

    riding = ride is not None
    return pl.pallas_call(
        body, name=name, grid=(nc,),
        in_specs=[chunk(CONV_DIM), pl.BlockSpec((_TAIL, CONV_DIM), lambda c: (jnp.maximum(c * tiles - 1, 0), 0)),
                  chunk(D_INNER), chunk(LANES), const((CONV_W, CONV_DIM)), const((1, CONV_DIM)),
                  const((1, LANES)), const((1, LANES)), const((1, D_INNER)), const((1, D_INNER))] + [_ANY] * riding,
        out_specs=[chunk(D_INNER), pl.BlockSpec((1, D_STATE, D_INNER), lambda c: (c, 0, 0)), chunk(CONV_DIM)]
        + [_ANY] * riding,
        out_shape=[jax.ShapeDtypeStruct((t, D_INNER), bf16), jax.ShapeDtypeStruct((nc, D_STATE, D_INNER), f32),
                   jax.ShapeDtypeStruct((t, CONV_DIM), f32)]
        + ([jax.ShapeDtypeStruct((N_CHIPS,) + ride.shape, ride.dtype)] if riding else []),
        scratch_shapes=[pltpu.VMEM((D_STATE, D_INNER), f32), pltpu.VMEM((CHUNK, CONV_DIM), f32)]
        + (list(_GATHER_SCRATCH) if riding else []),
        compiler_params=_params(("arbitrary",)),
    )(xbc, xbc, z, dtr, cw, cb, dtb, alog, dsk_x, gs, *([ride] if riding else []))


def _ssd_bwd(xbc, cv, z, dtr, hprev, dyb, cw, dtb, alog, dsk_x, gs, seg, *, ride=None, name):
    t = xbc.shape[0]
    nc = t // CHUNK

    def body(*refs):
        (cur_ref, cv_ref, z_ref, dtr_ref, hp_ref, dyb_ref, cw_ref, dtb_ref, alog_ref, dsk_ref, gs_ref,
         seg_ref) = refs[:12]
        rest = refs[12:]
        if ride is not None:
            ride_ref, got_ref, send_sems, recv_sems = rest[0], rest[10], rest[-2], rest[-1]
            rest = rest[1:10] + rest[11:-2]
        (dz_ref, dxbc_ref, ddt_ref, dcw_ref, dcb_ref, ddtb_ref, dalog_ref, ddsk_ref, dgs_ref,
         dh_ref, dcnext_ref, xc_ref, dxc_ref, x13_ref, x2_ref, rows_ref) = rest
        i = pl.program_id(0)
        if ride is not None:
            start, finish = _scatter_protocol(ride_ref, got_ref, send_sems, recv_sems)
            pl.when(i == 0)(start)

        @pl.when(i == 0)
        def _():
            for ref in (dh_ref, dcnext_ref, dcw_ref, dcb_ref, ddtb_ref, dalog_ref, ddsk_ref, dgs_ref, rows_ref):
                ref[...] = jnp.zeros_like(ref)

        for j in range(CONV_DIM // _CONV_COLS):
            sl = slice(j * _CONV_COLS, (j + 1) * _CONV_COLS)
            cvv = cv_ref[:, sl]
            xc_ref[:, sl] = cvv * _sigmoid(cvv)
        sc = _ssd_chunk_scalars(dtr_ref[...], dtb_ref[...], alog_ref[...])
        tril = _iota((CHUNK, CHUNK), 0) >= _iota((CHUNK, CHUNK), 1)
        triu = _iota((CHUNK, CHUNK), 0) <= _iota((CHUNK, CHUNK), 1)
        masks = _head_masks()
        rowh = _iota((N_HEADS, CHUNK), 0)
        dcs_t = jnp.zeros((N_HEADS, CHUNK), f32)
        for g in range(N_GROUPS):
            gsl = slice(g * GROUP_W, (g + 1) * GROUP_W)
            xs_g = xc_ref[:, gsl]
            bg = xc_ref[:, _B0 + g * D_STATE:_B0 + (g + 1) * D_STATE]
            cg = xc_ref[:, _C0 + g * D_STATE:_C0 + (g + 1) * D_STATE]
            dt_g, e_g, dec_g, dk_g = sc["dt_x"][:, gsl], sc["e_x"][:, gsl], sc["dec_x"][:, gsl], sc["dk_x"][:, gsl]
            dsk_g = dsk_ref[:, gsl]
            xdt_g = xs_g * dt_g
            xdt_stack = _stack_heads(xdt_g, masks)
            cbm = _dot(cg, bg, _NT)
            cbt = _dot(bg, cg, _NT)
            heads = range(g * HEADS_PER_GROUP, (g + 1) * HEADS_PER_GROUP)
            lmats = [jnp.exp(jnp.where(tril, sc["csb"][h] - sc["cs_t"][h:h + 1, :], -1e30)) for h in heads]
            mw = jnp.concatenate([cbm * lm for lm in lmats], axis=1)
            mtw = jnp.concatenate(
                [cbt * jnp.exp(jnp.where(triu, sc["cs_t"][h:h + 1, :] - sc["csb"][h], -1e30)) for h in heads], axis=1)
            ht_g = hp_ref[0, :, gsl]
            dhn_g = dh_ref[:, gsl]
            yoff = e_g * _dot(cg, ht_g)
            y_g = _dot(mw, xdt_stack) + yoff + dsk_g * xs_g
            zg = z_ref[:, gsl]
            sz = _sigmoid(zg)
            silu = zg * sz
            yg = y_g * silu
            rs = lax.rsqrt(jnp.mean(yg * yg, axis=1, keepdims=True) + NORM_EPS)
            yn = yg * rs
            dyb = dyb_ref[:, gsl]
            dgs_ref[:, gsl] += jnp.sum(dyb * yn, axis=0, keepdims=True)
            dyn = dyb * gs_ref[:, gsl]
            dyg = rs * (dyn - yn * jnp.mean(dyn * yn, axis=1, keepdims=True))
            dy_g = dyg * silu
            dz_ref[:, gsl] = (dyg * y_g * (sz * (1.0 + zg * (1.0 - sz)))).astype(bf16)
            dy_stack = _stack_heads(dy_g, masks)
            dm_w = _dot(dy_g, xdt_stack, _NT)
            dmt_w = _dot(xdt_g, dy_stack, _NT)
            dxdt = _dot(mtw, dy_stack)
            dcb_acc = jnp.zeros((CHUNK, CHUNK), f32)
            for r, h in enumerate(heads):
                hs = slice(r * CHUNK, (r + 1) * CHUNK)
                dml = dm_w[:, hs] * lmats[r]
                dcb_acc = dcb_acc + dml
                col = jnp.sum(dml * cbm, axis=0, keepdims=True)
                row = jnp.sum(dmt_w[:, hs] * mtw[:, hs], axis=0, keepdims=True)
                dcs_t = dcs_t + jnp.where(rowh == h, row - col, 0.0)
            w = _dot(bg, dhn_g)
            dxdt = dxdt + dec_g * w
            decx3 = dec_g * (xdt_g * w)
            dg_g = e_g * dy_g
            d_c = _dot(dg_g, ht_g, _NT) + _dot(dcb_acc, bg)
            d_b = _dot(dcb_acc, cg, _TN) + _dot(xdt_g * dec_g, dhn_g, _NT)
            dh_ref[:, gsl] = dhn_g * dk_g + _dot(cg, dg_g, _TN)
            dxc_ref[:, gsl] = dsk_g * dy_g + dxdt * dt_g
            dxc_ref[:, _B0 + g * D_STATE:_B0 + (g + 1) * D_STATE] = d_b
            dxc_ref[:, _C0 + g * D_STATE:_C0 + (g + 1) * D_STATE] = d_c
            x13_ref[:, gsl] = dy_g * yoff - decx3
            x2_ref[:, gsl] = dxdt * xs_g
            rows_ref[0:1, gsl] = jnp.sum(dhn_g * ht_g, axis=0, keepdims=True)
            rows_ref[1:2, gsl] = jnp.sum(decx3, axis=0, keepdims=True)
            rows_ref[2:3, gsl] = jnp.sum(dy_g * xs_g, axis=0, keepdims=True)
        segm = seg_ref[...]
        r13 = _seg_sum(x13_ref[...], segm)
        r2 = _seg_sum(x2_ref[...], segm)
        small = _seg_sum(rows_ref[...], segm)
        lane = _iota((CHUNK, LANES), 1)
        rowi = _iota((CHUNK, LANES), 0)
        dcl_row = small[0:1, :] * jnp.exp(sc["cs"][CHUNK - 1:CHUNK, :]) + small[1:2, :]
        dcs = r13 + jnp.where(rowi == CHUNK - 1, dcl_row, 0.0)
        dcs_t_all = dcs.T + jnp.concatenate([dcs_t, jnp.zeros((LANES - N_HEADS, CHUNK), f32)], axis=0)
        dda = _dot32(dcs_t_all, tril.astype(f32)).T
        a = sc["a"]
        ddt_total = r2 + dda * a
        dalog_ref[...] += jnp.sum(dda * sc["dtv"], axis=0, keepdims=True) * a
        ddtr = jnp.where(lane < N_HEADS, ddt_total * _sigmoid(sc["xdt_pre"]), 0.0)
        ddtb_ref[...] += jnp.sum(ddtr, axis=0, keepdims=True)
        ddt_ref[...] = ddtr.astype(bf16)
        ddsk_ref[...] += small[2:3, :]
        row8 = _iota((_TAIL, _CONV_COLS), 0)
        for j in range(CONV_DIM // _CONV_COLS):
            sl = slice(j * _CONV_COLS, (j + 1) * _CONV_COLS)
            cvv = cv_ref[:, sl]
            sg = _sigmoid(cvv)
            dconv = dxc_ref[:, sl] * (sg * (1.0 + cvv * (1.0 - sg)))
            nxt = dcnext_ref[:, sl]
            cur = cur_ref[:, sl]
            dxin = dconv * cw_ref[CONV_W - 1:CONV_W, sl]
            dcw_ref[CONV_W - 1:CONV_W, sl] += jnp.sum(dconv * cur, axis=0, keepdims=True)
            for s in range(1, CONV_W):
                rolled = pltpu.roll(dconv, CHUNK - s, 0)
                bot = jnp.where(row8 < _TAIL - s, rolled[CHUNK - _TAIL:], pltpu.roll(nxt, _TAIL - s, 0))
                up = jnp.concatenate([rolled[:CHUNK - _TAIL], bot], axis=0)
                dxin = dxin + up * cw_ref[CONV_W - 1 - s:CONV_W - s, sl]
                dcw_ref[CONV_W - 1 - s:CONV_W - s, sl] += jnp.sum(up * cur, axis=0, keepdims=True)
            dcb_ref[:, sl] += jnp.sum(dconv, axis=0, keepdims=True)
            dxbc_ref[:, sl] = dxin.astype(bf16)
            dcnext_ref[:, sl] = dconv[:_TAIL]
        if ride is not None:
            pl.when(i == nc - 1)(finish)

    def chunk(w):
        return pl.BlockSpec((CHUNK, w), lambda i: (nc - 1 - i, 0))

    def const(shape):
        return pl.BlockSpec(shape, lambda i: (0,) * len(shape))

    riding = ride is not None
    return pl.pallas_call(
        body, name=name, grid=(nc,),
        in_specs=[chunk(CONV_DIM), chunk(CONV_DIM),
                  chunk(D_INNER), chunk(LANES), pl.BlockSpec((1, D_STATE, D_INNER), lambda i: (nc - 1 - i, 0, 0)),
                  chunk(D_INNER), const((CONV_W, CONV_DIM)),
                  const((1, LANES)), const((1, LANES)), const((1, D_INNER)), const((1, D_INNER)),
                  const((D_INNER, LANES))] + [_ANY] * riding,
        out_specs=[chunk(D_INNER), chunk(CONV_DIM), chunk(LANES), const((CONV_W, CONV_DIM)), const((1, CONV_DIM)),
                   const((1, LANES)), const((1, LANES)), const((1, LANES)), const((1, D_INNER))] + [_ANY] * riding,
        out_shape=[jax.ShapeDtypeStruct((t, D_INNER), bf16), jax.ShapeDtypeStruct((t, CONV_DIM), bf16),
                   jax.ShapeDtypeStruct((t, LANES), bf16), jax.ShapeDtypeStruct((CONV_W, CONV_DIM), f32),
                   jax.ShapeDtypeStruct((1, CONV_DIM), f32), jax.ShapeDtypeStruct((1, LANES), f32),
                   jax.ShapeDtypeStruct((1, LANES), f32), jax.ShapeDtypeStruct((1, LANES), f32),
                   jax.ShapeDtypeStruct((1, D_INNER), f32)]
        + ([jax.ShapeDtypeStruct((N_CHIPS - 1,) + ride.shape[1:], ride.dtype)] if riding else []),
        scratch_shapes=[pltpu.VMEM((D_STATE, D_INNER), f32), pltpu.VMEM((_TAIL, CONV_DIM), f32),
                        pltpu.VMEM((CHUNK, CONV_DIM), f32), pltpu.VMEM((CHUNK, CONV_DIM), f32),
                        pltpu.VMEM((CHUNK, D_INNER), f32), pltpu.VMEM((CHUNK, D_INNER), f32),
                        pltpu.VMEM((_TAIL, D_INNER), f32)]
        + (list(_SCATTER_SCRATCH) if riding else []),
        compiler_params=_params(("arbitrary",)),
    )(xbc, cv, z, dtr, hprev, dyb, cw, dtb, alog, dsk_x, gs, seg, *([ride] if riding else []))


def _adamw(w, g, m, v, *, name):
    r, c = w.shape
    tr = r
    while tr * c * 4 > 2 * _MB and tr % 16 == 0:
        tr //= 2

    def body(w_ref, g_ref, m_ref, v_ref, d_ref, m2_ref, v2_ref):
        gv = g_ref[...]
        m2 = ADAM_B1 * m_ref[...] + (1.0 - ADAM_B1) * gv
        v2 = ADAM_B2 * v_ref[...] + (1.0 - ADAM_B2) * (gv * gv)
        m_hat = m2 / (1.0 - ADAM_B1 ** ADAM_STEP)
        v_hat = v2 / (1.0 - ADAM_B2 ** ADAM_STEP)
        d_ref[...] = -ADAM_LR * (m_hat / (jnp.sqrt(v_hat) + ADAM_EPS) + ADAM_WD * w_ref[...])
        m2_ref[...] = m2
        v2_ref[...] = v2

    blk = pl.BlockSpec((tr, c), lambda i: (i, 0))
    return pl.pallas_call(
        body, name=name, grid=(r // tr,),
        in_specs=[blk] * 4, out_specs=[blk] * 3,
        out_shape=[jax.ShapeDtypeStruct((r, c), f32)] * 3,
        compiler_params=_params(("parallel",)),
    )(w, g, m, v)


def _row_block(rows, cols):
    cap = max(16, 2 * _MB // (4 * cols))
    return max(tr for tr in range(16, min(cap, rows) + 1, 16) if rows % tr == 0)


def _cast_bf16(a, *, name):
    r, c = a.shape
    tr = _row_block(r, c)

    def body(a_ref, o_ref):
        o_ref[...] = a_ref[...].astype(bf16)

    blk = pl.BlockSpec((tr, c), lambda i: (i, 0))
    return pl.pallas_call(
        body, name=name, grid=(r // tr,), in_specs=[blk], out_specs=blk,
        out_shape=jax.ShapeDtypeStruct((r, c), bf16), compiler_params=_params(("parallel",)),
    )(a)


_ANY = pl.BlockSpec(memory_space=pl.ANY)


def _place():
    x, y, c = lax.axis_index("x"), lax.axis_index("y"), lax.axis_index("c")
    other_chips = [(1 - x, y), (x, 1 - y), (1 - x, 1 - y)]
    return x, y, c, other_chips


def _gather_protocol(in_ref, out_ref, send_sems, recv_sems):
    x, y, c, chips = _place()
    me = 2 * x + y
    sibling = (x, y, 1 - c)
    where = [2 * cx + cy for cx, cy in chips]

    def cp(k, chip, half, to, src=None):
        dst = out_ref.at[chip, half]
        return pltpu.make_async_remote_copy(
            src_ref=dst if src is None else src, dst_ref=dst, send_sem=send_sems.at[k], recv_sem=recv_sems.at[k],
            device_id=to, device_id_type=MESH)

    def sends():
        return [cp(j, me, c, (*chips[j], c), src=in_ref.at[c]) for j in range(2)]

    def relays():
        return [cp(3 + j, where[j], c, sibling) for j in range(3)]

    def landed(j):
        return cp(j, where[j], c, sibling)

    def start():
        for f in sends():
            f.start()

    def relay():
        onward = relays()
        for first in range(2):
            @pl.when(c == first)
            def _(first=first):
                landed(first).wait_recv()
                cp(2, where[first], c, (*chips[1 - first], c)).start()
                onward[first].start()
                landed(1 - first).wait_recv()
                onward[1 - first].start()

    def finish():
        landed(2).wait_recv()
        relays()[2].start()
        for j in range(3):
            cp(3 + j, where[j], 1 - c, sibling).wait_recv()
        for f in sends() + [landed(2)] + relays():
            f.wait_send()

    return start, relay, finish


_GATHER_SCRATCH = [pltpu.SemaphoreType.DMA((6,)), pltpu.SemaphoreType.DMA((6,))]


def _gather_shards(shard, *, name):
    _, rh, lanes = shard.shape

    def body(in_ref, out_ref, send_sems, recv_sems):
        start, relay, finish = _gather_protocol(in_ref, out_ref, send_sems, recv_sems)
        start()
        relay()
        finish()

    return pl.pallas_call(
        body, name=name, in_specs=[_ANY], out_specs=_ANY,
        out_shape=jax.ShapeDtypeStruct((N_CHIPS, 2, rh, lanes), shard.dtype),
        scratch_shapes=list(_GATHER_SCRATCH),
    )(shard)


def _scatter_protocol(p_ref, out_ref, send_sems, recv_sems):
    x, y, c, chips = _place()

    def copies():
        return [pltpu.make_async_remote_copy(
            src_ref=p_ref.at[2 * cx + cy], dst_ref=out_ref.at[j], send_sem=send_sems.at[j], recv_sem=recv_sems.at[j],
            device_id=(cx, cy, c), device_id_type=MESH) for j, (cx, cy) in enumerate(chips)]

    def start():
        for cpy in copies():
            cpy.start()

    def finish():
        for cpy in copies():
            cpy.wait()

    return start, finish


_SCATTER_SCRATCH = [pltpu.SemaphoreType.DMA((3,)), pltpu.SemaphoreType.DMA((3,))]


def _swap_protocol(g_ref, out_ref, send_sems, recv_sems):
    x, y, c, _ = _place()

    def copies():
        return [pltpu.make_async_remote_copy(
            src_ref=g_ref.at[k, 1 - c], dst_ref=out_ref.at[k], send_sem=send_sems.at[k], recv_sem=recv_sems.at[k],
            device_id=(x, y, 1 - c), device_id_type=MESH) for k in range(N_CHIPS)]

    def start():
        for cpy in copies():
            cpy.start()

    def finish():
        for cpy in copies():
            cpy.wait()

    return start, finish


_SWAP_SCRATCH = [pltpu.SemaphoreType.DMA((N_CHIPS,)), pltpu.SemaphoreType.DMA((N_CHIPS,))]


def _rs_swap_halves(g, *, name):
    nch, _, rh, lanes = g.shape

    def body(g_ref, out_ref, send_sems, recv_sems):
        start, finish = _swap_protocol(g_ref, out_ref, send_sems, recv_sems)
        start()
        finish()

    return pl.pallas_call(
        body, name=name, in_specs=[_ANY], out_specs=_ANY,
        out_shape=jax.ShapeDtypeStruct((nch, rh, lanes), g.dtype),
        scratch_shapes=list(_SWAP_SCRATCH),
    )(g)


def _rs_add_pair(g, got, c_idx, *, name):
    nch, _, rh, lanes = g.shape
    tr = _row_block(rh, lanes)

    def body(c_ref, g_ref, got_ref, p16_ref):
        p16_ref[...] = (g_ref[...] + got_ref[...]).astype(bf16)

    blk = pl.BlockSpec((None, tr, lanes), lambda k, i, c_ref: (k, i, 0))
    return pl.pallas_call(
        body, name=name,
        grid_spec=pltpu.PrefetchScalarGridSpec(
            num_scalar_prefetch=1, grid=(nch, rh // tr),
            in_specs=[pl.BlockSpec((None, None, tr, lanes), lambda k, i, c_ref: (k, c_ref[0], i, 0)), blk],
            out_specs=blk),
        out_shape=jax.ShapeDtypeStruct((nch, rh, lanes), bf16),
        compiler_params=_params(("parallel", "parallel")),
    )(c_idx, g, got)


def _rs_add_chips(g, got_pair, got, place, *, name):
    _, _, rh, lanes = g.shape
    tr = _row_block(rh, lanes)

    def body(place_ref, g_ref, pair_ref, got_ref, o_ref):
        own = g_ref[...] + pair_ref[...]
        o_ref[...] = ((own + got_ref[0].astype(f32)) + got_ref[1].astype(f32)) + got_ref[2].astype(f32)

    return pl.pallas_call(
        body, name=name,
        grid_spec=pltpu.PrefetchScalarGridSpec(
            num_scalar_prefetch=1, grid=(rh // tr,),
            in_specs=[pl.BlockSpec((None, None, tr, lanes), lambda i, place_ref: (place_ref[0], place_ref[1], i, 0)),
                      pl.BlockSpec((None, tr, lanes), lambda i, place_ref: (place_ref[0], i, 0)),
                      pl.BlockSpec((3, tr, lanes), lambda i, place_ref: (0, i, 0))],
            out_specs=pl.BlockSpec((None, tr, lanes), lambda i, place_ref: (place_ref[1], i, 0))),
        out_shape=jax.ShapeDtypeStruct((2, rh, lanes), f32),
        compiler_params=_params(("parallel",)),
    )(place, g, got_pair, got)


def _rs_join_halves(halves, *, name):
    def body(h_ref, out_ref, send_sem, recv_sem):
        x, y, c, _ = _place()
        cpy = pltpu.make_async_remote_copy(
            src_ref=h_ref.at[c], dst_ref=out_ref.at[c], send_sem=send_sem, recv_sem=recv_sem,
            device_id=(x, y, 1 - c), device_id_type=MESH)
        cpy.start()
        cpy.wait()

    return pl.pallas_call(
        body, name=name, in_specs=[_ANY], out_specs=_ANY,
        out_shape=jax.ShapeDtypeStruct(halves.shape, halves.dtype), input_output_aliases={0: 0},
        scratch_shapes=[pltpu.SemaphoreType.DMA, pltpu.SemaphoreType.DMA],
    )(halves)


def _all_reduce_small(s, *, name):
    rs, lanes = s.shape
    rh = rs // 2

    def body(s_ref, o_ref, sib_ref, mine_ref, chips_ref, send_sems, recv_sems):
        x, y, c, chips = _place()
        me = 2 * x + y
        sibling = (x, y, 1 - c)
        rows = pl.ds(pl.multiple_of(c * rh, 8), rh)

        def cp(k, src, dst, to):
            return pltpu.make_async_remote_copy(src_ref=src, dst_ref=dst, send_sem=send_sems.at[k],
                                                recv_sem=recv_sems.at[k], device_id=to, device_id_type=MESH)

        swap = cp(0, s_ref, sib_ref, sibling)
        swap.start()
        swap.wait()
        mine_ref[...] = s_ref[rows, :] + sib_ref[rows, :]
        sends = [cp(1 + j, mine_ref, chips_ref.at[j], (cx, cy, c)) for j, (cx, cy) in enumerate(chips)]
        for cpy in sends:
            cpy.start()
        for cpy in sends:
            cpy.wait()
        where = [2 * cx + cy for cx, cy in chips]
        total = None
        for q in range(N_CHIPS):
            term = jnp.where(q == me, mine_ref[...], jnp.where(
                q == where[0], chips_ref[0], jnp.where(q == where[1], chips_ref[1], chips_ref[2])))
            total = term if total is None else total + term
        o_ref[rows, :] = total
        push = cp(4, o_ref.at[rows, :], o_ref.at[rows, :], sibling)
        push.start()
        push.wait()

    vm = pl.BlockSpec(memory_space=pltpu.VMEM)
    return pl.pallas_call(
        body, name=name, in_specs=[vm], out_specs=vm,
        out_shape=jax.ShapeDtypeStruct((rs, lanes), f32),
        scratch_shapes=[pltpu.VMEM((rs, lanes), f32), pltpu.VMEM((rh, lanes), f32),
                        pltpu.VMEM((N_CHIPS - 1, rh, lanes), f32), pltpu.SemaphoreType.DMA((5,)),
                        pltpu.SemaphoreType.DMA((5,))],
        compiler_params=pltpu.CompilerParams(vmem_limit_bytes=32 * _MB),
    )(s)


def _pad_lanes(a, width=LANES):
    return jnp.pad(a, ((0, 0), (0, width - a.shape[1])))


def _local_grads(x, tgt, wts, small, *, fwd_ride=None, late_weights=None, swap_ride=None, bwd_ride=None,
                 last_ride=None):
    t = x.shape[0]
    tm = min(t, 1024)
    d = D_MODEL
    mm = functools.partial(_matmul, tm=tm)

    dtb = _pad_lanes(small["dt_bias"])
    alog = _pad_lanes(small["a_log"])
    dsk = jnp.repeat(small["d_skip"], HEAD_DIM, axis=1)
    bsp_t = _pad_lanes(small["b_spatial"].T)
    wsp = small["w_spatial"]

    h = _rms_fwd(x, small["norm_mix_g"], name="rms_mix")
    uv = mm(h, wts["uv"], tn=1024, tk=d, out_dtypes=[f32], name="proj_uv")
    z = mm(h, wts["z"], tn=1024, tk=d, out_dtypes=[f32], name="proj_z")
    xbc = mm(h, wts["xbc"], tn=1024, tk=d, out_dtypes=[f32], name="proj_xbc")
    dtr = mm(h, wts["dt"], tn=LANES, tk=d, out_dtypes=[f32], name="proj_dt")
    gl = mm(h, wts["gate"], tn=1024, tk=d, out_dtypes=[f32], name="proj_gate")
    ya = _gmlp_fwd(uv, small["v_norm_g"], small["v_norm_b"], wsp, bsp_t, name="gmlp_fwd")
    yb, hprev, cv, *gathered = _ssd_fwd(xbc, z, dtr, small["conv_w"], small["conv_b"], dtb, alog, dsk,
                                        small["ssm_norm_g"], ride=fwd_ride, name="ssd_fwd")
    if fwd_ride is not None:
        wts = {**wts, **late_weights(gathered[0])}
    pa = mm(ya, wts["pa"], tn=1024, tk=1024, out_dtypes=[f32], name="proj_a")
    tm_gate = min(t, 512)
    row_vec = [pl.BlockSpec((1, d), lambda i, j, k, half=half: (0, half)) for half in range(2)]
    gate_tiles = [pl.BlockSpec((tm_gate, d), lambda i, j, k, half=half: (i, half)) for half in range(2)]

    def merge(pb_acc, pa_t, gla, glb, bga, bgb):
        return pb_acc, _sigmoid(gla + bga) * pa_t + _sigmoid(glb + bgb) * pb_acc

    pb, merged = _matmul(yb, wts["pb"], tm=tm_gate, tn=d, tk=1024, out_dtypes=[f32, bf16], epilogue=merge,
                         extras=[pa, gl, gl, small["b_gates"], small["b_gates"]],
                         extra_specs=[None] + gate_tiles + row_vec, name="proj_b")

    def residual_norm(acc, res, g):
        x_new = res + acc
        r = lax.rsqrt(jnp.mean(x_new * x_new, axis=1, keepdims=True) + NORM_EPS)
        return x_new, x_new * r * g

    x1, h2 = mm(merged, wts["out"], tn=d, tk=1024, out_dtypes=[f32, bf16], epilogue=residual_norm,
                extras=[x, small["norm_mlp_g"]], extra_specs=[None, row_vec[0]], name="out_proj")
    act = mm(h2, wts["up"], tn=1024, tk=d, out_dtypes=[bf16],
             epilogue=lambda acc: (jnp.square(jnp.maximum(acc, 0.0)),), name="mlp_up")
    x2 = mm(act, wts["down"], tn=1024, tk=2048, out_dtypes=[f32], extras=[x1],
            epilogue=lambda acc, res: (res + acc,), name="mlp_down")

    dx2, dx2b, dgf, loss = _loss_head(x2, tgt, small["norm_final_g"], name="loss_head")
    tt = min(t, 2048)
    tn_mm = functools.partial(_matmul_tn, tt=tt)
    dw = {}
    dw["down"] = tn_mm(act, dx2b, tka=1024, tn=1024, name="dw_down")
    dup = mm(dx2b, wts["down"], nt=True, tn=1024, tk=1024, out_dtypes=[bf16], extras=[act],
             epilogue=lambda acc, a2: (acc * (2.0 * jnp.sqrt(a2).astype(f32)),), name="d_act")
    dw["up"] = tn_mm(h2, dup, tka=1024, tn=1024, name="dw_up")
    dh2 = mm(dup, wts["up"], nt=True, tn=1024, tk=2048, out_dtypes=[f32], name="d_h2")
    dx1, dx1b, dg_mlp = _rms_bwd(x1, small["norm_mlp_g"], dh2, dx2, want_bf16=True, name="rms_mlp_bwd")
    dw["out"] = tn_mm(merged, dx1b, tka=1024, tn=1024, name="dw_out")
    dmerged = mm(dx1b, wts["out"], nt=True, tn=1024, tk=1024, out_dtypes=[f32], name="d_merged")
    dpa, dpb, dgl, dbg = _merge_bwd(dmerged, pa, pb, gl, small["b_gates"], name="merge_bwd")
    dw["pa"] = tn_mm(ya, dpa, tka=1024, tn=1024, name="dw_pa")
    dw["pb"] = tn_mm(yb, dpb, tka=1024, tn=1024, name="dw_pb")
    dya = mm(dpa, wts["pa"], nt=True, tn=1024, tk=1024, out_dtypes=[f32], name="d_ya")
    dyb = mm(dpb, wts["pb"], nt=True, tn=1024, tk=1024, out_dtypes=[f32], name="d_yb")
    swapped = swap_ride(dw) if swap_ride is not None else None
    duv, dwsp, dbsp_t, dvg, dvb, *got_pair = _gmlp_bwd(uv, dya, small["v_norm_g"], small["v_norm_b"], wsp, bsp_t,
                                                       ride=swapped, name="gmlp_bwd")
    ride = bwd_ride(swapped, got_pair[0]) if bwd_ride is not None else None
    dz, dxbc, ddt, dcw, dcb, ddtb, dalog, ddsk, dgs, *got = _ssd_bwd(
        xbc, cv, z, dtr, hprev, dyb, small["conv_w"], dtb, alog, dsk, small["ssm_norm_g"],
        _head_seg_matrix(), ride=ride, name="ssd_bwd")
    dw["uv"] = tn_mm(h, duv, tka=1024, tn=1024, name="dw_uv")
    dw["z"] = tn_mm(h, dz, tka=1024, tn=1024, name="dw_z")
    dw["xbc"] = tn_mm(h, dxbc, tka=1024, tn=1024, name="dw_xbc")
    dw["dt"] = tn_mm(h, ddt, tka=1024, tn=LANES, name="dw_dt")
    dw["gate"] = tn_mm(h, dgl, tka=1024, tn=1024, name="dw_gate")
    last = last_ride(dw) if last_ride is not None else None
    res = _matmul_nt_sum(
        [(duv, wts["uv"]), (dz, wts["z"]), (dxbc, wts["xbc"]), (dgl, wts["gate"]), (ddt, wts["dt"])],
        tm=tm, tks=[1024] * 4 + [LANES], ride=last, name="d_h")
    dh, got_last = (res[0], res[1]) if last is not None else (res, None)
    dx, dg_mix = _rms_bwd(x, small["norm_mix_g"], dh, dx1, want_bf16=False, name="rms_mix_bwd")

    dsmall = {
        "norm_mix_g": dg_mix, "conv_w": dcw, "conv_b": dcb, "dt_bias": ddtb[:, :N_HEADS], "a_log": dalog[:, :N_HEADS],
        "d_skip": ddsk[:, :N_HEADS], "ssm_norm_g": dgs, "v_norm_g": dvg, "v_norm_b": dvb, "w_spatial": dwsp,
        "b_spatial": dbsp_t[:, :GMLP_GROUPS].T, "b_gates": dbg, "norm_mlp_g": dg_mlp, "norm_final_g": dgf,
    }
    return loss, dx, dw, dsmall, (got[0] if got else None), got_last


_IN_SHARD = IN_PROJ // N_CHIPS
_LATE = ("w_proj_a", "w_proj_b", "w_out", "w_mlp_up", "w_mlp_down")
_LATE_ROWS = {"w_proj_a": GMLP_WIDTH // N_CHIPS, "w_proj_b": D_INNER // N_CHIPS, "w_out": D_MODEL // N_CHIPS,
              "w_mlp_up": D_MODEL, "w_mlp_down": D_FF // N_CHIPS}
_LATE_TOTAL = sum(_LATE_ROWS.values())


def _late_offsets():
    off, out = 0, {}
    for k in _LATE:
        out[k] = off
        off += _LATE_ROWS[k]
    return out


_LATE_OFF = _late_offsets()

_SMALL = ("norm_mix_g", "conv_w", "conv_b", "dt_bias", "a_log", "d_skip", "ssm_norm_g", "v_norm_g", "v_norm_b",
          "w_spatial", "b_spatial", "b_gates", "norm_mlp_g", "norm_final_g")


def _pack_small(parts):
    flat = jnp.concatenate([parts[k].reshape(-1) for k in _SMALL])
    rows = -(-flat.shape[0] // (16 * LANES)) * 16
    return jnp.pad(flat, (0, rows * LANES - flat.shape[0])).reshape(rows, LANES)


def _unpack_small(packed, shapes):
    flat = packed.reshape(-1)
    out, off = {}, 0
    for k in _SMALL:
        n = math.prod(shapes[k])
        out[k] = flat[off:off + n].reshape(shapes[k])
        off += n
    return out


def _from_chip_columns(stacked):
    _, rows, cols = stacked.shape
    return stacked.transpose(1, 0, 2).reshape(rows, N_CHIPS * cols)


def _to_chip_columns(full):
    rows, cols = full.shape
    return full.reshape(rows, N_CHIPS, cols // N_CHIPS).transpose(1, 0, 2)


def _w_in_grad_by_chip(dw):
    pieces = [dw["uv"], dw["z"], dw["xbc"], dw["dt"][:, :N_HEADS], dw["gate"]]
    bounds = [0]
    for p in pieces:
        bounds.append(bounds[-1] + p.shape[1])
    chips = []
    for k in range(N_CHIPS):
        lo, hi = k * _IN_SHARD, (k + 1) * _IN_SHARD
        parts = [p[:, max(lo, b0) - b0:min(hi, b1) - b0]
                 for p, b0, b1 in zip(pieces, bounds[:-1], bounds[1:]) if min(hi, b1) > max(lo, b0)]
        chips.append(jnp.concatenate(parts, axis=1))
    return jnp.stack(chips)


def kernel(x, norm_mix_g, w_in, conv_w, conv_b, dt_bias, a_log, d_skip, ssm_norm_g, v_norm_g, v_norm_b, w_spatial, b_spatial, b_gates, w_proj_a, w_proj_b, w_out, norm_mlp_g, w_mlp_up, w_mlp_down, norm_final_g, loss_target, m_norm_mix_g, m_w_in, m_conv_w, m_conv_b, m_dt_bias, m_a_log, m_d_skip, m_ssm_norm_g, m_v_norm_g, m_v_norm_b, m_w_spatial, m_b_spatial, m_b_gates, m_w_proj_a, m_w_proj_b, m_w_out, m_norm_mlp_g, m_w_mlp_up, m_w_mlp_down, m_norm_final_g, v_norm_mix_g, v_w_in, v_conv_w, v_conv_b, v_dt_bias, v_a_log, v_d_skip, v_ssm_norm_g, v_v_norm_g, v_v_norm_b, v_w_spatial, v_b_spatial, v_b_gates, v_w_proj_a, v_w_proj_b, v_w_out, v_norm_mlp_g, v_w_mlp_up, v_w_mlp_down, v_norm_final_g):
    given = dict(locals())
    names = ("norm_mix_g", "w_in", "conv_w", "conv_b", "dt_bias", "a_log", "d_skip", "ssm_norm_g", "v_norm_g",
             "v_norm_b", "w_spatial", "b_spatial", "b_gates", "w_proj_a", "w_proj_b", "w_out", "norm_mlp_g",
             "w_mlp_up", "w_mlp_down", "norm_final_g")
    xi, yi, ci = lax.axis_index("x"), lax.axis_index("y"), lax.axis_index("c")
    me_chip = (2 * xi + yi).astype(jnp.int32)

    def halves(a):
        return a.reshape(2, a.shape[0] // 2, a.shape[1])

    def with_own(got, shard):
        whole = lax.dynamic_update_slice(got, shard[None], (me_chip, 0, 0, 0))
        return whole.reshape(N_CHIPS, 2 * shard.shape[1], shard.shape[2])

    shard_in = halves(_cast_bf16(w_in[0], name="cast_w_in"))
    shard_late = halves(_cast_bf16(jnp.concatenate([given[k][0] for k in _LATE]), name="cast_w_late"))
    shard_conv = halves(conv_w.reshape(2 * _TAIL, -1))
    w_in_full = _from_chip_columns(with_own(_gather_shards(shard_in, name="gather_w_in"), shard_in))
    o_dt, o_gate = 2 * GMLP_WIDTH + D_INNER + CONV_DIM, 2 * GMLP_WIDTH + D_INNER + CONV_DIM + N_HEADS
    wts = {
        "uv": w_in_full[:, :2 * GMLP_WIDTH], "z": w_in_full[:, 2 * GMLP_WIDTH:2 * GMLP_WIDTH + D_INNER],
        "xbc": w_in_full[:, 2 * GMLP_WIDTH + D_INNER:o_dt], "dt": _pad_lanes(w_in_full[:, o_dt:o_gate]),
        "gate": w_in_full[:, o_gate:],
    }
    conv_all = with_own(_gather_shards(shard_conv, name="gather_conv_w"), shard_conv)
    conv_full = _from_chip_columns(conv_all.reshape(N_CHIPS, CONV_W, CONV_DIM // N_CHIPS))

    def late_weights(got):
        g_late = with_own(got, shard_late)

        def rows_of(k):
            return g_late[:, _LATE_OFF[k]:_LATE_OFF[k] + _LATE_ROWS[k]]

        return {
            "pa": rows_of("w_proj_a").reshape(GMLP_WIDTH, D_MODEL),
            "pb": rows_of("w_proj_b").reshape(D_INNER, D_MODEL), "out": rows_of("w_out").reshape(D_MODEL, D_MODEL),
            "up": _from_chip_columns(rows_of("w_mlp_up")), "down": rows_of("w_mlp_down").reshape(D_FF, D_MODEL),
        }

    small = {
        "norm_mix_g": norm_mix_g, "conv_w": conv_full, "conv_b": conv_b, "dt_bias": dt_bias, "a_log": a_log,
        "d_skip": d_skip, "ssm_norm_g": ssm_norm_g, "v_norm_g": v_norm_g, "v_norm_b": v_norm_b,
        "w_spatial": w_spatial[0], "b_spatial": b_spatial[0], "b_gates": b_gates, "norm_mlp_g": norm_mlp_g,
        "norm_final_g": norm_final_g.reshape(1, D_MODEL),
    }

    c_idx = ci.astype(jnp.int32).reshape(1)
    place = jnp.stack([me_chip, ci.astype(jnp.int32)])
    partials = {}

    def reduced_shard(tag, got_chips):
        own = _rs_add_chips(*partials[tag], got_chips, place, name="rs_add_chips_" + tag)
        both = _rs_join_halves(own, name="rs_join_" + tag)
        return both.reshape(2 * both.shape[1], both.shape[2])

    def late_grads(dw):
        def by_rows(a):
            return a.reshape(N_CHIPS, a.shape[0] // N_CHIPS, a.shape[1])

        g = jnp.concatenate([by_rows(dw["pa"]), by_rows(dw["pb"]), by_rows(dw["out"]), _to_chip_columns(dw["up"]),
                             by_rows(dw["down"])], axis=1)
        return g.reshape(N_CHIPS, 2, g.shape[1] // 2, g.shape[2])

    def late_partials(g, got_pair):
        partials["late"] = (g, got_pair)
        return _rs_add_pair(g, got_pair, c_idx, name="rs_add_pair_late")

    def in_partials(dw):
        g = _w_in_grad_by_chip(dw).reshape(N_CHIPS, 2, D_MODEL // 2, _IN_SHARD)
        got_pair = _rs_swap_halves(g, name="rs_swap_in")
        partials["in"] = (g, got_pair)
        return _rs_add_pair(g, got_pair, c_idx, name="rs_add_pair_in")

    loss_part, grad_x, dw, dsmall, got_late, got_in = _local_grads(
        x[0], loss_target[0], wts, small, fwd_ride=shard_late, late_weights=late_weights, swap_ride=late_grads,
        bwd_ride=late_partials, last_ride=in_partials)
    loss = lax.psum(loss_part[0, 0], ("x", "y", "c"))
    g_late = reduced_shard("late", got_late)
    g_in_shard = reduced_shard("in", got_in)

    small_shapes = {k: dsmall[k].shape for k in _SMALL}
    red = _unpack_small(_all_reduce_small(_pack_small(dsmall), name="all_reduce_small"), small_shapes)
    conv_cols = CONV_DIM // N_CHIPS
    red["conv_w"] = lax.dynamic_slice_in_dim(red["conv_w"], me_chip * conv_cols, conv_cols, axis=1)

    grads, deltas, new_m, new_v = {}, {}, {}, {}
    for k in ("w_in",) + _LATE:
        g2 = g_in_shard if k == "w_in" else g_late[_LATE_OFF[k]:_LATE_OFF[k] + _LATE_ROWS[k]]
        dlt, m2, v2 = _adamw(given[k][0], g2, given["m_" + k][0], given["v_" + k][0], name="adamw_" + k)
        grads[k], deltas[k], new_m[k], new_v[k] = g2, dlt, m2, v2
    adam_shapes = dict(small_shapes)
    adam_shapes["conv_w"] = (CONV_W, conv_cols)

    def small_pack_of(prefix):
        return _pack_small({k: given[prefix + k].reshape(adam_shapes[k]) for k in _SMALL})

    dlt_s, m_s, v_s = _adamw(small_pack_of(""), _pack_small(red), small_pack_of("m_"), small_pack_of("v_"),
                             name="adamw_small")
    for dst, packed in ((deltas, dlt_s), (new_m, m_s), (new_v, v_s)):
        dst.update(_unpack_small(packed, adam_shapes))
    grads.update(red)

    def shaped(dct):
        return [dct[k].reshape(given[k].shape) for k in names]

    return (loss, grad_x[None], *shaped(grads), *shaped(deltas), *shaped(new_m), *shaped(new_v))
```

```python
import functools
import math

import jax
import jax.numpy as jnp
from jax import lax
from jax.experimental import pallas as pl
from jax.experimental.pallas import tpu as pltpu

f32 = jnp.float32
bf16 = jnp.bfloat16

D_MODEL = 1024
CHUNK = 128
GMLP_WIDTH = 1024
GMLP_GROUPS = 8
D_INNER = 2048
HEAD_DIM = 64
N_HEADS = 32
N_GROUPS = 8
HEADS_PER_GROUP = 4
GROUP_W = HEADS_PER_GROUP * HEAD_DIM
D_STATE = 128
CONV_W = 4
CONV_DIM = 4096
D_FF = 4096
IN_PROJ = 10272
NORM_EPS = 1e-6
N_CHIPS = 4
N_DEV = 8
LANES = 128

ADAM_LR = 0.001
ADAM_B1 = 0.9
ADAM_B2 = 0.999
ADAM_EPS = 1e-08
ADAM_WD = 0.01
ADAM_STEP = 10

MESH = pl.DeviceIdType.MESH
_NT = (((1,), (1,)), ((), ()))
_NN = (((1,), (0,)), ((), ()))
_TN = (((0,), (0,)), ((), ()))
_MB = 2 ** 20


def _params(sem, vmem_mb=48):
    return pltpu.CompilerParams(dimension_semantics=sem, vmem_limit_bytes=vmem_mb * _MB)


def _dot(a, b, dims=_NN):
    return lax.dot_general(a.astype(bf16), b.astype(bf16), dims, preferred_element_type=f32)


def _dot32(a, b):
    return jnp.dot(a, b, preferred_element_type=f32, precision=lax.Precision.HIGHEST)


def _sigmoid(x):
    return 1.0 / (1.0 + jnp.exp(-x))


def _sum_all(a):
    return jnp.sum(jnp.sum(a, axis=1, keepdims=True), axis=0, keepdims=True)


def _iota(shape, dim):
    return lax.broadcasted_iota(jnp.int32, shape, dim)


def _matmul(a, b, *, nt=False, tm, tn, tk, out_dtypes, epilogue=None, extras=(), extra_specs=None, name):
    m, k_dim = a.shape
    n = b.shape[0] if nt else b.shape[1]
    nk = k_dim // tk
    ne, no = len(extras), len(out_dtypes)
    dims = _NT if nt else _NN

    def body(*refs):
        a_ref, b_ref = refs[0], refs[1]
        ex = refs[2:2 + ne]
        outs = refs[2 + ne:2 + ne + no]

        def finish(acc):
            vals = epilogue(acc, *[e[...] for e in ex]) if epilogue is not None else (acc,)
            for o, v in zip(outs, vals):
                o[...] = v.astype(o.dtype)

        part = lax.dot_general(a_ref[...], b_ref[...], dims, preferred_element_type=f32)
        if nk == 1:
            finish(part)
        else:
            acc_ref = refs[-1]
            kk = pl.program_id(2)

            @pl.when(kk == 0)
            def _():
                acc_ref[...] = part

            @pl.when(kk > 0)
            def _():
                acc_ref[...] += part

            @pl.when(kk == nk - 1)
            def _():
                finish(acc_ref[...])

    b_spec = pl.BlockSpec((tn, tk), lambda i, j, k: (j, k)) if nt else pl.BlockSpec((tk, tn), lambda i, j, k: (k, j))
    tile = pl.BlockSpec((tm, tn), lambda i, j, k: (i, j))
    ex_specs = [tile if s is None else s for s in (extra_specs or [None] * ne)]
    outs = pl.pallas_call(
        body, name=name, grid=(m // tm, n // tn, nk),
        in_specs=[pl.BlockSpec((tm, tk), lambda i, j, k: (i, k)), b_spec] + ex_specs,
        out_specs=[tile] * no,
        out_shape=[jax.ShapeDtypeStruct((m, n), dt) for dt in out_dtypes],
        scratch_shapes=[pltpu.VMEM((tm, tn), f32)] if nk > 1 else [],
        compiler_params=_params(("parallel", "parallel", "arbitrary")),
    )(a, b, *extras)
    return outs if no > 1 else outs[0]


def _matmul_nt_sum(pairs, *, tm, tks, ride=None, name):
    m = pairs[0][0].shape[0]
    n = pairs[0][1].shape[0]
    nblk = [a.shape[1] // tk for (a, _), tk in zip(pairs, tks)]
    starts = [sum(nblk[:p]) for p in range(len(pairs))]
    nk = sum(nblk)
    npairs = len(pairs)
    ni = m // tm
    riding = ride is not None

    def body(*refs):
        rest = refs[2 * npairs:]
        if riding:
            ride_ref, o_ref, got_ref, acc_ref, send_sems, recv_sems = rest
        else:
            o_ref, acc_ref = rest
        i, kk = pl.program_id(0), pl.program_id(1)
        if riding:
            start, finish = _scatter_protocol(ride_ref, got_ref, send_sems, recv_sems)
            pl.when((i == 0) & (kk == 0))(start)

        @pl.when(kk == 0)
        def _():
            acc_ref[...] = jnp.zeros_like(acc_ref)

        for p in range(npairs):
            @pl.when((kk >= starts[p]) & (kk < starts[p] + nblk[p]))
            def _(p=p):
                acc_ref[...] += lax.dot_general(refs[2 * p][...], refs[2 * p + 1][...], _NT, preferred_element_type=f32)

        @pl.when(kk == nk - 1)
        def _():
            o_ref[...] = acc_ref[...]

        if riding:
            pl.when((i == ni - 1) & (kk == nk - 1))(finish)

    in_specs, args = [], []
    for p, (a, b) in enumerate(pairs):
        def kblock(k, s=starts[p], nb=nblk[p]):
            return jnp.clip(k - s, 0, nb - 1)
        in_specs.append(pl.BlockSpec((tm, tks[p]), lambda i, k, kb=kblock: (i, kb(k))))
        in_specs.append(pl.BlockSpec((n, tks[p]), lambda i, k, kb=kblock: (0, kb(k))))
        args += [a, b]
    tile = pl.BlockSpec((tm, n), lambda i, k: (i, 0))
    outs = pl.pallas_call(
        body, name=name, grid=(ni, nk), in_specs=in_specs + [_ANY] * riding, out_specs=[tile] + [_ANY] * riding,
        out_shape=[jax.ShapeDtypeStruct((m, n), f32)]
        + ([jax.ShapeDtypeStruct((N_CHIPS - 1,) + ride.shape[1:], ride.dtype)] if riding else []),
        scratch_shapes=[pltpu.VMEM((tm, n), f32)] + (list(_SCATTER_SCRATCH) if riding else []),
        compiler_params=_params(("arbitrary", "arbitrary"), vmem_mb=56),
    )(*args, *([ride] if riding else []))
    return outs if riding else outs[0]


def _matmul_tn(a, b, *, tka, tn, tt, name):
    t, ka = a.shape
    n = b.shape[1]

    def body(a_ref, b_ref, o_ref):
        part = lax.dot_general(a_ref[...], b_ref[...], _TN, preferred_element_type=f32)
        kk = pl.program_id(2)

        @pl.when(kk == 0)
        def _():
            o_ref[...] = part

        @pl.when(kk > 0)
        def _():
            o_ref[...] += part

    return pl.pallas_call(
        body, name=name, grid=(ka // tka, n // tn, t // tt),
        in_specs=[pl.BlockSpec((tt, tka), lambda i, j, k: (k, i)), pl.BlockSpec((tt, tn), lambda i, j, k: (k, j))],
        out_specs=pl.BlockSpec((tka, tn), lambda i, j, k: (i, j)),
        out_shape=jax.ShapeDtypeStruct((ka, n), f32),
        compiler_params=_params(("parallel", "parallel", "arbitrary")),
    )(a, b)


def _row_tile(t):
    return min(t, 512)


def _rms_fwd(x, g, *, name):
    t, d = x.shape
    tr = _row_tile(t)

    def body(x_ref, g_ref, h_ref):
        xv = x_ref[...]
        r = lax.rsqrt(jnp.mean(xv * xv, axis=1, keepdims=True) + NORM_EPS)
        h_ref[...] = (xv * r * g_ref[...]).astype(bf16)

    return pl.pallas_call(
        body, name=name, grid=(t // tr,),
        in_specs=[pl.BlockSpec((tr, d), lambda i: (i, 0)), pl.BlockSpec((1, d), lambda i: (0, 0))],
        out_specs=pl.BlockSpec((tr, d), lambda i: (i, 0)),
        out_shape=jax.ShapeDtypeStruct((t, d), bf16),
        compiler_params=_params(("parallel",)),
    )(x, g)


def _rms_bwd(xin, g, dh, dres, *, want_bf16, name):
    t, d = xin.shape
    tr = _row_tile(t)

    def body(x_ref, g_ref, dh_ref, dres_ref, dx_ref, *rest):
        dg_ref = rest[-1]
        xv = x_ref[...]
        r = lax.rsqrt(jnp.mean(xv * xv, axis=1, keepdims=True) + NORM_EPS)
        xn = xv * r
        dhv = dh_ref[...]
        dxn = dhv * g_ref[...]
        dx = dres_ref[...] + r * (dxn - xn * jnp.mean(dxn * xn, axis=1, keepdims=True))
        dx_ref[...] = dx
        if want_bf16:
            rest[0][...] = dx.astype(bf16)
        part = jnp.sum(dhv * xn, axis=0, keepdims=True)

        @pl.when(pl.program_id(0) == 0)
        def _():
            dg_ref[...] = part

        @pl.when(pl.program_id(0) > 0)
        def _():
            dg_ref[...] += part

    row = pl.BlockSpec((tr, d), lambda i: (i, 0))
    vec = pl.BlockSpec((1, d), lambda i: (0, 0))
    out_shape = [jax.ShapeDtypeStruct((t, d), f32)] + ([jax.ShapeDtypeStruct((t, d), bf16)] if want_bf16 else []) \
        + [jax.ShapeDtypeStruct((1, d), f32)]
    return pl.pallas_call(
        body, name=name, grid=(t // tr,),
        in_specs=[row, vec, row, row],
        out_specs=[row] + ([row] if want_bf16 else []) + [vec],
        out_shape=out_shape,
        compiler_params=_params(("arbitrary",)),
    )(xin, g, dh, dres)


def _loss_head(x2, tgt, g, *, name):
    t, d = x2.shape
    tr = _row_tile(t)

    def body(x_ref, t_ref, g_ref, dx_ref, dxb_ref, dg_ref, loss_ref):
        xv = x_ref[...]
        gv = g_ref[...]
        r = lax.rsqrt(jnp.mean(xv * xv, axis=1, keepdims=True) + NORM_EPS)
        xn = xv * r
        e = xn * gv - t_ref[...]
        lpart = jnp.zeros((1, LANES), f32) + 0.5 * _sum_all(jnp.mean(e * e, axis=1, keepdims=True))
        dy = e * (1.0 / d)
        dxn = dy * gv
        dx = r * (dxn - xn * jnp.mean(dxn * xn, axis=1, keepdims=True))
        dx_ref[...] = dx
        dxb_ref[...] = dx.astype(bf16)
        gpart = jnp.sum(dy * xn, axis=0, keepdims=True)

        @pl.when(pl.program_id(0) == 0)
        def _():
            dg_ref[...] = gpart
            loss_ref[...] = lpart

        @pl.when(pl.program_id(0) > 0)
        def _():
            dg_ref[...] += gpart
            loss_ref[...] += lpart

    row = pl.BlockSpec((tr, d), lambda i: (i, 0))
    vec = pl.BlockSpec((1, d), lambda i: (0, 0))
    return pl.pallas_call(
        body, name=name, grid=(t // tr,),
        in_specs=[row, row, vec],
        out_specs=[row, row, vec, pl.BlockSpec((1, LANES), lambda i: (0, 0))],
        out_shape=[jax.ShapeDtypeStruct((t, d), f32), jax.ShapeDtypeStruct((t, d), bf16),
                   jax.ShapeDtypeStruct((1, d), f32), jax.ShapeDtypeStruct((1, LANES), f32)],
        compiler_params=_params(("arbitrary",)),
    )(x2, tgt, g)


def _merge_bwd(dm, pa, pb, gl, bg, *, name):
    t, d = pa.shape
    tr = _row_tile(t)

    def body(dm_ref, pa_ref, pb_ref, gla_ref, glb_ref, bga_ref, bgb_ref, dpa_ref, dpb_ref, dgl_ref, dbg_ref):
        dmv = dm_ref[...]
        ga = _sigmoid(gla_ref[...] + bga_ref[...])
        gb = _sigmoid(glb_ref[...] + bgb_ref[...])
        dpa_ref[...] = (dmv * ga).astype(bf16)
        dpb_ref[...] = (dmv * gb).astype(bf16)
        dla = dmv * pa_ref[...] * ga * (1.0 - ga)
        dlb = dmv * pb_ref[...] * gb * (1.0 - gb)
        dgl_ref[:, :d] = dla.astype(bf16)
        dgl_ref[:, d:] = dlb.astype(bf16)
        sa = jnp.sum(dla, axis=0, keepdims=True)
        sb = jnp.sum(dlb, axis=0, keepdims=True)

        @pl.when(pl.program_id(0) == 0)
        def _():
            dbg_ref[:, :d] = sa
            dbg_ref[:, d:] = sb

        @pl.when(pl.program_id(0) > 0)
        def _():
            dbg_ref[:, :d] += sa
            dbg_ref[:, d:] += sb

    row = pl.BlockSpec((tr, d), lambda i: (i, 0))
    return pl.pallas_call(
        body, name=name, grid=(t // tr,),
        in_specs=[row, row, row, row, pl.BlockSpec((tr, d), lambda i: (i, 1)),
                  pl.BlockSpec((1, d), lambda i: (0, 0)), pl.BlockSpec((1, d), lambda i: (0, 1))],
        out_specs=[row, row, pl.BlockSpec((tr, 2 * d), lambda i: (i, 0)), pl.BlockSpec((1, 2 * d), lambda i: (0, 0))],
        out_shape=[jax.ShapeDtypeStruct((t, d), bf16), jax.ShapeDtypeStruct((t, d), bf16),
                   jax.ShapeDtypeStruct((t, 2 * d), bf16), jax.ShapeDtypeStruct((1, 2 * d), f32)],
        compiler_params=_params(("arbitrary",)),
    )(dm, pa, pb, gl, gl, bg, bg)


_INV_SQRT2 = 1.0 / math.sqrt(2.0)
_INV_SQRT2PI = 1.0 / math.sqrt(2.0 * math.pi)


def _gmlp_common(uv, vg, vb, with_grad=False):
    cdf = 0.5 * (1.0 + lax.erf(uv * _INV_SQRT2))
    zz = uv * cdf
    u, vhat, rstd, vn = _gmlp_norm(zz, vg, vb)
    if not with_grad:
        return u, vhat, rstd, vn
    return u, vhat, rstd, vn, cdf + uv * jnp.exp(-0.5 * uv * uv) * _INV_SQRT2PI


def _gmlp_norm(zz, vg, vb):
    u = zz[:, :GMLP_WIDTH]
    v = zz[:, GMLP_WIDTH:]
    mu = jnp.mean(v, axis=1, keepdims=True)
    vc = v - mu
    rstd = lax.rsqrt(jnp.mean(vc * vc, axis=1, keepdims=True) + NORM_EPS)
    vhat = vc * rstd
    vn = vhat * vg + vb
    return u, vhat, rstd, vn


def _gmlp_fwd(uv, vg, vb, wsp, bsp_t, *, name):
    t = uv.shape[0]
    per_step = 4 if t % (4 * CHUNK) == 0 else 1
    rows = per_step * CHUNK

    def body(uv_ref, vg_ref, vb_ref, w_ref, b_ref, y_ref):
        tril = _iota((CHUNK, CHUNK), 0) >= _iota((CHUNK, CHUNK), 1)
        bt = b_ref[...]
        for q in range(per_step):
            qs = slice(q * CHUNK, (q + 1) * CHUNK)
            u, _, _, vn = _gmlp_common(uv_ref[qs, :], vg_ref[...], vb_ref[...])
            for g in range(GMLP_GROUPS):
                sl = slice(g * CHUNK, (g + 1) * CHUNK)
                w = jnp.where(tril, w_ref[g], 0.0)
                s = _dot(w, vn[:, sl]) + bt[:, g:g + 1]
                y_ref[qs, sl] = (u[:, sl] * s).astype(bf16)

    return pl.pallas_call(
        body, name=name, grid=(t // rows,),
        in_specs=[pl.BlockSpec((rows, 2 * GMLP_WIDTH), lambda c: (c, 0)),
                  pl.BlockSpec((1, GMLP_WIDTH), lambda c: (0, 0)), pl.BlockSpec((1, GMLP_WIDTH), lambda c: (0, 0)),
                  pl.BlockSpec((GMLP_GROUPS, CHUNK, CHUNK), lambda c: (0, 0, 0)),
                  pl.BlockSpec((CHUNK, LANES), lambda c: (0, 0))],
        out_specs=pl.BlockSpec((rows, GMLP_WIDTH), lambda c: (c, 0)),
        out_shape=jax.ShapeDtypeStruct((t, GMLP_WIDTH), bf16),
        compiler_params=_params(("parallel",)),
    )(uv, vg, vb, wsp, bsp_t)


def _gmlp_bwd(uv, dya, vg, vb, wsp, bsp_t, *, ride=None, name):
    t = uv.shape[0]
    per_step = 4 if t % (4 * CHUNK) == 0 else 1
    rows = per_step * CHUNK
    steps = t // rows
    riding = ride is not None

    def body(*refs):
        uv_ref, dy_ref, vg_ref, vb_ref, w_ref, b_ref = refs[:6]
        duv_ref, dw_ref, db_ref, dvg_ref, dvb_ref = refs[6 + riding:11 + riding]
        first = pl.program_id(0) == 0
        if riding:
            start, finish = _swap_protocol(refs[6], refs[12], refs[13], refs[14])
            pl.when(first)(start)

        @pl.when(first)
        def _():
            dw_ref[...] = jnp.zeros_like(dw_ref)
            db_ref[...] = jnp.zeros_like(db_ref)
            dvg_ref[...] = jnp.zeros_like(dvg_ref)
            dvb_ref[...] = jnp.zeros_like(dvb_ref)

        vgv = vg_ref[...]
        tril = _iota((CHUNK, CHUNK), 0) >= _iota((CHUNK, CHUNK), 1)
        lane = _iota((CHUNK, LANES), 1)
        bt = b_ref[...]
        for q in range(per_step):
            qs = slice(q * CHUNK, (q + 1) * CHUNK)
            u, vhat, rstd, vn, gelu_grad = _gmlp_common(uv_ref[qs, :], vgv, vb_ref[...], with_grad=True)
            dy = dy_ref[qs, :]
            ds_all = dy * u
            dbacc = jnp.zeros((CHUNK, LANES), f32)
            dvh_parts = []
            for g in range(GMLP_GROUPS):
                sl = slice(g * CHUNK, (g + 1) * CHUNK)
                w = jnp.where(tril, w_ref[g], 0.0)
                vng = vn[:, sl]
                s = _dot(w, vng) + bt[:, g:g + 1]
                ds = ds_all[:, sl]
                duv_ref[qs, sl] = (dy[:, sl] * s * gelu_grad[:, sl]).astype(bf16)
                dw_ref[g] += jnp.where(tril, _dot(ds, vng, _NT), 0.0)
                dbacc = dbacc + jnp.where(lane == g, jnp.sum(ds, axis=1, keepdims=True), 0.0)
                dvn = _dot(w, ds, _TN)
                vh = vhat[:, sl]
                dvg_ref[:, sl] += jnp.sum(dvn * vh, axis=0, keepdims=True)
                dvb_ref[:, sl] += jnp.sum(dvn, axis=0, keepdims=True)
                dvh_parts.append(dvn * vgv[:, sl])
            db_ref[...] += dbacc
            dvhat = jnp.concatenate(dvh_parts, axis=1)
            m1 = jnp.mean(dvhat, axis=1, keepdims=True)
            m2 = jnp.mean(dvhat * vhat, axis=1, keepdims=True)
            dv = rstd * (dvhat - m1 - vhat * m2)
            duv_ref[qs, GMLP_WIDTH:] = (dv * gelu_grad[:, GMLP_WIDTH:]).astype(bf16)
        if riding:
            pl.when(pl.program_id(0) == steps - 1)(finish)

    vec = pl.BlockSpec((1, GMLP_WIDTH), lambda c: (0, 0))
    return pl.pallas_call(
        body, name=name, grid=(steps,),
        in_specs=[pl.BlockSpec((rows, 2 * GMLP_WIDTH), lambda c: (c, 0)),
                  pl.BlockSpec((rows, GMLP_WIDTH), lambda c: (c, 0)), vec, vec,
                  pl.BlockSpec((GMLP_GROUPS, CHUNK, CHUNK), lambda c: (0, 0, 0)),
                  pl.BlockSpec((CHUNK, LANES), lambda c: (0, 0))] + [_ANY] * riding,
        out_specs=[pl.BlockSpec((rows, 2 * GMLP_WIDTH), lambda c: (c, 0)),
                   pl.BlockSpec((GMLP_GROUPS, CHUNK, CHUNK), lambda c: (0, 0, 0)),
                   pl.BlockSpec((CHUNK, LANES), lambda c: (0, 0)), vec, vec] + [_ANY] * riding,
        out_shape=[jax.ShapeDtypeStruct((t, 2 * GMLP_WIDTH), bf16),
                   jax.ShapeDtypeStruct((GMLP_GROUPS, CHUNK, CHUNK), f32),
                   jax.ShapeDtypeStruct((CHUNK, LANES), f32),
                   jax.ShapeDtypeStruct((1, GMLP_WIDTH), f32), jax.ShapeDtypeStruct((1, GMLP_WIDTH), f32)]
        + ([jax.ShapeDtypeStruct(ride.shape[:1] + ride.shape[2:], ride.dtype)] if riding else []),
        scratch_shapes=list(_SWAP_SCRATCH) if riding else [],
        compiler_params=_params(("arbitrary",)),
    )(uv, dya, vg, vb, wsp, bsp_t, *([ride] if riding else []))


_CONV_COLS = 512
_XS0, _B0, _C0 = 0, D_INNER, D_INNER + N_GROUPS * D_STATE


_TAIL = 8


def _conv_silu(cur_ref, tail_ref, w_ref, b_ref, has_prev, xc_ref, cv_ref):
    row = _iota((_TAIL, _CONV_COLS), 0)
    for j in range(CONV_DIM // _CONV_COLS):
        sl = slice(j * _CONV_COLS, (j + 1) * _CONV_COLS)
        cur = cur_ref[:, sl]
        tail = jnp.where(has_prev, tail_ref[:, sl], 0.0)
        acc = cur * w_ref[CONV_W - 1:CONV_W, sl] + b_ref[:, sl]
        for s in range(1, CONV_W):
            rolled = pltpu.roll(cur, s, 0)
            top = jnp.where(row >= s, rolled[:_TAIL], pltpu.roll(tail, s, 0))
            sh = jnp.concatenate([top, rolled[_TAIL:]], axis=0)
            acc = acc + sh * w_ref[CONV_W - 1 - s:CONV_W - s, sl]
        cv_ref[:, sl] = acc
        xc_ref[:, sl] = acc * _sigmoid(acc)


def _col_bcast(mat, h):
    return jnp.broadcast_to(mat[:, h:h + 1], (CHUNK, LANES))


def _head_expand(cols):
    lo = _iota((CHUNK, LANES), 1) < HEAD_DIM
    return jnp.concatenate([jnp.where(lo, cols[2 * j], cols[2 * j + 1]) for j in range(N_HEADS // 2)], axis=1)


def _ssd_chunk_scalars(dtr, dtb, alog):
    xdt_pre = dtr + dtb
    dtv = jnp.maximum(xdt_pre, 0.0) + jnp.log(1.0 + jnp.exp(-jnp.abs(xdt_pre)))
    a = -jnp.exp(alog)
    ltri = (_iota((CHUNK, CHUNK), 0) >= _iota((CHUNK, CHUNK), 1)).astype(f32)
    cs = _dot32(ltri, dtv * a)
    csb = [_col_bcast(cs, h) for h in range(N_HEADS)]
    cs_x = _head_expand(csb)
    dt_x = _head_expand([_col_bcast(dtv, h) for h in range(N_HEADS)])
    cl_x = cs_x[CHUNK - 1:CHUNK, :]
    return dict(xdt_pre=xdt_pre, dtv=dtv, a=a, cs=cs, cs_t=cs.T, csb=csb, dt_x=dt_x, e_x=jnp.exp(cs_x),
                dec_x=jnp.exp(cl_x - cs_x), dk_x=jnp.exp(cl_x))


def _head_masks():
    lane = _iota((CHUNK, GROUP_W), 1)
    return [(lane >= r * HEAD_DIM) & (lane < (r + 1) * HEAD_DIM) for r in range(HEADS_PER_GROUP)]


def _stack_heads(a, masks):
    return jnp.concatenate([jnp.where(m, a, 0.0) for m in masks], axis=0).astype(bf16)


def _seg_sum(a, seg):
    hi = a.astype(jnp.bfloat16)
    lo = (a - hi.astype(f32)).astype(jnp.bfloat16)
    return (lax.dot_general(hi, seg, _NN, preferred_element_type=f32)
            + lax.dot_general(lo, seg, _NN, preferred_element_type=f32))


def _head_seg_matrix():
    return (_iota((D_INNER, LANES), 0) // HEAD_DIM == _iota((D_INNER, LANES), 1)).astype(jnp.bfloat16)


def _ssd_fwd(xbc, z, dtr, cw, cb, dtb, alog, dsk_x, gs, *, ride=None, name):
    t = xbc.shape[0]
    nc = t // CHUNK
    tiles = CHUNK // _TAIL

    def body(*refs):
        cur_ref, tail_ref, z_ref, dtr_ref, cw_ref, cb_ref, dtb_ref, alog_ref, dsk_ref, gs_ref = refs[:10]
        if ride is None:
            yb_ref, hp_ref, cv_ref, state_ref, xc_ref = refs[10:]
        else:
            ride_ref, yb_ref, hp_ref, cv_ref, got_ref, state_ref, xc_ref, send_sems, recv_sems = refs[10:]
        c = pl.program_id(0)
        if ride is not None:
            start, relay, finish = _gather_protocol(ride_ref, got_ref, send_sems, recv_sems)
            pl.when(c == 0)(start)
            pl.when(c == nc // 2)(relay)

        @pl.when(c == 0)
        def _():
            state_ref[...] = jnp.zeros_like(state_ref)

        _conv_silu(cur_ref, tail_ref, cw_ref, cb_ref, c > 0, xc_ref, cv_ref)
        sc = _ssd_chunk_scalars(dtr_ref[...], dtb_ref[...], alog_ref[...])
        tril = _iota((CHUNK, CHUNK), 0) >= _iota((CHUNK, CHUNK), 1)
        masks = _head_masks()
        hp_ref[0] = state_ref[...]
        for g in range(N_GROUPS):
            gsl = slice(g * GROUP_W, (g + 1) * GROUP_W)
            xs_g = xc_ref[:, gsl]
            bg = xc_ref[:, _B0 + g * D_STATE:_B0 + (g + 1) * D_STATE]
            cg = xc_ref[:, _C0 + g * D_STATE:_C0 + (g + 1) * D_STATE]
            xdt_g = xs_g * sc["dt_x"][:, gsl]
            cbm = _dot(cg, bg, _NT)
            mw = jnp.concatenate(
                [cbm * jnp.exp(jnp.where(tril, sc["csb"][h] - sc["cs_t"][h:h + 1, :], -1e30))
                 for h in range(g * HEADS_PER_GROUP, (g + 1) * HEADS_PER_GROUP)], axis=1)
            ht_g = state_ref[:, gsl]
            y_g = _dot(mw, _stack_heads(xdt_g, masks)) + sc["e_x"][:, gsl] * _dot(cg, ht_g) + dsk_ref[:, gsl] * xs_g
            state_ref[:, gsl] = ht_g * sc["dk_x"][:, gsl] + _dot(bg, xdt_g * sc["dec_x"][:, gsl], _TN)
            zg = z_ref[:, gsl]
            yg = y_g * zg * _sigmoid(zg)
            rs = lax.rsqrt(jnp.mean(yg * yg, axis=1, keepdims=True) + NORM_EPS)
            yb_ref[:, gsl] = (yg * rs * gs_ref[:, gsl]).astype(bf16)
        if ride is not None:
            pl.when(c == nc - 1)(finish)

    def chunk(w):
        return pl.BlockSpec((CHUNK, w), lambda c: (c, 0))

    def const(shape):
        return pl.BlockSpec(shape, lambda c: (0,) * len(shape))

    riding = ride is not None
    return pl.pallas_call(
        body, name=name, grid=(nc,),
        in_specs=[chunk(CONV_DIM), pl.BlockSpec((_TAIL, CONV_DIM), lambda c: (jnp.maximum(c * tiles - 1, 0), 0)),
                  chunk(D_INNER), chunk(LANES), const((CONV_W, CONV_DIM)), const((1, CONV_DIM)),
                  const((1, LANES)), const((1, LANES)), const((1, D_INNER)), const((1, D_INNER))] + [_ANY] * riding,
        out_specs=[chunk(D_INNER), pl.BlockSpec((1, D_STATE, D_INNER), lambda c: (c, 0, 0)), chunk(CONV_DIM)]
        + [_ANY] * riding,
        out_shape=[jax.ShapeDtypeStruct((t, D_INNER), bf16), jax.ShapeDtypeStruct((nc, D_STATE, D_INNER), f32),
                   jax.ShapeDtypeStruct((t, CONV_DIM), f32)]
        + ([jax.ShapeDtypeStruct((N_CHIPS,) + ride.shape, ride.dtype)] if riding else []),
        scratch_shapes=[pltpu.VMEM((D_STATE, D_INNER), f32), pltpu.VMEM((CHUNK, CONV_DIM), f32)]
        + (list(_GATHER_SCRATCH) if riding else []),
        compiler_params=_params(("arbitrary",)),
    )(xbc, xbc, z, dtr, cw, cb, dtb, alog, dsk_x, gs, *([ride] if riding else []))


def _ssd_bwd(xbc, cv, z, dtr, hprev, dyb, cw, dtb, alog, dsk_x, gs, seg, *, ride=None, name):
    t = xbc.shape[0]
    nc = t // CHUNK

    def body(*refs):
        (cur_ref, cv_ref, z_ref, dtr_ref, hp_ref, dyb_ref, cw_ref, dtb_ref, alog_ref, dsk_ref, gs_ref,
         seg_ref) = refs[:12]
        rest = refs[12:]
        if ride is not None:
            ride_ref, got_ref, send_sems, recv_sems = rest[0], rest[10], rest[-2], rest[-1]
            rest = rest[1:10] + rest[11:-2]
        (dz_ref, dxbc_ref, ddt_ref, dcw_ref, dcb_ref, ddtb_ref, dalog_ref, ddsk_ref, dgs_ref,
         dh_ref, dcnext_ref, xc_ref, dxc_ref, x13_ref, x2_ref, rows_ref) = rest
        i = pl.program_id(0)
        if ride is not None:
            start, finish = _scatter_protocol(ride_ref, got_ref, send_sems, recv_sems)
            pl.when(i == 0)(start)

        @pl.when(i == 0)
        def _():
            for ref in (dh_ref, dcnext_ref, dcw_ref, dcb_ref, ddtb_ref, dalog_ref, ddsk_ref, dgs_ref, rows_ref):
                ref[...] = jnp.zeros_like(ref)

        for j in range(CONV_DIM // _CONV_COLS):
            sl = slice(j * _CONV_COLS, (j + 1) * _CONV_COLS)
            cvv = cv_ref[:, sl]
            xc_ref[:, sl] = cvv * _sigmoid(cvv)
        sc = _ssd_chunk_scalars(dtr_ref[...], dtb_ref[...], alog_ref[...])
        tril = _iota((CHUNK, CHUNK), 0) >= _iota((CHUNK, CHUNK), 1)
        triu = _iota((CHUNK, CHUNK), 0) <= _iota((CHUNK, CHUNK), 1)
        masks = _head_masks()
        rowh = _iota((N_HEADS, CHUNK), 0)
        dcs_t = jnp.zeros((N_HEADS, CHUNK), f32)
        for g in range(N_GROUPS):
            gsl = slice(g * GROUP_W, (g + 1) * GROUP_W)
            xs_g = xc_ref[:, gsl]
            bg = xc_ref[:, _B0 + g * D_STATE:_B0 + (g + 1) * D_STATE]
            cg = xc_ref[:, _C0 + g * D_STATE:_C0 + (g + 1) * D_STATE]
            dt_g, e_g, dec_g, dk_g = sc["dt_x"][:, gsl], sc["e_x"][:, gsl], sc["dec_x"][:, gsl], sc["dk_x"][:, gsl]
            dsk_g = dsk_ref[:, gsl]
            xdt_g = xs_g * dt_g
            xdt_stack = _stack_heads(xdt_g, masks)
            cbm = _dot(cg, bg, _NT)
            cbt = _dot(bg, cg, _NT)
            heads = range(g * HEADS_PER_GROUP, (g + 1) * HEADS_PER_GROUP)
            lmats = [jnp.exp(jnp.where(tril, sc["csb"][h] - sc["cs_t"][h:h + 1, :], -1e30)) for h in heads]
            mw = jnp.concatenate([cbm * lm for lm in lmats], axis=1)
            mtw = jnp.concatenate(
                [cbt * jnp.exp(jnp.where(triu, sc["cs_t"][h:h + 1, :] - sc["csb"][h], -1e30)) for h in heads], axis=1)
            ht_g = hp_ref[0, :, gsl]
            dhn_g = dh_ref[:, gsl]
            yoff = e_g * _dot(cg, ht_g)
            y_g = _dot(mw, xdt_stack) + yoff + dsk_g * xs_g
            zg = z_ref[:, gsl]
            sz = _sigmoid(zg)
            silu = zg * sz
            yg = y_g * silu
            rs = lax.rsqrt(jnp.mean(yg * yg, axis=1, keepdims=True) + NORM_EPS)
            yn = yg * rs
            dyb = dyb_ref[:, gsl]
            dgs_ref[:, gsl] += jnp.sum(dyb * yn, axis=0, keepdims=True)
            dyn = dyb * gs_ref[:, gsl]
            dyg = rs * (dyn - yn * jnp.mean(dyn * yn, axis=1, keepdims=True))
            dy_g = dyg * silu
            dz_ref[:, gsl] = (dyg * y_g * (sz * (1.0 + zg * (1.0 - sz)))).astype(bf16)
            dy_stack = _stack_heads(dy_g, masks)
            dm_w = _dot(dy_g, xdt_stack, _NT)
            dmt_w = _dot(xdt_g, dy_stack, _NT)
            dxdt = _dot(mtw, dy_stack)
            dcb_acc = jnp.zeros((CHUNK, CHUNK), f32)
            for r, h in enumerate(heads):
                hs = slice(r * CHUNK, (r + 1) * CHUNK)
                dml = dm_w[:, hs] * lmats[r]
                dcb_acc = dcb_acc + dml
                col = jnp.sum(dml * cbm, axis=0, keepdims=True)
                row = jnp.sum(dmt_w[:, hs] * mtw[:, hs], axis=0, keepdims=True)
                dcs_t = dcs_t + jnp.where(rowh == h, row - col, 0.0)
            w = _dot(bg, dhn_g)
            dxdt = dxdt + dec_g * w
            decx3 = dec_g * (xdt_g * w)
            dg_g = e_g * dy_g
            d_c = _dot(dg_g, ht_g, _NT) + _dot(dcb_acc, bg)
            d_b = _dot(dcb_acc, cg, _TN) + _dot(xdt_g * dec_g, dhn_g, _NT)
            dh_ref[:, gsl] = dhn_g * dk_g + _dot(cg, dg_g, _TN)
            dxc_ref[:, gsl] = dsk_g * dy_g + dxdt * dt_g
            dxc_ref[:, _B0 + g * D_STATE:_B0 + (g + 1) * D_STATE] = d_b
            dxc_ref[:, _C0 + g * D_STATE:_C0 + (g + 1) * D_STATE] = d_c
            x13_ref[:, gsl] = dy_g * yoff - decx3
            x2_ref[:, gsl] = dxdt * xs_g
            rows_ref[0:1, gsl] = jnp.sum(dhn_g * ht_g, axis=0, keepdims=True)
            rows_ref[1:2, gsl] = jnp.sum(decx3, axis=0, keepdims=True)
            rows_ref[2:3, gsl] = jnp.sum(dy_g * xs_g, axis=0, keepdims=True)
        segm = seg_ref[...]
        r13 = _seg_sum(x13_ref[...], segm)
        r2 = _seg_sum(x2_ref[...], segm)
        small = _seg_sum(rows_ref[...], segm)
        lane = _iota((CHUNK, LANES), 1)
        rowi = _iota((CHUNK, LANES), 0)
        dcl_row = small[0:1, :] * jnp.exp(sc["cs"][CHUNK - 1:CHUNK, :]) + small[1:2, :]
        dcs = r13 + jnp.where(rowi == CHUNK - 1, dcl_row, 0.0)
        dcs_t_all = dcs.T + jnp.concatenate([dcs_t, jnp.zeros((LANES - N_HEADS, CHUNK), f32)], axis=0)
        dda = _dot32(dcs_t_all, tril.astype(f32)).T
        a = sc["a"]
        ddt_total = r2 + dda * a
        dalog_ref[...] += jnp.sum(dda * sc["dtv"], axis=0, keepdims=True) * a
        ddtr = jnp.where(lane < N_HEADS, ddt_total * _sigmoid(sc["xdt_pre"]), 0.0)
        ddtb_ref[...] += jnp.sum(ddtr, axis=0, keepdims=True)
        ddt_ref[...] = ddtr.astype(bf16)
        ddsk_ref[...] += small[2:3, :]
        row8 = _iota((_TAIL, _CONV_COLS), 0)
        for j in range(CONV_DIM // _CONV_COLS):
            sl = slice(j * _CONV_COLS, (j + 1) * _CONV_COLS)
            cvv = cv_ref[:, sl]
            sg = _sigmoid(cvv)
            dconv = dxc_ref[:, sl] * (sg * (1.0 + cvv * (1.0 - sg)))
            nxt = dcnext_ref[:, sl]
            cur = cur_ref[:, sl]
            dxin = dconv * cw_ref[CONV_W - 1:CONV_W, sl]
            dcw_ref[CONV_W - 1:CONV_W, sl] += jnp.sum(dconv * cur, axis=0, keepdims=True)
            for s in range(1, CONV_W):
                rolled = pltpu.roll(dconv, CHUNK - s, 0)
                bot = jnp.where(row8 < _TAIL - s, rolled[CHUNK - _TAIL:], pltpu.roll(nxt, _TAIL - s, 0))
                up = jnp.concatenate([rolled[:CHUNK - _TAIL], bot], axis=0)
                dxin = dxin + up * cw_ref[CONV_W - 1 - s:CONV_W - s, sl]
                dcw_ref[CONV_W - 1 - s:CONV_W - s, sl] += jnp.sum(up * cur, axis=0, keepdims=True)
            dcb_ref[:, sl] += jnp.sum(dconv, axis=0, keepdims=True)
            dxbc_ref[:, sl] = dxin.astype(bf16)
            dcnext_ref[:, sl] = dconv[:_TAIL]
        if ride is not None:
            pl.when(i == nc - 1)(finish)

    def chunk(w):
        return pl.BlockSpec((CHUNK, w), lambda i: (nc - 1 - i, 0))

    def const(shape):
        return pl.BlockSpec(shape, lambda i: (0,) * len(shape))

    riding = ride is not None
    return pl.pallas_call(
        body, name=name, grid=(nc,),
        in_specs=[chunk(CONV_DIM), chunk(CONV_DIM),
                  chunk(D_INNER), chunk(LANES), pl.BlockSpec((1, D_STATE, D_INNER), lambda i: (nc - 1 - i, 0, 0)),
                  chunk(D_INNER), const((CONV_W, CONV_DIM)),
                  const((1, LANES)), const((1, LANES)), const((1, D_INNER)), const((1, D_INNER)),
                  const((D_INNER, LANES))] + [_ANY] * riding,
        out_specs=[chunk(D_INNER), chunk(CONV_DIM), chunk(LANES), const((CONV_W, CONV_DIM)), const((1, CONV_DIM)),
                   const((1, LANES)), const((1, LANES)), const((1, LANES)), const((1, D_INNER))] + [_ANY] * riding,
        out_shape=[jax.ShapeDtypeStruct((t, D_INNER), bf16), jax.ShapeDtypeStruct((t, CONV_DIM), bf16),
                   jax.ShapeDtypeStruct((t, LANES), bf16), jax.ShapeDtypeStruct((CONV_W, CONV_DIM), f32),
                   jax.ShapeDtypeStruct((1, CONV_DIM), f32), jax.ShapeDtypeStruct((1, LANES), f32),
                   jax.ShapeDtypeStruct((1, LANES), f32), jax.ShapeDtypeStruct((1, LANES), f32),
                   jax.ShapeDtypeStruct((1, D_INNER), f32)]
        + ([jax.ShapeDtypeStruct((N_CHIPS - 1,) + ride.shape[1:], ride.dtype)] if riding else []),
        scratch_shapes=[pltpu.VMEM((D_STATE, D_INNER), f32), pltpu.VMEM((_TAIL, CONV_DIM), f32),
                        pltpu.VMEM((CHUNK, CONV_DIM), f32), pltpu.VMEM((CHUNK, CONV_DIM), f32),
                        pltpu.VMEM((CHUNK, D_INNER), f32), pltpu.VMEM((CHUNK, D_INNER), f32),
                        pltpu.VMEM((_TAIL, D_INNER), f32)]
        + (list(_SCATTER_SCRATCH) if riding else []),
        compiler_params=_params(("arbitrary",)),
    )(xbc, cv, z, dtr, hprev, dyb, cw, dtb, alog, dsk_x, gs, seg, *([ride] if riding else []))


def _adamw(w, g, m, v, *, name):
    r, c = w.shape
    tr = r
    while tr * c * 4 > 2 * _MB and tr % 16 == 0:
        tr //= 2

    def body(w_ref, g_ref, m_ref, v_ref, d_ref, m2_ref, v2_ref):
        gv = g_ref[...]
        m2 = ADAM_B1 * m_ref[...] + (1.0 - ADAM_B1) * gv
        v2 = ADAM_B2 * v_ref[...] + (1.0 - ADAM_B2) * (gv * gv)
        m_hat = m2 / (1.0 - ADAM_B1 ** ADAM_STEP)
        v_hat = v2 / (1.0 - ADAM_B2 ** ADAM_STEP)
        d_ref[...] = -ADAM_LR * (m_hat / (jnp.sqrt(v_hat) + ADAM_EPS) + ADAM_WD * w_ref[...])
        m2_ref[...] = m2
        v2_ref[...] = v2

    blk = pl.BlockSpec((tr, c), lambda i: (i, 0))
    return pl.pallas_call(
        body, name=name, grid=(r // tr,),
        in_specs=[blk] * 4, out_specs=[blk] * 3,
        out_shape=[jax.ShapeDtypeStruct((r, c), f32)] * 3,
        compiler_params=_params(("parallel",)),
    )(w, g, m, v)


def _row_block(rows, cols):
    cap = max(16, 2 * _MB // (4 * cols))
    return max(tr for tr in range(16, min(cap, rows) + 1, 16) if rows % tr == 0)


def _cast_bf16(a, *, name):
    r, c = a.shape
    tr = _row_block(r, c)

    def body(a_ref, o_ref):
        o_ref[...] = a_ref[...].astype(bf16)

    blk = pl.BlockSpec((tr, c), lambda i: (i, 0))
    return pl.pallas_call(
        body, name=name, grid=(r // tr,), in_specs=[blk], out_specs=blk,
        out_shape=jax.ShapeDtypeStruct((r, c), bf16), compiler_params=_params(("parallel",)),
    )(a)


_ANY = pl.BlockSpec(memory_space=pl.ANY)


def _place():
    x, y, c = lax.axis_index("x"), lax.axis_index("y"), lax.axis_index("c")
    other_chips = [(1 - x, y), (x, 1 - y), (1 - x, 1 - y)]
    return x, y, c, other_chips


def _gather_protocol(in_ref, out_ref, send_sems, recv_sems):
    x, y, c, chips = _place()
    me = 2 * x + y
    sibling = (x, y, 1 - c)
    where = [2 * cx + cy for cx, cy in chips]

    def cp(k, chip, half, to, src=None):
        dst = out_ref.at[chip, half]
        return pltpu.make_async_remote_copy(
            src_ref=dst if src is None else src, dst_ref=dst, send_sem=send_sems.at[k], recv_sem=recv_sems.at[k],
            device_id=to, device_id_type=MESH)

    def sends():
        return [cp(j, me, c, (*chips[j], c), src=in_ref.at[c]) for j in range(2)]

    def relays():
        return [cp(3 + j, where[j], c, sibling) for j in range(3)]

    def landed(j):
        return cp(j, where[j], c, sibling)

    def start():
        for f in sends():
            f.start()

    def relay():
        onward = relays()
        for first in range(2):
            @pl.when(c == first)
            def _(first=first):
                landed(first).wait_recv()
                cp(2, where[first], c, (*chips[1 - first], c)).start()
                onward[first].start()
                landed(1 - first).wait_recv()
                onward[1 - first].start()

    def finish():
        landed(2).wait_recv()
        relays()[2].start()
        for j in range(3):
            cp(3 + j, where[j], 1 - c, sibling).wait_recv()
        for f in sends() + [landed(2)] + relays():
            f.wait_send()

    return start, relay, finish


_GATHER_SCRATCH = [pltpu.SemaphoreType.DMA((6,)), pltpu.SemaphoreType.DMA((6,))]


def _gather_shards(shard, *, name):
    _, rh, lanes = shard.shape

    def body(in_ref, out_ref, send_sems, recv_sems):
        start, relay, finish = _gather_protocol(in_ref, out_ref, send_sems, recv_sems)
        start()
        relay()
        finish()

    return pl.pallas_call(
        body, name=name, in_specs=[_ANY], out_specs=_ANY,
        out_shape=jax.ShapeDtypeStruct((N_CHIPS, 2, rh, lanes), shard.dtype),
        scratch_shapes=list(_GATHER_SCRATCH),
    )(shard)


def _scatter_protocol(p_ref, out_ref, send_sems, recv_sems):
    x, y, c, chips = _place()

    def copies():
        return [pltpu.make_async_remote_copy(
            src_ref=p_ref.at[2 * cx + cy], dst_ref=out_ref.at[j], send_sem=send_sems.at[j], recv_sem=recv_sems.at[j],
            device_id=(cx, cy, c), device_id_type=MESH) for j, (cx, cy) in enumerate(chips)]

    def start():
        for cpy in copies():
            cpy.start()

    def finish():
        for cpy in copies():
            cpy.wait()

    return start, finish


_SCATTER_SCRATCH = [pltpu.SemaphoreType.DMA((3,)), pltpu.SemaphoreType.DMA((3,))]


def _swap_protocol(g_ref, out_ref, send_sems, recv_sems):
    x, y, c, _ = _place()

    def copies():
        return [pltpu.make_async_remote_copy(
            src_ref=g_ref.at[k, 1 - c], dst_ref=out_ref.at[k], send_sem=send_sems.at[k], recv_sem=recv_sems.at[k],
            device_id=(x, y, 1 - c), device_id_type=MESH) for k in range(N_CHIPS)]

    def start():
        for cpy in copies():
            cpy.start()

    def finish():
        for cpy in copies():
            cpy.wait()

    return start, finish


_SWAP_SCRATCH = [pltpu.SemaphoreType.DMA((N_CHIPS,)), pltpu.SemaphoreType.DMA((N_CHIPS,))]


def _rs_swap_halves(g, *, name):
    nch, _, rh, lanes = g.shape

    def body(g_ref, out_ref, send_sems, recv_sems):
        start, finish = _swap_protocol(g_ref, out_ref, send_sems, recv_sems)
        start()
        finish()

    return pl.pallas_call(
        body, name=name, in_specs=[_ANY], out_specs=_ANY,
        out_shape=jax.ShapeDtypeStruct((nch, rh, lanes), g.dtype),
        scratch_shapes=list(_SWAP_SCRATCH),
    )(g)


def _rs_add_pair(g, got, c_idx, *, name):
    nch, _, rh, lanes = g.shape
    tr = _row_block(rh, lanes)

    def body(c_ref, g_ref, got_ref, p16_ref):
        p16_ref[...] = (g_ref[...] + got_ref[...]).astype(bf16)

    blk = pl.BlockSpec((None, tr, lanes), lambda k, i, c_ref: (k, i, 0))
    return pl.pallas_call(
        body, name=name,
        grid_spec=pltpu.PrefetchScalarGridSpec(
            num_scalar_prefetch=1, grid=(nch, rh // tr),
            in_specs=[pl.BlockSpec((None, None, tr, lanes), lambda k, i, c_ref: (k, c_ref[0], i, 0)), blk],
            out_specs=blk),
        out_shape=jax.ShapeDtypeStruct((nch, rh, lanes), bf16),
        compiler_params=_params(("parallel", "parallel")),
    )(c_idx, g, got)


def _rs_add_chips(g, got_pair, got, place, *, name):
    _, _, rh, lanes = g.shape
    tr = _row_block(rh, lanes)

    def body(place_ref, g_ref, pair_ref, got_ref, o_ref):
        own = g_ref[...] + pair_ref[...]
        o_ref[...] = ((own + got_ref[0].astype(f32)) + got_ref[1].astype(f32)) + got_ref[2].astype(f32)

    return pl.pallas_call(
        body, name=name,
        grid_spec=pltpu.PrefetchScalarGridSpec(
            num_scalar_prefetch=1, grid=(rh // tr,),
            in_specs=[pl.BlockSpec((None, None, tr, lanes), lambda i, place_ref: (place_ref[0], place_ref[1], i, 0)),
                      pl.BlockSpec((None, tr, lanes), lambda i, place_ref: (place_ref[0], i, 0)),
                      pl.BlockSpec((3, tr, lanes), lambda i, place_ref: (0, i, 0))],
            out_specs=pl.BlockSpec((None, tr, lanes), lambda i, place_ref: (place_ref[1], i, 0))),
        out_shape=jax.ShapeDtypeStruct((2, rh, lanes), f32),
        compiler_params=_params(("parallel",)),
    )(place, g, got_pair, got)


def _rs_join_halves(halves, *, name):
    def body(h_ref, out_ref, send_sem, recv_sem):
        x, y, c, _ = _place()
        cpy = pltpu.make_async_remote_copy(
            src_ref=h_ref.at[c], dst_ref=out_ref.at[c], send_sem=send_sem, recv_sem=recv_sem,
            device_id=(x, y, 1 - c), device_id_type=MESH)
        cpy.start()
        cpy.wait()

    return pl.pallas_call(
        body, name=name, in_specs=[_ANY], out_specs=_ANY,
        out_shape=jax.ShapeDtypeStruct(halves.shape, halves.dtype), input_output_aliases={0: 0},
        scratch_shapes=[pltpu.SemaphoreType.DMA, pltpu.SemaphoreType.DMA],
    )(halves)


def _all_reduce_small(s, *, name):
    rs, lanes = s.shape
    rh = rs // 2

    def body(s_ref, o_ref, sib_ref, mine_ref, chips_ref, send_sems, recv_sems):
        x, y, c, chips = _place()
        me = 2 * x + y
        sibling = (x, y, 1 - c)
        rows = pl.ds(pl.multiple_of(c * rh, 8), rh)

        def cp(k, src, dst, to):
            return pltpu.make_async_remote_copy(src_ref=src, dst_ref=dst, send_sem=send_sems.at[k],
                                                recv_sem=recv_sems.at[k], device_id=to, device_id_type=MESH)

        swap = cp(0, s_ref, sib_ref, sibling)
        swap.start()
        swap.wait()
        mine_ref[...] = s_ref[rows, :] + sib_ref[rows, :]
        sends = [cp(1 + j, mine_ref, chips_ref.at[j], (cx, cy, c)) for j, (cx, cy) in enumerate(chips)]
        for cpy in sends:
            cpy.start()
        for cpy in sends:
            cpy.wait()
        where = [2 * cx + cy for cx, cy in chips]
        total = None
        for q in range(N_CHIPS):
            term = jnp.where(q == me, mine_ref[...], jnp.where(
                q == where[0], chips_ref[0], jnp.where(q == where[1], chips_ref[1], chips_ref[2])))
            total = term if total is None else total + term
        o_ref[rows, :] = total
        push = cp(4, o_ref.at[rows, :], o_ref.at[rows, :], sibling)
        push.start()
        push.wait()

    vm = pl.BlockSpec(memory_space=pltpu.VMEM)
    return pl.pallas_call(
        body, name=name, in_specs=[vm], out_specs=vm,
        out_shape=jax.ShapeDtypeStruct((rs, lanes), f32),
        scratch_shapes=[pltpu.VMEM((rs, lanes), f32), pltpu.VMEM((rh, lanes), f32),
                        pltpu.VMEM((N_CHIPS - 1, rh, lanes), f32), pltpu.SemaphoreType.DMA((5,)),
                        pltpu.SemaphoreType.DMA((5,))],
        compiler_params=pltpu.CompilerParams(vmem_limit_bytes=32 * _MB),
    )(s)


def _pad_lanes(a, width=LANES):
    return jnp.pad(a, ((0, 0), (0, width - a.shape[1])))


def _local_grads(x, tgt, wts, small, *, fwd_ride=None, late_weights=None, swap_ride=None, bwd_ride=None,
                 last_ride=None):
    t = x.shape[0]
    tm = min(t, 1024)
    d = D_MODEL
    mm = functools.partial(_matmul, tm=tm)

    dtb = _pad_lanes(small["dt_bias"])
    alog = _pad_lanes(small["a_log"])
    dsk = jnp.repeat(small["d_skip"], HEAD_DIM, axis=1)
    bsp_t = _pad_lanes(small["b_spatial"].T)
    wsp = small["w_spatial"]

    h = _rms_fwd(x, small["norm_mix_g"], name="rms_mix")
    uv = mm(h, wts["uv"], tn=2048, tk=d, out_dtypes=[f32], name="proj_uv")
    z = mm(h, wts["z"], tn=2048, tk=d, out_dtypes=[f32], name="proj_z")
    xbc = mm(h, wts["xbc"], tn=2048, tk=d, out_dtypes=[f32], name="proj_xbc")
    dtr = mm(h, wts["dt"], tn=LANES, tk=d, out_dtypes=[f32], name="proj_dt")
    gl = mm(h, wts["gate"], tn=2048, tk=d, out_dtypes=[f32], name="proj_gate")
    ya = _gmlp_fwd(uv, small["v_norm_g"], small["v_norm_b"], wsp, bsp_t, name="gmlp_fwd")
    yb, hprev, cv, *gathered = _ssd_fwd(xbc, z, dtr, small["conv_w"], small["conv_b"], dtb, alog, dsk,
                                        small["ssm_norm_g"], ride=fwd_ride, name="ssd_fwd")
    if fwd_ride is not None:
        wts = {**wts, **late_weights(gathered[0])}
    pa = mm(ya, wts["pa"], tn=1024, tk=1024, out_dtypes=[f32], name="proj_a")
    tm_gate = min(t, 512)
    row_vec = [pl.BlockSpec((1, d), lambda i, j, k, half=half: (0, half)) for half in range(2)]
    gate_tiles = [pl.BlockSpec((tm_gate, d), lambda i, j, k, half=half: (i, half)) for half in range(2)]

    def merge(pb_acc, pa_t, gla, glb, bga, bgb):
        return pb_acc, _sigmoid(gla + bga) * pa_t + _sigmoid(glb + bgb) * pb_acc

    pb, merged = _matmul(yb, wts["pb"], tm=tm_gate, tn=d, tk=1024, out_dtypes=[f32, bf16], epilogue=merge,
                         extras=[pa, gl, gl, small["b_gates"], small["b_gates"]],
                         extra_specs=[None] + gate_tiles + row_vec, name="proj_b")

    def residual_norm(acc, res, g):
        x_new = res + acc
        r = lax.rsqrt(jnp.mean(x_new * x_new, axis=1, keepdims=True) + NORM_EPS)
        return x_new, x_new * r * g

    x1, h2 = mm(merged, wts["out"], tn=d, tk=1024, out_dtypes=[f32, bf16], epilogue=residual_norm,
                extras=[x, small["norm_mlp_g"]], extra_specs=[None, row_vec[0]], name="out_proj")
    act = mm(h2, wts["up"], tn=2048, tk=d, out_dtypes=[bf16],
             epilogue=lambda acc: (jnp.square(jnp.maximum(acc, 0.0)),), name="mlp_up")
    x2 = mm(act, wts["down"], tn=1024, tk=2048, out_dtypes=[f32], extras=[x1],
            epilogue=lambda acc, res: (res + acc,), name="mlp_down")

    dx2, dx2b, dgf, loss = _loss_head(x2, tgt, small["norm_final_g"], name="loss_head")
    tt = min(t, 2048)
    tn_mm = functools.partial(_matmul_tn, tt=tt)
    dw = {}
    dw["down"] = tn_mm(act, dx2b, tka=1024, tn=1024, name="dw_down")
    dup = mm(dx2b, wts["down"], nt=True, tn=2048, tk=1024, out_dtypes=[bf16], extras=[act],
             epilogue=lambda acc, a2: (acc * (2.0 * jnp.sqrt(a2).astype(f32)),), name="d_act")
    dw["up"] = tn_mm(h2, dup, tka=1024, tn=1024, name="dw_up")
    dh2 = mm(dup, wts["up"], nt=True, tn=1024, tk=2048, out_dtypes=[f32], name="d_h2")
    dx1, dx1b, dg_mlp = _rms_bwd(x1, small["norm_mlp_g"], dh2, dx2, want_bf16=True, name="rms_mlp_bwd")
    dw["out"] = tn_mm(merged, dx1b, tka=1024, tn=1024, name="dw_out")
    dmerged = mm(dx1b, wts["out"], nt=True, tn=1024, tk=1024, out_dtypes=[f32], name="d_merged")
    dpa, dpb, dgl, dbg = _merge_bwd(dmerged, pa, pb, gl, small["b_gates"], name="merge_bwd")
    dw["pa"] = tn_mm(ya, dpa, tka=1024, tn=1024, name="dw_pa")
    dw["pb"] = tn_mm(yb, dpb, tka=1024, tn=1024, name="dw_pb")
    dya = mm(dpa, wts["pa"], nt=True, tn=1024, tk=1024, out_dtypes=[f32], name="d_ya")
    dyb = mm(dpb, wts["pb"], nt=True, tn=2048, tk=1024, out_dtypes=[f32], name="d_yb")
    swapped = swap_ride(dw) if swap_ride is not None else None
    duv, dwsp, dbsp_t, dvg, dvb, *got_pair = _gmlp_bwd(uv, dya, small["v_norm_g"], small["v_norm_b"], wsp, bsp_t,
                                                       ride=swapped, name="gmlp_bwd")
    ride = bwd_ride(swapped, got_pair[0]) if bwd_ride is not None else None
    dz, dxbc, ddt, dcw, dcb, ddtb, dalog, ddsk, dgs, *got = _ssd_bwd(
        xbc, cv, z, dtr, hprev, dyb, small["conv_w"], dtb, alog, dsk, small["ssm_norm_g"],
        _head_seg_matrix(), ride=ride, name="ssd_bwd")
    dw["uv"] = tn_mm(h, duv, tka=1024, tn=1024, name="dw_uv")
    dw["z"] = tn_mm(h, dz, tka=1024, tn=1024, name="dw_z")
    dw["xbc"] = tn_mm(h, dxbc, tka=1024, tn=1024, name="dw_xbc")
    dw["dt"] = tn_mm(h, ddt, tka=1024, tn=LANES, name="dw_dt")
    dw["gate"] = tn_mm(h, dgl, tka=1024, tn=1024, name="dw_gate")
    last = last_ride(dw) if last_ride is not None else None
    res = _matmul_nt_sum(
        [(duv, wts["uv"]), (dz, wts["z"]), (dxbc, wts["xbc"]), (dgl, wts["gate"]), (ddt, wts["dt"])],
        tm=tm, tks=[1024] * 4 + [LANES], ride=last, name="d_h")
    dh, got_last = (res[0], res[1]) if last is not None else (res, None)
    dx, dg_mix = _rms_bwd(x, small["norm_mix_g"], dh, dx1, want_bf16=False, name="rms_mix_bwd")

    dsmall = {
        "norm_mix_g": dg_mix, "conv_w": dcw, "conv_b": dcb, "dt_bias": ddtb[:, :N_HEADS], "a_log": dalog[:, :N_HEADS],
        "d_skip": ddsk[:, :N_HEADS], "ssm_norm_g": dgs, "v_norm_g": dvg, "v_norm_b": dvb, "w_spatial": dwsp,
        "b_spatial": dbsp_t[:, :GMLP_GROUPS].T, "b_gates": dbg, "norm_mlp_g": dg_mlp, "norm_final_g": dgf,
    }
    return loss, dx, dw, dsmall, (got[0] if got else None), got_last


_IN_SHARD = IN_PROJ // N_CHIPS
_LATE = ("w_proj_a", "w_proj_b", "w_out", "w_mlp_up", "w_mlp_down")
_LATE_ROWS = {"w_proj_a": GMLP_WIDTH // N_CHIPS, "w_proj_b": D_INNER // N_CHIPS, "w_out": D_MODEL // N_CHIPS,
              "w_mlp_up": D_MODEL, "w_mlp_down": D_FF // N_CHIPS}
_LATE_TOTAL = sum(_LATE_ROWS.values())


def _late_offsets():
    off, out = 0, {}
    for k in _LATE:
        out[k] = off
        off += _LATE_ROWS[k]
    return out


_LATE_OFF = _late_offsets()

_SMALL = ("norm_mix_g", "conv_w", "conv_b", "dt_bias", "a_log", "d_skip", "ssm_norm_g", "v_norm_g", "v_norm_b",
          "w_spatial", "b_spatial", "b_gates", "norm_mlp_g", "norm_final_g")


def _pack_small(parts):
    flat = jnp.concatenate([parts[k].reshape(-1) for k in _SMALL])
    rows = -(-flat.shape[0] // (16 * LANES)) * 16
    return jnp.pad(flat, (0, rows * LANES - flat.shape[0])).reshape(rows, LANES)


def _unpack_small(packed, shapes):
    flat = packed.reshape(-1)
    out, off = {}, 0
    for k in _SMALL:
        n = math.prod(shapes[k])
        out[k] = flat[off:off + n].reshape(shapes[k])
        off += n
    return out


def _from_chip_columns(stacked):
    _, rows, cols = stacked.shape
    return stacked.transpose(1, 0, 2).reshape(rows, N_CHIPS * cols)


def _to_chip_columns(full):
    rows, cols = full.shape
    return full.reshape(rows, N_CHIPS, cols // N_CHIPS).transpose(1, 0, 2)


def _w_in_grad_by_chip(dw):
    pieces = [dw["uv"], dw["z"], dw["xbc"], dw["dt"][:, :N_HEADS], dw["gate"]]
    bounds = [0]
    for p in pieces:
        bounds.append(bounds[-1] + p.shape[1])
    chips = []
    for k in range(N_CHIPS):
        lo, hi = k * _IN_SHARD, (k + 1) * _IN_SHARD
        parts = [p[:, max(lo, b0) - b0:min(hi, b1) - b0]
                 for p, b0, b1 in zip(pieces, bounds[:-1], bounds[1:]) if min(hi, b1) > max(lo, b0)]
        chips.append(jnp.concatenate(parts, axis=1))
    return jnp.stack(chips)


def kernel(x, norm_mix_g, w_in, conv_w, conv_b, dt_bias, a_log, d_skip, ssm_norm_g, v_norm_g, v_norm_b, w_spatial, b_spatial, b_gates, w_proj_a, w_proj_b, w_out, norm_mlp_g, w_mlp_up, w_mlp_down, norm_final_g, loss_target, m_norm_mix_g, m_w_in, m_conv_w, m_conv_b, m_dt_bias, m_a_log, m_d_skip, m_ssm_norm_g, m_v_norm_g, m_v_norm_b, m_w_spatial, m_b_spatial, m_b_gates, m_w_proj_a, m_w_proj_b, m_w_out, m_norm_mlp_g, m_w_mlp_up, m_w_mlp_down, m_norm_final_g, v_norm_mix_g, v_w_in, v_conv_w, v_conv_b, v_dt_bias, v_a_log, v_d_skip, v_ssm_norm_g, v_v_norm_g, v_v_norm_b, v_w_spatial, v_b_spatial, v_b_gates, v_w_proj_a, v_w_proj_b, v_w_out, v_norm_mlp_g, v_w_mlp_up, v_w_mlp_down, v_norm_final_g):
    given = dict(locals())
    names = ("norm_mix_g", "w_in", "conv_w", "conv_b", "dt_bias", "a_log", "d_skip", "ssm_norm_g", "v_norm_g",
             "v_norm_b", "w_spatial", "b_spatial", "b_gates", "w_proj_a", "w_proj_b", "w_out", "norm_mlp_g",
             "w_mlp_up", "w_mlp_down", "norm_final_g")
    xi, yi, ci = lax.axis_index("x"), lax.axis_index("y"), lax.axis_index("c")
    me_chip = (2 * xi + yi).astype(jnp.int32)

    def halves(a):
        return a.reshape(2, a.shape[0] // 2, a.shape[1])

    def with_own(got, shard):
        whole = lax.dynamic_update_slice(got, shard[None], (me_chip, 0, 0, 0))
        return whole.reshape(N_CHIPS, 2 * shard.shape[1], shard.shape[2])

    shard_in = halves(_cast_bf16(w_in[0], name="cast_w_in"))
    shard_late = halves(_cast_bf16(jnp.concatenate([given[k][0] for k in _LATE]), name="cast_w_late"))
    shard_conv = halves(conv_w.reshape(2 * _TAIL, -1))
    w_in_full = _from_chip_columns(with_own(_gather_shards(shard_in, name="gather_w_in"), shard_in))
    o_dt, o_gate = 2 * GMLP_WIDTH + D_INNER + CONV_DIM, 2 * GMLP_WIDTH + D_INNER + CONV_DIM + N_HEADS
    wts = {
        "uv": w_in_full[:, :2 * GMLP_WIDTH], "z": w_in_full[:, 2 * GMLP_WIDTH:2 * GMLP_WIDTH + D_INNER],
        "xbc": w_in_full[:, 2 * GMLP_WIDTH + D_INNER:o_dt], "dt": _pad_lanes(w_in_full[:, o_dt:o_gate]),
        "gate": w_in_full[:, o_gate:],
    }
    conv_all = with_own(_gather_shards(shard_conv, name="gather_conv_w"), shard_conv)
    conv_full = _from_chip_columns(conv_all.reshape(N_CHIPS, CONV_W, CONV_DIM // N_CHIPS))

    def late_weights(got):
        g_late = with_own(got, shard_late)

        def rows_of(k):
            return g_late[:, _LATE_OFF[k]:_LATE_OFF[k] + _LATE_ROWS[k]]

        return {
            "pa": rows_of("w_proj_a").reshape(GMLP_WIDTH, D_MODEL),
            "pb": rows_of("w_proj_b").reshape(D_INNER, D_MODEL), "out": rows_of("w_out").reshape(D_MODEL, D_MODEL),
            "up": _from_chip_columns(rows_of("w_mlp_up")), "down": rows_of("w_mlp_down").reshape(D_FF, D_MODEL),
        }

    small = {
        "norm_mix_g": norm_mix_g, "conv_w": conv_full, "conv_b": conv_b, "dt_bias": dt_bias, "a_log": a_log,
        "d_skip": d_skip, "ssm_norm_g": ssm_norm_g, "v_norm_g": v_norm_g, "v_norm_b": v_norm_b,
        "w_spatial": w_spatial[0], "b_spatial": b_spatial[0], "b_gates": b_gates, "norm_mlp_g": norm_mlp_g,
        "norm_final_g": norm_final_g.reshape(1, D_MODEL),
    }

    c_idx = ci.astype(jnp.int32).reshape(1)
    place = jnp.stack([me_chip, ci.astype(jnp.int32)])
    partials = {}

    def reduced_shard(tag, got_chips):
        own = _rs_add_chips(*partials[tag], got_chips, place, name="rs_add_chips_" + tag)
        both = _rs_join_halves(own, name="rs_join_" + tag)
        return both.reshape(2 * both.shape[1], both.shape[2])

    def late_grads(dw):
        def by_rows(a):
            return a.reshape(N_CHIPS, a.shape[0] // N_CHIPS, a.shape[1])

        g = jnp.concatenate([by_rows(dw["pa"]), by_rows(dw["pb"]), by_rows(dw["out"]), _to_chip_columns(dw["up"]),
                             by_rows(dw["down"])], axis=1)
        return g.reshape(N_CHIPS, 2, g.shape[1] // 2, g.shape[2])

    def late_partials(g, got_pair):
        partials["late"] = (g, got_pair)
        return _rs_add_pair(g, got_pair, c_idx, name="rs_add_pair_late")

    def in_partials(dw):
        g = _w_in_grad_by_chip(dw).reshape(N_CHIPS, 2, D_MODEL // 2, _IN_SHARD)
        got_pair = _rs_swap_halves(g, name="rs_swap_in")
        partials["in"] = (g, got_pair)
        return _rs_add_pair(g, got_pair, c_idx, name="rs_add_pair_in")

    loss_part, grad_x, dw, dsmall, got_late, got_in = _local_grads(
        x[0], loss_target[0], wts, small, fwd_ride=shard_late, late_weights=late_weights, swap_ride=late_grads,
        bwd_ride=late_partials, last_ride=in_partials)
    loss = lax.psum(loss_part[0, 0], ("x", "y", "c"))
    g_late = reduced_shard("late", got_late)
    g_in_shard = reduced_shard("in", got_in)

    small_shapes = {k: dsmall[k].shape for k in _SMALL}
    red = _unpack_small(_all_reduce_small(_pack_small(dsmall), name="all_reduce_small"), small_shapes)
    conv_cols = CONV_DIM // N_CHIPS
    red["conv_w"] = lax.dynamic_slice_in_dim(red["conv_w"], me_chip * conv_cols, conv_cols, axis=1)

    grads, deltas, new_m, new_v = {}, {}, {}, {}
    for k in ("w_in",) + _LATE:
        g2 = g_in_shard if k == "w_in" else g_late[_LATE_OFF[k]:_LATE_OFF[k] + _LATE_ROWS[k]]
        dlt, m2, v2 = _adamw(given[k][0], g2, given["m_" + k][0], given["v_" + k][0], name="adamw_" + k)
        grads[k], deltas[k], new_m[k], new_v[k] = g2, dlt, m2, v2
    adam_shapes = dict(small_shapes)
    adam_shapes["conv_w"] = (CONV_W, conv_cols)

    def small_pack_of(prefix):
        return _pack_small({k: given[prefix + k].reshape(adam_shapes[k]) for k in _SMALL})

    dlt_s, m_s, v_s = _adamw(small_pack_of(""), _pack_small(red), small_pack_of("m_"), small_pack_of("v_"),
                             name="adamw_small")
    for dst, packed in ((deltas, dlt_s), (new_m, m_s), (new_v, v_s)):
        dst.update(_unpack_small(packed, adam_shapes))
    grads.update(red)

    def shaped(dct):
        return [dct[k].reshape(given[k].shape) for k in names]

    return (loss, grad_x[None], *shaped(grads), *shaped(deltas), *shaped(new_m), *shaped(new_v))
```

```python
import functools
import math

import jax
import jax.numpy as jnp
from jax import lax
from jax.experimental import pallas as pl
from jax.experimental.pallas import tpu as pltpu

f32 = jnp.float32
bf16 = jnp.bfloat16

D_MODEL = 1024
CHUNK = 128
GMLP_WIDTH = 1024
GMLP_GROUPS = 8
D_INNER = 2048
HEAD_DIM = 64
N_HEADS = 32
N_GROUPS = 8
HEADS_PER_GROUP = 4
GROUP_W = HEADS_PER_GROUP * HEAD_DIM
D_STATE = 128
CONV_W = 4
CONV_DIM = 4096
D_FF = 4096
IN_PROJ = 10272
NORM_EPS = 1e-6
N_CHIPS = 4
N_DEV = 8
LANES = 128

ADAM_LR = 0.001
ADAM_B1 = 0.9
ADAM_B2 = 0.999
ADAM_EPS = 1e-08
ADAM_WD = 0.01
ADAM_STEP = 10

MESH = pl.DeviceIdType.MESH
_NT = (((1,), (1,)), ((), ()))
_NN = (((1,), (0,)), ((), ()))
_TN = (((0,), (0,)), ((), ()))
_MB = 2 ** 20


def _params(sem, vmem_mb=48):
    return pltpu.CompilerParams(dimension_semantics=sem, vmem_limit_bytes=vmem_mb * _MB)


def _dot(a, b, dims=_NN):
    return lax.dot_general(a.astype(bf16), b.astype(bf16), dims, preferred_element_type=f32)


def _dot32(a, b):
    return jnp.dot(a, b, preferred_element_type=f32, precision=lax.Precision.HIGHEST)


def _sigmoid(x):
    return 1.0 / (1.0 + jnp.exp(-x))


def _sum_all(a):
    return jnp.sum(jnp.sum(a, axis=1, keepdims=True), axis=0, keepdims=True)


def _iota(shape, dim):
    return lax.broadcasted_iota(jnp.int32, shape, dim)


def _matmul(a, b, *, nt=False, tm, tn, tk, out_dtypes, epilogue=None, extras=(), extra_specs=None, name):
    m, k_dim = a.shape
    n = b.shape[0] if nt else b.shape[1]
    nk = k_dim // tk
    ne, no = len(extras), len(out_dtypes)
    dims = _NT if nt else _NN

    def body(*refs):
        a_ref, b_ref = refs[0], refs[1]
        ex = refs[2:2 + ne]
        outs = refs[2 + ne:2 + ne + no]

        def finish(acc):
            vals = epilogue(acc, *[e[...] for e in ex]) if epilogue is not None else (acc,)
            for o, v in zip(outs, vals):
                o[...] = v.astype(o.dtype)

        part = lax.dot_general(a_ref[...], b_ref[...], dims, preferred_element_type=f32)
        if nk == 1:
            finish(part)
        else:
            acc_ref = refs[-1]
            kk = pl.program_id(2)

            @pl.when(kk == 0)
            def _():
                acc_ref[...] = part

            @pl.when(kk > 0)
            def _():
                acc_ref[...] += part

            @pl.when(kk == nk - 1)
            def _():
                finish(acc_ref[...])

    b_spec = pl.BlockSpec((tn, tk), lambda i, j, k: (j, k)) if nt else pl.BlockSpec((tk, tn), lambda i, j, k: (k, j))
    tile = pl.BlockSpec((tm, tn), lambda i, j, k: (i, j))
    ex_specs = [tile if s is None else s for s in (extra_specs or [None] * ne)]
    outs = pl.pallas_call(
        body, name=name, grid=(m // tm, n // tn, nk),
        in_specs=[pl.BlockSpec((tm, tk), lambda i, j, k: (i, k)), b_spec] + ex_specs,
        out_specs=[tile] * no,
        out_shape=[jax.ShapeDtypeStruct((m, n), dt) for dt in out_dtypes],
        scratch_shapes=[pltpu.VMEM((tm, tn), f32)] if nk > 1 else [],
        compiler_params=_params(("parallel", "parallel", "arbitrary")),
    )(a, b, *extras)
    return outs if no > 1 else outs[0]


def _matmul_nt_sum(pairs, *, tm, tks, ride=None, name):
    m = pairs[0][0].shape[0]
    n = pairs[0][1].shape[0]
    nblk = [a.shape[1] // tk for (a, _), tk in zip(pairs, tks)]
    starts = [sum(nblk[:p]) for p in range(len(pairs))]
    nk = sum(nblk)
    npairs = len(pairs)
    ni = m // tm
    riding = ride is not None

    def body(*refs):
        rest = refs[2 * npairs:]
        if riding:
            ride_ref, o_ref, got_ref, acc_ref, send_sems, recv_sems = rest
        else:
            o_ref, acc_ref = rest
        i, kk = pl.program_id(0), pl.program_id(1)
        if riding:
            start, finish = _scatter_protocol(ride_ref, got_ref, send_sems, recv_sems)
            pl.when((i == 0) & (kk == 0))(start)

        @pl.when(kk == 0)
        def _():
            acc_ref[...] = jnp.zeros_like(acc_ref)

        for p in range(npairs):
            @pl.when((kk >= starts[p]) & (kk < starts[p] + nblk[p]))
            def _(p=p):
                acc_ref[...] += lax.dot_general(refs[2 * p][...], refs[2 * p + 1][...], _NT, preferred_element_type=f32)

        @pl.when(kk == nk - 1)
        def _():
            o_ref[...] = acc_ref[...]

        if riding:
            pl.when((i == ni - 1) & (kk == nk - 1))(finish)

    in_specs, args = [], []
    for p, (a, b) in enumerate(pairs):
        def kblock(k, s=starts[p], nb=nblk[p]):
            return jnp.clip(k - s, 0, nb - 1)
        in_specs.append(pl.BlockSpec((tm, tks[p]), lambda i, k, kb=kblock: (i, kb(k))))
        in_specs.append(pl.BlockSpec((n, tks[p]), lambda i, k, kb=kblock: (0, kb(k))))
        args += [a, b]
    tile = pl.BlockSpec((tm, n), lambda i, k: (i, 0))
    outs = pl.pallas_call(
        body, name=name, grid=(ni, nk), in_specs=in_specs + [_ANY] * riding, out_specs=[tile] + [_ANY] * riding,
        out_shape=[jax.ShapeDtypeStruct((m, n), f32)]
        + ([jax.ShapeDtypeStruct((N_CHIPS - 1,) + ride.shape[1:], ride.dtype)] if riding else []),
        scratch_shapes=[pltpu.VMEM((tm, n), f32)] + (list(_SCATTER_SCRATCH) if riding else []),
        compiler_params=_params(("arbitrary", "arbitrary"), vmem_mb=56),
    )(*args, *([ride] if riding else []))
    return outs if riding else outs[0]


def _matmul_tn(a, b, *, tka, tn, tt, name):
    t, ka = a.shape
    n = b.shape[1]

    def body(a_ref, b_ref, o_ref):
        part = lax.dot_general(a_ref[...], b_ref[...], _TN, preferred_element_type=f32)
        kk = pl.program_id(2)

        @pl.when(kk == 0)
        def _():
            o_ref[...] = part

        @pl.when(kk > 0)
        def _():
            o_ref[...] += part

    return pl.pallas_call(
        body, name=name, grid=(ka // tka, n // tn, t // tt),
        in_specs=[pl.BlockSpec((tt, tka), lambda i, j, k: (k, i)), pl.BlockSpec((tt, tn), lambda i, j, k: (k, j))],
        out_specs=pl.BlockSpec((tka, tn), lambda i, j, k: (i, j)),
        out_shape=jax.ShapeDtypeStruct((ka, n), f32),
        compiler_params=_params(("parallel", "parallel", "arbitrary")),
    )(a, b)


def _row_tile(t):
    return min(t, 512)


def _rms_fwd(x, g, *, name):
    t, d = x.shape
    tr = _row_tile(t)

    def body(x_ref, g_ref, h_ref):
        xv = x_ref[...]
        r = lax.rsqrt(jnp.mean(xv * xv, axis=1, keepdims=True) + NORM_EPS)
        h_ref[...] = (xv * r * g_ref[...]).astype(bf16)

    return pl.pallas_call(
        body, name=name, grid=(t // tr,),
        in_specs=[pl.BlockSpec((tr, d), lambda i: (i, 0)), pl.BlockSpec((1, d), lambda i: (0, 0))],
        out_specs=pl.BlockSpec((tr, d), lambda i: (i, 0)),
        out_shape=jax.ShapeDtypeStruct((t, d), bf16),
        compiler_params=_params(("parallel",)),
    )(x, g)


def _rms_bwd(xin, g, dh, dres, *, want_bf16, name):
    t, d = xin.shape
    tr = _row_tile(t)

    def body(x_ref, g_ref, dh_ref, dres_ref, dx_ref, *rest):
        dg_ref = rest[-1]
        xv = x_ref[...]
        r = lax.rsqrt(jnp.mean(xv * xv, axis=1, keepdims=True) + NORM_EPS)
        xn = xv * r
        dhv = dh_ref[...]
        dxn = dhv * g_ref[...]
        dx = dres_ref[...] + r * (dxn - xn * jnp.mean(dxn * xn, axis=1, keepdims=True))
        dx_ref[...] = dx
        if want_bf16:
            rest[0][...] = dx.astype(bf16)
        part = jnp.sum(dhv * xn, axis=0, keepdims=True)

        @pl.when(pl.program_id(0) == 0)
        def _():
            dg_ref[...] = part

        @pl.when(pl.program_id(0) > 0)
        def _():
            dg_ref[...] += part

    row = pl.BlockSpec((tr, d), lambda i: (i, 0))
    vec = pl.BlockSpec((1, d), lambda i: (0, 0))
    out_shape = [jax.ShapeDtypeStruct((t, d), f32)] + ([jax.ShapeDtypeStruct((t, d), bf16)] if want_bf16 else []) \
        + [jax.ShapeDtypeStruct((1, d), f32)]
    return pl.pallas_call(
        body, name=name, grid=(t // tr,),
        in_specs=[row, vec, row, row],
        out_specs=[row] + ([row] if want_bf16 else []) + [vec],
        out_shape=out_shape,
        compiler_params=_params(("arbitrary",)),
    )(xin, g, dh, dres)


def _loss_head(x2, tgt, g, *, name):
    t, d = x2.shape
    tr = _row_tile(t)

    def body(x_ref, t_ref, g_ref, dx_ref, dxb_ref, dg_ref, loss_ref):
        xv = x_ref[...]
        gv = g_ref[...]
        r = lax.rsqrt(jnp.mean(xv * xv, axis=1, keepdims=True) + NORM_EPS)
        xn = xv * r
        e = xn * gv - t_ref[...]
        lpart = jnp.zeros((1, LANES), f32) + 0.5 * _sum_all(jnp.mean(e * e, axis=1, keepdims=True))
        dy = e * (1.0 / d)
        dxn = dy * gv
        dx = r * (dxn - xn * jnp.mean(dxn * xn, axis=1, keepdims=True))
        dx_ref[...] = dx
        dxb_ref[...] = dx.astype(bf16)
        gpart = jnp.sum(dy * xn, axis=0, keepdims=True)

        @pl.when(pl.program_id(0) == 0)
        def _():
            dg_ref[...] = gpart
            loss_ref[...] = lpart

        @pl.when(pl.program_id(0) > 0)
        def _():
            dg_ref[...] += gpart
            loss_ref[...] += lpart

    row = pl.BlockSpec((tr, d), lambda i: (i, 0))
    vec = pl.BlockSpec((1, d), lambda i: (0, 0))
    return pl.pallas_call(
        body, name=name, grid=(t // tr,),
        in_specs=[row, row, vec],
        out_specs=[row, row, vec, pl.BlockSpec((1, LANES), lambda i: (0, 0))],
        out_shape=[jax.ShapeDtypeStruct((t, d), f32), jax.ShapeDtypeStruct((t, d), bf16),
                   jax.ShapeDtypeStruct((1, d), f32), jax.ShapeDtypeStruct((1, LANES), f32)],
        compiler_params=_params(("arbitrary",)),
    )(x2, tgt, g)


def _merge_bwd(dm, pa, pb, gl, bg, *, name):
    t, d = pa.shape
    tr = _row_tile(t)

    def body(dm_ref, pa_ref, pb_ref, gla_ref, glb_ref, bga_ref, bgb_ref, dpa_ref, dpb_ref, dgl_ref, dbg_ref):
        dmv = dm_ref[...]
        ga = _sigmoid(gla_ref[...] + bga_ref[...])
        gb = _sigmoid(glb_ref[...] + bgb_ref[...])
        dpa_ref[...] = (dmv * ga).astype(bf16)
        dpb_ref[...] = (dmv * gb).astype(bf16)
        dla = dmv * pa_ref[...] * ga * (1.0 - ga)
        dlb = dmv * pb_ref[...] * gb * (1.0 - gb)
        dgl_ref[:, :d] = dla.astype(bf16)
        dgl_ref[:, d:] = dlb.astype(bf16)
        sa = jnp.sum(dla, axis=0, keepdims=True)
        sb = jnp.sum(dlb, axis=0, keepdims=True)

        @pl.when(pl.program_id(0) == 0)
        def _():
            dbg_ref[:, :d] = sa
            dbg_ref[:, d:] = sb

        @pl.when(pl.program_id(0) > 0)
        def _():
            dbg_ref[:, :d] += sa
            dbg_ref[:, d:] += sb

    row = pl.BlockSpec((tr, d), lambda i: (i, 0))
    return pl.pallas_call(
        body, name=name, grid=(t // tr,),
        in_specs=[row, row, row, row, pl.BlockSpec((tr, d), lambda i: (i, 1)),
                  pl.BlockSpec((1, d), lambda i: (0, 0)), pl.BlockSpec((1, d), lambda i: (0, 1))],
        out_specs=[row, row, pl.BlockSpec((tr, 2 * d), lambda i: (i, 0)), pl.BlockSpec((1, 2 * d), lambda i: (0, 0))],
        out_shape=[jax.ShapeDtypeStruct((t, d), bf16), jax.ShapeDtypeStruct((t, d), bf16),
                   jax.ShapeDtypeStruct((t, 2 * d), bf16), jax.ShapeDtypeStruct((1, 2 * d), f32)],
        compiler_params=_params(("arbitrary",)),
    )(dm, pa, pb, gl, gl, bg, bg)


_INV_SQRT2 = 1.0 / math.sqrt(2.0)
_INV_SQRT2PI = 1.0 / math.sqrt(2.0 * math.pi)


def _gmlp_common(uv, vg, vb, with_grad=False):
    cdf = 0.5 * (1.0 + lax.erf(uv * _INV_SQRT2))
    zz = uv * cdf
    u, vhat, rstd, vn = _gmlp_norm(zz, vg, vb)
    if not with_grad:
        return u, vhat, rstd, vn
    return u, vhat, rstd, vn, cdf + uv * jnp.exp(-0.5 * uv * uv) * _INV_SQRT2PI


def _gmlp_norm(zz, vg, vb):
    u = zz[:, :GMLP_WIDTH]
    v = zz[:, GMLP_WIDTH:]
    mu = jnp.mean(v, axis=1, keepdims=True)
    vc = v - mu
    rstd = lax.rsqrt(jnp.mean(vc * vc, axis=1, keepdims=True) + NORM_EPS)
    vhat = vc * rstd
    vn = vhat * vg + vb
    return u, vhat, rstd, vn


def _gmlp_fwd(uv, vg, vb, wsp, bsp_t, *, name):
    t = uv.shape[0]
    per_step = 4 if t % (4 * CHUNK) == 0 else 1
    rows = per_step * CHUNK

    def body(uv_ref, vg_ref, vb_ref, w_ref, b_ref, y_ref):
        tril = _iota((CHUNK, CHUNK), 0) >= _iota((CHUNK, CHUNK), 1)
        bt = b_ref[...]
        for q in range(per_step):
            qs = slice(q * CHUNK, (q + 1) * CHUNK)
            u, _, _, vn = _gmlp_common(uv_ref[qs, :], vg_ref[...], vb_ref[...])
            for g in range(GMLP_GROUPS):
                sl = slice(g * CHUNK, (g + 1) * CHUNK)
                w = jnp.where(tril, w_ref[g], 0.0)
                s = _dot(w, vn[:, sl]) + bt[:, g:g + 1]
                y_ref[qs, sl] = (u[:, sl] * s).astype(bf16)

    return pl.pallas_call(
        body, name=name, grid=(t // rows,),
        in_specs=[pl.BlockSpec((rows, 2 * GMLP_WIDTH), lambda c: (c, 0)),
                  pl.BlockSpec((1, GMLP_WIDTH), lambda c: (0, 0)), pl.BlockSpec((1, GMLP_WIDTH), lambda c: (0, 0)),
                  pl.BlockSpec((GMLP_GROUPS, CHUNK, CHUNK), lambda c: (0, 0, 0)),
                  pl.BlockSpec((CHUNK, LANES), lambda c: (0, 0))],
        out_specs=pl.BlockSpec((rows, GMLP_WIDTH), lambda c: (c, 0)),
        out_shape=jax.ShapeDtypeStruct((t, GMLP_WIDTH), bf16),
        compiler_params=_params(("parallel",)),
    )(uv, vg, vb, wsp, bsp_t)


def _gmlp_bwd(uv, dya, vg, vb, wsp, bsp_t, *, ride=None, name):
    t = uv.shape[0]
    per_step = 4 if t % (4 * CHUNK) == 0 else 1
    rows = per_step * CHUNK
    steps = t // rows
    riding = ride is not None

    def body(*refs):
        uv_ref, dy_ref, vg_ref, vb_ref, w_ref, b_ref = refs[:6]
        duv_ref, dw_ref, db_ref, dvg_ref, dvb_ref = refs[6 + riding:11 + riding]
        first = pl.program_id(0) == 0
        if riding:
            start, finish = _swap_protocol(refs[6], refs[12], refs[13], refs[14])
            pl.when(first)(start)

        @pl.when(first)
        def _():
            dw_ref[...] = jnp.zeros_like(dw_ref)
            db_ref[...] = jnp.zeros_like(db_ref)
            dvg_ref[...] = jnp.zeros_like(dvg_ref)
            dvb_ref[...] = jnp.zeros_like(dvb_ref)

        vgv = vg_ref[...]
        tril = _iota((CHUNK, CHUNK), 0) >= _iota((CHUNK, CHUNK), 1)
        lane = _iota((CHUNK, LANES), 1)
        bt = b_ref[...]
        for q in range(per_step):
            qs = slice(q * CHUNK, (q + 1) * CHUNK)
            u, vhat, rstd, vn, gelu_grad = _gmlp_common(uv_ref[qs, :], vgv, vb_ref[...], with_grad=True)
            dy = dy_ref[qs, :]
            ds_all = dy * u
            dbacc = jnp.zeros((CHUNK, LANES), f32)
            dvh_parts = []
            for g in range(GMLP_GROUPS):
                sl = slice(g * CHUNK, (g + 1) * CHUNK)
                w = jnp.where(tril, w_ref[g], 0.0)
                vng = vn[:, sl]
                s = _dot(w, vng) + bt[:, g:g + 1]
                ds = ds_all[:, sl]
                duv_ref[qs, sl] = (dy[:, sl] * s * gelu_grad[:, sl]).astype(bf16)
                dw_ref[g] += jnp.where(tril, _dot(ds, vng, _NT), 0.0)
                dbacc = dbacc + jnp.where(lane == g, jnp.sum(ds, axis=1, keepdims=True), 0.0)
                dvn = _dot(w, ds, _TN)
                vh = vhat[:, sl]
                dvg_ref[:, sl] += jnp.sum(dvn * vh, axis=0, keepdims=True)
                dvb_ref[:, sl] += jnp.sum(dvn, axis=0, keepdims=True)
                dvh_parts.append(dvn * vgv[:, sl])
            db_ref[...] += dbacc
            dvhat = jnp.concatenate(dvh_parts, axis=1)
            m1 = jnp.mean(dvhat, axis=1, keepdims=True)
            m2 = jnp.mean(dvhat * vhat, axis=1, keepdims=True)
            dv = rstd * (dvhat - m1 - vhat * m2)
            duv_ref[qs, GMLP_WIDTH:] = (dv * gelu_grad[:, GMLP_WIDTH:]).astype(bf16)
        if riding:
            pl.when(pl.program_id(0) == steps - 1)(finish)

    vec = pl.BlockSpec((1, GMLP_WIDTH), lambda c: (0, 0))
    return pl.pallas_call(
        body, name=name, grid=(steps,),
        in_specs=[pl.BlockSpec((rows, 2 * GMLP_WIDTH), lambda c: (c, 0)),
                  pl.BlockSpec((rows, GMLP_WIDTH), lambda c: (c, 0)), vec, vec,
                  pl.BlockSpec((GMLP_GROUPS, CHUNK, CHUNK), lambda c: (0, 0, 0)),
                  pl.BlockSpec((CHUNK, LANES), lambda c: (0, 0))] + [_ANY] * riding,
        out_specs=[pl.BlockSpec((rows, 2 * GMLP_WIDTH), lambda c: (c, 0)),
                   pl.BlockSpec((GMLP_GROUPS, CHUNK, CHUNK), lambda c: (0, 0, 0)),
                   pl.BlockSpec((CHUNK, LANES), lambda c: (0, 0)), vec, vec] + [_ANY] * riding,
        out_shape=[jax.ShapeDtypeStruct((t, 2 * GMLP_WIDTH), bf16),
                   jax.ShapeDtypeStruct((GMLP_GROUPS, CHUNK, CHUNK), f32),
                   jax.ShapeDtypeStruct((CHUNK, LANES), f32),
                   jax.ShapeDtypeStruct((1, GMLP_WIDTH), f32), jax.ShapeDtypeStruct((1, GMLP_WIDTH), f32)]
        + ([jax.ShapeDtypeStruct(ride.shape[:1] + ride.shape[2:], ride.dtype)] if riding else []),
        scratch_shapes=list(_SWAP_SCRATCH) if riding else [],
        compiler_params=_params(("arbitrary",)),
    )(uv, dya, vg, vb, wsp, bsp_t, *([ride] if riding else []))


_CONV_COLS = 512
_B0, _C0 = D_INNER, D_INNER + N_GROUPS * D_STATE


_TAIL = 8


def _conv_silu(cur_ref, tail_ref, w_ref, b_ref, has_prev, xc_ref, cv_ref):
    row = _iota((_TAIL, _CONV_COLS), 0)
    for j in range(CONV_DIM // _CONV_COLS):
        sl = slice(j * _CONV_COLS, (j + 1) * _CONV_COLS)
        cur = cur_ref[:, sl]
        tail = jnp.where(has_prev, tail_ref[:, sl], 0.0)
        acc = cur * w_ref[CONV_W - 1:CONV_W, sl] + b_ref[:, sl]
        for s in range(1, CONV_W):
            rolled = pltpu.roll(cur, s, 0)
            top = jnp.where(row >= s, rolled[:_TAIL], pltpu.roll(tail, s, 0))
            sh = jnp.concatenate([top, rolled[_TAIL:]], axis=0)
            acc = acc + sh * w_ref[CONV_W - 1 - s:CONV_W - s, sl]
        cv_ref[:, sl] = acc
        xc_ref[:, sl] = acc * _sigmoid(acc)


def _col_bcast(mat, h):
    return jnp.broadcast_to(mat[:, h:h + 1], (CHUNK, LANES))


def _head_expand(cols):
    lo = _iota((CHUNK, LANES), 1) < HEAD_DIM
    return jnp.concatenate([jnp.where(lo, cols[2 * j], cols[2 * j + 1]) for j in range(N_HEADS // 2)], axis=1)


def _ssd_chunk_scalars(dtr, dtb, alog):
    xdt_pre = dtr + dtb
    dtv = jnp.maximum(xdt_pre, 0.0) + jnp.log(1.0 + jnp.exp(-jnp.abs(xdt_pre)))
    a = -jnp.exp(alog)
    ltri = (_iota((CHUNK, CHUNK), 0) >= _iota((CHUNK, CHUNK), 1)).astype(f32)
    cs = _dot32(ltri, dtv * a)
    csb = [_col_bcast(cs, h) for h in range(N_HEADS)]
    cs_x = _head_expand(csb)
    dt_x = _head_expand([_col_bcast(dtv, h) for h in range(N_HEADS)])
    cl_x = cs_x[CHUNK - 1:CHUNK, :]
    return dict(xdt_pre=xdt_pre, dtv=dtv, a=a, cs=cs, cs_t=cs.T, csb=csb, dt_x=dt_x, e_x=jnp.exp(cs_x),
                dec_x=jnp.exp(cl_x - cs_x), dk_x=jnp.exp(cl_x))


def _head_masks():
    lane = _iota((CHUNK, GROUP_W), 1)
    return [(lane >= r * HEAD_DIM) & (lane < (r + 1) * HEAD_DIM) for r in range(HEADS_PER_GROUP)]


def _stack_heads(a, masks):
    return jnp.concatenate([jnp.where(m, a, 0.0) for m in masks], axis=0).astype(bf16)


def _seg_sum(a, seg):
    hi = a.astype(jnp.bfloat16)
    lo = (a - hi.astype(f32)).astype(jnp.bfloat16)
    return (lax.dot_general(hi, seg, _NN, preferred_element_type=f32)
            + lax.dot_general(lo, seg, _NN, preferred_element_type=f32))


def _head_seg_matrix():
    return (_iota((D_INNER, LANES), 0) // HEAD_DIM == _iota((D_INNER, LANES), 1)).astype(jnp.bfloat16)


def _ssd_fwd(xbc, z, dtr, cw, cb, dtb, alog, dsk_x, gs, *, ride=None, name):
    t = xbc.shape[0]
    nc = t // CHUNK
    tiles = CHUNK // _TAIL

    def body(*refs):
        cur_ref, tail_ref, z_ref, dtr_ref, cw_ref, cb_ref, dtb_ref, alog_ref, dsk_ref, gs_ref = refs[:10]
        if ride is None:
            yb_ref, hp_ref, cv_ref, state_ref, xc_ref = refs[10:]
        else:
            ride_ref, yb_ref, hp_ref, cv_ref, got_ref, state_ref, xc_ref, send_sems, recv_sems = refs[10:]
        c = pl.program_id(0)
        if ride is not None:
            start, relay, finish = _gather_protocol(ride_ref, got_ref, send_sems, recv_sems)
            pl.when(c == 0)(start)
            pl.when(c == nc // 2)(relay)

        @pl.when(c == 0)
        def _():
            state_ref[...] = jnp.zeros_like(state_ref)

        _conv_silu(cur_ref, tail_ref, cw_ref, cb_ref, c > 0, xc_ref, cv_ref)
        sc = _ssd_chunk_scalars(dtr_ref[...], dtb_ref[...], alog_ref[...])
        tril = _iota((CHUNK, CHUNK), 0) >= _iota((CHUNK, CHUNK), 1)
        masks = _head_masks()
        hp_ref[0] = state_ref[...]
        for g in range(N_GROUPS):
            gsl = slice(g * GROUP_W, (g + 1) * GROUP_W)
            xs_g = xc_ref[:, gsl]
            bg = xc_ref[:, _B0 + g * D_STATE:_B0 + (g + 1) * D_STATE]
            cg = xc_ref[:, _C0 + g * D_STATE:_C0 + (g + 1) * D_STATE]
            xdt_g = xs_g * sc["dt_x"][:, gsl]
            cbm = _dot(cg, bg, _NT)
            mw = jnp.concatenate(
                [cbm * jnp.exp(jnp.where(tril, sc["csb"][h] - sc["cs_t"][h:h + 1, :], -1e30))
                 for h in range(g * HEADS_PER_GROUP, (g + 1) * HEADS_PER_GROUP)], axis=1)
            ht_g = state_ref[:, gsl]
            y_g = _dot(mw, _stack_heads(xdt_g, masks)) + sc["e_x"][:, gsl] * _dot(cg, ht_g) + dsk_ref[:, gsl] * xs_g
            state_ref[:, gsl] = ht_g * sc["dk_x"][:, gsl] + _dot(bg, xdt_g * sc["dec_x"][:, gsl], _TN)
            zg = z_ref[:, gsl]
            yg = y_g * zg * _sigmoid(zg)
            rs = lax.rsqrt(jnp.mean(yg * yg, axis=1, keepdims=True) + NORM_EPS)
            yb_ref[:, gsl] = (yg * rs * gs_ref[:, gsl]).astype(bf16)
        if ride is not None:
            pl.when(c == nc - 1)(finish)

    def chunk(w):
        return pl.BlockSpec((CHUNK, w), lambda c: (c, 0))

    def const(shape):
        return pl.BlockSpec(shape, lambda c: (0,) * len(shape))

    riding = ride is not None
    return pl.pallas_call(
        body, name=name, grid=(nc,),
        in_specs=[chunk(CONV_DIM), pl.BlockSpec((_TAIL, CONV_DIM), lambda c: (jnp.maximum(c * tiles - 1, 0), 0)),
                  chunk(D_INNER), chunk(LANES), const((CONV_W, CONV_DIM)), const((1, CONV_DIM)),
                  const((1, LANES)), const((1, LANES)), const((1, D_INNER)), const((1, D_INNER))] + [_ANY] * riding,
        out_specs=[chunk(D_INNER), pl.BlockSpec((1, D_STATE, D_INNER), lambda c: (c, 0, 0)), chunk(CONV_DIM)]
        + [_ANY] * riding,
        out_shape=[jax.ShapeDtypeStruct((t, D_INNER), bf16), jax.ShapeDtypeStruct((nc, D_STATE, D_INNER), f32),
                   jax.ShapeDtypeStruct((t, CONV_DIM), f32)]
        + ([jax.ShapeDtypeStruct((N_CHIPS,) + ride.shape, ride.dtype)] if riding else []),
        scratch_shapes=[pltpu.VMEM((D_STATE, D_INNER), f32), pltpu.VMEM((CHUNK, CONV_DIM), f32)]
        + (list(_GATHER_SCRATCH) if riding else []),
        compiler_params=_params(("arbitrary",)),
    )(xbc, xbc, z, dtr, cw, cb, dtb, alog, dsk_x, gs, *([ride] if riding else []))


def _ssd_bwd(xbc, cv, z, dtr, hprev, dyb, cw, dtb, alog, dsk_x, gs, seg, *, ride=None, name):
    t = xbc.shape[0]
    nc = t // CHUNK

    def body(*refs):
        (cur_ref, cv_ref, z_ref, dtr_ref, hp_ref, dyb_ref, cw_ref, dtb_ref, alog_ref, dsk_ref, gs_ref,
         seg_ref) = refs[:12]
        rest = refs[12:]
        if ride is not None:
            ride_ref, got_ref, send_sems, recv_sems = rest[0], rest[10], rest[-2], rest[-1]
            rest = rest[1:10] + rest[11:-2]
        (dz_ref, dxbc_ref, ddt_ref, dcw_ref, dcb_ref, ddtb_ref, dalog_ref, ddsk_ref, dgs_ref,
         dh_ref, dcnext_ref, xc_ref, dxc_ref, x13_ref, x2_ref, rows_ref) = rest
        i = pl.program_id(0)
        if ride is not None:
            start, finish = _scatter_protocol(ride_ref, got_ref, send_sems, recv_sems)
            pl.when(i == 0)(start)

        @pl.when(i == 0)
        def _():
            for ref in (dh_ref, dcnext_ref, dcw_ref, dcb_ref, ddtb_ref, dalog_ref, ddsk_ref, dgs_ref, rows_ref):
                ref[...] = jnp.zeros_like(ref)

        for j in range(CONV_DIM // _CONV_COLS):
            sl = slice(j * _CONV_COLS, (j + 1) * _CONV_COLS)
            cvv = cv_ref[:, sl]
            xc_ref[:, sl] = cvv * _sigmoid(cvv)
        sc = _ssd_chunk_scalars(dtr_ref[...], dtb_ref[...], alog_ref[...])
        tril = _iota((CHUNK, CHUNK), 0) >= _iota((CHUNK, CHUNK), 1)
        triu = _iota((CHUNK, CHUNK), 0) <= _iota((CHUNK, CHUNK), 1)
        masks = _head_masks()
        rowh = _iota((N_HEADS, CHUNK), 0)
        dcs_t = jnp.zeros((N_HEADS, CHUNK), f32)
        for g in range(N_GROUPS):
            gsl = slice(g * GROUP_W, (g + 1) * GROUP_W)
            xs_g = xc_ref[:, gsl]
            bg = xc_ref[:, _B0 + g * D_STATE:_B0 + (g + 1) * D_STATE]
            cg = xc_ref[:, _C0 + g * D_STATE:_C0 + (g + 1) * D_STATE]
            dt_g, e_g, dec_g, dk_g = sc["dt_x"][:, gsl], sc["e_x"][:, gsl], sc["dec_x"][:, gsl], sc["dk_x"][:, gsl]
            dsk_g = dsk_ref[:, gsl]
            xdt_g = xs_g * dt_g
            xdt_stack = _stack_heads(xdt_g, masks)
            cbm = _dot(cg, bg, _NT)
            cbt = _dot(bg, cg, _NT)
            heads = range(g * HEADS_PER_GROUP, (g + 1) * HEADS_PER_GROUP)
            lmats = [jnp.exp(jnp.where(tril, sc["csb"][h] - sc["cs_t"][h:h + 1, :], -1e30)) for h in heads]
            mw = jnp.concatenate([cbm * lm for lm in lmats], axis=1)
            mtw = jnp.concatenate(
                [cbt * jnp.exp(jnp.where(triu, sc["cs_t"][h:h + 1, :] - sc["csb"][h], -1e30)) for h in heads], axis=1)
            ht_g = hp_ref[0, :, gsl]
            dhn_g = dh_ref[:, gsl]
            yoff = e_g * _dot(cg, ht_g)
            y_g = _dot(mw, xdt_stack) + yoff + dsk_g * xs_g
            zg = z_ref[:, gsl]
            sz = _sigmoid(zg)
            silu = zg * sz
            yg = y_g * silu
            rs = lax.rsqrt(jnp.mean(yg * yg, axis=1, keepdims=True) + NORM_EPS)
            yn = yg * rs
            dyb = dyb_ref[:, gsl]
            dgs_ref[:, gsl] += jnp.sum(dyb * yn, axis=0, keepdims=True)
            dyn = dyb * gs_ref[:, gsl]
            dyg = rs * (dyn - yn * jnp.mean(dyn * yn, axis=1, keepdims=True))
            dy_g = dyg * silu
            dz_ref[:, gsl] = (dyg * y_g * (sz * (1.0 + zg * (1.0 - sz)))).astype(bf16)
            dy_stack = _stack_heads(dy_g, masks)
            dm_w = _dot(dy_g, xdt_stack, _NT)
            dmt_w = _dot(xdt_g, dy_stack, _NT)
            dxdt = _dot(mtw, dy_stack)
            dcb_acc = jnp.zeros((CHUNK, CHUNK), f32)
            for r, h in enumerate(heads):
                hs = slice(r * CHUNK, (r + 1) * CHUNK)
                dml = dm_w[:, hs] * lmats[r]
                dcb_acc = dcb_acc + dml
                col = jnp.sum(dml * cbm, axis=0, keepdims=True)
                row = jnp.sum(dmt_w[:, hs] * mtw[:, hs], axis=0, keepdims=True)
                dcs_t = dcs_t + jnp.where(rowh == h, row - col, 0.0)
            w = _dot(bg, dhn_g)
            dxdt = dxdt + dec_g * w
            decx3 = dec_g * (xdt_g * w)
            dg_g = e_g * dy_g
            d_c = _dot(dg_g, ht_g, _NT) + _dot(dcb_acc, bg)
            d_b = _dot(dcb_acc, cg, _TN) + _dot(xdt_g * dec_g, dhn_g, _NT)
            dh_ref[:, gsl] = dhn_g * dk_g + _dot(cg, dg_g, _TN)
            dxc_ref[:, gsl] = dsk_g * dy_g + dxdt * dt_g
            dxc_ref[:, _B0 + g * D_STATE:_B0 + (g + 1) * D_STATE] = d_b
            dxc_ref[:, _C0 + g * D_STATE:_C0 + (g + 1) * D_STATE] = d_c
            x13_ref[:, gsl] = dy_g * yoff - decx3
            x2_ref[:, gsl] = dxdt * xs_g
            rows_ref[0:1, gsl] = jnp.sum(dhn_g * ht_g, axis=0, keepdims=True)
            rows_ref[1:2, gsl] = jnp.sum(decx3, axis=0, keepdims=True)
            rows_ref[2:3, gsl] = jnp.sum(dy_g * xs_g, axis=0, keepdims=True)
        segm = seg_ref[...]
        r13 = _seg_sum(x13_ref[...], segm)
        r2 = _seg_sum(x2_ref[...], segm)
        small = _seg_sum(rows_ref[...], segm)
        lane = _iota((CHUNK, LANES), 1)
        rowi = _iota((CHUNK, LANES), 0)
        dcl_row = small[0:1, :] * jnp.exp(sc["cs"][CHUNK - 1:CHUNK, :]) + small[1:2, :]
        dcs = r13 + jnp.where(rowi == CHUNK - 1, dcl_row, 0.0)
        dcs_t_all = dcs.T + jnp.concatenate([dcs_t, jnp.zeros((LANES - N_HEADS, CHUNK), f32)], axis=0)
        dda = _dot32(dcs_t_all, tril.astype(f32)).T
        a = sc["a"]
        ddt_total = r2 + dda * a
        dalog_ref[...] += jnp.sum(dda * sc["dtv"], axis=0, keepdims=True) * a
        ddtr = jnp.where(lane < N_HEADS, ddt_total * _sigmoid(sc["xdt_pre"]), 0.0)
        ddtb_ref[...] += jnp.sum(ddtr, axis=0, keepdims=True)
        ddt_ref[...] = ddtr.astype(bf16)
        ddsk_ref[...] += small[2:3, :]
        row8 = _iota((_TAIL, _CONV_COLS), 0)
        for j in range(CONV_DIM // _CONV_COLS):
            sl = slice(j * _CONV_COLS, (j + 1) * _CONV_COLS)
            cvv = cv_ref[:, sl]
            sg = _sigmoid(cvv)
            dconv = dxc_ref[:, sl] * (sg * (1.0 + cvv * (1.0 - sg)))
            nxt = dcnext_ref[:, sl]
            cur = cur_ref[:, sl]
            dxin = dconv * cw_ref[CONV_W - 1:CONV_W, sl]
            dcw_ref[CONV_W - 1:CONV_W, sl] += jnp.sum(dconv * cur, axis=0, keepdims=True)
            for s in range(1, CONV_W):
                rolled = pltpu.roll(dconv, CHUNK - s, 0)
                bot = jnp.where(row8 < _TAIL - s, rolled[CHUNK - _TAIL:], pltpu.roll(nxt, _TAIL - s, 0))
                up = jnp.concatenate([rolled[:CHUNK - _TAIL], bot], axis=0)
                dxin = dxin + up * cw_ref[CONV_W - 1 - s:CONV_W - s, sl]
                dcw_ref[CONV_W - 1 - s:CONV_W - s, sl] += jnp.sum(up * cur, axis=0, keepdims=True)
            dcb_ref[:, sl] += jnp.sum(dconv, axis=0, keepdims=True)
            dxbc_ref[:, sl] = dxin.astype(bf16)
            dcnext_ref[:, sl] = dconv[:_TAIL]
        if ride is not None:
            pl.when(i == nc - 1)(finish)

    def chunk(w):
        return pl.BlockSpec((CHUNK, w), lambda i: (nc - 1 - i, 0))

    def const(shape):
        return pl.BlockSpec(shape, lambda i: (0,) * len(shape))

    riding = ride is not None
    return pl.pallas_call(
        body, name=name, grid=(nc,),
        in_specs=[chunk(CONV_DIM), chunk(CONV_DIM),
                  chunk(D_INNER), chunk(LANES), pl.BlockSpec((1, D_STATE, D_INNER), lambda i: (nc - 1 - i, 0, 0)),
                  chunk(D_INNER), const((CONV_W, CONV_DIM)),
                  const((1, LANES)), const((1, LANES)), const((1, D_INNER)), const((1, D_INNER)),
                  const((D_INNER, LANES))] + [_ANY] * riding,
        out_specs=[chunk(D_INNER), chunk(CONV_DIM), chunk(LANES), const((CONV_W, CONV_DIM)), const((1, CONV_DIM)),
                   const((1, LANES)), const((1, LANES)), const((1, LANES)), const((1, D_INNER))] + [_ANY] * riding,
        out_shape=[jax.ShapeDtypeStruct((t, D_INNER), bf16), jax.ShapeDtypeStruct((t, CONV_DIM), bf16),
                   jax.ShapeDtypeStruct((t, LANES), bf16), jax.ShapeDtypeStruct((CONV_W, CONV_DIM), f32),
                   jax.ShapeDtypeStruct((1, CONV_DIM), f32), jax.ShapeDtypeStruct((1, LANES), f32),
                   jax.ShapeDtypeStruct((1, LANES), f32), jax.ShapeDtypeStruct((1, LANES), f32),
                   jax.ShapeDtypeStruct((1, D_INNER), f32)]
        + ([jax.ShapeDtypeStruct((N_CHIPS - 1,) + ride.shape[1:], ride.dtype)] if riding else []),
        scratch_shapes=[pltpu.VMEM((D_STATE, D_INNER), f32), pltpu.VMEM((_TAIL, CONV_DIM), f32),
                        pltpu.VMEM((CHUNK, CONV_DIM), f32), pltpu.VMEM((CHUNK, CONV_DIM), f32),
                        pltpu.VMEM((CHUNK, D_INNER), f32), pltpu.VMEM((CHUNK, D_INNER), f32),
                        pltpu.VMEM((_TAIL, D_INNER), f32)]
        + (list(_SCATTER_SCRATCH) if riding else []),
        compiler_params=_params(("arbitrary",)),
    )(xbc, cv, z, dtr, hprev, dyb, cw, dtb, alog, dsk_x, gs, seg, *([ride] if riding else []))


def _adamw(w, g, m, v, *, name):
    r, c = w.shape
    tr = r
    while tr * c * 4 > 2 * _MB and tr % 16 == 0:
        tr //= 2

    def body(w_ref, g_ref, m_ref, v_ref, d_ref, m2_ref, v2_ref):
        gv = g_ref[...]
        m2 = ADAM_B1 * m_ref[...] + (1.0 - ADAM_B1) * gv
        v2 = ADAM_B2 * v_ref[...] + (1.0 - ADAM_B2) * (gv * gv)
        m_hat = m2 / (1.0 - ADAM_B1 ** ADAM_STEP)
        v_hat = v2 / (1.0 - ADAM_B2 ** ADAM_STEP)
        d_ref[...] = -ADAM_LR * (m_hat / (jnp.sqrt(v_hat) + ADAM_EPS) + ADAM_WD * w_ref[...])
        m2_ref[...] = m2
        v2_ref[...] = v2

    blk = pl.BlockSpec((tr, c), lambda i: (i, 0))
    return pl.pallas_call(
        body, name=name, grid=(r // tr,),
        in_specs=[blk] * 4, out_specs=[blk] * 3,
        out_shape=[jax.ShapeDtypeStruct((r, c), f32)] * 3,
        compiler_params=_params(("parallel",)),
    )(w, g, m, v)


def _row_block(rows, cols):
    cap = max(16, 2 * _MB // (4 * cols))
    return max(tr for tr in range(16, min(cap, rows) + 1, 16) if rows % tr == 0)


def _cast_bf16(a, *, name):
    r, c = a.shape
    tr = _row_block(r, c)

    def body(a_ref, o_ref):
        o_ref[...] = a_ref[...].astype(bf16)

    blk = pl.BlockSpec((tr, c), lambda i: (i, 0))
    return pl.pallas_call(
        body, name=name, grid=(r // tr,), in_specs=[blk], out_specs=blk,
        out_shape=jax.ShapeDtypeStruct((r, c), bf16), compiler_params=_params(("parallel",)),
    )(a)


_ANY = pl.BlockSpec(memory_space=pl.ANY)


def _place():
    x, y, c = lax.axis_index("x"), lax.axis_index("y"), lax.axis_index("c")
    other_chips = [(1 - x, y), (x, 1 - y), (1 - x, 1 - y)]
    return x, y, c, other_chips


def _gather_protocol(in_ref, out_ref, send_sems, recv_sems):
    x, y, c, chips = _place()
    me = 2 * x + y
    sibling = (x, y, 1 - c)
    where = [2 * cx + cy for cx, cy in chips]

    def cp(k, chip, half, to, src=None):
        dst = out_ref.at[chip, half]
        return pltpu.make_async_remote_copy(
            src_ref=dst if src is None else src, dst_ref=dst, send_sem=send_sems.at[k], recv_sem=recv_sems.at[k],
            device_id=to, device_id_type=MESH)

    def sends():
        return [cp(j, me, c, (*chips[j], c), src=in_ref.at[c]) for j in range(2)]

    def relays():
        return [cp(3 + j, where[j], c, sibling) for j in range(3)]

    def landed(j):
        return cp(j, where[j], c, sibling)

    def start():
        for f in sends():
            f.start()

    def relay():
        onward = relays()
        for first in range(2):
            @pl.when(c == first)
            def _(first=first):
                landed(first).wait_recv()
                cp(2, where[first], c, (*chips[1 - first], c)).start()
                onward[first].start()
                landed(1 - first).wait_recv()
                onward[1 - first].start()

    def finish():
        landed(2).wait_recv()
        relays()[2].start()
        for j in range(3):
            cp(3 + j, where[j], 1 - c, sibling).wait_recv()
        for f in sends() + [landed(2)] + relays():
            f.wait_send()

    return start, relay, finish


_GATHER_SCRATCH = [pltpu.SemaphoreType.DMA((6,)), pltpu.SemaphoreType.DMA((6,))]


def _gather_shards(shard, *, name):
    _, rh, lanes = shard.shape

    def body(in_ref, out_ref, send_sems, recv_sems):
        start, relay, finish = _gather_protocol(in_ref, out_ref, send_sems, recv_sems)
        start()
        relay()
        finish()

    return pl.pallas_call(
        body, name=name, in_specs=[_ANY], out_specs=_ANY,
        out_shape=jax.ShapeDtypeStruct((N_CHIPS, 2, rh, lanes), shard.dtype),
        scratch_shapes=list(_GATHER_SCRATCH),
    )(shard)


def _scatter_protocol(p_ref, out_ref, send_sems, recv_sems):
    x, y, c, chips = _place()

    def copies():
        return [pltpu.make_async_remote_copy(
            src_ref=p_ref.at[2 * cx + cy], dst_ref=out_ref.at[j], send_sem=send_sems.at[j], recv_sem=recv_sems.at[j],
            device_id=(cx, cy, c), device_id_type=MESH) for j, (cx, cy) in enumerate(chips)]

    def start():
        for cpy in copies():
            cpy.start()

    def finish():
        for cpy in copies():
            cpy.wait()

    return start, finish


_SCATTER_SCRATCH = [pltpu.SemaphoreType.DMA((3,)), pltpu.SemaphoreType.DMA((3,))]


def _swap_protocol(g_ref, out_ref, send_sems, recv_sems):
    x, y, c, _ = _place()

    def copies():
        return [pltpu.make_async_remote_copy(
            src_ref=g_ref.at[k, 1 - c], dst_ref=out_ref.at[k], send_sem=send_sems.at[k], recv_sem=recv_sems.at[k],
            device_id=(x, y, 1 - c), device_id_type=MESH) for k in range(N_CHIPS)]

    def start():
        for cpy in copies():
            cpy.start()

    def finish():
        for cpy in copies():
            cpy.wait()

    return start, finish


_SWAP_SCRATCH = [pltpu.SemaphoreType.DMA((N_CHIPS,)), pltpu.SemaphoreType.DMA((N_CHIPS,))]


def _rs_swap_halves(g, *, name):
    nch, _, rh, lanes = g.shape

    def body(g_ref, out_ref, send_sems, recv_sems):
        start, finish = _swap_protocol(g_ref, out_ref, send_sems, recv_sems)
        start()
        finish()

    return pl.pallas_call(
        body, name=name, in_specs=[_ANY], out_specs=_ANY,
        out_shape=jax.ShapeDtypeStruct((nch, rh, lanes), g.dtype),
        scratch_shapes=list(_SWAP_SCRATCH),
    )(g)


def _rs_add_pair(g, got, c_idx, *, name):
    nch, _, rh, lanes = g.shape
    tr = _row_block(rh, lanes)

    def body(c_ref, g_ref, got_ref, p16_ref):
        p16_ref[...] = (g_ref[...] + got_ref[...]).astype(bf16)

    blk = pl.BlockSpec((None, tr, lanes), lambda k, i, c_ref: (k, i, 0))
    return pl.pallas_call(
        body, name=name,
        grid_spec=pltpu.PrefetchScalarGridSpec(
            num_scalar_prefetch=1, grid=(nch, rh // tr),
            in_specs=[pl.BlockSpec((None, None, tr, lanes), lambda k, i, c_ref: (k, c_ref[0], i, 0)), blk],
            out_specs=blk),
        out_shape=jax.ShapeDtypeStruct((nch, rh, lanes), bf16),
        compiler_params=_params(("parallel", "parallel")),
    )(c_idx, g, got)


def _rs_add_chips(g, got_pair, got, place, *, name):
    _, _, rh, lanes = g.shape
    tr = _row_block(rh, lanes)

    def body(place_ref, g_ref, pair_ref, got_ref, o_ref):
        own = g_ref[...] + pair_ref[...]
        o_ref[...] = ((own + got_ref[0].astype(f32)) + got_ref[1].astype(f32)) + got_ref[2].astype(f32)

    return pl.pallas_call(
        body, name=name,
        grid_spec=pltpu.PrefetchScalarGridSpec(
            num_scalar_prefetch=1, grid=(rh // tr,),
            in_specs=[pl.BlockSpec((None, None, tr, lanes), lambda i, place_ref: (place_ref[0], place_ref[1], i, 0)),
                      pl.BlockSpec((None, tr, lanes), lambda i, place_ref: (place_ref[0], i, 0)),
                      pl.BlockSpec((3, tr, lanes), lambda i, place_ref: (0, i, 0))],
            out_specs=pl.BlockSpec((None, tr, lanes), lambda i, place_ref: (place_ref[1], i, 0))),
        out_shape=jax.ShapeDtypeStruct((2, rh, lanes), f32),
        compiler_params=_params(("parallel",)),
    )(place, g, got_pair, got)


def _rs_join_halves(halves, *, name):
    def body(h_ref, out_ref, send_sem, recv_sem):
        x, y, c, _ = _place()
        cpy = pltpu.make_async_remote_copy(
            src_ref=h_ref.at[c], dst_ref=out_ref.at[c], send_sem=send_sem, recv_sem=recv_sem,
            device_id=(x, y, 1 - c), device_id_type=MESH)
        cpy.start()
        cpy.wait()

    return pl.pallas_call(
        body, name=name, in_specs=[_ANY], out_specs=_ANY,
        out_shape=jax.ShapeDtypeStruct(halves.shape, halves.dtype), input_output_aliases={0: 0},
        scratch_shapes=[pltpu.SemaphoreType.DMA, pltpu.SemaphoreType.DMA],
    )(halves)


def _all_reduce_small(s, *, name):
    rs, lanes = s.shape
    rh = rs // 2

    def body(s_ref, o_ref, sib_ref, mine_ref, chips_ref, send_sems, recv_sems):
        x, y, c, chips = _place()
        me = 2 * x + y
        sibling = (x, y, 1 - c)
        rows = pl.ds(pl.multiple_of(c * rh, 8), rh)

        def cp(k, src, dst, to):
            return pltpu.make_async_remote_copy(src_ref=src, dst_ref=dst, send_sem=send_sems.at[k],
                                                recv_sem=recv_sems.at[k], device_id=to, device_id_type=MESH)

        swap = cp(0, s_ref, sib_ref, sibling)
        swap.start()
        swap.wait()
        mine_ref[...] = s_ref[rows, :] + sib_ref[rows, :]
        sends = [cp(1 + j, mine_ref, chips_ref.at[j], (cx, cy, c)) for j, (cx, cy) in enumerate(chips)]
        for cpy in sends:
            cpy.start()
        for cpy in sends:
            cpy.wait()
        where = [2 * cx + cy for cx, cy in chips]
        total = None
        for q in range(N_CHIPS):
            term = jnp.where(q == me, mine_ref[...], jnp.where(
                q == where[0], chips_ref[0], jnp.where(q == where[1], chips_ref[1], chips_ref[2])))
            total = term if total is None else total + term
        o_ref[rows, :] = total
        push = cp(4, o_ref.at[rows, :], o_ref.at[rows, :], sibling)
        push.start()
        push.wait()

    vm = pl.BlockSpec(memory_space=pltpu.VMEM)
    return pl.pallas_call(
        body, name=name, in_specs=[vm], out_specs=vm,
        out_shape=jax.ShapeDtypeStruct((rs, lanes), f32),
        scratch_shapes=[pltpu.VMEM((rs, lanes), f32), pltpu.VMEM((rh, lanes), f32),
                        pltpu.VMEM((N_CHIPS - 1, rh, lanes), f32), pltpu.SemaphoreType.DMA((5,)),
                        pltpu.SemaphoreType.DMA((5,))],
        compiler_params=pltpu.CompilerParams(vmem_limit_bytes=32 * _MB),
    )(s)


def _pad_lanes(a, width=LANES):
    return jnp.pad(a, ((0, 0), (0, width - a.shape[1])))


def _local_grads(x, tgt, wts, small, *, fwd_ride=None, late_weights=None, swap_ride=None, bwd_ride=None,
                 last_ride=None):
    t = x.shape[0]
    tm = min(t, 1024)
    d = D_MODEL
    mm = functools.partial(_matmul, tm=tm)

    dtb = _pad_lanes(small["dt_bias"])
    alog = _pad_lanes(small["a_log"])
    dsk = jnp.repeat(small["d_skip"], HEAD_DIM, axis=1)
    bsp_t = _pad_lanes(small["b_spatial"].T)
    wsp = small["w_spatial"]

    h = _rms_fwd(x, small["norm_mix_g"], name="rms_mix")
    uv = mm(h, wts["uv"], tn=2048, tk=d, out_dtypes=[f32], name="proj_uv")
    z = mm(h, wts["z"], tn=2048, tk=d, out_dtypes=[f32], name="proj_z")
    xbc = mm(h, wts["xbc"], tn=2048, tk=d, out_dtypes=[f32], name="proj_xbc")
    dtr = mm(h, wts["dt"], tn=LANES, tk=d, out_dtypes=[f32], name="proj_dt")
    gl = mm(h, wts["gate"], tn=2048, tk=d, out_dtypes=[f32], name="proj_gate")
    ya = _gmlp_fwd(uv, small["v_norm_g"], small["v_norm_b"], wsp, bsp_t, name="gmlp_fwd")
    yb, hprev, cv, *gathered = _ssd_fwd(xbc, z, dtr, small["conv_w"], small["conv_b"], dtb, alog, dsk,
                                        small["ssm_norm_g"], ride=fwd_ride, name="ssd_fwd")
    if fwd_ride is not None:
        wts = {**wts, **late_weights(gathered[0])}
    tall = functools.partial(_matmul, tm=min(t, 2048))
    pa = tall(ya, wts["pa"], tn=1024, tk=1024, out_dtypes=[f32], name="proj_a")
    tm_gate = min(t, 512)
    row_vec = [pl.BlockSpec((1, d), lambda i, j, k, half=half: (0, half)) for half in range(2)]
    gate_tiles = [pl.BlockSpec((tm_gate, d), lambda i, j, k, half=half: (i, half)) for half in range(2)]

    def merge(pb_acc, pa_t, gla, glb, bga, bgb):
        return pb_acc, _sigmoid(gla + bga) * pa_t + _sigmoid(glb + bgb) * pb_acc

    pb, merged = _matmul(yb, wts["pb"], tm=tm_gate, tn=d, tk=1024, out_dtypes=[f32, bf16], epilogue=merge,
                         extras=[pa, gl, gl, small["b_gates"], small["b_gates"]],
                         extra_specs=[None] + gate_tiles + row_vec, name="proj_b")

    def residual_norm(acc, res, g):
        x_new = res + acc
        r = lax.rsqrt(jnp.mean(x_new * x_new, axis=1, keepdims=True) + NORM_EPS)
        return x_new, x_new * r * g

    x1, h2 = mm(merged, wts["out"], tn=d, tk=1024, out_dtypes=[f32, bf16], epilogue=residual_norm,
                extras=[x, small["norm_mlp_g"]], extra_specs=[None, row_vec[0]], name="out_proj")
    act = mm(h2, wts["up"], tn=2048, tk=d, out_dtypes=[bf16],
             epilogue=lambda acc: (jnp.square(jnp.maximum(acc, 0.0)),), name="mlp_up")
    x2 = mm(act, wts["down"], tn=1024, tk=2048, out_dtypes=[f32], extras=[x1],
            epilogue=lambda acc, res: (res + acc,), name="mlp_down")

    dx2, dx2b, dgf, loss = _loss_head(x2, tgt, small["norm_final_g"], name="loss_head")
    tt = min(t, 2048)
    tn_mm = functools.partial(_matmul_tn, tt=tt)
    dw = {}
    dw["down"] = tn_mm(act, dx2b, tka=1024, tn=1024, name="dw_down")
    dup = mm(dx2b, wts["down"], nt=True, tn=2048, tk=1024, out_dtypes=[bf16], extras=[act],
             epilogue=lambda acc, a2: (acc * (2.0 * jnp.sqrt(a2).astype(f32)),), name="d_act")
    dw["up"] = tn_mm(h2, dup, tka=1024, tn=1024, name="dw_up")
    dh2 = mm(dup, wts["up"], nt=True, tn=1024, tk=2048, out_dtypes=[f32], name="d_h2")
    dx1, dx1b, dg_mlp = _rms_bwd(x1, small["norm_mlp_g"], dh2, dx2, want_bf16=True, name="rms_mlp_bwd")
    dw["out"] = tn_mm(merged, dx1b, tka=1024, tn=1024, name="dw_out")
    dmerged = tall(dx1b, wts["out"], nt=True, tn=1024, tk=1024, out_dtypes=[f32], name="d_merged")
    dpa, dpb, dgl, dbg = _merge_bwd(dmerged, pa, pb, gl, small["b_gates"], name="merge_bwd")
    dw["pa"] = tn_mm(ya, dpa, tka=1024, tn=1024, name="dw_pa")
    dw["pb"] = tn_mm(yb, dpb, tka=1024, tn=1024, name="dw_pb")
    dya = tall(dpa, wts["pa"], nt=True, tn=1024, tk=1024, out_dtypes=[f32], name="d_ya")
    dyb = mm(dpb, wts["pb"], nt=True, tn=2048, tk=1024, out_dtypes=[f32], name="d_yb")
    swapped = swap_ride(dw) if swap_ride is not None else None
    duv, dwsp, dbsp_t, dvg, dvb, *got_pair = _gmlp_bwd(uv, dya, small["v_norm_g"], small["v_norm_b"], wsp, bsp_t,
                                                       ride=swapped, name="gmlp_bwd")
    ride = bwd_ride(swapped, got_pair[0]) if bwd_ride is not None else None
    dz, dxbc, ddt, dcw, dcb, ddtb, dalog, ddsk, dgs, *got = _ssd_bwd(
        xbc, cv, z, dtr, hprev, dyb, small["conv_w"], dtb, alog, dsk, small["ssm_norm_g"],
        _head_seg_matrix(), ride=ride, name="ssd_bwd")
    dw["uv"] = tn_mm(h, duv, tka=1024, tn=1024, name="dw_uv")
    dw["z"] = tn_mm(h, dz, tka=1024, tn=1024, name="dw_z")
    dw["xbc"] = tn_mm(h, dxbc, tka=1024, tn=1024, name="dw_xbc")
    dw["dt"] = tn_mm(h, ddt, tka=1024, tn=LANES, name="dw_dt")
    dw["gate"] = tn_mm(h, dgl, tka=1024, tn=1024, name="dw_gate")
    last = last_ride(dw) if last_ride is not None else None
    res = _matmul_nt_sum(
        [(duv, wts["uv"]), (dz, wts["z"]), (dxbc, wts["xbc"]), (dgl, wts["gate"]), (ddt, wts["dt"])],
        tm=tm, tks=[1024] * 4 + [LANES], ride=last, name="d_h")
    dh, got_last = (res[0], res[1]) if last is not None else (res, None)
    dx, dg_mix = _rms_bwd(x, small["norm_mix_g"], dh, dx1, want_bf16=False, name="rms_mix_bwd")

    dsmall = {
        "norm_mix_g": dg_mix, "conv_w": dcw, "conv_b": dcb, "dt_bias": ddtb[:, :N_HEADS], "a_log": dalog[:, :N_HEADS],
        "d_skip": ddsk[:, :N_HEADS], "ssm_norm_g": dgs, "v_norm_g": dvg, "v_norm_b": dvb, "w_spatial": dwsp,
        "b_spatial": dbsp_t[:, :GMLP_GROUPS].T, "b_gates": dbg, "norm_mlp_g": dg_mlp, "norm_final_g": dgf,
    }
    return loss, dx, dw, dsmall, (got[0] if got else None), got_last


_IN_SHARD = IN_PROJ // N_CHIPS
_LATE = ("w_proj_a", "w_proj_b", "w_out", "w_mlp_up", "w_mlp_down")
_LATE_ROWS = {"w_proj_a": GMLP_WIDTH // N_CHIPS, "w_proj_b": D_INNER // N_CHIPS, "w_out": D_MODEL // N_CHIPS,
              "w_mlp_up": D_MODEL, "w_mlp_down": D_FF // N_CHIPS}
_LATE_TOTAL = sum(_LATE_ROWS.values())


def _late_offsets():
    off, out = 0, {}
    for k in _LATE:
        out[k] = off
        off += _LATE_ROWS[k]
    return out


_LATE_OFF = _late_offsets()

_SMALL = ("norm_mix_g", "conv_w", "conv_b", "dt_bias", "a_log", "d_skip", "ssm_norm_g", "v_norm_g", "v_norm_b",
          "w_spatial", "b_spatial", "b_gates", "norm_mlp_g", "norm_final_g")


def _pack_small(parts):
    flat = jnp.concatenate([parts[k].reshape(-1) for k in _SMALL])
    rows = -(-flat.shape[0] // (16 * LANES)) * 16
    return jnp.pad(flat, (0, rows * LANES - flat.shape[0])).reshape(rows, LANES)


def _unpack_small(packed, shapes):
    flat = packed.reshape(-1)
    out, off = {}, 0
    for k in _SMALL:
        n = math.prod(shapes[k])
        out[k] = flat[off:off + n].reshape(shapes[k])
        off += n
    return out


def _from_chip_columns(stacked):
    _, rows, cols = stacked.shape
    return stacked.transpose(1, 0, 2).reshape(rows, N_CHIPS * cols)


def _to_chip_columns(full):
    rows, cols = full.shape
    return full.reshape(rows, N_CHIPS, cols // N_CHIPS).transpose(1, 0, 2)


def _w_in_grad_by_chip(dw):
    pieces = [dw["uv"], dw["z"], dw["xbc"], dw["dt"][:, :N_HEADS], dw["gate"]]
    bounds = [0]
    for p in pieces:
        bounds.append(bounds[-1] + p.shape[1])
    chips = []
    for k in range(N_CHIPS):
        lo, hi = k * _IN_SHARD, (k + 1) * _IN_SHARD
        parts = [p[:, max(lo, b0) - b0:min(hi, b1) - b0]
                 for p, b0, b1 in zip(pieces, bounds[:-1], bounds[1:]) if min(hi, b1) > max(lo, b0)]
        chips.append(jnp.concatenate(parts, axis=1))
    return jnp.stack(chips)


def kernel(x, norm_mix_g, w_in, conv_w, conv_b, dt_bias, a_log, d_skip, ssm_norm_g, v_norm_g, v_norm_b, w_spatial, b_spatial, b_gates, w_proj_a, w_proj_b, w_out, norm_mlp_g, w_mlp_up, w_mlp_down, norm_final_g, loss_target, m_norm_mix_g, m_w_in, m_conv_w, m_conv_b, m_dt_bias, m_a_log, m_d_skip, m_ssm_norm_g, m_v_norm_g, m_v_norm_b, m_w_spatial, m_b_spatial, m_b_gates, m_w_proj_a, m_w_proj_b, m_w_out, m_norm_mlp_g, m_w_mlp_up, m_w_mlp_down, m_norm_final_g, v_norm_mix_g, v_w_in, v_conv_w, v_conv_b, v_dt_bias, v_a_log, v_d_skip, v_ssm_norm_g, v_v_norm_g, v_v_norm_b, v_w_spatial, v_b_spatial, v_b_gates, v_w_proj_a, v_w_proj_b, v_w_out, v_norm_mlp_g, v_w_mlp_up, v_w_mlp_down, v_norm_final_g):
    given = dict(locals())
    names = ("norm_mix_g", "w_in", "conv_w", "conv_b", "dt_bias", "a_log", "d_skip", "ssm_norm_g", "v_norm_g",
             "v_norm_b", "w_spatial", "b_spatial", "b_gates", "w_proj_a", "w_proj_b", "w_out", "norm_mlp_g",
             "w_mlp_up", "w_mlp_down", "norm_final_g")
    xi, yi, ci = lax.axis_index("x"), lax.axis_index("y"), lax.axis_index("c")
    me_chip = (2 * xi + yi).astype(jnp.int32)

    def halves(a):
        return a.reshape(2, a.shape[0] // 2, a.shape[1])

    def with_own(got, shard):
        whole = lax.dynamic_update_slice(got, shard[None], (me_chip, 0, 0, 0))
        return whole.reshape(N_CHIPS, 2 * shard.shape[1], shard.shape[2])

    shard_in = halves(_cast_bf16(w_in[0], name="cast_w_in"))
    shard_late = halves(_cast_bf16(jnp.concatenate([given[k][0] for k in _LATE]), name="cast_w_late"))
    shard_conv = halves(conv_w.reshape(2 * _TAIL, -1))
    w_in_full = _from_chip_columns(with_own(_gather_shards(shard_in, name="gather_w_in"), shard_in))
    o_dt, o_gate = 2 * GMLP_WIDTH + D_INNER + CONV_DIM, 2 * GMLP_WIDTH + D_INNER + CONV_DIM + N_HEADS
    wts = {
        "uv": w_in_full[:, :2 * GMLP_WIDTH], "z": w_in_full[:, 2 * GMLP_WIDTH:2 * GMLP_WIDTH + D_INNER],
        "xbc": w_in_full[:, 2 * GMLP_WIDTH + D_INNER:o_dt], "dt": _pad_lanes(w_in_full[:, o_dt:o_gate]),
        "gate": w_in_full[:, o_gate:],
    }
    conv_all = with_own(_gather_shards(shard_conv, name="gather_conv_w"), shard_conv)
    conv_full = _from_chip_columns(conv_all.reshape(N_CHIPS, CONV_W, CONV_DIM // N_CHIPS))

    def late_weights(got):
        g_late = with_own(got, shard_late)

        def rows_of(k):
            return g_late[:, _LATE_OFF[k]:_LATE_OFF[k] + _LATE_ROWS[k]]

        return {
            "pa": rows_of("w_proj_a").reshape(GMLP_WIDTH, D_MODEL),
            "pb": rows_of("w_proj_b").reshape(D_INNER, D_MODEL), "out": rows_of("w_out").reshape(D_MODEL, D_MODEL),
            "up": _from_chip_columns(rows_of("w_mlp_up")), "down": rows_of("w_mlp_down").reshape(D_FF, D_MODEL),
        }

    small = {
        "norm_mix_g": norm_mix_g, "conv_w": conv_full, "conv_b": conv_b, "dt_bias": dt_bias, "a_log": a_log,
        "d_skip": d_skip, "ssm_norm_g": ssm_norm_g, "v_norm_g": v_norm_g, "v_norm_b": v_norm_b,
        "w_spatial": w_spatial[0], "b_spatial": b_spatial[0], "b_gates": b_gates, "norm_mlp_g": norm_mlp_g,
        "norm_final_g": norm_final_g.reshape(1, D_MODEL),
    }

    c_idx = ci.astype(jnp.int32).reshape(1)
    place = jnp.stack([me_chip, ci.astype(jnp.int32)])
    partials = {}

    def reduced_shard(tag, got_chips):
        own = _rs_add_chips(*partials[tag], got_chips, place, name="rs_add_chips_" + tag)
        both = _rs_join_halves(own, name="rs_join_" + tag)
        return both.reshape(2 * both.shape[1], both.shape[2])

    def late_grads(dw):
        def by_rows(a):
            return a.reshape(N_CHIPS, a.shape[0] // N_CHIPS, a.shape[1])

        g = jnp.concatenate([by_rows(dw["pa"]), by_rows(dw["pb"]), by_rows(dw["out"]), _to_chip_columns(dw["up"]),
                             by_rows(dw["down"])], axis=1)
        return g.reshape(N_CHIPS, 2, g.shape[1] // 2, g.shape[2])

    def late_partials(g, got_pair):
        partials["late"] = (g, got_pair)
        return _rs_add_pair(g, got_pair, c_idx, name="rs_add_pair_late")

    def in_partials(dw):
        g = _w_in_grad_by_chip(dw).reshape(N_CHIPS, 2, D_MODEL // 2, _IN_SHARD)
        got_pair = _rs_swap_halves(g, name="rs_swap_in")
        partials["in"] = (g, got_pair)
        return _rs_add_pair(g, got_pair, c_idx, name="rs_add_pair_in")

    loss_part, grad_x, dw, dsmall, got_late, got_in = _local_grads(
        x[0], loss_target[0], wts, small, fwd_ride=shard_late, late_weights=late_weights, swap_ride=late_grads,
        bwd_ride=late_partials, last_ride=in_partials)
    loss = lax.psum(loss_part[0, 0], ("x", "y", "c"))
    g_late = reduced_shard("late", got_late)
    g_in_shard = reduced_shard("in", got_in)

    small_shapes = {k: dsmall[k].shape for k in _SMALL}
    red = _unpack_small(_all_reduce_small(_pack_small(dsmall), name="all_reduce_small"), small_shapes)
    conv_cols = CONV_DIM // N_CHIPS
    red["conv_w"] = lax.dynamic_slice_in_dim(red["conv_w"], me_chip * conv_cols, conv_cols, axis=1)

    grads, deltas, new_m, new_v = {}, {}, {}, {}
    for k in ("w_in",) + _LATE:
        g2 = g_in_shard if k == "w_in" else g_late[_LATE_OFF[k]:_LATE_OFF[k] + _LATE_ROWS[k]]
        dlt, m2, v2 = _adamw(given[k][0], g2, given["m_" + k][0], given["v_" + k][0], name="adamw_" + k)
        grads[k], deltas[k], new_m[k], new_v[k] = g2, dlt, m2, v2
    adam_shapes = dict(small_shapes)
    adam_shapes["conv_w"] = (CONV_W, conv_cols)

    def small_pack_of(prefix):
        return _pack_small({k: given[prefix + k].reshape(adam_shapes[k]) for k in _SMALL})

    dlt_s, m_s, v_s = _adamw(small_pack_of(""), _pack_small(red), small_pack_of("m_"), small_pack_of("v_"),
                             name="adamw_small")
    for dst, packed in ((deltas, dlt_s), (new_m, m_s), (new_v, v_s)):
        dst.update(_unpack_small(packed, adam_shapes))
    grads.update(red)

    def shaped(dct):
        return [dct[k].reshape(given[k].shape) for k in names]

    return (loss, grad_x[None], *shaped(grads), *shaped(deltas), *shaped(new_m), *shaped(new_v))
```

```python
import functools
import math

import jax
import jax.numpy as jnp
from jax import lax
from jax.experimental import pallas as pl
from jax.experimental.pallas import tpu as pltpu

f32 = jnp.float32
bf16 = jnp.bfloat16

D_MODEL = 1024
CHUNK = 128
GMLP_WIDTH = 1024
GMLP_GROUPS = 8
D_INNER = 2048
HEAD_DIM = 64
N_HEADS = 32
N_GROUPS = 8
HEADS_PER_GROUP = 4
GROUP_W = HEADS_PER_GROUP * HEAD_DIM
D_STATE = 128
CONV_W = 4
CONV_DIM = 4096
D_FF = 4096
IN_PROJ = 10272
NORM_EPS = 1e-6
N_CHIPS = 4
N_DEV = 8
LANES = 128

ADAM_LR = 0.001
ADAM_B1 = 0.9
ADAM_B2 = 0.999
ADAM_EPS = 1e-08
ADAM_WD = 0.01
ADAM_STEP = 10

MESH = pl.DeviceIdType.MESH
_NT = (((1,), (1,)), ((), ()))
_NN = (((1,), (0,)), ((), ()))
_TN = (((0,), (0,)), ((), ()))
_MB = 2 ** 20


def _params(sem, vmem_mb=48):
    return pltpu.CompilerParams(dimension_semantics=sem, vmem_limit_bytes=vmem_mb * _MB)


def _dot(a, b, dims=_NN):
    return lax.dot_general(a.astype(bf16), b.astype(bf16), dims, preferred_element_type=f32)


def _dot32(a, b):
    return jnp.dot(a, b, preferred_element_type=f32, precision=lax.Precision.HIGHEST)


def _sigmoid(x):
    return 1.0 / (1.0 + jnp.exp(-x))


def _sum_all(a):
    return jnp.sum(jnp.sum(a, axis=1, keepdims=True), axis=0, keepdims=True)


def _iota(shape, dim):
    return lax.broadcasted_iota(jnp.int32, shape, dim)


def _matmul(a, b, *, nt=False, tm, tn, tk, out_dtypes, epilogue=None, extras=(), extra_specs=None, name):
    m, k_dim = a.shape
    n = b.shape[0] if nt else b.shape[1]
    nk = k_dim // tk
    ne, no = len(extras), len(out_dtypes)
    dims = _NT if nt else _NN

    def body(*refs):
        a_ref, b_ref = refs[0], refs[1]
        ex = refs[2:2 + ne]
        outs = refs[2 + ne:2 + ne + no]

        def finish(acc):
            vals = epilogue(acc, *[e[...] for e in ex]) if epilogue is not None else (acc,)
            for o, v in zip(outs, vals):
                o[...] = v.astype(o.dtype)

        part = lax.dot_general(a_ref[...], b_ref[...], dims, preferred_element_type=f32)
        if nk == 1:
            finish(part)
        else:
            acc_ref = refs[-1]
            kk = pl.program_id(2)

            @pl.when(kk == 0)
            def _():
                acc_ref[...] = part

            @pl.when(kk > 0)
            def _():
                acc_ref[...] += part

            @pl.when(kk == nk - 1)
            def _():
                finish(acc_ref[...])

    b_spec = pl.BlockSpec((tn, tk), lambda i, j, k: (j, k)) if nt else pl.BlockSpec((tk, tn), lambda i, j, k: (k, j))
    tile = pl.BlockSpec((tm, tn), lambda i, j, k: (i, j))
    ex_specs = [tile if s is None else s for s in (extra_specs or [None] * ne)]
    outs = pl.pallas_call(
        body, name=name, grid=(m // tm, n // tn, nk),
        in_specs=[pl.BlockSpec((tm, tk), lambda i, j, k: (i, k)), b_spec] + ex_specs,
        out_specs=[tile] * no,
        out_shape=[jax.ShapeDtypeStruct((m, n), dt) for dt in out_dtypes],
        scratch_shapes=[pltpu.VMEM((tm, tn), f32)] if nk > 1 else [],
        compiler_params=_params(("parallel", "parallel", "arbitrary")),
    )(a, b, *extras)
    return outs if no > 1 else outs[0]


def _matmul_nt_sum(pairs, *, tm, tks, ride=None, name):
    m = pairs[0][0].shape[0]
    n = pairs[0][1].shape[0]
    nblk = [a.shape[1] // tk for (a, _), tk in zip(pairs, tks)]
    starts = [sum(nblk[:p]) for p in range(len(pairs))]
    nk = sum(nblk)
    npairs = len(pairs)
    ni = m // tm
    riding = ride is not None

    def body(*refs):
        rest = refs[2 * npairs:]
        if riding:
            ride_ref, o_ref, got_ref, acc_ref, send_sems, recv_sems = rest
        else:
            o_ref, acc_ref = rest
        i, kk = pl.program_id(0), pl.program_id(1)
        if riding:
            start, finish = _scatter_protocol(ride_ref, got_ref, send_sems, recv_sems)
            pl.when((i == 0) & (kk == 0))(start)

        @pl.when(kk == 0)
        def _():
            acc_ref[...] = jnp.zeros_like(acc_ref)

        for p in range(npairs):
            @pl.when((kk >= starts[p]) & (kk < starts[p] + nblk[p]))
            def _(p=p):
                acc_ref[...] += lax.dot_general(refs[2 * p][...], refs[2 * p + 1][...], _NT, preferred_element_type=f32)

        @pl.when(kk == nk - 1)
        def _():
            o_ref[...] = acc_ref[...]

        if riding:
            pl.when((i == ni - 1) & (kk == nk - 1))(finish)

    in_specs, args = [], []
    for p, (a, b) in enumerate(pairs):
        def kblock(k, s=starts[p], nb=nblk[p]):
            return jnp.clip(k - s, 0, nb - 1)
        in_specs.append(pl.BlockSpec((tm, tks[p]), lambda i, k, kb=kblock: (i, kb(k))))
        in_specs.append(pl.BlockSpec((n, tks[p]), lambda i, k, kb=kblock: (0, kb(k))))
        args += [a, b]
    tile = pl.BlockSpec((tm, n), lambda i, k: (i, 0))
    outs = pl.pallas_call(
        body, name=name, grid=(ni, nk), in_specs=in_specs + [_ANY] * riding, out_specs=[tile] + [_ANY] * riding,
        out_shape=[jax.ShapeDtypeStruct((m, n), f32)]
        + ([jax.ShapeDtypeStruct((N_CHIPS - 1,) + ride.shape[1:], ride.dtype)] if riding else []),
        scratch_shapes=[pltpu.VMEM((tm, n), f32)] + (list(_SCATTER_SCRATCH) if riding else []),
        compiler_params=_params(("arbitrary", "arbitrary"), vmem_mb=56),
    )(*args, *([ride] if riding else []))
    return outs if riding else outs[0]


def _matmul_tn(a, b, *, tka, tn, tt, name, packed=None, place=None):
    t, ka = a.shape
    n = b.shape[1]

    def body(a_ref, b_ref, *rest):
        o_ref = rest[-1]
        part = lax.dot_general(a_ref[...], b_ref[...], _TN, preferred_element_type=f32)
        kk = pl.program_id(2)

        @pl.when(kk == 0)
        def _():
            o_ref[...] = part

        @pl.when(kk > 0)
        def _():
            o_ref[...] += part

    in_specs = [pl.BlockSpec((tt, tka), lambda i, j, k: (k, i)), pl.BlockSpec((tt, tn), lambda i, j, k: (k, j))]
    if place is None:
        out_spec = pl.BlockSpec((tka, tn), lambda i, j, k: (i, j))
        out_shape = jax.ShapeDtypeStruct((ka, n), f32)
    else:
        out_spec = pl.BlockSpec((None, tka, tn), lambda i, j, k: (*place(i, j), 0))
        out_shape = jax.ShapeDtypeStruct((N_CHIPS, _LATE_TOTAL, tn), f32)
    aliased = packed is not None
    return pl.pallas_call(
        body, name=name, grid=(ka // tka, n // tn, t // tt),
        in_specs=in_specs + [_ANY] * aliased, out_specs=out_spec, out_shape=out_shape,
        input_output_aliases={2: 0} if aliased else {},
        compiler_params=_params(("parallel", "parallel", "arbitrary")),
    )(a, b, *([packed] if aliased else []))


def _row_tile(t):
    return min(t, 512)


def _rms_fwd(x, g, *, name):
    t, d = x.shape
    tr = _row_tile(t)

    def body(x_ref, g_ref, h_ref):
        xv = x_ref[...]
        r = lax.rsqrt(jnp.mean(xv * xv, axis=1, keepdims=True) + NORM_EPS)
        h_ref[...] = (xv * r * g_ref[...]).astype(bf16)

    return pl.pallas_call(
        body, name=name, grid=(t // tr,),
        in_specs=[pl.BlockSpec((tr, d), lambda i: (i, 0)), pl.BlockSpec((1, d), lambda i: (0, 0))],
        out_specs=pl.BlockSpec((tr, d), lambda i: (i, 0)),
        out_shape=jax.ShapeDtypeStruct((t, d), bf16),
        compiler_params=_params(("parallel",)),
    )(x, g)


def _rms_bwd(xin, g, dh, dres, *, want_bf16, name):
    t, d = xin.shape
    tr = _row_tile(t)

    def body(x_ref, g_ref, dh_ref, dres_ref, dx_ref, *rest):
        dg_ref = rest[-1]
        xv = x_ref[...]
        r = lax.rsqrt(jnp.mean(xv * xv, axis=1, keepdims=True) + NORM_EPS)
        xn = xv * r
        dhv = dh_ref[...]
        dxn = dhv * g_ref[...]
        dx = dres_ref[...] + r * (dxn - xn * jnp.mean(dxn * xn, axis=1, keepdims=True))
        dx_ref[...] = dx
        if want_bf16:
            rest[0][...] = dx.astype(bf16)
        part = jnp.sum(dhv * xn, axis=0, keepdims=True)

        @pl.when(pl.program_id(0) == 0)
        def _():
            dg_ref[...] = part

        @pl.when(pl.program_id(0) > 0)
        def _():
            dg_ref[...] += part

    row = pl.BlockSpec((tr, d), lambda i: (i, 0))
    vec = pl.BlockSpec((1, d), lambda i: (0, 0))
    out_shape = [jax.ShapeDtypeStruct((t, d), f32)] + ([jax.ShapeDtypeStruct((t, d), bf16)] if want_bf16 else []) \
        + [jax.ShapeDtypeStruct((1, d), f32)]
    return pl.pallas_call(
        body, name=name, grid=(t // tr,),
        in_specs=[row, vec, row, row],
        out_specs=[row] + ([row] if want_bf16 else []) + [vec],
        out_shape=out_shape,
        compiler_params=_params(("arbitrary",)),
    )(xin, g, dh, dres)


def _loss_head(x2, tgt, g, *, name):
    t, d = x2.shape
    tr = _row_tile(t)

    def body(x_ref, t_ref, g_ref, dx_ref, dxb_ref, dg_ref, loss_ref):
        xv = x_ref[...]
        gv = g_ref[...]
        r = lax.rsqrt(jnp.mean(xv * xv, axis=1, keepdims=True) + NORM_EPS)
        xn = xv * r
        e = xn * gv - t_ref[...]
        lpart = jnp.zeros((1, LANES), f32) + 0.5 * _sum_all(jnp.mean(e * e, axis=1, keepdims=True))
        dy = e * (1.0 / d)
        dxn = dy * gv
        dx = r * (dxn - xn * jnp.mean(dxn * xn, axis=1, keepdims=True))
        dx_ref[...] = dx
        dxb_ref[...] = dx.astype(bf16)
        gpart = jnp.sum(dy * xn, axis=0, keepdims=True)

        @pl.when(pl.program_id(0) == 0)
        def _():
            dg_ref[...] = gpart
            loss_ref[...] = lpart

        @pl.when(pl.program_id(0) > 0)
        def _():
            dg_ref[...] += gpart
            loss_ref[...] += lpart

    row = pl.BlockSpec((tr, d), lambda i: (i, 0))
    vec = pl.BlockSpec((1, d), lambda i: (0, 0))
    return pl.pallas_call(
        body, name=name, grid=(t // tr,),
        in_specs=[row, row, vec],
        out_specs=[row, row, vec, pl.BlockSpec((1, LANES), lambda i: (0, 0))],
        out_shape=[jax.ShapeDtypeStruct((t, d), f32), jax.ShapeDtypeStruct((t, d), bf16),
                   jax.ShapeDtypeStruct((1, d), f32), jax.ShapeDtypeStruct((1, LANES), f32)],
        compiler_params=_params(("arbitrary",)),
    )(x2, tgt, g)


def _merge_bwd(dm, pa, pb, gl, bg, *, name):
    t, d = pa.shape
    tr = _row_tile(t)

    def body(dm_ref, pa_ref, pb_ref, gla_ref, glb_ref, bga_ref, bgb_ref, dpa_ref, dpb_ref, dgl_ref, dbg_ref):
        dmv = dm_ref[...]
        ga = _sigmoid(gla_ref[...] + bga_ref[...])
        gb = _sigmoid(glb_ref[...] + bgb_ref[...])
        dpa_ref[...] = (dmv * ga).astype(bf16)
        dpb_ref[...] = (dmv * gb).astype(bf16)
        dla = dmv * pa_ref[...] * ga * (1.0 - ga)
        dlb = dmv * pb_ref[...] * gb * (1.0 - gb)
        dgl_ref[:, :d] = dla.astype(bf16)
        dgl_ref[:, d:] = dlb.astype(bf16)
        sa = jnp.sum(dla, axis=0, keepdims=True)
        sb = jnp.sum(dlb, axis=0, keepdims=True)

        @pl.when(pl.program_id(0) == 0)
        def _():
            dbg_ref[:, :d] = sa
            dbg_ref[:, d:] = sb

        @pl.when(pl.program_id(0) > 0)
        def _():
            dbg_ref[:, :d] += sa
            dbg_ref[:, d:] += sb

    row = pl.BlockSpec((tr, d), lambda i: (i, 0))
    return pl.pallas_call(
        body, name=name, grid=(t // tr,),
        in_specs=[row, row, row, row, pl.BlockSpec((tr, d), lambda i: (i, 1)),
                  pl.BlockSpec((1, d), lambda i: (0, 0)), pl.BlockSpec((1, d), lambda i: (0, 1))],
        out_specs=[row, row, pl.BlockSpec((tr, 2 * d), lambda i: (i, 0)), pl.BlockSpec((1, 2 * d), lambda i: (0, 0))],
        out_shape=[jax.ShapeDtypeStruct((t, d), bf16), jax.ShapeDtypeStruct((t, d), bf16),
                   jax.ShapeDtypeStruct((t, 2 * d), bf16), jax.ShapeDtypeStruct((1, 2 * d), f32)],
        compiler_params=_params(("arbitrary",)),
    )(dm, pa, pb, gl, gl, bg, bg)


_INV_SQRT2 = 1.0 / math.sqrt(2.0)
_INV_SQRT2PI = 1.0 / math.sqrt(2.0 * math.pi)


def _gmlp_common(uv, vg, vb, with_grad=False):
    cdf = 0.5 * (1.0 + lax.erf(uv * _INV_SQRT2))
    zz = uv * cdf
    u, vhat, rstd, vn = _gmlp_norm(zz, vg, vb)
    if not with_grad:
        return u, vhat, rstd, vn
    return u, vhat, rstd, vn, cdf + uv * jnp.exp(-0.5 * uv * uv) * _INV_SQRT2PI


def _gmlp_norm(zz, vg, vb):
    u = zz[:, :GMLP_WIDTH]
    v = zz[:, GMLP_WIDTH:]
    mu = jnp.mean(v, axis=1, keepdims=True)
    vc = v - mu
    rstd = lax.rsqrt(jnp.mean(vc * vc, axis=1, keepdims=True) + NORM_EPS)
    vhat = vc * rstd
    vn = vhat * vg + vb
    return u, vhat, rstd, vn


def _gmlp_fwd(uv, vg, vb, wsp, bsp_t, *, name):
    t = uv.shape[0]
    per_step = 4 if t % (4 * CHUNK) == 0 else 1
    rows = per_step * CHUNK

    def body(uv_ref, vg_ref, vb_ref, w_ref, b_ref, y_ref):
        tril = _iota((CHUNK, CHUNK), 0) >= _iota((CHUNK, CHUNK), 1)
        bt = b_ref[...]
        for q in range(per_step):
            qs = slice(q * CHUNK, (q + 1) * CHUNK)
            u, _, _, vn = _gmlp_common(uv_ref[qs, :], vg_ref[...], vb_ref[...])
            for g in range(GMLP_GROUPS):
                sl = slice(g * CHUNK, (g + 1) * CHUNK)
                w = jnp.where(tril, w_ref[g], 0.0)
                s = _dot(w, vn[:, sl]) + bt[:, g:g + 1]
                y_ref[qs, sl] = (u[:, sl] * s).astype(bf16)

    return pl.pallas_call(
        body, name=name, grid=(t // rows,),
        in_specs=[pl.BlockSpec((rows, 2 * GMLP_WIDTH), lambda c: (c, 0)),
                  pl.BlockSpec((1, GMLP_WIDTH), lambda c: (0, 0)), pl.BlockSpec((1, GMLP_WIDTH), lambda c: (0, 0)),
                  pl.BlockSpec((GMLP_GROUPS, CHUNK, CHUNK), lambda c: (0, 0, 0)),
                  pl.BlockSpec((CHUNK, LANES), lambda c: (0, 0))],
        out_specs=pl.BlockSpec((rows, GMLP_WIDTH), lambda c: (c, 0)),
        out_shape=jax.ShapeDtypeStruct((t, GMLP_WIDTH), bf16),
        compiler_params=_params(("parallel",)),
    )(uv, vg, vb, wsp, bsp_t)


def _gmlp_bwd(uv, dya, vg, vb, wsp, bsp_t, *, ride=None, name):
    t = uv.shape[0]
    per_step = 4 if t % (4 * CHUNK) == 0 else 1
    rows = per_step * CHUNK
    steps = t // rows
    riding = ride is not None

    def body(*refs):
        uv_ref, dy_ref, vg_ref, vb_ref, w_ref, b_ref = refs[:6]
        duv_ref, dw_ref, db_ref, dvg_ref, dvb_ref = refs[6 + riding:11 + riding]
        first = pl.program_id(0) == 0
        if riding:
            start, finish = _swap_protocol(refs[6], refs[12], refs[13], refs[14])
            pl.when(first)(start)

        @pl.when(first)
        def _():
            dw_ref[...] = jnp.zeros_like(dw_ref)
            db_ref[...] = jnp.zeros_like(db_ref)
            dvg_ref[...] = jnp.zeros_like(dvg_ref)
            dvb_ref[...] = jnp.zeros_like(dvb_ref)

        vgv = vg_ref[...]
        tril = _iota((CHUNK, CHUNK), 0) >= _iota((CHUNK, CHUNK), 1)
        lane = _iota((CHUNK, LANES), 1)
        bt = b_ref[...]
        for q in range(per_step):
            qs = slice(q * CHUNK, (q + 1) * CHUNK)
            u, vhat, rstd, vn, gelu_grad = _gmlp_common(uv_ref[qs, :], vgv, vb_ref[...], with_grad=True)
            dy = dy_ref[qs, :]
            ds_all = dy * u
            dbacc = jnp.zeros((CHUNK, LANES), f32)
            dvh_parts = []
            for g in range(GMLP_GROUPS):
                sl = slice(g * CHUNK, (g + 1) * CHUNK)
                w = jnp.where(tril, w_ref[g], 0.0)
                vng = vn[:, sl]
                s = _dot(w, vng) + bt[:, g:g + 1]
                ds = ds_all[:, sl]
                duv_ref[qs, sl] = (dy[:, sl] * s * gelu_grad[:, sl]).astype(bf16)
                dw_ref[g] += jnp.where(tril, _dot(ds, vng, _NT), 0.0)
                dbacc = dbacc + jnp.where(lane == g, jnp.sum(ds, axis=1, keepdims=True), 0.0)
                dvn = _dot(w, ds, _TN)
                vh = vhat[:, sl]
                dvg_ref[:, sl] += jnp.sum(dvn * vh, axis=0, keepdims=True)
                dvb_ref[:, sl] += jnp.sum(dvn, axis=0, keepdims=True)
                dvh_parts.append(dvn * vgv[:, sl])
            db_ref[...] += dbacc
            dvhat = jnp.concatenate(dvh_parts, axis=1)
            m1 = jnp.mean(dvhat, axis=1, keepdims=True)
            m2 = jnp.mean(dvhat * vhat, axis=1, keepdims=True)
            dv = rstd * (dvhat - m1 - vhat * m2)
            duv_ref[qs, GMLP_WIDTH:] = (dv * gelu_grad[:, GMLP_WIDTH:]).astype(bf16)
        if riding:
            pl.when(pl.program_id(0) == steps - 1)(finish)

    vec = pl.BlockSpec((1, GMLP_WIDTH), lambda c: (0, 0))
    return pl.pallas_call(
        body, name=name, grid=(steps,),
        in_specs=[pl.BlockSpec((rows, 2 * GMLP_WIDTH), lambda c: (c, 0)),
                  pl.BlockSpec((rows, GMLP_WIDTH), lambda c: (c, 0)), vec, vec,
                  pl.BlockSpec((GMLP_GROUPS, CHUNK, CHUNK), lambda c: (0, 0, 0)),
                  pl.BlockSpec((CHUNK, LANES), lambda c: (0, 0))] + [_ANY] * riding,
        out_specs=[pl.BlockSpec((rows, 2 * GMLP_WIDTH), lambda c: (c, 0)),
                   pl.BlockSpec((GMLP_GROUPS, CHUNK, CHUNK), lambda c: (0, 0, 0)),
                   pl.BlockSpec((CHUNK, LANES), lambda c: (0, 0)), vec, vec] + [_ANY] * riding,
        out_shape=[jax.ShapeDtypeStruct((t, 2 * GMLP_WIDTH), bf16),
                   jax.ShapeDtypeStruct((GMLP_GROUPS, CHUNK, CHUNK), f32),
                   jax.ShapeDtypeStruct((CHUNK, LANES), f32),
                   jax.ShapeDtypeStruct((1, GMLP_WIDTH), f32), jax.ShapeDtypeStruct((1, GMLP_WIDTH), f32)]
        + ([jax.ShapeDtypeStruct(ride.shape[:1] + ride.shape[2:], ride.dtype)] if riding else []),
        scratch_shapes=list(_SWAP_SCRATCH) if riding else [],
        compiler_params=_params(("arbitrary",)),
    )(uv, dya, vg, vb, wsp, bsp_t, *([ride] if riding else []))


_CONV_COLS = 512
_B0, _C0 = D_INNER, D_INNER + N_GROUPS * D_STATE


_TAIL = 8


def _conv_silu(cur_ref, tail_ref, w_ref, b_ref, has_prev, xc_ref, cv_ref):
    row = _iota((_TAIL, _CONV_COLS), 0)
    for j in range(CONV_DIM // _CONV_COLS):
        sl = slice(j * _CONV_COLS, (j + 1) * _CONV_COLS)
        cur = cur_ref[:, sl]
        tail = jnp.where(has_prev, tail_ref[:, sl], 0.0)
        acc = cur * w_ref[CONV_W - 1:CONV_W, sl] + b_ref[:, sl]
        for s in range(1, CONV_W):
            rolled = pltpu.roll(cur, s, 0)
            top = jnp.where(row >= s, rolled[:_TAIL], pltpu.roll(tail, s, 0))
            sh = jnp.concatenate([top, rolled[_TAIL:]], axis=0)
            acc = acc + sh * w_ref[CONV_W - 1 - s:CONV_W - s, sl]
        cv_ref[:, sl] = acc
        xc_ref[:, sl] = acc * _sigmoid(acc)


def _col_bcast(mat, h):
    return jnp.broadcast_to(mat[:, h:h + 1], (CHUNK, LANES))


def _head_expand(cols):
    lo = _iota((CHUNK, LANES), 1) < HEAD_DIM
    return jnp.concatenate([jnp.where(lo, cols[2 * j], cols[2 * j + 1]) for j in range(N_HEADS // 2)], axis=1)


def _ssd_chunk_scalars(dtr, dtb, alog):
    xdt_pre = dtr + dtb
    dtv = jnp.maximum(xdt_pre, 0.0) + jnp.log(1.0 + jnp.exp(-jnp.abs(xdt_pre)))
    a = -jnp.exp(alog)
    ltri = (_iota((CHUNK, CHUNK), 0) >= _iota((CHUNK, CHUNK), 1)).astype(f32)
    cs = _dot32(ltri, dtv * a)
    csb = [_col_bcast(cs, h) for h in range(N_HEADS)]
    cs_x = _head_expand(csb)
    dt_x = _head_expand([_col_bcast(dtv, h) for h in range(N_HEADS)])
    cl_x = cs_x[CHUNK - 1:CHUNK, :]
    return dict(xdt_pre=xdt_pre, dtv=dtv, a=a, cs=cs, cs_t=cs.T, csb=csb, dt_x=dt_x, e_x=jnp.exp(cs_x),
                dec_x=jnp.exp(cl_x - cs_x), dk_x=jnp.exp(cl_x))


def _head_masks():
    lane = _iota((CHUNK, GROUP_W), 1)
    return [(lane >= r * HEAD_DIM) & (lane < (r + 1) * HEAD_DIM) for r in range(HEADS_PER_GROUP)]


def _stack_heads(a, masks):
    return jnp.concatenate([jnp.where(m, a, 0.0) for m in masks], axis=0).astype(bf16)


def _seg_sum(a, seg):
    hi = a.astype(jnp.bfloat16)
    lo = (a - hi.astype(f32)).astype(jnp.bfloat16)
    return (lax.dot_general(hi, seg, _NN, preferred_element_type=f32)
            + lax.dot_general(lo, seg, _NN, preferred_element_type=f32))


def _head_seg_matrix():
    return (_iota((D_INNER, LANES), 0) // HEAD_DIM == _iota((D_INNER, LANES), 1)).astype(jnp.bfloat16)


def _ssd_fwd(xbc, z, dtr, cw, cb, dtb, alog, dsk_x, gs, *, ride=None, name):
    t = xbc.shape[0]
    nc = t // CHUNK
    tiles = CHUNK // _TAIL

    def body(*refs):
        cur_ref, tail_ref, z_ref, dtr_ref, cw_ref, cb_ref, dtb_ref, alog_ref, dsk_ref, gs_ref = refs[:10]
        if ride is None:
            yb_ref, hp_ref, cv_ref, state_ref, xc_ref = refs[10:]
        else:
            ride_ref, yb_ref, hp_ref, cv_ref, got_ref, state_ref, xc_ref, send_sems, recv_sems = refs[10:]
        c = pl.program_id(0)
        if ride is not None:
            start, relay, finish = _gather_protocol(ride_ref, got_ref, send_sems, recv_sems)
            pl.when(c == 0)(start)
            pl.when(c == nc // 2)(relay)

        @pl.when(c == 0)
        def _():
            state_ref[...] = jnp.zeros_like(state_ref)

        _conv_silu(cur_ref, tail_ref, cw_ref, cb_ref, c > 0, xc_ref, cv_ref)
        sc = _ssd_chunk_scalars(dtr_ref[...], dtb_ref[...], alog_ref[...])
        tril = _iota((CHUNK, CHUNK), 0) >= _iota((CHUNK, CHUNK), 1)
        masks = _head_masks()
        hp_ref[0] = state_ref[...]
        for g in range(N_GROUPS):
            gsl = slice(g * GROUP_W, (g + 1) * GROUP_W)
            xs_g = xc_ref[:, gsl]
            bg = xc_ref[:, _B0 + g * D_STATE:_B0 + (g + 1) * D_STATE]
            cg = xc_ref[:, _C0 + g * D_STATE:_C0 + (g + 1) * D_STATE]
            xdt_g = xs_g * sc["dt_x"][:, gsl]
            cbm = _dot(cg, bg, _NT)
            mw = jnp.concatenate(
                [cbm * jnp.exp(jnp.where(tril, sc["csb"][h] - sc["cs_t"][h:h + 1, :], -1e30))
                 for h in range(g * HEADS_PER_GROUP, (g + 1) * HEADS_PER_GROUP)], axis=1)
            ht_g = state_ref[:, gsl]
            y_g = _dot(mw, _stack_heads(xdt_g, masks)) + sc["e_x"][:, gsl] * _dot(cg, ht_g) + dsk_ref[:, gsl] * xs_g
            state_ref[:, gsl] = ht_g * sc["dk_x"][:, gsl] + _dot(bg, xdt_g * sc["dec_x"][:, gsl], _TN)
            zg = z_ref[:, gsl]
            yg = y_g * zg * _sigmoid(zg)
            rs = lax.rsqrt(jnp.mean(yg * yg, axis=1, keepdims=True) + NORM_EPS)
            yb_ref[:, gsl] = (yg * rs * gs_ref[:, gsl]).astype(bf16)
        if ride is not None:
            pl.when(c == nc - 1)(finish)

    def chunk(w):
        return pl.BlockSpec((CHUNK, w), lambda c: (c, 0))

    def const(shape):
        return pl.BlockSpec(shape, lambda c: (0,) * len(shape))

    riding = ride is not None
    return pl.pallas_call(
        body, name=name, grid=(nc,),
        in_specs=[chunk(CONV_DIM), pl.BlockSpec((_TAIL, CONV_DIM), lambda c: (jnp.maximum(c * tiles - 1, 0), 0)),
                  chunk(D_INNER), chunk(LANES), const((CONV_W, CONV_DIM)), const((1, CONV_DIM)),
                  const((1, LANES)), const((1, LANES)), const((1, D_INNER)), const((1, D_INNER))] + [_ANY] * riding,
        out_specs=[chunk(D_INNER), pl.BlockSpec((1, D_STATE, D_INNER), lambda c: (c, 0, 0)), chunk(CONV_DIM)]
        + [_ANY] * riding,
        out_shape=[jax.ShapeDtypeStruct((t, D_INNER), bf16), jax.ShapeDtypeStruct((nc, D_STATE, D_INNER), f32),
                   jax.ShapeDtypeStruct((t, CONV_DIM), f32)]
        + ([jax.ShapeDtypeStruct((N_CHIPS,) + ride.shape, ride.dtype)] if riding else []),
        scratch_shapes=[pltpu.VMEM((D_STATE, D_INNER), f32), pltpu.VMEM((CHUNK, CONV_DIM), f32)]
        + (list(_GATHER_SCRATCH) if riding else []),
        compiler_params=_params(("arbitrary",)),
    )(xbc, xbc, z, dtr, cw, cb, dtb, alog, dsk_x, gs, *([ride] if riding else []))


def _ssd_bwd(xbc, cv, z, dtr, hprev, dyb, cw, dtb, alog, dsk_x, gs, seg, *, ride=None, name):
    t = xbc.shape[0]
    nc = t // CHUNK

    def body(*refs):
        (cur_ref, cv_ref, z_ref, dtr_ref, hp_ref, dyb_ref, cw_ref, dtb_ref, alog_ref, dsk_ref, gs_ref,
         seg_ref) = refs[:12]
        rest = refs[12:]
        if ride is not None:
            ride_ref, got_ref, send_sems, recv_sems = rest[0], rest[10], rest[-2], rest[-1]
            rest = rest[1:10] + rest[11:-2]
        (dz_ref, dxbc_ref, ddt_ref, dcw_ref, dcb_ref, ddtb_ref, dalog_ref, ddsk_ref, dgs_ref,
         dh_ref, dcnext_ref, xc_ref, dxc_ref, x13_ref, x2_ref, rows_ref) = rest
        i = pl.program_id(0)
        if ride is not None:
            start, finish = _scatter_protocol(ride_ref, got_ref, send_sems, recv_sems)
            pl.when(i == 0)(start)

        @pl.when(i == 0)
        def _():
            for ref in (dh_ref, dcnext_ref, dcw_ref, dcb_ref, ddtb_ref, dalog_ref, ddsk_ref, dgs_ref, rows_ref):
                ref[...] = jnp.zeros_like(ref)

        for j in range(CONV_DIM // _CONV_COLS):
            sl = slice(j * _CONV_COLS, (j + 1) * _CONV_COLS)
            cvv = cv_ref[:, sl]
            xc_ref[:, sl] = cvv * _sigmoid(cvv)
        sc = _ssd_chunk_scalars(dtr_ref[...], dtb_ref[...], alog_ref[...])
        tril = _iota((CHUNK, CHUNK), 0) >= _iota((CHUNK, CHUNK), 1)
        triu = _iota((CHUNK, CHUNK), 0) <= _iota((CHUNK, CHUNK), 1)
        masks = _head_masks()
        rowh = _iota((N_HEADS, CHUNK), 0)
        dcs_t = jnp.zeros((N_HEADS, CHUNK), f32)
        for g in range(N_GROUPS):
            gsl = slice(g * GROUP_W, (g + 1) * GROUP_W)
            xs_g = xc_ref[:, gsl]
            bg = xc_ref[:, _B0 + g * D_STATE:_B0 + (g + 1) * D_STATE]
            cg = xc_ref[:, _C0 + g * D_STATE:_C0 + (g + 1) * D_STATE]
            dt_g, e_g, dec_g, dk_g = sc["dt_x"][:, gsl], sc["e_x"][:, gsl], sc["dec_x"][:, gsl], sc["dk_x"][:, gsl]
            dsk_g = dsk_ref[:, gsl]
            xdt_g = xs_g * dt_g
            xdt_stack = _stack_heads(xdt_g, masks)
            cbm = _dot(cg, bg, _NT)
            cbt = _dot(bg, cg, _NT)
            heads = range(g * HEADS_PER_GROUP, (g + 1) * HEADS_PER_GROUP)
            lmats = [jnp.exp(jnp.where(tril, sc["csb"][h] - sc["cs_t"][h:h + 1, :], -1e30)) for h in heads]
            mw = jnp.concatenate([cbm * lm for lm in lmats], axis=1)
            mtw = jnp.concatenate(
                [cbt * jnp.exp(jnp.where(triu, sc["cs_t"][h:h + 1, :] - sc["csb"][h], -1e30)) for h in heads], axis=1)
            ht_g = hp_ref[0, :, gsl]
            dhn_g = dh_ref[:, gsl]
            yoff = e_g * _dot(cg, ht_g)
            y_g = _dot(mw, xdt_stack) + yoff + dsk_g * xs_g
            zg = z_ref[:, gsl]
            sz = _sigmoid(zg)
            silu = zg * sz
            yg = y_g * silu
            rs = lax.rsqrt(jnp.mean(yg * yg, axis=1, keepdims=True) + NORM_EPS)
            yn = yg * rs
            dyb = dyb_ref[:, gsl]
            dgs_ref[:, gsl] += jnp.sum(dyb * yn, axis=0, keepdims=True)
            dyn = dyb * gs_ref[:, gsl]
            dyg = rs * (dyn - yn * jnp.mean(dyn * yn, axis=1, keepdims=True))
            dy_g = dyg * silu
            dz_ref[:, gsl] = (dyg * y_g * (sz * (1.0 + zg * (1.0 - sz)))).astype(bf16)
            dy_stack = _stack_heads(dy_g, masks)
            dm_w = _dot(dy_g, xdt_stack, _NT)
            dmt_w = _dot(xdt_g, dy_stack, _NT)
            dxdt = _dot(mtw, dy_stack)
            dcb_acc = jnp.zeros((CHUNK, CHUNK), f32)
            for r, h in enumerate(heads):
                hs = slice(r * CHUNK, (r + 1) * CHUNK)
                dml = dm_w[:, hs] * lmats[r]
                dcb_acc = dcb_acc + dml
                col = jnp.sum(dml * cbm, axis=0, keepdims=True)
                row = jnp.sum(dmt_w[:, hs] * mtw[:, hs], axis=0, keepdims=True)
                dcs_t = dcs_t + jnp.where(rowh == h, row - col, 0.0)
            w = _dot(bg, dhn_g)
            dxdt = dxdt + dec_g * w
            decx3 = dec_g * (xdt_g * w)
            dg_g = e_g * dy_g
            d_c = _dot(dg_g, ht_g, _NT) + _dot(dcb_acc, bg)
            d_b = _dot(dcb_acc, cg, _TN) + _dot(xdt_g * dec_g, dhn_g, _NT)
            dh_ref[:, gsl] = dhn_g * dk_g + _dot(cg, dg_g, _TN)
            dxc_ref[:, gsl] = dsk_g * dy_g + dxdt * dt_g
            dxc_ref[:, _B0 + g * D_STATE:_B0 + (g + 1) * D_STATE] = d_b
            dxc_ref[:, _C0 + g * D_STATE:_C0 + (g + 1) * D_STATE] = d_c
            x13_ref[:, gsl] = dy_g * yoff - decx3
            x2_ref[:, gsl] = dxdt * xs_g
            rows_ref[0:1, gsl] = jnp.sum(dhn_g * ht_g, axis=0, keepdims=True)
            rows_ref[1:2, gsl] = jnp.sum(decx3, axis=0, keepdims=True)
            rows_ref[2:3, gsl] = jnp.sum(dy_g * xs_g, axis=0, keepdims=True)
        segm = seg_ref[...]
        r13 = _seg_sum(x13_ref[...], segm)
        r2 = _seg_sum(x2_ref[...], segm)
        small = _seg_sum(rows_ref[...], segm)
        lane = _iota((CHUNK, LANES), 1)
        rowi = _iota((CHUNK, LANES), 0)
        dcl_row = small[0:1, :] * jnp.exp(sc["cs"][CHUNK - 1:CHUNK, :]) + small[1:2, :]
        dcs = r13 + jnp.where(rowi == CHUNK - 1, dcl_row, 0.0)
        dcs_t_all = dcs.T + jnp.concatenate([dcs_t, jnp.zeros((LANES - N_HEADS, CHUNK), f32)], axis=0)
        dda = _dot32(dcs_t_all, tril.astype(f32)).T
        a = sc["a"]
        ddt_total = r2 + dda * a
        dalog_ref[...] += jnp.sum(dda * sc["dtv"], axis=0, keepdims=True) * a
        ddtr = jnp.where(lane < N_HEADS, ddt_total * _sigmoid(sc["xdt_pre"]), 0.0)
        ddtb_ref[...] += jnp.sum(ddtr, axis=0, keepdims=True)
        ddt_ref[...] = ddtr.astype(bf16)
        ddsk_ref[...] += small[2:3, :]
        row8 = _iota((_TAIL, _CONV_COLS), 0)
        for j in range(CONV_DIM // _CONV_COLS):
            sl = slice(j * _CONV_COLS, (j + 1) * _CONV_COLS)
            cvv = cv_ref[:, sl]
            sg = _sigmoid(cvv)
            dconv = dxc_ref[:, sl] * (sg * (1.0 + cvv * (1.0 - sg)))
            nxt = dcnext_ref[:, sl]
            cur = cur_ref[:, sl]
            dxin = dconv * cw_ref[CONV_W - 1:CONV_W, sl]
            dcw_ref[CONV_W - 1:CONV_W, sl] += jnp.sum(dconv * cur, axis=0, keepdims=True)
            for s in range(1, CONV_W):
                rolled = pltpu.roll(dconv, CHUNK - s, 0)
                bot = jnp.where(row8 < _TAIL - s, rolled[CHUNK - _TAIL:], pltpu.roll(nxt, _TAIL - s, 0))
                up = jnp.concatenate([rolled[:CHUNK - _TAIL], bot], axis=0)
                dxin = dxin + up * cw_ref[CONV_W - 1 - s:CONV_W - s, sl]
                dcw_ref[CONV_W - 1 - s:CONV_W - s, sl] += jnp.sum(up * cur, axis=0, keepdims=True)
            dcb_ref[:, sl] += jnp.sum(dconv, axis=0, keepdims=True)
            dxbc_ref[:, sl] = dxin.astype(bf16)
            dcnext_ref[:, sl] = dconv[:_TAIL]
        if ride is not None:
            pl.when(i == nc - 1)(finish)

    def chunk(w):
        return pl.BlockSpec((CHUNK, w), lambda i: (nc - 1 - i, 0))

    def const(shape):
        return pl.BlockSpec(shape, lambda i: (0,) * len(shape))

    riding = ride is not None
    return pl.pallas_call(
        body, name=name, grid=(nc,),
        in_specs=[chunk(CONV_DIM), chunk(CONV_DIM),
                  chunk(D_INNER), chunk(LANES), pl.BlockSpec((1, D_STATE, D_INNER), lambda i: (nc - 1 - i, 0, 0)),
                  chunk(D_INNER), const((CONV_W, CONV_DIM)),
                  const((1, LANES)), const((1, LANES)), const((1, D_INNER)), const((1, D_INNER)),
                  const((D_INNER, LANES))] + [_ANY] * riding,
        out_specs=[chunk(D_INNER), chunk(CONV_DIM), chunk(LANES), const((CONV_W, CONV_DIM)), const((1, CONV_DIM)),
                   const((1, LANES)), const((1, LANES)), const((1, LANES)), const((1, D_INNER))] + [_ANY] * riding,
        out_shape=[jax.ShapeDtypeStruct((t, D_INNER), bf16), jax.ShapeDtypeStruct((t, CONV_DIM), bf16),
                   jax.ShapeDtypeStruct((t, LANES), bf16), jax.ShapeDtypeStruct((CONV_W, CONV_DIM), f32),
                   jax.ShapeDtypeStruct((1, CONV_DIM), f32), jax.ShapeDtypeStruct((1, LANES), f32),
                   jax.ShapeDtypeStruct((1, LANES), f32), jax.ShapeDtypeStruct((1, LANES), f32),
                   jax.ShapeDtypeStruct((1, D_INNER), f32)]
        + ([jax.ShapeDtypeStruct((N_CHIPS - 1,) + ride.shape[1:], ride.dtype)] if riding else []),
        scratch_shapes=[pltpu.VMEM((D_STATE, D_INNER), f32), pltpu.VMEM((_TAIL, CONV_DIM), f32),
                        pltpu.VMEM((CHUNK, CONV_DIM), f32), pltpu.VMEM((CHUNK, CONV_DIM), f32),
                        pltpu.VMEM((CHUNK, D_INNER), f32), pltpu.VMEM((CHUNK, D_INNER), f32),
                        pltpu.VMEM((_TAIL, D_INNER), f32)]
        + (list(_SCATTER_SCRATCH) if riding else []),
        compiler_params=_params(("arbitrary",)),
    )(xbc, cv, z, dtr, hprev, dyb, cw, dtb, alog, dsk_x, gs, seg, *([ride] if riding else []))


def _adamw(w, g, m, v, *, name):
    r, c = w.shape
    tr = r
    while tr * c * 4 > 2 * _MB and tr % 16 == 0:
        tr //= 2

    def body(w_ref, g_ref, m_ref, v_ref, d_ref, m2_ref, v2_ref):
        gv = g_ref[...]
        m2 = ADAM_B1 * m_ref[...] + (1.0 - ADAM_B1) * gv
        v2 = ADAM_B2 * v_ref[...] + (1.0 - ADAM_B2) * (gv * gv)
        m_hat = m2 / (1.0 - ADAM_B1 ** ADAM_STEP)
        v_hat = v2 / (1.0 - ADAM_B2 ** ADAM_STEP)
        d_ref[...] = -ADAM_LR * (m_hat / (jnp.sqrt(v_hat) + ADAM_EPS) + ADAM_WD * w_ref[...])
        m2_ref[...] = m2
        v2_ref[...] = v2

    blk = pl.BlockSpec((tr, c), lambda i: (i, 0))
    return pl.pallas_call(
        body, name=name, grid=(r // tr,),
        in_specs=[blk] * 4, out_specs=[blk] * 3,
        out_shape=[jax.ShapeDtypeStruct((r, c), f32)] * 3,
        compiler_params=_params(("parallel",)),
    )(w, g, m, v)


def _row_block(rows, cols):
    cap = max(16, 2 * _MB // (4 * cols))
    return max(tr for tr in range(16, min(cap, rows) + 1, 16) if rows % tr == 0)


def _cast_bf16(a, *, name):
    r, c = a.shape
    tr = _row_block(r, c)

    def body(a_ref, o_ref):
        o_ref[...] = a_ref[...].astype(bf16)

    blk = pl.BlockSpec((tr, c), lambda i: (i, 0))
    return pl.pallas_call(
        body, name=name, grid=(r // tr,), in_specs=[blk], out_specs=blk,
        out_shape=jax.ShapeDtypeStruct((r, c), bf16), compiler_params=_params(("parallel",)),
    )(a)


_ANY = pl.BlockSpec(memory_space=pl.ANY)


def _place():
    x, y, c = lax.axis_index("x"), lax.axis_index("y"), lax.axis_index("c")
    other_chips = [(1 - x, y), (x, 1 - y), (1 - x, 1 - y)]
    return x, y, c, other_chips


def _gather_protocol(in_ref, out_ref, send_sems, recv_sems):
    x, y, c, chips = _place()
    me = 2 * x + y
    sibling = (x, y, 1 - c)
    where = [2 * cx + cy for cx, cy in chips]

    def cp(k, chip, half, to, src=None):
        dst = out_ref.at[chip, half]
        return pltpu.make_async_remote_copy(
            src_ref=dst if src is None else src, dst_ref=dst, send_sem=send_sems.at[k], recv_sem=recv_sems.at[k],
            device_id=to, device_id_type=MESH)

    def sends():
        return [cp(j, me, c, (*chips[j], c), src=in_ref.at[c]) for j in range(2)]

    def relays():
        return [cp(3 + j, where[j], c, sibling) for j in range(3)]

    def landed(j):
        return cp(j, where[j], c, sibling)

    def start():
        for f in sends():
            f.start()

    def relay():
        onward = relays()
        for first in range(2):
            @pl.when(c == first)
            def _(first=first):
                landed(first).wait_recv()
                cp(2, where[first], c, (*chips[1 - first], c)).start()
                onward[first].start()
                landed(1 - first).wait_recv()
                onward[1 - first].start()

    def finish():
        landed(2).wait_recv()
        relays()[2].start()
        for j in range(3):
            cp(3 + j, where[j], 1 - c, sibling).wait_recv()
        for f in sends() + [landed(2)] + relays():
            f.wait_send()

    return start, relay, finish


_GATHER_SCRATCH = [pltpu.SemaphoreType.DMA((6,)), pltpu.SemaphoreType.DMA((6,))]


def _gather_shards(shard, *, name):
    _, rh, lanes = shard.shape

    def body(in_ref, out_ref, send_sems, recv_sems):
        start, relay, finish = _gather_protocol(in_ref, out_ref, send_sems, recv_sems)
        start()
        relay()
        finish()

    return pl.pallas_call(
        body, name=name, in_specs=[_ANY], out_specs=_ANY,
        out_shape=jax.ShapeDtypeStruct((N_CHIPS, 2, rh, lanes), shard.dtype),
        scratch_shapes=list(_GATHER_SCRATCH),
    )(shard)


def _scatter_protocol(p_ref, out_ref, send_sems, recv_sems):
    x, y, c, chips = _place()

    def copies():
        return [pltpu.make_async_remote_copy(
            src_ref=p_ref.at[2 * cx + cy], dst_ref=out_ref.at[j], send_sem=send_sems.at[j], recv_sem=recv_sems.at[j],
            device_id=(cx, cy, c), device_id_type=MESH) for j, (cx, cy) in enumerate(chips)]

    def start():
        for cpy in copies():
            cpy.start()

    def finish():
        for cpy in copies():
            cpy.wait()

    return start, finish


_SCATTER_SCRATCH = [pltpu.SemaphoreType.DMA((3,)), pltpu.SemaphoreType.DMA((3,))]


def _swap_protocol(g_ref, out_ref, send_sems, recv_sems):
    x, y, c, _ = _place()

    def copies():
        return [pltpu.make_async_remote_copy(
            src_ref=g_ref.at[k, 1 - c], dst_ref=out_ref.at[k], send_sem=send_sems.at[k], recv_sem=recv_sems.at[k],
            device_id=(x, y, 1 - c), device_id_type=MESH) for k in range(N_CHIPS)]

    def start():
        for cpy in copies():
            cpy.start()

    def finish():
        for cpy in copies():
            cpy.wait()

    return start, finish


_SWAP_SCRATCH = [pltpu.SemaphoreType.DMA((N_CHIPS,)), pltpu.SemaphoreType.DMA((N_CHIPS,))]


def _rs_swap_halves(g, *, name):
    nch, _, rh, lanes = g.shape

    def body(g_ref, out_ref, send_sems, recv_sems):
        start, finish = _swap_protocol(g_ref, out_ref, send_sems, recv_sems)
        start()
        finish()

    return pl.pallas_call(
        body, name=name, in_specs=[_ANY], out_specs=_ANY,
        out_shape=jax.ShapeDtypeStruct((nch, rh, lanes), g.dtype),
        scratch_shapes=list(_SWAP_SCRATCH),
    )(g)


def _rs_add_pair(g, got, c_idx, *, name):
    nch, _, rh, lanes = g.shape
    tr = _row_block(rh, lanes)

    def body(c_ref, g_ref, got_ref, p16_ref):
        p16_ref[...] = (g_ref[...] + got_ref[...]).astype(bf16)

    blk = pl.BlockSpec((None, tr, lanes), lambda k, i, c_ref: (k, i, 0))
    return pl.pallas_call(
        body, name=name,
        grid_spec=pltpu.PrefetchScalarGridSpec(
            num_scalar_prefetch=1, grid=(nch, rh // tr),
            in_specs=[pl.BlockSpec((None, None, tr, lanes), lambda k, i, c_ref: (k, c_ref[0], i, 0)), blk],
            out_specs=blk),
        out_shape=jax.ShapeDtypeStruct((nch, rh, lanes), bf16),
        compiler_params=_params(("parallel", "parallel")),
    )(c_idx, g, got)


def _rs_add_chips(g, got_pair, got, place, *, name):
    _, _, rh, lanes = g.shape
    tr = _row_block(rh, lanes)

    def body(place_ref, g_ref, pair_ref, got_ref, o_ref):
        own = g_ref[...] + pair_ref[...]
        o_ref[...] = ((own + got_ref[0].astype(f32)) + got_ref[1].astype(f32)) + got_ref[2].astype(f32)

    return pl.pallas_call(
        body, name=name,
        grid_spec=pltpu.PrefetchScalarGridSpec(
            num_scalar_prefetch=1, grid=(rh // tr,),
            in_specs=[pl.BlockSpec((None, None, tr, lanes), lambda i, place_ref: (place_ref[0], place_ref[1], i, 0)),
                      pl.BlockSpec((None, tr, lanes), lambda i, place_ref: (place_ref[0], i, 0)),
                      pl.BlockSpec((3, tr, lanes), lambda i, place_ref: (0, i, 0))],
            out_specs=pl.BlockSpec((None, tr, lanes), lambda i, place_ref: (place_ref[1], i, 0))),
        out_shape=jax.ShapeDtypeStruct((2, rh, lanes), f32),
        compiler_params=_params(("parallel",)),
    )(place, g, got_pair, got)


def _rs_join_halves(halves, *, name):
    def body(h_ref, out_ref, send_sem, recv_sem):
        x, y, c, _ = _place()
        cpy = pltpu.make_async_remote_copy(
            src_ref=h_ref.at[c], dst_ref=out_ref.at[c], send_sem=send_sem, recv_sem=recv_sem,
            device_id=(x, y, 1 - c), device_id_type=MESH)
        cpy.start()
        cpy.wait()

    return pl.pallas_call(
        body, name=name, in_specs=[_ANY], out_specs=_ANY,
        out_shape=jax.ShapeDtypeStruct(halves.shape, halves.dtype), input_output_aliases={0: 0},
        scratch_shapes=[pltpu.SemaphoreType.DMA, pltpu.SemaphoreType.DMA],
    )(halves)


def _all_reduce_small(s, *, name):
    rs, lanes = s.shape
    rh = rs // 2

    def body(s_ref, o_ref, sib_ref, mine_ref, chips_ref, send_sems, recv_sems):
        x, y, c, chips = _place()
        me = 2 * x + y
        sibling = (x, y, 1 - c)
        rows = pl.ds(pl.multiple_of(c * rh, 8), rh)

        def cp(k, src, dst, to):
            return pltpu.make_async_remote_copy(src_ref=src, dst_ref=dst, send_sem=send_sems.at[k],
                                                recv_sem=recv_sems.at[k], device_id=to, device_id_type=MESH)

        swap = cp(0, s_ref, sib_ref, sibling)
        swap.start()
        swap.wait()
        mine_ref[...] = s_ref[rows, :] + sib_ref[rows, :]
        sends = [cp(1 + j, mine_ref, chips_ref.at[j], (cx, cy, c)) for j, (cx, cy) in enumerate(chips)]
        for cpy in sends:
            cpy.start()
        for cpy in sends:
            cpy.wait()
        where = [2 * cx + cy for cx, cy in chips]
        total = None
        for q in range(N_CHIPS):
            term = jnp.where(q == me, mine_ref[...], jnp.where(
                q == where[0], chips_ref[0], jnp.where(q == where[1], chips_ref[1], chips_ref[2])))
            total = term if total is None else total + term
        o_ref[rows, :] = total
        push = cp(4, o_ref.at[rows, :], o_ref.at[rows, :], sibling)
        push.start()
        push.wait()

    vm = pl.BlockSpec(memory_space=pltpu.VMEM)
    return pl.pallas_call(
        body, name=name, in_specs=[vm], out_specs=vm,
        out_shape=jax.ShapeDtypeStruct((rs, lanes), f32),
        scratch_shapes=[pltpu.VMEM((rs, lanes), f32), pltpu.VMEM((rh, lanes), f32),
                        pltpu.VMEM((N_CHIPS - 1, rh, lanes), f32), pltpu.SemaphoreType.DMA((5,)),
                        pltpu.SemaphoreType.DMA((5,))],
        compiler_params=pltpu.CompilerParams(vmem_limit_bytes=32 * _MB),
    )(s)


def _pad_lanes(a, width=LANES):
    return jnp.pad(a, ((0, 0), (0, width - a.shape[1])))


def _local_grads(x, tgt, wts, small, *, fwd_ride=None, late_weights=None, swap_ride=None, bwd_ride=None,
                 last_ride=None):
    t = x.shape[0]
    tm = min(t, 1024)
    d = D_MODEL
    mm = functools.partial(_matmul, tm=tm)

    dtb = _pad_lanes(small["dt_bias"])
    alog = _pad_lanes(small["a_log"])
    dsk = jnp.repeat(small["d_skip"], HEAD_DIM, axis=1)
    bsp_t = _pad_lanes(small["b_spatial"].T)
    wsp = small["w_spatial"]

    h = _rms_fwd(x, small["norm_mix_g"], name="rms_mix")
    uv = mm(h, wts["uv"], tn=2048, tk=d, out_dtypes=[f32], name="proj_uv")
    z = mm(h, wts["z"], tn=2048, tk=d, out_dtypes=[f32], name="proj_z")
    xbc = mm(h, wts["xbc"], tn=2048, tk=d, out_dtypes=[f32], name="proj_xbc")
    dtr = mm(h, wts["dt"], tn=LANES, tk=d, out_dtypes=[f32], name="proj_dt")
    gl = mm(h, wts["gate"], tn=2048, tk=d, out_dtypes=[f32], name="proj_gate")
    ya = _gmlp_fwd(uv, small["v_norm_g"], small["v_norm_b"], wsp, bsp_t, name="gmlp_fwd")
    yb, hprev, cv, *gathered = _ssd_fwd(xbc, z, dtr, small["conv_w"], small["conv_b"], dtb, alog, dsk,
                                        small["ssm_norm_g"], ride=fwd_ride, name="ssd_fwd")
    if fwd_ride is not None:
        wts = {**wts, **late_weights(gathered[0])}
    tall = functools.partial(_matmul, tm=min(t, 2048))
    pa = tall(ya, wts["pa"], tn=1024, tk=1024, out_dtypes=[f32], name="proj_a")
    tm_gate = min(t, 512)
    row_vec = [pl.BlockSpec((1, d), lambda i, j, k, half=half: (0, half)) for half in range(2)]
    gate_tiles = [pl.BlockSpec((tm_gate, d), lambda i, j, k, half=half: (i, half)) for half in range(2)]

    def merge(pb_acc, pa_t, gla, glb, bga, bgb):
        return pb_acc, _sigmoid(gla + bga) * pa_t + _sigmoid(glb + bgb) * pb_acc

    pb, merged = _matmul(yb, wts["pb"], tm=tm_gate, tn=d, tk=1024, out_dtypes=[f32, bf16], epilogue=merge,
                         extras=[pa, gl, gl, small["b_gates"], small["b_gates"]],
                         extra_specs=[None] + gate_tiles + row_vec, name="proj_b")

    def residual_norm(acc, res, g):
        x_new = res + acc
        r = lax.rsqrt(jnp.mean(x_new * x_new, axis=1, keepdims=True) + NORM_EPS)
        return x_new, x_new * r * g

    x1, h2 = mm(merged, wts["out"], tn=d, tk=1024, out_dtypes=[f32, bf16], epilogue=residual_norm,
                extras=[x, small["norm_mlp_g"]], extra_specs=[None, row_vec[0]], name="out_proj")
    act = mm(h2, wts["up"], tn=2048, tk=d, out_dtypes=[bf16],
             epilogue=lambda acc: (jnp.square(jnp.maximum(acc, 0.0)),), name="mlp_up")
    x2 = mm(act, wts["down"], tn=1024, tk=2048, out_dtypes=[f32], extras=[x1],
            epilogue=lambda acc, res: (res + acc,), name="mlp_down")

    dx2, dx2b, dgf, loss = _loss_head(x2, tgt, small["norm_final_g"], name="loss_head")
    tt = min(t, 2048)
    tn_mm = functools.partial(_matmul_tn, tt=tt)
    dw = {}
    def slab(key, chip_of):
        rows = _LATE_ROWS[key]
        return dict(tka=min(rows, 1024), tn=1024, place=lambda i, j: (chip_of(i, j), _LATE_OFF[key] // min(rows, 1024)))

    dw["late"] = tn_mm(act, dx2b, name="dw_down", **slab("w_mlp_down", lambda i, j: i))
    dup = mm(dx2b, wts["down"], nt=True, tn=2048, tk=1024, out_dtypes=[bf16], extras=[act],
             epilogue=lambda acc, a2: (acc * (2.0 * jnp.sqrt(a2).astype(f32)),), name="d_act")
    dw["late"] = tn_mm(h2, dup, name="dw_up", packed=dw["late"], **slab("w_mlp_up", lambda i, j: j))
    dh2 = mm(dup, wts["up"], nt=True, tn=1024, tk=2048, out_dtypes=[f32], name="d_h2")
    dx1, dx1b, dg_mlp = _rms_bwd(x1, small["norm_mlp_g"], dh2, dx2, want_bf16=True, name="rms_mlp_bwd")
    dw["late"] = tn_mm(merged, dx1b, name="dw_out", packed=dw["late"], **slab("w_out", lambda i, j: i))
    dmerged = tall(dx1b, wts["out"], nt=True, tn=1024, tk=1024, out_dtypes=[f32], name="d_merged")
    dpa, dpb, dgl, dbg = _merge_bwd(dmerged, pa, pb, gl, small["b_gates"], name="merge_bwd")
    dw["late"] = tn_mm(ya, dpa, name="dw_pa", packed=dw["late"], **slab("w_proj_a", lambda i, j: i))
    dw["late"] = tn_mm(yb, dpb, name="dw_pb", packed=dw["late"], **slab("w_proj_b", lambda i, j: i))
    dya = tall(dpa, wts["pa"], nt=True, tn=1024, tk=1024, out_dtypes=[f32], name="d_ya")
    dyb = mm(dpb, wts["pb"], nt=True, tn=2048, tk=1024, out_dtypes=[f32], name="d_yb")
    swapped = swap_ride(dw) if swap_ride is not None else None
    duv, dwsp, dbsp_t, dvg, dvb, *got_pair = _gmlp_bwd(uv, dya, small["v_norm_g"], small["v_norm_b"], wsp, bsp_t,
                                                       ride=swapped, name="gmlp_bwd")
    ride = bwd_ride(swapped, got_pair[0]) if bwd_ride is not None else None
    dz, dxbc, ddt, dcw, dcb, ddtb, dalog, ddsk, dgs, *got = _ssd_bwd(
        xbc, cv, z, dtr, hprev, dyb, small["conv_w"], dtb, alog, dsk, small["ssm_norm_g"],
        _head_seg_matrix(), ride=ride, name="ssd_bwd")
    dw["uv"] = tn_mm(h, duv, tka=1024, tn=1024, name="dw_uv")
    dw["z"] = tn_mm(h, dz, tka=1024, tn=1024, name="dw_z")
    dw["xbc"] = tn_mm(h, dxbc, tka=1024, tn=1024, name="dw_xbc")
    dw["dt"] = tn_mm(h, ddt, tka=1024, tn=LANES, name="dw_dt")
    dw["gate"] = tn_mm(h, dgl, tka=1024, tn=1024, name="dw_gate")
    last = last_ride(dw) if last_ride is not None else None
    res = _matmul_nt_sum(
        [(duv, wts["uv"]), (dz, wts["z"]), (dxbc, wts["xbc"]), (dgl, wts["gate"]), (ddt, wts["dt"])],
        tm=tm, tks=[1024] * 4 + [LANES], ride=last, name="d_h")
    dh, got_last = (res[0], res[1]) if last is not None else (res, None)
    dx, dg_mix = _rms_bwd(x, small["norm_mix_g"], dh, dx1, want_bf16=False, name="rms_mix_bwd")

    dsmall = {
        "norm_mix_g": dg_mix, "conv_w": dcw, "conv_b": dcb, "dt_bias": ddtb[:, :N_HEADS], "a_log": dalog[:, :N_HEADS],
        "d_skip": ddsk[:, :N_HEADS], "ssm_norm_g": dgs, "v_norm_g": dvg, "v_norm_b": dvb, "w_spatial": dwsp,
        "b_spatial": dbsp_t[:, :GMLP_GROUPS].T, "b_gates": dbg, "norm_mlp_g": dg_mlp, "norm_final_g": dgf,
    }
    return loss, dx, dw, dsmall, (got[0] if got else None), got_last


_IN_SHARD = IN_PROJ // N_CHIPS
_LATE = ("w_mlp_down", "w_mlp_up", "w_proj_b", "w_proj_a", "w_out")
_LATE_ROWS = {"w_proj_a": GMLP_WIDTH // N_CHIPS, "w_proj_b": D_INNER // N_CHIPS, "w_out": D_MODEL // N_CHIPS,
              "w_mlp_up": D_MODEL, "w_mlp_down": D_FF // N_CHIPS}
_LATE_TOTAL = sum(_LATE_ROWS.values())


def _late_offsets():
    off, out = 0, {}
    for k in _LATE:
        out[k] = off
        off += _LATE_ROWS[k]
    return out


_LATE_OFF = _late_offsets()

_SMALL = ("norm_mix_g", "conv_w", "conv_b", "dt_bias", "a_log", "d_skip", "ssm_norm_g", "v_norm_g", "v_norm_b",
          "w_spatial", "b_spatial", "b_gates", "norm_mlp_g", "norm_final_g")


def _pack_small(parts):
    flat = jnp.concatenate([parts[k].reshape(-1) for k in _SMALL])
    rows = -(-flat.shape[0] // (16 * LANES)) * 16
    return jnp.pad(flat, (0, rows * LANES - flat.shape[0])).reshape(rows, LANES)


def _unpack_small(packed, shapes):
    flat = packed.reshape(-1)
    out, off = {}, 0
    for k in _SMALL:
        n = math.prod(shapes[k])
        out[k] = flat[off:off + n].reshape(shapes[k])
        off += n
    return out


def _from_chip_columns(stacked):
    _, rows, cols = stacked.shape
    return stacked.transpose(1, 0, 2).reshape(rows, N_CHIPS * cols)


def _to_chip_columns(full):
    rows, cols = full.shape
    return full.reshape(rows, N_CHIPS, cols // N_CHIPS).transpose(1, 0, 2)


def _w_in_grad_by_chip(dw):
    pieces = [dw["uv"], dw["z"], dw["xbc"], dw["dt"][:, :N_HEADS], dw["gate"]]
    bounds = [0]
    for p in pieces:
        bounds.append(bounds[-1] + p.shape[1])
    chips = []
    for k in range(N_CHIPS):
        lo, hi = k * _IN_SHARD, (k + 1) * _IN_SHARD
        parts = [p[:, max(lo, b0) - b0:min(hi, b1) - b0]
                 for p, b0, b1 in zip(pieces, bounds[:-1], bounds[1:]) if min(hi, b1) > max(lo, b0)]
        chips.append(jnp.concatenate(parts, axis=1))
    return jnp.stack(chips)


def kernel(x, norm_mix_g, w_in, conv_w, conv_b, dt_bias, a_log, d_skip, ssm_norm_g, v_norm_g, v_norm_b, w_spatial, b_spatial, b_gates, w_proj_a, w_proj_b, w_out, norm_mlp_g, w_mlp_up, w_mlp_down, norm_final_g, loss_target, m_norm_mix_g, m_w_in, m_conv_w, m_conv_b, m_dt_bias, m_a_log, m_d_skip, m_ssm_norm_g, m_v_norm_g, m_v_norm_b, m_w_spatial, m_b_spatial, m_b_gates, m_w_proj_a, m_w_proj_b, m_w_out, m_norm_mlp_g, m_w_mlp_up, m_w_mlp_down, m_norm_final_g, v_norm_mix_g, v_w_in, v_conv_w, v_conv_b, v_dt_bias, v_a_log, v_d_skip, v_ssm_norm_g, v_v_norm_g, v_v_norm_b, v_w_spatial, v_b_spatial, v_b_gates, v_w_proj_a, v_w_proj_b, v_w_out, v_norm_mlp_g, v_w_mlp_up, v_w_mlp_down, v_norm_final_g):
    given = dict(locals())
    names = ("norm_mix_g", "w_in", "conv_w", "conv_b", "dt_bias", "a_log", "d_skip", "ssm_norm_g", "v_norm_g",
             "v_norm_b", "w_spatial", "b_spatial", "b_gates", "w_proj_a", "w_proj_b", "w_out", "norm_mlp_g",
             "w_mlp_up", "w_mlp_down", "norm_final_g")
    xi, yi, ci = lax.axis_index("x"), lax.axis_index("y"), lax.axis_index("c")
    me_chip = (2 * xi + yi).astype(jnp.int32)

    def halves(a):
        return a.reshape(2, a.shape[0] // 2, a.shape[1])

    def with_own(got, shard):
        whole = lax.dynamic_update_slice(got, shard[None], (me_chip, 0, 0, 0))
        return whole.reshape(N_CHIPS, 2 * shard.shape[1], shard.shape[2])

    shard_in = halves(_cast_bf16(w_in[0], name="cast_w_in"))
    shard_late = halves(_cast_bf16(jnp.concatenate([given[k][0] for k in _LATE]), name="cast_w_late"))
    shard_conv = halves(conv_w.reshape(2 * _TAIL, -1))
    w_in_full = _from_chip_columns(with_own(_gather_shards(shard_in, name="gather_w_in"), shard_in))
    o_dt, o_gate = 2 * GMLP_WIDTH + D_INNER + CONV_DIM, 2 * GMLP_WIDTH + D_INNER + CONV_DIM + N_HEADS
    wts = {
        "uv": w_in_full[:, :2 * GMLP_WIDTH], "z": w_in_full[:, 2 * GMLP_WIDTH:2 * GMLP_WIDTH + D_INNER],
        "xbc": w_in_full[:, 2 * GMLP_WIDTH + D_INNER:o_dt], "dt": _pad_lanes(w_in_full[:, o_dt:o_gate]),
        "gate": w_in_full[:, o_gate:],
    }
    conv_all = with_own(_gather_shards(shard_conv, name="gather_conv_w"), shard_conv)
    conv_full = _from_chip_columns(conv_all.reshape(N_CHIPS, CONV_W, CONV_DIM // N_CHIPS))

    def late_weights(got):
        g_late = with_own(got, shard_late)

        def rows_of(k):
            return g_late[:, _LATE_OFF[k]:_LATE_OFF[k] + _LATE_ROWS[k]]

        return {
            "pa": rows_of("w_proj_a").reshape(GMLP_WIDTH, D_MODEL),
            "pb": rows_of("w_proj_b").reshape(D_INNER, D_MODEL), "out": rows_of("w_out").reshape(D_MODEL, D_MODEL),
            "up": _from_chip_columns(rows_of("w_mlp_up")), "down": rows_of("w_mlp_down").reshape(D_FF, D_MODEL),
        }

    small = {
        "norm_mix_g": norm_mix_g, "conv_w": conv_full, "conv_b": conv_b, "dt_bias": dt_bias, "a_log": a_log,
        "d_skip": d_skip, "ssm_norm_g": ssm_norm_g, "v_norm_g": v_norm_g, "v_norm_b": v_norm_b,
        "w_spatial": w_spatial[0], "b_spatial": b_spatial[0], "b_gates": b_gates, "norm_mlp_g": norm_mlp_g,
        "norm_final_g": norm_final_g.reshape(1, D_MODEL),
    }

    c_idx = ci.astype(jnp.int32).reshape(1)
    place = jnp.stack([me_chip, ci.astype(jnp.int32)])
    partials = {}

    def reduced_shard(tag, got_chips):
        own = _rs_add_chips(*partials[tag], got_chips, place, name="rs_add_chips_" + tag)
        both = _rs_join_halves(own, name="rs_join_" + tag)
        return both.reshape(2 * both.shape[1], both.shape[2])

    def late_grads(dw):
        return dw["late"].reshape(N_CHIPS, 2, _LATE_TOTAL // 2, D_MODEL)

    def late_partials(g, got_pair):
        partials["late"] = (g, got_pair)
        return _rs_add_pair(g, got_pair, c_idx, name="rs_add_pair_late")

    def in_partials(dw):
        g = _w_in_grad_by_chip(dw).reshape(N_CHIPS, 2, D_MODEL // 2, _IN_SHARD)
        got_pair = _rs_swap_halves(g, name="rs_swap_in")
        partials["in"] = (g, got_pair)
        return _rs_add_pair(g, got_pair, c_idx, name="rs_add_pair_in")

    loss_part, grad_x, dw, dsmall, got_late, got_in = _local_grads(
        x[0], loss_target[0], wts, small, fwd_ride=shard_late, late_weights=late_weights, swap_ride=late_grads,
        bwd_ride=late_partials, last_ride=in_partials)
    loss = lax.psum(loss_part[0, 0], ("x", "y", "c"))
    g_late = reduced_shard("late", got_late)
    g_in_shard = reduced_shard("in", got_in)

    small_shapes = {k: dsmall[k].shape for k in _SMALL}
    red = _unpack_small(_all_reduce_small(_pack_small(dsmall), name="all_reduce_small"), small_shapes)
    conv_cols = CONV_DIM // N_CHIPS
    red["conv_w"] = lax.dynamic_slice_in_dim(red["conv_w"], me_chip * conv_cols, conv_cols, axis=1)

    grads, deltas, new_m, new_v = {}, {}, {}, {}
    for k in ("w_in",) + _LATE:
        g2 = g_in_shard if k == "w_in" else g_late[_LATE_OFF[k]:_LATE_OFF[k] + _LATE_ROWS[k]]
        dlt, m2, v2 = _adamw(given[k][0], g2, given["m_" + k][0], given["v_" + k][0], name="adamw_" + k)
        grads[k], deltas[k], new_m[k], new_v[k] = g2, dlt, m2, v2
    adam_shapes = dict(small_shapes)
    adam_shapes["conv_w"] = (CONV_W, conv_cols)

    def small_pack_of(prefix):
        return _pack_small({k: given[prefix + k].reshape(adam_shapes[k]) for k in _SMALL})

    dlt_s, m_s, v_s = _adamw(small_pack_of(""), _pack_small(red), small_pack_of("m_"), small_pack_of("v_"),
                             name="adamw_small")
    for dst, packed in ((deltas, dlt_s), (new_m, m_s), (new_v, v_s)):
        dst.update(_unpack_small(packed, adam_shapes))
    grads.update(red)

    def shaped(dct):
        return [dct[k].reshape(given[k].shape) for k in names]

    return (loss, grad_x[None], *shaped(grads), *shaped(deltas), *shaped(new_m), *shaped(new_v))
```

```python
import functools
import math

import jax
import jax.numpy as jnp
from jax import lax
from jax.experimental import pallas as pl
from jax.experimental.pallas import tpu as pltpu

f32 = jnp.float32
bf16 = jnp.bfloat16

D_MODEL = 1024
CHUNK = 128
GMLP_WIDTH = 1024
GMLP_GROUPS = 8
D_INNER = 2048
HEAD_DIM = 64
N_HEADS = 32
N_GROUPS = 8
HEADS_PER_GROUP = 4
GROUP_W = HEADS_PER_GROUP * HEAD_DIM
D_STATE = 128
CONV_W = 4
CONV_DIM = 4096
D_FF = 4096
IN_PROJ = 10272
NORM_EPS = 1e-6
N_CHIPS = 4
N_DEV = 8
LANES = 128

ADAM_LR = 0.001
ADAM_B1 = 0.9
ADAM_B2 = 0.999
ADAM_EPS = 1e-08
ADAM_WD = 0.01
ADAM_STEP = 10

MESH = pl.DeviceIdType.MESH
_NT = (((1,), (1,)), ((), ()))
_NN = (((1,), (0,)), ((), ()))
_TN = (((0,), (0,)), ((), ()))
_MB = 2 ** 20


def _params(sem, vmem_mb=48):
    return pltpu.CompilerParams(dimension_semantics=sem, vmem_limit_bytes=vmem_mb * _MB)


def _dot(a, b, dims=_NN):
    return lax.dot_general(a.astype(bf16), b.astype(bf16), dims, preferred_element_type=f32)


def _dot32(a, b):
    return jnp.dot(a, b, preferred_element_type=f32, precision=lax.Precision.HIGHEST)


def _sigmoid(x):
    return 1.0 / (1.0 + jnp.exp(-x))


def _sum_all(a):
    return jnp.sum(jnp.sum(a, axis=1, keepdims=True), axis=0, keepdims=True)


def _iota(shape, dim):
    return lax.broadcasted_iota(jnp.int32, shape, dim)


def _matmul(a, b, *, nt=False, tm, tn, tk, out_dtypes, epilogue=None, extras=(), extra_specs=None, name):
    m, k_dim = a.shape
    n = b.shape[0] if nt else b.shape[1]
    nk = k_dim // tk
    ne, no = len(extras), len(out_dtypes)
    dims = _NT if nt else _NN

    def body(*refs):
        a_ref, b_ref = refs[0], refs[1]
        ex = refs[2:2 + ne]
        outs = refs[2 + ne:2 + ne + no]

        def finish(acc):
            vals = epilogue(acc, *[e[...] for e in ex]) if epilogue is not None else (acc,)
            for o, v in zip(outs, vals):
                o[...] = v.astype(o.dtype)

        part = lax.dot_general(a_ref[...], b_ref[...], dims, preferred_element_type=f32)
        if nk == 1:
            finish(part)
        else:
            acc_ref = refs[-1]
            kk = pl.program_id(2)

            @pl.when(kk == 0)
            def _():
                acc_ref[...] = part

            @pl.when(kk > 0)
            def _():
                acc_ref[...] += part

            @pl.when(kk == nk - 1)
            def _():
                finish(acc_ref[...])

    b_spec = pl.BlockSpec((tn, tk), lambda i, j, k: (j, k)) if nt else pl.BlockSpec((tk, tn), lambda i, j, k: (k, j))
    tile = pl.BlockSpec((tm, tn), lambda i, j, k: (i, j))
    ex_specs = [tile if s is None else s for s in (extra_specs or [None] * ne)]
    outs = pl.pallas_call(
        body, name=name, grid=(m // tm, n // tn, nk),
        in_specs=[pl.BlockSpec((tm, tk), lambda i, j, k: (i, k)), b_spec] + ex_specs,
        out_specs=[tile] * no,
        out_shape=[jax.ShapeDtypeStruct((m, n), dt) for dt in out_dtypes],
        scratch_shapes=[pltpu.VMEM((tm, tn), f32)] if nk > 1 else [],
        compiler_params=_params(("parallel", "parallel", "arbitrary")),
    )(a, b, *extras)
    return outs if no > 1 else outs[0]


def _matmul_nt_sum(pairs, *, tm, tks, ride=None, name):
    m = pairs[0][0].shape[0]
    n = pairs[0][1].shape[0]
    nblk = [a.shape[1] // tk for (a, _), tk in zip(pairs, tks)]
    starts = [sum(nblk[:p]) for p in range(len(pairs))]
    nk = sum(nblk)
    npairs = len(pairs)
    ni = m // tm
    riding = ride is not None

    def body(*refs):
        rest = refs[2 * npairs:]
        if riding:
            ride_ref, o_ref, got_ref, acc_ref, send_sems, recv_sems = rest
        else:
            o_ref, acc_ref = rest
        i, kk = pl.program_id(0), pl.program_id(1)
        if riding:
            start, finish = _scatter_protocol(ride_ref, got_ref, send_sems, recv_sems)
            pl.when((i == 0) & (kk == 0))(start)

        @pl.when(kk == 0)
        def _():
            acc_ref[...] = jnp.zeros_like(acc_ref)

        for p in range(npairs):
            @pl.when((kk >= starts[p]) & (kk < starts[p] + nblk[p]))
            def _(p=p):
                acc_ref[...] += lax.dot_general(refs[2 * p][...], refs[2 * p + 1][...], _NT, preferred_element_type=f32)

        @pl.when(kk == nk - 1)
        def _():
            o_ref[...] = acc_ref[...]

        if riding:
            pl.when((i == ni - 1) & (kk == nk - 1))(finish)

    in_specs, args = [], []
    for p, (a, b) in enumerate(pairs):
        def kblock(k, s=starts[p], nb=nblk[p]):
            return jnp.clip(k - s, 0, nb - 1)
        in_specs.append(pl.BlockSpec((tm, tks[p]), lambda i, k, kb=kblock: (i, kb(k))))
        in_specs.append(pl.BlockSpec((n, tks[p]), lambda i, k, kb=kblock: (0, kb(k))))
        args += [a, b]
    tile = pl.BlockSpec((tm, n), lambda i, k: (i, 0))
    outs = pl.pallas_call(
        body, name=name, grid=(ni, nk), in_specs=in_specs + [_ANY] * riding, out_specs=[tile] + [_ANY] * riding,
        out_shape=[jax.ShapeDtypeStruct((m, n), f32)]
        + ([jax.ShapeDtypeStruct((N_CHIPS - 1,) + ride.shape[1:], ride.dtype)] if riding else []),
        scratch_shapes=[pltpu.VMEM((tm, n), f32)] + (list(_SCATTER_SCRATCH) if riding else []),
        compiler_params=_params(("arbitrary", "arbitrary"), vmem_mb=56),
    )(*args, *([ride] if riding else []))
    return outs if riding else outs[0]


ALL_CHIPS = "all"


def _matmul_tn(a, b, *, tka, tn, tt, name, packed=None, place=None):
    t, ka = a.shape
    n = b.shape[1]
    spread = place is not None and place(0, 0)[0] is ALL_CHIPS

    def body(a_ref, b_ref, *rest):
        o_ref = rest[-1]
        part = lax.dot_general(a_ref[...], b_ref[...], _TN, preferred_element_type=f32).reshape(o_ref.shape)
        kk = pl.program_id(2)

        @pl.when(kk == 0)
        def _():
            o_ref[...] = part

        @pl.when(kk > 0)
        def _():
            o_ref[...] += part

    in_specs = [pl.BlockSpec((tt, tka), lambda i, j, k: (k, i)), pl.BlockSpec((tt, tn), lambda i, j, k: (k, j))]
    if place is None:
        out_spec = pl.BlockSpec((tka, tn), lambda i, j, k: (i, j))
        out_shape = jax.ShapeDtypeStruct((ka, n), f32)
    elif spread:
        out_spec = pl.BlockSpec((N_CHIPS, tka // N_CHIPS, tn), lambda i, j, k: (0, place(i, j)[1], 0))
        out_shape = jax.ShapeDtypeStruct((N_CHIPS, _LATE_TOTAL, tn), f32)
    else:
        out_spec = pl.BlockSpec((None, tka, tn), lambda i, j, k: (*place(i, j), 0))
        out_shape = jax.ShapeDtypeStruct((N_CHIPS, _LATE_TOTAL, tn), f32)
    aliased = packed is not None
    return pl.pallas_call(
        body, name=name, grid=(ka // tka, n // tn, t // tt),
        in_specs=in_specs + [_ANY] * aliased, out_specs=out_spec, out_shape=out_shape,
        input_output_aliases={2: 0} if aliased else {},
        compiler_params=_params(("parallel", "parallel", "arbitrary")),
    )(a, b, *([packed] if aliased else []))


def _row_tile(t):
    return min(t, 512)


def _rms_fwd(x, g, *, name):
    t, d = x.shape
    tr = _row_tile(t)

    def body(x_ref, g_ref, h_ref):
        xv = x_ref[...]
        r = lax.rsqrt(jnp.mean(xv * xv, axis=1, keepdims=True) + NORM_EPS)
        h_ref[...] = (xv * r * g_ref[...]).astype(bf16)

    return pl.pallas_call(
        body, name=name, grid=(t // tr,),
        in_specs=[pl.BlockSpec((tr, d), lambda i: (i, 0)), pl.BlockSpec((1, d), lambda i: (0, 0))],
        out_specs=pl.BlockSpec((tr, d), lambda i: (i, 0)),
        out_shape=jax.ShapeDtypeStruct((t, d), bf16),
        compiler_params=_params(("parallel",)),
    )(x, g)


def _rms_bwd(xin, g, dh, dres, *, want_bf16, name):
    t, d = xin.shape
    tr = _row_tile(t)

    def body(x_ref, g_ref, dh_ref, dres_ref, dx_ref, *rest):
        dg_ref = rest[-1]
        xv = x_ref[...]
        r = lax.rsqrt(jnp.mean(xv * xv, axis=1, keepdims=True) + NORM_EPS)
        xn = xv * r
        dhv = dh_ref[...]
        dxn = dhv * g_ref[...]
        dx = dres_ref[...] + r * (dxn - xn * jnp.mean(dxn * xn, axis=1, keepdims=True))
        dx_ref[...] = dx
        if want_bf16:
            rest[0][...] = dx.astype(bf16)
        part = jnp.sum(dhv * xn, axis=0, keepdims=True)

        @pl.when(pl.program_id(0) == 0)
        def _():
            dg_ref[...] = part

        @pl.when(pl.program_id(0) > 0)
        def _():
            dg_ref[...] += part

    row = pl.BlockSpec((tr, d), lambda i: (i, 0))
    vec = pl.BlockSpec((1, d), lambda i: (0, 0))
    out_shape = [jax.ShapeDtypeStruct((t, d), f32)] + ([jax.ShapeDtypeStruct((t, d), bf16)] if want_bf16 else []) \
        + [jax.ShapeDtypeStruct((1, d), f32)]
    return pl.pallas_call(
        body, name=name, grid=(t // tr,),
        in_specs=[row, vec, row, row],
        out_specs=[row] + ([row] if want_bf16 else []) + [vec],
        out_shape=out_shape,
        compiler_params=_params(("arbitrary",)),
    )(xin, g, dh, dres)


def _loss_head(x2, tgt, g, *, name):
    t, d = x2.shape
    tr = _row_tile(t)

    def body(x_ref, t_ref, g_ref, dx_ref, dxb_ref, dg_ref, loss_ref):
        xv = x_ref[...]
        gv = g_ref[...]
        r = lax.rsqrt(jnp.mean(xv * xv, axis=1, keepdims=True) + NORM_EPS)
        xn = xv * r
        e = xn * gv - t_ref[...]
        lpart = jnp.zeros((1, LANES), f32) + 0.5 * _sum_all(jnp.mean(e * e, axis=1, keepdims=True))
        dy = e * (1.0 / d)
        dxn = dy * gv
        dx = r * (dxn - xn * jnp.mean(dxn * xn, axis=1, keepdims=True))
        dx_ref[...] = dx
        dxb_ref[...] = dx.astype(bf16)
        gpart = jnp.sum(dy * xn, axis=0, keepdims=True)

        @pl.when(pl.program_id(0) == 0)
        def _():
            dg_ref[...] = gpart
            loss_ref[...] = lpart

        @pl.when(pl.program_id(0) > 0)
        def _():
            dg_ref[...] += gpart
            loss_ref[...] += lpart

    row = pl.BlockSpec((tr, d), lambda i: (i, 0))
    vec = pl.BlockSpec((1, d), lambda i: (0, 0))
    return pl.pallas_call(
        body, name=name, grid=(t // tr,),
        in_specs=[row, row, vec],
        out_specs=[row, row, vec, pl.BlockSpec((1, LANES), lambda i: (0, 0))],
        out_shape=[jax.ShapeDtypeStruct((t, d), f32), jax.ShapeDtypeStruct((t, d), bf16),
                   jax.ShapeDtypeStruct((1, d), f32), jax.ShapeDtypeStruct((1, LANES), f32)],
        compiler_params=_params(("arbitrary",)),
    )(x2, tgt, g)


def _merge_bwd(dm, pa, pb, gl, bg, *, name):
    t, d = pa.shape
    tr = _row_tile(t)

    def body(dm_ref, pa_ref, pb_ref, gla_ref, glb_ref, bga_ref, bgb_ref, dpa_ref, dpb_ref, dgl_ref, dbg_ref):
        dmv = dm_ref[...]
        ga = _sigmoid(gla_ref[...] + bga_ref[...])
        gb = _sigmoid(glb_ref[...] + bgb_ref[...])
        dpa_ref[...] = (dmv * ga).astype(bf16)
        dpb_ref[...] = (dmv * gb).astype(bf16)
        dla = dmv * pa_ref[...] * ga * (1.0 - ga)
        dlb = dmv * pb_ref[...] * gb * (1.0 - gb)
        dgl_ref[:, :d] = dla.astype(bf16)
        dgl_ref[:, d:] = dlb.astype(bf16)
        sa = jnp.sum(dla, axis=0, keepdims=True)
        sb = jnp.sum(dlb, axis=0, keepdims=True)

        @pl.when(pl.program_id(0) == 0)
        def _():
            dbg_ref[:, :d] = sa
            dbg_ref[:, d:] = sb

        @pl.when(pl.program_id(0) > 0)
        def _():
            dbg_ref[:, :d] += sa
            dbg_ref[:, d:] += sb

    row = pl.BlockSpec((tr, d), lambda i: (i, 0))
    return pl.pallas_call(
        body, name=name, grid=(t // tr,),
        in_specs=[row, row, row, row, pl.BlockSpec((tr, d), lambda i: (i, 1)),
                  pl.BlockSpec((1, d), lambda i: (0, 0)), pl.BlockSpec((1, d), lambda i: (0, 1))],
        out_specs=[row, row, pl.BlockSpec((tr, 2 * d), lambda i: (i, 0)), pl.BlockSpec((1, 2 * d), lambda i: (0, 0))],
        out_shape=[jax.ShapeDtypeStruct((t, d), bf16), jax.ShapeDtypeStruct((t, d), bf16),
                   jax.ShapeDtypeStruct((t, 2 * d), bf16), jax.ShapeDtypeStruct((1, 2 * d), f32)],
        compiler_params=_params(("arbitrary",)),
    )(dm, pa, pb, gl, gl, bg, bg)


_INV_SQRT2 = 1.0 / math.sqrt(2.0)
_INV_SQRT2PI = 1.0 / math.sqrt(2.0 * math.pi)


def _gmlp_common(uv, vg, vb, with_grad=False):
    cdf = 0.5 * (1.0 + lax.erf(uv * _INV_SQRT2))
    zz = uv * cdf
    u, vhat, rstd, vn = _gmlp_norm(zz, vg, vb)
    if not with_grad:
        return u, vhat, rstd, vn
    return u, vhat, rstd, vn, cdf + uv * jnp.exp(-0.5 * uv * uv) * _INV_SQRT2PI


def _gmlp_norm(zz, vg, vb):
    u = zz[:, :GMLP_WIDTH]
    v = zz[:, GMLP_WIDTH:]
    mu = jnp.mean(v, axis=1, keepdims=True)
    vc = v - mu
    rstd = lax.rsqrt(jnp.mean(vc * vc, axis=1, keepdims=True) + NORM_EPS)
    vhat = vc * rstd
    vn = vhat * vg + vb
    return u, vhat, rstd, vn


def _gmlp_fwd(uv, vg, vb, wsp, bsp_t, *, name):
    t = uv.shape[0]
    per_step = 4 if t % (4 * CHUNK) == 0 else 1
    rows = per_step * CHUNK

    def body(uv_ref, vg_ref, vb_ref, w_ref, b_ref, y_ref):
        tril = _iota((CHUNK, CHUNK), 0) >= _iota((CHUNK, CHUNK), 1)
        bt = b_ref[...]
        for q in range(per_step):
            qs = slice(q * CHUNK, (q + 1) * CHUNK)
            u, _, _, vn = _gmlp_common(uv_ref[qs, :], vg_ref[...], vb_ref[...])
            for g in range(GMLP_GROUPS):
                sl = slice(g * CHUNK, (g + 1) * CHUNK)
                w = jnp.where(tril, w_ref[g], 0.0)
                s = _dot(w, vn[:, sl]) + bt[:, g:g + 1]
                y_ref[qs, sl] = (u[:, sl] * s).astype(bf16)

    return pl.pallas_call(
        body, name=name, grid=(t // rows,),
        in_specs=[pl.BlockSpec((rows, 2 * GMLP_WIDTH), lambda c: (c, 0)),
                  pl.BlockSpec((1, GMLP_WIDTH), lambda c: (0, 0)), pl.BlockSpec((1, GMLP_WIDTH), lambda c: (0, 0)),
                  pl.BlockSpec((GMLP_GROUPS, CHUNK, CHUNK), lambda c: (0, 0, 0)),
                  pl.BlockSpec((CHUNK, LANES), lambda c: (0, 0))],
        out_specs=pl.BlockSpec((rows, GMLP_WIDTH), lambda c: (c, 0)),
        out_shape=jax.ShapeDtypeStruct((t, GMLP_WIDTH), bf16),
        compiler_params=_params(("parallel",)),
    )(uv, vg, vb, wsp, bsp_t)


def _gmlp_bwd(uv, dya, vg, vb, wsp, bsp_t, *, ride=None, name):
    t = uv.shape[0]
    per_step = 4 if t % (4 * CHUNK) == 0 else 1
    rows = per_step * CHUNK
    steps = t // rows
    riding = ride is not None

    def body(*refs):
        uv_ref, dy_ref, vg_ref, vb_ref, w_ref, b_ref = refs[:6]
        duv_ref, dw_ref, db_ref, dvg_ref, dvb_ref = refs[6 + riding:11 + riding]
        first = pl.program_id(0) == 0
        if riding:
            start, finish = _swap_protocol(refs[6], refs[12], refs[13], refs[14])
            pl.when(first)(start)

        @pl.when(first)
        def _():
            dw_ref[...] = jnp.zeros_like(dw_ref)
            db_ref[...] = jnp.zeros_like(db_ref)
            dvg_ref[...] = jnp.zeros_like(dvg_ref)
            dvb_ref[...] = jnp.zeros_like(dvb_ref)

        vgv = vg_ref[...]
        tril = _iota((CHUNK, CHUNK), 0) >= _iota((CHUNK, CHUNK), 1)
        lane = _iota((CHUNK, LANES), 1)
        bt = b_ref[...]
        for q in range(per_step):
            qs = slice(q * CHUNK, (q + 1) * CHUNK)
            u, vhat, rstd, vn, gelu_grad = _gmlp_common(uv_ref[qs, :], vgv, vb_ref[...], with_grad=True)
            dy = dy_ref[qs, :]
            ds_all = dy * u
            dbacc = jnp.zeros((CHUNK, LANES), f32)
            dvh_parts = []
            for g in range(GMLP_GROUPS):
                sl = slice(g * CHUNK, (g + 1) * CHUNK)
                w = jnp.where(tril, w_ref[g], 0.0)
                vng = vn[:, sl]
                s = _dot(w, vng) + bt[:, g:g + 1]
                ds = ds_all[:, sl]
                duv_ref[qs, sl] = (dy[:, sl] * s * gelu_grad[:, sl]).astype(bf16)
                dw_ref[g] += jnp.where(tril, _dot(ds, vng, _NT), 0.0)
                dbacc = dbacc + jnp.where(lane == g, jnp.sum(ds, axis=1, keepdims=True), 0.0)
                dvn = _dot(w, ds, _TN)
                vh = vhat[:, sl]
                dvg_ref[:, sl] += jnp.sum(dvn * vh, axis=0, keepdims=True)
                dvb_ref[:, sl] += jnp.sum(dvn, axis=0, keepdims=True)
                dvh_parts.append(dvn * vgv[:, sl])
            db_ref[...] += dbacc
            dvhat = jnp.concatenate(dvh_parts, axis=1)
            m1 = jnp.mean(dvhat, axis=1, keepdims=True)
            m2 = jnp.mean(dvhat * vhat, axis=1, keepdims=True)
            dv = rstd * (dvhat - m1 - vhat * m2)
            duv_ref[qs, GMLP_WIDTH:] = (dv * gelu_grad[:, GMLP_WIDTH:]).astype(bf16)
        if riding:
            pl.when(pl.program_id(0) == steps - 1)(finish)

    vec = pl.BlockSpec((1, GMLP_WIDTH), lambda c: (0, 0))
    return pl.pallas_call(
        body, name=name, grid=(steps,),
        in_specs=[pl.BlockSpec((rows, 2 * GMLP_WIDTH), lambda c: (c, 0)),
                  pl.BlockSpec((rows, GMLP_WIDTH), lambda c: (c, 0)), vec, vec,
                  pl.BlockSpec((GMLP_GROUPS, CHUNK, CHUNK), lambda c: (0, 0, 0)),
                  pl.BlockSpec((CHUNK, LANES), lambda c: (0, 0))] + [_ANY] * riding,
        out_specs=[pl.BlockSpec((rows, 2 * GMLP_WIDTH), lambda c: (c, 0)),
                   pl.BlockSpec((GMLP_GROUPS, CHUNK, CHUNK), lambda c: (0, 0, 0)),
                   pl.BlockSpec((CHUNK, LANES), lambda c: (0, 0)), vec, vec] + [_ANY] * riding,
        out_shape=[jax.ShapeDtypeStruct((t, 2 * GMLP_WIDTH), bf16),
                   jax.ShapeDtypeStruct((GMLP_GROUPS, CHUNK, CHUNK), f32),
                   jax.ShapeDtypeStruct((CHUNK, LANES), f32),
                   jax.ShapeDtypeStruct((1, GMLP_WIDTH), f32), jax.ShapeDtypeStruct((1, GMLP_WIDTH), f32)]
        + ([jax.ShapeDtypeStruct(ride.shape[:1] + ride.shape[2:], ride.dtype)] if riding else []),
        scratch_shapes=list(_SWAP_SCRATCH) if riding else [],
        compiler_params=_params(("arbitrary",)),
    )(uv, dya, vg, vb, wsp, bsp_t, *([ride] if riding else []))


_CONV_COLS = 512
_B0, _C0 = D_INNER, D_INNER + N_GROUPS * D_STATE


_TAIL = 8


def _conv_silu(cur_ref, tail_ref, w_ref, b_ref, has_prev, xc_ref, cv_ref):
    row = _iota((_TAIL, _CONV_COLS), 0)
    for j in range(CONV_DIM // _CONV_COLS):
        sl = slice(j * _CONV_COLS, (j + 1) * _CONV_COLS)
        cur = cur_ref[:, sl]
        tail = jnp.where(has_prev, tail_ref[:, sl], 0.0)
        acc = cur * w_ref[CONV_W - 1:CONV_W, sl] + b_ref[:, sl]
        for s in range(1, CONV_W):
            rolled = pltpu.roll(cur, s, 0)
            top = jnp.where(row >= s, rolled[:_TAIL], pltpu.roll(tail, s, 0))
            sh = jnp.concatenate([top, rolled[_TAIL:]], axis=0)
            acc = acc + sh * w_ref[CONV_W - 1 - s:CONV_W - s, sl]
        cv_ref[:, sl] = acc
        xc_ref[:, sl] = acc * _sigmoid(acc)


def _col_bcast(mat, h):
    return jnp.broadcast_to(mat[:, h:h + 1], (CHUNK, LANES))


def _head_expand(cols):
    lo = _iota((CHUNK, LANES), 1) < HEAD_DIM
    return jnp.concatenate([jnp.where(lo, cols[2 * j], cols[2 * j + 1]) for j in range(N_HEADS // 2)], axis=1)


def _ssd_chunk_scalars(dtr, dtb, alog):
    xdt_pre = dtr + dtb
    dtv = jnp.maximum(xdt_pre, 0.0) + jnp.log(1.0 + jnp.exp(-jnp.abs(xdt_pre)))
    a = -jnp.exp(alog)
    ltri = (_iota((CHUNK, CHUNK), 0) >= _iota((CHUNK, CHUNK), 1)).astype(f32)
    cs = _dot32(ltri, dtv * a)
    csb = [_col_bcast(cs, h) for h in range(N_HEADS)]
    cs_x = _head_expand(csb)
    dt_x = _head_expand([_col_bcast(dtv, h) for h in range(N_HEADS)])
    cl_x = cs_x[CHUNK - 1:CHUNK, :]
    return dict(xdt_pre=xdt_pre, dtv=dtv, a=a, cs=cs, cs_t=cs.T, csb=csb, dt_x=dt_x, e_x=jnp.exp(cs_x),
                dec_x=jnp.exp(cl_x - cs_x), dk_x=jnp.exp(cl_x))


def _head_masks():
    lane = _iota((CHUNK, GROUP_W), 1)
    return [(lane >= r * HEAD_DIM) & (lane < (r + 1) * HEAD_DIM) for r in range(HEADS_PER_GROUP)]


def _stack_heads(a, masks):
    return jnp.concatenate([jnp.where(m, a, 0.0) for m in masks], axis=0).astype(bf16)


def _seg_sum(a, seg):
    hi = a.astype(jnp.bfloat16)
    lo = (a - hi.astype(f32)).astype(jnp.bfloat16)
    return (lax.dot_general(hi, seg, _NN, preferred_element_type=f32)
            + lax.dot_general(lo, seg, _NN, preferred_element_type=f32))


def _head_seg_matrix():
    return (_iota((D_INNER, LANES), 0) // HEAD_DIM == _iota((D_INNER, LANES), 1)).astype(jnp.bfloat16)


def _ssd_fwd(xbc, z, dtr, cw, cb, dtb, alog, dsk_x, gs, *, ride=None, name):
    t = xbc.shape[0]
    nc = t // CHUNK
    tiles = CHUNK // _TAIL

    def body(*refs):
        cur_ref, tail_ref, z_ref, dtr_ref, cw_ref, cb_ref, dtb_ref, alog_ref, dsk_ref, gs_ref = refs[:10]
        if ride is None:
            yb_ref, hp_ref, cv_ref, state_ref, xc_ref = refs[10:]
        else:
            ride_ref, yb_ref, hp_ref, cv_ref, got_ref, state_ref, xc_ref, send_sems, recv_sems = refs[10:]
        c = pl.program_id(0)
        if ride is not None:
            start, relay, finish = _gather_protocol(ride_ref, got_ref, send_sems, recv_sems)
            pl.when(c == 0)(start)
            pl.when(c == nc // 2)(relay)

        @pl.when(c == 0)
        def _():
            state_ref[...] = jnp.zeros_like(state_ref)

        _conv_silu(cur_ref, tail_ref, cw_ref, cb_ref, c > 0, xc_ref, cv_ref)
        sc = _ssd_chunk_scalars(dtr_ref[...], dtb_ref[...], alog_ref[...])
        tril = _iota((CHUNK, CHUNK), 0) >= _iota((CHUNK, CHUNK), 1)
        masks = _head_masks()
        hp_ref[0] = state_ref[...]
        for g in range(N_GROUPS):
            gsl = slice(g * GROUP_W, (g + 1) * GROUP_W)
            xs_g = xc_ref[:, gsl]
            bg = xc_ref[:, _B0 + g * D_STATE:_B0 + (g + 1) * D_STATE]
            cg = xc_ref[:, _C0 + g * D_STATE:_C0 + (g + 1) * D_STATE]
            xdt_g = xs_g * sc["dt_x"][:, gsl]
            cbm = _dot(cg, bg, _NT)
            mw = jnp.concatenate(
                [cbm * jnp.exp(jnp.where(tril, sc["csb"][h] - sc["cs_t"][h:h + 1, :], -1e30))
                 for h in range(g * HEADS_PER_GROUP, (g + 1) * HEADS_PER_GROUP)], axis=1)
            ht_g = state_ref[:, gsl]
            y_g = _dot(mw, _stack_heads(xdt_g, masks)) + sc["e_x"][:, gsl] * _dot(cg, ht_g) + dsk_ref[:, gsl] * xs_g
            state_ref[:, gsl] = ht_g * sc["dk_x"][:, gsl] + _dot(bg, xdt_g * sc["dec_x"][:, gsl], _TN)
            zg = z_ref[:, gsl]
            yg = y_g * zg * _sigmoid(zg)
            rs = lax.rsqrt(jnp.mean(yg * yg, axis=1, keepdims=True) + NORM_EPS)
            yb_ref[:, gsl] = (yg * rs * gs_ref[:, gsl]).astype(bf16)
        if ride is not None:
            pl.when(c == nc - 1)(finish)

    def chunk(w):
        return pl.BlockSpec((CHUNK, w), lambda c: (c, 0))

    def const(shape):
        return pl.BlockSpec(shape, lambda c: (0,) * len(shape))

    riding = ride is not None
    return pl.pallas_call(
        body, name=name, grid=(nc,),
        in_specs=[chunk(CONV_DIM), pl.BlockSpec((_TAIL, CONV_DIM), lambda c: (jnp.maximum(c * tiles - 1, 0), 0)),
                  chunk(D_INNER), chunk(LANES), const((CONV_W, CONV_DIM)), const((1, CONV_DIM)),
                  const((1, LANES)), const((1, LANES)), const((1, D_INNER)), const((1, D_INNER))] + [_ANY] * riding,
        out_specs=[chunk(D_INNER), pl.BlockSpec((1, D_STATE, D_INNER), lambda c: (c, 0, 0)), chunk(CONV_DIM)]
        + [_ANY] * riding,
        out_shape=[jax.ShapeDtypeStruct((t, D_INNER), bf16), jax.ShapeDtypeStruct((nc, D_STATE, D_INNER), f32),
                   jax.ShapeDtypeStruct((t, CONV_DIM), f32)]
        + ([jax.ShapeDtypeStruct((N_CHIPS,) + ride.shape, ride.dtype)] if riding else []),
        scratch_shapes=[pltpu.VMEM((D_STATE, D_INNER), f32), pltpu.VMEM((CHUNK, CONV_DIM), f32)]
        + (list(_GATHER_SCRATCH) if riding else []),
        compiler_params=_params(("arbitrary",)),
    )(xbc, xbc, z, dtr, cw, cb, dtb, alog, dsk_x, gs, *([ride] if riding else []))


def _ssd_bwd(xbc, cv, z, dtr, hprev, dyb, cw, dtb, alog, dsk_x, gs, seg, *, ride=None, name):
    t = xbc.shape[0]
    nc = t // CHUNK

    def body(*refs):
        (cur_ref, cv_ref, z_ref, dtr_ref, hp_ref, dyb_ref, cw_ref, dtb_ref, alog_ref, dsk_ref, gs_ref,
         seg_ref) = refs[:12]
        rest = refs[12:]
        if ride is not None:
            ride_ref, got_ref, send_sems, recv_sems = rest[0], rest[10], rest[-2], rest[-1]
            rest = rest[1:10] + rest[11:-2]
        (dz_ref, dxbc_ref, ddt_ref, dcw_ref, dcb_ref, ddtb_ref, dalog_ref, ddsk_ref, dgs_ref,
         dh_ref, dcnext_ref, xc_ref, dxc_ref, x13_ref, x2_ref, rows_ref) = rest
        i = pl.program_id(0)
        if ride is not None:
            start, finish = _scatter_protocol(ride_ref, got_ref, send_sems, recv_sems)
            pl.when(i == 0)(start)

        @pl.when(i == 0)
        def _():
            for ref in (dh_ref, dcnext_ref, dcw_ref, dcb_ref, ddtb_ref, dalog_ref, ddsk_ref, dgs_ref, rows_ref):
                ref[...] = jnp.zeros_like(ref)

        for j in range(CONV_DIM // _CONV_COLS):
            sl = slice(j * _CONV_COLS, (j + 1) * _CONV_COLS)
            cvv = cv_ref[:, sl]
            xc_ref[:, sl] = cvv * _sigmoid(cvv)
        sc = _ssd_chunk_scalars(dtr_ref[...], dtb_ref[...], alog_ref[...])
        tril = _iota((CHUNK, CHUNK), 0) >= _iota((CHUNK, CHUNK), 1)
        triu = _iota((CHUNK, CHUNK), 0) <= _iota((CHUNK, CHUNK), 1)
        masks = _head_masks()
        rowh = _iota((N_HEADS, CHUNK), 0)
        dcs_t = jnp.zeros((N_HEADS, CHUNK), f32)
        for g in range(N_GROUPS):
            gsl = slice(g * GROUP_W, (g + 1) * GROUP_W)
            xs_g = xc_ref[:, gsl]
            bg = xc_ref[:, _B0 + g * D_STATE:_B0 + (g + 1) * D_STATE]
            cg = xc_ref[:, _C0 + g * D_STATE:_C0 + (g + 1) * D_STATE]
            dt_g, e_g, dec_g, dk_g = sc["dt_x"][:, gsl], sc["e_x"][:, gsl], sc["dec_x"][:, gsl], sc["dk_x"][:, gsl]
            dsk_g = dsk_ref[:, gsl]
            xdt_g = xs_g * dt_g
            xdt_stack = _stack_heads(xdt_g, masks)
            cbm = _dot(cg, bg, _NT)
            cbt = _dot(bg, cg, _NT)
            heads = range(g * HEADS_PER_GROUP, (g + 1) * HEADS_PER_GROUP)
            lmats = [jnp.exp(jnp.where(tril, sc["csb"][h] - sc["cs_t"][h:h + 1, :], -1e30)) for h in heads]
            mw = jnp.concatenate([cbm * lm for lm in lmats], axis=1)
            mtw = jnp.concatenate(
                [cbt * jnp.exp(jnp.where(triu, sc["cs_t"][h:h + 1, :] - sc["csb"][h], -1e30)) for h in heads], axis=1)
            ht_g = hp_ref[0, :, gsl]
            dhn_g = dh_ref[:, gsl]
            yoff = e_g * _dot(cg, ht_g)
            y_g = _dot(mw, xdt_stack) + yoff + dsk_g * xs_g
            zg = z_ref[:, gsl]
            sz = _sigmoid(zg)
            silu = zg * sz
            yg = y_g * silu
            rs = lax.rsqrt(jnp.mean(yg * yg, axis=1, keepdims=True) + NORM_EPS)
            yn = yg * rs
            dyb = dyb_ref[:, gsl]
            dgs_ref[:, gsl] += jnp.sum(dyb * yn, axis=0, keepdims=True)
            dyn = dyb * gs_ref[:, gsl]
            dyg = rs * (dyn - yn * jnp.mean(dyn * yn, axis=1, keepdims=True))
            dy_g = dyg * silu
            dz_ref[:, gsl] = (dyg * y_g * (sz * (1.0 + zg * (1.0 - sz)))).astype(bf16)
            dy_stack = _stack_heads(dy_g, masks)
            dm_w = _dot(dy_g, xdt_stack, _NT)
            dmt_w = _dot(xdt_g, dy_stack, _NT)
            dxdt = _dot(mtw, dy_stack)
            dcb_acc = jnp.zeros((CHUNK, CHUNK), f32)
            for r, h in enumerate(heads):
                hs = slice(r * CHUNK, (r + 1) * CHUNK)
                dml = dm_w[:, hs] * lmats[r]
                dcb_acc = dcb_acc + dml
                col = jnp.sum(dml * cbm, axis=0, keepdims=True)
                row = jnp.sum(dmt_w[:, hs] * mtw[:, hs], axis=0, keepdims=True)
                dcs_t = dcs_t + jnp.where(rowh == h, row - col, 0.0)
            w = _dot(bg, dhn_g)
            dxdt = dxdt + dec_g * w
            decx3 = dec_g * (xdt_g * w)
            dg_g = e_g * dy_g
            d_c = _dot(dg_g, ht_g, _NT) + _dot(dcb_acc, bg)
            d_b = _dot(dcb_acc, cg, _TN) + _dot(xdt_g * dec_g, dhn_g, _NT)
            dh_ref[:, gsl] = dhn_g * dk_g + _dot(cg, dg_g, _TN)
            dxc_ref[:, gsl] = dsk_g * dy_g + dxdt * dt_g
            dxc_ref[:, _B0 + g * D_STATE:_B0 + (g + 1) * D_STATE] = d_b
            dxc_ref[:, _C0 + g * D_STATE:_C0 + (g + 1) * D_STATE] = d_c
            x13_ref[:, gsl] = dy_g * yoff - decx3
            x2_ref[:, gsl] = dxdt * xs_g
            rows_ref[0:1, gsl] = jnp.sum(dhn_g * ht_g, axis=0, keepdims=True)
            rows_ref[1:2, gsl] = jnp.sum(decx3, axis=0, keepdims=True)
            rows_ref[2:3, gsl] = jnp.sum(dy_g * xs_g, axis=0, keepdims=True)
        segm = seg_ref[...]
        r13 = _seg_sum(x13_ref[...], segm)
        r2 = _seg_sum(x2_ref[...], segm)
        small = _seg_sum(rows_ref[...], segm)
        lane = _iota((CHUNK, LANES), 1)
        rowi = _iota((CHUNK, LANES), 0)
        dcl_row = small[0:1, :] * jnp.exp(sc["cs"][CHUNK - 1:CHUNK, :]) + small[1:2, :]
        dcs = r13 + jnp.where(rowi == CHUNK - 1, dcl_row, 0.0)
        dcs_t_all = dcs.T + jnp.concatenate([dcs_t, jnp.zeros((LANES - N_HEADS, CHUNK), f32)], axis=0)
        dda = _dot32(dcs_t_all, tril.astype(f32)).T
        a = sc["a"]
        ddt_total = r2 + dda * a
        dalog_ref[...] += jnp.sum(dda * sc["dtv"], axis=0, keepdims=True) * a
        ddtr = jnp.where(lane < N_HEADS, ddt_total * _sigmoid(sc["xdt_pre"]), 0.0)
        ddtb_ref[...] += jnp.sum(ddtr, axis=0, keepdims=True)
        ddt_ref[...] = ddtr.astype(bf16)
        ddsk_ref[...] += small[2:3, :]
        row8 = _iota((_TAIL, _CONV_COLS), 0)
        for j in range(CONV_DIM // _CONV_COLS):
            sl = slice(j * _CONV_COLS, (j + 1) * _CONV_COLS)
            cvv = cv_ref[:, sl]
            sg = _sigmoid(cvv)
            dconv = dxc_ref[:, sl] * (sg * (1.0 + cvv * (1.0 - sg)))
            nxt = dcnext_ref[:, sl]
            cur = cur_ref[:, sl]
            dxin = dconv * cw_ref[CONV_W - 1:CONV_W, sl]
            dcw_ref[CONV_W - 1:CONV_W, sl] += jnp.sum(dconv * cur, axis=0, keepdims=True)
            for s in range(1, CONV_W):
                rolled = pltpu.roll(dconv, CHUNK - s, 0)
                bot = jnp.where(row8 < _TAIL - s, rolled[CHUNK - _TAIL:], pltpu.roll(nxt, _TAIL - s, 0))
                up = jnp.concatenate([rolled[:CHUNK - _TAIL], bot], axis=0)
                dxin = dxin + up * cw_ref[CONV_W - 1 - s:CONV_W - s, sl]
                dcw_ref[CONV_W - 1 - s:CONV_W - s, sl] += jnp.sum(up * cur, axis=0, keepdims=True)
            dcb_ref[:, sl] += jnp.sum(dconv, axis=0, keepdims=True)
            dxbc_ref[:, sl] = dxin.astype(bf16)
            dcnext_ref[:, sl] = dconv[:_TAIL]
        if ride is not None:
            pl.when(i == nc - 1)(finish)

    def chunk(w):
        return pl.BlockSpec((CHUNK, w), lambda i: (nc - 1 - i, 0))

    def const(shape):
        return pl.BlockSpec(shape, lambda i: (0,) * len(shape))

    riding = ride is not None
    return pl.pallas_call(
        body, name=name, grid=(nc,),
        in_specs=[chunk(CONV_DIM), chunk(CONV_DIM),
                  chunk(D_INNER), chunk(LANES), pl.BlockSpec((1, D_STATE, D_INNER), lambda i: (nc - 1 - i, 0, 0)),
                  chunk(D_INNER), const((CONV_W, CONV_DIM)),
                  const((1, LANES)), const((1, LANES)), const((1, D_INNER)), const((1, D_INNER)),
                  const((D_INNER, LANES))] + [_ANY] * riding,
        out_specs=[chunk(D_INNER), chunk(CONV_DIM), chunk(LANES), const((CONV_W, CONV_DIM)), const((1, CONV_DIM)),
                   const((1, LANES)), const((1, LANES)), const((1, LANES)), const((1, D_INNER))] + [_ANY] * riding,
        out_shape=[jax.ShapeDtypeStruct((t, D_INNER), bf16), jax.ShapeDtypeStruct((t, CONV_DIM), bf16),
                   jax.ShapeDtypeStruct((t, LANES), bf16), jax.ShapeDtypeStruct((CONV_W, CONV_DIM), f32),
                   jax.ShapeDtypeStruct((1, CONV_DIM), f32), jax.ShapeDtypeStruct((1, LANES), f32),
                   jax.ShapeDtypeStruct((1, LANES), f32), jax.ShapeDtypeStruct((1, LANES), f32),
                   jax.ShapeDtypeStruct((1, D_INNER), f32)]
        + ([jax.ShapeDtypeStruct((N_CHIPS - 1,) + ride.shape[1:], ride.dtype)] if riding else []),
        scratch_shapes=[pltpu.VMEM((D_STATE, D_INNER), f32), pltpu.VMEM((_TAIL, CONV_DIM), f32),
                        pltpu.VMEM((CHUNK, CONV_DIM), f32), pltpu.VMEM((CHUNK, CONV_DIM), f32),
                        pltpu.VMEM((CHUNK, D_INNER), f32), pltpu.VMEM((CHUNK, D_INNER), f32),
                        pltpu.VMEM((_TAIL, D_INNER), f32)]
        + (list(_SCATTER_SCRATCH) if riding else []),
        compiler_params=_params(("arbitrary",)),
    )(xbc, cv, z, dtr, hprev, dyb, cw, dtb, alog, dsk_x, gs, seg, *([ride] if riding else []))


def _adamw(w, g, m, v, *, name):
    r, c = w.shape
    tr = r
    while tr * c * 4 > 2 * _MB and tr % 16 == 0:
        tr //= 2

    def body(w_ref, g_ref, m_ref, v_ref, d_ref, m2_ref, v2_ref):
        gv = g_ref[...]
        m2 = ADAM_B1 * m_ref[...] + (1.0 - ADAM_B1) * gv
        v2 = ADAM_B2 * v_ref[...] + (1.0 - ADAM_B2) * (gv * gv)
        m_hat = m2 / (1.0 - ADAM_B1 ** ADAM_STEP)
        v_hat = v2 / (1.0 - ADAM_B2 ** ADAM_STEP)
        d_ref[...] = -ADAM_LR * (m_hat / (jnp.sqrt(v_hat) + ADAM_EPS) + ADAM_WD * w_ref[...])
        m2_ref[...] = m2
        v2_ref[...] = v2

    blk = pl.BlockSpec((tr, c), lambda i: (i, 0))
    return pl.pallas_call(
        body, name=name, grid=(r // tr,),
        in_specs=[blk] * 4, out_specs=[blk] * 3,
        out_shape=[jax.ShapeDtypeStruct((r, c), f32)] * 3,
        compiler_params=_params(("parallel",)),
    )(w, g, m, v)


def _row_block(rows, cols):
    cap = max(16, 2 * _MB // (4 * cols))
    return max(tr for tr in range(16, min(cap, rows) + 1, 16) if rows % tr == 0)


def _cast_bf16(a, *, name):
    r, c = a.shape
    tr = _row_block(r, c)

    def body(a_ref, o_ref):
        o_ref[...] = a_ref[...].astype(bf16)

    blk = pl.BlockSpec((tr, c), lambda i: (i, 0))
    return pl.pallas_call(
        body, name=name, grid=(r // tr,), in_specs=[blk], out_specs=blk,
        out_shape=jax.ShapeDtypeStruct((r, c), bf16), compiler_params=_params(("parallel",)),
    )(a)


_ANY = pl.BlockSpec(memory_space=pl.ANY)


def _place():
    x, y, c = lax.axis_index("x"), lax.axis_index("y"), lax.axis_index("c")
    other_chips = [(1 - x, y), (x, 1 - y), (1 - x, 1 - y)]
    return x, y, c, other_chips


def _gather_protocol(in_ref, out_ref, send_sems, recv_sems):
    x, y, c, chips = _place()
    me = 2 * x + y
    sibling = (x, y, 1 - c)
    where = [2 * cx + cy for cx, cy in chips]

    def cp(k, chip, half, to, src=None):
        dst = out_ref.at[chip, half]
        return pltpu.make_async_remote_copy(
            src_ref=dst if src is None else src, dst_ref=dst, send_sem=send_sems.at[k], recv_sem=recv_sems.at[k],
            device_id=to, device_id_type=MESH)

    def sends():
        return [cp(j, me, c, (*chips[j], c), src=in_ref.at[c]) for j in range(2)]

    def relays():
        return [cp(3 + j, where[j], c, sibling) for j in range(3)]

    def landed(j):
        return cp(j, where[j], c, sibling)

    def start():
        for f in sends():
            f.start()

    def relay():
        onward = relays()
        for first in range(2):
            @pl.when(c == first)
            def _(first=first):
                landed(first).wait_recv()
                cp(2, where[first], c, (*chips[1 - first], c)).start()
                onward[first].start()
                landed(1 - first).wait_recv()
                onward[1 - first].start()

    def finish():
        landed(2).wait_recv()
        relays()[2].start()
        for j in range(3):
            cp(3 + j, where[j], 1 - c, sibling).wait_recv()
        for f in sends() + [landed(2)] + relays():
            f.wait_send()

    return start, relay, finish


_GATHER_SCRATCH = [pltpu.SemaphoreType.DMA((6,)), pltpu.SemaphoreType.DMA((6,))]


def _gather_shards(shard, *, name):
    _, rh, lanes = shard.shape

    def body(in_ref, out_ref, send_sems, recv_sems):
        start, relay, finish = _gather_protocol(in_ref, out_ref, send_sems, recv_sems)
        start()
        relay()
        finish()

    return pl.pallas_call(
        body, name=name, in_specs=[_ANY], out_specs=_ANY,
        out_shape=jax.ShapeDtypeStruct((N_CHIPS, 2, rh, lanes), shard.dtype),
        scratch_shapes=list(_GATHER_SCRATCH),
    )(shard)


def _scatter_protocol(p_ref, out_ref, send_sems, recv_sems):
    x, y, c, chips = _place()

    def copies():
        return [pltpu.make_async_remote_copy(
            src_ref=p_ref.at[2 * cx + cy], dst_ref=out_ref.at[j], send_sem=send_sems.at[j], recv_sem=recv_sems.at[j],
            device_id=(cx, cy, c), device_id_type=MESH) for j, (cx, cy) in enumerate(chips)]

    def start():
        for cpy in copies():
            cpy.start()

    def finish():
        for cpy in copies():
            cpy.wait()

    return start, finish


_SCATTER_SCRATCH = [pltpu.SemaphoreType.DMA((3,)), pltpu.SemaphoreType.DMA((3,))]


def _swap_protocol(g_ref, out_ref, send_sems, recv_sems):
    x, y, c, _ = _place()

    def copies():
        return [pltpu.make_async_remote_copy(
            src_ref=g_ref.at[k, 1 - c], dst_ref=out_ref.at[k], send_sem=send_sems.at[k], recv_sem=recv_sems.at[k],
            device_id=(x, y, 1 - c), device_id_type=MESH) for k in range(N_CHIPS)]

    def start():
        for cpy in copies():
            cpy.start()

    def finish():
        for cpy in copies():
            cpy.wait()

    return start, finish


_SWAP_SCRATCH = [pltpu.SemaphoreType.DMA((N_CHIPS,)), pltpu.SemaphoreType.DMA((N_CHIPS,))]


def _rs_swap_halves(g, *, name):
    nch, _, rh, lanes = g.shape

    def body(g_ref, out_ref, send_sems, recv_sems):
        start, finish = _swap_protocol(g_ref, out_ref, send_sems, recv_sems)
        start()
        finish()

    return pl.pallas_call(
        body, name=name, in_specs=[_ANY], out_specs=_ANY,
        out_shape=jax.ShapeDtypeStruct((nch, rh, lanes), g.dtype),
        scratch_shapes=list(_SWAP_SCRATCH),
    )(g)


def _rs_add_pair(g, got, c_idx, *, name):
    nch, _, rh, lanes = g.shape
    tr = _row_block(rh, lanes)

    def body(c_ref, g_ref, got_ref, p16_ref):
        p16_ref[...] = (g_ref[...] + got_ref[...]).astype(bf16)

    blk = pl.BlockSpec((None, tr, lanes), lambda k, i, c_ref: (k, i, 0))
    return pl.pallas_call(
        body, name=name,
        grid_spec=pltpu.PrefetchScalarGridSpec(
            num_scalar_prefetch=1, grid=(nch, rh // tr),
            in_specs=[pl.BlockSpec((None, None, tr, lanes), lambda k, i, c_ref: (k, c_ref[0], i, 0)), blk],
            out_specs=blk),
        out_shape=jax.ShapeDtypeStruct((nch, rh, lanes), bf16),
        compiler_params=_params(("parallel", "parallel")),
    )(c_idx, g, got)


def _rs_add_chips(g, got_pair, got, place, *, name):
    _, _, rh, lanes = g.shape
    tr = _row_block(rh, lanes)

    def body(place_ref, g_ref, pair_ref, got_ref, o_ref):
        own = g_ref[...] + pair_ref[...]
        o_ref[...] = ((own + got_ref[0].astype(f32)) + got_ref[1].astype(f32)) + got_ref[2].astype(f32)

    return pl.pallas_call(
        body, name=name,
        grid_spec=pltpu.PrefetchScalarGridSpec(
            num_scalar_prefetch=1, grid=(rh // tr,),
            in_specs=[pl.BlockSpec((None, None, tr, lanes), lambda i, place_ref: (place_ref[0], place_ref[1], i, 0)),
                      pl.BlockSpec((None, tr, lanes), lambda i, place_ref: (place_ref[0], i, 0)),
                      pl.BlockSpec((3, tr, lanes), lambda i, place_ref: (0, i, 0))],
            out_specs=pl.BlockSpec((None, tr, lanes), lambda i, place_ref: (place_ref[1], i, 0))),
        out_shape=jax.ShapeDtypeStruct((2, rh, lanes), f32),
        compiler_params=_params(("parallel",)),
    )(place, g, got_pair, got)


def _rs_join_halves(halves, *, name):
    def body(h_ref, out_ref, send_sem, recv_sem):
        x, y, c, _ = _place()
        cpy = pltpu.make_async_remote_copy(
            src_ref=h_ref.at[c], dst_ref=out_ref.at[c], send_sem=send_sem, recv_sem=recv_sem,
            device_id=(x, y, 1 - c), device_id_type=MESH)
        cpy.start()
        cpy.wait()

    return pl.pallas_call(
        body, name=name, in_specs=[_ANY], out_specs=_ANY,
        out_shape=jax.ShapeDtypeStruct(halves.shape, halves.dtype), input_output_aliases={0: 0},
        scratch_shapes=[pltpu.SemaphoreType.DMA, pltpu.SemaphoreType.DMA],
    )(halves)


def _all_reduce_small(s, *, name):
    rs, lanes = s.shape
    rh = rs // 2

    def body(s_ref, o_ref, sib_ref, mine_ref, chips_ref, send_sems, recv_sems):
        x, y, c, chips = _place()
        me = 2 * x + y
        sibling = (x, y, 1 - c)
        rows = pl.ds(pl.multiple_of(c * rh, 8), rh)

        def cp(k, src, dst, to):
            return pltpu.make_async_remote_copy(src_ref=src, dst_ref=dst, send_sem=send_sems.at[k],
                                                recv_sem=recv_sems.at[k], device_id=to, device_id_type=MESH)

        swap = cp(0, s_ref, sib_ref, sibling)
        swap.start()
        swap.wait()
        mine_ref[...] = s_ref[rows, :] + sib_ref[rows, :]
        sends = [cp(1 + j, mine_ref, chips_ref.at[j], (cx, cy, c)) for j, (cx, cy) in enumerate(chips)]
        for cpy in sends:
            cpy.start()
        for cpy in sends:
            cpy.wait()
        where = [2 * cx + cy for cx, cy in chips]
        total = None
        for q in range(N_CHIPS):
            term = jnp.where(q == me, mine_ref[...], jnp.where(
                q == where[0], chips_ref[0], jnp.where(q == where[1], chips_ref[1], chips_ref[2])))
            total = term if total is None else total + term
        o_ref[rows, :] = total
        push = cp(4, o_ref.at[rows, :], o_ref.at[rows, :], sibling)
        push.start()
        push.wait()

    vm = pl.BlockSpec(memory_space=pltpu.VMEM)
    return pl.pallas_call(
        body, name=name, in_specs=[vm], out_specs=vm,
        out_shape=jax.ShapeDtypeStruct((rs, lanes), f32),
        scratch_shapes=[pltpu.VMEM((rs, lanes), f32), pltpu.VMEM((rh, lanes), f32),
                        pltpu.VMEM((N_CHIPS - 1, rh, lanes), f32), pltpu.SemaphoreType.DMA((5,)),
                        pltpu.SemaphoreType.DMA((5,))],
        compiler_params=pltpu.CompilerParams(vmem_limit_bytes=32 * _MB),
    )(s)


def _pad_lanes(a, width=LANES):
    return jnp.pad(a, ((0, 0), (0, width - a.shape[1])))


def _local_grads(x, tgt, wts, small, *, fwd_ride=None, late_weights=None, swap_ride=None, bwd_ride=None,
                 last_ride=None):
    t = x.shape[0]
    tm = min(t, 1024)
    d = D_MODEL
    mm = functools.partial(_matmul, tm=tm)

    dtb = _pad_lanes(small["dt_bias"])
    alog = _pad_lanes(small["a_log"])
    dsk = jnp.repeat(small["d_skip"], HEAD_DIM, axis=1)
    bsp_t = _pad_lanes(small["b_spatial"].T)
    wsp = small["w_spatial"]

    h = _rms_fwd(x, small["norm_mix_g"], name="rms_mix")
    uv = mm(h, wts["uv"], tn=2048, tk=d, out_dtypes=[f32], name="proj_uv")
    z = mm(h, wts["z"], tn=2048, tk=d, out_dtypes=[f32], name="proj_z")
    xbc = mm(h, wts["xbc"], tn=2048, tk=d, out_dtypes=[f32], name="proj_xbc")
    dtr = mm(h, wts["dt"], tn=LANES, tk=d, out_dtypes=[f32], name="proj_dt")
    gl = mm(h, wts["gate"], tn=2048, tk=d, out_dtypes=[f32], name="proj_gate")
    ya = _gmlp_fwd(uv, small["v_norm_g"], small["v_norm_b"], wsp, bsp_t, name="gmlp_fwd")
    yb, hprev, cv, *gathered = _ssd_fwd(xbc, z, dtr, small["conv_w"], small["conv_b"], dtb, alog, dsk,
                                        small["ssm_norm_g"], ride=fwd_ride, name="ssd_fwd")
    if fwd_ride is not None:
        wts = {**wts, **late_weights(gathered[0])}
    tall = functools.partial(_matmul, tm=min(t, 2048))
    pa = tall(ya, wts["pa"], tn=1024, tk=1024, out_dtypes=[f32], name="proj_a")
    tm_gate = min(t, 512)
    row_vec = [pl.BlockSpec((1, d), lambda i, j, k, half=half: (0, half)) for half in range(2)]
    gate_tiles = [pl.BlockSpec((tm_gate, d), lambda i, j, k, half=half: (i, half)) for half in range(2)]

    def merge(pb_acc, pa_t, gla, glb, bga, bgb):
        return pb_acc, _sigmoid(gla + bga) * pa_t + _sigmoid(glb + bgb) * pb_acc

    pb, merged = _matmul(yb, wts["pb"], tm=tm_gate, tn=d, tk=1024, out_dtypes=[f32, bf16], epilogue=merge,
                         extras=[pa, gl, gl, small["b_gates"], small["b_gates"]],
                         extra_specs=[None] + gate_tiles + row_vec, name="proj_b")

    def residual_norm(acc, res, g):
        x_new = res + acc
        r = lax.rsqrt(jnp.mean(x_new * x_new, axis=1, keepdims=True) + NORM_EPS)
        return x_new, x_new * r * g

    x1, h2 = mm(merged, wts["out"], tn=d, tk=1024, out_dtypes=[f32, bf16], epilogue=residual_norm,
                extras=[x, small["norm_mlp_g"]], extra_specs=[None, row_vec[0]], name="out_proj")
    act = mm(h2, wts["up"], tn=2048, tk=d, out_dtypes=[bf16],
             epilogue=lambda acc: (jnp.square(jnp.maximum(acc, 0.0)),), name="mlp_up")
    x2 = mm(act, wts["down"], tn=1024, tk=2048, out_dtypes=[f32], extras=[x1],
            epilogue=lambda acc, res: (res + acc,), name="mlp_down")

    dx2, dx2b, dgf, loss = _loss_head(x2, tgt, small["norm_final_g"], name="loss_head")
    tt = min(t, 2048)
    tn_mm = functools.partial(_matmul_tn, tt=tt)
    dw = {}
    def slab(key, chip_of):
        rows = _LATE_ROWS[key]
        if N_CHIPS * rows == 1024:
            return dict(tka=1024, tn=1024, place=lambda i, j: (ALL_CHIPS, _LATE_OFF[key] // rows))
        return dict(tka=min(rows, 1024), tn=1024, place=lambda i, j: (chip_of(i, j), _LATE_OFF[key] // min(rows, 1024)))

    dw["late"] = tn_mm(act, dx2b, name="dw_down", **slab("w_mlp_down", lambda i, j: i))
    dup = mm(dx2b, wts["down"], nt=True, tn=2048, tk=1024, out_dtypes=[bf16], extras=[act],
             epilogue=lambda acc, a2: (acc * (2.0 * jnp.sqrt(a2).astype(f32)),), name="d_act")
    dw["late"] = tn_mm(h2, dup, name="dw_up", packed=dw["late"], **slab("w_mlp_up", lambda i, j: j))
    dh2 = mm(dup, wts["up"], nt=True, tn=1024, tk=2048, out_dtypes=[f32], name="d_h2")
    dx1, dx1b, dg_mlp = _rms_bwd(x1, small["norm_mlp_g"], dh2, dx2, want_bf16=True, name="rms_mlp_bwd")
    dw["late"] = tn_mm(merged, dx1b, name="dw_out", packed=dw["late"], **slab("w_out", lambda i, j: i))
    dmerged = tall(dx1b, wts["out"], nt=True, tn=1024, tk=1024, out_dtypes=[f32], name="d_merged")
    dpa, dpb, dgl, dbg = _merge_bwd(dmerged, pa, pb, gl, small["b_gates"], name="merge_bwd")
    dw["late"] = tn_mm(ya, dpa, name="dw_pa", packed=dw["late"], **slab("w_proj_a", lambda i, j: i))
    dw["late"] = tn_mm(yb, dpb, name="dw_pb", packed=dw["late"], **slab("w_proj_b", lambda i, j: i))
    dya = tall(dpa, wts["pa"], nt=True, tn=1024, tk=1024, out_dtypes=[f32], name="d_ya")
    dyb = mm(dpb, wts["pb"], nt=True, tn=2048, tk=1024, out_dtypes=[f32], name="d_yb")
    swapped = swap_ride(dw) if swap_ride is not None else None
    duv, dwsp, dbsp_t, dvg, dvb, *got_pair = _gmlp_bwd(uv, dya, small["v_norm_g"], small["v_norm_b"], wsp, bsp_t,
                                                       ride=swapped, name="gmlp_bwd")
    ride = bwd_ride(swapped, got_pair[0]) if bwd_ride is not None else None
    dz, dxbc, ddt, dcw, dcb, ddtb, dalog, ddsk, dgs, *got = _ssd_bwd(
        xbc, cv, z, dtr, hprev, dyb, small["conv_w"], dtb, alog, dsk, small["ssm_norm_g"],
        _head_seg_matrix(), ride=ride, name="ssd_bwd")
    dw["uv"] = tn_mm(h, duv, tka=1024, tn=1024, name="dw_uv")
    dw["z"] = tn_mm(h, dz, tka=1024, tn=1024, name="dw_z")
    dw["xbc"] = tn_mm(h, dxbc, tka=1024, tn=1024, name="dw_xbc")
    dw["dt"] = tn_mm(h, ddt, tka=1024, tn=LANES, name="dw_dt")
    dw["gate"] = tn_mm(h, dgl, tka=1024, tn=1024, name="dw_gate")
    last = last_ride(dw) if last_ride is not None else None
    res = _matmul_nt_sum(
        [(duv, wts["uv"]), (dz, wts["z"]), (dxbc, wts["xbc"]), (dgl, wts["gate"]), (ddt, wts["dt"])],
        tm=tm, tks=[1024] * 4 + [LANES], ride=last, name="d_h")
    dh, got_last = (res[0], res[1]) if last is not None else (res, None)
    dx, dg_mix = _rms_bwd(x, small["norm_mix_g"], dh, dx1, want_bf16=False, name="rms_mix_bwd")

    dsmall = {
        "norm_mix_g": dg_mix, "conv_w": dcw, "conv_b": dcb, "dt_bias": ddtb[:, :N_HEADS], "a_log": dalog[:, :N_HEADS],
        "d_skip": ddsk[:, :N_HEADS], "ssm_norm_g": dgs, "v_norm_g": dvg, "v_norm_b": dvb, "w_spatial": dwsp,
        "b_spatial": dbsp_t[:, :GMLP_GROUPS].T, "b_gates": dbg, "norm_mlp_g": dg_mlp, "norm_final_g": dgf,
    }
    return loss, dx, dw, dsmall, (got[0] if got else None), got_last


_IN_SHARD = IN_PROJ // N_CHIPS
_LATE = ("w_mlp_down", "w_mlp_up", "w_proj_b", "w_proj_a", "w_out")
_LATE_ROWS = {"w_proj_a": GMLP_WIDTH // N_CHIPS, "w_proj_b": D_INNER // N_CHIPS, "w_out": D_MODEL // N_CHIPS,
              "w_mlp_up": D_MODEL, "w_mlp_down": D_FF // N_CHIPS}
_LATE_TOTAL = sum(_LATE_ROWS.values())


def _late_offsets():
    off, out = 0, {}
    for k in _LATE:
        out[k] = off
        off += _LATE_ROWS[k]
    return out


_LATE_OFF = _late_offsets()

_SMALL = ("norm_mix_g", "conv_w", "conv_b", "dt_bias", "a_log", "d_skip", "ssm_norm_g", "v_norm_g", "v_norm_b",
          "w_spatial", "b_spatial", "b_gates", "norm_mlp_g", "norm_final_g")


def _pack_small(parts):
    flat = jnp.concatenate([parts[k].reshape(-1) for k in _SMALL])
    rows = -(-flat.shape[0] // (16 * LANES)) * 16
    return jnp.pad(flat, (0, rows * LANES - flat.shape[0])).reshape(rows, LANES)


def _unpack_small(packed, shapes):
    flat = packed.reshape(-1)
    out, off = {}, 0
    for k in _SMALL:
        n = math.prod(shapes[k])
        out[k] = flat[off:off + n].reshape(shapes[k])
        off += n
    return out


def _from_chip_columns(stacked):
    _, rows, cols = stacked.shape
    return stacked.transpose(1, 0, 2).reshape(rows, N_CHIPS * cols)


def _to_chip_columns(full):
    rows, cols = full.shape
    return full.reshape(rows, N_CHIPS, cols // N_CHIPS).transpose(1, 0, 2)


def _w_in_grad_by_chip(dw):
    pieces = [dw["uv"], dw["z"], dw["xbc"], dw["dt"][:, :N_HEADS], dw["gate"]]
    bounds = [0]
    for p in pieces:
        bounds.append(bounds[-1] + p.shape[1])
    chips = []
    for k in range(N_CHIPS):
        lo, hi = k * _IN_SHARD, (k + 1) * _IN_SHARD
        parts = [p[:, max(lo, b0) - b0:min(hi, b1) - b0]
                 for p, b0, b1 in zip(pieces, bounds[:-1], bounds[1:]) if min(hi, b1) > max(lo, b0)]
        chips.append(jnp.concatenate(parts, axis=1))
    return jnp.stack(chips)


def kernel(x, norm_mix_g, w_in, conv_w, conv_b, dt_bias, a_log, d_skip, ssm_norm_g, v_norm_g, v_norm_b, w_spatial, b_spatial, b_gates, w_proj_a, w_proj_b, w_out, norm_mlp_g, w_mlp_up, w_mlp_down, norm_final_g, loss_target, m_norm_mix_g, m_w_in, m_conv_w, m_conv_b, m_dt_bias, m_a_log, m_d_skip, m_ssm_norm_g, m_v_norm_g, m_v_norm_b, m_w_spatial, m_b_spatial, m_b_gates, m_w_proj_a, m_w_proj_b, m_w_out, m_norm_mlp_g, m_w_mlp_up, m_w_mlp_down, m_norm_final_g, v_norm_mix_g, v_w_in, v_conv_w, v_conv_b, v_dt_bias, v_a_log, v_d_skip, v_ssm_norm_g, v_v_norm_g, v_v_norm_b, v_w_spatial, v_b_spatial, v_b_gates, v_w_proj_a, v_w_proj_b, v_w_out, v_norm_mlp_g, v_w_mlp_up, v_w_mlp_down, v_norm_final_g):
    given = dict(locals())
    names = ("norm_mix_g", "w_in", "conv_w", "conv_b", "dt_bias", "a_log", "d_skip", "ssm_norm_g", "v_norm_g",
             "v_norm_b", "w_spatial", "b_spatial", "b_gates", "w_proj_a", "w_proj_b", "w_out", "norm_mlp_g",
             "w_mlp_up", "w_mlp_down", "norm_final_g")
    xi, yi, ci = lax.axis_index("x"), lax.axis_index("y"), lax.axis_index("c")
    me_chip = (2 * xi + yi).astype(jnp.int32)

    def halves(a):
        return a.reshape(2, a.shape[0] // 2, a.shape[1])

    def with_own(got, shard):
        whole = lax.dynamic_update_slice(got, shard[None], (me_chip, 0, 0, 0))
        return whole.reshape(N_CHIPS, 2 * shard.shape[1], shard.shape[2])

    shard_in = halves(_cast_bf16(w_in[0], name="cast_w_in"))
    shard_late = halves(_cast_bf16(jnp.concatenate([given[k][0] for k in _LATE]), name="cast_w_late"))
    shard_conv = halves(conv_w.reshape(2 * _TAIL, -1))
    w_in_full = _from_chip_columns(with_own(_gather_shards(shard_in, name="gather_w_in"), shard_in))
    o_dt, o_gate = 2 * GMLP_WIDTH + D_INNER + CONV_DIM, 2 * GMLP_WIDTH + D_INNER + CONV_DIM + N_HEADS
    wts = {
        "uv": w_in_full[:, :2 * GMLP_WIDTH], "z": w_in_full[:, 2 * GMLP_WIDTH:2 * GMLP_WIDTH + D_INNER],
        "xbc": w_in_full[:, 2 * GMLP_WIDTH + D_INNER:o_dt], "dt": _pad_lanes(w_in_full[:, o_dt:o_gate]),
        "gate": w_in_full[:, o_gate:],
    }
    conv_all = with_own(_gather_shards(shard_conv, name="gather_conv_w"), shard_conv)
    conv_full = _from_chip_columns(conv_all.reshape(N_CHIPS, CONV_W, CONV_DIM // N_CHIPS))

    def late_weights(got):
        g_late = with_own(got, shard_late)

        def rows_of(k):
            return g_late[:, _LATE_OFF[k]:_LATE_OFF[k] + _LATE_ROWS[k]]

        return {
            "pa": rows_of("w_proj_a").reshape(GMLP_WIDTH, D_MODEL),
            "pb": rows_of("w_proj_b").reshape(D_INNER, D_MODEL), "out": rows_of("w_out").reshape(D_MODEL, D_MODEL),
            "up": _from_chip_columns(rows_of("w_mlp_up")), "down": rows_of("w_mlp_down").reshape(D_FF, D_MODEL),
        }

    small = {
        "norm_mix_g": norm_mix_g, "conv_w": conv_full, "conv_b": conv_b, "dt_bias": dt_bias, "a_log": a_log,
        "d_skip": d_skip, "ssm_norm_g": ssm_norm_g, "v_norm_g": v_norm_g, "v_norm_b": v_norm_b,
        "w_spatial": w_spatial[0], "b_spatial": b_spatial[0], "b_gates": b_gates, "norm_mlp_g": norm_mlp_g,
        "norm_final_g": norm_final_g.reshape(1, D_MODEL),
    }

    c_idx = ci.astype(jnp.int32).reshape(1)
    place = jnp.stack([me_chip, ci.astype(jnp.int32)])
    partials = {}

    def reduced_shard(tag, got_chips):
        own = _rs_add_chips(*partials[tag], got_chips, place, name="rs_add_chips_" + tag)
        both = _rs_join_halves(own, name="rs_join_" + tag)
        return both.reshape(2 * both.shape[1], both.shape[2])

    def late_grads(dw):
        return dw["late"].reshape(N_CHIPS, 2, _LATE_TOTAL // 2, D_MODEL)

    def late_partials(g, got_pair):
        partials["late"] = (g, got_pair)
        return _rs_add_pair(g, got_pair, c_idx, name="rs_add_pair_late")

    def in_partials(dw):
        g = _w_in_grad_by_chip(dw).reshape(N_CHIPS, 2, D_MODEL // 2, _IN_SHARD)
        got_pair = _rs_swap_halves(g, name="rs_swap_in")
        partials["in"] = (g, got_pair)
        return _rs_add_pair(g, got_pair, c_idx, name="rs_add_pair_in")

    loss_part, grad_x, dw, dsmall, got_late, got_in = _local_grads(
        x[0], loss_target[0], wts, small, fwd_ride=shard_late, late_weights=late_weights, swap_ride=late_grads,
        bwd_ride=late_partials, last_ride=in_partials)
    loss = lax.psum(loss_part[0, 0], ("x", "y", "c"))
    g_late = reduced_shard("late", got_late)
    g_in_shard = reduced_shard("in", got_in)

    small_shapes = {k: dsmall[k].shape for k in _SMALL}
    red = _unpack_small(_all_reduce_small(_pack_small(dsmall), name="all_reduce_small"), small_shapes)
    conv_cols = CONV_DIM // N_CHIPS
    red["conv_w"] = lax.dynamic_slice_in_dim(red["conv_w"], me_chip * conv_cols, conv_cols, axis=1)

    grads, deltas, new_m, new_v = {}, {}, {}, {}
    for k in ("w_in",) + _LATE:
        g2 = g_in_shard if k == "w_in" else g_late[_LATE_OFF[k]:_LATE_OFF[k] + _LATE_ROWS[k]]
        dlt, m2, v2 = _adamw(given[k][0], g2, given["m_" + k][0], given["v_" + k][0], name="adamw_" + k)
        grads[k], deltas[k], new_m[k], new_v[k] = g2, dlt, m2, v2
    adam_shapes = dict(small_shapes)
    adam_shapes["conv_w"] = (CONV_W, conv_cols)

    def small_pack_of(prefix):
        return _pack_small({k: given[prefix + k].reshape(adam_shapes[k]) for k in _SMALL})

    dlt_s, m_s, v_s = _adamw(small_pack_of(""), _pack_small(red), small_pack_of("m_"), small_pack_of("v_"),
                             name="adamw_small")
    for dst, packed in ((deltas, dlt_s), (new_m, m_s), (new_v, v_s)):
        dst.update(_unpack_small(packed, adam_shapes))
    grads.update(red)

    def shaped(dct):
        return [dct[k].reshape(given[k].shape) for k in names]

    return (loss, grad_x[None], *shaped(grads), *shaped(deltas), *shaped(new_m), *shaped(new_v))
```

```python
import functools
import math

import jax
import jax.numpy as jnp
from jax import lax
from jax.experimental import pallas as pl
from jax.experimental.pallas import tpu as pltpu

f32 = jnp.float32
bf16 = jnp.bfloat16

D_MODEL = 1024
CHUNK = 128
GMLP_WIDTH = 1024
GMLP_GROUPS = 8
D_INNER = 2048
HEAD_DIM = 64
N_HEADS = 32
N_GROUPS = 8
HEADS_PER_GROUP = 4
GROUP_W = HEADS_PER_GROUP * HEAD_DIM
D_STATE = 128
CONV_W = 4
CONV_DIM = 4096
D_FF = 4096
IN_PROJ = 10272
NORM_EPS = 1e-6
N_CHIPS = 4
LANES = 128

ADAM_LR = 0.001
ADAM_B1 = 0.9
ADAM_B2 = 0.999
ADAM_EPS = 1e-08
ADAM_WD = 0.01
ADAM_STEP = 10

MESH = pl.DeviceIdType.MESH
_NT = (((1,), (1,)), ((), ()))
_NN = (((1,), (0,)), ((), ()))
_TN = (((0,), (0,)), ((), ()))
_MB = 2 ** 20


def _params(sem, vmem_mb=48):
    return pltpu.CompilerParams(dimension_semantics=sem, vmem_limit_bytes=vmem_mb * _MB)


def _dot(a, b, dims=_NN):
    return lax.dot_general(a.astype(bf16), b.astype(bf16), dims, preferred_element_type=f32)


def _dot32(a, b):
    return jnp.dot(a, b, preferred_element_type=f32, precision=lax.Precision.HIGHEST)


def _sigmoid(x):
    return 1.0 / (1.0 + jnp.exp(-x))


def _sum_all(a):
    return jnp.sum(jnp.sum(a, axis=1, keepdims=True), axis=0, keepdims=True)


def _iota(shape, dim):
    return lax.broadcasted_iota(jnp.int32, shape, dim)


def _matmul(a, b, *, nt=False, tm, tn, tk, out_dtypes, epilogue=None, extras=(), extra_specs=None, name):
    m, k_dim = a.shape
    n = b.shape[0] if nt else b.shape[1]
    nk = k_dim // tk
    ne, no = len(extras), len(out_dtypes)
    dims = _NT if nt else _NN

    def body(*refs):
        a_ref, b_ref = refs[0], refs[1]
        ex = refs[2:2 + ne]
        outs = refs[2 + ne:2 + ne + no]

        def finish(acc):
            vals = epilogue(acc, *[e[...] for e in ex]) if epilogue is not None else (acc,)
            for o, v in zip(outs, vals):
                o[...] = v.astype(o.dtype)

        part = lax.dot_general(a_ref[...], b_ref[...], dims, preferred_element_type=f32)
        if nk == 1:
            finish(part)
        else:
            acc_ref = refs[-1]
            kk = pl.program_id(2)

            @pl.when(kk == 0)
            def _():
                acc_ref[...] = part

            @pl.when(kk > 0)
            def _():
                acc_ref[...] += part

            @pl.when(kk == nk - 1)
            def _():
                finish(acc_ref[...])

    b_spec = pl.BlockSpec((tn, tk), lambda i, j, k: (j, k)) if nt else pl.BlockSpec((tk, tn), lambda i, j, k: (k, j))
    tile = pl.BlockSpec((tm, tn), lambda i, j, k: (i, j))
    ex_specs = [tile if s is None else s for s in (extra_specs or [None] * ne)]
    outs = pl.pallas_call(
        body, name=name, grid=(m // tm, n // tn, nk),
        in_specs=[pl.BlockSpec((tm, tk), lambda i, j, k: (i, k)), b_spec] + ex_specs,
        out_specs=[tile] * no,
        out_shape=[jax.ShapeDtypeStruct((m, n), dt) for dt in out_dtypes],
        scratch_shapes=[pltpu.VMEM((tm, tn), f32)] if nk > 1 else [],
        compiler_params=_params(("parallel", "parallel", "arbitrary")),
    )(a, b, *extras)
    return outs if no > 1 else outs[0]


def _matmul_nt_sum(pairs, *, tm, tks, ride=None, name):
    m = pairs[0][0].shape[0]
    n = pairs[0][1].shape[0]
    nblk = [a.shape[1] // tk for (a, _), tk in zip(pairs, tks)]
    starts = [sum(nblk[:p]) for p in range(len(pairs))]
    nk = sum(nblk)
    npairs = len(pairs)
    ni = m // tm
    riding = ride is not None

    def body(*refs):
        rest = refs[2 * npairs:]
        if riding:
            ride_ref, o_ref, got_ref, acc_ref, send_sems, recv_sems = rest
        else:
            o_ref, acc_ref = rest
        i, kk = pl.program_id(0), pl.program_id(1)
        if riding:
            start, finish = _scatter_protocol(ride_ref, got_ref, send_sems, recv_sems)
            pl.when((i == 0) & (kk == 0))(start)

        @pl.when(kk == 0)
        def _():
            acc_ref[...] = jnp.zeros_like(acc_ref)

        for p in range(npairs):
            @pl.when((kk >= starts[p]) & (kk < starts[p] + nblk[p]))
            def _(p=p):
                acc_ref[...] += lax.dot_general(refs[2 * p][...], refs[2 * p + 1][...], _NT, preferred_element_type=f32)

        @pl.when(kk == nk - 1)
        def _():
            o_ref[...] = acc_ref[...]

        if riding:
            pl.when((i == ni - 1) & (kk == nk - 1))(finish)

    in_specs, args = [], []
    for p, (a, b) in enumerate(pairs):
        def kblock(k, s=starts[p], nb=nblk[p]):
            return jnp.clip(k - s, 0, nb - 1)
        in_specs.append(pl.BlockSpec((tm, tks[p]), lambda i, k, kb=kblock: (i, kb(k))))
        in_specs.append(pl.BlockSpec((n, tks[p]), lambda i, k, kb=kblock: (0, kb(k))))
        args += [a, b]
    tile = pl.BlockSpec((tm, n), lambda i, k: (i, 0))
    outs = pl.pallas_call(
        body, name=name, grid=(ni, nk), in_specs=in_specs + [_ANY] * riding, out_specs=[tile] + [_ANY] * riding,
        out_shape=[jax.ShapeDtypeStruct((m, n), f32)]
        + ([jax.ShapeDtypeStruct((N_CHIPS - 1,) + ride.shape[1:], ride.dtype)] if riding else []),
        scratch_shapes=[pltpu.VMEM((tm, n), f32)] + (list(_SCATTER_SCRATCH) if riding else []),
        compiler_params=_params(("arbitrary", "arbitrary"), vmem_mb=56),
    )(*args, *([ride] if riding else []))
    return outs if riding else outs[0]


ALL_CHIPS = "all"


def _matmul_tn(a, b, *, tka, tn, tt, name, packed=None, place=None):
    t, ka = a.shape
    n = b.shape[1]
    spread = place is not None and place(0, 0)[0] is ALL_CHIPS

    def body(a_ref, b_ref, *rest):
        o_ref = rest[-1]
        part = lax.dot_general(a_ref[...], b_ref[...], _TN, preferred_element_type=f32).reshape(o_ref.shape)
        kk = pl.program_id(2)

        @pl.when(kk == 0)
        def _():
            o_ref[...] = part

        @pl.when(kk > 0)
        def _():
            o_ref[...] += part

    in_specs = [pl.BlockSpec((tt, tka), lambda i, j, k: (k, i)), pl.BlockSpec((tt, tn), lambda i, j, k: (k, j))]
    if place is None:
        out_spec = pl.BlockSpec((tka, tn), lambda i, j, k: (i, j))
        out_shape = jax.ShapeDtypeStruct((ka, n), f32)
    elif spread:
        out_spec = pl.BlockSpec((N_CHIPS, tka // N_CHIPS, tn), lambda i, j, k: (0, place(i, j)[1], 0))
        out_shape = jax.ShapeDtypeStruct((N_CHIPS, _LATE_TOTAL, tn), f32)
    else:
        out_spec = pl.BlockSpec((None, tka, tn), lambda i, j, k: (*place(i, j), 0))
        out_shape = jax.ShapeDtypeStruct((N_CHIPS, _LATE_TOTAL, tn), f32)
    aliased = packed is not None
    return pl.pallas_call(
        body, name=name, grid=(ka // tka, n // tn, t // tt),
        in_specs=in_specs + [_ANY] * aliased, out_specs=out_spec, out_shape=out_shape,
        input_output_aliases={2: 0} if aliased else {},
        compiler_params=_params(("parallel", "parallel", "arbitrary")),
    )(a, b, *([packed] if aliased else []))


def _row_tile(t):
    return min(t, 512)


def _rms_fwd(x, g, *, name):
    t, d = x.shape
    tr = _row_tile(t)

    def body(x_ref, g_ref, h_ref):
        xv = x_ref[...]
        r = lax.rsqrt(jnp.mean(xv * xv, axis=1, keepdims=True) + NORM_EPS)
        h_ref[...] = (xv * r * g_ref[...]).astype(bf16)

    return pl.pallas_call(
        body, name=name, grid=(t // tr,),
        in_specs=[pl.BlockSpec((tr, d), lambda i: (i, 0)), pl.BlockSpec((1, d), lambda i: (0, 0))],
        out_specs=pl.BlockSpec((tr, d), lambda i: (i, 0)),
        out_shape=jax.ShapeDtypeStruct((t, d), bf16),
        compiler_params=_params(("parallel",)),
    )(x, g)


def _rms_bwd(xin, g, dh, dres, *, want_bf16, name):
    t, d = xin.shape
    tr = _row_tile(t)

    def body(x_ref, g_ref, dh_ref, dres_ref, dx_ref, *rest):
        dg_ref = rest[-1]
        xv = x_ref[...]
        r = lax.rsqrt(jnp.mean(xv * xv, axis=1, keepdims=True) + NORM_EPS)
        xn = xv * r
        dhv = dh_ref[...]
        dxn = dhv * g_ref[...]
        dx = dres_ref[...] + r * (dxn - xn * jnp.mean(dxn * xn, axis=1, keepdims=True))
        dx_ref[...] = dx
        if want_bf16:
            rest[0][...] = dx.astype(bf16)
        part = jnp.sum(dhv * xn, axis=0, keepdims=True)

        @pl.when(pl.program_id(0) == 0)
        def _():
            dg_ref[...] = part

        @pl.when(pl.program_id(0) > 0)
        def _():
            dg_ref[...] += part

    row = pl.BlockSpec((tr, d), lambda i: (i, 0))
    vec = pl.BlockSpec((1, d), lambda i: (0, 0))
    out_shape = [jax.ShapeDtypeStruct((t, d), f32)] + ([jax.ShapeDtypeStruct((t, d), bf16)] if want_bf16 else []) \
        + [jax.ShapeDtypeStruct((1, d), f32)]
    return pl.pallas_call(
        body, name=name, grid=(t // tr,),
        in_specs=[row, vec, row, row],
        out_specs=[row] + ([row] if want_bf16 else []) + [vec],
        out_shape=out_shape,
        compiler_params=_params(("arbitrary",)),
    )(xin, g, dh, dres)


def _loss_head(x2, tgt, g, *, name):
    t, d = x2.shape
    tr = _row_tile(t)

    def body(x_ref, t_ref, g_ref, dx_ref, dxb_ref, dg_ref, loss_ref):
        xv = x_ref[...]
        gv = g_ref[...]
        r = lax.rsqrt(jnp.mean(xv * xv, axis=1, keepdims=True) + NORM_EPS)
        xn = xv * r
        e = xn * gv - t_ref[...]
        lpart = jnp.zeros((1, LANES), f32) + 0.5 * _sum_all(jnp.mean(e * e, axis=1, keepdims=True))
        dy = e * (1.0 / d)
        dxn = dy * gv
        dx = r * (dxn - xn * jnp.mean(dxn * xn, axis=1, keepdims=True))
        dx_ref[...] = dx
        dxb_ref[...] = dx.astype(bf16)
        gpart = jnp.sum(dy * xn, axis=0, keepdims=True)

        @pl.when(pl.program_id(0) == 0)
        def _():
            dg_ref[...] = gpart
            loss_ref[...] = lpart

        @pl.when(pl.program_id(0) > 0)
        def _():
            dg_ref[...] += gpart
            loss_ref[...] += lpart

    row = pl.BlockSpec((tr, d), lambda i: (i, 0))
    vec = pl.BlockSpec((1, d), lambda i: (0, 0))
    return pl.pallas_call(
        body, name=name, grid=(t // tr,),
        in_specs=[row, row, vec],
        out_specs=[row, row, vec, pl.BlockSpec((1, LANES), lambda i: (0, 0))],
        out_shape=[jax.ShapeDtypeStruct((t, d), f32), jax.ShapeDtypeStruct((t, d), bf16),
                   jax.ShapeDtypeStruct((1, d), f32), jax.ShapeDtypeStruct((1, LANES), f32)],
        compiler_params=_params(("arbitrary",)),
    )(x2, tgt, g)


def _merge_bwd(dm, pa, pb, gl, bg, *, name):
    t, d = pa.shape
    tr = _row_tile(t)

    def body(dm_ref, pa_ref, pb_ref, gla_ref, glb_ref, bga_ref, bgb_ref, dpa_ref, dpb_ref, dgl_ref, dbg_ref):
        dmv = dm_ref[...]
        ga = _sigmoid(gla_ref[...] + bga_ref[...])
        gb = _sigmoid(glb_ref[...] + bgb_ref[...])
        dpa_ref[...] = (dmv * ga).astype(bf16)
        dpb_ref[...] = (dmv * gb).astype(bf16)
        dla = dmv * pa_ref[...] * ga * (1.0 - ga)
        dlb = dmv * pb_ref[...] * gb * (1.0 - gb)
        dgl_ref[:, :d] = dla.astype(bf16)
        dgl_ref[:, d:] = dlb.astype(bf16)
        sa = jnp.sum(dla, axis=0, keepdims=True)
        sb = jnp.sum(dlb, axis=0, keepdims=True)

        @pl.when(pl.program_id(0) == 0)
        def _():
            dbg_ref[:, :d] = sa
            dbg_ref[:, d:] = sb

        @pl.when(pl.program_id(0) > 0)
        def _():
            dbg_ref[:, :d] += sa
            dbg_ref[:, d:] += sb

    row = pl.BlockSpec((tr, d), lambda i: (i, 0))
    return pl.pallas_call(
        body, name=name, grid=(t // tr,),
        in_specs=[row, row, row, row, pl.BlockSpec((tr, d), lambda i: (i, 1)),
                  pl.BlockSpec((1, d), lambda i: (0, 0)), pl.BlockSpec((1, d), lambda i: (0, 1))],
        out_specs=[row, row, pl.BlockSpec((tr, 2 * d), lambda i: (i, 0)), pl.BlockSpec((1, 2 * d), lambda i: (0, 0))],
        out_shape=[jax.ShapeDtypeStruct((t, d), bf16), jax.ShapeDtypeStruct((t, d), bf16),
                   jax.ShapeDtypeStruct((t, 2 * d), bf16), jax.ShapeDtypeStruct((1, 2 * d), f32)],
        compiler_params=_params(("arbitrary",)),
    )(dm, pa, pb, gl, gl, bg, bg)


_INV_SQRT2 = 1.0 / math.sqrt(2.0)
_INV_SQRT2PI = 1.0 / math.sqrt(2.0 * math.pi)


def _gmlp_common(uv, vg, vb, with_grad=False):
    cdf = 0.5 * (1.0 + lax.erf(uv * _INV_SQRT2))
    zz = uv * cdf
    u, vhat, rstd, vn = _gmlp_norm(zz, vg, vb)
    if not with_grad:
        return u, vhat, rstd, vn
    return u, vhat, rstd, vn, cdf + uv * jnp.exp(-0.5 * uv * uv) * _INV_SQRT2PI


def _gmlp_norm(zz, vg, vb):
    u = zz[:, :GMLP_WIDTH]
    v = zz[:, GMLP_WIDTH:]
    mu = jnp.mean(v, axis=1, keepdims=True)
    vc = v - mu
    rstd = lax.rsqrt(jnp.mean(vc * vc, axis=1, keepdims=True) + NORM_EPS)
    vhat = vc * rstd
    vn = vhat * vg + vb
    return u, vhat, rstd, vn


def _gmlp_fwd(uv, vg, vb, wsp, bsp_t, *, name):
    t = uv.shape[0]
    per_step = 8 if t % (8 * CHUNK) == 0 else 1
    rows = per_step * CHUNK

    def body(uv_ref, vg_ref, vb_ref, w_ref, b_ref, y_ref):
        tril = _iota((CHUNK, CHUNK), 0) >= _iota((CHUNK, CHUNK), 1)
        bt = b_ref[...]
        for q in range(per_step):
            qs = slice(q * CHUNK, (q + 1) * CHUNK)
            u, _, _, vn = _gmlp_common(uv_ref[qs, :], vg_ref[...], vb_ref[...])
            for g in range(GMLP_GROUPS):
                sl = slice(g * CHUNK, (g + 1) * CHUNK)
                w = jnp.where(tril, w_ref[g], 0.0)
                s = _dot(w, vn[:, sl]) + bt[:, g:g + 1]
                y_ref[qs, sl] = (u[:, sl] * s).astype(bf16)

    return pl.pallas_call(
        body, name=name, grid=(t // rows,),
        in_specs=[pl.BlockSpec((rows, 2 * GMLP_WIDTH), lambda c: (c, 0)),
                  pl.BlockSpec((1, GMLP_WIDTH), lambda c: (0, 0)), pl.BlockSpec((1, GMLP_WIDTH), lambda c: (0, 0)),
                  pl.BlockSpec((GMLP_GROUPS, CHUNK, CHUNK), lambda c: (0, 0, 0)),
                  pl.BlockSpec((CHUNK, LANES), lambda c: (0, 0))],
        out_specs=pl.BlockSpec((rows, GMLP_WIDTH), lambda c: (c, 0)),
        out_shape=jax.ShapeDtypeStruct((t, GMLP_WIDTH), bf16),
        compiler_params=_params(("parallel",)),
    )(uv, vg, vb, wsp, bsp_t)


def _gmlp_bwd(uv, dya, vg, vb, wsp, bsp_t, *, ride=None, name):
    t = uv.shape[0]
    per_step = 8 if t % (8 * CHUNK) == 0 else 1
    rows = per_step * CHUNK
    steps = t // rows
    riding = ride is not None

    def body(*refs):
        uv_ref, dy_ref, vg_ref, vb_ref, w_ref, b_ref = refs[:6]
        duv_ref, dw_ref, db_ref, dvg_ref, dvb_ref = refs[6 + riding:11 + riding]
        first = pl.program_id(0) == 0
        if riding:
            start, finish = _swap_protocol(refs[6], refs[12], refs[13], refs[14])
            pl.when(first)(start)

        @pl.when(first)
        def _():
            dw_ref[...] = jnp.zeros_like(dw_ref)
            db_ref[...] = jnp.zeros_like(db_ref)
            dvg_ref[...] = jnp.zeros_like(dvg_ref)
            dvb_ref[...] = jnp.zeros_like(dvb_ref)

        vgv = vg_ref[...]
        tril = _iota((CHUNK, CHUNK), 0) >= _iota((CHUNK, CHUNK), 1)
        lane = _iota((CHUNK, LANES), 1)
        bt = b_ref[...]
        for q in range(per_step):
            qs = slice(q * CHUNK, (q + 1) * CHUNK)
            u, vhat, rstd, vn, gelu_grad = _gmlp_common(uv_ref[qs, :], vgv, vb_ref[...], with_grad=True)
            dy = dy_ref[qs, :]
            ds_all = dy * u
            dbacc = jnp.zeros((CHUNK, LANES), f32)
            dvh_parts = []
            for g in range(GMLP_GROUPS):
                sl = slice(g * CHUNK, (g + 1) * CHUNK)
                w = jnp.where(tril, w_ref[g], 0.0)
                vng = vn[:, sl]
                s = _dot(w, vng) + bt[:, g:g + 1]
                ds = ds_all[:, sl]
                duv_ref[qs, sl] = (dy[:, sl] * s * gelu_grad[:, sl]).astype(bf16)
                dw_ref[g] += jnp.where(tril, _dot(ds, vng, _NT), 0.0)
                dbacc = dbacc + jnp.where(lane == g, jnp.sum(ds, axis=1, keepdims=True), 0.0)
                dvn = _dot(w, ds, _TN)
                vh = vhat[:, sl]
                dvg_ref[:, sl] += jnp.sum(dvn * vh, axis=0, keepdims=True)
                dvb_ref[:, sl] += jnp.sum(dvn, axis=0, keepdims=True)
                dvh_parts.append(dvn * vgv[:, sl])
            db_ref[...] += dbacc
            dvhat = jnp.concatenate(dvh_parts, axis=1)
            m1 = jnp.mean(dvhat, axis=1, keepdims=True)
            m2 = jnp.mean(dvhat * vhat, axis=1, keepdims=True)
            dv = rstd * (dvhat - m1 - vhat * m2)
            duv_ref[qs, GMLP_WIDTH:] = (dv * gelu_grad[:, GMLP_WIDTH:]).astype(bf16)
        if riding:
            pl.when(pl.program_id(0) == steps - 1)(finish)

    vec = pl.BlockSpec((1, GMLP_WIDTH), lambda c: (0, 0))
    return pl.pallas_call(
        body, name=name, grid=(steps,),
        in_specs=[pl.BlockSpec((rows, 2 * GMLP_WIDTH), lambda c: (c, 0)),
                  pl.BlockSpec((rows, GMLP_WIDTH), lambda c: (c, 0)), vec, vec,
                  pl.BlockSpec((GMLP_GROUPS, CHUNK, CHUNK), lambda c: (0, 0, 0)),
                  pl.BlockSpec((CHUNK, LANES), lambda c: (0, 0))] + [_ANY] * riding,
        out_specs=[pl.BlockSpec((rows, 2 * GMLP_WIDTH), lambda c: (c, 0)),
                   pl.BlockSpec((GMLP_GROUPS, CHUNK, CHUNK), lambda c: (0, 0, 0)),
                   pl.BlockSpec((CHUNK, LANES), lambda c: (0, 0)), vec, vec] + [_ANY] * riding,
        out_shape=[jax.ShapeDtypeStruct((t, 2 * GMLP_WIDTH), bf16),
                   jax.ShapeDtypeStruct((GMLP_GROUPS, CHUNK, CHUNK), f32),
                   jax.ShapeDtypeStruct((CHUNK, LANES), f32),
                   jax.ShapeDtypeStruct((1, GMLP_WIDTH), f32), jax.ShapeDtypeStruct((1, GMLP_WIDTH), f32)]
        + ([jax.ShapeDtypeStruct(ride.shape[:1] + ride.shape[2:], ride.dtype)] if riding else []),
        scratch_shapes=list(_SWAP_SCRATCH) if riding else [],
        compiler_params=_params(("arbitrary",)),
    )(uv, dya, vg, vb, wsp, bsp_t, *([ride] if riding else []))


_CONV_COLS = 512
_B0, _C0 = D_INNER, D_INNER + N_GROUPS * D_STATE


_TAIL = 8


def _conv_silu(cur_ref, tail_ref, w_ref, b_ref, has_prev, xc_ref, cv_ref):
    row = _iota((_TAIL, _CONV_COLS), 0)
    for j in range(CONV_DIM // _CONV_COLS):
        sl = slice(j * _CONV_COLS, (j + 1) * _CONV_COLS)
        cur = cur_ref[:, sl]
        tail = jnp.where(has_prev, tail_ref[:, sl], 0.0)
        acc = cur * w_ref[CONV_W - 1:CONV_W, sl] + b_ref[:, sl]
        for s in range(1, CONV_W):
            rolled = pltpu.roll(cur, s, 0)
            top = jnp.where(row >= s, rolled[:_TAIL], pltpu.roll(tail, s, 0))
            sh = jnp.concatenate([top, rolled[_TAIL:]], axis=0)
            acc = acc + sh * w_ref[CONV_W - 1 - s:CONV_W - s, sl]
        cv_ref[:, sl] = acc
        xc_ref[:, sl] = acc * _sigmoid(acc)


def _col_bcast(mat, h):
    return jnp.broadcast_to(mat[:, h:h + 1], (CHUNK, LANES))


def _head_expand(cols):
    lo = _iota((CHUNK, LANES), 1) < HEAD_DIM
    return jnp.concatenate([jnp.where(lo, cols[2 * j], cols[2 * j + 1]) for j in range(N_HEADS // 2)], axis=1)


def _ssd_chunk_scalars(dtr, dtb, alog):
    xdt_pre = dtr + dtb
    dtv = jnp.maximum(xdt_pre, 0.0) + jnp.log(1.0 + jnp.exp(-jnp.abs(xdt_pre)))
    a = -jnp.exp(alog)
    ltri = (_iota((CHUNK, CHUNK), 0) >= _iota((CHUNK, CHUNK), 1)).astype(f32)
    cs = _dot32(ltri, dtv * a)
    csb = [_col_bcast(cs, h) for h in range(N_HEADS)]
    cs_x = _head_expand(csb)
    dt_x = _head_expand([_col_bcast(dtv, h) for h in range(N_HEADS)])
    cl_x = cs_x[CHUNK - 1:CHUNK, :]
    return dict(xdt_pre=xdt_pre, dtv=dtv, a=a, cs=cs, cs_t=cs.T, csb=csb, dt_x=dt_x, e_x=jnp.exp(cs_x),
                dec_x=jnp.exp(cl_x - cs_x), dk_x=jnp.exp(cl_x))


def _head_masks():
    lane = _iota((CHUNK, GROUP_W), 1)
    return [(lane >= r * HEAD_DIM) & (lane < (r + 1) * HEAD_DIM) for r in range(HEADS_PER_GROUP)]


def _stack_heads(a, masks):
    return jnp.concatenate([jnp.where(m, a, 0.0) for m in masks], axis=0).astype(bf16)


def _seg_sum(a, seg):
    hi = a.astype(jnp.bfloat16)
    lo = (a - hi.astype(f32)).astype(jnp.bfloat16)
    return (lax.dot_general(hi, seg, _NN, preferred_element_type=f32)
            + lax.dot_general(lo, seg, _NN, preferred_element_type=f32))


def _head_seg_matrix():
    return (_iota((D_INNER, LANES), 0) // HEAD_DIM == _iota((D_INNER, LANES), 1)).astype(jnp.bfloat16)


def _ssd_fwd(xbc, z, dtr, cw, cb, dtb, alog, dsk_x, gs, *, ride=None, name):
    t = xbc.shape[0]
    nc = t // CHUNK
    tiles = CHUNK // _TAIL

    def body(*refs):
        cur_ref, tail_ref, z_ref, dtr_ref, cw_ref, cb_ref, dtb_ref, alog_ref, dsk_ref, gs_ref = refs[:10]
        if ride is None:
            yb_ref, hp_ref, cv_ref, state_ref, xc_ref = refs[10:]
        else:
            ride_ref, yb_ref, hp_ref, cv_ref, got_ref, state_ref, xc_ref, send_sems, recv_sems = refs[10:]
        c = pl.program_id(0)
        if ride is not None:
            start, relay, finish = _gather_protocol(ride_ref, got_ref, send_sems, recv_sems)
            pl.when(c == 0)(start)
            pl.when(c == nc // 2)(relay)

        @pl.when(c == 0)
        def _():
            state_ref[...] = jnp.zeros_like(state_ref)

        _conv_silu(cur_ref, tail_ref, cw_ref, cb_ref, c > 0, xc_ref, cv_ref)
        sc = _ssd_chunk_scalars(dtr_ref[...], dtb_ref[...], alog_ref[...])
        tril = _iota((CHUNK, CHUNK), 0) >= _iota((CHUNK, CHUNK), 1)
        masks = _head_masks()
        hp_ref[0] = state_ref[...]
        for g in range(N_GROUPS):
            gsl = slice(g * GROUP_W, (g + 1) * GROUP_W)
            xs_g = xc_ref[:, gsl]
            bg = xc_ref[:, _B0 + g * D_STATE:_B0 + (g + 1) * D_STATE]
            cg = xc_ref[:, _C0 + g * D_STATE:_C0 + (g + 1) * D_STATE]
            xdt_g = xs_g * sc["dt_x"][:, gsl]
            cbm = _dot(cg, bg, _NT)
            mw = jnp.concatenate(
                [cbm * jnp.exp(jnp.where(tril, sc["csb"][h] - sc["cs_t"][h:h + 1, :], -1e30))
                 for h in range(g * HEADS_PER_GROUP, (g + 1) * HEADS_PER_GROUP)], axis=1)
            ht_g = state_ref[:, gsl]
            y_g = _dot(mw, _stack_heads(xdt_g, masks)) + sc["e_x"][:, gsl] * _dot(cg, ht_g) + dsk_ref[:, gsl] * xs_g
            state_ref[:, gsl] = ht_g * sc["dk_x"][:, gsl] + _dot(bg, xdt_g * sc["dec_x"][:, gsl], _TN)
            zg = z_ref[:, gsl]
            yg = y_g * zg * _sigmoid(zg)
            rs = lax.rsqrt(jnp.mean(yg * yg, axis=1, keepdims=True) + NORM_EPS)
            yb_ref[:, gsl] = (yg * rs * gs_ref[:, gsl]).astype(bf16)
        if ride is not None:
            pl.when(c == nc - 1)(finish)

    def chunk(w):
        return pl.BlockSpec((CHUNK, w), lambda c: (c, 0))

    def const(shape):
        return pl.BlockSpec(shape, lambda c: (0,) * len(shape))

    riding = ride is not None
    return pl.pallas_call(
        body, name=name, grid=(nc,),
        in_specs=[chunk(CONV_DIM), pl.BlockSpec((_TAIL, CONV_DIM), lambda c: (jnp.maximum(c * tiles - 1, 0), 0)),
                  chunk(D_INNER), chunk(LANES), const((CONV_W, CONV_DIM)), const((1, CONV_DIM)),
                  const((1, LANES)), const((1, LANES)), const((1, D_INNER)), const((1, D_INNER))] + [_ANY] * riding,
        out_specs=[chunk(D_INNER), pl.BlockSpec((1, D_STATE, D_INNER), lambda c: (c, 0, 0)), chunk(CONV_DIM)]
        + [_ANY] * riding,
        out_shape=[jax.ShapeDtypeStruct((t, D_INNER), bf16), jax.ShapeDtypeStruct((nc, D_STATE, D_INNER), f32),
                   jax.ShapeDtypeStruct((t, CONV_DIM), f32)]
        + ([jax.ShapeDtypeStruct((N_CHIPS,) + ride.shape, ride.dtype)] if riding else []),
        scratch_shapes=[pltpu.VMEM((D_STATE, D_INNER), f32), pltpu.VMEM((CHUNK, CONV_DIM), f32)]
        + (list(_GATHER_SCRATCH) if riding else []),
        compiler_params=_params(("arbitrary",)),
    )(xbc, xbc, z, dtr, cw, cb, dtb, alog, dsk_x, gs, *([ride] if riding else []))


def _ssd_bwd(xbc, cv, z, dtr, hprev, dyb, cw, dtb, alog, dsk_x, gs, seg, *, ride=None, name):
    t = xbc.shape[0]
    nc = t // CHUNK

    def body(*refs):
        (cur_ref, cv_ref, z_ref, dtr_ref, hp_ref, dyb_ref, cw_ref, dtb_ref, alog_ref, dsk_ref, gs_ref,
         seg_ref) = refs[:12]
        rest = refs[12:]
        if ride is not None:
            ride_ref, got_ref, send_sems, recv_sems = rest[0], rest[10], rest[-2], rest[-1]
            rest = rest[1:10] + rest[11:-2]
        (dz_ref, dxbc_ref, ddt_ref, dcw_ref, dcb_ref, ddtb_ref, dalog_ref, ddsk_ref, dgs_ref,
         dh_ref, dcnext_ref, xc_ref, dxc_ref, x13_ref, x2_ref, rows_ref) = rest
        i = pl.program_id(0)
        if ride is not None:
            start, finish = _scatter_protocol(ride_ref, got_ref, send_sems, recv_sems)
            pl.when(i == 0)(start)

        @pl.when(i == 0)
        def _():
            for ref in (dh_ref, dcnext_ref, dcw_ref, dcb_ref, ddtb_ref, dalog_ref, ddsk_ref, dgs_ref, rows_ref):
                ref[...] = jnp.zeros_like(ref)

        for j in range(CONV_DIM // _CONV_COLS):
            sl = slice(j * _CONV_COLS, (j + 1) * _CONV_COLS)
            cvv = cv_ref[:, sl]
            xc_ref[:, sl] = cvv * _sigmoid(cvv)
        sc = _ssd_chunk_scalars(dtr_ref[...], dtb_ref[...], alog_ref[...])
        tril = _iota((CHUNK, CHUNK), 0) >= _iota((CHUNK, CHUNK), 1)
        triu = _iota((CHUNK, CHUNK), 0) <= _iota((CHUNK, CHUNK), 1)
        masks = _head_masks()
        rowh = _iota((N_HEADS, CHUNK), 0)
        dcs_t = jnp.zeros((N_HEADS, CHUNK), f32)
        for g in range(N_GROUPS):
            gsl = slice(g * GROUP_W, (g + 1) * GROUP_W)
            xs_g = xc_ref[:, gsl]
            bg = xc_ref[:, _B0 + g * D_STATE:_B0 + (g + 1) * D_STATE]
            cg = xc_ref[:, _C0 + g * D_STATE:_C0 + (g + 1) * D_STATE]
            dt_g, e_g, dec_g, dk_g = sc["dt_x"][:, gsl], sc["e_x"][:, gsl], sc["dec_x"][:, gsl], sc["dk_x"][:, gsl]
            dsk_g = dsk_ref[:, gsl]
            xdt_g = xs_g * dt_g
            xdt_stack = _stack_heads(xdt_g, masks)
            cbm = _dot(cg, bg, _NT)
            cbt = _dot(bg, cg, _NT)
            heads = range(g * HEADS_PER_GROUP, (g + 1) * HEADS_PER_GROUP)
            lmats = [jnp.exp(jnp.where(tril, sc["csb"][h] - sc["cs_t"][h:h + 1, :], -1e30)) for h in heads]
            mw = jnp.concatenate([cbm * lm for lm in lmats], axis=1)
            mtw = jnp.concatenate(
                [cbt * jnp.exp(jnp.where(triu, sc["cs_t"][h:h + 1, :] - sc["csb"][h], -1e30)) for h in heads], axis=1)
            ht_g = hp_ref[0, :, gsl]
            dhn_g = dh_ref[:, gsl]
            yoff = e_g * _dot(cg, ht_g)
            y_g = _dot(mw, xdt_stack) + yoff + dsk_g * xs_g
            zg = z_ref[:, gsl]
            sz = _sigmoid(zg)
            silu = zg * sz
            yg = y_g * silu
            rs = lax.rsqrt(jnp.mean(yg * yg, axis=1, keepdims=True) + NORM_EPS)
            yn = yg * rs
            dyb = dyb_ref[:, gsl]
            dgs_ref[:, gsl] += jnp.sum(dyb * yn, axis=0, keepdims=True)
            dyn = dyb * gs_ref[:, gsl]
            dyg = rs * (dyn - yn * jnp.mean(dyn * yn, axis=1, keepdims=True))
            dy_g = dyg * silu
            dz_ref[:, gsl] = (dyg * y_g * (sz * (1.0 + zg * (1.0 - sz)))).astype(bf16)
            dy_stack = _stack_heads(dy_g, masks)
            dm_w = _dot(dy_g, xdt_stack, _NT)
            dmt_w = _dot(xdt_g, dy_stack, _NT)
            dxdt = _dot(mtw, dy_stack)
            dcb_acc = jnp.zeros((CHUNK, CHUNK), f32)
            for r, h in enumerate(heads):
                hs = slice(r * CHUNK, (r + 1) * CHUNK)
                dml = dm_w[:, hs] * lmats[r]
                dcb_acc = dcb_acc + dml
                col = jnp.sum(dml * cbm, axis=0, keepdims=True)
                row = jnp.sum(dmt_w[:, hs] * mtw[:, hs], axis=0, keepdims=True)
                dcs_t = dcs_t + jnp.where(rowh == h, row - col, 0.0)
            w = _dot(bg, dhn_g)
            dxdt = dxdt + dec_g * w
            decx3 = dec_g * (xdt_g * w)
            dg_g = e_g * dy_g
            d_c = _dot(dg_g, ht_g, _NT) + _dot(dcb_acc, bg)
            d_b = _dot(dcb_acc, cg, _TN) + _dot(xdt_g * dec_g, dhn_g, _NT)
            dh_ref[:, gsl] = dhn_g * dk_g + _dot(cg, dg_g, _TN)
            dxc_ref[:, gsl] = dsk_g * dy_g + dxdt * dt_g
            dxc_ref[:, _B0 + g * D_STATE:_B0 + (g + 1) * D_STATE] = d_b
            dxc_ref[:, _C0 + g * D_STATE:_C0 + (g + 1) * D_STATE] = d_c
            x13_ref[:, gsl] = dy_g * yoff - decx3
            x2_ref[:, gsl] = dxdt * xs_g
            rows_ref[0:1, gsl] = jnp.sum(dhn_g * ht_g, axis=0, keepdims=True)
            rows_ref[1:2, gsl] = jnp.sum(decx3, axis=0, keepdims=True)
            rows_ref[2:3, gsl] = jnp.sum(dy_g * xs_g, axis=0, keepdims=True)
        segm = seg_ref[...]
        r13 = _seg_sum(x13_ref[...], segm)
        r2 = _seg_sum(x2_ref[...], segm)
        small = _seg_sum(rows_ref[...], segm)
        lane = _iota((CHUNK, LANES), 1)
        rowi = _iota((CHUNK, LANES), 0)
        dcl_row = small[0:1, :] * jnp.exp(sc["cs"][CHUNK - 1:CHUNK, :]) + small[1:2, :]
        dcs = r13 + jnp.where(rowi == CHUNK - 1, dcl_row, 0.0)
        dcs_t_all = dcs.T + jnp.concatenate([dcs_t, jnp.zeros((LANES - N_HEADS, CHUNK), f32)], axis=0)
        dda = _dot32(dcs_t_all, tril.astype(f32)).T
        a = sc["a"]
        ddt_total = r2 + dda * a
        dalog_ref[...] += jnp.sum(dda * sc["dtv"], axis=0, keepdims=True) * a
        ddtr = jnp.where(lane < N_HEADS, ddt_total * _sigmoid(sc["xdt_pre"]), 0.0)
        ddtb_ref[...] += jnp.sum(ddtr, axis=0, keepdims=True)
        ddt_ref[...] = ddtr.astype(bf16)
        ddsk_ref[...] += small[2:3, :]
        row8 = _iota((_TAIL, _CONV_COLS), 0)
        for j in range(CONV_DIM // _CONV_COLS):
            sl = slice(j * _CONV_COLS, (j + 1) * _CONV_COLS)
            cvv = cv_ref[:, sl]
            sg = _sigmoid(cvv)
            dconv = dxc_ref[:, sl] * (sg * (1.0 + cvv * (1.0 - sg)))
            nxt = dcnext_ref[:, sl]
            cur = cur_ref[:, sl]
            dxin = dconv * cw_ref[CONV_W - 1:CONV_W, sl]
            dcw_ref[CONV_W - 1:CONV_W, sl] += jnp.sum(dconv * cur, axis=0, keepdims=True)
            for s in range(1, CONV_W):
                rolled = pltpu.roll(dconv, CHUNK - s, 0)
                bot = jnp.where(row8 < _TAIL - s, rolled[CHUNK - _TAIL:], pltpu.roll(nxt, _TAIL - s, 0))
                up = jnp.concatenate([rolled[:CHUNK - _TAIL], bot], axis=0)
                dxin = dxin + up * cw_ref[CONV_W - 1 - s:CONV_W - s, sl]
                dcw_ref[CONV_W - 1 - s:CONV_W - s, sl] += jnp.sum(up * cur, axis=0, keepdims=True)
            dcb_ref[:, sl] += jnp.sum(dconv, axis=0, keepdims=True)
            dxbc_ref[:, sl] = dxin.astype(bf16)
            dcnext_ref[:, sl] = dconv[:_TAIL]
        if ride is not None:
            pl.when(i == nc - 1)(finish)

    def chunk(w):
        return pl.BlockSpec((CHUNK, w), lambda i: (nc - 1 - i, 0))

    def const(shape):
        return pl.BlockSpec(shape, lambda i: (0,) * len(shape))

    riding = ride is not None
    return pl.pallas_call(
        body, name=name, grid=(nc,),
        in_specs=[chunk(CONV_DIM), chunk(CONV_DIM),
                  chunk(D_INNER), chunk(LANES), pl.BlockSpec((1, D_STATE, D_INNER), lambda i: (nc - 1 - i, 0, 0)),
                  chunk(D_INNER), const((CONV_W, CONV_DIM)),
                  const((1, LANES)), const((1, LANES)), const((1, D_INNER)), const((1, D_INNER)),
                  const((D_INNER, LANES))] + [_ANY] * riding,
        out_specs=[chunk(D_INNER), chunk(CONV_DIM), chunk(LANES), const((CONV_W, CONV_DIM)), const((1, CONV_DIM)),
                   const((1, LANES)), const((1, LANES)), const((1, LANES)), const((1, D_INNER))] + [_ANY] * riding,
        out_shape=[jax.ShapeDtypeStruct((t, D_INNER), bf16), jax.ShapeDtypeStruct((t, CONV_DIM), bf16),
                   jax.ShapeDtypeStruct((t, LANES), bf16), jax.ShapeDtypeStruct((CONV_W, CONV_DIM), f32),
                   jax.ShapeDtypeStruct((1, CONV_DIM), f32), jax.ShapeDtypeStruct((1, LANES), f32),
                   jax.ShapeDtypeStruct((1, LANES), f32), jax.ShapeDtypeStruct((1, LANES), f32),
                   jax.ShapeDtypeStruct((1, D_INNER), f32)]
        + ([jax.ShapeDtypeStruct((N_CHIPS - 1,) + ride.shape[1:], ride.dtype)] if riding else []),
        scratch_shapes=[pltpu.VMEM((D_STATE, D_INNER), f32), pltpu.VMEM((_TAIL, CONV_DIM), f32),
                        pltpu.VMEM((CHUNK, CONV_DIM), f32), pltpu.VMEM((CHUNK, CONV_DIM), f32),
                        pltpu.VMEM((CHUNK, D_INNER), f32), pltpu.VMEM((CHUNK, D_INNER), f32),
                        pltpu.VMEM((_TAIL, D_INNER), f32)]
        + (list(_SCATTER_SCRATCH) if riding else []),
        compiler_params=_params(("arbitrary",)),
    )(xbc, cv, z, dtr, hprev, dyb, cw, dtb, alog, dsk_x, gs, seg, *([ride] if riding else []))


def _adamw(w, g, m, v, *, name):
    r, c = w.shape
    tr = r
    while tr * c * 4 > 2 * _MB and tr % 16 == 0:
        tr //= 2

    def body(w_ref, g_ref, m_ref, v_ref, d_ref, m2_ref, v2_ref):
        gv = g_ref[...]
        m2 = ADAM_B1 * m_ref[...] + (1.0 - ADAM_B1) * gv
        v2 = ADAM_B2 * v_ref[...] + (1.0 - ADAM_B2) * (gv * gv)
        m_hat = m2 / (1.0 - ADAM_B1 ** ADAM_STEP)
        v_hat = v2 / (1.0 - ADAM_B2 ** ADAM_STEP)
        d_ref[...] = -ADAM_LR * (m_hat / (jnp.sqrt(v_hat) + ADAM_EPS) + ADAM_WD * w_ref[...])
        m2_ref[...] = m2
        v2_ref[...] = v2

    blk = pl.BlockSpec((tr, c), lambda i: (i, 0))
    return pl.pallas_call(
        body, name=name, grid=(r // tr,),
        in_specs=[blk] * 4, out_specs=[blk] * 3,
        out_shape=[jax.ShapeDtypeStruct((r, c), f32)] * 3,
        compiler_params=_params(("parallel",)),
    )(w, g, m, v)


def _row_block(rows, cols):
    cap = max(16, 2 * _MB // (4 * cols))
    return max(tr for tr in range(16, min(cap, rows) + 1, 16) if rows % tr == 0)


def _cast_bf16(a, *, name):
    r, c = a.shape
    tr = _row_block(r, c)

    def body(a_ref, o_ref):
        o_ref[...] = a_ref[...].astype(bf16)

    blk = pl.BlockSpec((tr, c), lambda i: (i, 0))
    return pl.pallas_call(
        body, name=name, grid=(r // tr,), in_specs=[blk], out_specs=blk,
        out_shape=jax.ShapeDtypeStruct((r, c), bf16), compiler_params=_params(("parallel",)),
    )(a)


_ANY = pl.BlockSpec(memory_space=pl.ANY)


def _place():
    x, y, c = lax.axis_index("x"), lax.axis_index("y"), lax.axis_index("c")
    other_chips = [(1 - x, y), (x, 1 - y), (1 - x, 1 - y)]
    return x, y, c, other_chips


def _gather_protocol(in_ref, out_ref, send_sems, recv_sems):
    x, y, c, chips = _place()
    me = 2 * x + y
    sibling = (x, y, 1 - c)
    where = [2 * cx + cy for cx, cy in chips]

    def cp(k, chip, half, to, src=None):
        dst = out_ref.at[chip, half]
        return pltpu.make_async_remote_copy(
            src_ref=dst if src is None else src, dst_ref=dst, send_sem=send_sems.at[k], recv_sem=recv_sems.at[k],
            device_id=to, device_id_type=MESH)

    def sends():
        return [cp(j, me, c, (*chips[j], c), src=in_ref.at[c]) for j in range(2)]

    def relays():
        return [cp(3 + j, where[j], c, sibling) for j in range(3)]

    def landed(j):
        return cp(j, where[j], c, sibling)

    def start():
        for f in sends():
            f.start()

    def relay():
        onward = relays()
        for first in range(2):
            @pl.when(c == first)
            def _(first=first):
                landed(first).wait_recv()
                cp(2, where[first], c, (*chips[1 - first], c)).start()
                onward[first].start()
                landed(1 - first).wait_recv()
                onward[1 - first].start()

    def finish():
        landed(2).wait_recv()
        relays()[2].start()
        for j in range(3):
            cp(3 + j, where[j], 1 - c, sibling).wait_recv()
        for f in sends() + [landed(2)] + relays():
            f.wait_send()

    return start, relay, finish


_GATHER_SCRATCH = [pltpu.SemaphoreType.DMA((6,)), pltpu.SemaphoreType.DMA((6,))]


def _gather_shards(shard, *, name):
    _, rh, lanes = shard.shape

    def body(in_ref, out_ref, send_sems, recv_sems):
        start, relay, finish = _gather_protocol(in_ref, out_ref, send_sems, recv_sems)
        start()
        relay()
        finish()

    return pl.pallas_call(
        body, name=name, in_specs=[_ANY], out_specs=_ANY,
        out_shape=jax.ShapeDtypeStruct((N_CHIPS, 2, rh, lanes), shard.dtype),
        scratch_shapes=list(_GATHER_SCRATCH),
    )(shard)


def _scatter_protocol(p_ref, out_ref, send_sems, recv_sems):
    x, y, c, chips = _place()

    def copies():
        return [pltpu.make_async_remote_copy(
            src_ref=p_ref.at[2 * cx + cy], dst_ref=out_ref.at[j], send_sem=send_sems.at[j], recv_sem=recv_sems.at[j],
            device_id=(cx, cy, c), device_id_type=MESH) for j, (cx, cy) in enumerate(chips)]

    def start():
        for cpy in copies():
            cpy.start()

    def finish():
        for cpy in copies():
            cpy.wait()

    return start, finish


_SCATTER_SCRATCH = [pltpu.SemaphoreType.DMA((3,)), pltpu.SemaphoreType.DMA((3,))]


def _swap_protocol(g_ref, out_ref, send_sems, recv_sems):
    x, y, c, _ = _place()

    def copies():
        return [pltpu.make_async_remote_copy(
            src_ref=g_ref.at[k, 1 - c], dst_ref=out_ref.at[k], send_sem=send_sems.at[k], recv_sem=recv_sems.at[k],
            device_id=(x, y, 1 - c), device_id_type=MESH) for k in range(N_CHIPS)]

    def start():
        for cpy in copies():
            cpy.start()

    def finish():
        for cpy in copies():
            cpy.wait()

    return start, finish


_SWAP_SCRATCH = [pltpu.SemaphoreType.DMA((N_CHIPS,)), pltpu.SemaphoreType.DMA((N_CHIPS,))]


def _rs_swap_halves(g, *, name):
    nch, _, rh, lanes = g.shape

    def body(g_ref, out_ref, send_sems, recv_sems):
        start, finish = _swap_protocol(g_ref, out_ref, send_sems, recv_sems)
        start()
        finish()

    return pl.pallas_call(
        body, name=name, in_specs=[_ANY], out_specs=_ANY,
        out_shape=jax.ShapeDtypeStruct((nch, rh, lanes), g.dtype),
        scratch_shapes=list(_SWAP_SCRATCH),
    )(g)


def _rs_add_pair(g, got, c_idx, *, name):
    nch, _, rh, lanes = g.shape
    tr = _row_block(rh, lanes)

    def body(c_ref, g_ref, got_ref, p16_ref):
        p16_ref[...] = (g_ref[...] + got_ref[...]).astype(bf16)

    blk = pl.BlockSpec((None, tr, lanes), lambda k, i, c_ref: (k, i, 0))
    return pl.pallas_call(
        body, name=name,
        grid_spec=pltpu.PrefetchScalarGridSpec(
            num_scalar_prefetch=1, grid=(nch, rh // tr),
            in_specs=[pl.BlockSpec((None, None, tr, lanes), lambda k, i, c_ref: (k, c_ref[0], i, 0)), blk],
            out_specs=blk),
        out_shape=jax.ShapeDtypeStruct((nch, rh, lanes), bf16),
        compiler_params=_params(("parallel", "parallel")),
    )(c_idx, g, got)


def _rs_add_chips(g, got_pair, got, place, *, name):
    _, _, rh, lanes = g.shape
    tr = _row_block(rh, lanes)

    def body(place_ref, g_ref, pair_ref, got_ref, o_ref):
        own = g_ref[...] + pair_ref[...]
        o_ref[...] = ((own + got_ref[0].astype(f32)) + got_ref[1].astype(f32)) + got_ref[2].astype(f32)

    return pl.pallas_call(
        body, name=name,
        grid_spec=pltpu.PrefetchScalarGridSpec(
            num_scalar_prefetch=1, grid=(rh // tr,),
            in_specs=[pl.BlockSpec((None, None, tr, lanes), lambda i, place_ref: (place_ref[0], place_ref[1], i, 0)),
                      pl.BlockSpec((None, tr, lanes), lambda i, place_ref: (place_ref[0], i, 0)),
                      pl.BlockSpec((3, tr, lanes), lambda i, place_ref: (0, i, 0))],
            out_specs=pl.BlockSpec((None, tr, lanes), lambda i, place_ref: (place_ref[1], i, 0))),
        out_shape=jax.ShapeDtypeStruct((2, rh, lanes), f32),
        compiler_params=_params(("parallel",)),
    )(place, g, got_pair, got)


def _rs_join_halves(halves, *, name):
    def body(h_ref, out_ref, send_sem, recv_sem):
        x, y, c, _ = _place()
        cpy = pltpu.make_async_remote_copy(
            src_ref=h_ref.at[c], dst_ref=out_ref.at[c], send_sem=send_sem, recv_sem=recv_sem,
            device_id=(x, y, 1 - c), device_id_type=MESH)
        cpy.start()
        cpy.wait()

    return pl.pallas_call(
        body, name=name, in_specs=[_ANY], out_specs=_ANY,
        out_shape=jax.ShapeDtypeStruct(halves.shape, halves.dtype), input_output_aliases={0: 0},
        scratch_shapes=[pltpu.SemaphoreType.DMA, pltpu.SemaphoreType.DMA],
    )(halves)


def _all_reduce_small(s, *, name):
    rs, lanes = s.shape
    rh = rs // 2

    def body(s_ref, o_ref, sib_ref, mine_ref, chips_ref, send_sems, recv_sems):
        x, y, c, chips = _place()
        me = 2 * x + y
        sibling = (x, y, 1 - c)
        rows = pl.ds(pl.multiple_of(c * rh, 8), rh)

        def cp(k, src, dst, to):
            return pltpu.make_async_remote_copy(src_ref=src, dst_ref=dst, send_sem=send_sems.at[k],
                                                recv_sem=recv_sems.at[k], device_id=to, device_id_type=MESH)

        swap = cp(0, s_ref, sib_ref, sibling)
        swap.start()
        swap.wait()
        mine_ref[...] = s_ref[rows, :] + sib_ref[rows, :]
        sends = [cp(1 + j, mine_ref, chips_ref.at[j], (cx, cy, c)) for j, (cx, cy) in enumerate(chips)]
        for cpy in sends:
            cpy.start()
        for cpy in sends:
            cpy.wait()
        where = [2 * cx + cy for cx, cy in chips]
        total = None
        for q in range(N_CHIPS):
            term = jnp.where(q == me, mine_ref[...], jnp.where(
                q == where[0], chips_ref[0], jnp.where(q == where[1], chips_ref[1], chips_ref[2])))
            total = term if total is None else total + term
        o_ref[rows, :] = total
        push = cp(4, o_ref.at[rows, :], o_ref.at[rows, :], sibling)
        push.start()
        push.wait()

    vm = pl.BlockSpec(memory_space=pltpu.VMEM)
    return pl.pallas_call(
        body, name=name, in_specs=[vm], out_specs=vm,
        out_shape=jax.ShapeDtypeStruct((rs, lanes), f32),
        scratch_shapes=[pltpu.VMEM((rs, lanes), f32), pltpu.VMEM((rh, lanes), f32),
                        pltpu.VMEM((N_CHIPS - 1, rh, lanes), f32), pltpu.SemaphoreType.DMA((5,)),
                        pltpu.SemaphoreType.DMA((5,))],
        compiler_params=pltpu.CompilerParams(vmem_limit_bytes=32 * _MB),
    )(s)


def _pad_lanes(a, width=LANES):
    return jnp.pad(a, ((0, 0), (0, width - a.shape[1])))


def _local_grads(x, tgt, wts, small, *, fwd_ride=None, late_weights=None, swap_ride=None, bwd_ride=None,
                 last_ride=None):
    t = x.shape[0]
    tm = min(t, 1024)
    d = D_MODEL
    mm = functools.partial(_matmul, tm=tm)

    dtb = _pad_lanes(small["dt_bias"])
    alog = _pad_lanes(small["a_log"])
    dsk = jnp.repeat(small["d_skip"], HEAD_DIM, axis=1)
    bsp_t = _pad_lanes(small["b_spatial"].T)
    wsp = small["w_spatial"]

    h = _rms_fwd(x, small["norm_mix_g"], name="rms_mix")
    uv = mm(h, wts["uv"], tn=2048, tk=d, out_dtypes=[f32], name="proj_uv")
    z = mm(h, wts["z"], tn=2048, tk=d, out_dtypes=[f32], name="proj_z")
    xbc = mm(h, wts["xbc"], tn=2048, tk=d, out_dtypes=[f32], name="proj_xbc")
    dtr = mm(h, wts["dt"], tn=LANES, tk=d, out_dtypes=[f32], name="proj_dt")
    gl = mm(h, wts["gate"], tn=2048, tk=d, out_dtypes=[f32], name="proj_gate")
    ya = _gmlp_fwd(uv, small["v_norm_g"], small["v_norm_b"], wsp, bsp_t, name="gmlp_fwd")
    yb, hprev, cv, *gathered = _ssd_fwd(xbc, z, dtr, small["conv_w"], small["conv_b"], dtb, alog, dsk,
                                        small["ssm_norm_g"], ride=fwd_ride, name="ssd_fwd")
    if fwd_ride is not None:
        wts = {**wts, **late_weights(gathered[0])}
    tall = functools.partial(_matmul, tm=min(t, 2048))
    pa = tall(ya, wts["pa"], tn=1024, tk=1024, out_dtypes=[f32], name="proj_a")
    tm_gate = min(t, 512)
    row_vec = [pl.BlockSpec((1, d), lambda i, j, k, half=half: (0, half)) for half in range(2)]
    gate_tiles = [pl.BlockSpec((tm_gate, d), lambda i, j, k, half=half: (i, half)) for half in range(2)]

    def merge(pb_acc, pa_t, gla, glb, bga, bgb):
        return pb_acc, _sigmoid(gla + bga) * pa_t + _sigmoid(glb + bgb) * pb_acc

    pb, merged = _matmul(yb, wts["pb"], tm=tm_gate, tn=d, tk=1024, out_dtypes=[f32, bf16], epilogue=merge,
                         extras=[pa, gl, gl, small["b_gates"], small["b_gates"]],
                         extra_specs=[None] + gate_tiles + row_vec, name="proj_b")

    def residual_norm(acc, res, g):
        x_new = res + acc
        r = lax.rsqrt(jnp.mean(x_new * x_new, axis=1, keepdims=True) + NORM_EPS)
        return x_new, x_new * r * g

    x1, h2 = mm(merged, wts["out"], tn=d, tk=1024, out_dtypes=[f32, bf16], epilogue=residual_norm,
                extras=[x, small["norm_mlp_g"]], extra_specs=[None, row_vec[0]], name="out_proj")
    act = mm(h2, wts["up"], tn=2048, tk=d, out_dtypes=[bf16],
             epilogue=lambda acc: (jnp.square(jnp.maximum(acc, 0.0)),), name="mlp_up")
    x2 = mm(act, wts["down"], tn=1024, tk=2048, out_dtypes=[f32], extras=[x1],
            epilogue=lambda acc, res: (res + acc,), name="mlp_down")

    dx2, dx2b, dgf, loss = _loss_head(x2, tgt, small["norm_final_g"], name="loss_head")
    tt = min(t, 2048)
    tn_mm = functools.partial(_matmul_tn, tt=tt)
    dw = {}
    def slab(key, chip_of):
        rows = _LATE_ROWS[key]
        if N_CHIPS * rows == 1024:
            return dict(tka=1024, tn=1024, place=lambda i, j: (ALL_CHIPS, _LATE_OFF[key] // rows))
        return dict(tka=min(rows, 1024), tn=1024, place=lambda i, j: (chip_of(i, j), _LATE_OFF[key] // min(rows, 1024)))

    dw["late"] = tn_mm(act, dx2b, name="dw_down", **slab("w_mlp_down", lambda i, j: i))
    dup = mm(dx2b, wts["down"], nt=True, tn=2048, tk=1024, out_dtypes=[bf16], extras=[act],
             epilogue=lambda acc, a2: (acc * (2.0 * jnp.sqrt(a2).astype(f32)),), name="d_act")
    dw["late"] = tn_mm(h2, dup, name="dw_up", packed=dw["late"], **slab("w_mlp_up", lambda i, j: j))
    dh2 = mm(dup, wts["up"], nt=True, tn=1024, tk=2048, out_dtypes=[f32], name="d_h2")
    dx1, dx1b, dg_mlp = _rms_bwd(x1, small["norm_mlp_g"], dh2, dx2, want_bf16=True, name="rms_mlp_bwd")
    dw["late"] = tn_mm(merged, dx1b, name="dw_out", packed=dw["late"], **slab("w_out", lambda i, j: i))
    dmerged = tall(dx1b, wts["out"], nt=True, tn=1024, tk=1024, out_dtypes=[f32], name="d_merged")
    dpa, dpb, dgl, dbg = _merge_bwd(dmerged, pa, pb, gl, small["b_gates"], name="merge_bwd")
    dw["late"] = tn_mm(ya, dpa, name="dw_pa", packed=dw["late"], **slab("w_proj_a", lambda i, j: i))
    dw["late"] = tn_mm(yb, dpb, name="dw_pb", packed=dw["late"], **slab("w_proj_b", lambda i, j: i))
    dya = tall(dpa, wts["pa"], nt=True, tn=1024, tk=1024, out_dtypes=[f32], name="d_ya")
    dyb = mm(dpb, wts["pb"], nt=True, tn=2048, tk=1024, out_dtypes=[f32], name="d_yb")
    swapped = swap_ride(dw) if swap_ride is not None else None
    duv, dwsp, dbsp_t, dvg, dvb, *got_pair = _gmlp_bwd(uv, dya, small["v_norm_g"], small["v_norm_b"], wsp, bsp_t,
                                                       ride=swapped, name="gmlp_bwd")
    ride = bwd_ride(swapped, got_pair[0]) if bwd_ride is not None else None
    dz, dxbc, ddt, dcw, dcb, ddtb, dalog, ddsk, dgs, *got = _ssd_bwd(
        xbc, cv, z, dtr, hprev, dyb, small["conv_w"], dtb, alog, dsk, small["ssm_norm_g"],
        _head_seg_matrix(), ride=ride, name="ssd_bwd")
    dw["uv"] = tn_mm(h, duv, tka=1024, tn=1024, name="dw_uv")
    dw["z"] = tn_mm(h, dz, tka=1024, tn=1024, name="dw_z")
    dw["xbc"] = tn_mm(h, dxbc, tka=1024, tn=1024, name="dw_xbc")
    dw["dt"] = tn_mm(h, ddt, tka=1024, tn=LANES, name="dw_dt")
    dw["gate"] = tn_mm(h, dgl, tka=1024, tn=1024, name="dw_gate")
    last = last_ride(dw) if last_ride is not None else None
    res = _matmul_nt_sum(
        [(duv, wts["uv"]), (dz, wts["z"]), (dxbc, wts["xbc"]), (dgl, wts["gate"]), (ddt, wts["dt"])],
        tm=tm, tks=[1024] * 4 + [LANES], ride=last, name="d_h")
    dh, got_last = (res[0], res[1]) if last is not None else (res, None)
    dx, dg_mix = _rms_bwd(x, small["norm_mix_g"], dh, dx1, want_bf16=False, name="rms_mix_bwd")

    dsmall = {
        "norm_mix_g": dg_mix, "conv_w": dcw, "conv_b": dcb, "dt_bias": ddtb[:, :N_HEADS], "a_log": dalog[:, :N_HEADS],
        "d_skip": ddsk[:, :N_HEADS], "ssm_norm_g": dgs, "v_norm_g": dvg, "v_norm_b": dvb, "w_spatial": dwsp,
        "b_spatial": dbsp_t[:, :GMLP_GROUPS].T, "b_gates": dbg, "norm_mlp_g": dg_mlp, "norm_final_g": dgf,
    }
    return loss, dx, dw, dsmall, (got[0] if got else None), got_last


_IN_SHARD = IN_PROJ // N_CHIPS
_LATE = ("w_mlp_down", "w_mlp_up", "w_proj_b", "w_proj_a", "w_out")
_LATE_ROWS = {"w_proj_a": GMLP_WIDTH // N_CHIPS, "w_proj_b": D_INNER // N_CHIPS, "w_out": D_MODEL // N_CHIPS,
              "w_mlp_up": D_MODEL, "w_mlp_down": D_FF // N_CHIPS}
_LATE_TOTAL = sum(_LATE_ROWS.values())


def _late_offsets():
    off, out = 0, {}
    for k in _LATE:
        out[k] = off
        off += _LATE_ROWS[k]
    return out


_LATE_OFF = _late_offsets()

_SMALL = ("norm_mix_g", "conv_w", "conv_b", "dt_bias", "a_log", "d_skip", "ssm_norm_g", "v_norm_g", "v_norm_b",
          "w_spatial", "b_spatial", "b_gates", "norm_mlp_g", "norm_final_g")


def _pack_small(parts):
    flat = jnp.concatenate([parts[k].reshape(-1) for k in _SMALL])
    rows = -(-flat.shape[0] // (16 * LANES)) * 16
    return jnp.pad(flat, (0, rows * LANES - flat.shape[0])).reshape(rows, LANES)


def _unpack_small(packed, shapes):
    flat = packed.reshape(-1)
    out, off = {}, 0
    for k in _SMALL:
        n = math.prod(shapes[k])
        out[k] = flat[off:off + n].reshape(shapes[k])
        off += n
    return out


def _from_chip_columns(stacked):
    _, rows, cols = stacked.shape
    return stacked.transpose(1, 0, 2).reshape(rows, N_CHIPS * cols)


def _w_in_grad_by_chip(dw):
    pieces = [dw["uv"], dw["z"], dw["xbc"], dw["dt"][:, :N_HEADS], dw["gate"]]
    bounds = [0]
    for p in pieces:
        bounds.append(bounds[-1] + p.shape[1])
    chips = []
    for k in range(N_CHIPS):
        lo, hi = k * _IN_SHARD, (k + 1) * _IN_SHARD
        parts = [p[:, max(lo, b0) - b0:min(hi, b1) - b0]
                 for p, b0, b1 in zip(pieces, bounds[:-1], bounds[1:]) if min(hi, b1) > max(lo, b0)]
        chips.append(jnp.concatenate(parts, axis=1))
    return jnp.stack(chips)


def kernel(x, norm_mix_g, w_in, conv_w, conv_b, dt_bias, a_log, d_skip, ssm_norm_g, v_norm_g, v_norm_b, w_spatial, b_spatial, b_gates, w_proj_a, w_proj_b, w_out, norm_mlp_g, w_mlp_up, w_mlp_down, norm_final_g, loss_target, m_norm_mix_g, m_w_in, m_conv_w, m_conv_b, m_dt_bias, m_a_log, m_d_skip, m_ssm_norm_g, m_v_norm_g, m_v_norm_b, m_w_spatial, m_b_spatial, m_b_gates, m_w_proj_a, m_w_proj_b, m_w_out, m_norm_mlp_g, m_w_mlp_up, m_w_mlp_down, m_norm_final_g, v_norm_mix_g, v_w_in, v_conv_w, v_conv_b, v_dt_bias, v_a_log, v_d_skip, v_ssm_norm_g, v_v_norm_g, v_v_norm_b, v_w_spatial, v_b_spatial, v_b_gates, v_w_proj_a, v_w_proj_b, v_w_out, v_norm_mlp_g, v_w_mlp_up, v_w_mlp_down, v_norm_final_g):
    given = dict(locals())
    names = ("norm_mix_g", "w_in", "conv_w", "conv_b", "dt_bias", "a_log", "d_skip", "ssm_norm_g", "v_norm_g",
             "v_norm_b", "w_spatial", "b_spatial", "b_gates", "w_proj_a", "w_proj_b", "w_out", "norm_mlp_g",
             "w_mlp_up", "w_mlp_down", "norm_final_g")
    xi, yi, ci = lax.axis_index("x"), lax.axis_index("y"), lax.axis_index("c")
    me_chip = (2 * xi + yi).astype(jnp.int32)

    def halves(a):
        return a.reshape(2, a.shape[0] // 2, a.shape[1])

    def with_own(got, shard):
        whole = lax.dynamic_update_slice(got, shard[None], (me_chip, 0, 0, 0))
        return whole.reshape(N_CHIPS, 2 * shard.shape[1], shard.shape[2])

    shard_in = halves(_cast_bf16(w_in[0], name="cast_w_in"))
    shard_late = halves(_cast_bf16(jnp.concatenate([given[k][0] for k in _LATE]), name="cast_w_late"))
    shard_conv = halves(conv_w.reshape(2 * _TAIL, -1))
    w_in_full = _from_chip_columns(with_own(_gather_shards(shard_in, name="gather_w_in"), shard_in))
    o_dt, o_gate = 2 * GMLP_WIDTH + D_INNER + CONV_DIM, 2 * GMLP_WIDTH + D_INNER + CONV_DIM + N_HEADS
    wts = {
        "uv": w_in_full[:, :2 * GMLP_WIDTH], "z": w_in_full[:, 2 * GMLP_WIDTH:2 * GMLP_WIDTH + D_INNER],
        "xbc": w_in_full[:, 2 * GMLP_WIDTH + D_INNER:o_dt], "dt": _pad_lanes(w_in_full[:, o_dt:o_gate]),
        "gate": w_in_full[:, o_gate:],
    }
    conv_all = with_own(_gather_shards(shard_conv, name="gather_conv_w"), shard_conv)
    conv_full = _from_chip_columns(conv_all.reshape(N_CHIPS, CONV_W, CONV_DIM // N_CHIPS))

    def late_weights(got):
        g_late = with_own(got, shard_late)

        def rows_of(k):
            return g_late[:, _LATE_OFF[k]:_LATE_OFF[k] + _LATE_ROWS[k]]

        return {
            "pa": rows_of("w_proj_a").reshape(GMLP_WIDTH, D_MODEL),
            "pb": rows_of("w_proj_b").reshape(D_INNER, D_MODEL), "out": rows_of("w_out").reshape(D_MODEL, D_MODEL),
            "up": _from_chip_columns(rows_of("w_mlp_up")), "down": rows_of("w_mlp_down").reshape(D_FF, D_MODEL),
        }

    small = {
        "norm_mix_g": norm_mix_g, "conv_w": conv_full, "conv_b": conv_b, "dt_bias": dt_bias, "a_log": a_log,
        "d_skip": d_skip, "ssm_norm_g": ssm_norm_g, "v_norm_g": v_norm_g, "v_norm_b": v_norm_b,
        "w_spatial": w_spatial[0], "b_spatial": b_spatial[0], "b_gates": b_gates, "norm_mlp_g": norm_mlp_g,
        "norm_final_g": norm_final_g.reshape(1, D_MODEL),
    }

    c_idx = ci.astype(jnp.int32).reshape(1)
    place = jnp.stack([me_chip, ci.astype(jnp.int32)])
    partials = {}

    def reduced_shard(tag, got_chips):
        own = _rs_add_chips(*partials[tag], got_chips, place, name="rs_add_chips_" + tag)
        both = _rs_join_halves(own, name="rs_join_" + tag)
        return both.reshape(2 * both.shape[1], both.shape[2])

    def late_grads(dw):
        return dw["late"].reshape(N_CHIPS, 2, _LATE_TOTAL // 2, D_MODEL)

    def late_partials(g, got_pair):
        partials["late"] = (g, got_pair)
        return _rs_add_pair(g, got_pair, c_idx, name="rs_add_pair_late")

    def in_partials(dw):
        g = _w_in_grad_by_chip(dw).reshape(N_CHIPS, 2, D_MODEL // 2, _IN_SHARD)
        got_pair = _rs_swap_halves(g, name="rs_swap_in")
        partials["in"] = (g, got_pair)
        return _rs_add_pair(g, got_pair, c_idx, name="rs_add_pair_in")

    loss_part, grad_x, dw, dsmall, got_late, got_in = _local_grads(
        x[0], loss_target[0], wts, small, fwd_ride=shard_late, late_weights=late_weights, swap_ride=late_grads,
        bwd_ride=late_partials, last_ride=in_partials)
    loss = lax.psum(loss_part[0, 0], ("x", "y", "c"))
    g_late = reduced_shard("late", got_late)
    g_in_shard = reduced_shard("in", got_in)

    small_shapes = {k: dsmall[k].shape for k in _SMALL}
    red = _unpack_small(_all_reduce_small(_pack_small(dsmall), name="all_reduce_small"), small_shapes)
    conv_cols = CONV_DIM // N_CHIPS
    red["conv_w"] = lax.dynamic_slice_in_dim(red["conv_w"], me_chip * conv_cols, conv_cols, axis=1)

    grads, deltas, new_m, new_v = {}, {}, {}, {}
    for k in ("w_in",) + _LATE:
        g2 = g_in_shard if k == "w_in" else g_late[_LATE_OFF[k]:_LATE_OFF[k] + _LATE_ROWS[k]]
        dlt, m2, v2 = _adamw(given[k][0], g2, given["m_" + k][0], given["v_" + k][0], name="adamw_" + k)
        grads[k], deltas[k], new_m[k], new_v[k] = g2, dlt, m2, v2
    adam_shapes = dict(small_shapes)
    adam_shapes["conv_w"] = (CONV_W, conv_cols)

    def small_pack_of(prefix):
        return _pack_small({k: given[prefix + k].reshape(adam_shapes[k]) for k in _SMALL})

    dlt_s, m_s, v_s = _adamw(small_pack_of(""), _pack_small(red), small_pack_of("m_"), small_pack_of("v_"),
                             name="adamw_small")
    for dst, packed in ((deltas, dlt_s), (new_m, m_s), (new_v, v_s)):
        dst.update(_unpack_small(packed, adam_shapes))
    grads.update(red)

    def shaped(dct):
        return [dct[k].reshape(given[k].shape) for k in names]

    return (loss, grad_x[None], *shaped(grads), *shaped(deltas), *shaped(new_m), *shaped(new_v))
```

```python
import functools
import math

import jax
import jax.numpy as jnp
from jax import lax
from jax.experimental import pallas as pl
from jax.experimental.pallas import tpu as pltpu

f32 = jnp.float32
bf16 = jnp.bfloat16

D_MODEL = 1024
CHUNK = 128
GMLP_WIDTH = 1024
GMLP_GROUPS = 8
D_INNER = 2048
HEAD_DIM = 64
N_HEADS = 32
N_GROUPS = 8
HEADS_PER_GROUP = 4
GROUP_W = HEADS_PER_GROUP * HEAD_DIM
D_STATE = 128
CONV_W = 4
CONV_DIM = 4096
D_FF = 4096
IN_PROJ = 10272
NORM_EPS = 1e-6
N_CHIPS = 4
LANES = 128

ADAM_LR = 0.001
ADAM_B1 = 0.9
ADAM_B2 = 0.999
ADAM_EPS = 1e-08
ADAM_WD = 0.01
ADAM_STEP = 10

MESH = pl.DeviceIdType.MESH
_NT = (((1,), (1,)), ((), ()))
_NN = (((1,), (0,)), ((), ()))
_TN = (((0,), (0,)), ((), ()))
_MB = 2 ** 20


def _params(sem, vmem_mb=48):
    return pltpu.CompilerParams(dimension_semantics=sem, vmem_limit_bytes=vmem_mb * _MB)


def _dot(a, b, dims=_NN):
    return lax.dot_general(a.astype(bf16), b.astype(bf16), dims, preferred_element_type=f32)


def _dot32(a, b):
    return jnp.dot(a, b, preferred_element_type=f32, precision=lax.Precision.HIGHEST)


def _sigmoid(x):
    return 1.0 / (1.0 + jnp.exp(-x))


def _sum_all(a):
    return jnp.sum(jnp.sum(a, axis=1, keepdims=True), axis=0, keepdims=True)


def _iota(shape, dim):
    return lax.broadcasted_iota(jnp.int32, shape, dim)


def _matmul(a, b, *, nt=False, tm, tn, tk, out_dtypes, epilogue=None, extras=(), extra_specs=None, name):
    m, k_dim = a.shape
    n = b.shape[0] if nt else b.shape[1]
    nk = k_dim // tk
    ne, no = len(extras), len(out_dtypes)
    dims = _NT if nt else _NN

    def body(*refs):
        a_ref, b_ref = refs[0], refs[1]
        ex = refs[2:2 + ne]
        outs = refs[2 + ne:2 + ne + no]

        def finish(acc):
            vals = epilogue(acc, *[e[...] for e in ex]) if epilogue is not None else (acc,)
            for o, v in zip(outs, vals):
                o[...] = v.astype(o.dtype)

        part = lax.dot_general(a_ref[...], b_ref[...], dims, preferred_element_type=f32)
        if nk == 1:
            finish(part)
        else:
            acc_ref = refs[-1]
            kk = pl.program_id(2)

            @pl.when(kk == 0)
            def _():
                acc_ref[...] = part

            @pl.when(kk > 0)
            def _():
                acc_ref[...] += part

            @pl.when(kk == nk - 1)
            def _():
                finish(acc_ref[...])

    b_spec = pl.BlockSpec((tn, tk), lambda i, j, k: (j, k)) if nt else pl.BlockSpec((tk, tn), lambda i, j, k: (k, j))
    tile = pl.BlockSpec((tm, tn), lambda i, j, k: (i, j))
    ex_specs = [tile if s is None else s for s in (extra_specs or [None] * ne)]
    outs = pl.pallas_call(
        body, name=name, grid=(m // tm, n // tn, nk),
        in_specs=[pl.BlockSpec((tm, tk), lambda i, j, k: (i, k)), b_spec] + ex_specs,
        out_specs=[tile] * no,
        out_shape=[jax.ShapeDtypeStruct((m, n), dt) for dt in out_dtypes],
        scratch_shapes=[pltpu.VMEM((tm, tn), f32)] if nk > 1 else [],
        compiler_params=_params(("parallel", "parallel", "arbitrary")),
    )(a, b, *extras)
    return outs if no > 1 else outs[0]


def _matmul_nt_sum(pairs, *, tm, tks, ride=None, name):
    m = pairs[0][0].shape[0]
    n = pairs[0][1].shape[0]
    nblk = [a.shape[1] // tk for (a, _), tk in zip(pairs, tks)]
    starts = [sum(nblk[:p]) for p in range(len(pairs))]
    nk = sum(nblk)
    npairs = len(pairs)
    ni = m // tm
    riding = ride is not None

    def body(*refs):
        rest = refs[2 * npairs:]
        if riding:
            ride_ref, o_ref, got_ref, acc_ref, send_sems, recv_sems = rest
        else:
            o_ref, acc_ref = rest
        i, kk = pl.program_id(0), pl.program_id(1)
        if riding:
            start, finish = _scatter_protocol(ride_ref, got_ref, send_sems, recv_sems)
            pl.when((i == 0) & (kk == 0))(start)

        @pl.when(kk == 0)
        def _():
            acc_ref[...] = jnp.zeros_like(acc_ref)

        for p in range(npairs):
            @pl.when((kk >= starts[p]) & (kk < starts[p] + nblk[p]))
            def _(p=p):
                acc_ref[...] += lax.dot_general(refs[2 * p][...], refs[2 * p + 1][...], _NT, preferred_element_type=f32)

        @pl.when(kk == nk - 1)
        def _():
            o_ref[...] = acc_ref[...]

        if riding:
            pl.when((i == ni - 1) & (kk == nk - 1))(finish)

    in_specs, args = [], []
    for p, (a, b) in enumerate(pairs):
        def kblock(k, s=starts[p], nb=nblk[p]):
            return jnp.clip(k - s, 0, nb - 1)
        in_specs.append(pl.BlockSpec((tm, tks[p]), lambda i, k, kb=kblock: (i, kb(k))))
        in_specs.append(pl.BlockSpec((n, tks[p]), lambda i, k, kb=kblock: (0, kb(k))))
        args += [a, b]
    tile = pl.BlockSpec((tm, n), lambda i, k: (i, 0))
    outs = pl.pallas_call(
        body, name=name, grid=(ni, nk), in_specs=in_specs + [_ANY] * riding, out_specs=[tile] + [_ANY] * riding,
        out_shape=[jax.ShapeDtypeStruct((m, n), f32)]
        + ([jax.ShapeDtypeStruct((N_CHIPS - 1,) + ride.shape[1:], ride.dtype)] if riding else []),
        scratch_shapes=[pltpu.VMEM((tm, n), f32)] + (list(_SCATTER_SCRATCH) if riding else []),
        compiler_params=_params(("arbitrary", "arbitrary"), vmem_mb=56),
    )(*args, *([ride] if riding else []))
    return outs if riding else outs[0]


ALL_CHIPS = "all"


def _matmul_tn(a, b, *, tka, tn, tt, name, packed=None, place=None):
    t, ka = a.shape
    n = b.shape[1]
    spread = place is not None and place(0, 0)[0] is ALL_CHIPS

    def body(a_ref, b_ref, *rest):
        o_ref = rest[-1]
        part = lax.dot_general(a_ref[...], b_ref[...], _TN, preferred_element_type=f32).reshape(o_ref.shape)
        kk = pl.program_id(2)

        @pl.when(kk == 0)
        def _():
            o_ref[...] = part

        @pl.when(kk > 0)
        def _():
            o_ref[...] += part

    in_specs = [pl.BlockSpec((tt, tka), lambda i, j, k: (k, i)), pl.BlockSpec((tt, tn), lambda i, j, k: (k, j))]
    if place is None:
        out_spec = pl.BlockSpec((tka, tn), lambda i, j, k: (i, j))
        out_shape = jax.ShapeDtypeStruct((ka, n), f32)
    elif spread:
        out_spec = pl.BlockSpec((N_CHIPS, tka // N_CHIPS, tn), lambda i, j, k: (0, place(i, j)[1], 0))
        out_shape = jax.ShapeDtypeStruct((N_CHIPS, _LATE_TOTAL, tn), f32)
    else:
        out_spec = pl.BlockSpec((None, tka, tn), lambda i, j, k: (*place(i, j), 0))
        out_shape = jax.ShapeDtypeStruct((N_CHIPS, _LATE_TOTAL, tn), f32)
    aliased = packed is not None
    return pl.pallas_call(
        body, name=name, grid=(ka // tka, n // tn, t // tt),
        in_specs=in_specs + [_ANY] * aliased, out_specs=out_spec, out_shape=out_shape,
        input_output_aliases={2: 0} if aliased else {},
        compiler_params=_params(("parallel", "parallel", "arbitrary")),
    )(a, b, *([packed] if aliased else []))


def _row_tile(t):
    return min(t, 512)


def _rms_fwd(x, g, *, name):
    t, d = x.shape
    tr = _row_tile(t)

    def body(x_ref, g_ref, h_ref):
        xv = x_ref[...]
        r = lax.rsqrt(jnp.mean(xv * xv, axis=1, keepdims=True) + NORM_EPS)
        h_ref[...] = (xv * r * g_ref[...]).astype(bf16)

    return pl.pallas_call(
        body, name=name, grid=(t // tr,),
        in_specs=[pl.BlockSpec((tr, d), lambda i: (i, 0)), pl.BlockSpec((1, d), lambda i: (0, 0))],
        out_specs=pl.BlockSpec((tr, d), lambda i: (i, 0)),
        out_shape=jax.ShapeDtypeStruct((t, d), bf16),
        compiler_params=_params(("parallel",)),
    )(x, g)


def _rms_bwd(xin, g, dh, dres, *, want_bf16, name):
    t, d = xin.shape
    tr = _row_tile(t)

    def body(x_ref, g_ref, dh_ref, dres_ref, dx_ref, *rest):
        dg_ref = rest[-1]
        xv = x_ref[...]
        r = lax.rsqrt(jnp.mean(xv * xv, axis=1, keepdims=True) + NORM_EPS)
        xn = xv * r
        dhv = dh_ref[...]
        dxn = dhv * g_ref[...]
        dx = dres_ref[...] + r * (dxn - xn * jnp.mean(dxn * xn, axis=1, keepdims=True))
        dx_ref[...] = dx
        if want_bf16:
            rest[0][...] = dx.astype(bf16)
        part = jnp.sum(dhv * xn, axis=0, keepdims=True)

        @pl.when(pl.program_id(0) == 0)
        def _():
            dg_ref[...] = part

        @pl.when(pl.program_id(0) > 0)
        def _():
            dg_ref[...] += part

    row = pl.BlockSpec((tr, d), lambda i: (i, 0))
    vec = pl.BlockSpec((1, d), lambda i: (0, 0))
    out_shape = [jax.ShapeDtypeStruct((t, d), f32)] + ([jax.ShapeDtypeStruct((t, d), bf16)] if want_bf16 else []) \
        + [jax.ShapeDtypeStruct((1, d), f32)]
    return pl.pallas_call(
        body, name=name, grid=(t // tr,),
        in_specs=[row, vec, row, row],
        out_specs=[row] + ([row] if want_bf16 else []) + [vec],
        out_shape=out_shape,
        compiler_params=_params(("arbitrary",)),
    )(xin, g, dh, dres)


def _loss_head(x2, tgt, g, *, name):
    t, d = x2.shape
    tr = _row_tile(t)

    def body(x_ref, t_ref, g_ref, dx_ref, dxb_ref, dg_ref, loss_ref):
        xv = x_ref[...]
        gv = g_ref[...]
        r = lax.rsqrt(jnp.mean(xv * xv, axis=1, keepdims=True) + NORM_EPS)
        xn = xv * r
        e = xn * gv - t_ref[...]
        lpart = jnp.zeros((1, LANES), f32) + 0.5 * _sum_all(jnp.mean(e * e, axis=1, keepdims=True))
        dy = e * (1.0 / d)
        dxn = dy * gv
        dx = r * (dxn - xn * jnp.mean(dxn * xn, axis=1, keepdims=True))
        dx_ref[...] = dx
        dxb_ref[...] = dx.astype(bf16)
        gpart = jnp.sum(dy * xn, axis=0, keepdims=True)

        @pl.when(pl.program_id(0) == 0)
        def _():
            dg_ref[...] = gpart
            loss_ref[...] = lpart

        @pl.when(pl.program_id(0) > 0)
        def _():
            dg_ref[...] += gpart
            loss_ref[...] += lpart

    row = pl.BlockSpec((tr, d), lambda i: (i, 0))
    vec = pl.BlockSpec((1, d), lambda i: (0, 0))
    return pl.pallas_call(
        body, name=name, grid=(t // tr,),
        in_specs=[row, row, vec],
        out_specs=[row, row, vec, pl.BlockSpec((1, LANES), lambda i: (0, 0))],
        out_shape=[jax.ShapeDtypeStruct((t, d), f32), jax.ShapeDtypeStruct((t, d), bf16),
                   jax.ShapeDtypeStruct((1, d), f32), jax.ShapeDtypeStruct((1, LANES), f32)],
        compiler_params=_params(("arbitrary",)),
    )(x2, tgt, g)


def _merge_bwd(dm, pa, pb, gl, bg, *, name):
    t, d = pa.shape
    tr = _row_tile(t)

    def body(dm_ref, pa_ref, pb_ref, gla_ref, glb_ref, bga_ref, bgb_ref, dpa_ref, dpb_ref, dgl_ref, dbg_ref):
        dmv = dm_ref[...]
        ga = _sigmoid(gla_ref[...] + bga_ref[...])
        gb = _sigmoid(glb_ref[...] + bgb_ref[...])
        dpa_ref[...] = (dmv * ga).astype(bf16)
        dpb_ref[...] = (dmv * gb).astype(bf16)
        dla = dmv * pa_ref[...] * ga * (1.0 - ga)
        dlb = dmv * pb_ref[...] * gb * (1.0 - gb)
        dgl_ref[:, :d] = dla.astype(bf16)
        dgl_ref[:, d:] = dlb.astype(bf16)
        sa = jnp.sum(dla, axis=0, keepdims=True)
        sb = jnp.sum(dlb, axis=0, keepdims=True)

        @pl.when(pl.program_id(0) == 0)
        def _():
            dbg_ref[:, :d] = sa
            dbg_ref[:, d:] = sb

        @pl.when(pl.program_id(0) > 0)
        def _():
            dbg_ref[:, :d] += sa
            dbg_ref[:, d:] += sb

    row = pl.BlockSpec((tr, d), lambda i: (i, 0))
    return pl.pallas_call(
        body, name=name, grid=(t // tr,),
        in_specs=[row, row, row, row, pl.BlockSpec((tr, d), lambda i: (i, 1)),
                  pl.BlockSpec((1, d), lambda i: (0, 0)), pl.BlockSpec((1, d), lambda i: (0, 1))],
        out_specs=[row, row, pl.BlockSpec((tr, 2 * d), lambda i: (i, 0)), pl.BlockSpec((1, 2 * d), lambda i: (0, 0))],
        out_shape=[jax.ShapeDtypeStruct((t, d), bf16), jax.ShapeDtypeStruct((t, d), bf16),
                   jax.ShapeDtypeStruct((t, 2 * d), bf16), jax.ShapeDtypeStruct((1, 2 * d), f32)],
        compiler_params=_params(("arbitrary",)),
    )(dm, pa, pb, gl, gl, bg, bg)


_INV_SQRT2 = 1.0 / math.sqrt(2.0)
_INV_SQRT2PI = 1.0 / math.sqrt(2.0 * math.pi)


def _gmlp_common(uv, vg, vb, with_grad=False):
    cdf = 0.5 * (1.0 + lax.erf(uv * _INV_SQRT2))
    zz = uv * cdf
    u, vhat, rstd, vn = _gmlp_norm(zz, vg, vb)
    if not with_grad:
        return u, vhat, rstd, vn
    return u, vhat, rstd, vn, cdf + uv * jnp.exp(-0.5 * uv * uv) * _INV_SQRT2PI


def _gmlp_norm(zz, vg, vb):
    u = zz[:, :GMLP_WIDTH]
    v = zz[:, GMLP_WIDTH:]
    mu = jnp.mean(v, axis=1, keepdims=True)
    vc = v - mu
    rstd = lax.rsqrt(jnp.mean(vc * vc, axis=1, keepdims=True) + NORM_EPS)
    vhat = vc * rstd
    vn = vhat * vg + vb
    return u, vhat, rstd, vn


def _gmlp_fwd(uv, vg, vb, wsp, bsp_t, *, name):
    t = uv.shape[0]
    per_step = 8 if t % (8 * CHUNK) == 0 else 1
    rows = per_step * CHUNK

    def body(uv_ref, vg_ref, vb_ref, w_ref, b_ref, y_ref):
        tril = _iota((CHUNK, CHUNK), 0) >= _iota((CHUNK, CHUNK), 1)
        bt = b_ref[...]
        for q in range(per_step):
            qs = slice(q * CHUNK, (q + 1) * CHUNK)
            u, _, _, vn = _gmlp_common(uv_ref[qs, :], vg_ref[...], vb_ref[...])
            for g in range(GMLP_GROUPS):
                sl = slice(g * CHUNK, (g + 1) * CHUNK)
                w = jnp.where(tril, w_ref[g], 0.0)
                s = _dot(w, vn[:, sl]) + bt[:, g:g + 1]
                y_ref[qs, sl] = (u[:, sl] * s).astype(bf16)

    return pl.pallas_call(
        body, name=name, grid=(t // rows,),
        in_specs=[pl.BlockSpec((rows, 2 * GMLP_WIDTH), lambda c: (c, 0)),
                  pl.BlockSpec((1, GMLP_WIDTH), lambda c: (0, 0)), pl.BlockSpec((1, GMLP_WIDTH), lambda c: (0, 0)),
                  pl.BlockSpec((GMLP_GROUPS, CHUNK, CHUNK), lambda c: (0, 0, 0)),
                  pl.BlockSpec((CHUNK, LANES), lambda c: (0, 0))],
        out_specs=pl.BlockSpec((rows, GMLP_WIDTH), lambda c: (c, 0)),
        out_shape=jax.ShapeDtypeStruct((t, GMLP_WIDTH), bf16),
        compiler_params=_params(("parallel",)),
    )(uv, vg, vb, wsp, bsp_t)


def _gmlp_bwd(uv, dya, vg, vb, wsp, bsp_t, *, ride=None, name):
    t = uv.shape[0]
    per_step = 8 if t % (8 * CHUNK) == 0 else 1
    rows = per_step * CHUNK
    steps = t // rows
    riding = ride is not None

    def body(*refs):
        uv_ref, dy_ref, vg_ref, vb_ref, w_ref, b_ref = refs[:6]
        duv_ref, dw_ref, db_ref, dvg_ref, dvb_ref = refs[6 + riding:11 + riding]
        first = pl.program_id(0) == 0
        if riding:
            start, finish = _swap_protocol(refs[6], refs[12], refs[13], refs[14])
            pl.when(first)(start)

        @pl.when(first)
        def _():
            dw_ref[...] = jnp.zeros_like(dw_ref)
            db_ref[...] = jnp.zeros_like(db_ref)
            dvg_ref[...] = jnp.zeros_like(dvg_ref)
            dvb_ref[...] = jnp.zeros_like(dvb_ref)

        vgv = vg_ref[...]
        tril = _iota((CHUNK, CHUNK), 0) >= _iota((CHUNK, CHUNK), 1)
        lane = _iota((CHUNK, LANES), 1)
        bt = b_ref[...]
        for q in range(per_step):
            qs = slice(q * CHUNK, (q + 1) * CHUNK)
            u, vhat, rstd, vn, gelu_grad = _gmlp_common(uv_ref[qs, :], vgv, vb_ref[...], with_grad=True)
            dy = dy_ref[qs, :]
            ds_all = dy * u
            dbacc = jnp.zeros((CHUNK, LANES), f32)
            dvh_parts = []
            for g in range(GMLP_GROUPS):
                sl = slice(g * CHUNK, (g + 1) * CHUNK)
                w = jnp.where(tril, w_ref[g], 0.0)
                vng = vn[:, sl]
                s = _dot(w, vng) + bt[:, g:g + 1]
                ds = ds_all[:, sl]
                duv_ref[qs, sl] = (dy[:, sl] * s * gelu_grad[:, sl]).astype(bf16)
                dw_ref[g] += jnp.where(tril, _dot(ds, vng, _NT), 0.0)
                dbacc = dbacc + jnp.where(lane == g, jnp.sum(ds, axis=1, keepdims=True), 0.0)
                dvn = _dot(w, ds, _TN)
                vh = vhat[:, sl]
                dvg_ref[:, sl] += jnp.sum(dvn * vh, axis=0, keepdims=True)
                dvb_ref[:, sl] += jnp.sum(dvn, axis=0, keepdims=True)
                dvh_parts.append(dvn * vgv[:, sl])
            db_ref[...] += dbacc
            dvhat = jnp.concatenate(dvh_parts, axis=1)
            m1 = jnp.mean(dvhat, axis=1, keepdims=True)
            m2 = jnp.mean(dvhat * vhat, axis=1, keepdims=True)
            dv = rstd * (dvhat - m1 - vhat * m2)
            duv_ref[qs, GMLP_WIDTH:] = (dv * gelu_grad[:, GMLP_WIDTH:]).astype(bf16)
        if riding:
            pl.when(pl.program_id(0) == steps - 1)(finish)

    vec = pl.BlockSpec((1, GMLP_WIDTH), lambda c: (0, 0))
    return pl.pallas_call(
        body, name=name, grid=(steps,),
        in_specs=[pl.BlockSpec((rows, 2 * GMLP_WIDTH), lambda c: (c, 0)),
                  pl.BlockSpec((rows, GMLP_WIDTH), lambda c: (c, 0)), vec, vec,
                  pl.BlockSpec((GMLP_GROUPS, CHUNK, CHUNK), lambda c: (0, 0, 0)),
                  pl.BlockSpec((CHUNK, LANES), lambda c: (0, 0))] + [_ANY] * riding,
        out_specs=[pl.BlockSpec((rows, 2 * GMLP_WIDTH), lambda c: (c, 0)),
                   pl.BlockSpec((GMLP_GROUPS, CHUNK, CHUNK), lambda c: (0, 0, 0)),
                   pl.BlockSpec((CHUNK, LANES), lambda c: (0, 0)), vec, vec] + [_ANY] * riding,
        out_shape=[jax.ShapeDtypeStruct((t, 2 * GMLP_WIDTH), bf16),
                   jax.ShapeDtypeStruct((GMLP_GROUPS, CHUNK, CHUNK), f32),
                   jax.ShapeDtypeStruct((CHUNK, LANES), f32),
                   jax.ShapeDtypeStruct((1, GMLP_WIDTH), f32), jax.ShapeDtypeStruct((1, GMLP_WIDTH), f32)]
        + ([jax.ShapeDtypeStruct(ride.shape[:1] + ride.shape[2:], ride.dtype)] if riding else []),
        scratch_shapes=list(_SWAP_SCRATCH) if riding else [],
        compiler_params=_params(("arbitrary",)),
    )(uv, dya, vg, vb, wsp, bsp_t, *([ride] if riding else []))


_CONV_COLS = 512
_B0, _C0 = D_INNER, D_INNER + N_GROUPS * D_STATE


_TAIL = 8


def _conv_silu(cur_ref, tail_ref, w_ref, b_ref, has_prev, xc_ref, cv_ref):
    row = _iota((_TAIL, _CONV_COLS), 0)
    for j in range(CONV_DIM // _CONV_COLS):
        sl = slice(j * _CONV_COLS, (j + 1) * _CONV_COLS)
        cur = cur_ref[:, sl]
        tail = jnp.where(has_prev, tail_ref[:, sl], 0.0)
        acc = cur * w_ref[CONV_W - 1:CONV_W, sl] + b_ref[:, sl]
        for s in range(1, CONV_W):
            rolled = pltpu.roll(cur, s, 0)
            top = jnp.where(row >= s, rolled[:_TAIL], pltpu.roll(tail, s, 0))
            sh = jnp.concatenate([top, rolled[_TAIL:]], axis=0)
            acc = acc + sh * w_ref[CONV_W - 1 - s:CONV_W - s, sl]
        cv_ref[:, sl] = acc
        xc_ref[:, sl] = acc * _sigmoid(acc)


def _col_bcast(mat, h):
    return jnp.broadcast_to(mat[:, h:h + 1], (CHUNK, LANES))


def _head_expand(cols):
    lo = _iota((CHUNK, LANES), 1) < HEAD_DIM
    return jnp.concatenate([jnp.where(lo, cols[2 * j], cols[2 * j + 1]) for j in range(N_HEADS // 2)], axis=1)


def _ssd_chunk_scalars(dtr, dtb, alog):
    xdt_pre = dtr + dtb
    dtv = jnp.maximum(xdt_pre, 0.0) + jnp.log(1.0 + jnp.exp(-jnp.abs(xdt_pre)))
    a = -jnp.exp(alog)
    ltri = (_iota((CHUNK, CHUNK), 0) >= _iota((CHUNK, CHUNK), 1)).astype(f32)
    cs = _dot32(ltri, dtv * a)
    csb = [_col_bcast(cs, h) for h in range(N_HEADS)]
    cs_x = _head_expand(csb)
    dt_x = _head_expand([_col_bcast(dtv, h) for h in range(N_HEADS)])
    cl_x = cs_x[CHUNK - 1:CHUNK, :]
    return dict(xdt_pre=xdt_pre, dtv=dtv, a=a, cs=cs, cs_t=cs.T, csb=csb, dt_x=dt_x, e_x=jnp.exp(cs_x),
                dec_x=jnp.exp(cl_x - cs_x), dk_x=jnp.exp(cl_x))


def _head_masks():
    lane = _iota((CHUNK, GROUP_W), 1)
    return [(lane >= r * HEAD_DIM) & (lane < (r + 1) * HEAD_DIM) for r in range(HEADS_PER_GROUP)]


def _stack_heads(a, masks):
    return jnp.concatenate([jnp.where(m, a, 0.0) for m in masks], axis=0).astype(bf16)


def _seg_sum(a, seg):
    hi = a.astype(jnp.bfloat16)
    lo = (a - hi.astype(f32)).astype(jnp.bfloat16)
    return (lax.dot_general(hi, seg, _NN, preferred_element_type=f32)
            + lax.dot_general(lo, seg, _NN, preferred_element_type=f32))


def _head_seg_matrix():
    return (_iota((D_INNER, LANES), 0) // HEAD_DIM == _iota((D_INNER, LANES), 1)).astype(jnp.bfloat16)


def _ssd_fwd(xbc, z, dtr, cw, cb, dtb, alog, dsk_x, gs, *, ride=None, name):
    t = xbc.shape[0]
    nc = t // CHUNK
    tiles = CHUNK // _TAIL

    def body(*refs):
        cur_ref, tail_ref, z_ref, dtr_ref, cw_ref, cb_ref, dtb_ref, alog_ref, dsk_ref, gs_ref = refs[:10]
        if ride is None:
            yb_ref, hp_ref, cv_ref, state_ref, xc_ref = refs[10:]
        else:
            ride_ref, yb_ref, hp_ref, cv_ref, got_ref, state_ref, xc_ref, send_sems, recv_sems = refs[10:]
        c = pl.program_id(0)
        if ride is not None:
            start, relay, finish = _gather_protocol(ride_ref, got_ref, send_sems, recv_sems)
            pl.when(c == 0)(start)
            pl.when(c == nc // 2)(relay)

        @pl.when(c == 0)
        def _():
            state_ref[...] = jnp.zeros_like(state_ref)

        _conv_silu(cur_ref, tail_ref, cw_ref, cb_ref, c > 0, xc_ref, cv_ref)
        sc = _ssd_chunk_scalars(dtr_ref[...], dtb_ref[...], alog_ref[...])
        tril = _iota((CHUNK, CHUNK), 0) >= _iota((CHUNK, CHUNK), 1)
        masks = _head_masks()
        hp_ref[0] = state_ref[...]
        for g in range(N_GROUPS):
            gsl = slice(g * GROUP_W, (g + 1) * GROUP_W)
            xs_g = xc_ref[:, gsl]
            bg = xc_ref[:, _B0 + g * D_STATE:_B0 + (g + 1) * D_STATE]
            cg = xc_ref[:, _C0 + g * D_STATE:_C0 + (g + 1) * D_STATE]
            xdt_g = xs_g * sc["dt_x"][:, gsl]
            cbm = _dot(cg, bg, _NT)
            mw = jnp.concatenate(
                [cbm * jnp.exp(jnp.where(tril, sc["csb"][h] - sc["cs_t"][h:h + 1, :], -1e30))
                 for h in range(g * HEADS_PER_GROUP, (g + 1) * HEADS_PER_GROUP)], axis=1)
            ht_g = state_ref[:, gsl]
            y_g = _dot(mw, _stack_heads(xdt_g, masks)) + sc["e_x"][:, gsl] * _dot(cg, ht_g) + dsk_ref[:, gsl] * xs_g
            state_ref[:, gsl] = ht_g * sc["dk_x"][:, gsl] + _dot(bg, xdt_g * sc["dec_x"][:, gsl], _TN)
            zg = z_ref[:, gsl]
            yg = y_g * zg * _sigmoid(zg)
            rs = lax.rsqrt(jnp.mean(yg * yg, axis=1, keepdims=True) + NORM_EPS)
            yb_ref[:, gsl] = (yg * rs * gs_ref[:, gsl]).astype(bf16)
        if ride is not None:
            pl.when(c == nc - 1)(finish)

    def chunk(w):
        return pl.BlockSpec((CHUNK, w), lambda c: (c, 0))

    def const(shape):
        return pl.BlockSpec(shape, lambda c: (0,) * len(shape))

    riding = ride is not None
    return pl.pallas_call(
        body, name=name, grid=(nc,),
        in_specs=[chunk(CONV_DIM), pl.BlockSpec((_TAIL, CONV_DIM), lambda c: (jnp.maximum(c * tiles - 1, 0), 0)),
                  chunk(D_INNER), chunk(LANES), const((CONV_W, CONV_DIM)), const((1, CONV_DIM)),
                  const((1, LANES)), const((1, LANES)), const((1, D_INNER)), const((1, D_INNER))] + [_ANY] * riding,
        out_specs=[chunk(D_INNER), pl.BlockSpec((1, D_STATE, D_INNER), lambda c: (c, 0, 0)), chunk(CONV_DIM)]
        + [_ANY] * riding,
        out_shape=[jax.ShapeDtypeStruct((t, D_INNER), bf16), jax.ShapeDtypeStruct((nc, D_STATE, D_INNER), f32),
                   jax.ShapeDtypeStruct((t, CONV_DIM), f32)]
        + ([jax.ShapeDtypeStruct((N_CHIPS,) + ride.shape, ride.dtype)] if riding else []),
        scratch_shapes=[pltpu.VMEM((D_STATE, D_INNER), f32), pltpu.VMEM((CHUNK, CONV_DIM), f32)]
        + (list(_GATHER_SCRATCH) if riding else []),
        compiler_params=_params(("arbitrary",)),
    )(xbc, xbc, z, dtr, cw, cb, dtb, alog, dsk_x, gs, *([ride] if riding else []))


def _ssd_bwd(xbc, cv, z, dtr, hprev, dyb, cw, dtb, alog, dsk_x, gs, seg, *, ride=None, name):
    t = xbc.shape[0]
    nc = t // CHUNK

    def body(*refs):
        (cur_ref, cv_ref, z_ref, dtr_ref, hp_ref, dyb_ref, cw_ref, dtb_ref, alog_ref, dsk_ref, gs_ref,
         seg_ref) = refs[:12]
        rest = refs[12:]
        if ride is not None:
            ride_ref, got_ref, send_sems, recv_sems = rest[0], rest[10], rest[-2], rest[-1]
            rest = rest[1:10] + rest[11:-2]
        (dz_ref, dxbc_ref, ddt_ref, dcw_ref, dcb_ref, ddtb_ref, dalog_ref, ddsk_ref, dgs_ref,
         dh_ref, dcnext_ref, xc_ref, dxc_ref, x13_ref, x2_ref, rows_ref) = rest
        i = pl.program_id(0)
        if ride is not None:
            start, finish = _scatter_protocol(ride_ref, got_ref, send_sems, recv_sems)
            pl.when(i == 0)(start)

        @pl.when(i == 0)
        def _():
            for ref in (dh_ref, dcnext_ref, dcw_ref, dcb_ref, ddtb_ref, dalog_ref, ddsk_ref, dgs_ref, rows_ref):
                ref[...] = jnp.zeros_like(ref)

        for j in range(CONV_DIM // _CONV_COLS):
            sl = slice(j * _CONV_COLS, (j + 1) * _CONV_COLS)
            cvv = cv_ref[:, sl]
            xc_ref[:, sl] = cvv * _sigmoid(cvv)
        sc = _ssd_chunk_scalars(dtr_ref[...], dtb_ref[...], alog_ref[...])
        tril = _iota((CHUNK, CHUNK), 0) >= _iota((CHUNK, CHUNK), 1)
        triu = _iota((CHUNK, CHUNK), 0) <= _iota((CHUNK, CHUNK), 1)
        masks = _head_masks()
        rowh = _iota((N_HEADS, CHUNK), 0)
        dcs_t = jnp.zeros((N_HEADS, CHUNK), f32)
        for g in range(N_GROUPS):
            gsl = slice(g * GROUP_W, (g + 1) * GROUP_W)
            xs_g = xc_ref[:, gsl]
            bg = xc_ref[:, _B0 + g * D_STATE:_B0 + (g + 1) * D_STATE]
            cg = xc_ref[:, _C0 + g * D_STATE:_C0 + (g + 1) * D_STATE]
            dt_g, e_g, dec_g, dk_g = sc["dt_x"][:, gsl], sc["e_x"][:, gsl], sc["dec_x"][:, gsl], sc["dk_x"][:, gsl]
            dsk_g = dsk_ref[:, gsl]
            xdt_g = xs_g * dt_g
            xdt_stack = _stack_heads(xdt_g, masks)
            cbm = _dot(cg, bg, _NT)
            cbt = _dot(bg, cg, _NT)
            heads = range(g * HEADS_PER_GROUP, (g + 1) * HEADS_PER_GROUP)
            lmats = [jnp.exp(jnp.where(tril, sc["csb"][h] - sc["cs_t"][h:h + 1, :], -1e30)) for h in heads]
            mw = jnp.concatenate([cbm * lm for lm in lmats], axis=1)
            mtw = jnp.concatenate(
                [cbt * jnp.exp(jnp.where(triu, sc["cs_t"][h:h + 1, :] - sc["csb"][h], -1e30)) for h in heads], axis=1)
            ht_g = hp_ref[0, :, gsl]
            dhn_g = dh_ref[:, gsl]
            yoff = e_g * _dot(cg, ht_g)
            y_g = _dot(mw, xdt_stack) + yoff + dsk_g * xs_g
            zg = z_ref[:, gsl]
            sz = _sigmoid(zg)
            silu = zg * sz
            yg = y_g * silu
            rs = lax.rsqrt(jnp.mean(yg * yg, axis=1, keepdims=True) + NORM_EPS)
            yn = yg * rs
            dyb = dyb_ref[:, gsl]
            dgs_ref[:, gsl] += jnp.sum(dyb * yn, axis=0, keepdims=True)
            dyn = dyb * gs_ref[:, gsl]
            dyg = rs * (dyn - yn * jnp.mean(dyn * yn, axis=1, keepdims=True))
            dy_g = dyg * silu
            dz_ref[:, gsl] = (dyg * y_g * (sz * (1.0 + zg * (1.0 - sz)))).astype(bf16)
            dy_stack = _stack_heads(dy_g, masks)
            dm_w = _dot(dy_g, xdt_stack, _NT)
            dmt_w = _dot(xdt_g, dy_stack, _NT)
            dxdt = _dot(mtw, dy_stack)
            dcb_acc = jnp.zeros((CHUNK, CHUNK), f32)
            for r, h in enumerate(heads):
                hs = slice(r * CHUNK, (r + 1) * CHUNK)
                dml = dm_w[:, hs] * lmats[r]
                dcb_acc = dcb_acc + dml
                col = jnp.sum(dml * cbm, axis=0, keepdims=True)
                row = jnp.sum(dmt_w[:, hs] * mtw[:, hs], axis=0, keepdims=True)
                dcs_t = dcs_t + jnp.where(rowh == h, row - col, 0.0)
            w = _dot(bg, dhn_g)
            dxdt = dxdt + dec_g * w
            decx3 = dec_g * (xdt_g * w)
            dg_g = e_g * dy_g
            d_c = _dot(dg_g, ht_g, _NT) + _dot(dcb_acc, bg)
            d_b = _dot(dcb_acc, cg, _TN) + _dot(xdt_g * dec_g, dhn_g, _NT)
            dh_ref[:, gsl] = dhn_g * dk_g + _dot(cg, dg_g, _TN)
            dxc_ref[:, gsl] = dsk_g * dy_g + dxdt * dt_g
            dxc_ref[:, _B0 + g * D_STATE:_B0 + (g + 1) * D_STATE] = d_b
            dxc_ref[:, _C0 + g * D_STATE:_C0 + (g + 1) * D_STATE] = d_c
            x13_ref[:, gsl] = dy_g * yoff - decx3
            x2_ref[:, gsl] = dxdt * xs_g
            rows_ref[0:1, gsl] = jnp.sum(dhn_g * ht_g, axis=0, keepdims=True)
            rows_ref[1:2, gsl] = jnp.sum(decx3, axis=0, keepdims=True)
            rows_ref[2:3, gsl] = jnp.sum(dy_g * xs_g, axis=0, keepdims=True)
        segm = seg_ref[...]
        r13 = _seg_sum(x13_ref[...], segm)
        r2 = _seg_sum(x2_ref[...], segm)
        small = _seg_sum(rows_ref[...], segm)
        lane = _iota((CHUNK, LANES), 1)
        rowi = _iota((CHUNK, LANES), 0)
        dcl_row = small[0:1, :] * jnp.exp(sc["cs"][CHUNK - 1:CHUNK, :]) + small[1:2, :]
        dcs = r13 + jnp.where(rowi == CHUNK - 1, dcl_row, 0.0)
        dcs_t_all = dcs.T + jnp.concatenate([dcs_t, jnp.zeros((LANES - N_HEADS, CHUNK), f32)], axis=0)
        dda = _dot32(dcs_t_all, tril.astype(f32)).T
        a = sc["a"]
        ddt_total = r2 + dda * a
        dalog_ref[...] += jnp.sum(dda * sc["dtv"], axis=0, keepdims=True) * a
        ddtr = jnp.where(lane < N_HEADS, ddt_total * _sigmoid(sc["xdt_pre"]), 0.0)
        ddtb_ref[...] += jnp.sum(ddtr, axis=0, keepdims=True)
        ddt_ref[...] = ddtr.astype(bf16)
        ddsk_ref[...] += small[2:3, :]
        row8 = _iota((_TAIL, _CONV_COLS), 0)
        for j in range(CONV_DIM // _CONV_COLS):
            sl = slice(j * _CONV_COLS, (j + 1) * _CONV_COLS)
            cvv = cv_ref[:, sl]
            sg = _sigmoid(cvv)
            dconv = dxc_ref[:, sl] * (sg * (1.0 + cvv * (1.0 - sg)))
            nxt = dcnext_ref[:, sl]
            cur = cur_ref[:, sl]
            dxin = dconv * cw_ref[CONV_W - 1:CONV_W, sl]
            dcw_ref[CONV_W - 1:CONV_W, sl] += jnp.sum(dconv * cur, axis=0, keepdims=True)
            for s in range(1, CONV_W):
                rolled = pltpu.roll(dconv, CHUNK - s, 0)
                bot = jnp.where(row8 < _TAIL - s, rolled[CHUNK - _TAIL:], pltpu.roll(nxt, _TAIL - s, 0))
                up = jnp.concatenate([rolled[:CHUNK - _TAIL], bot], axis=0)
                dxin = dxin + up * cw_ref[CONV_W - 1 - s:CONV_W - s, sl]
                dcw_ref[CONV_W - 1 - s:CONV_W - s, sl] += jnp.sum(up * cur, axis=0, keepdims=True)
            dcb_ref[:, sl] += jnp.sum(dconv, axis=0, keepdims=True)
            dxbc_ref[:, sl] = dxin.astype(bf16)
            dcnext_ref[:, sl] = dconv[:_TAIL]
        if ride is not None:
            pl.when(i == nc - 1)(finish)

    def chunk(w):
        return pl.BlockSpec((CHUNK, w), lambda i: (nc - 1 - i, 0))

    def const(shape):
        return pl.BlockSpec(shape, lambda i: (0,) * len(shape))

    riding = ride is not None
    return pl.pallas_call(
        body, name=name, grid=(nc,),
        in_specs=[chunk(CONV_DIM), chunk(CONV_DIM),
                  chunk(D_INNER), chunk(LANES), pl.BlockSpec((1, D_STATE, D_INNER), lambda i: (nc - 1 - i, 0, 0)),
                  chunk(D_INNER), const((CONV_W, CONV_DIM)),
                  const((1, LANES)), const((1, LANES)), const((1, D_INNER)), const((1, D_INNER)),
                  const((D_INNER, LANES))] + [_ANY] * riding,
        out_specs=[chunk(D_INNER), chunk(CONV_DIM), chunk(LANES), const((CONV_W, CONV_DIM)), const((1, CONV_DIM)),
                   const((1, LANES)), const((1, LANES)), const((1, LANES)), const((1, D_INNER))] + [_ANY] * riding,
        out_shape=[jax.ShapeDtypeStruct((t, D_INNER), bf16), jax.ShapeDtypeStruct((t, CONV_DIM), bf16),
                   jax.ShapeDtypeStruct((t, LANES), bf16), jax.ShapeDtypeStruct((CONV_W, CONV_DIM), f32),
                   jax.ShapeDtypeStruct((1, CONV_DIM), f32), jax.ShapeDtypeStruct((1, LANES), f32),
                   jax.ShapeDtypeStruct((1, LANES), f32), jax.ShapeDtypeStruct((1, LANES), f32),
                   jax.ShapeDtypeStruct((1, D_INNER), f32)]
        + ([jax.ShapeDtypeStruct((N_CHIPS - 1,) + ride.shape[1:], ride.dtype)] if riding else []),
        scratch_shapes=[pltpu.VMEM((D_STATE, D_INNER), f32), pltpu.VMEM((_TAIL, CONV_DIM), f32),
                        pltpu.VMEM((CHUNK, CONV_DIM), f32), pltpu.VMEM((CHUNK, CONV_DIM), f32),
                        pltpu.VMEM((CHUNK, D_INNER), f32), pltpu.VMEM((CHUNK, D_INNER), f32),
                        pltpu.VMEM((_TAIL, D_INNER), f32)]
        + (list(_SCATTER_SCRATCH) if riding else []),
        compiler_params=_params(("arbitrary",)),
    )(xbc, cv, z, dtr, hprev, dyb, cw, dtb, alog, dsk_x, gs, seg, *([ride] if riding else []))


def _adamw(w, g, m, v, *, name):
    r, c = w.shape
    tr = r
    while tr * c * 4 > 2 * _MB and tr % 16 == 0:
        tr //= 2

    def body(w_ref, g_ref, m_ref, v_ref, d_ref, m2_ref, v2_ref):
        gv = g_ref[...]
        m2 = ADAM_B1 * m_ref[...] + (1.0 - ADAM_B1) * gv
        v2 = ADAM_B2 * v_ref[...] + (1.0 - ADAM_B2) * (gv * gv)
        m_hat = m2 / (1.0 - ADAM_B1 ** ADAM_STEP)
        v_hat = v2 / (1.0 - ADAM_B2 ** ADAM_STEP)
        d_ref[...] = -ADAM_LR * (m_hat / (jnp.sqrt(v_hat) + ADAM_EPS) + ADAM_WD * w_ref[...])
        m2_ref[...] = m2
        v2_ref[...] = v2

    blk = pl.BlockSpec((tr, c), lambda i: (i, 0))
    return pl.pallas_call(
        body, name=name, grid=(r // tr,),
        in_specs=[blk] * 4, out_specs=[blk] * 3,
        out_shape=[jax.ShapeDtypeStruct((r, c), f32)] * 3,
        compiler_params=_params(("parallel",)),
    )(w, g, m, v)


def _row_block(rows, cols):
    cap = max(16, 2 * _MB // (4 * cols))
    return max(tr for tr in range(16, min(cap, rows) + 1, 16) if rows % tr == 0)


def _cast_bf16(a, *, name):
    r, c = a.shape
    tr = _row_block(r, c)

    def body(a_ref, o_ref):
        o_ref[...] = a_ref[...].astype(bf16)

    blk = pl.BlockSpec((tr, c), lambda i: (i, 0))
    return pl.pallas_call(
        body, name=name, grid=(r // tr,), in_specs=[blk], out_specs=blk,
        out_shape=jax.ShapeDtypeStruct((r, c), bf16), compiler_params=_params(("parallel",)),
    )(a)


_ANY = pl.BlockSpec(memory_space=pl.ANY)


def _place():
    x, y, c = lax.axis_index("x"), lax.axis_index("y"), lax.axis_index("c")
    other_chips = [(1 - x, y), (x, 1 - y), (1 - x, 1 - y)]
    return x, y, c, other_chips


def _gather_protocol(in_ref, out_ref, send_sems, recv_sems):
    x, y, c, chips = _place()
    me = 2 * x + y
    sibling = (x, y, 1 - c)
    where = [2 * cx + cy for cx, cy in chips]

    def cp(k, chip, half, to, src=None):
        dst = out_ref.at[chip, half]
        return pltpu.make_async_remote_copy(
            src_ref=dst if src is None else src, dst_ref=dst, send_sem=send_sems.at[k], recv_sem=recv_sems.at[k],
            device_id=to, device_id_type=MESH)

    def sends():
        return [cp(j, me, c, (*chips[j], c), src=in_ref.at[c]) for j in range(2)]

    def relays():
        return [cp(3 + j, where[j], c, sibling) for j in range(3)]

    def landed(j):
        return cp(j, where[j], c, sibling)

    def start():
        for f in sends():
            f.start()

    def relay():
        onward = relays()
        for first in range(2):
            @pl.when(c == first)
            def _(first=first):
                landed(first).wait_recv()
                cp(2, where[first], c, (*chips[1 - first], c)).start()
                onward[first].start()
                landed(1 - first).wait_recv()
                onward[1 - first].start()

    def finish():
        landed(2).wait_recv()
        relays()[2].start()
        for j in range(3):
            cp(3 + j, where[j], 1 - c, sibling).wait_recv()
        for f in sends() + [landed(2)] + relays():
            f.wait_send()

    return start, relay, finish


_GATHER_SCRATCH = [pltpu.SemaphoreType.DMA((6,)), pltpu.SemaphoreType.DMA((6,))]


def _gather_shards(shard, *, name):
    _, rh, lanes = shard.shape

    def body(in_ref, out_ref, send_sems, recv_sems):
        start, relay, finish = _gather_protocol(in_ref, out_ref, send_sems, recv_sems)
        start()
        relay()
        finish()

    return pl.pallas_call(
        body, name=name, in_specs=[_ANY], out_specs=_ANY,
        out_shape=jax.ShapeDtypeStruct((N_CHIPS, 2, rh, lanes), shard.dtype),
        scratch_shapes=list(_GATHER_SCRATCH),
    )(shard)


def _scatter_protocol(p_ref, out_ref, send_sems, recv_sems):
    x, y, c, chips = _place()

    def copies():
        return [pltpu.make_async_remote_copy(
            src_ref=p_ref.at[2 * cx + cy], dst_ref=out_ref.at[j], send_sem=send_sems.at[j], recv_sem=recv_sems.at[j],
            device_id=(cx, cy, c), device_id_type=MESH) for j, (cx, cy) in enumerate(chips)]

    def start():
        for cpy in copies():
            cpy.start()

    def finish():
        for cpy in copies():
            cpy.wait()

    return start, finish


_SCATTER_SCRATCH = [pltpu.SemaphoreType.DMA((3,)), pltpu.SemaphoreType.DMA((3,))]


def _swap_protocol(g_ref, out_ref, send_sems, recv_sems):
    x, y, c, _ = _place()

    def copies():
        return [pltpu.make_async_remote_copy(
            src_ref=g_ref.at[k, 1 - c], dst_ref=out_ref.at[k], send_sem=send_sems.at[k], recv_sem=recv_sems.at[k],
            device_id=(x, y, 1 - c), device_id_type=MESH) for k in range(N_CHIPS)]

    def start():
        for cpy in copies():
            cpy.start()

    def finish():
        for cpy in copies():
            cpy.wait()

    return start, finish


_SWAP_SCRATCH = [pltpu.SemaphoreType.DMA((N_CHIPS,)), pltpu.SemaphoreType.DMA((N_CHIPS,))]


def _rs_swap_halves(g, *, name):
    nch, _, rh, lanes = g.shape

    def body(g_ref, out_ref, send_sems, recv_sems):
        start, finish = _swap_protocol(g_ref, out_ref, send_sems, recv_sems)
        start()
        finish()

    return pl.pallas_call(
        body, name=name, in_specs=[_ANY], out_specs=_ANY,
        out_shape=jax.ShapeDtypeStruct((nch, rh, lanes), g.dtype),
        scratch_shapes=list(_SWAP_SCRATCH),
    )(g)


def _rs_add_pair(g, got, c_idx, *, name):
    nch, _, rh, lanes = g.shape
    tr = _row_block(rh, lanes)

    def body(c_ref, g_ref, got_ref, p16_ref):
        p16_ref[...] = (g_ref[...] + got_ref[...]).astype(bf16)

    blk = pl.BlockSpec((None, tr, lanes), lambda k, i, c_ref: (k, i, 0))
    return pl.pallas_call(
        body, name=name,
        grid_spec=pltpu.PrefetchScalarGridSpec(
            num_scalar_prefetch=1, grid=(nch, rh // tr),
            in_specs=[pl.BlockSpec((None, None, tr, lanes), lambda k, i, c_ref: (k, c_ref[0], i, 0)), blk],
            out_specs=blk),
        out_shape=jax.ShapeDtypeStruct((nch, rh, lanes), bf16),
        compiler_params=_params(("parallel", "parallel")),
    )(c_idx, g, got)


def _rs_add_chips(g, got_pair, got, place, *, name):
    _, _, rh, lanes = g.shape
    tr = _row_block(rh, lanes)

    def body(place_ref, g_ref, pair_ref, got_ref, o_ref):
        own = g_ref[...] + pair_ref[...]
        o_ref[...] = ((own + got_ref[0].astype(f32)) + got_ref[1].astype(f32)) + got_ref[2].astype(f32)

    return pl.pallas_call(
        body, name=name,
        grid_spec=pltpu.PrefetchScalarGridSpec(
            num_scalar_prefetch=1, grid=(rh // tr,),
            in_specs=[pl.BlockSpec((None, None, tr, lanes), lambda i, place_ref: (place_ref[0], place_ref[1], i, 0)),
                      pl.BlockSpec((None, tr, lanes), lambda i, place_ref: (place_ref[0], i, 0)),
                      pl.BlockSpec((3, tr, lanes), lambda i, place_ref: (0, i, 0))],
            out_specs=pl.BlockSpec((None, tr, lanes), lambda i, place_ref: (place_ref[1], i, 0))),
        out_shape=jax.ShapeDtypeStruct((2, rh, lanes), f32),
        compiler_params=_params(("parallel",)),
    )(place, g, got_pair, got)


def _rs_join_halves(halves, *, name):
    def body(h_ref, out_ref, send_sem, recv_sem):
        x, y, c, _ = _place()
        cpy = pltpu.make_async_remote_copy(
            src_ref=h_ref.at[c], dst_ref=out_ref.at[c], send_sem=send_sem, recv_sem=recv_sem,
            device_id=(x, y, 1 - c), device_id_type=MESH)
        cpy.start()
        cpy.wait()

    return pl.pallas_call(
        body, name=name, in_specs=[_ANY], out_specs=_ANY,
        out_shape=jax.ShapeDtypeStruct(halves.shape, halves.dtype), input_output_aliases={0: 0},
        scratch_shapes=[pltpu.SemaphoreType.DMA, pltpu.SemaphoreType.DMA],
    )(halves)


def _all_reduce_small(s, *, name):
    rs, lanes = s.shape
    rh = rs // 2

    def body(s_ref, o_ref, sib_ref, mine_ref, chips_ref, send_sems, recv_sems):
        x, y, c, chips = _place()
        me = 2 * x + y
        sibling = (x, y, 1 - c)
        rows = pl.ds(pl.multiple_of(c * rh, 8), rh)

        def cp(k, src, dst, to):
            return pltpu.make_async_remote_copy(src_ref=src, dst_ref=dst, send_sem=send_sems.at[k],
                                                recv_sem=recv_sems.at[k], device_id=to, device_id_type=MESH)

        swap = cp(0, s_ref, sib_ref, sibling)
        swap.start()
        swap.wait()
        mine_ref[...] = s_ref[rows, :] + sib_ref[rows, :]
        sends = [cp(1 + j, mine_ref, chips_ref.at[j], (cx, cy, c)) for j, (cx, cy) in enumerate(chips)]
        for cpy in sends:
            cpy.start()
        for cpy in sends:
            cpy.wait()
        where = [2 * cx + cy for cx, cy in chips]
        total = None
        for q in range(N_CHIPS):
            term = jnp.where(q == me, mine_ref[...], jnp.where(
                q == where[0], chips_ref[0], jnp.where(q == where[1], chips_ref[1], chips_ref[2])))
            total = term if total is None else total + term
        o_ref[rows, :] = total
        push = cp(4, o_ref.at[rows, :], o_ref.at[rows, :], sibling)
        push.start()
        push.wait()

    vm = pl.BlockSpec(memory_space=pltpu.VMEM)
    return pl.pallas_call(
        body, name=name, in_specs=[vm], out_specs=vm,
        out_shape=jax.ShapeDtypeStruct((rs, lanes), f32),
        scratch_shapes=[pltpu.VMEM((rs, lanes), f32), pltpu.VMEM((rh, lanes), f32),
                        pltpu.VMEM((N_CHIPS - 1, rh, lanes), f32), pltpu.SemaphoreType.DMA((5,)),
                        pltpu.SemaphoreType.DMA((5,))],
        compiler_params=pltpu.CompilerParams(vmem_limit_bytes=32 * _MB),
    )(s)


def _pad_lanes(a, width=LANES):
    return jnp.pad(a, ((0, 0), (0, width - a.shape[1])))


def _local_grads(x, tgt, wts, small, *, fwd_ride=None, late_weights=None, swap_ride=None, bwd_ride=None,
                 last_ride=None):
    t = x.shape[0]
    tm = min(t, 1024)
    d = D_MODEL
    mm = functools.partial(_matmul, tm=tm)

    dtb = _pad_lanes(small["dt_bias"])
    alog = _pad_lanes(small["a_log"])
    dsk = jnp.repeat(small["d_skip"], HEAD_DIM, axis=1)
    bsp_t = _pad_lanes(small["b_spatial"].T)
    wsp = small["w_spatial"]

    h = _rms_fwd(x, small["norm_mix_g"], name="rms_mix")
    uv = mm(h, wts["uv"], tn=2048, tk=d, out_dtypes=[f32], name="proj_uv")
    z = mm(h, wts["z"], tn=2048, tk=d, out_dtypes=[f32], name="proj_z")
    xbc = mm(h, wts["xbc"], tn=2048, tk=d, out_dtypes=[f32], name="proj_xbc")
    dtr = mm(h, wts["dt"], tn=LANES, tk=d, out_dtypes=[f32], name="proj_dt")
    gl = mm(h, wts["gate"], tn=2048, tk=d, out_dtypes=[f32], name="proj_gate")
    ya = _gmlp_fwd(uv, small["v_norm_g"], small["v_norm_b"], wsp, bsp_t, name="gmlp_fwd")
    yb, hprev, cv, *gathered = _ssd_fwd(xbc, z, dtr, small["conv_w"], small["conv_b"], dtb, alog, dsk,
                                        small["ssm_norm_g"], ride=fwd_ride, name="ssd_fwd")
    if fwd_ride is not None:
        wts = {**wts, **late_weights(gathered[0])}
    tall = functools.partial(_matmul, tm=min(t, 2048))
    pa = tall(ya, wts["pa"], tn=1024, tk=1024, out_dtypes=[f32], name="proj_a")
    tm_gate = min(t, 512)
    row_vec = [pl.BlockSpec((1, d), lambda i, j, k, half=half: (0, half)) for half in range(2)]
    gate_tiles = [pl.BlockSpec((tm_gate, d), lambda i, j, k, half=half: (i, half)) for half in range(2)]

    def merge(pb_acc, pa_t, gla, glb, bga, bgb):
        return pb_acc, _sigmoid(gla + bga) * pa_t + _sigmoid(glb + bgb) * pb_acc

    pb, merged = _matmul(yb, wts["pb"], tm=tm_gate, tn=d, tk=1024, out_dtypes=[f32, bf16], epilogue=merge,
                         extras=[pa, gl, gl, small["b_gates"], small["b_gates"]],
                         extra_specs=[None] + gate_tiles + row_vec, name="proj_b")

    def residual_norm(acc, res, g):
        x_new = res + acc
        r = lax.rsqrt(jnp.mean(x_new * x_new, axis=1, keepdims=True) + NORM_EPS)
        return x_new, x_new * r * g

    x1, h2 = mm(merged, wts["out"], tn=d, tk=1024, out_dtypes=[f32, bf16], epilogue=residual_norm,
                extras=[x, small["norm_mlp_g"]], extra_specs=[None, row_vec[0]], name="out_proj")
    act = tall(h2, wts["up"], tn=2048, tk=d, out_dtypes=[bf16],
             epilogue=lambda acc: (jnp.square(jnp.maximum(acc, 0.0)),), name="mlp_up")
    x2 = mm(act, wts["down"], tn=1024, tk=2048, out_dtypes=[f32], extras=[x1],
            epilogue=lambda acc, res: (res + acc,), name="mlp_down")

    dx2, dx2b, dgf, loss = _loss_head(x2, tgt, small["norm_final_g"], name="loss_head")
    tt = min(t, 2048)
    tn_mm = functools.partial(_matmul_tn, tt=tt)
    dw = {}
    def slab(key, chip_of):
        rows = _LATE_ROWS[key]
        if N_CHIPS * rows == 1024:
            return dict(tka=1024, tn=1024, place=lambda i, j: (ALL_CHIPS, _LATE_OFF[key] // rows))
        return dict(tka=min(rows, 1024), tn=1024, place=lambda i, j: (chip_of(i, j), _LATE_OFF[key] // min(rows, 1024)))

    dw["late"] = tn_mm(act, dx2b, name="dw_down", **slab("w_mlp_down", lambda i, j: i))
    dup = mm(dx2b, wts["down"], nt=True, tn=2048, tk=1024, out_dtypes=[bf16], extras=[act],
             epilogue=lambda acc, a2: (acc * (2.0 * jnp.sqrt(a2).astype(f32)),), name="d_act")
    dw["late"] = tn_mm(h2, dup, name="dw_up", packed=dw["late"], **slab("w_mlp_up", lambda i, j: j))
    dh2 = mm(dup, wts["up"], nt=True, tn=1024, tk=2048, out_dtypes=[f32], name="d_h2")
    dx1, dx1b, dg_mlp = _rms_bwd(x1, small["norm_mlp_g"], dh2, dx2, want_bf16=True, name="rms_mlp_bwd")
    dw["late"] = tn_mm(merged, dx1b, name="dw_out", packed=dw["late"], **slab("w_out", lambda i, j: i))
    dmerged = tall(dx1b, wts["out"], nt=True, tn=1024, tk=1024, out_dtypes=[f32], name="d_merged")
    dpa, dpb, dgl, dbg = _merge_bwd(dmerged, pa, pb, gl, small["b_gates"], name="merge_bwd")
    dw["late"] = tn_mm(ya, dpa, name="dw_pa", packed=dw["late"], **slab("w_proj_a", lambda i, j: i))
    dw["late"] = tn_mm(yb, dpb, name="dw_pb", packed=dw["late"], **slab("w_proj_b", lambda i, j: i))
    dya = tall(dpa, wts["pa"], nt=True, tn=1024, tk=1024, out_dtypes=[f32], name="d_ya")
    dyb = mm(dpb, wts["pb"], nt=True, tn=2048, tk=1024, out_dtypes=[f32], name="d_yb")
    swapped = swap_ride(dw) if swap_ride is not None else None
    duv, dwsp, dbsp_t, dvg, dvb, *got_pair = _gmlp_bwd(uv, dya, small["v_norm_g"], small["v_norm_b"], wsp, bsp_t,
                                                       ride=swapped, name="gmlp_bwd")
    ride = bwd_ride(swapped, got_pair[0]) if bwd_ride is not None else None
    dz, dxbc, ddt, dcw, dcb, ddtb, dalog, ddsk, dgs, *got = _ssd_bwd(
        xbc, cv, z, dtr, hprev, dyb, small["conv_w"], dtb, alog, dsk, small["ssm_norm_g"],
        _head_seg_matrix(), ride=ride, name="ssd_bwd")
    dw["uv"] = tn_mm(h, duv, tka=1024, tn=1024, name="dw_uv")
    dw["z"] = tn_mm(h, dz, tka=1024, tn=1024, name="dw_z")
    dw["xbc"] = tn_mm(h, dxbc, tka=1024, tn=1024, name="dw_xbc")
    dw["dt"] = tn_mm(h, ddt, tka=1024, tn=LANES, name="dw_dt")
    dw["gate"] = tn_mm(h, dgl, tka=1024, tn=1024, name="dw_gate")
    last = last_ride(dw) if last_ride is not None else None
    res = _matmul_nt_sum(
        [(duv, wts["uv"]), (dz, wts["z"]), (dxbc, wts["xbc"]), (dgl, wts["gate"]), (ddt, wts["dt"])],
        tm=tm, tks=[1024] * 4 + [LANES], ride=last, name="d_h")
    dh, got_last = (res[0], res[1]) if last is not None else (res, None)
    dx, dg_mix = _rms_bwd(x, small["norm_mix_g"], dh, dx1, want_bf16=False, name="rms_mix_bwd")

    dsmall = {
        "norm_mix_g": dg_mix, "conv_w": dcw, "conv_b": dcb, "dt_bias": ddtb[:, :N_HEADS], "a_log": dalog[:, :N_HEADS],
        "d_skip": ddsk[:, :N_HEADS], "ssm_norm_g": dgs, "v_norm_g": dvg, "v_norm_b": dvb, "w_spatial": dwsp,
        "b_spatial": dbsp_t[:, :GMLP_GROUPS].T, "b_gates": dbg, "norm_mlp_g": dg_mlp, "norm_final_g": dgf,
    }
    return loss, dx, dw, dsmall, (got[0] if got else None), got_last


_IN_SHARD = IN_PROJ // N_CHIPS
_LATE = ("w_mlp_down", "w_mlp_up", "w_proj_b", "w_proj_a", "w_out")
_LATE_ROWS = {"w_proj_a": GMLP_WIDTH // N_CHIPS, "w_proj_b": D_INNER // N_CHIPS, "w_out": D_MODEL // N_CHIPS,
              "w_mlp_up": D_MODEL, "w_mlp_down": D_FF // N_CHIPS}
_LATE_TOTAL = sum(_LATE_ROWS.values())


def _late_offsets():
    off, out = 0, {}
    for k in _LATE:
        out[k] = off
        off += _LATE_ROWS[k]
    return out


_LATE_OFF = _late_offsets()

_SMALL = ("norm_mix_g", "conv_w", "conv_b", "dt_bias", "a_log", "d_skip", "ssm_norm_g", "v_norm_g", "v_norm_b",
          "w_spatial", "b_spatial", "b_gates", "norm_mlp_g", "norm_final_g")


def _pack_small(parts):
    flat = jnp.concatenate([parts[k].reshape(-1) for k in _SMALL])
    rows = -(-flat.shape[0] // (16 * LANES)) * 16
    return jnp.pad(flat, (0, rows * LANES - flat.shape[0])).reshape(rows, LANES)


def _unpack_small(packed, shapes):
    flat = packed.reshape(-1)
    out, off = {}, 0
    for k in _SMALL:
        n = math.prod(shapes[k])
        out[k] = flat[off:off + n].reshape(shapes[k])
        off += n
    return out


def _from_chip_columns(stacked):
    _, rows, cols = stacked.shape
    return stacked.transpose(1, 0, 2).reshape(rows, N_CHIPS * cols)


def _w_in_grad_by_chip(dw):
    pieces = [dw["uv"], dw["z"], dw["xbc"], dw["dt"][:, :N_HEADS], dw["gate"]]
    bounds = [0]
    for p in pieces:
        bounds.append(bounds[-1] + p.shape[1])
    chips = []
    for k in range(N_CHIPS):
        lo, hi = k * _IN_SHARD, (k + 1) * _IN_SHARD
        parts = [p[:, max(lo, b0) - b0:min(hi, b1) - b0]
                 for p, b0, b1 in zip(pieces, bounds[:-1], bounds[1:]) if min(hi, b1) > max(lo, b0)]
        chips.append(jnp.concatenate(parts, axis=1))
    return jnp.stack(chips)


def kernel(x, norm_mix_g, w_in, conv_w, conv_b, dt_bias, a_log, d_skip, ssm_norm_g, v_norm_g, v_norm_b, w_spatial, b_spatial, b_gates, w_proj_a, w_proj_b, w_out, norm_mlp_g, w_mlp_up, w_mlp_down, norm_final_g, loss_target, m_norm_mix_g, m_w_in, m_conv_w, m_conv_b, m_dt_bias, m_a_log, m_d_skip, m_ssm_norm_g, m_v_norm_g, m_v_norm_b, m_w_spatial, m_b_spatial, m_b_gates, m_w_proj_a, m_w_proj_b, m_w_out, m_norm_mlp_g, m_w_mlp_up, m_w_mlp_down, m_norm_final_g, v_norm_mix_g, v_w_in, v_conv_w, v_conv_b, v_dt_bias, v_a_log, v_d_skip, v_ssm_norm_g, v_v_norm_g, v_v_norm_b, v_w_spatial, v_b_spatial, v_b_gates, v_w_proj_a, v_w_proj_b, v_w_out, v_norm_mlp_g, v_w_mlp_up, v_w_mlp_down, v_norm_final_g):
    given = dict(locals())
    names = ("norm_mix_g", "w_in", "conv_w", "conv_b", "dt_bias", "a_log", "d_skip", "ssm_norm_g", "v_norm_g",
             "v_norm_b", "w_spatial", "b_spatial", "b_gates", "w_proj_a", "w_proj_b", "w_out", "norm_mlp_g",
             "w_mlp_up", "w_mlp_down", "norm_final_g")
    xi, yi, ci = lax.axis_index("x"), lax.axis_index("y"), lax.axis_index("c")
    me_chip = (2 * xi + yi).astype(jnp.int32)

    def halves(a):
        return a.reshape(2, a.shape[0] // 2, a.shape[1])

    def with_own(got, shard):
        whole = lax.dynamic_update_slice(got, shard[None], (me_chip, 0, 0, 0))
        return whole.reshape(N_CHIPS, 2 * shard.shape[1], shard.shape[2])

    shard_in = halves(_cast_bf16(w_in[0], name="cast_w_in"))
    shard_late = halves(_cast_bf16(jnp.concatenate([given[k][0] for k in _LATE]), name="cast_w_late"))
    shard_conv = halves(conv_w.reshape(2 * _TAIL, -1))
    w_in_full = _from_chip_columns(with_own(_gather_shards(shard_in, name="gather_w_in"), shard_in))
    o_dt, o_gate = 2 * GMLP_WIDTH + D_INNER + CONV_DIM, 2 * GMLP_WIDTH + D_INNER + CONV_DIM + N_HEADS
    wts = {
        "uv": w_in_full[:, :2 * GMLP_WIDTH], "z": w_in_full[:, 2 * GMLP_WIDTH:2 * GMLP_WIDTH + D_INNER],
        "xbc": w_in_full[:, 2 * GMLP_WIDTH + D_INNER:o_dt], "dt": _pad_lanes(w_in_full[:, o_dt:o_gate]),
        "gate": w_in_full[:, o_gate:],
    }
    conv_all = with_own(_gather_shards(shard_conv, name="gather_conv_w"), shard_conv)
    conv_full = _from_chip_columns(conv_all.reshape(N_CHIPS, CONV_W, CONV_DIM // N_CHIPS))

    def late_weights(got):
        g_late = with_own(got, shard_late)

        def rows_of(k):
            return g_late[:, _LATE_OFF[k]:_LATE_OFF[k] + _LATE_ROWS[k]]

        return {
            "pa": rows_of("w_proj_a").reshape(GMLP_WIDTH, D_MODEL),
            "pb": rows_of("w_proj_b").reshape(D_INNER, D_MODEL), "out": rows_of("w_out").reshape(D_MODEL, D_MODEL),
            "up": _from_chip_columns(rows_of("w_mlp_up")), "down": rows_of("w_mlp_down").reshape(D_FF, D_MODEL),
        }

    small = {
        "norm_mix_g": norm_mix_g, "conv_w": conv_full, "conv_b": conv_b, "dt_bias": dt_bias, "a_log": a_log,
        "d_skip": d_skip, "ssm_norm_g": ssm_norm_g, "v_norm_g": v_norm_g, "v_norm_b": v_norm_b,
        "w_spatial": w_spatial[0], "b_spatial": b_spatial[0], "b_gates": b_gates, "norm_mlp_g": norm_mlp_g,
        "norm_final_g": norm_final_g.reshape(1, D_MODEL),
    }

    c_idx = ci.astype(jnp.int32).reshape(1)
    place = jnp.stack([me_chip, ci.astype(jnp.int32)])
    partials = {}

    def reduced_shard(tag, got_chips):
        own = _rs_add_chips(*partials[tag], got_chips, place, name="rs_add_chips_" + tag)
        both = _rs_join_halves(own, name="rs_join_" + tag)
        return both.reshape(2 * both.shape[1], both.shape[2])

    def late_grads(dw):
        return dw["late"].reshape(N_CHIPS, 2, _LATE_TOTAL // 2, D_MODEL)

    def late_partials(g, got_pair):
        partials["late"] = (g, got_pair)
        return _rs_add_pair(g, got_pair, c_idx, name="rs_add_pair_late")

    def in_partials(dw):
        g = _w_in_grad_by_chip(dw).reshape(N_CHIPS, 2, D_MODEL // 2, _IN_SHARD)
        got_pair = _rs_swap_halves(g, name="rs_swap_in")
        partials["in"] = (g, got_pair)
        return _rs_add_pair(g, got_pair, c_idx, name="rs_add_pair_in")

    loss_part, grad_x, dw, dsmall, got_late, got_in = _local_grads(
        x[0], loss_target[0], wts, small, fwd_ride=shard_late, late_weights=late_weights, swap_ride=late_grads,
        bwd_ride=late_partials, last_ride=in_partials)
    g_late = reduced_shard("late", got_late)
    g_in_shard = reduced_shard("in", got_in)

    small_shapes = {k: dsmall[k].shape for k in _SMALL}
    packed = _pack_small(dsmall)
    assert packed.size > sum(math.prod(s) for s in small_shapes.values())
    reduced = _all_reduce_small(packed.at[-1, -1].set(loss_part[0, 0]), name="all_reduce_small")
    loss = reduced[-1, -1]
    red = _unpack_small(reduced, small_shapes)
    conv_cols = CONV_DIM // N_CHIPS
    red["conv_w"] = lax.dynamic_slice_in_dim(red["conv_w"], me_chip * conv_cols, conv_cols, axis=1)

    grads, deltas, new_m, new_v = {}, {}, {}, {}
    for k in ("w_in",) + _LATE:
        g2 = g_in_shard if k == "w_in" else g_late[_LATE_OFF[k]:_LATE_OFF[k] + _LATE_ROWS[k]]
        dlt, m2, v2 = _adamw(given[k][0], g2, given["m_" + k][0], given["v_" + k][0], name="adamw_" + k)
        grads[k], deltas[k], new_m[k], new_v[k] = g2, dlt, m2, v2
    adam_shapes = dict(small_shapes)
    adam_shapes["conv_w"] = (CONV_W, conv_cols)

    def small_pack_of(prefix):
        return _pack_small({k: given[prefix + k].reshape(adam_shapes[k]) for k in _SMALL})

    dlt_s, m_s, v_s = _adamw(small_pack_of(""), _pack_small(red), small_pack_of("m_"), small_pack_of("v_"),
                             name="adamw_small")
    for dst, packed in ((deltas, dlt_s), (new_m, m_s), (new_v, v_s)):
        dst.update(_unpack_small(packed, adam_shapes))
    grads.update(red)

    def shaped(dct):
        return [dct[k].reshape(given[k].shape) for k in names]

    return (loss, grad_x[None], *shaped(grads), *shaped(deltas), *shaped(new_m), *shaped(new_v))
```

```python
import functools
import math

import jax
import jax.numpy as jnp
from jax import lax
from jax.experimental import pallas as pl
from jax.experimental.pallas import tpu as pltpu

f32 = jnp.float32
bf16 = jnp.bfloat16

D_MODEL = 1024
CHUNK = 128
GMLP_WIDTH = 1024
GMLP_GROUPS = 8
D_INNER = 2048
HEAD_DIM = 64
N_HEADS = 32
N_GROUPS = 8
HEADS_PER_GROUP = 4
GROUP_W = HEADS_PER_GROUP * HEAD_DIM
D_STATE = 128
CONV_W = 4
CONV_DIM = 4096
D_FF = 4096
IN_PROJ = 10272
NORM_EPS = 1e-6
N_CHIPS = 4
LANES = 128

ADAM_LR = 0.001
ADAM_B1 = 0.9
ADAM_B2 = 0.999
ADAM_EPS = 1e-08
ADAM_WD = 0.01
ADAM_STEP = 10

MESH = pl.DeviceIdType.MESH
_NT = (((1,), (1,)), ((), ()))
_NN = (((1,), (0,)), ((), ()))
_TN = (((0,), (0,)), ((), ()))
_MB = 2 ** 20


def _params(sem, vmem_mb=48):
    return pltpu.CompilerParams(dimension_semantics=sem, vmem_limit_bytes=vmem_mb * _MB)


def _dot(a, b, dims=_NN):
    return lax.dot_general(a.astype(bf16), b.astype(bf16), dims, preferred_element_type=f32)


def _dot32(a, b):
    return jnp.dot(a, b, preferred_element_type=f32, precision=lax.Precision.HIGHEST)


def _sigmoid(x):
    return 1.0 / (1.0 + jnp.exp(-x))


def _sum_all(a):
    return jnp.sum(jnp.sum(a, axis=1, keepdims=True), axis=0, keepdims=True)


def _iota(shape, dim):
    return lax.broadcasted_iota(jnp.int32, shape, dim)


def _matmul(a, b, *, nt=False, tm, tn, tk, out_dtypes, epilogue=None, extras=(), extra_specs=None, name):
    m, k_dim = a.shape
    n = b.shape[0] if nt else b.shape[1]
    nk = k_dim // tk
    ne, no = len(extras), len(out_dtypes)
    dims = _NT if nt else _NN

    def body(*refs):
        a_ref, b_ref = refs[0], refs[1]
        ex = refs[2:2 + ne]
        outs = refs[2 + ne:2 + ne + no]

        def finish(acc):
            vals = epilogue(acc, *[e[...] for e in ex]) if epilogue is not None else (acc,)
            for o, v in zip(outs, vals):
                o[...] = v.astype(o.dtype)

        part = lax.dot_general(a_ref[...], b_ref[...], dims, preferred_element_type=f32)
        if nk == 1:
            finish(part)
        else:
            acc_ref = refs[-1]
            kk = pl.program_id(2)

            @pl.when(kk == 0)
            def _():
                acc_ref[...] = part

            @pl.when(kk > 0)
            def _():
                acc_ref[...] += part

            @pl.when(kk == nk - 1)
            def _():
                finish(acc_ref[...])

    b_spec = pl.BlockSpec((tn, tk), lambda i, j, k: (j, k)) if nt else pl.BlockSpec((tk, tn), lambda i, j, k: (k, j))
    tile = pl.BlockSpec((tm, tn), lambda i, j, k: (i, j))
    ex_specs = [tile if s is None else s for s in (extra_specs or [None] * ne)]
    outs = pl.pallas_call(
        body, name=name, grid=(m // tm, n // tn, nk),
        in_specs=[pl.BlockSpec((tm, tk), lambda i, j, k: (i, k)), b_spec] + ex_specs,
        out_specs=[tile] * no,
        out_shape=[jax.ShapeDtypeStruct((m, n), dt) for dt in out_dtypes],
        scratch_shapes=[pltpu.VMEM((tm, tn), f32)] if nk > 1 else [],
        compiler_params=_params(("parallel", "parallel", "arbitrary")),
    )(a, b, *extras)
    return outs if no > 1 else outs[0]


def _matmul_nt_sum(pairs, *, tm, tks, ride=None, name):
    m = pairs[0][0].shape[0]
    n = pairs[0][1].shape[1]
    nblk = [a.shape[1] // tk for (a, _), tk in zip(pairs, tks)]
    starts = [sum(nblk[:p]) for p in range(len(pairs))]
    nk = sum(nblk)
    npairs = len(pairs)
    ni = m // tm
    riding = ride is not None

    def body(*refs):
        rest = refs[2 * npairs:]
        if riding:
            ride_ref, o_ref, got_ref, acc_ref, send_sems, recv_sems = rest
        else:
            o_ref, acc_ref = rest
        i, kk = pl.program_id(0), pl.program_id(1)
        if riding:
            start, finish = _scatter_protocol(ride_ref, got_ref, send_sems, recv_sems)
            pl.when((i == 0) & (kk == 0))(start)

        @pl.when(kk == 0)
        def _():
            acc_ref[...] = jnp.zeros_like(acc_ref)

        for p in range(npairs):
            @pl.when((kk >= starts[p]) & (kk < starts[p] + nblk[p]))
            def _(p=p):
                acc_ref[...] += lax.dot_general(refs[2 * p][...], refs[2 * p + 1][...], _NN, preferred_element_type=f32)

        @pl.when(kk == nk - 1)
        def _():
            o_ref[...] = acc_ref[...]

        if riding:
            pl.when((i == ni - 1) & (kk == nk - 1))(finish)

    in_specs, args = [], []
    for p, (a, b) in enumerate(pairs):
        def kblock(k, s=starts[p], nb=nblk[p]):
            return jnp.clip(k - s, 0, nb - 1)
        in_specs.append(pl.BlockSpec((tm, tks[p]), lambda i, k, kb=kblock: (i, kb(k))))
        in_specs.append(pl.BlockSpec((tks[p], n), lambda i, k, kb=kblock: (kb(k), 0)))
        args += [a, b]
    tile = pl.BlockSpec((tm, n), lambda i, k: (i, 0))
    outs = pl.pallas_call(
        body, name=name, grid=(ni, nk), in_specs=in_specs + [_ANY] * riding, out_specs=[tile] + [_ANY] * riding,
        out_shape=[jax.ShapeDtypeStruct((m, n), f32)]
        + ([jax.ShapeDtypeStruct((N_CHIPS - 1,) + ride.shape[1:], ride.dtype)] if riding else []),
        scratch_shapes=[pltpu.VMEM((tm, n), f32)] + (list(_SCATTER_SCRATCH) if riding else []),
        compiler_params=_params(("arbitrary", "arbitrary"), vmem_mb=56),
    )(*args, *([ride] if riding else []))
    return outs if riding else outs[0]


ALL_CHIPS = "all"


def _matmul_tn(a, b, *, tka, tn, tt, name, packed=None, place=None):
    t, ka = a.shape
    n = b.shape[1]
    spread = place is not None and place(0, 0)[0] is ALL_CHIPS

    def body(a_ref, b_ref, *rest):
        o_ref = rest[-1]
        part = lax.dot_general(a_ref[...], b_ref[...], _TN, preferred_element_type=f32).reshape(o_ref.shape)
        kk = pl.program_id(2)

        @pl.when(kk == 0)
        def _():
            o_ref[...] = part

        @pl.when(kk > 0)
        def _():
            o_ref[...] += part

    in_specs = [pl.BlockSpec((tt, tka), lambda i, j, k: (k, i)), pl.BlockSpec((tt, tn), lambda i, j, k: (k, j))]
    if place is None:
        out_spec = pl.BlockSpec((tka, tn), lambda i, j, k: (i, j))
        out_shape = jax.ShapeDtypeStruct((ka, n), f32)
    elif spread:
        out_spec = pl.BlockSpec((N_CHIPS, tka // N_CHIPS, tn), lambda i, j, k: (0, place(i, j)[1], 0))
        out_shape = jax.ShapeDtypeStruct((N_CHIPS, _LATE_TOTAL, tn), f32)
    else:
        out_spec = pl.BlockSpec((None, tka, tn), lambda i, j, k: (*place(i, j), 0))
        out_shape = jax.ShapeDtypeStruct((N_CHIPS, _LATE_TOTAL, tn), f32)
    aliased = packed is not None
    return pl.pallas_call(
        body, name=name, grid=(ka // tka, n // tn, t // tt),
        in_specs=in_specs + [_ANY] * aliased, out_specs=out_spec, out_shape=out_shape,
        input_output_aliases={2: 0} if aliased else {},
        compiler_params=_params(("parallel", "parallel", "arbitrary")),
    )(a, b, *([packed] if aliased else []))


def _row_tile(t):
    return min(t, 512)


def _rms_fwd(x, g, *, name):
    t, d = x.shape
    tr = _row_tile(t)

    def body(x_ref, g_ref, h_ref):
        xv = x_ref[...]
        r = lax.rsqrt(jnp.mean(xv * xv, axis=1, keepdims=True) + NORM_EPS)
        h_ref[...] = (xv * r * g_ref[...]).astype(bf16)

    return pl.pallas_call(
        body, name=name, grid=(t // tr,),
        in_specs=[pl.BlockSpec((tr, d), lambda i: (i, 0)), pl.BlockSpec((1, d), lambda i: (0, 0))],
        out_specs=pl.BlockSpec((tr, d), lambda i: (i, 0)),
        out_shape=jax.ShapeDtypeStruct((t, d), bf16),
        compiler_params=_params(("parallel",)),
    )(x, g)


def _rms_bwd(xin, g, dh, dres, *, want_bf16, name):
    t, d = xin.shape
    tr = _row_tile(t)

    def body(x_ref, g_ref, dh_ref, dres_ref, dx_ref, *rest):
        dg_ref = rest[-1]
        xv = x_ref[...]
        r = lax.rsqrt(jnp.mean(xv * xv, axis=1, keepdims=True) + NORM_EPS)
        xn = xv * r
        dhv = dh_ref[...]
        dxn = dhv * g_ref[...]
        dx = dres_ref[...] + r * (dxn - xn * jnp.mean(dxn * xn, axis=1, keepdims=True))
        dx_ref[...] = dx
        if want_bf16:
            rest[0][...] = dx.astype(bf16)
        part = jnp.sum(dhv * xn, axis=0, keepdims=True)

        @pl.when(pl.program_id(0) == 0)
        def _():
            dg_ref[...] = part

        @pl.when(pl.program_id(0) > 0)
        def _():
            dg_ref[...] += part

    row = pl.BlockSpec((tr, d), lambda i: (i, 0))
    vec = pl.BlockSpec((1, d), lambda i: (0, 0))
    out_shape = [jax.ShapeDtypeStruct((t, d), f32)] + ([jax.ShapeDtypeStruct((t, d), bf16)] if want_bf16 else []) \
        + [jax.ShapeDtypeStruct((1, d), f32)]
    return pl.pallas_call(
        body, name=name, grid=(t // tr,),
        in_specs=[row, vec, row, row],
        out_specs=[row] + ([row] if want_bf16 else []) + [vec],
        out_shape=out_shape,
        compiler_params=_params(("arbitrary",)),
    )(xin, g, dh, dres)


def _loss_head(x2, tgt, g, *, name):
    t, d = x2.shape
    tr = _row_tile(t)

    def body(x_ref, t_ref, g_ref, dx_ref, dxb_ref, dg_ref, loss_ref):
        xv = x_ref[...]
        gv = g_ref[...]
        r = lax.rsqrt(jnp.mean(xv * xv, axis=1, keepdims=True) + NORM_EPS)
        xn = xv * r
        e = xn * gv - t_ref[...]
        lpart = jnp.zeros((1, LANES), f32) + 0.5 * _sum_all(jnp.mean(e * e, axis=1, keepdims=True))
        dy = e * (1.0 / d)
        dxn = dy * gv
        dx = r * (dxn - xn * jnp.mean(dxn * xn, axis=1, keepdims=True))
        dx_ref[...] = dx
        dxb_ref[...] = dx.astype(bf16)
        gpart = jnp.sum(dy * xn, axis=0, keepdims=True)

        @pl.when(pl.program_id(0) == 0)
        def _():
            dg_ref[...] = gpart
            loss_ref[...] = lpart

        @pl.when(pl.program_id(0) > 0)
        def _():
            dg_ref[...] += gpart
            loss_ref[...] += lpart

    row = pl.BlockSpec((tr, d), lambda i: (i, 0))
    vec = pl.BlockSpec((1, d), lambda i: (0, 0))
    return pl.pallas_call(
        body, name=name, grid=(t // tr,),
        in_specs=[row, row, vec],
        out_specs=[row, row, vec, pl.BlockSpec((1, LANES), lambda i: (0, 0))],
        out_shape=[jax.ShapeDtypeStruct((t, d), f32), jax.ShapeDtypeStruct((t, d), bf16),
                   jax.ShapeDtypeStruct((1, d), f32), jax.ShapeDtypeStruct((1, LANES), f32)],
        compiler_params=_params(("arbitrary",)),
    )(x2, tgt, g)


def _merge_bwd(dm, pa, pb, gl, bg, *, name):
    t, d = pa.shape
    tr = _row_tile(t)

    def body(dm_ref, pa_ref, pb_ref, gla_ref, glb_ref, bga_ref, bgb_ref, dpa_ref, dpb_ref, dgl_ref, dbg_ref):
        dmv = dm_ref[...]
        ga = _sigmoid(gla_ref[...] + bga_ref[...])
        gb = _sigmoid(glb_ref[...] + bgb_ref[...])
        dpa_ref[...] = (dmv * ga).astype(bf16)
        dpb_ref[...] = (dmv * gb).astype(bf16)
        dla = dmv * pa_ref[...] * ga * (1.0 - ga)
        dlb = dmv * pb_ref[...] * gb * (1.0 - gb)
        dgl_ref[:, :d] = dla.astype(bf16)
        dgl_ref[:, d:] = dlb.astype(bf16)
        sa = jnp.sum(dla, axis=0, keepdims=True)
        sb = jnp.sum(dlb, axis=0, keepdims=True)

        @pl.when(pl.program_id(0) == 0)
        def _():
            dbg_ref[:, :d] = sa
            dbg_ref[:, d:] = sb

        @pl.when(pl.program_id(0) > 0)
        def _():
            dbg_ref[:, :d] += sa
            dbg_ref[:, d:] += sb

    row = pl.BlockSpec((tr, d), lambda i: (i, 0))
    return pl.pallas_call(
        body, name=name, grid=(t // tr,),
        in_specs=[row, row, row, row, pl.BlockSpec((tr, d), lambda i: (i, 1)),
                  pl.BlockSpec((1, d), lambda i: (0, 0)), pl.BlockSpec((1, d), lambda i: (0, 1))],
        out_specs=[row, row, pl.BlockSpec((tr, 2 * d), lambda i: (i, 0)), pl.BlockSpec((1, 2 * d), lambda i: (0, 0))],
        out_shape=[jax.ShapeDtypeStruct((t, d), bf16), jax.ShapeDtypeStruct((t, d), bf16),
                   jax.ShapeDtypeStruct((t, 2 * d), bf16), jax.ShapeDtypeStruct((1, 2 * d), f32)],
        compiler_params=_params(("arbitrary",)),
    )(dm, pa, pb, gl, gl, bg, bg)


_INV_SQRT2 = 1.0 / math.sqrt(2.0)
_INV_SQRT2PI = 1.0 / math.sqrt(2.0 * math.pi)


def _gmlp_common(uv, vg, vb, with_grad=False):
    cdf = 0.5 * (1.0 + lax.erf(uv * _INV_SQRT2))
    zz = uv * cdf
    u, vhat, rstd, vn = _gmlp_norm(zz, vg, vb)
    if not with_grad:
        return u, vhat, rstd, vn
    return u, vhat, rstd, vn, cdf + uv * jnp.exp(-0.5 * uv * uv) * _INV_SQRT2PI


def _gmlp_norm(zz, vg, vb):
    u = zz[:, :GMLP_WIDTH]
    v = zz[:, GMLP_WIDTH:]
    mu = jnp.mean(v, axis=1, keepdims=True)
    vc = v - mu
    rstd = lax.rsqrt(jnp.mean(vc * vc, axis=1, keepdims=True) + NORM_EPS)
    vhat = vc * rstd
    vn = vhat * vg + vb
    return u, vhat, rstd, vn


def _gmlp_fwd(uv, vg, vb, wsp, bsp_t, *, name):
    t = uv.shape[0]
    per_step = 8 if t % (8 * CHUNK) == 0 else 1
    rows = per_step * CHUNK

    def body(uv_ref, vg_ref, vb_ref, w_ref, b_ref, y_ref):
        tril = _iota((CHUNK, CHUNK), 0) >= _iota((CHUNK, CHUNK), 1)
        bt = b_ref[...]
        for q in range(per_step):
            qs = slice(q * CHUNK, (q + 1) * CHUNK)
            u, _, _, vn = _gmlp_common(uv_ref[qs, :], vg_ref[...], vb_ref[...])
            for g in range(GMLP_GROUPS):
                sl = slice(g * CHUNK, (g + 1) * CHUNK)
                w = jnp.where(tril, w_ref[g], 0.0)
                s = _dot(w, vn[:, sl]) + bt[:, g:g + 1]
                y_ref[qs, sl] = (u[:, sl] * s).astype(bf16)

    return pl.pallas_call(
        body, name=name, grid=(t // rows,),
        in_specs=[pl.BlockSpec((rows, 2 * GMLP_WIDTH), lambda c: (c, 0)),
                  pl.BlockSpec((1, GMLP_WIDTH), lambda c: (0, 0)), pl.BlockSpec((1, GMLP_WIDTH), lambda c: (0, 0)),
                  pl.BlockSpec((GMLP_GROUPS, CHUNK, CHUNK), lambda c: (0, 0, 0)),
                  pl.BlockSpec((CHUNK, LANES), lambda c: (0, 0))],
        out_specs=pl.BlockSpec((rows, GMLP_WIDTH), lambda c: (c, 0)),
        out_shape=jax.ShapeDtypeStruct((t, GMLP_WIDTH), bf16),
        compiler_params=_params(("parallel",)),
    )(uv, vg, vb, wsp, bsp_t)


def _gmlp_bwd(uv, dya, vg, vb, wsp, bsp_t, *, ride=None, name):
    t = uv.shape[0]
    per_step = 8 if t % (8 * CHUNK) == 0 else 1
    rows = per_step * CHUNK
    steps = t // rows
    riding = ride is not None

    def body(*refs):
        uv_ref, dy_ref, vg_ref, vb_ref, w_ref, b_ref = refs[:6]
        duv_ref, dw_ref, db_ref, dvg_ref, dvb_ref = refs[6 + riding:11 + riding]
        first = pl.program_id(0) == 0
        if riding:
            start, finish = _swap_protocol(refs[6], refs[12], refs[13], refs[14])
            pl.when(first)(start)

        @pl.when(first)
        def _():
            dw_ref[...] = jnp.zeros_like(dw_ref)
            db_ref[...] = jnp.zeros_like(db_ref)
            dvg_ref[...] = jnp.zeros_like(dvg_ref)
            dvb_ref[...] = jnp.zeros_like(dvb_ref)

        vgv = vg_ref[...]
        tril = _iota((CHUNK, CHUNK), 0) >= _iota((CHUNK, CHUNK), 1)
        lane = _iota((CHUNK, LANES), 1)
        bt = b_ref[...]
        for q in range(per_step):
            qs = slice(q * CHUNK, (q + 1) * CHUNK)
            u, vhat, rstd, vn, gelu_grad = _gmlp_common(uv_ref[qs, :], vgv, vb_ref[...], with_grad=True)
            dy = dy_ref[qs, :]
            ds_all = dy * u
            dbacc = jnp.zeros((CHUNK, LANES), f32)
            dvh_parts = []
            for g in range(GMLP_GROUPS):
                sl = slice(g * CHUNK, (g + 1) * CHUNK)
                w = jnp.where(tril, w_ref[g], 0.0)
                vng = vn[:, sl]
                s = _dot(w, vng) + bt[:, g:g + 1]
                ds = ds_all[:, sl]
                duv_ref[qs, sl] = (dy[:, sl] * s * gelu_grad[:, sl]).astype(bf16)
                dw_ref[g] += jnp.where(tril, _dot(ds, vng, _NT), 0.0)
                dbacc = dbacc + jnp.where(lane == g, jnp.sum(ds, axis=1, keepdims=True), 0.0)
                dvn = _dot(w, ds, _TN)
                vh = vhat[:, sl]
                dvg_ref[:, sl] += jnp.sum(dvn * vh, axis=0, keepdims=True)
                dvb_ref[:, sl] += jnp.sum(dvn, axis=0, keepdims=True)
                dvh_parts.append(dvn * vgv[:, sl])
            db_ref[...] += dbacc
            dvhat = jnp.concatenate(dvh_parts, axis=1)
            m1 = jnp.mean(dvhat, axis=1, keepdims=True)
            m2 = jnp.mean(dvhat * vhat, axis=1, keepdims=True)
            dv = rstd * (dvhat - m1 - vhat * m2)
            duv_ref[qs, GMLP_WIDTH:] = (dv * gelu_grad[:, GMLP_WIDTH:]).astype(bf16)
        if riding:
            pl.when(pl.program_id(0) == steps - 1)(finish)

    vec = pl.BlockSpec((1, GMLP_WIDTH), lambda c: (0, 0))
    return pl.pallas_call(
        body, name=name, grid=(steps,),
        in_specs=[pl.BlockSpec((rows, 2 * GMLP_WIDTH), lambda c: (c, 0)),
                  pl.BlockSpec((rows, GMLP_WIDTH), lambda c: (c, 0)), vec, vec,
                  pl.BlockSpec((GMLP_GROUPS, CHUNK, CHUNK), lambda c: (0, 0, 0)),
                  pl.BlockSpec((CHUNK, LANES), lambda c: (0, 0))] + [_ANY] * riding,
        out_specs=[pl.BlockSpec((rows, 2 * GMLP_WIDTH), lambda c: (c, 0)),
                   pl.BlockSpec((GMLP_GROUPS, CHUNK, CHUNK), lambda c: (0, 0, 0)),
                   pl.BlockSpec((CHUNK, LANES), lambda c: (0, 0)), vec, vec] + [_ANY] * riding,
        out_shape=[jax.ShapeDtypeStruct((t, 2 * GMLP_WIDTH), bf16),
                   jax.ShapeDtypeStruct((GMLP_GROUPS, CHUNK, CHUNK), f32),
                   jax.ShapeDtypeStruct((CHUNK, LANES), f32),
                   jax.ShapeDtypeStruct((1, GMLP_WIDTH), f32), jax.ShapeDtypeStruct((1, GMLP_WIDTH), f32)]
        + ([jax.ShapeDtypeStruct(ride.shape[:1] + ride.shape[2:], ride.dtype)] if riding else []),
        scratch_shapes=list(_SWAP_SCRATCH) if riding else [],
        compiler_params=_params(("arbitrary",)),
    )(uv, dya, vg, vb, wsp, bsp_t, *([ride] if riding else []))


_CONV_COLS = 512
_B0, _C0 = D_INNER, D_INNER + N_GROUPS * D_STATE


_TAIL = 8


def _conv_silu(cur_ref, tail_ref, w_ref, b_ref, has_prev, xc_ref, cv_ref):
    row = _iota((_TAIL, _CONV_COLS), 0)
    for j in range(CONV_DIM // _CONV_COLS):
        sl = slice(j * _CONV_COLS, (j + 1) * _CONV_COLS)
        cur = cur_ref[:, sl]
        tail = jnp.where(has_prev, tail_ref[:, sl], 0.0)
        acc = cur * w_ref[CONV_W - 1:CONV_W, sl] + b_ref[:, sl]
        for s in range(1, CONV_W):
            rolled = pltpu.roll(cur, s, 0)
            top = jnp.where(row >= s, rolled[:_TAIL], pltpu.roll(tail, s, 0))
            sh = jnp.concatenate([top, rolled[_TAIL:]], axis=0)
            acc = acc + sh * w_ref[CONV_W - 1 - s:CONV_W - s, sl]
        cv_ref[:, sl] = acc
        xc_ref[:, sl] = acc * _sigmoid(acc)


def _col_bcast(mat, h):
    return jnp.broadcast_to(mat[:, h:h + 1], (CHUNK, LANES))


def _head_expand(cols):
    lo = _iota((CHUNK, LANES), 1) < HEAD_DIM
    return jnp.concatenate([jnp.where(lo, cols[2 * j], cols[2 * j + 1]) for j in range(N_HEADS // 2)], axis=1)


def _ssd_chunk_scalars(dtr, dtb, alog):
    xdt_pre = dtr + dtb
    dtv = jnp.maximum(xdt_pre, 0.0) + jnp.log(1.0 + jnp.exp(-jnp.abs(xdt_pre)))
    a = -jnp.exp(alog)
    ltri = (_iota((CHUNK, CHUNK), 0) >= _iota((CHUNK, CHUNK), 1)).astype(f32)
    cs = _dot32(ltri, dtv * a)
    csb = [_col_bcast(cs, h) for h in range(N_HEADS)]
    cs_x = _head_expand(csb)
    dt_x = _head_expand([_col_bcast(dtv, h) for h in range(N_HEADS)])
    cl_x = cs_x[CHUNK - 1:CHUNK, :]
    return dict(xdt_pre=xdt_pre, dtv=dtv, a=a, cs=cs, cs_t=cs.T, csb=csb, dt_x=dt_x, e_x=jnp.exp(cs_x),
                dec_x=jnp.exp(cl_x - cs_x), dk_x=jnp.exp(cl_x))


def _head_masks():
    lane = _iota((CHUNK, GROUP_W), 1)
    return [(lane >= r * HEAD_DIM) & (lane < (r + 1) * HEAD_DIM) for r in range(HEADS_PER_GROUP)]


def _stack_heads(a, masks):
    return jnp.concatenate([jnp.where(m, a, 0.0) for m in masks], axis=0).astype(bf16)


def _seg_sum(a, seg):
    hi = a.astype(jnp.bfloat16)
    lo = (a - hi.astype(f32)).astype(jnp.bfloat16)
    return (lax.dot_general(hi, seg, _NN, preferred_element_type=f32)
            + lax.dot_general(lo, seg, _NN, preferred_element_type=f32))


def _head_seg_matrix():
    return (_iota((D_INNER, LANES), 0) // HEAD_DIM == _iota((D_INNER, LANES), 1)).astype(jnp.bfloat16)


def _ssd_fwd(xbc, z, dtr, cw, cb, dtb, alog, dsk_x, gs, *, ride=None, name):
    t = xbc.shape[0]
    nc = t // CHUNK
    tiles = CHUNK // _TAIL

    def body(*refs):
        cur_ref, tail_ref, z_ref, dtr_ref, cw_ref, cb_ref, dtb_ref, alog_ref, dsk_ref, gs_ref = refs[:10]
        if ride is None:
            yb_ref, hp_ref, cv_ref, state_ref, xc_ref = refs[10:]
        else:
            ride_ref, yb_ref, hp_ref, cv_ref, got_ref, state_ref, xc_ref, send_sems, recv_sems = refs[10:]
        c = pl.program_id(0)
        if ride is not None:
            start, relay, finish = _gather_protocol(ride_ref, got_ref, send_sems, recv_sems)
            pl.when(c == 0)(start)
            pl.when(c == nc // 2)(relay)

        @pl.when(c == 0)
        def _():
            state_ref[...] = jnp.zeros_like(state_ref)

        _conv_silu(cur_ref, tail_ref, cw_ref, cb_ref, c > 0, xc_ref, cv_ref)
        sc = _ssd_chunk_scalars(dtr_ref[...], dtb_ref[...], alog_ref[...])
        tril = _iota((CHUNK, CHUNK), 0) >= _iota((CHUNK, CHUNK), 1)
        masks = _head_masks()
        hp_ref[0] = state_ref[...]
        for g in range(N_GROUPS):
            gsl = slice(g * GROUP_W, (g + 1) * GROUP_W)
            xs_g = xc_ref[:, gsl]
            bg = xc_ref[:, _B0 + g * D_STATE:_B0 + (g + 1) * D_STATE]
            cg = xc_ref[:, _C0 + g * D_STATE:_C0 + (g + 1) * D_STATE]
            xdt_g = xs_g * sc["dt_x"][:, gsl]
            cbm = _dot(cg, bg, _NT)
            mw = jnp.concatenate(
                [cbm * jnp.exp(jnp.where(tril, sc["csb"][h] - sc["cs_t"][h:h + 1, :], -1e30))
                 for h in range(g * HEADS_PER_GROUP, (g + 1) * HEADS_PER_GROUP)], axis=1)
            ht_g = state_ref[:, gsl]
            y_g = _dot(mw, _stack_heads(xdt_g, masks)) + sc["e_x"][:, gsl] * _dot(cg, ht_g) + dsk_ref[:, gsl] * xs_g
            state_ref[:, gsl] = ht_g * sc["dk_x"][:, gsl] + _dot(bg, xdt_g * sc["dec_x"][:, gsl], _TN)
            zg = z_ref[:, gsl]
            yg = y_g * zg * _sigmoid(zg)
            rs = lax.rsqrt(jnp.mean(yg * yg, axis=1, keepdims=True) + NORM_EPS)
            yb_ref[:, gsl] = (yg * rs * gs_ref[:, gsl]).astype(bf16)
        if ride is not None:
            pl.when(c == nc - 1)(finish)

    def chunk(w):
        return pl.BlockSpec((CHUNK, w), lambda c: (c, 0))

    def const(shape):
        return pl.BlockSpec(shape, lambda c: (0,) * len(shape))

    riding = ride is not None
    return pl.pallas_call(
        body, name=name, grid=(nc,),
        in_specs=[chunk(CONV_DIM), pl.BlockSpec((_TAIL, CONV_DIM), lambda c: (jnp.maximum(c * tiles - 1, 0), 0)),
                  chunk(D_INNER), chunk(LANES), const((CONV_W, CONV_DIM)), const((1, CONV_DIM)),
                  const((1, LANES)), const((1, LANES)), const((1, D_INNER)), const((1, D_INNER))] + [_ANY] * riding,
        out_specs=[chunk(D_INNER), pl.BlockSpec((1, D_STATE, D_INNER), lambda c: (c, 0, 0)), chunk(CONV_DIM)]
        + [_ANY] * riding,
        out_shape=[jax.ShapeDtypeStruct((t, D_INNER), bf16), jax.ShapeDtypeStruct((nc, D_STATE, D_INNER), f32),
                   jax.ShapeDtypeStruct((t, CONV_DIM), f32)]
        + ([jax.ShapeDtypeStruct((N_CHIPS,) + ride.shape, ride.dtype)] if riding else []),
        scratch_shapes=[pltpu.VMEM((D_STATE, D_INNER), f32), pltpu.VMEM((CHUNK, CONV_DIM), f32)]
        + (list(_GATHER_SCRATCH) if riding else []),
        compiler_params=_params(("arbitrary",)),
    )(xbc, xbc, z, dtr, cw, cb, dtb, alog, dsk_x, gs, *([ride] if riding else []))


def _ssd_bwd(xbc, cv, z, dtr, hprev, dyb, cw, dtb, alog, dsk_x, gs, seg, *, ride=None, name):
    t = xbc.shape[0]
    nc = t // CHUNK

    def body(*refs):
        (cur_ref, cv_ref, z_ref, dtr_ref, hp_ref, dyb_ref, cw_ref, dtb_ref, alog_ref, dsk_ref, gs_ref,
         seg_ref) = refs[:12]
        rest = refs[12:]
        if ride is not None:
            ride_ref, got_ref, send_sems, recv_sems = rest[0], rest[10], rest[-2], rest[-1]
            rest = rest[1:10] + rest[11:-2]
        (dz_ref, dxbc_ref, ddt_ref, dcw_ref, dcb_ref, ddtb_ref, dalog_ref, ddsk_ref, dgs_ref,
         dh_ref, dcnext_ref, xc_ref, dxc_ref, x13_ref, x2_ref, rows_ref) = rest
        i = pl.program_id(0)
        if ride is not None:
            start, finish = _scatter_protocol(ride_ref, got_ref, send_sems, recv_sems)
            pl.when(i == 0)(start)

        @pl.when(i == 0)
        def _():
            for ref in (dh_ref, dcnext_ref, dcw_ref, dcb_ref, ddtb_ref, dalog_ref, ddsk_ref, dgs_ref, rows_ref):
                ref[...] = jnp.zeros_like(ref)

        for j in range(CONV_DIM // _CONV_COLS):
            sl = slice(j * _CONV_COLS, (j + 1) * _CONV_COLS)
            cvv = cv_ref[:, sl]
            xc_ref[:, sl] = cvv * _sigmoid(cvv)
        sc = _ssd_chunk_scalars(dtr_ref[...], dtb_ref[...], alog_ref[...])
        tril = _iota((CHUNK, CHUNK), 0) >= _iota((CHUNK, CHUNK), 1)
        triu = _iota((CHUNK, CHUNK), 0) <= _iota((CHUNK, CHUNK), 1)
        masks = _head_masks()
        rowh = _iota((N_HEADS, CHUNK), 0)
        dcs_t = jnp.zeros((N_HEADS, CHUNK), f32)
        for g in range(N_GROUPS):
            gsl = slice(g * GROUP_W, (g + 1) * GROUP_W)
            xs_g = xc_ref[:, gsl]
            bg = xc_ref[:, _B0 + g * D_STATE:_B0 + (g + 1) * D_STATE]
            cg = xc_ref[:, _C0 + g * D_STATE:_C0 + (g + 1) * D_STATE]
            dt_g, e_g, dec_g, dk_g = sc["dt_x"][:, gsl], sc["e_x"][:, gsl], sc["dec_x"][:, gsl], sc["dk_x"][:, gsl]
            dsk_g = dsk_ref[:, gsl]
            xdt_g = xs_g * dt_g
            xdt_stack = _stack_heads(xdt_g, masks)
            cbm = _dot(cg, bg, _NT)
            cbt = _dot(bg, cg, _NT)
            heads = range(g * HEADS_PER_GROUP, (g + 1) * HEADS_PER_GROUP)
            lmats = [jnp.exp(jnp.where(tril, sc["csb"][h] - sc["cs_t"][h:h + 1, :], -1e30)) for h in heads]
            mw = jnp.concatenate([cbm * lm for lm in lmats], axis=1)
            mtw = jnp.concatenate(
                [cbt * jnp.exp(jnp.where(triu, sc["cs_t"][h:h + 1, :] - sc["csb"][h], -1e30)) for h in heads], axis=1)
            ht_g = hp_ref[0, :, gsl]
            dhn_g = dh_ref[:, gsl]
            yoff = e_g * _dot(cg, ht_g)
            y_g = _dot(mw, xdt_stack) + yoff + dsk_g * xs_g
            zg = z_ref[:, gsl]
            sz = _sigmoid(zg)
            silu = zg * sz
            yg = y_g * silu
            rs = lax.rsqrt(jnp.mean(yg * yg, axis=1, keepdims=True) + NORM_EPS)
            yn = yg * rs
            dyb = dyb_ref[:, gsl]
            dgs_ref[:, gsl] += jnp.sum(dyb * yn, axis=0, keepdims=True)
            dyn = dyb * gs_ref[:, gsl]
            dyg = rs * (dyn - yn * jnp.mean(dyn * yn, axis=1, keepdims=True))
            dy_g = dyg * silu
            dz_ref[:, gsl] = (dyg * y_g * (sz * (1.0 + zg * (1.0 - sz)))).astype(bf16)
            dy_stack = _stack_heads(dy_g, masks)
            dm_w = _dot(dy_g, xdt_stack, _NT)
            dmt_w = _dot(xdt_g, dy_stack, _NT)
            dxdt = _dot(mtw, dy_stack)
            dcb_acc = jnp.zeros((CHUNK, CHUNK), f32)
            for r, h in enumerate(heads):
                hs = slice(r * CHUNK, (r + 1) * CHUNK)
                dml = dm_w[:, hs] * lmats[r]
                dcb_acc = dcb_acc + dml
                col = jnp.sum(dml * cbm, axis=0, keepdims=True)
                row = jnp.sum(dmt_w[:, hs] * mtw[:, hs], axis=0, keepdims=True)
                dcs_t = dcs_t + jnp.where(rowh == h, row - col, 0.0)
            w = _dot(bg, dhn_g)
            dxdt = dxdt + dec_g * w
            decx3 = dec_g * (xdt_g * w)
            dg_g = e_g * dy_g
            d_c = _dot(dg_g, ht_g, _NT) + _dot(dcb_acc, bg)
            d_b = _dot(dcb_acc, cg, _TN) + _dot(xdt_g * dec_g, dhn_g, _NT)
            dh_ref[:, gsl] = dhn_g * dk_g + _dot(cg, dg_g, _TN)
            dxc_ref[:, gsl] = dsk_g * dy_g + dxdt * dt_g
            dxc_ref[:, _B0 + g * D_STATE:_B0 + (g + 1) * D_STATE] = d_b
            dxc_ref[:, _C0 + g * D_STATE:_C0 + (g + 1) * D_STATE] = d_c
            x13_ref[:, gsl] = dy_g * yoff - decx3
            x2_ref[:, gsl] = dxdt * xs_g
            rows_ref[0:1, gsl] = jnp.sum(dhn_g * ht_g, axis=0, keepdims=True)
            rows_ref[1:2, gsl] = jnp.sum(decx3, axis=0, keepdims=True)
            rows_ref[2:3, gsl] = jnp.sum(dy_g * xs_g, axis=0, keepdims=True)
        segm = seg_ref[...]
        r13 = _seg_sum(x13_ref[...], segm)
        r2 = _seg_sum(x2_ref[...], segm)
        small = _seg_sum(rows_ref[...], segm)
        lane = _iota((CHUNK, LANES), 1)
        rowi = _iota((CHUNK, LANES), 0)
        dcl_row = small[0:1, :] * jnp.exp(sc["cs"][CHUNK - 1:CHUNK, :]) + small[1:2, :]
        dcs = r13 + jnp.where(rowi == CHUNK - 1, dcl_row, 0.0)
        dcs_t_all = dcs.T + jnp.concatenate([dcs_t, jnp.zeros((LANES - N_HEADS, CHUNK), f32)], axis=0)
        dda = _dot32(dcs_t_all, tril.astype(f32)).T
        a = sc["a"]
        ddt_total = r2 + dda * a
        dalog_ref[...] += jnp.sum(dda * sc["dtv"], axis=0, keepdims=True) * a
        ddtr = jnp.where(lane < N_HEADS, ddt_total * _sigmoid(sc["xdt_pre"]), 0.0)
        ddtb_ref[...] += jnp.sum(ddtr, axis=0, keepdims=True)
        ddt_ref[...] = ddtr.astype(bf16)
        ddsk_ref[...] += small[2:3, :]
        row8 = _iota((_TAIL, _CONV_COLS), 0)
        for j in range(CONV_DIM // _CONV_COLS):
            sl = slice(j * _CONV_COLS, (j + 1) * _CONV_COLS)
            cvv = cv_ref[:, sl]
            sg = _sigmoid(cvv)
            dconv = dxc_ref[:, sl] * (sg * (1.0 + cvv * (1.0 - sg)))
            nxt = dcnext_ref[:, sl]
            cur = cur_ref[:, sl]
            dxin = dconv * cw_ref[CONV_W - 1:CONV_W, sl]
            dcw_ref[CONV_W - 1:CONV_W, sl] += jnp.sum(dconv * cur, axis=0, keepdims=True)
            for s in range(1, CONV_W):
                rolled = pltpu.roll(dconv, CHUNK - s, 0)
                bot = jnp.where(row8 < _TAIL - s, rolled[CHUNK - _TAIL:], pltpu.roll(nxt, _TAIL - s, 0))
                up = jnp.concatenate([rolled[:CHUNK - _TAIL], bot], axis=0)
                dxin = dxin + up * cw_ref[CONV_W - 1 - s:CONV_W - s, sl]
                dcw_ref[CONV_W - 1 - s:CONV_W - s, sl] += jnp.sum(up * cur, axis=0, keepdims=True)
            dcb_ref[:, sl] += jnp.sum(dconv, axis=0, keepdims=True)
            dxbc_ref[:, sl] = dxin.astype(bf16)
            dcnext_ref[:, sl] = dconv[:_TAIL]
        if ride is not None:
            pl.when(i == nc - 1)(finish)

    def chunk(w):
        return pl.BlockSpec((CHUNK, w), lambda i: (nc - 1 - i, 0))

    def const(shape):
        return pl.BlockSpec(shape, lambda i: (0,) * len(shape))

    riding = ride is not None
    return pl.pallas_call(
        body, name=name, grid=(nc,),
        in_specs=[chunk(CONV_DIM), chunk(CONV_DIM),
                  chunk(D_INNER), chunk(LANES), pl.BlockSpec((1, D_STATE, D_INNER), lambda i: (nc - 1 - i, 0, 0)),
                  chunk(D_INNER), const((CONV_W, CONV_DIM)),
                  const((1, LANES)), const((1, LANES)), const((1, D_INNER)), const((1, D_INNER)),
                  const((D_INNER, LANES))] + [_ANY] * riding,
        out_specs=[chunk(D_INNER), chunk(CONV_DIM), chunk(LANES), const((CONV_W, CONV_DIM)), const((1, CONV_DIM)),
                   const((1, LANES)), const((1, LANES)), const((1, LANES)), const((1, D_INNER))] + [_ANY] * riding,
        out_shape=[jax.ShapeDtypeStruct((t, D_INNER), bf16), jax.ShapeDtypeStruct((t, CONV_DIM), bf16),
                   jax.ShapeDtypeStruct((t, LANES), bf16), jax.ShapeDtypeStruct((CONV_W, CONV_DIM), f32),
                   jax.ShapeDtypeStruct((1, CONV_DIM), f32), jax.ShapeDtypeStruct((1, LANES), f32),
                   jax.ShapeDtypeStruct((1, LANES), f32), jax.ShapeDtypeStruct((1, LANES), f32),
                   jax.ShapeDtypeStruct((1, D_INNER), f32)]
        + ([jax.ShapeDtypeStruct((N_CHIPS - 1,) + ride.shape[1:], ride.dtype)] if riding else []),
        scratch_shapes=[pltpu.VMEM((D_STATE, D_INNER), f32), pltpu.VMEM((_TAIL, CONV_DIM), f32),
                        pltpu.VMEM((CHUNK, CONV_DIM), f32), pltpu.VMEM((CHUNK, CONV_DIM), f32),
                        pltpu.VMEM((CHUNK, D_INNER), f32), pltpu.VMEM((CHUNK, D_INNER), f32),
                        pltpu.VMEM((_TAIL, D_INNER), f32)]
        + (list(_SCATTER_SCRATCH) if riding else []),
        compiler_params=_params(("arbitrary",)),
    )(xbc, cv, z, dtr, hprev, dyb, cw, dtb, alog, dsk_x, gs, seg, *([ride] if riding else []))


def _adamw(w, g, m, v, *, name):
    r, c = w.shape
    tr = r
    while tr * c * 4 > 2 * _MB and tr % 16 == 0:
        tr //= 2

    def body(w_ref, g_ref, m_ref, v_ref, d_ref, m2_ref, v2_ref):
        gv = g_ref[...]
        m2 = ADAM_B1 * m_ref[...] + (1.0 - ADAM_B1) * gv
        v2 = ADAM_B2 * v_ref[...] + (1.0 - ADAM_B2) * (gv * gv)
        m_hat = m2 / (1.0 - ADAM_B1 ** ADAM_STEP)
        v_hat = v2 / (1.0 - ADAM_B2 ** ADAM_STEP)
        d_ref[...] = -ADAM_LR * (m_hat / (jnp.sqrt(v_hat) + ADAM_EPS) + ADAM_WD * w_ref[...])
        m2_ref[...] = m2
        v2_ref[...] = v2

    blk = pl.BlockSpec((tr, c), lambda i: (i, 0))
    return pl.pallas_call(
        body, name=name, grid=(r // tr,),
        in_specs=[blk] * 4, out_specs=[blk] * 3,
        out_shape=[jax.ShapeDtypeStruct((r, c), f32)] * 3,
        compiler_params=_params(("parallel",)),
    )(w, g, m, v)


def _row_block(rows, cols):
    cap = max(16, 2 * _MB // (4 * cols))
    return max(tr for tr in range(16, min(cap, rows) + 1, 16) if rows % tr == 0)


def _cast_bf16(a, *, name):
    r, c = a.shape
    tr = _row_block(r, c)

    def body(a_ref, o_ref):
        o_ref[...] = a_ref[...].astype(bf16)

    blk = pl.BlockSpec((tr, c), lambda i: (i, 0))
    return pl.pallas_call(
        body, name=name, grid=(r // tr,), in_specs=[blk], out_specs=blk,
        out_shape=jax.ShapeDtypeStruct((r, c), bf16), compiler_params=_params(("parallel",)),
    )(a)


_ANY = pl.BlockSpec(memory_space=pl.ANY)


def _place():
    x, y, c = lax.axis_index("x"), lax.axis_index("y"), lax.axis_index("c")
    other_chips = [(1 - x, y), (x, 1 - y), (1 - x, 1 - y)]
    return x, y, c, other_chips


def _gather_protocol(in_ref, out_ref, send_sems, recv_sems):
    x, y, c, chips = _place()
    me = 2 * x + y
    sibling = (x, y, 1 - c)
    where = [2 * cx + cy for cx, cy in chips]

    def cp(k, chip, half, to, src=None):
        dst = out_ref.at[chip, half]
        return pltpu.make_async_remote_copy(
            src_ref=dst if src is None else src, dst_ref=dst, send_sem=send_sems.at[k], recv_sem=recv_sems.at[k],
            device_id=to, device_id_type=MESH)

    def sends():
        return [cp(j, me, c, (*chips[j], c), src=in_ref.at[c]) for j in range(2)]

    def relays():
        return [cp(3 + j, where[j], c, sibling) for j in range(3)]

    def landed(j):
        return cp(j, where[j], c, sibling)

    def start():
        for f in sends():
            f.start()

    def relay():
        onward = relays()
        for first in range(2):
            @pl.when(c == first)
            def _(first=first):
                landed(first).wait_recv()
                cp(2, where[first], c, (*chips[1 - first], c)).start()
                onward[first].start()
                landed(1 - first).wait_recv()
                onward[1 - first].start()

    def finish():
        landed(2).wait_recv()
        relays()[2].start()
        for j in range(3):
            cp(3 + j, where[j], 1 - c, sibling).wait_recv()
        for f in sends() + [landed(2)] + relays():
            f.wait_send()

    return start, relay, finish


_GATHER_SCRATCH = [pltpu.SemaphoreType.DMA((6,)), pltpu.SemaphoreType.DMA((6,))]


def _gather_shards(shard, *, name):
    _, rh, lanes = shard.shape

    def body(in_ref, out_ref, send_sems, recv_sems):
        start, relay, finish = _gather_protocol(in_ref, out_ref, send_sems, recv_sems)
        start()
        relay()
        finish()

    return pl.pallas_call(
        body, name=name, in_specs=[_ANY], out_specs=_ANY,
        out_shape=jax.ShapeDtypeStruct((N_CHIPS, 2, rh, lanes), shard.dtype),
        scratch_shapes=list(_GATHER_SCRATCH),
    )(shard)


def _scatter_protocol(p_ref, out_ref, send_sems, recv_sems):
    x, y, c, chips = _place()

    def copies():
        return [pltpu.make_async_remote_copy(
            src_ref=p_ref.at[2 * cx + cy], dst_ref=out_ref.at[j], send_sem=send_sems.at[j], recv_sem=recv_sems.at[j],
            device_id=(cx, cy, c), device_id_type=MESH) for j, (cx, cy) in enumerate(chips)]

    def start():
        for cpy in copies():
            cpy.start()

    def finish():
        for cpy in copies():
            cpy.wait()

    return start, finish


_SCATTER_SCRATCH = [pltpu.SemaphoreType.DMA((3,)), pltpu.SemaphoreType.DMA((3,))]


def _swap_protocol(g_ref, out_ref, send_sems, recv_sems):
    x, y, c, _ = _place()

    def copies():
        return [pltpu.make_async_remote_copy(
            src_ref=g_ref.at[k, 1 - c], dst_ref=out_ref.at[k], send_sem=send_sems.at[k], recv_sem=recv_sems.at[k],
            device_id=(x, y, 1 - c), device_id_type=MESH) for k in range(N_CHIPS)]

    def start():
        for cpy in copies():
            cpy.start()

    def finish():
        for cpy in copies():
            cpy.wait()

    return start, finish


_SWAP_SCRATCH = [pltpu.SemaphoreType.DMA((N_CHIPS,)), pltpu.SemaphoreType.DMA((N_CHIPS,))]


def _rs_swap_halves(g, *, name):
    nch, _, rh, lanes = g.shape

    def body(g_ref, out_ref, send_sems, recv_sems):
        start, finish = _swap_protocol(g_ref, out_ref, send_sems, recv_sems)
        start()
        finish()

    return pl.pallas_call(
        body, name=name, in_specs=[_ANY], out_specs=_ANY,
        out_shape=jax.ShapeDtypeStruct((nch, rh, lanes), g.dtype),
        scratch_shapes=list(_SWAP_SCRATCH),
    )(g)


def _rs_add_pair(g, got, c_idx, *, name):
    nch, _, rh, lanes = g.shape
    tr = _row_block(rh, lanes)

    def body(c_ref, g_ref, got_ref, p16_ref):
        p16_ref[...] = (g_ref[...] + got_ref[...]).astype(bf16)

    blk = pl.BlockSpec((None, tr, lanes), lambda k, i, c_ref: (k, i, 0))
    return pl.pallas_call(
        body, name=name,
        grid_spec=pltpu.PrefetchScalarGridSpec(
            num_scalar_prefetch=1, grid=(nch, rh // tr),
            in_specs=[pl.BlockSpec((None, None, tr, lanes), lambda k, i, c_ref: (k, c_ref[0], i, 0)), blk],
            out_specs=blk),
        out_shape=jax.ShapeDtypeStruct((nch, rh, lanes), bf16),
        compiler_params=_params(("parallel", "parallel")),
    )(c_idx, g, got)


def _rs_add_chips(g, got_pair, got, place, *, name):
    _, _, rh, lanes = g.shape
    tr = _row_block(rh, lanes)

    def body(place_ref, g_ref, pair_ref, got_ref, o_ref):
        own = g_ref[...] + pair_ref[...]
        o_ref[...] = ((own + got_ref[0].astype(f32)) + got_ref[1].astype(f32)) + got_ref[2].astype(f32)

    return pl.pallas_call(
        body, name=name,
        grid_spec=pltpu.PrefetchScalarGridSpec(
            num_scalar_prefetch=1, grid=(rh // tr,),
            in_specs=[pl.BlockSpec((None, None, tr, lanes), lambda i, place_ref: (place_ref[0], place_ref[1], i, 0)),
                      pl.BlockSpec((None, tr, lanes), lambda i, place_ref: (place_ref[0], i, 0)),
                      pl.BlockSpec((3, tr, lanes), lambda i, place_ref: (0, i, 0))],
            out_specs=pl.BlockSpec((None, tr, lanes), lambda i, place_ref: (place_ref[1], i, 0))),
        out_shape=jax.ShapeDtypeStruct((2, rh, lanes), f32),
        compiler_params=_params(("parallel",)),
    )(place, g, got_pair, got)


def _rs_join_halves(halves, *, name):
    def body(h_ref, out_ref, send_sem, recv_sem):
        x, y, c, _ = _place()
        cpy = pltpu.make_async_remote_copy(
            src_ref=h_ref.at[c], dst_ref=out_ref.at[c], send_sem=send_sem, recv_sem=recv_sem,
            device_id=(x, y, 1 - c), device_id_type=MESH)
        cpy.start()
        cpy.wait()

    return pl.pallas_call(
        body, name=name, in_specs=[_ANY], out_specs=_ANY,
        out_shape=jax.ShapeDtypeStruct(halves.shape, halves.dtype), input_output_aliases={0: 0},
        scratch_shapes=[pltpu.SemaphoreType.DMA, pltpu.SemaphoreType.DMA],
    )(halves)


def _all_reduce_small(s, *, name):
    rs, lanes = s.shape
    rh = rs // 2

    def body(s_ref, o_ref, sib_ref, mine_ref, chips_ref, send_sems, recv_sems):
        x, y, c, chips = _place()
        me = 2 * x + y
        sibling = (x, y, 1 - c)
        rows = pl.ds(pl.multiple_of(c * rh, 8), rh)

        def cp(k, src, dst, to):
            return pltpu.make_async_remote_copy(src_ref=src, dst_ref=dst, send_sem=send_sems.at[k],
                                                recv_sem=recv_sems.at[k], device_id=to, device_id_type=MESH)

        swap = cp(0, s_ref, sib_ref, sibling)
        swap.start()
        swap.wait()
        mine_ref[...] = s_ref[rows, :] + sib_ref[rows, :]
        sends = [cp(1 + j, mine_ref, chips_ref.at[j], (cx, cy, c)) for j, (cx, cy) in enumerate(chips)]
        for cpy in sends:
            cpy.start()
        for cpy in sends:
            cpy.wait()
        where = [2 * cx + cy for cx, cy in chips]
        total = None
        for q in range(N_CHIPS):
            term = jnp.where(q == me, mine_ref[...], jnp.where(
                q == where[0], chips_ref[0], jnp.where(q == where[1], chips_ref[1], chips_ref[2])))
            total = term if total is None else total + term
        o_ref[rows, :] = total
        push = cp(4, o_ref.at[rows, :], o_ref.at[rows, :], sibling)
        push.start()
        push.wait()

    vm = pl.BlockSpec(memory_space=pltpu.VMEM)
    return pl.pallas_call(
        body, name=name, in_specs=[vm], out_specs=vm,
        out_shape=jax.ShapeDtypeStruct((rs, lanes), f32),
        scratch_shapes=[pltpu.VMEM((rs, lanes), f32), pltpu.VMEM((rh, lanes), f32),
                        pltpu.VMEM((N_CHIPS - 1, rh, lanes), f32), pltpu.SemaphoreType.DMA((5,)),
                        pltpu.SemaphoreType.DMA((5,))],
        compiler_params=pltpu.CompilerParams(vmem_limit_bytes=32 * _MB),
    )(s)


def _pad_lanes(a, width=LANES):
    return jnp.pad(a, ((0, 0), (0, width - a.shape[1])))


def _local_grads(x, tgt, wts, small, *, fwd_ride=None, late_weights=None, swap_ride=None, bwd_ride=None,
                 last_ride=None):
    t = x.shape[0]
    tm = min(t, 1024)
    d = D_MODEL
    mm = functools.partial(_matmul, tm=tm)

    dtb = _pad_lanes(small["dt_bias"])
    alog = _pad_lanes(small["a_log"])
    dsk = jnp.repeat(small["d_skip"], HEAD_DIM, axis=1)
    bsp_t = _pad_lanes(small["b_spatial"].T)
    wsp = small["w_spatial"]

    h = _rms_fwd(x, small["norm_mix_g"], name="rms_mix")
    uv = mm(h, wts["uv"], tn=2048, tk=d, out_dtypes=[f32], name="proj_uv")
    z = mm(h, wts["z"], tn=2048, tk=d, out_dtypes=[f32], name="proj_z")
    xbc = mm(h, wts["xbc"], tn=2048, tk=d, out_dtypes=[f32], name="proj_xbc")
    dtr = mm(h, wts["dt"], tn=LANES, tk=d, out_dtypes=[f32], name="proj_dt")
    gl = mm(h, wts["gate"], tn=2048, tk=d, out_dtypes=[f32], name="proj_gate")
    ya = _gmlp_fwd(uv, small["v_norm_g"], small["v_norm_b"], wsp, bsp_t, name="gmlp_fwd")
    yb, hprev, cv, *gathered = _ssd_fwd(xbc, z, dtr, small["conv_w"], small["conv_b"], dtb, alog, dsk,
                                        small["ssm_norm_g"], ride=fwd_ride, name="ssd_fwd")
    if fwd_ride is not None:
        wts = {**wts, **late_weights(gathered[0])}
    tall = functools.partial(_matmul, tm=min(t, 2048))
    pa = tall(ya, wts["pa"], tn=1024, tk=1024, out_dtypes=[f32], name="proj_a")
    tm_gate = min(t, 512)
    row_vec = [pl.BlockSpec((1, d), lambda i, j, k, half=half: (0, half)) for half in range(2)]
    gate_tiles = [pl.BlockSpec((tm_gate, d), lambda i, j, k, half=half: (i, half)) for half in range(2)]

    def merge(pb_acc, pa_t, gla, glb, bga, bgb):
        return pb_acc, _sigmoid(gla + bga) * pa_t + _sigmoid(glb + bgb) * pb_acc

    pb, merged = _matmul(yb, wts["pb"], tm=tm_gate, tn=d, tk=1024, out_dtypes=[f32, bf16], epilogue=merge,
                         extras=[pa, gl, gl, small["b_gates"], small["b_gates"]],
                         extra_specs=[None] + gate_tiles + row_vec, name="proj_b")

    def residual_norm(acc, res, g):
        x_new = res + acc
        r = lax.rsqrt(jnp.mean(x_new * x_new, axis=1, keepdims=True) + NORM_EPS)
        return x_new, x_new * r * g

    x1, h2 = mm(merged, wts["out"], tn=d, tk=1024, out_dtypes=[f32, bf16], epilogue=residual_norm,
                extras=[x, small["norm_mlp_g"]], extra_specs=[None, row_vec[0]], name="out_proj")
    act = mm(h2, wts["up"], tn=2048, tk=d, out_dtypes=[bf16],
             epilogue=lambda acc: (jnp.square(jnp.maximum(acc, 0.0)),), name="mlp_up")
    x2 = mm(act, wts["down"], tn=1024, tk=2048, out_dtypes=[f32], extras=[x1],
            epilogue=lambda acc, res: (res + acc,), name="mlp_down")

    dx2, dx2b, dgf, loss = _loss_head(x2, tgt, small["norm_final_g"], name="loss_head")
    tt = min(t, 2048)
    tn_mm = functools.partial(_matmul_tn, tt=tt)
    dw = {}
    def slab(key, chip_of):
        rows = _LATE_ROWS[key]
        if N_CHIPS * rows == 1024:
            return dict(tka=1024, tn=1024, place=lambda i, j: (ALL_CHIPS, _LATE_OFF[key] // rows))
        return dict(tka=min(rows, 1024), tn=1024, place=lambda i, j: (chip_of(i, j), _LATE_OFF[key] // min(rows, 1024)))

    dw["late"] = tn_mm(act, dx2b, name="dw_down", **slab("w_mlp_down", lambda i, j: i))
    dup = mm(dx2b, wts["down"], nt=True, tn=2048, tk=1024, out_dtypes=[bf16], extras=[act],
             epilogue=lambda acc, a2: (acc * (2.0 * jnp.sqrt(a2).astype(f32)),), name="d_act")
    dw["late"] = tn_mm(h2, dup, name="dw_up", packed=dw["late"], **slab("w_mlp_up", lambda i, j: j))
    dh2 = mm(dup, wts["up"], nt=True, tn=1024, tk=2048, out_dtypes=[f32], name="d_h2")
    dx1, dx1b, dg_mlp = _rms_bwd(x1, small["norm_mlp_g"], dh2, dx2, want_bf16=True, name="rms_mlp_bwd")
    dw["late"] = tn_mm(merged, dx1b, name="dw_out", packed=dw["late"], **slab("w_out", lambda i, j: i))
    dmerged = tall(dx1b, wts["out"], nt=True, tn=1024, tk=1024, out_dtypes=[f32], name="d_merged")
    dpa, dpb, dgl, dbg = _merge_bwd(dmerged, pa, pb, gl, small["b_gates"], name="merge_bwd")
    dw["late"] = tn_mm(ya, dpa, name="dw_pa", packed=dw["late"], **slab("w_proj_a", lambda i, j: i))
    dw["late"] = tn_mm(yb, dpb, name="dw_pb", packed=dw["late"], **slab("w_proj_b", lambda i, j: i))
    dya = tall(dpa, wts["pa"], nt=True, tn=1024, tk=1024, out_dtypes=[f32], name="d_ya")
    dyb = mm(dpb, wts["pb"], nt=True, tn=2048, tk=1024, out_dtypes=[f32], name="d_yb")
    swapped = swap_ride(dw) if swap_ride is not None else None
    duv, dwsp, dbsp_t, dvg, dvb, *got_pair = _gmlp_bwd(uv, dya, small["v_norm_g"], small["v_norm_b"], wsp, bsp_t,
                                                       ride=swapped, name="gmlp_bwd")
    ride = bwd_ride(swapped, got_pair[0]) if bwd_ride is not None else None
    dz, dxbc, ddt, dcw, dcb, ddtb, dalog, ddsk, dgs, *got = _ssd_bwd(
        xbc, cv, z, dtr, hprev, dyb, small["conv_w"], dtb, alog, dsk, small["ssm_norm_g"],
        _head_seg_matrix(), ride=ride, name="ssd_bwd")
    dw["uv"] = tn_mm(h, duv, tka=1024, tn=1024, name="dw_uv")
    dw["z"] = tn_mm(h, dz, tka=1024, tn=1024, name="dw_z")
    dw["xbc"] = tn_mm(h, dxbc, tka=1024, tn=1024, name="dw_xbc")
    dw["dt"] = tn_mm(h, ddt, tka=1024, tn=LANES, name="dw_dt")
    dw["gate"] = tn_mm(h, dgl, tka=1024, tn=1024, name="dw_gate")
    last = last_ride(dw) if last_ride is not None else None
    res = _matmul_nt_sum(
        [(duv, wts["uv"].T), (dz, wts["z"].T), (dxbc, wts["xbc"].T), (dgl, wts["gate"].T), (ddt, wts["dt"].T)],
        tm=tm, tks=[1024] * 4 + [LANES], ride=last, name="d_h")
    dh, got_last = (res[0], res[1]) if last is not None else (res, None)
    dx, dg_mix = _rms_bwd(x, small["norm_mix_g"], dh, dx1, want_bf16=False, name="rms_mix_bwd")

    dsmall = {
        "norm_mix_g": dg_mix, "conv_w": dcw, "conv_b": dcb, "dt_bias": ddtb[:, :N_HEADS], "a_log": dalog[:, :N_HEADS],
        "d_skip": ddsk[:, :N_HEADS], "ssm_norm_g": dgs, "v_norm_g": dvg, "v_norm_b": dvb, "w_spatial": dwsp,
        "b_spatial": dbsp_t[:, :GMLP_GROUPS].T, "b_gates": dbg, "norm_mlp_g": dg_mlp, "norm_final_g": dgf,
    }
    return loss, dx, dw, dsmall, (got[0] if got else None), got_last


_IN_SHARD = IN_PROJ // N_CHIPS
_LATE = ("w_mlp_down", "w_mlp_up", "w_proj_b", "w_proj_a", "w_out")
_LATE_ROWS = {"w_proj_a": GMLP_WIDTH // N_CHIPS, "w_proj_b": D_INNER // N_CHIPS, "w_out": D_MODEL // N_CHIPS,
              "w_mlp_up": D_MODEL, "w_mlp_down": D_FF // N_CHIPS}
_LATE_TOTAL = sum(_LATE_ROWS.values())


def _late_offsets():
    off, out = 0, {}
    for k in _LATE:
        out[k] = off
        off += _LATE_ROWS[k]
    return out


_LATE_OFF = _late_offsets()

_SMALL = ("norm_mix_g", "conv_w", "conv_b", "dt_bias", "a_log", "d_skip", "ssm_norm_g", "v_norm_g", "v_norm_b",
          "w_spatial", "b_spatial", "b_gates", "norm_mlp_g", "norm_final_g")


def _pack_small(parts):
    flat = jnp.concatenate([parts[k].reshape(-1) for k in _SMALL])
    rows = -(-flat.shape[0] // (16 * LANES)) * 16
    return jnp.pad(flat, (0, rows * LANES - flat.shape[0])).reshape(rows, LANES)


def _unpack_small(packed, shapes):
    flat = packed.reshape(-1)
    out, off = {}, 0
    for k in _SMALL:
        n = math.prod(shapes[k])
        out[k] = flat[off:off + n].reshape(shapes[k])
        off += n
    return out


def _from_chip_columns(stacked):
    _, rows, cols = stacked.shape
    return stacked.transpose(1, 0, 2).reshape(rows, N_CHIPS * cols)


def _w_in_grad_by_chip(dw):
    pieces = [dw["uv"], dw["z"], dw["xbc"], dw["dt"][:, :N_HEADS], dw["gate"]]
    bounds = [0]
    for p in pieces:
        bounds.append(bounds[-1] + p.shape[1])
    chips = []
    for k in range(N_CHIPS):
        lo, hi = k * _IN_SHARD, (k + 1) * _IN_SHARD
        parts = [p[:, max(lo, b0) - b0:min(hi, b1) - b0]
                 for p, b0, b1 in zip(pieces, bounds[:-1], bounds[1:]) if min(hi, b1) > max(lo, b0)]
        chips.append(jnp.concatenate(parts, axis=1))
    return jnp.stack(chips)


def kernel(x, norm_mix_g, w_in, conv_w, conv_b, dt_bias, a_log, d_skip, ssm_norm_g, v_norm_g, v_norm_b, w_spatial, b_spatial, b_gates, w_proj_a, w_proj_b, w_out, norm_mlp_g, w_mlp_up, w_mlp_down, norm_final_g, loss_target, m_norm_mix_g, m_w_in, m_conv_w, m_conv_b, m_dt_bias, m_a_log, m_d_skip, m_ssm_norm_g, m_v_norm_g, m_v_norm_b, m_w_spatial, m_b_spatial, m_b_gates, m_w_proj_a, m_w_proj_b, m_w_out, m_norm_mlp_g, m_w_mlp_up, m_w_mlp_down, m_norm_final_g, v_norm_mix_g, v_w_in, v_conv_w, v_conv_b, v_dt_bias, v_a_log, v_d_skip, v_ssm_norm_g, v_v_norm_g, v_v_norm_b, v_w_spatial, v_b_spatial, v_b_gates, v_w_proj_a, v_w_proj_b, v_w_out, v_norm_mlp_g, v_w_mlp_up, v_w_mlp_down, v_norm_final_g):
    given = dict(locals())
    names = ("norm_mix_g", "w_in", "conv_w", "conv_b", "dt_bias", "a_log", "d_skip", "ssm_norm_g", "v_norm_g",
             "v_norm_b", "w_spatial", "b_spatial", "b_gates", "w_proj_a", "w_proj_b", "w_out", "norm_mlp_g",
             "w_mlp_up", "w_mlp_down", "norm_final_g")
    xi, yi, ci = lax.axis_index("x"), lax.axis_index("y"), lax.axis_index("c")
    me_chip = (2 * xi + yi).astype(jnp.int32)

    def halves(a):
        return a.reshape(2, a.shape[0] // 2, a.shape[1])

    def with_own(got, shard):
        whole = lax.dynamic_update_slice(got, shard[None], (me_chip, 0, 0, 0))
        return whole.reshape(N_CHIPS, 2 * shard.shape[1], shard.shape[2])

    shard_in = halves(_cast_bf16(w_in[0], name="cast_w_in"))
    shard_late = halves(_cast_bf16(jnp.concatenate([given[k][0] for k in _LATE]), name="cast_w_late"))
    shard_conv = halves(conv_w.reshape(2 * _TAIL, -1))
    w_in_full = _from_chip_columns(with_own(_gather_shards(shard_in, name="gather_w_in"), shard_in))
    o_dt, o_gate = 2 * GMLP_WIDTH + D_INNER + CONV_DIM, 2 * GMLP_WIDTH + D_INNER + CONV_DIM + N_HEADS
    wts = {
        "uv": w_in_full[:, :2 * GMLP_WIDTH], "z": w_in_full[:, 2 * GMLP_WIDTH:2 * GMLP_WIDTH + D_INNER],
        "xbc": w_in_full[:, 2 * GMLP_WIDTH + D_INNER:o_dt], "dt": _pad_lanes(w_in_full[:, o_dt:o_gate]),
        "gate": w_in_full[:, o_gate:],
    }
    conv_all = with_own(_gather_shards(shard_conv, name="gather_conv_w"), shard_conv)
    conv_full = _from_chip_columns(conv_all.reshape(N_CHIPS, CONV_W, CONV_DIM // N_CHIPS))

    def late_weights(got):
        g_late = with_own(got, shard_late)

        def rows_of(k):
            return g_late[:, _LATE_OFF[k]:_LATE_OFF[k] + _LATE_ROWS[k]]

        return {
            "pa": rows_of("w_proj_a").reshape(GMLP_WIDTH, D_MODEL),
            "pb": rows_of("w_proj_b").reshape(D_INNER, D_MODEL), "out": rows_of("w_out").reshape(D_MODEL, D_MODEL),
            "up": _from_chip_columns(rows_of("w_mlp_up")), "down": rows_of("w_mlp_down").reshape(D_FF, D_MODEL),
        }

    small = {
        "norm_mix_g": norm_mix_g, "conv_w": conv_full, "conv_b": conv_b, "dt_bias": dt_bias, "a_log": a_log,
        "d_skip": d_skip, "ssm_norm_g": ssm_norm_g, "v_norm_g": v_norm_g, "v_norm_b": v_norm_b,
        "w_spatial": w_spatial[0], "b_spatial": b_spatial[0], "b_gates": b_gates, "norm_mlp_g": norm_mlp_g,
        "norm_final_g": norm_final_g.reshape(1, D_MODEL),
    }

    c_idx = ci.astype(jnp.int32).reshape(1)
    place = jnp.stack([me_chip, ci.astype(jnp.int32)])
    partials = {}

    def reduced_shard(tag, got_chips):
        own = _rs_add_chips(*partials[tag], got_chips, place, name="rs_add_chips_" + tag)
        both = _rs_join_halves(own, name="rs_join_" + tag)
        return both.reshape(2 * both.shape[1], both.shape[2])

    def late_grads(dw):
        return dw["late"].reshape(N_CHIPS, 2, _LATE_TOTAL // 2, D_MODEL)

    def late_partials(g, got_pair):
        partials["late"] = (g, got_pair)
        return _rs_add_pair(g, got_pair, c_idx, name="rs_add_pair_late")

    def in_partials(dw):
        g = _w_in_grad_by_chip(dw).reshape(N_CHIPS, 2, D_MODEL // 2, _IN_SHARD)
        got_pair = _rs_swap_halves(g, name="rs_swap_in")
        partials["in"] = (g, got_pair)
        return _rs_add_pair(g, got_pair, c_idx, name="rs_add_pair_in")

    loss_part, grad_x, dw, dsmall, got_late, got_in = _local_grads(
        x[0], loss_target[0], wts, small, fwd_ride=shard_late, late_weights=late_weights, swap_ride=late_grads,
        bwd_ride=late_partials, last_ride=in_partials)
    loss = lax.psum(loss_part[0, 0], ("x", "y", "c"))
    g_late = reduced_shard("late", got_late)
    g_in_shard = reduced_shard("in", got_in)

    small_shapes = {k: dsmall[k].shape for k in _SMALL}
    red = _unpack_small(_all_reduce_small(_pack_small(dsmall), name="all_reduce_small"), small_shapes)
    conv_cols = CONV_DIM // N_CHIPS
    red["conv_w"] = lax.dynamic_slice_in_dim(red["conv_w"], me_chip * conv_cols, conv_cols, axis=1)

    grads, deltas, new_m, new_v = {}, {}, {}, {}
    for k in ("w_in",) + _LATE:
        g2 = g_in_shard if k == "w_in" else g_late[_LATE_OFF[k]:_LATE_OFF[k] + _LATE_ROWS[k]]
        dlt, m2, v2 = _adamw(given[k][0], g2, given["m_" + k][0], given["v_" + k][0], name="adamw_" + k)
        grads[k], deltas[k], new_m[k], new_v[k] = g2, dlt, m2, v2
    adam_shapes = dict(small_shapes)
    adam_shapes["conv_w"] = (CONV_W, conv_cols)

    def small_pack_of(prefix):
        return _pack_small({k: given[prefix + k].reshape(adam_shapes[k]) for k in _SMALL})

    dlt_s, m_s, v_s = _adamw(small_pack_of(""), _pack_small(red), small_pack_of("m_"), small_pack_of("v_"),
                             name="adamw_small")
    for dst, packed in ((deltas, dlt_s), (new_m, m_s), (new_v, v_s)):
        dst.update(_unpack_small(packed, adam_shapes))
    grads.update(red)

    def shaped(dct):
        return [dct[k].reshape(given[k].shape) for k in names]

    return (loss, grad_x[None], *shaped(grads), *shaped(deltas), *shaped(new_m), *shaped(new_v))
```
